```python
import jax, jax.numpy as jnp
from jax import lax
import numpy as np

D_MODEL = 1024
BATCH = 8
SEQ = 8192
DEPTH = 1

CONV_WIDTH = 512
CONV_K = 3
ATTN_GROUPS = ((128, 1), (512, 4), (2048, 16))
N_GROUPS = 3
HEADS_PER_GROUP = 4
N_HEADS = N_GROUPS * HEADS_PER_GROUP
HEAD_DIM = 64
ATTN_WIDTH = N_HEADS * HEAD_DIM
ATTN_BLOCK = 128
D_FF = 2816
FFN_K = 3
EPS = 1e-6
NEG_INF = -1e30
IN_SIZES = (CONV_WIDTH, CONV_WIDTH, CONV_WIDTH, ATTN_WIDTH, ATTN_WIDTH, ATTN_WIDTH, D_MODEL, D_MODEL)
D_IN = sum(IN_SIZES)

kernel_name = "hybrid_shortconv_dilated_swa_convffn"


def rms_norm(x, g):
    xf = x.astype(jnp.float32)
    y = xf * lax.rsqrt(jnp.mean(xf * xf, axis=-1, keepdims=True) + EPS)
    return (y * g.astype(jnp.float32)).astype(x.dtype)


def causal_dwconv(x, w, b):
    K, C = w.shape
    y = lax.conv_general_dilated(
        x, w[:, None, :], window_strides=(1,), padding=[(K - 1, 0)],
        dimension_numbers=("NWC", "WIO", "NWC"), feature_group_count=C)
    return y + b


def dilated_window_attention(q, k, v, window, dilation):
    B, S, H, dh = q.shape
    Q = ATTN_BLOCK
    n_back = window // dilation
    L = S // dilation
    Lp = L + (-L) % Q
    nb = Lp // Q

    def to_streams(t):
        t = t.reshape(B, L, dilation, H, dh).transpose(0, 2, 1, 3, 4)
        return jnp.pad(t, ((0, 0), (0, 0), (0, Lp - L), (0, 0), (0, 0)))

    qs, ks, vs = to_streams(q), to_streams(k), to_streams(v)
    qb = qs.reshape(B, dilation, nb, Q, H, dh)

    def band_blocks(t):
        tp = jnp.pad(t, ((0, 0), (0, 0), (Q, 0), (0, 0), (0, 0)))
        prev = tp[:, :, :Lp].reshape(B, dilation, nb, Q, H, dh)
        cur = t.reshape(B, dilation, nb, Q, H, dh)
        return jnp.concatenate([prev, cur], axis=3)

    kb, vb = band_blocks(ks), band_blocks(vs)
    scores = jnp.einsum("brnqhe,brnkhe->brnhqk", qb, kb,
                        preferred_element_type=jnp.float32) * (dh ** -0.5)
    qi = jnp.arange(Q)[:, None]
    kj = jnp.arange(2 * Q)[None, :]
    dist = qi + Q - kj
    key_pos = jnp.arange(nb)[:, None, None] * Q + kj - Q
    valid = (dist >= 0) & (dist <= n_back) & (key_pos >= 0)
    scores = jnp.where(valid[:, None], scores, NEG_INF)
    m = jnp.max(scores, axis=-1)
    p = jnp.exp(scores - m[..., None])
    l = jnp.sum(p, axis=-1)
    o = jnp.einsum("brnhqk,brnkhe->brnqhe", p.astype(v.dtype), vb,
                   preferred_element_type=jnp.float32)
    o = o / jnp.swapaxes(l, 3, 4)[..., None]

    def from_streams(t):
        t = t.reshape((B, dilation, Lp) + t.shape[4:])[:, :, :L]
        t = jnp.moveaxis(t, 1, 2)
        return t.reshape((B, S) + t.shape[3:])

    return (from_streams(o), from_streams(jnp.swapaxes(m, 3, 4)),
            from_streams(jnp.swapaxes(l, 3, 4)))


def _fwd_setup_inputs(seed: int = 0) -> dict:
    key = jax.random.key(seed)
    ks = jax.random.split(key, 16)
    f32 = jnp.float32

    def nrm(k, shape, scale):
        return jax.random.normal(k, shape, f32) * scale

    return {
        "x": nrm(ks[0], (BATCH, SEQ, D_MODEL), 1.0),
        "norm_mix_g": 1.0 + nrm(ks[1], (DEPTH, D_MODEL), 0.02),
        "w_in": nrm(ks[2], (DEPTH, D_MODEL, D_IN), D_MODEL ** -0.5),
        "b_gate": nrm(ks[3], (DEPTH, 2, D_MODEL), 0.01),
        "conv_a_w": nrm(ks[4], (DEPTH, CONV_K, CONV_WIDTH), CONV_K ** -0.5),
        "conv_a_b": nrm(ks[5], (DEPTH, CONV_WIDTH), 0.01),
        "w_proj_a": nrm(ks[6], (DEPTH, CONV_WIDTH, D_MODEL), CONV_WIDTH ** -0.5),
        "w_proj_b": nrm(ks[7], (DEPTH, ATTN_WIDTH, D_MODEL), ATTN_WIDTH ** -0.5),
        "w_out": nrm(ks[8], (DEPTH, D_MODEL, D_MODEL), D_MODEL ** -0.5),
        "norm_ffn_g": 1.0 + nrm(ks[9], (DEPTH, D_MODEL), 0.02),
        "w_up": nrm(ks[10], (DEPTH, D_MODEL, 2 * D_FF), D_MODEL ** -0.5),
        "ffn_conv_w": nrm(ks[11], (DEPTH, FFN_K, 2 * D_FF), FFN_K ** -0.5),
        "ffn_conv_b": nrm(ks[12], (DEPTH, 2 * D_FF), 0.01),
        "w_down": nrm(ks[13], (DEPTH, D_FF, D_MODEL), D_FF ** -0.5),
        "final_norm_g": 1.0 + nrm(ks[14], (D_MODEL,), 0.02),
    }


def _fwd_reference(x, norm_mix_g, w_in, b_gate, conv_a_w, conv_a_b, w_proj_a, w_proj_b,
              w_out, norm_ffn_g, w_up, ffn_conv_w, ffn_conv_b, w_down, final_norm_g):
    B, S, _ = x.shape
    split_points = [int(s) for s in np.cumsum(IN_SIZES)[:-1]]
    for layer in range(DEPTH):
        h = rms_norm(x, norm_mix_g[layer])
        proj = h @ w_in[layer]
        a_b, a_c, a_v, q, k, v, g_a, g_b = jnp.split(proj, split_points, axis=-1)

        y_a = a_b * causal_dwconv(a_c * a_v, conv_a_w[layer], conv_a_b[layer])
        y_a = y_a @ w_proj_a[layer]

        q = q.reshape(B, S, N_GROUPS, HEADS_PER_GROUP, HEAD_DIM)
        k = k.reshape(B, S, N_GROUPS, HEADS_PER_GROUP, HEAD_DIM)
        v = v.reshape(B, S, N_GROUPS, HEADS_PER_GROUP, HEAD_DIM)
        outs, ms, ls = [], [], []
        for gi, (window, dilation) in enumerate(ATTN_GROUPS):
            o_g, m_g, l_g = dilated_window_attention(
                q[:, :, gi], k[:, :, gi], v[:, :, gi], window, dilation)
            outs.append(o_g)
            ms.append(m_g)
            ls.append(l_g)
        m_all = jnp.stack(ms, axis=2)
        l_all = jnp.stack(ls, axis=2)
        o_all = jnp.stack(outs, axis=2)
        w_den = l_all * jnp.exp(m_all - jnp.max(m_all, axis=2, keepdims=True))
        alpha = w_den / jnp.sum(w_den, axis=2, keepdims=True)
        y_b = (alpha[..., None] * o_all).reshape(B, S, ATTN_WIDTH).astype(x.dtype)
        y_b = y_b @ w_proj_b[layer]

        merged = (jax.nn.sigmoid(g_a + b_gate[layer, 0]) * y_a
                  + jax.nn.sigmoid(g_b + b_gate[layer, 1]) * y_b)
        x = x + merged @ w_out[layer]

        h = rms_norm(x, norm_ffn_g[layer])
        up = causal_dwconv(h @ w_up[layer], ffn_conv_w[layer], ffn_conv_b[layer])
        gate, val = jnp.split(up, 2, axis=-1)
        x = x + (jax.nn.silu(gate) * val) @ w_down[layer]
    return rms_norm(x, final_norm_g)


import jax as _jax
import jax.numpy as _jnp

TWIN_FORMAT = 'train_step'
FWD_PARAMS = ['x', 'norm_mix_g', 'w_in', 'b_gate', 'conv_a_w', 'conv_a_b', 'w_proj_a', 'w_proj_b', 'w_out', 'norm_ffn_g', 'w_up', 'ffn_conv_w', 'ffn_conv_b', 'w_down', 'final_norm_g']
TWIN_WEIGHTS = ['norm_mix_g', 'w_in', 'b_gate', 'conv_a_w', 'conv_a_b', 'w_proj_a', 'w_proj_b', 'w_out', 'norm_ffn_g', 'w_up', 'ffn_conv_w', 'ffn_conv_b', 'w_down', 'final_norm_g']
TWIN_DIFF_INPUT = 'x'
TWIN_INPUTS = ['x', 'norm_mix_g', 'w_in', 'b_gate', 'conv_a_w', 'conv_a_b', 'w_proj_a', 'w_proj_b', 'w_out', 'norm_ffn_g', 'w_up', 'ffn_conv_w', 'ffn_conv_b', 'w_down', 'final_norm_g', 'loss_target', 'm_norm_mix_g', 'm_w_in', 'm_b_gate', 'm_conv_a_w', 'm_conv_a_b', 'm_w_proj_a', 'm_w_proj_b', 'm_w_out', 'm_norm_ffn_g', 'm_w_up', 'm_ffn_conv_w', 'm_ffn_conv_b', 'm_w_down', 'm_final_norm_g', 'v_norm_mix_g', 'v_w_in', 'v_b_gate', 'v_conv_a_w', 'v_conv_a_b', 'v_w_proj_a', 'v_w_proj_b', 'v_w_out', 'v_norm_ffn_g', 'v_w_up', 'v_ffn_conv_w', 'v_ffn_conv_b', 'v_w_down', 'v_final_norm_g']
TWIN_OUTPUTS = ['loss', 'grad_x', 'grad_norm_mix_g', 'grad_w_in', 'grad_b_gate', 'grad_conv_a_w', 'grad_conv_a_b', 'grad_w_proj_a', 'grad_w_proj_b', 'grad_w_out', 'grad_norm_ffn_g', 'grad_w_up', 'grad_ffn_conv_w', 'grad_ffn_conv_b', 'grad_w_down', 'grad_final_norm_g', 'delta_norm_mix_g', 'delta_w_in', 'delta_b_gate', 'delta_conv_a_w', 'delta_conv_a_b', 'delta_w_proj_a', 'delta_w_proj_b', 'delta_w_out', 'delta_norm_ffn_g', 'delta_w_up', 'delta_ffn_conv_w', 'delta_ffn_conv_b', 'delta_w_down', 'delta_final_norm_g', 'new_m_norm_mix_g', 'new_m_w_in', 'new_m_b_gate', 'new_m_conv_a_w', 'new_m_conv_a_b', 'new_m_w_proj_a', 'new_m_w_proj_b', 'new_m_w_out', 'new_m_norm_ffn_g', 'new_m_w_up', 'new_m_ffn_conv_w', 'new_m_ffn_conv_b', 'new_m_w_down', 'new_m_final_norm_g', 'new_v_norm_mix_g', 'new_v_w_in', 'new_v_b_gate', 'new_v_conv_a_w', 'new_v_conv_a_b', 'new_v_w_proj_a', 'new_v_w_proj_b', 'new_v_w_out', 'new_v_norm_ffn_g', 'new_v_w_up', 'new_v_ffn_conv_w', 'new_v_ffn_conv_b', 'new_v_w_down', 'new_v_final_norm_g']
TWIN_LEAF_KINDS = {'loss': 'loss', 'grad_x': 'grad_x', 'grad_norm_mix_g': 'grad_w', 'grad_w_in': 'grad_w', 'grad_b_gate': 'grad_w', 'grad_conv_a_w': 'grad_w', 'grad_conv_a_b': 'grad_w', 'grad_w_proj_a': 'grad_w', 'grad_w_proj_b': 'grad_w', 'grad_w_out': 'grad_w', 'grad_norm_ffn_g': 'grad_w', 'grad_w_up': 'grad_w', 'grad_ffn_conv_w': 'grad_w', 'grad_ffn_conv_b': 'grad_w', 'grad_w_down': 'grad_w', 'grad_final_norm_g': 'grad_w', 'delta_norm_mix_g': 'delta_w', 'delta_w_in': 'delta_w', 'delta_b_gate': 'delta_w', 'delta_conv_a_w': 'delta_w', 'delta_conv_a_b': 'delta_w', 'delta_w_proj_a': 'delta_w', 'delta_w_proj_b': 'delta_w', 'delta_w_out': 'delta_w', 'delta_norm_ffn_g': 'delta_w', 'delta_w_up': 'delta_w', 'delta_ffn_conv_w': 'delta_w', 'delta_ffn_conv_b': 'delta_w', 'delta_w_down': 'delta_w', 'delta_final_norm_g': 'delta_w', 'new_m_norm_mix_g': 'new_m', 'new_m_w_in': 'new_m', 'new_m_b_gate': 'new_m', 'new_m_conv_a_w': 'new_m', 'new_m_conv_a_b': 'new_m', 'new_m_w_proj_a': 'new_m', 'new_m_w_proj_b': 'new_m', 'new_m_w_out': 'new_m', 'new_m_norm_ffn_g': 'new_m', 'new_m_w_up': 'new_m', 'new_m_ffn_conv_w': 'new_m', 'new_m_ffn_conv_b': 'new_m', 'new_m_w_down': 'new_m', 'new_m_final_norm_g': 'new_m', 'new_v_norm_mix_g': 'new_v', 'new_v_w_in': 'new_v', 'new_v_b_gate': 'new_v', 'new_v_conv_a_w': 'new_v', 'new_v_conv_a_b': 'new_v', 'new_v_w_proj_a': 'new_v', 'new_v_w_proj_b': 'new_v', 'new_v_w_out': 'new_v', 'new_v_norm_ffn_g': 'new_v', 'new_v_w_up': 'new_v', 'new_v_ffn_conv_w': 'new_v', 'new_v_ffn_conv_b': 'new_v', 'new_v_w_down': 'new_v', 'new_v_final_norm_g': 'new_v'}


def _forward(args):
    return _fwd_reference(*[args[k] for k in FWD_PARAMS])


def _output_shape():
    def fwd():
        inp = _fwd_setup_inputs(0)
        return _fwd_reference(*[inp[k] for k in FWD_PARAMS])
    out = _jax.eval_shape(fwd)
    return out.shape, out.dtype

N_MICROBATCH = 1
ADAM_LR = 0.001
ADAM_B1 = 0.9
ADAM_B2 = 0.999
ADAM_EPS = 1e-08
ADAM_WD = 0.01
ADAM_STEP = 10
PER_EXAMPLE_BATCH_AXIS = {'x': 0, 'loss_target': 0}
SHARED_INPUTS = []
_WEIGHT_DTYPES = {'norm_mix_g': _jnp.float32, 'w_in': _jnp.float32, 'b_gate': _jnp.float32, 'conv_a_w': _jnp.float32, 'conv_a_b': _jnp.float32, 'w_proj_a': _jnp.float32, 'w_proj_b': _jnp.float32, 'w_out': _jnp.float32, 'norm_ffn_g': _jnp.float32, 'w_up': _jnp.float32, 'ffn_conv_w': _jnp.float32, 'ffn_conv_b': _jnp.float32, 'w_down': _jnp.float32, 'final_norm_g': _jnp.float32}
MOMENT_SCALE = {'norm_mix_g': 2.326957e-01, 'w_in': 9.777598e-02, 'b_gate': 3.691148e-02, 'conv_a_w': 1.953631e-01, 'conv_a_b': 1.908543e-01, 'w_proj_a': 1.321887e-01, 'w_proj_b': 1.445652e-02, 'w_out': 1.323409e-01, 'norm_ffn_g': 1.749267e-01, 'w_up': 7.469769e-02, 'ffn_conv_w': 7.292797e-02, 'ffn_conv_b': 7.163657e-02, 'w_down': 1.219842e-01, 'final_norm_g': 6.404794e+01}


def _to_microbatches(a, axis):
    t = _jnp.moveaxis(a, axis, 0)
    t = t.reshape((N_MICROBATCH, t.shape[0] // N_MICROBATCH) + t.shape[1:])
    return _jnp.moveaxis(t, 1, axis + 1)


def setup_inputs(seed: int = 0) -> dict:
    inp = _fwd_setup_inputs(seed)
    key = _jax.random.fold_in(_jax.random.key(seed), 7919)
    shape, _ = _output_shape()
    out = dict(inp)
    out["loss_target"] = _jax.random.normal(_jax.random.fold_in(key, 0), shape, _jnp.float32)
    for i, name in enumerate(TWIN_WEIGHTS):
        w = inp[name].astype(_jnp.float32)
        if MOMENT_SCALE is None:
            s = _jnp.sqrt(_jnp.mean(_jnp.square(w)) + 1e-30)
        else:
            s = MOMENT_SCALE[name]
        km, kv = _jax.random.split(_jax.random.fold_in(key, i + 1))
        out[name] = w
        out["m_" + name] = s * _jax.random.normal(km, w.shape, _jnp.float32)
        out["v_" + name] = (s * s) * _jax.random.uniform(kv, w.shape, _jnp.float32, 0.5, 1.5)
    if N_MICROBATCH > 1:
        for name, axis in PER_EXAMPLE_BATCH_AXIS.items():
            out[name] = _to_microbatches(out[name], axis)
    return {'x': out['x'], 'norm_mix_g': out['norm_mix_g'], 'w_in': out['w_in'], 'b_gate': out['b_gate'], 'conv_a_w': out['conv_a_w'], 'conv_a_b': out['conv_a_b'], 'w_proj_a': out['w_proj_a'], 'w_proj_b': out['w_proj_b'], 'w_out': out['w_out'], 'norm_ffn_g': out['norm_ffn_g'], 'w_up': out['w_up'], 'ffn_conv_w': out['ffn_conv_w'], 'ffn_conv_b': out['ffn_conv_b'], 'w_down': out['w_down'], 'final_norm_g': out['final_norm_g'], 'loss_target': out['loss_target'], 'm_norm_mix_g': out['m_norm_mix_g'], 'm_w_in': out['m_w_in'], 'm_b_gate': out['m_b_gate'], 'm_conv_a_w': out['m_conv_a_w'], 'm_conv_a_b': out['m_conv_a_b'], 'm_w_proj_a': out['m_w_proj_a'], 'm_w_proj_b': out['m_w_proj_b'], 'm_w_out': out['m_w_out'], 'm_norm_ffn_g': out['m_norm_ffn_g'], 'm_w_up': out['m_w_up'], 'm_ffn_conv_w': out['m_ffn_conv_w'], 'm_ffn_conv_b': out['m_ffn_conv_b'], 'm_w_down': out['m_w_down'], 'm_final_norm_g': out['m_final_norm_g'], 'v_norm_mix_g': out['v_norm_mix_g'], 'v_w_in': out['v_w_in'], 'v_b_gate': out['v_b_gate'], 'v_conv_a_w': out['v_conv_a_w'], 'v_conv_a_b': out['v_conv_a_b'], 'v_w_proj_a': out['v_w_proj_a'], 'v_w_proj_b': out['v_w_proj_b'], 'v_w_out': out['v_w_out'], 'v_norm_ffn_g': out['v_norm_ffn_g'], 'v_w_up': out['v_w_up'], 'v_ffn_conv_w': out['v_ffn_conv_w'], 'v_ffn_conv_b': out['v_ffn_conv_b'], 'v_w_down': out['v_w_down'], 'v_final_norm_g': out['v_final_norm_g']}


def _loss(weights, diff, rest, loss_target):
    with _jax.named_scope("forward"):
        args = {**rest, TWIN_DIFF_INPUT: diff, **{k: w.astype(_WEIGHT_DTYPES[k]) for k, w in weights.items()}}
        y = _forward(args)
    with _jax.named_scope("loss_head"):
        err = _jnp.square(y.astype(_jnp.float32) - loss_target)
        return 0.5 * _jnp.sum(_jnp.mean(err, axis=-1)) if err.ndim else 0.5 * err


def _adamw(w, g, m, v):
    m = ADAM_B1 * m + (1.0 - ADAM_B1) * g
    v = ADAM_B2 * v + (1.0 - ADAM_B2) * _jnp.square(g)
    m_hat = m / (1.0 - ADAM_B1 ** ADAM_STEP)
    v_hat = v / (1.0 - ADAM_B2 ** ADAM_STEP)
    delta = -ADAM_LR * (m_hat / (_jnp.sqrt(v_hat) + ADAM_EPS) + ADAM_WD * w)
    return delta, m, v


def reference(x, norm_mix_g, w_in, b_gate, conv_a_w, conv_a_b, w_proj_a, w_proj_b, w_out, norm_ffn_g, w_up, ffn_conv_w, ffn_conv_b, w_down, final_norm_g, loss_target, m_norm_mix_g, m_w_in, m_b_gate, m_conv_a_w, m_conv_a_b, m_w_proj_a, m_w_proj_b, m_w_out, m_norm_ffn_g, m_w_up, m_ffn_conv_w, m_ffn_conv_b, m_w_down, m_final_norm_g, v_norm_mix_g, v_w_in, v_b_gate, v_conv_a_w, v_conv_a_b, v_w_proj_a, v_w_proj_b, v_w_out, v_norm_ffn_g, v_w_up, v_ffn_conv_w, v_ffn_conv_b, v_w_down, v_final_norm_g):
    given = dict(x=x, norm_mix_g=norm_mix_g, w_in=w_in, b_gate=b_gate, conv_a_w=conv_a_w, conv_a_b=conv_a_b, w_proj_a=w_proj_a, w_proj_b=w_proj_b, w_out=w_out, norm_ffn_g=norm_ffn_g, w_up=w_up, ffn_conv_w=ffn_conv_w, ffn_conv_b=ffn_conv_b, w_down=w_down, final_norm_g=final_norm_g, loss_target=loss_target, m_norm_mix_g=m_norm_mix_g, m_w_in=m_w_in, m_b_gate=m_b_gate, m_conv_a_w=m_conv_a_w, m_conv_a_b=m_conv_a_b, m_w_proj_a=m_w_proj_a, m_w_proj_b=m_w_proj_b, m_w_out=m_w_out, m_norm_ffn_g=m_norm_ffn_g, m_w_up=m_w_up, m_ffn_conv_w=m_ffn_conv_w, m_ffn_conv_b=m_ffn_conv_b, m_w_down=m_w_down, m_final_norm_g=m_final_norm_g, v_norm_mix_g=v_norm_mix_g, v_w_in=v_w_in, v_b_gate=v_b_gate, v_conv_a_w=v_conv_a_w, v_conv_a_b=v_conv_a_b, v_w_proj_a=v_w_proj_a, v_w_proj_b=v_w_proj_b, v_w_out=v_w_out, v_norm_ffn_g=v_norm_ffn_g, v_w_up=v_w_up, v_ffn_conv_w=v_ffn_conv_w, v_ffn_conv_b=v_ffn_conv_b, v_w_down=v_w_down, v_final_norm_g=v_final_norm_g)
    weights = {n: given[n] for n in TWIN_WEIGHTS}
    shared = {n: given[n] for n in SHARED_INPUTS}
    per_example = {n: given[n] for n in ['x']}
    grad_fn = _jax.value_and_grad(_loss, argnums=(0, 1))

    def one_microbatch(ex, loss_target):
        ex = dict(ex)
        diff = ex.pop(TWIN_DIFF_INPUT)
        return grad_fn(weights, diff, {**shared, **ex}, loss_target)

    if N_MICROBATCH == 1:
        loss, (grad_w, grad_x) = one_microbatch(per_example, given["loss_target"])
    else:
        def body(carry, xs):
            loss_sum, grad_sum = carry
            l_k, (gw_k, gx_k) = one_microbatch(xs[0], xs[1])
            with _jax.named_scope("update"):
                return (loss_sum + l_k, _jax.tree.map(_jnp.add, grad_sum, gw_k)), gx_k

        init = (_jnp.zeros((), _jnp.float32), _jax.tree.map(_jnp.zeros_like, weights))
        (loss, grad_w), grad_x = _jax.lax.scan(body, init, (per_example, given["loss_target"]))
    with _jax.named_scope("update"):
        delta_w, new_m, new_v = {}, {}, {}
        for n in TWIN_WEIGHTS:
            delta_w[n], new_m[n], new_v[n] = _adamw(weights[n], grad_w[n], given["m_" + n], given["v_" + n])
    return (loss, grad_x, *[grad_w[n] for n in TWIN_WEIGHTS], *[delta_w[n] for n in TWIN_WEIGHTS],
            *[new_m[n] for n in TWIN_WEIGHTS], *[new_v[n] for n in TWIN_WEIGHTS])
```

```python
import functools

import jax
import jax.numpy as jnp
from jax import lax
from jax.experimental import pallas as pl
from jax.experimental.pallas import tpu as pltpu

F32 = jnp.float32
BF16 = jnp.bfloat16

D_MODEL = 1024
CONV_WIDTH = 512
ATTN_WIDTH = 768
GROUP_WIDTH = 256
HEAD_DIM = 64
HEADS_PER_GROUP = 4
DILATIONS = (1, 4, 16)
ATTN_BLOCK = 128
D_FF = 2816
D_IN = 5888
EPS = 1e-6
NEG_INF = -1e30
ATTN_SCALE = HEAD_DIM ** -0.5

COL_ABCV = 0
COL_Q = 1536
COL_K = 2304
COL_V = 3072
COL_GATES = 3840

ADAM_LR = 0.001
ADAM_B1 = 0.9
ADAM_B2 = 0.999
ADAM_EPS = 1e-08
ADAM_WD = 0.01
ADAM_STEP = 10

LANES = 128
SUBLANES = 8
ROW_TILE = 512
VMEM_LIMIT = 56 * 1024 * 1024

_NT = (((1,), (1,)), ((), ()))
_TN = (((0,), (0,)), ((), ()))


def _params(n_axes, vmem=VMEM_LIMIT):
    return pltpu.CompilerParams(dimension_semantics=("arbitrary",) * n_axes, vmem_limit_bytes=vmem)


def _resident(shape):
    nd = len(shape)
    return pl.BlockSpec(shape, lambda *_: (0,) * nd, pipeline_mode=pl.Buffered(1))


def _rows(tm, width, col_block=0):
    return pl.BlockSpec((tm, width), lambda i: (i, col_block))


def _col_chunks(n, cmax):
    out, lo = [], 0
    while lo < n:
        size = min(cmax, n - lo)
        out.append((lo, size))
        lo += size
    return out


def _dot(a, b):
    return jnp.dot(a, b, preferred_element_type=F32)


def _dot_nt(a, b):
    return lax.dot_general(a, b, _NT, preferred_element_type=F32)


def _dot_tn(a, b):
    return lax.dot_general(a, b, _TN, preferred_element_type=F32)


def _sigmoid(x):
    return 1.0 / (1.0 + jnp.exp(-x))


def _shift_down(v, k, halo8):
    tm = v.shape[0]
    rolled = pltpu.roll(v, k, 0)
    fix = jnp.tile(pltpu.roll(halo8, k, 0), (tm // SUBLANES, 1))
    row = lax.broadcasted_iota(jnp.int32, v.shape, 0)
    return jnp.where(row < k, fix, rolled)


def _shift_up(v, k, halo8):
    tm = v.shape[0]
    rolled = pltpu.roll(v, tm - k, 0)
    fix = jnp.tile(pltpu.roll(halo8, SUBLANES - k, 0), (tm // SUBLANES, 1))
    row = lax.broadcasted_iota(jnp.int32, v.shape, 0)
    return jnp.where(row >= tm - k, fix, rolled)


def _colsum(v):
    return jnp.sum(v, axis=0, keepdims=True)


def _to_streams(val, scr, out_ref, d, col0):
    tm = val.shape[0]
    panels = val.shape[1] // LANES
    for p in range(panels):
        scr[pl.ds(p * tm, tm), :] = val[:, p * LANES:(p + 1) * LANES]
    for r in range(d):
        for p in range(panels):
            piece = scr[pl.ds(p * tm + r, tm // d, stride=d), :]
            out_ref[r, :, col0 + p * LANES: col0 + (p + 1) * LANES] = piece.astype(out_ref.dtype)


def _from_streams(in_ref, scr, d, col0, width):
    panels = width // LANES
    rows = in_ref.shape[1]
    tm = rows * d
    for r in range(d):
        for p in range(panels):
            scr[pl.ds(p * tm + r, rows, stride=d), :] = in_ref[r, :, col0 + p * LANES: col0 + (p + 1) * LANES].astype(F32)
    return jnp.concatenate([scr[pl.ds(p * tm, tm), :] for p in range(panels)], axis=1)


def _stream_block(tm, d, width):
    return pl.BlockSpec((d, tm // d, width), lambda i: (0, i, 0))


def _rev_stream_block(tm, d, width, nt):
    return pl.BlockSpec((d, tm // d, width), lambda i: (0, nt - 1 - i, 0))


def _inproj_fwd(x, g, w_in):
    s = x.shape[0]
    tm = ROW_TILE

    def body(x_ref, g_ref, w_ref, h_ref, abcv_ref, gates_ref, qkv0_ref, qkv1_ref, qkv2_ref, scr):
        xv = x_ref[...]
        r = lax.rsqrt(jnp.mean(xv * xv, axis=-1, keepdims=True) + EPS)
        h = (xv * r * g_ref[...]).astype(BF16)
        h_ref[...] = h
        for lo, size in _col_chunks(3 * CONV_WIDTH, 512):
            abcv_ref[:, lo:lo + size] = _dot(h, w_ref[:, COL_ABCV + lo: COL_ABCV + lo + size]).astype(BF16)
        for lo, size in _col_chunks(2 * D_MODEL, 512):
            gates_ref[:, lo:lo + size] = _dot(h, w_ref[:, COL_GATES + lo: COL_GATES + lo + size]).astype(BF16)
        for gi, (d, out_ref) in enumerate(zip(DILATIONS, (qkv0_ref, qkv1_ref, qkv2_ref))):
            for j, base in enumerate((COL_Q, COL_K, COL_V)):
                lo = base + gi * GROUP_WIDTH
                y = _dot(h, w_ref[:, lo:lo + GROUP_WIDTH])
                _to_streams(y, scr, out_ref, d, j * GROUP_WIDTH)

    outs = [jax.ShapeDtypeStruct((s, D_MODEL), BF16),
            jax.ShapeDtypeStruct((s, 3 * CONV_WIDTH), BF16),
            jax.ShapeDtypeStruct((s, 2 * D_MODEL), BF16)]
    outs += [jax.ShapeDtypeStruct((d, s // d, 3 * GROUP_WIDTH), BF16) for d in DILATIONS]
    return pl.pallas_call(
        body, name="inproj_fwd", grid=(s // tm,), out_shape=outs,
        in_specs=[_rows(tm, D_MODEL), _resident((1, D_MODEL)), _resident((D_MODEL, D_IN))],
        out_specs=[_rows(tm, D_MODEL), _rows(tm, 3 * CONV_WIDTH), _rows(tm, 2 * D_MODEL)]
        + [_stream_block(tm, d, 3 * GROUP_WIDTH) for d in DILATIONS],
        scratch_shapes=[pltpu.VMEM((GROUP_WIDTH // LANES * tm, LANES), F32)],
        compiler_params=_params(1))(x, g, w_in)


def _head_of_lane(shape):
    return lax.broadcasted_iota(jnp.int32, shape, 1) // HEAD_DIM


def _stack_heads(v):
    head = _head_of_lane(v.shape)
    return jnp.concatenate([jnp.where(head == h, v, jnp.zeros_like(v)) for h in range(HEADS_PER_GROUP)], axis=0)


def _unstack_heads(v):
    q = ATTN_BLOCK
    head = _head_of_lane((q, v.shape[1]))
    out = jnp.zeros((q, v.shape[1]), v.dtype)
    for h in range(HEADS_PER_GROUP):
        out = jnp.where(head == h, v[h * q:(h + 1) * q], out)
    return out


def _per_head_rows(col):
    q = ATTN_BLOCK
    head = _head_of_lane((q, GROUP_WIDTH))
    out = jnp.zeros((q, GROUP_WIDTH), col.dtype)
    for h in range(HEADS_PER_GROUP):
        out = jnp.where(head == h, col[h * q:(h + 1) * q], out)
    return out


def _per_head_col(v):
    head = _head_of_lane(v.shape)
    cols = [jnp.max(jnp.where(head == h, v, -jnp.inf), axis=1, keepdims=True) for h in range(HEADS_PER_GROUP)]
    return jnp.concatenate(cols, axis=0)


def _band_mask(first_block):
    rows = HEADS_PER_GROUP * ATTN_BLOCK
    qi = lax.broadcasted_iota(jnp.int32, (rows, 2 * ATTN_BLOCK), 0) % ATTN_BLOCK
    kj = lax.broadcasted_iota(jnp.int32, (rows, 2 * ATTN_BLOCK), 1)
    dist = qi + ATTN_BLOCK - kj
    return (dist >= 0) & (dist <= ATTN_BLOCK) & ((kj >= ATTN_BLOCK) | jnp.logical_not(first_block))


def _attn_fwd(qkv, gi):
    d, length, _ = qkv.shape
    nb = length // ATTN_BLOCK
    q = ATTN_BLOCK

    def body(q_ref, kp_ref, kc_ref, vp_ref, vc_ref, o_ref, lse_ref):
        n = pl.program_id(1)
        qs = _stack_heads(q_ref[...])
        k2 = jnp.concatenate([kp_ref[...], kc_ref[...]], axis=0)
        v2 = jnp.concatenate([vp_ref[...], vc_ref[...]], axis=0)
        sc = _dot_nt(qs, k2) * ATTN_SCALE
        sc = jnp.where(_band_mask(n == 0), sc, NEG_INF)
        m = jnp.max(sc, axis=1, keepdims=True)
        p = jnp.exp(sc - m)
        l = jnp.sum(p, axis=1, keepdims=True)
        of = _dot(p.astype(BF16), v2) / l
        o_ref[...] = _unstack_heads(of).astype(BF16)
        lse_ref[...] = _per_head_rows(m + jnp.log(l))

    def blk(col, prev):
        if prev:
            return pl.BlockSpec((None, q, GROUP_WIDTH), lambda r, n: (r, jnp.maximum(n - 1, 0), col))
        return pl.BlockSpec((None, q, GROUP_WIDTH), lambda r, n: (r, n, col))

    return pl.pallas_call(
        body, name=f"attn_fwd_g{gi}", grid=(d, nb),
        out_shape=[jax.ShapeDtypeStruct((d, length, GROUP_WIDTH), BF16),
                   jax.ShapeDtypeStruct((d, length, GROUP_WIDTH), F32)],
        in_specs=[blk(0, False), blk(1, True), blk(1, False), blk(2, True), blk(2, False)],
        out_specs=[blk(0, False), blk(0, False)],
        compiler_params=_params(2))(qkv, qkv, qkv, qkv, qkv)


def _conv_branch(ab, ac, av, halo_u, w, b):
    u = ac * av
    sh1 = _shift_down(u, 1, halo_u)
    sh2 = _shift_down(u, 2, halo_u)
    cv = w[0:1] * sh2 + w[1:2] * sh1 + w[2:3] * u + b
    return ab * cv, cv, u, sh1, sh2


def _mix_fwd(x, abcv, gates, o_list, lse_list, conv_w, conv_b, b_gate, w_pa, w_pb, w_out):
    s = x.shape[0]
    tm = ROW_TILE

    def body(x_ref, abcv_ref, gates_ref, o0_ref, o1_ref, o2_ref, l0_ref, l1_ref, l2_ref,
             cw_ref, cb_ref, bg_ref, wpa_ref, wpb_ref, wout_ref,
             x1_ref, ya0_ref, yb0_ref, mrg_ref, ya_ref, yb_ref, lsetot_ref, halo_ref, scr):
        i = pl.program_id(0)

        @pl.when(i == 0)
        def _():
            halo_ref[...] = jnp.zeros_like(halo_ref)

        ab = abcv_ref[:, 0:CONV_WIDTH].astype(F32)
        ac = abcv_ref[:, CONV_WIDTH:2 * CONV_WIDTH].astype(F32)
        av = abcv_ref[:, 2 * CONV_WIDTH:3 * CONV_WIDTH].astype(F32)
        ya0, _, u, _, _ = _conv_branch(ab, ac, av, halo_ref[...], cw_ref[...], cb_ref[...])
        halo_ref[...] = u[tm - SUBLANES:tm]
        ya0 = ya0.astype(BF16)
        ya0_ref[...] = ya0
        ya = _dot(ya0, wpa_ref[...])

        o_refs, l_refs = (o0_ref, o1_ref, o2_ref), (l0_ref, l1_ref, l2_ref)
        lses = [_from_streams(l_refs[g], scr, DILATIONS[g], 0, GROUP_WIDTH) for g in range(3)]
        top = jnp.maximum(jnp.maximum(lses[0], lses[1]), lses[2])
        lsetot = top + jnp.log(jnp.exp(lses[0] - top) + jnp.exp(lses[1] - top) + jnp.exp(lses[2] - top))
        lsetot_ref[...] = lsetot
        yb = jnp.zeros((tm, D_MODEL), F32)
        for g in range(3):
            og = _from_streams(o_refs[g], scr, DILATIONS[g], 0, GROUP_WIDTH)
            yb0 = (jnp.exp(lses[g] - lsetot) * og).astype(BF16)
            yb0_ref[:, g * GROUP_WIDTH:(g + 1) * GROUP_WIDTH] = yb0
            yb = yb + _dot(yb0, wpb_ref[g * GROUP_WIDTH:(g + 1) * GROUP_WIDTH, :])

        sa = _sigmoid(gates_ref[:, 0:D_MODEL].astype(F32) + bg_ref[0:1, :])
        sb = _sigmoid(gates_ref[:, D_MODEL:2 * D_MODEL].astype(F32) + bg_ref[1:2, :])
        ya_ref[...] = ya.astype(BF16)
        yb_ref[...] = yb.astype(BF16)
        mrg = (sa * ya + sb * yb).astype(BF16)
        mrg_ref[...] = mrg
        x1_ref[...] = x_ref[...] + _dot(mrg, wout_ref[...])

    outs = [jax.ShapeDtypeStruct((s, D_MODEL), F32),
            jax.ShapeDtypeStruct((s, CONV_WIDTH), BF16),
            jax.ShapeDtypeStruct((s, ATTN_WIDTH), BF16),
            jax.ShapeDtypeStruct((s, D_MODEL), BF16),
            jax.ShapeDtypeStruct((s, D_MODEL), BF16),
            jax.ShapeDtypeStruct((s, D_MODEL), BF16),
            jax.ShapeDtypeStruct((s, GROUP_WIDTH), F32)]
    return pl.pallas_call(
        body, name="mix_fwd", grid=(s // tm,), out_shape=outs,
        in_specs=[_rows(tm, D_MODEL), _rows(tm, 3 * CONV_WIDTH), _rows(tm, 2 * D_MODEL)]
        + [_stream_block(tm, d, GROUP_WIDTH) for d in DILATIONS] * 2
        + [_resident((3, CONV_WIDTH)), _resident((1, CONV_WIDTH)), _resident((2, D_MODEL)),
           _resident((CONV_WIDTH, D_MODEL)), _resident((ATTN_WIDTH, D_MODEL)), _resident((D_MODEL, D_MODEL))],
        out_specs=[_rows(tm, D_MODEL), _rows(tm, CONV_WIDTH), _rows(tm, ATTN_WIDTH), _rows(tm, D_MODEL),
                   _rows(tm, D_MODEL), _rows(tm, D_MODEL), _rows(tm, GROUP_WIDTH)],
        scratch_shapes=[pltpu.VMEM((SUBLANES, CONV_WIDTH), F32),
                        pltpu.VMEM((GROUP_WIDTH // LANES * tm, LANES), F32)],
        compiler_params=_params(1))(x, abcv, gates, *o_list, *lse_list, conv_w, conv_b, b_gate, w_pa, w_pb, w_out)


def _ffn_up_fwd(x1, g, w_up):
    s = x1.shape[0]
    n = w_up.shape[1]
    tm = ROW_TILE

    def body(x_ref, g_ref, w_ref, h_ref, y_ref):
        xv = x_ref[...]
        r = lax.rsqrt(jnp.mean(xv * xv, axis=-1, keepdims=True) + EPS)
        h = (xv * r * g_ref[...]).astype(BF16)
        h_ref[...] = h
        for lo, size in _col_chunks(n, 512):
            y_ref[:, lo:lo + size] = _dot(h, w_ref[:, lo:lo + size]).astype(BF16)

    return pl.pallas_call(
        body, name="ffn_up_fwd", grid=(s // tm,),
        out_shape=[jax.ShapeDtypeStruct((s, D_MODEL), BF16), jax.ShapeDtypeStruct((s, n), BF16)],
        in_specs=[_rows(tm, D_MODEL), _resident((1, D_MODEL)), _resident((D_MODEL, n))],
        out_specs=[_rows(tm, D_MODEL), _rows(tm, n)],
        compiler_params=_params(1))(x1, g, w_up)


FFN_CHUNK = 256


def _ffn_conv(up0_ref, halo16, lo, w_ref, b_ref):
    cur = up0_ref[:, lo:lo + FFN_CHUNK].astype(F32)
    halo8 = halo16[SUBLANES:2 * SUBLANES]
    sh1 = _shift_down(cur, 1, halo8)
    sh2 = _shift_down(cur, 2, halo8)
    w = w_ref[:, lo:lo + FFN_CHUNK]
    up = w[0:1] * sh2 + w[1:2] * sh1 + w[2:3] * cur + b_ref[:, lo:lo + FFN_CHUNK]
    return up, cur, sh1, sh2


def _ffn_act_fwd(x1, up0, target, conv_w, conv_b, w_down, g_final):
    s = x1.shape[0]
    tm = ROW_TILE

    def body(x1_ref, up0_ref, tgt_ref, cw_ref, cb_ref, wd_ref, gf_ref,
             act_ref, dx2_ref, dgf_ref, loss_ref, halo_ref):
        i = pl.program_id(0)

        @pl.when(i == 0)
        def _():
            halo_ref[...] = jnp.zeros_like(halo_ref)
            dgf_ref[...] = jnp.zeros_like(dgf_ref)
            loss_ref[...] = jnp.zeros_like(loss_ref)

        acc = jnp.zeros((tm, D_MODEL), F32)
        for lo, _ in _col_chunks(D_FF, FFN_CHUNK):
            gate, _, _, _ = _ffn_conv(up0_ref, halo_ref[:, lo:lo + FFN_CHUNK].astype(F32), lo, cw_ref, cb_ref)
            val, _, _, _ = _ffn_conv(up0_ref, halo_ref[:, D_FF + lo:D_FF + lo + FFN_CHUNK].astype(F32),
                                     D_FF + lo, cw_ref, cb_ref)
            act = (gate * _sigmoid(gate) * val).astype(BF16)
            act_ref[:, lo:lo + FFN_CHUNK] = act
            acc = acc + _dot(act, wd_ref[lo:lo + FFN_CHUNK, :])
        halo_ref[...] = up0_ref[tm - 2 * SUBLANES:tm, :]

        x2 = x1_ref[...] + acc
        r = lax.rsqrt(jnp.mean(x2 * x2, axis=-1, keepdims=True) + EPS)
        xn = x2 * r
        gf = gf_ref[...]
        err = xn * gf - tgt_ref[...]
        loss_ref[...] += (0.5 / D_MODEL) * jnp.sum(err * err)
        dy = err * (1.0 / D_MODEL)
        dgf_ref[...] += _colsum(dy * xn)
        dxn = dy * gf
        dx2_ref[...] = r * (dxn - xn * jnp.mean(dxn * xn, axis=-1, keepdims=True))

    return pl.pallas_call(
        body, name="ffn_act_fwd", grid=(s // tm,),
        out_shape=[jax.ShapeDtypeStruct((s, D_FF), BF16), jax.ShapeDtypeStruct((s, D_MODEL), F32),
                   jax.ShapeDtypeStruct((1, D_MODEL), F32), jax.ShapeDtypeStruct((1, LANES), F32)],
        in_specs=[_rows(tm, D_MODEL), _rows(tm, 2 * D_FF), _rows(tm, D_MODEL),
                  _resident((3, 2 * D_FF)), _resident((1, 2 * D_FF)), _resident((D_FF, D_MODEL)),
                  _resident((1, D_MODEL))],
        out_specs=[_rows(tm, D_FF), _rows(tm, D_MODEL),
                   pl.BlockSpec((1, D_MODEL), lambda i: (0, 0)), pl.BlockSpec((1, LANES), lambda i: (0, 0))],
        scratch_shapes=[pltpu.VMEM((2 * SUBLANES, 2 * D_FF), BF16)],
        compiler_params=_params(1))(x1, up0, target, conv_w, conv_b, w_down, g_final)


def _ffn_act_bwd(dx2, up0, conv_w, conv_b, w_down):
    s = dx2.shape[0]
    tm = ROW_TILE
    nt = s // tm
    hb = tm // (2 * SUBLANES)

    def body(dx2_ref, up0_ref, pre_ref, cw_ref, cb_ref, wd_ref, dup0_ref, small_ref, next_ref):
        i = pl.program_id(0)

        @pl.when(i == 0)
        def _():
            next_ref[...] = jnp.zeros_like(next_ref)
            small_ref[...] = jnp.zeros_like(small_ref)

        not_first = (i < nt - 1).astype(F32)
        dx2b = dx2_ref[...].astype(BF16)
        for lo, _ in _col_chunks(D_FF, FFN_CHUNK):
            glo, vlo = lo, D_FF + lo
            gate, g0, g1, g2 = _ffn_conv(up0_ref, pre_ref[:, glo:glo + FFN_CHUNK].astype(F32) * not_first,
                                         glo, cw_ref, cb_ref)
            val, v0, v1, v2 = _ffn_conv(up0_ref, pre_ref[:, vlo:vlo + FFN_CHUNK].astype(F32) * not_first,
                                        vlo, cw_ref, cb_ref)
            dact = _dot_nt(dx2b, wd_ref[lo:lo + FFN_CHUNK, :])
            sg = _sigmoid(gate)
            dval = dact * (gate * sg)
            dgate = dact * val * (sg * (1.0 + gate * (1.0 - sg)))
            for clo, dz, z0, z1, z2 in ((glo, dgate, g0, g1, g2), (vlo, dval, v0, v1, v2)):
                small_ref[0:1, clo:clo + FFN_CHUNK] += _colsum(dz * z2)
                small_ref[1:2, clo:clo + FFN_CHUNK] += _colsum(dz * z1)
                small_ref[2:3, clo:clo + FFN_CHUNK] += _colsum(dz * z0)
                small_ref[3:4, clo:clo + FFN_CHUNK] += _colsum(dz)
                nxt = next_ref[:, clo:clo + FFN_CHUNK]
                w = cw_ref[:, clo:clo + FFN_CHUNK]
                dz0 = w[2:3] * dz + w[1:2] * _shift_up(dz, 1, nxt) + w[0:1] * _shift_up(dz, 2, nxt)
                dup0_ref[:, clo:clo + FFN_CHUNK] = dz0.astype(BF16)
                next_ref[:, clo:clo + FFN_CHUNK] = dz[0:SUBLANES]

    rev = lambda i: (nt - 1 - i, 0)
    pre = lambda i: (jnp.maximum((nt - 1 - i) * hb - 1, 0), 0)
    return pl.pallas_call(
        body, name="ffn_act_bwd", grid=(nt,),
        out_shape=[jax.ShapeDtypeStruct((s, 2 * D_FF), BF16), jax.ShapeDtypeStruct((SUBLANES, 2 * D_FF), F32)],
        in_specs=[pl.BlockSpec((tm, D_MODEL), rev), pl.BlockSpec((tm, 2 * D_FF), rev),
                  pl.BlockSpec((2 * SUBLANES, 2 * D_FF), pre),
                  _resident((3, 2 * D_FF)), _resident((1, 2 * D_FF)), _resident((D_FF, D_MODEL))],
        out_specs=[pl.BlockSpec((tm, 2 * D_FF), rev), pl.BlockSpec((SUBLANES, 2 * D_FF), lambda i: (0, 0))],
        scratch_shapes=[pltpu.VMEM((SUBLANES, 2 * D_FF), F32)],
        compiler_params=_params(1))(dx2, up0, up0, conv_w, conv_b, w_down)


def _norm_bwd(name, dys, w, pieces, xin, g, dres, stream_dils=None):
    s = xin.shape[0]
    tm = ROW_TILE
    nk = len(dys)
    dils = stream_dils or (1,) * nk
    widths = [dy.shape[-1] for dy in dys]
    n_total = w.shape[1]
    relayout = [k for k in range(nk) if dils[k] > 1]

    def body(*refs):
        dy_refs = refs[:nk]
        w_ref, x_ref, g_ref, dres_ref = refs[nk:nk + 4]
        dx_ref, dg_ref = refs[nk + 4:nk + 6]
        nat_refs = refs[nk + 6:nk + 6 + len(relayout)]
        scr = refs[-1]
        i = pl.program_id(0)

        @pl.when(i == 0)
        def _():
            dg_ref[...] = jnp.zeros_like(dg_ref)

        dh = jnp.zeros((tm, D_MODEL), F32)
        for k in range(nk):
            if dils[k] > 1:
                nat_ref = nat_refs[relayout.index(k)]
                for lo, size in _col_chunks(widths[k], GROUP_WIDTH):
                    nat_ref[:, lo:lo + size] = _from_streams(dy_refs[k], scr, dils[k], lo, size).astype(BF16)
                src = nat_ref
            else:
                src = dy_refs[k]
            for first, width, wcol in pieces[k]:
                for lo, size in _col_chunks(width, 512):
                    dh = dh + _dot_nt(src[:, first + lo:first + lo + size], w_ref[:, wcol + lo:wcol + lo + size])
        xv = x_ref[...]
        r = lax.rsqrt(jnp.mean(xv * xv, axis=-1, keepdims=True) + EPS)
        xn = xv * r
        dg_ref[...] += _colsum(dh * xn)
        dxn = dh * g_ref[...]
        dx_ref[...] = dres_ref[...] + r * (dxn - xn * jnp.mean(dxn * xn, axis=-1, keepdims=True))

    dy_specs = [(_stream_block(tm, dils[k], widths[k]) if dils[k] > 1 else _rows(tm, widths[k])) for k in range(nk)]
    outs = [jax.ShapeDtypeStruct((s, D_MODEL), F32), jax.ShapeDtypeStruct((1, D_MODEL), F32)]
    outs += [jax.ShapeDtypeStruct((s, widths[k]), BF16) for k in relayout]
    return pl.pallas_call(
        body, name=name, grid=(s // tm,), out_shape=outs,
        in_specs=dy_specs + [_resident((D_MODEL, n_total)), _rows(tm, D_MODEL), _resident((1, D_MODEL)),
                             _rows(tm, D_MODEL)],
        out_specs=[_rows(tm, D_MODEL), pl.BlockSpec((1, D_MODEL), lambda i: (0, 0))]
        + [_rows(tm, widths[k]) for k in relayout],
        scratch_shapes=[pltpu.VMEM((GROUP_WIDTH // LANES * tm, LANES), F32)],
        compiler_params=_params(1))(*dys, w, xin, g, dres)


def _mix_bwd(dx1, abcv, gates, ya, yb, yb0, lsetot, conv_w, conv_b, b_gate, w_pa, w_pb, w_out):
    s = dx1.shape[0]
    tm = ROW_TILE
    nt = s // tm
    hb = tm // (2 * SUBLANES)

    def body(dx1_ref, abcv_ref, pre_ref, gates_ref, ya_ref, yb_ref, yb0_ref, lsetot_ref,
             cw_ref, cb_ref, bg_ref, wpa_ref, wpb_ref, wout_ref,
             dya_ref, dyb_ref, dgates_ref, dabcv_ref, dyb0_ref, dyl0_ref, dyl1_ref, dyl2_ref, aux0_ref, aux1_ref,
             aux2_ref, sm_gate_ref, sm_conv_ref, next_ref, scr):
        i = pl.program_id(0)

        @pl.when(i == 0)
        def _():
            next_ref[...] = jnp.zeros_like(next_ref)
            sm_gate_ref[...] = jnp.zeros_like(sm_gate_ref)
            sm_conv_ref[...] = jnp.zeros_like(sm_conv_ref)

        not_first = (i < nt - 1).astype(F32)
        dm = _dot_nt(dx1_ref[...].astype(BF16), wout_ref[...])
        sa = _sigmoid(gates_ref[:, 0:D_MODEL].astype(F32) + bg_ref[0:1, :])
        sb = _sigmoid(gates_ref[:, D_MODEL:2 * D_MODEL].astype(F32) + bg_ref[1:2, :])
        dya = (dm * sa).astype(BF16)
        dyb = (dm * sb).astype(BF16)
        dya_ref[...] = dya
        dyb_ref[...] = dyb
        dga = dm * ya_ref[...].astype(F32) * (sa * (1.0 - sa))
        dgb = dm * yb_ref[...].astype(F32) * (sb * (1.0 - sb))
        dgates_ref[:, 0:D_MODEL] = dga.astype(BF16)
        dgates_ref[:, D_MODEL:2 * D_MODEL] = dgb.astype(BF16)
        sm_gate_ref[0:1, :] += _colsum(dga)
        sm_gate_ref[1:2, :] += _colsum(dgb)

        dya0 = _dot_nt(dya, wpa_ref[...])
        ab = abcv_ref[:, 0:CONV_WIDTH].astype(F32)
        ac = abcv_ref[:, CONV_WIDTH:2 * CONV_WIDTH].astype(F32)
        av = abcv_ref[:, 2 * CONV_WIDTH:3 * CONV_WIDTH].astype(F32)
        pre = pre_ref[...].astype(F32) * not_first
        halo_u = (pre[:, CONV_WIDTH:2 * CONV_WIDTH] * pre[:, 2 * CONV_WIDTH:3 * CONV_WIDTH])[SUBLANES:2 * SUBLANES]
        w = cw_ref[...]
        _, cv, u, sh1, sh2 = _conv_branch(ab, ac, av, halo_u, w, cb_ref[...])
        dcv = dya0 * ab
        sm_conv_ref[0:1, :] += _colsum(dcv * sh2)
        sm_conv_ref[1:2, :] += _colsum(dcv * sh1)
        sm_conv_ref[2:3, :] += _colsum(dcv * u)
        sm_conv_ref[3:4, :] += _colsum(dcv)
        nxt = next_ref[...]
        du = w[2:3] * dcv + w[1:2] * _shift_up(dcv, 1, nxt) + w[0:1] * _shift_up(dcv, 2, nxt)
        next_ref[...] = dcv[0:SUBLANES]
        dabcv_ref[:, 0:CONV_WIDTH] = (dya0 * cv).astype(BF16)
        dabcv_ref[:, CONV_WIDTH:2 * CONV_WIDTH] = (du * av).astype(BF16)
        dabcv_ref[:, 2 * CONV_WIDTH:3 * CONV_WIDTH] = (du * ac).astype(BF16)

        head_r = lax.broadcasted_iota(jnp.int32, (GROUP_WIDTH, GROUP_WIDTH), 0) // HEAD_DIM
        head_c = lax.broadcasted_iota(jnp.int32, (GROUP_WIDTH, GROUP_WIDTH), 1) // HEAD_DIM
        same_head = (head_r == head_c).astype(BF16)
        prod = jnp.zeros((tm, GROUP_WIDTH), F32)
        dyb0s = []
        for g in range(3):
            cols = slice(g * GROUP_WIDTH, (g + 1) * GROUP_WIDTH)
            dyb0 = _dot_nt(dyb, wpb_ref[cols, :])
            dyb0_ref[:, cols] = dyb0.astype(BF16)
            dyb0s.append(dyb0)
            prod = prod + dyb0 * yb0_ref[:, cols].astype(F32)
        hi = prod.astype(BF16)
        mid = (prod - hi.astype(F32)).astype(BF16)
        lo = (prod - hi.astype(F32) - mid.astype(F32)).astype(BF16)
        delta = _dot(hi, same_head) + _dot(mid, same_head) + _dot(lo, same_head)
        lsetot = lsetot_ref[...]
        for g, (dy_ref, aux_ref) in enumerate(zip((dyl0_ref, dyl1_ref, dyl2_ref), (aux0_ref, aux1_ref, aux2_ref))):
            d = DILATIONS[g]
            _to_streams(dyb0s[g], scr, dy_ref, d, 0)
            _to_streams(lsetot, scr, aux_ref, d, 0)
            _to_streams(delta, scr, aux_ref, d, GROUP_WIDTH)

    rev = lambda i: (nt - 1 - i, 0)
    pre = lambda i: (jnp.maximum((nt - 1 - i) * hb - 1, 0), 0)
    rows = lambda width: pl.BlockSpec((tm, width), rev)
    outs = [jax.ShapeDtypeStruct((s, D_MODEL), BF16), jax.ShapeDtypeStruct((s, D_MODEL), BF16),
            jax.ShapeDtypeStruct((s, 2 * D_MODEL), BF16), jax.ShapeDtypeStruct((s, 3 * CONV_WIDTH), BF16),
            jax.ShapeDtypeStruct((s, ATTN_WIDTH), BF16)]
    outs += [jax.ShapeDtypeStruct((d, s // d, GROUP_WIDTH), BF16) for d in DILATIONS]
    outs += [jax.ShapeDtypeStruct((d, s // d, 2 * GROUP_WIDTH), F32) for d in DILATIONS]
    outs += [jax.ShapeDtypeStruct((SUBLANES, D_MODEL), F32), jax.ShapeDtypeStruct((SUBLANES, CONV_WIDTH), F32)]
    return pl.pallas_call(
        body, name="mix_bwd", grid=(nt,), out_shape=outs,
        in_specs=[rows(D_MODEL), rows(3 * CONV_WIDTH), pl.BlockSpec((2 * SUBLANES, 3 * CONV_WIDTH), pre),
                  rows(2 * D_MODEL), rows(D_MODEL), rows(D_MODEL), rows(ATTN_WIDTH), rows(GROUP_WIDTH),
                  _resident((3, CONV_WIDTH)), _resident((1, CONV_WIDTH)), _resident((2, D_MODEL)),
                  _resident((CONV_WIDTH, D_MODEL)), _resident((ATTN_WIDTH, D_MODEL)), _resident((D_MODEL, D_MODEL))],
        out_specs=[rows(D_MODEL), rows(D_MODEL), rows(2 * D_MODEL), rows(3 * CONV_WIDTH), rows(ATTN_WIDTH)]
        + [_rev_stream_block(tm, d, GROUP_WIDTH, nt) for d in DILATIONS]
        + [_rev_stream_block(tm, d, 2 * GROUP_WIDTH, nt) for d in DILATIONS]
        + [pl.BlockSpec((SUBLANES, D_MODEL), lambda i: (0, 0)), pl.BlockSpec((SUBLANES, CONV_WIDTH), lambda i: (0, 0))],
        scratch_shapes=[pltpu.VMEM((SUBLANES, CONV_WIDTH), F32),
                        pltpu.VMEM((GROUP_WIDTH // LANES * tm, LANES), F32)],
        compiler_params=_params(1))(dx1, abcv, abcv, gates, ya, yb, yb0, lsetot,
                                    conv_w, conv_b, b_gate, w_pa, w_pb, w_out)


def _attn_bwd(qkv, dy, aux, gi):
    d, length, _ = qkv.shape
    nb = length // ATTN_BLOCK
    q = ATTN_BLOCK
    gw = GROUP_WIDTH

    def body(q_ref, kp_ref, kc_ref, vp_ref, vc_ref, dy_ref, aux_ref, out_ref, dq_ref, dkv_ref):
        n = pl.program_id(1)

        @pl.when(n > 0)
        def _():
            out_ref[:, 0:gw] = dq_ref[...].astype(BF16)

        @pl.when(n == nb)
        def _():
            out_ref[:, gw:2 * gw] = dkv_ref[0].astype(BF16)
            out_ref[:, 2 * gw:3 * gw] = dkv_ref[1].astype(BF16)

        @pl.when(n < nb)
        def _():
            qs = _stack_heads(q_ref[...])
            dys = _stack_heads(dy_ref[...])
            k2 = jnp.concatenate([kp_ref[...], kc_ref[...]], axis=0)
            v2 = jnp.concatenate([vp_ref[...], vc_ref[...]], axis=0)
            lse = _per_head_col(aux_ref[:, 0:gw])
            delta = _per_head_col(aux_ref[:, gw:2 * gw])
            sc = _dot_nt(qs, k2) * ATTN_SCALE
            p = jnp.where(_band_mask(n == 0), jnp.exp(sc - lse), 0.0)
            dp = _dot_nt(dys, v2)
            ds = (p * (dp - delta) * ATTN_SCALE).astype(BF16)
            dq_ref[...] = _unstack_heads(_dot(ds, k2))
            dk2 = _dot_tn(ds, qs)
            dv2 = _dot_tn(p.astype(BF16), dys)

            @pl.when(n > 0)
            def _():
                out_ref[:, gw:2 * gw] = (dkv_ref[0] + dk2[0:q]).astype(BF16)
                out_ref[:, 2 * gw:3 * gw] = (dkv_ref[1] + dv2[0:q]).astype(BF16)

            dkv_ref[0] = dk2[q:2 * q]
            dkv_ref[1] = dv2[q:2 * q]

    last = nb - 1

    def blk(col, shift, width=gw):
        if shift:
            return pl.BlockSpec((None, q, width), lambda r, n: (r, jnp.maximum(n - 1, 0), col))
        return pl.BlockSpec((None, q, width), lambda r, n: (r, jnp.minimum(n, last), col))

    return pl.pallas_call(
        body, name=f"attn_bwd_g{gi}", grid=(d, nb + 1),
        out_shape=jax.ShapeDtypeStruct((d, length, 3 * gw), BF16),
        in_specs=[blk(0, False), blk(1, True), blk(1, False), blk(2, True), blk(2, False),
                  blk(0, False), blk(0, False, 2 * gw)],
        out_specs=pl.BlockSpec((None, q, 3 * gw), lambda r, n: (r, jnp.maximum(n - 1, 0), 0)),
        scratch_shapes=[pltpu.VMEM((q, gw), F32), pltpu.VMEM((2, q, gw), F32)],
        compiler_params=_params(2))(qkv, qkv, qkv, qkv, qkv, dy, aux)


def _matmul_tn(name, a, b, row_tile=1024, col_tile=1024):
    s, k = a.shape
    n = b.shape[1]
    tk = min(row_tile, s)
    tn = col_tile
    while n % tn:
        tn -= LANES
    steps = s // tk

    def body(a_ref, b_ref, o_ref):
        @pl.when(pl.program_id(1) == 0)
        def _():
            o_ref[...] = jnp.zeros_like(o_ref)

        o_ref[...] += _dot_tn(a_ref[...], b_ref[...])

    return pl.pallas_call(
        body, name=name, grid=(n // tn, steps),
        out_shape=jax.ShapeDtypeStruct((k, n), F32),
        in_specs=[pl.BlockSpec((tk, k), lambda j, t: (t, 0)), pl.BlockSpec((tk, tn), lambda j, t: (t, j))],
        out_specs=pl.BlockSpec((k, tn), lambda j, t: (0, j)),
        compiler_params=_params(2))(a, b)


def _local_step(x, target, g_mix, w_in, b_gate, conv_a_w, conv_a_b, w_pa, w_pb, w_out, g_ffn, w_up,
                ffn_conv_w, ffn_conv_b, w_down, g_final):
    h1, abcv, gates, qkv0, qkv1, qkv2 = _inproj_fwd(x, g_mix, w_in)
    qkvs = (qkv0, qkv1, qkv2)
    attn = [_attn_fwd(qkvs[g], g) for g in range(3)]
    x1, ya0, yb0, mrg, ya, yb, lsetot = _mix_fwd(
        x, abcv, gates, [a[0] for a in attn], [a[1] for a in attn], conv_a_w, conv_a_b, b_gate, w_pa, w_pb, w_out)
    h2, up0 = _ffn_up_fwd(x1, g_ffn, w_up)
    act, dx2, d_g_final, loss = _ffn_act_fwd(x1, up0, target, ffn_conv_w, ffn_conv_b, w_down, g_final)

    d_up0, ffn_small = _ffn_act_bwd(dx2, up0, ffn_conv_w, ffn_conv_b, w_down)
    d_w_down = _matmul_tn("dw_down", act, dx2.astype(BF16))
    dx1, d_g_ffn = _norm_bwd("ffn_up_bwd", [d_up0], w_up, [[(0, 2 * D_FF, 0)]], x1, g_ffn, dx2)
    d_w_up = _matmul_tn("dw_up", h2, d_up0)

    (d_ya, d_yb, d_gates, d_abcv, d_yb0, dyl0, dyl1, dyl2, aux0, aux1, aux2, gate_small, conv_small) = _mix_bwd(
        dx1, abcv, gates, ya, yb, yb0, lsetot, conv_a_w, conv_a_b, b_gate, w_pa, w_pb, w_out)
    d_w_out = _matmul_tn("dw_out", mrg, dx1.astype(BF16))
    d_w_pa = _matmul_tn("dw_proj_a", ya0, d_ya)
    d_w_pb = _matmul_tn("dw_proj_b", yb0, d_yb)
    d_qkvs = [_attn_bwd(qkvs[g], dy, aux, g) for g, (dy, aux) in enumerate(((dyl0, aux0), (dyl1, aux1), (dyl2, aux2)))]

    dq = [d_qkvs[0][0]] + d_qkvs[1:]
    group_pieces = [[(j * GROUP_WIDTH, GROUP_WIDTH, base + g * GROUP_WIDTH) for j, base in enumerate((COL_Q, COL_K, COL_V))]
                    for g in range(3)]
    grad_x, d_g_mix, nat1, nat2 = _norm_bwd(
        "inproj_bwd", [d_abcv, d_gates] + dq, w_in,
        [[(0, 3 * CONV_WIDTH, COL_ABCV)], [(0, 2 * D_MODEL, COL_GATES)]] + group_pieces,
        x, g_mix, dx1, stream_dils=(1, 1, 1, 4, 16))
    d_w_abcv = _matmul_tn("dw_in_abcv", h1, d_abcv, col_tile=768)
    d_w_gates = _matmul_tn("dw_in_gates", h1, d_gates)
    d_w_groups = [_matmul_tn(f"dw_in_qkv{g}", h1, t, col_tile=768) for g, t in enumerate((dq[0], nat1, nat2))]
    gw = GROUP_WIDTH
    d_w_in = jnp.concatenate(
        [d_w_abcv] + [d_w_groups[g][:, j * gw:(j + 1) * gw] for j in range(3) for g in range(3)] + [d_w_gates], axis=1)
    return (loss, grad_x, d_w_in, d_w_pa, d_w_pb, d_w_out, d_w_up, d_w_down,
            d_g_mix, d_g_ffn, d_g_final, gate_small, conv_small, ffn_small)


def _assemble_grads(loss, grad_x, d_w_in, d_w_pa, d_w_pb, d_w_out, d_w_up, d_w_down,
                    d_g_mix, d_g_ffn, d_g_final, gate_small, conv_small, ffn_small):
    return {"loss": loss[0, 0], "grad_x": grad_x, "norm_mix_g": d_g_mix, "w_in": d_w_in, "b_gate": gate_small[0:2],
            "conv_a_w": conv_small[0:3], "conv_a_b": conv_small[3:4], "w_proj_a": d_w_pa, "w_proj_b": d_w_pb,
            "w_out": d_w_out, "norm_ffn_g": d_g_ffn, "w_up": d_w_up, "ffn_conv_w": ffn_small[0:3],
            "ffn_conv_b": ffn_small[3:4], "w_down": d_w_down, "final_norm_g": d_g_final[0]}


N_CHIPS = 4
MESH_ID = pl.DeviceIdType.MESH
_ANY = pl.BlockSpec(memory_space=pl.ANY)


def _mesh_position():
    x, y, c = lax.axis_index("x"), lax.axis_index("y"), lax.axis_index("c")
    other_chips = [(1 - x, y), (x, 1 - y), (1 - x, 1 - y)]
    return x, y, c, other_chips


def _weight_allgather(big, small):
    rows = big.shape[0]
    half = rows // 2

    def body(big_ref, small_ref, big_out, small_out, send_sems, recv_sems, local_sems):
        x, y, c, chips = _mesh_position()
        me = 2 * x + y
        sibling = (x, y, 1 - c)
        mine = pl.ds(pl.multiple_of(c * half, 16), half)
        theirs = pl.ds(pl.multiple_of((1 - c) * half, 16), half)

        def copy(k, src, dst, to):
            return pltpu.make_async_remote_copy(src_ref=src, dst_ref=dst, send_sem=send_sems.at[k],
                                                recv_sem=recv_sems.at[k], device_id=to, device_id_type=MESH_ID)

        own_big = pltpu.make_async_copy(big_ref, big_out.at[me], local_sems.at[0])
        own_small = pltpu.make_async_copy(small_ref, small_out.at[me], local_sems.at[1])
        own_big.start()
        own_small.start()
        first = []
        for j, (px, py) in enumerate(chips):
            first.append(copy(j, big_ref.at[mine], big_out.at[me, mine], (px, py, c)))
            first.append(copy(3 + j, small_ref, small_out.at[me], (px, py, c)))
        for cp in first:
            cp.start()
        passed = []
        for j, (px, py) in enumerate(chips):
            landed = big_out.at[2 * px + py, mine]
            copy(j, landed, landed, (px, py, c)).wait_recv()
            fwd = copy(6 + j, landed, landed, sibling)
            fwd.start()
            passed.append(fwd)
        for j, (px, py) in enumerate(chips):
            copy(3 + j, small_ref, small_out.at[2 * px + py], (px, py, c)).wait_recv()
            from_sibling = big_out.at[2 * px + py, theirs]
            copy(6 + j, from_sibling, from_sibling, sibling).wait_recv()
        for cp in first + passed:
            cp.wait_send()
        own_big.wait()
        own_small.wait()

    return pl.pallas_call(
        body, name="weight_allgather",
        out_shape=[jax.ShapeDtypeStruct((N_CHIPS,) + big.shape, big.dtype),
                   jax.ShapeDtypeStruct((N_CHIPS,) + small.shape, small.dtype)],
        in_specs=[_ANY, _ANY], out_specs=[_ANY, _ANY],
        scratch_shapes=[pltpu.SemaphoreType.DMA((9,)), pltpu.SemaphoreType.DMA((9,)), pltpu.SemaphoreType.DMA((2,))],
    )(big, small)


def _sibling_swap_halves(packed):
    n, rows, lanes = packed.shape
    half = rows // 2

    def body(src_ref, out_ref, send_sem, recv_sem):
        x, y, c, _ = _mesh_position()
        theirs = pl.ds(pl.multiple_of((1 - c) * half, 16), half)
        cp = pltpu.make_async_remote_copy(src_ref=src_ref.at[:, theirs, :], dst_ref=out_ref, send_sem=send_sem,
                                          recv_sem=recv_sem, device_id=(x, y, 1 - c), device_id_type=MESH_ID)
        cp.start()
        cp.wait()

    return pl.pallas_call(
        body, name="grad_sibling_swap", out_shape=jax.ShapeDtypeStruct((n, half, lanes), packed.dtype),
        in_specs=[_ANY], out_specs=_ANY,
        scratch_shapes=[pltpu.SemaphoreType.DMA, pltpu.SemaphoreType.DMA])(packed)


def _chip_exchange(partial):
    n, rows, lanes = partial.shape

    def body(src_ref, out_ref, send_sems, recv_sems, local_sem):
        x, y, c, chips = _mesh_position()
        me = 2 * x + y
        own = pltpu.make_async_copy(src_ref.at[me], out_ref.at[me], local_sem)
        own.start()
        sends = []
        for j, (px, py) in enumerate(chips):
            cp = pltpu.make_async_remote_copy(src_ref=src_ref.at[2 * px + py], dst_ref=out_ref.at[me],
                                              send_sem=send_sems.at[j], recv_sem=recv_sems.at[j],
                                              device_id=(px, py, c), device_id_type=MESH_ID)
            cp.start()
            sends.append(cp)
        for j, (px, py) in enumerate(chips):
            landed = out_ref.at[2 * px + py]
            pltpu.make_async_remote_copy(src_ref=landed, dst_ref=landed, send_sem=send_sems.at[j],
                                         recv_sem=recv_sems.at[j], device_id=(px, py, c),
                                         device_id_type=MESH_ID).wait_recv()
        for cp in sends:
            cp.wait_send()
        own.wait()

    return pl.pallas_call(
        body, name="grad_chip_exchange", out_shape=jax.ShapeDtypeStruct((n, rows, lanes), partial.dtype),
        in_specs=[_ANY], out_specs=_ANY,
        scratch_shapes=[pltpu.SemaphoreType.DMA((3,)), pltpu.SemaphoreType.DMA((3,)), pltpu.SemaphoreType.DMA])(partial)


def _sibling_share_half(mine):
    half, lanes = mine.shape

    def body(src_ref, out_ref, send_sem, recv_sem, local_sem):
        x, y, c, _ = _mesh_position()
        rows = pl.ds(pl.multiple_of(c * half, 8), half)
        own = pltpu.make_async_copy(src_ref, out_ref.at[rows], local_sem)
        own.start()
        cp = pltpu.make_async_remote_copy(src_ref=src_ref, dst_ref=out_ref.at[rows], send_sem=send_sem,
                                          recv_sem=recv_sem, device_id=(x, y, 1 - c), device_id_type=MESH_ID)
        cp.start()
        cp.wait()
        own.wait()

    return pl.pallas_call(
        body, name="grad_sibling_share", out_shape=jax.ShapeDtypeStruct((2 * half, lanes), mine.dtype),
        in_specs=[_ANY], out_specs=_ANY,
        scratch_shapes=[pltpu.SemaphoreType.DMA, pltpu.SemaphoreType.DMA, pltpu.SemaphoreType.DMA])(mine)


def _small_allreduce(vals):
    shape = vals.shape

    def body(in_ref, out_ref, recv_ref, send_sems, recv_sems):
        x, y, c, _ = _mesh_position()
        out_ref[...] = in_ref[...]
        for stage, peer in enumerate(((x, y, 1 - c), (x, 1 - y, c), (1 - x, y, c))):
            cp = pltpu.make_async_remote_copy(src_ref=out_ref, dst_ref=recv_ref.at[stage], send_sem=send_sems.at[stage],
                                              recv_sem=recv_sems.at[stage], device_id=peer, device_id_type=MESH_ID)
            cp.start()
            cp.wait()
            out_ref[...] = out_ref[...] + recv_ref[stage]

    vm = pl.BlockSpec(memory_space=pltpu.VMEM)
    return pl.pallas_call(
        body, name="small_allreduce", out_shape=jax.ShapeDtypeStruct(shape, vals.dtype),
        in_specs=[vm], out_specs=vm,
        scratch_shapes=[pltpu.VMEM((3,) + shape, vals.dtype), pltpu.SemaphoreType.DMA((3,)),
                        pltpu.SemaphoreType.DMA((3,))])(vals)


SUM_ROWS = 2080


def _add_sibling(packed, received, core):
    n, rows, lanes = packed.shape
    half = rows // 2
    steps = half // SUM_ROWS

    def body(core_ref, a_ref, b_ref, o_ref):
        o_ref[...] = (a_ref[...].astype(F32) + b_ref[...].astype(F32)).astype(BF16)

    grid_spec = pltpu.PrefetchScalarGridSpec(
        num_scalar_prefetch=1, grid=(n, steps),
        in_specs=[pl.BlockSpec((None, SUM_ROWS, lanes), lambda s, i, core_ref: (s, core_ref[0] * steps + i, 0)),
                  pl.BlockSpec((None, SUM_ROWS, lanes), lambda s, i, core_ref: (s, i, 0))],
        out_specs=pl.BlockSpec((None, SUM_ROWS, lanes), lambda s, i, core_ref: (s, i, 0)))
    return pl.pallas_call(body, name="grad_add_sibling", grid_spec=grid_spec,
                          out_shape=jax.ShapeDtypeStruct((n, half, lanes), BF16),
                          compiler_params=_params(2))(core, packed, received)


def _sum_chips(parts):
    n, rows, lanes = parts.shape

    def body(p_ref, o_ref):
        acc = p_ref[0].astype(F32)
        for k in range(1, n):
            acc = acc + p_ref[k].astype(F32)
        o_ref[...] = acc

    return pl.pallas_call(
        body, name="grad_sum_chips", grid=(rows // SUM_ROWS,),
        out_shape=jax.ShapeDtypeStruct((rows, lanes), F32),
        in_specs=[pl.BlockSpec((n, SUM_ROWS, lanes), lambda i: (0, i, 0))],
        out_specs=pl.BlockSpec((SUM_ROWS, lanes), lambda i: (i, 0)),
        compiler_params=_params(1))(parts)


def _adamw(name, w, g, m, v):
    rows, cols = w.shape
    tr = rows
    for cand in (512, 256, 128, 64, 32, 16, 8):
        if rows % cand == 0 and rows > cand:
            tr = cand
            break

    def body(w_ref, g_ref, m_ref, v_ref, d_ref, nm_ref, nv_ref):
        gv = g_ref[...]
        nm = ADAM_B1 * m_ref[...] + (1.0 - ADAM_B1) * gv
        nv = ADAM_B2 * v_ref[...] + (1.0 - ADAM_B2) * jnp.square(gv)
        m_hat = nm / (1.0 - ADAM_B1 ** ADAM_STEP)
        v_hat = nv / (1.0 - ADAM_B2 ** ADAM_STEP)
        d_ref[...] = -ADAM_LR * (m_hat / (jnp.sqrt(v_hat) + ADAM_EPS) + ADAM_WD * w_ref[...])
        nm_ref[...] = nm
        nv_ref[...] = nv

    spec = pl.BlockSpec((tr, cols), lambda i: (i, 0))
    return pl.pallas_call(
        body, name=name, grid=(rows // tr,), out_shape=[jax.ShapeDtypeStruct(w.shape, F32)] * 3,
        in_specs=[spec] * 4, out_specs=[spec] * 3, compiler_params=_params(1))(w, g, m, v)


BIG_WEIGHTS = (("w_in", (D_MODEL, D_IN), 1), ("w_proj_a", (CONV_WIDTH, D_MODEL), 1), ("w_proj_b", (ATTN_WIDTH, D_MODEL), 1),
               ("w_out", (D_MODEL, D_MODEL), 0), ("w_up", (D_MODEL, 2 * D_FF), 1), ("w_down", (D_FF, D_MODEL), 0))
SMALL_SHARDED = (("b_gate", (2, D_MODEL), 1), ("conv_a_w", (3, CONV_WIDTH), 1), ("ffn_conv_w", (3, 2 * D_FF), 1))


def _shard_shape(shape, axis):
    return tuple(n // N_CHIPS if a == axis else n for a, n in enumerate(shape))


def _pack(shards):
    return jnp.concatenate([s.reshape(-1, LANES) for s in shards], axis=0)


def _shard_of(full, axis, j):
    size = full.shape[axis] // N_CHIPS
    return lax.slice_in_dim(full, j * size, (j + 1) * size, axis=axis)


def _unpack_shards(packed, table):
    out, lo = [], 0
    for _, shape, axis in table:
        shp = _shard_shape(shape, axis)
        rows = shp[0] * shp[1] // LANES
        out.append(packed[lo:lo + rows].reshape(shp))
        lo += rows
    return out


def _unpack_full(gathered, table):
    out, lo = [], 0
    for _, shape, axis in table:
        shp = _shard_shape(shape, axis)
        rows = shp[0] * shp[1] // LANES
        parts = gathered[:, lo:lo + rows].reshape((N_CHIPS,) + shp)
        out.append(jnp.concatenate([parts[j] for j in range(N_CHIPS)], axis=axis))
        lo += rows
    return out


def _rows128(a):
    flat = a.reshape(-1)
    rows = -(-flat.shape[0] // LANES)
    rows = -(-rows // SUBLANES) * SUBLANES
    return jnp.pad(flat, (0, rows * LANES - flat.shape[0])).reshape(rows, LANES)


def kernel(x, norm_mix_g, w_in, b_gate, conv_a_w, conv_a_b, w_proj_a, w_proj_b, w_out, norm_ffn_g, w_up, ffn_conv_w, ffn_conv_b, w_down, final_norm_g, loss_target, m_norm_mix_g, m_w_in, m_b_gate, m_conv_a_w, m_conv_a_b, m_w_proj_a, m_w_proj_b, m_w_out, m_norm_ffn_g, m_w_up, m_ffn_conv_w, m_ffn_conv_b, m_w_down, m_final_norm_g, v_norm_mix_g, v_w_in, v_b_gate, v_conv_a_w, v_conv_a_b, v_w_proj_a, v_w_proj_b, v_w_out, v_norm_ffn_g, v_w_up, v_ffn_conv_w, v_ffn_conv_b, v_w_down, v_final_norm_g):
    core = lax.axis_index("c").astype(jnp.int32).reshape(1)
    chip = 2 * lax.axis_index("x") + lax.axis_index("y")
    weights = dict(norm_mix_g=norm_mix_g, w_in=w_in[0], b_gate=b_gate[0], conv_a_w=conv_a_w[0], conv_a_b=conv_a_b,
                   w_proj_a=w_proj_a[0], w_proj_b=w_proj_b[0], w_out=w_out[0], norm_ffn_g=norm_ffn_g, w_up=w_up[0],
                   ffn_conv_w=ffn_conv_w[0], ffn_conv_b=ffn_conv_b, w_down=w_down[0], final_norm_g=final_norm_g[None, :])
    m_in = dict(norm_mix_g=m_norm_mix_g, w_in=m_w_in[0], b_gate=m_b_gate[0], conv_a_w=m_conv_a_w[0], conv_a_b=m_conv_a_b,
                w_proj_a=m_w_proj_a[0], w_proj_b=m_w_proj_b[0], w_out=m_w_out[0], norm_ffn_g=m_norm_ffn_g, w_up=m_w_up[0],
                ffn_conv_w=m_ffn_conv_w[0], ffn_conv_b=m_ffn_conv_b, w_down=m_w_down[0],
                final_norm_g=m_final_norm_g[None, :])
    v_in = dict(norm_mix_g=v_norm_mix_g, w_in=v_w_in[0], b_gate=v_b_gate[0], conv_a_w=v_conv_a_w[0], conv_a_b=v_conv_a_b,
                w_proj_a=v_w_proj_a[0], w_proj_b=v_w_proj_b[0], w_out=v_w_out[0], norm_ffn_g=v_norm_ffn_g, w_up=v_w_up[0],
                ffn_conv_w=v_ffn_conv_w[0], ffn_conv_b=v_ffn_conv_b, w_down=v_w_down[0],
                final_norm_g=v_final_norm_g[None, :])

    big = _pack([weights[n].astype(BF16) for n, _, _ in BIG_WEIGHTS])
    small = _pack([weights[n] for n, _, _ in SMALL_SHARDED])
    big_all, small_all = _weight_allgather(big, small)
    full = dict(zip([n for n, _, _ in BIG_WEIGHTS], _unpack_full(big_all, BIG_WEIGHTS)))
    full.update(zip([n for n, _, _ in SMALL_SHARDED], _unpack_full(small_all, SMALL_SHARDED)))

    (loss, grad_x, d_w_in, d_w_pa, d_w_pb, d_w_out, d_w_up, d_w_down, d_g_mix, d_g_ffn, d_g_final,
     gate_small, conv_small, ffn_small) = _local_step(
        x[0], loss_target[0], norm_mix_g, full["w_in"], full["b_gate"], full["conv_a_w"], conv_a_b, full["w_proj_a"],
        full["w_proj_b"], full["w_out"], norm_ffn_g, full["w_up"], full["ffn_conv_w"], ffn_conv_b, full["w_down"],
        weights["final_norm_g"])

    big_grads = dict(w_in=d_w_in, w_proj_a=d_w_pa, w_proj_b=d_w_pb, w_out=d_w_out, w_up=d_w_up, w_down=d_w_down)
    packed = jnp.stack([_pack([_shard_of(big_grads[n], axis, j).astype(BF16) for n, _, axis in BIG_WEIGHTS])
                        for j in range(N_CHIPS)])
    from_sibling = _sibling_swap_halves(packed)
    chip_partial = _add_sibling(packed, from_sibling, core)
    chip_parts = _chip_exchange(chip_partial)
    my_half = _sum_chips(chip_parts)
    shard_grads = _sibling_share_half(my_half)
    grads = dict(zip([n for n, _, _ in BIG_WEIGHTS], _unpack_shards(shard_grads, BIG_WEIGHTS)))

    pieces = [("norm_mix_g", d_g_mix), ("norm_ffn_g", d_g_ffn), ("final_norm_g", d_g_final),
              ("b_gate", gate_small[0:2]), ("conv_a_w", conv_small[0:3]), ("conv_a_b", conv_small[3:4]),
              ("ffn_conv_w", ffn_small[0:3]), ("ffn_conv_b", ffn_small[3:4]), ("loss", loss[:, 0:1])]
    blocks = [_rows128(a) for _, a in pieces]
    summed = _small_allreduce(jnp.concatenate(blocks, axis=0))
    lo = 0
    small_sum = {}
    for (n, a), blk in zip(pieces, blocks):
        small_sum[n] = summed[lo:lo + blk.shape[0]].reshape(-1)[:a.size].reshape(a.shape)
        lo += blk.shape[0]
    total_loss = small_sum["loss"].reshape(())
    for n in ("norm_mix_g", "norm_ffn_g", "final_norm_g", "conv_a_b", "ffn_conv_b"):
        grads[n] = small_sum[n]
    for n, _, axis in SMALL_SHARDED:
        size = small_sum[n].shape[axis] // N_CHIPS
        grads[n] = lax.dynamic_slice_in_dim(small_sum[n], chip * size, size, axis=axis)

    names = ["norm_mix_g", "w_in", "b_gate", "conv_a_w", "conv_a_b", "w_proj_a", "w_proj_b", "w_out", "norm_ffn_g", "w_up",
             "ffn_conv_w", "ffn_conv_b", "w_down", "final_norm_g"]
    delta, new_m, new_v = {}, {}, {}
    for n in names:
        delta[n], new_m[n], new_v[n] = _adamw("adamw_" + n, weights[n], grads[n], m_in[n], v_in[n])

    def shaped(n, a):
        if n == "final_norm_g":
            return a[0]
        if n in ("norm_mix_g", "conv_a_b", "norm_ffn_g", "ffn_conv_b"):
            return a
        return a[None]

    out = [total_loss, grad_x[None]]
    for group in (grads, delta, new_m, new_v):
        out += [shaped(n, group[n]) for n in names]
    return tuple(out)
```

```python
import functools

import jax
import jax.numpy as jnp
from jax import lax
from jax.experimental import pallas as pl
from jax.experimental.pallas import tpu as pltpu

F32 = jnp.float32
BF16 = jnp.bfloat16

D_MODEL = 1024
CONV_WIDTH = 512
ATTN_WIDTH = 768
GROUP_WIDTH = 256
HEAD_DIM = 64
HEADS_PER_GROUP = 4
DILATIONS = (1, 4, 16)
ATTN_BLOCK = 128
D_FF = 2816
D_IN = 5888
EPS = 1e-6
NEG_INF = -1e30
ATTN_SCALE = HEAD_DIM ** -0.5

COL_ABCV = 0
COL_Q = 1536
COL_K = 2304
COL_V = 3072
COL_GATES = 3840

ADAM_LR = 0.001
ADAM_B1 = 0.9
ADAM_B2 = 0.999
ADAM_EPS = 1e-08
ADAM_WD = 0.01
ADAM_STEP = 10

LANES = 128
SUBLANES = 8
ROW_TILE = 512
VMEM_LIMIT = 56 * 1024 * 1024

_NT = (((1,), (1,)), ((), ()))
_TN = (((0,), (0,)), ((), ()))


def _params(n_axes, vmem=VMEM_LIMIT):
    return pltpu.CompilerParams(dimension_semantics=("arbitrary",) * n_axes, vmem_limit_bytes=vmem)


def _resident(shape):
    nd = len(shape)
    return pl.BlockSpec(shape, lambda *_: (0,) * nd, pipeline_mode=pl.Buffered(1))


def _rows(tm, width, col_block=0):
    return pl.BlockSpec((tm, width), lambda i: (i, col_block))


def _col_chunks(n, cmax):
    out, lo = [], 0
    while lo < n:
        size = min(cmax, n - lo)
        out.append((lo, size))
        lo += size
    return out


def _dot(a, b):
    return jnp.dot(a, b, preferred_element_type=F32)


def _dot_nt(a, b):
    return lax.dot_general(a, b, _NT, preferred_element_type=F32)


def _dot_tn(a, b):
    return lax.dot_general(a, b, _TN, preferred_element_type=F32)


def _sigmoid(x):
    return 1.0 / (1.0 + jnp.exp(-x))


def _shift_down(v, k, halo8):
    tm = v.shape[0]
    rolled = pltpu.roll(v, k, 0)
    fix = jnp.tile(pltpu.roll(halo8, k, 0), (tm // SUBLANES, 1))
    row = lax.broadcasted_iota(jnp.int32, v.shape, 0)
    return jnp.where(row < k, fix, rolled)


def _shift_up(v, k, halo8):
    tm = v.shape[0]
    rolled = pltpu.roll(v, tm - k, 0)
    fix = jnp.tile(pltpu.roll(halo8, SUBLANES - k, 0), (tm // SUBLANES, 1))
    row = lax.broadcasted_iota(jnp.int32, v.shape, 0)
    return jnp.where(row >= tm - k, fix, rolled)


def _colsum(v):
    return jnp.sum(v, axis=0, keepdims=True)


def _to_streams(val, scr, out_ref, d, col0):
    tm = val.shape[0]
    panels = val.shape[1] // LANES
    for p in range(panels):
        scr[pl.ds(p * tm, tm), :] = val[:, p * LANES:(p + 1) * LANES]
    for r in range(d):
        for p in range(panels):
            piece = scr[pl.ds(p * tm + r, tm // d, stride=d), :]
            out_ref[r, :, col0 + p * LANES: col0 + (p + 1) * LANES] = piece.astype(out_ref.dtype)


def _from_streams(in_ref, scr, d, col0, width):
    panels = width // LANES
    rows = in_ref.shape[1]
    tm = rows * d
    for r in range(d):
        for p in range(panels):
            scr[pl.ds(p * tm + r, rows, stride=d), :] = in_ref[r, :, col0 + p * LANES: col0 + (p + 1) * LANES].astype(F32)
    return jnp.concatenate([scr[pl.ds(p * tm, tm), :] for p in range(panels)], axis=1)


def _stream_block(tm, d, width):
    return pl.BlockSpec((d, tm // d, width), lambda i: (0, i, 0))


def _rev_stream_block(tm, d, width, nt):
    return pl.BlockSpec((d, tm // d, width), lambda i: (0, nt - 1 - i, 0))


def _inproj_fwd(x, g, w_in):
    s = x.shape[0]
    tm = ROW_TILE

    def body(x_ref, g_ref, w_ref, h_ref, abcv_ref, gates_ref, qkv0_ref, qkv1_ref, qkv2_ref, scr):
        xv = x_ref[...]
        r = lax.rsqrt(jnp.mean(xv * xv, axis=-1, keepdims=True) + EPS)
        h = (xv * r * g_ref[...]).astype(BF16)
        h_ref[...] = h
        for lo, size in _col_chunks(3 * CONV_WIDTH, 512):
            abcv_ref[:, lo:lo + size] = _dot(h, w_ref[:, COL_ABCV + lo: COL_ABCV + lo + size]).astype(BF16)
        for lo, size in _col_chunks(2 * D_MODEL, 512):
            gates_ref[:, lo:lo + size] = _dot(h, w_ref[:, COL_GATES + lo: COL_GATES + lo + size]).astype(BF16)
        for gi, (d, out_ref) in enumerate(zip(DILATIONS, (qkv0_ref, qkv1_ref, qkv2_ref))):
            for j, base in enumerate((COL_Q, COL_K, COL_V)):
                lo = base + gi * GROUP_WIDTH
                y = _dot(h, w_ref[:, lo:lo + GROUP_WIDTH])
                _to_streams(y, scr, out_ref, d, j * GROUP_WIDTH)

    outs = [jax.ShapeDtypeStruct((s, D_MODEL), BF16),
            jax.ShapeDtypeStruct((s, 3 * CONV_WIDTH), BF16),
            jax.ShapeDtypeStruct((s, 2 * D_MODEL), BF16)]
    outs += [jax.ShapeDtypeStruct((d, s // d, 3 * GROUP_WIDTH), BF16) for d in DILATIONS]
    return pl.pallas_call(
        body, name="inproj_fwd", grid=(s // tm,), out_shape=outs,
        in_specs=[_rows(tm, D_MODEL), _resident((1, D_MODEL)), _resident((D_MODEL, D_IN))],
        out_specs=[_rows(tm, D_MODEL), _rows(tm, 3 * CONV_WIDTH), _rows(tm, 2 * D_MODEL)]
        + [_stream_block(tm, d, 3 * GROUP_WIDTH) for d in DILATIONS],
        scratch_shapes=[pltpu.VMEM((GROUP_WIDTH // LANES * tm, LANES), F32)],
        compiler_params=_params(1))(x, g, w_in)


def _head_of_lane(shape):
    return lax.broadcasted_iota(jnp.int32, shape, 1) // HEAD_DIM


def _stack_heads(v):
    head = _head_of_lane(v.shape)
    return jnp.concatenate([jnp.where(head == h, v, jnp.zeros_like(v)) for h in range(HEADS_PER_GROUP)], axis=0)


def _unstack_heads(v):
    q = ATTN_BLOCK
    head = _head_of_lane((q, v.shape[1]))
    out = jnp.zeros((q, v.shape[1]), v.dtype)
    for h in range(HEADS_PER_GROUP):
        out = jnp.where(head == h, v[h * q:(h + 1) * q], out)
    return out


def _per_head_rows(col):
    q = ATTN_BLOCK
    head = _head_of_lane((q, GROUP_WIDTH))
    out = jnp.zeros((q, GROUP_WIDTH), col.dtype)
    for h in range(HEADS_PER_GROUP):
        out = jnp.where(head == h, col[h * q:(h + 1) * q], out)
    return out


def _per_head_col(v):
    head = _head_of_lane(v.shape)
    cols = [jnp.max(jnp.where(head == h, v, -jnp.inf), axis=1, keepdims=True) for h in range(HEADS_PER_GROUP)]
    return jnp.concatenate(cols, axis=0)


def _band_mask(first_block):
    rows = HEADS_PER_GROUP * ATTN_BLOCK
    qi = lax.broadcasted_iota(jnp.int32, (rows, 2 * ATTN_BLOCK), 0) % ATTN_BLOCK
    kj = lax.broadcasted_iota(jnp.int32, (rows, 2 * ATTN_BLOCK), 1)
    dist = qi + ATTN_BLOCK - kj
    return (dist >= 0) & (dist <= ATTN_BLOCK) & ((kj >= ATTN_BLOCK) | jnp.logical_not(first_block))


def _attn_fwd(qkv, gi):
    d, length, _ = qkv.shape
    nb = length // ATTN_BLOCK
    q = ATTN_BLOCK

    def body(q_ref, kp_ref, kc_ref, vp_ref, vc_ref, o_ref, lse_ref):
        n = pl.program_id(1)
        qs = _stack_heads(q_ref[...])
        k2 = jnp.concatenate([kp_ref[...], kc_ref[...]], axis=0)
        v2 = jnp.concatenate([vp_ref[...], vc_ref[...]], axis=0)
        sc = _dot_nt(qs, k2) * ATTN_SCALE
        sc = jnp.where(_band_mask(n == 0), sc, NEG_INF)
        m = jnp.max(sc, axis=1, keepdims=True)
        p = jnp.exp(sc - m)
        l = jnp.sum(p, axis=1, keepdims=True)
        of = _dot(p.astype(BF16), v2) / l
        o_ref[...] = _unstack_heads(of).astype(BF16)
        lse_ref[...] = _per_head_rows(m + jnp.log(l))

    def blk(col, prev):
        if prev:
            return pl.BlockSpec((None, q, GROUP_WIDTH), lambda r, n: (r, jnp.maximum(n - 1, 0), col))
        return pl.BlockSpec((None, q, GROUP_WIDTH), lambda r, n: (r, n, col))

    return pl.pallas_call(
        body, name=f"attn_fwd_g{gi}", grid=(d, nb),
        out_shape=[jax.ShapeDtypeStruct((d, length, GROUP_WIDTH), BF16),
                   jax.ShapeDtypeStruct((d, length, GROUP_WIDTH), F32)],
        in_specs=[blk(0, False), blk(1, True), blk(1, False), blk(2, True), blk(2, False)],
        out_specs=[blk(0, False), blk(0, False)],
        compiler_params=_params(2))(qkv, qkv, qkv, qkv, qkv)


def _conv_branch(ab, ac, av, halo_u, w, b):
    u = ac * av
    sh1 = _shift_down(u, 1, halo_u)
    sh2 = _shift_down(u, 2, halo_u)
    cv = w[0:1] * sh2 + w[1:2] * sh1 + w[2:3] * u + b
    return ab * cv, cv, u, sh1, sh2


def _mix_fwd(x, abcv, gates, o_list, lse_list, conv_w, conv_b, b_gate, w_pa, w_pb, w_out):
    s = x.shape[0]
    tm = ROW_TILE

    def body(x_ref, abcv_ref, gates_ref, o0_ref, o1_ref, o2_ref, l0_ref, l1_ref, l2_ref,
             cw_ref, cb_ref, bg_ref, wpa_ref, wpb_ref, wout_ref,
             x1_ref, ya0_ref, yb0_ref, mrg_ref, ya_ref, yb_ref, lsetot_ref, halo_ref, scr):
        i = pl.program_id(0)

        @pl.when(i == 0)
        def _():
            halo_ref[...] = jnp.zeros_like(halo_ref)

        ab = abcv_ref[:, 0:CONV_WIDTH].astype(F32)
        ac = abcv_ref[:, CONV_WIDTH:2 * CONV_WIDTH].astype(F32)
        av = abcv_ref[:, 2 * CONV_WIDTH:3 * CONV_WIDTH].astype(F32)
        ya0, _, u, _, _ = _conv_branch(ab, ac, av, halo_ref[...], cw_ref[...], cb_ref[...])
        halo_ref[...] = u[tm - SUBLANES:tm]
        ya0 = ya0.astype(BF16)
        ya0_ref[...] = ya0
        ya = _dot(ya0, wpa_ref[...])

        o_refs, l_refs = (o0_ref, o1_ref, o2_ref), (l0_ref, l1_ref, l2_ref)
        lses = [_from_streams(l_refs[g], scr, DILATIONS[g], 0, GROUP_WIDTH) for g in range(3)]
        top = jnp.maximum(jnp.maximum(lses[0], lses[1]), lses[2])
        lsetot = top + jnp.log(jnp.exp(lses[0] - top) + jnp.exp(lses[1] - top) + jnp.exp(lses[2] - top))
        lsetot_ref[...] = lsetot
        yb = jnp.zeros((tm, D_MODEL), F32)
        for g in range(3):
            og = _from_streams(o_refs[g], scr, DILATIONS[g], 0, GROUP_WIDTH)
            yb0 = (jnp.exp(lses[g] - lsetot) * og).astype(BF16)
            yb0_ref[:, g * GROUP_WIDTH:(g + 1) * GROUP_WIDTH] = yb0
            yb = yb + _dot(yb0, wpb_ref[g * GROUP_WIDTH:(g + 1) * GROUP_WIDTH, :])

        sa = _sigmoid(gates_ref[:, 0:D_MODEL].astype(F32) + bg_ref[0:1, :])
        sb = _sigmoid(gates_ref[:, D_MODEL:2 * D_MODEL].astype(F32) + bg_ref[1:2, :])
        ya_ref[...] = ya.astype(BF16)
        yb_ref[...] = yb.astype(BF16)
        mrg = (sa * ya + sb * yb).astype(BF16)
        mrg_ref[...] = mrg
        x1_ref[...] = x_ref[...] + _dot(mrg, wout_ref[...])

    outs = [jax.ShapeDtypeStruct((s, D_MODEL), F32),
            jax.ShapeDtypeStruct((s, CONV_WIDTH), BF16),
            jax.ShapeDtypeStruct((s, ATTN_WIDTH), BF16),
            jax.ShapeDtypeStruct((s, D_MODEL), BF16),
            jax.ShapeDtypeStruct((s, D_MODEL), BF16),
            jax.ShapeDtypeStruct((s, D_MODEL), BF16),
            jax.ShapeDtypeStruct((s, GROUP_WIDTH), F32)]
    return pl.pallas_call(
        body, name="mix_fwd", grid=(s // tm,), out_shape=outs,
        in_specs=[_rows(tm, D_MODEL), _rows(tm, 3 * CONV_WIDTH), _rows(tm, 2 * D_MODEL)]
        + [_stream_block(tm, d, GROUP_WIDTH) for d in DILATIONS] * 2
        + [_resident((3, CONV_WIDTH)), _resident((1, CONV_WIDTH)), _resident((2, D_MODEL)),
           _resident((CONV_WIDTH, D_MODEL)), _resident((ATTN_WIDTH, D_MODEL)), _resident((D_MODEL, D_MODEL))],
        out_specs=[_rows(tm, D_MODEL), _rows(tm, CONV_WIDTH), _rows(tm, ATTN_WIDTH), _rows(tm, D_MODEL),
                   _rows(tm, D_MODEL), _rows(tm, D_MODEL), _rows(tm, GROUP_WIDTH)],
        scratch_shapes=[pltpu.VMEM((SUBLANES, CONV_WIDTH), F32),
                        pltpu.VMEM((GROUP_WIDTH // LANES * tm, LANES), F32)],
        compiler_params=_params(1))(x, abcv, gates, *o_list, *lse_list, conv_w, conv_b, b_gate, w_pa, w_pb, w_out)


def _ffn_up_fwd(x1, g, w_up):
    s = x1.shape[0]
    n = w_up.shape[1]
    tm = ROW_TILE

    def body(x_ref, g_ref, w_ref, h_ref, y_ref):
        xv = x_ref[...]
        r = lax.rsqrt(jnp.mean(xv * xv, axis=-1, keepdims=True) + EPS)
        h = (xv * r * g_ref[...]).astype(BF16)
        h_ref[...] = h
        for lo, size in _col_chunks(n, 512):
            y_ref[:, lo:lo + size] = _dot(h, w_ref[:, lo:lo + size]).astype(BF16)

    return pl.pallas_call(
        body, name="ffn_up_fwd", grid=(s // tm,),
        out_shape=[jax.ShapeDtypeStruct((s, D_MODEL), BF16), jax.ShapeDtypeStruct((s, n), BF16)],
        in_specs=[_rows(tm, D_MODEL), _resident((1, D_MODEL)), _resident((D_MODEL, n))],
        out_specs=[_rows(tm, D_MODEL), _rows(tm, n)],
        compiler_params=_params(1))(x1, g, w_up)


FFN_CHUNK = 256


def _ffn_conv(up0_ref, halo16, lo, w_ref, b_ref):
    cur = up0_ref[:, lo:lo + FFN_CHUNK].astype(F32)
    halo8 = halo16[SUBLANES:2 * SUBLANES]
    sh1 = _shift_down(cur, 1, halo8)
    sh2 = _shift_down(cur, 2, halo8)
    w = w_ref[:, lo:lo + FFN_CHUNK]
    up = w[0:1] * sh2 + w[1:2] * sh1 + w[2:3] * cur + b_ref[:, lo:lo + FFN_CHUNK]
    return up, cur, sh1, sh2


def _ffn_act_fwd(x1, up0, target, conv_w, conv_b, w_down, g_final):
    s = x1.shape[0]
    tm = ROW_TILE

    def body(x1_ref, up0_ref, tgt_ref, cw_ref, cb_ref, wd_ref, gf_ref,
             act_ref, dx2_ref, dx2b_ref, dgf_ref, loss_ref, halo_ref):
        i = pl.program_id(0)

        @pl.when(i == 0)
        def _():
            halo_ref[...] = jnp.zeros_like(halo_ref)
            dgf_ref[...] = jnp.zeros_like(dgf_ref)
            loss_ref[...] = jnp.zeros_like(loss_ref)

        acc = jnp.zeros((tm, D_MODEL), F32)
        for lo, _ in _col_chunks(D_FF, FFN_CHUNK):
            gate, _, _, _ = _ffn_conv(up0_ref, halo_ref[:, lo:lo + FFN_CHUNK].astype(F32), lo, cw_ref, cb_ref)
            val, _, _, _ = _ffn_conv(up0_ref, halo_ref[:, D_FF + lo:D_FF + lo + FFN_CHUNK].astype(F32),
                                     D_FF + lo, cw_ref, cb_ref)
            act = (gate * _sigmoid(gate) * val).astype(BF16)
            act_ref[:, lo:lo + FFN_CHUNK] = act
            acc = acc + _dot(act, wd_ref[lo:lo + FFN_CHUNK, :])
        halo_ref[...] = up0_ref[tm - 2 * SUBLANES:tm, :]

        x2 = x1_ref[...] + acc
        r = lax.rsqrt(jnp.mean(x2 * x2, axis=-1, keepdims=True) + EPS)
        xn = x2 * r
        gf = gf_ref[...]
        err = xn * gf - tgt_ref[...]
        loss_ref[...] += (0.5 / D_MODEL) * jnp.sum(err * err)
        dy = err * (1.0 / D_MODEL)
        dgf_ref[...] += _colsum(dy * xn)
        dxn = dy * gf
        dx2 = r * (dxn - xn * jnp.mean(dxn * xn, axis=-1, keepdims=True))
        dx2_ref[...] = dx2
        dx2b_ref[...] = dx2.astype(BF16)

    return pl.pallas_call(
        body, name="ffn_act_fwd", grid=(s // tm,),
        out_shape=[jax.ShapeDtypeStruct((s, D_FF), BF16), jax.ShapeDtypeStruct((s, D_MODEL), F32),
                   jax.ShapeDtypeStruct((s, D_MODEL), BF16),
                   jax.ShapeDtypeStruct((1, D_MODEL), F32), jax.ShapeDtypeStruct((1, LANES), F32)],
        in_specs=[_rows(tm, D_MODEL), _rows(tm, 2 * D_FF), _rows(tm, D_MODEL),
                  _resident((3, 2 * D_FF)), _resident((1, 2 * D_FF)), _resident((D_FF, D_MODEL)),
                  _resident((1, D_MODEL))],
        out_specs=[_rows(tm, D_FF), _rows(tm, D_MODEL), _rows(tm, D_MODEL),
                   pl.BlockSpec((1, D_MODEL), lambda i: (0, 0)), pl.BlockSpec((1, LANES), lambda i: (0, 0))],
        scratch_shapes=[pltpu.VMEM((2 * SUBLANES, 2 * D_FF), BF16)],
        compiler_params=_params(1))(x1, up0, target, conv_w, conv_b, w_down, g_final)


def _ffn_act_bwd(dx2, up0, conv_w, conv_b, w_down):
    s = dx2.shape[0]
    tm = ROW_TILE
    nt = s // tm
    hb = tm // (2 * SUBLANES)

    def body(dx2_ref, up0_ref, pre_ref, cw_ref, cb_ref, wd_ref, dup0_ref, small_ref, next_ref):
        i = pl.program_id(0)

        @pl.when(i == 0)
        def _():
            next_ref[...] = jnp.zeros_like(next_ref)
            small_ref[...] = jnp.zeros_like(small_ref)

        not_first = (i < nt - 1).astype(F32)
        dx2b = dx2_ref[...].astype(BF16)
        for lo, _ in _col_chunks(D_FF, FFN_CHUNK):
            glo, vlo = lo, D_FF + lo
            gate, g0, g1, g2 = _ffn_conv(up0_ref, pre_ref[:, glo:glo + FFN_CHUNK].astype(F32) * not_first,
                                         glo, cw_ref, cb_ref)
            val, v0, v1, v2 = _ffn_conv(up0_ref, pre_ref[:, vlo:vlo + FFN_CHUNK].astype(F32) * not_first,
                                        vlo, cw_ref, cb_ref)
            dact = _dot_nt(dx2b, wd_ref[lo:lo + FFN_CHUNK, :])
            sg = _sigmoid(gate)
            dval = dact * (gate * sg)
            dgate = dact * val * (sg * (1.0 + gate * (1.0 - sg)))
            for clo, dz, z0, z1, z2 in ((glo, dgate, g0, g1, g2), (vlo, dval, v0, v1, v2)):
                small_ref[0:1, clo:clo + FFN_CHUNK] += _colsum(dz * z2)
                small_ref[1:2, clo:clo + FFN_CHUNK] += _colsum(dz * z1)
                small_ref[2:3, clo:clo + FFN_CHUNK] += _colsum(dz * z0)
                small_ref[3:4, clo:clo + FFN_CHUNK] += _colsum(dz)
                nxt = next_ref[:, clo:clo + FFN_CHUNK]
                w = cw_ref[:, clo:clo + FFN_CHUNK]
                dz0 = w[2:3] * dz + w[1:2] * _shift_up(dz, 1, nxt) + w[0:1] * _shift_up(dz, 2, nxt)
                dup0_ref[:, clo:clo + FFN_CHUNK] = dz0.astype(BF16)
                next_ref[:, clo:clo + FFN_CHUNK] = dz[0:SUBLANES]

    rev = lambda i: (nt - 1 - i, 0)
    pre = lambda i: (jnp.maximum((nt - 1 - i) * hb - 1, 0), 0)
    return pl.pallas_call(
        body, name="ffn_act_bwd", grid=(nt,),
        out_shape=[jax.ShapeDtypeStruct((s, 2 * D_FF), BF16), jax.ShapeDtypeStruct((SUBLANES, 2 * D_FF), F32)],
        in_specs=[pl.BlockSpec((tm, D_MODEL), rev), pl.BlockSpec((tm, 2 * D_FF), rev),
                  pl.BlockSpec((2 * SUBLANES, 2 * D_FF), pre),
                  _resident((3, 2 * D_FF)), _resident((1, 2 * D_FF)), _resident((D_FF, D_MODEL))],
        out_specs=[pl.BlockSpec((tm, 2 * D_FF), rev), pl.BlockSpec((SUBLANES, 2 * D_FF), lambda i: (0, 0))],
        scratch_shapes=[pltpu.VMEM((SUBLANES, 2 * D_FF), F32)],
        compiler_params=_params(1))(dx2, up0, up0, conv_w, conv_b, w_down)


def _norm_bwd(name, dys, w, pieces, xin, g, dres, stream_dils=None):
    s = xin.shape[0]
    tm = ROW_TILE
    nk = len(dys)
    dils = stream_dils or (1,) * nk
    widths = [dy.shape[-1] for dy in dys]
    n_total = w.shape[1]
    relayout = [k for k in range(nk) if dils[k] > 1]

    def body(*refs):
        dy_refs = refs[:nk]
        w_ref, x_ref, g_ref, dres_ref = refs[nk:nk + 4]
        dx_ref, dxb_ref, dg_ref = refs[nk + 4:nk + 7]
        nat_refs = refs[nk + 7:nk + 7 + len(relayout)]
        scr = refs[-1]
        i = pl.program_id(0)

        @pl.when(i == 0)
        def _():
            dg_ref[...] = jnp.zeros_like(dg_ref)

        dh = jnp.zeros((tm, D_MODEL), F32)
        for k in range(nk):
            if dils[k] > 1:
                nat_ref = nat_refs[relayout.index(k)]
                for lo, size in _col_chunks(widths[k], GROUP_WIDTH):
                    nat_ref[:, lo:lo + size] = _from_streams(dy_refs[k], scr, dils[k], lo, size).astype(BF16)
                src = nat_ref
            else:
                src = dy_refs[k]
            for first, width, wcol in pieces[k]:
                for lo, size in _col_chunks(width, 512):
                    dh = dh + _dot_nt(src[:, first + lo:first + lo + size], w_ref[:, wcol + lo:wcol + lo + size])
        xv = x_ref[...]
        r = lax.rsqrt(jnp.mean(xv * xv, axis=-1, keepdims=True) + EPS)
        xn = xv * r
        dg_ref[...] += _colsum(dh * xn)
        dxn = dh * g_ref[...]
        dx = dres_ref[...] + r * (dxn - xn * jnp.mean(dxn * xn, axis=-1, keepdims=True))
        dx_ref[...] = dx
        dxb_ref[...] = dx.astype(BF16)

    dy_specs = [(_stream_block(tm, dils[k], widths[k]) if dils[k] > 1 else _rows(tm, widths[k])) for k in range(nk)]
    outs = [jax.ShapeDtypeStruct((s, D_MODEL), F32), jax.ShapeDtypeStruct((s, D_MODEL), BF16),
            jax.ShapeDtypeStruct((1, D_MODEL), F32)]
    outs += [jax.ShapeDtypeStruct((s, widths[k]), BF16) for k in relayout]
    return pl.pallas_call(
        body, name=name, grid=(s // tm,), out_shape=outs,
        in_specs=dy_specs + [_resident((D_MODEL, n_total)), _rows(tm, D_MODEL), _resident((1, D_MODEL)),
                             _rows(tm, D_MODEL)],
        out_specs=[_rows(tm, D_MODEL), _rows(tm, D_MODEL), pl.BlockSpec((1, D_MODEL), lambda i: (0, 0))]
        + [_rows(tm, widths[k]) for k in relayout],
        scratch_shapes=[pltpu.VMEM((GROUP_WIDTH // LANES * tm, LANES), F32)],
        compiler_params=_params(1))(*dys, w, xin, g, dres)


def _mix_bwd(dx1, abcv, gates, ya, yb, yb0, lsetot, conv_w, conv_b, b_gate, w_pa, w_pb, w_out):
    s = dx1.shape[0]
    tm = ROW_TILE
    nt = s // tm
    hb = tm // (2 * SUBLANES)

    def body(dx1_ref, abcv_ref, pre_ref, gates_ref, ya_ref, yb_ref, yb0_ref, lsetot_ref,
             cw_ref, cb_ref, bg_ref, wpa_ref, wpb_ref, wout_ref,
             dya_ref, dyb_ref, dgates_ref, dabcv_ref, dyb0_ref, dyl0_ref, dyl1_ref, dyl2_ref, aux0_ref, aux1_ref,
             aux2_ref, sm_gate_ref, sm_conv_ref, next_ref, scr):
        i = pl.program_id(0)

        @pl.when(i == 0)
        def _():
            next_ref[...] = jnp.zeros_like(next_ref)
            sm_gate_ref[...] = jnp.zeros_like(sm_gate_ref)
            sm_conv_ref[...] = jnp.zeros_like(sm_conv_ref)

        not_first = (i < nt - 1).astype(F32)
        dm = _dot_nt(dx1_ref[...].astype(BF16), wout_ref[...])
        sa = _sigmoid(gates_ref[:, 0:D_MODEL].astype(F32) + bg_ref[0:1, :])
        sb = _sigmoid(gates_ref[:, D_MODEL:2 * D_MODEL].astype(F32) + bg_ref[1:2, :])
        dya = (dm * sa).astype(BF16)
        dyb = (dm * sb).astype(BF16)
        dya_ref[...] = dya
        dyb_ref[...] = dyb
        dga = dm * ya_ref[...].astype(F32) * (sa * (1.0 - sa))
        dgb = dm * yb_ref[...].astype(F32) * (sb * (1.0 - sb))
        dgates_ref[:, 0:D_MODEL] = dga.astype(BF16)
        dgates_ref[:, D_MODEL:2 * D_MODEL] = dgb.astype(BF16)
        sm_gate_ref[0:1, :] += _colsum(dga)
        sm_gate_ref[1:2, :] += _colsum(dgb)

        dya0 = _dot_nt(dya, wpa_ref[...])
        ab = abcv_ref[:, 0:CONV_WIDTH].astype(F32)
        ac = abcv_ref[:, CONV_WIDTH:2 * CONV_WIDTH].astype(F32)
        av = abcv_ref[:, 2 * CONV_WIDTH:3 * CONV_WIDTH].astype(F32)
        pre = pre_ref[...].astype(F32) * not_first
        halo_u = (pre[:, CONV_WIDTH:2 * CONV_WIDTH] * pre[:, 2 * CONV_WIDTH:3 * CONV_WIDTH])[SUBLANES:2 * SUBLANES]
        w = cw_ref[...]
        _, cv, u, sh1, sh2 = _conv_branch(ab, ac, av, halo_u, w, cb_ref[...])
        dcv = dya0 * ab
        sm_conv_ref[0:1, :] += _colsum(dcv * sh2)
        sm_conv_ref[1:2, :] += _colsum(dcv * sh1)
        sm_conv_ref[2:3, :] += _colsum(dcv * u)
        sm_conv_ref[3:4, :] += _colsum(dcv)
        nxt = next_ref[...]
        du = w[2:3] * dcv + w[1:2] * _shift_up(dcv, 1, nxt) + w[0:1] * _shift_up(dcv, 2, nxt)
        next_ref[...] = dcv[0:SUBLANES]
        dabcv_ref[:, 0:CONV_WIDTH] = (dya0 * cv).astype(BF16)
        dabcv_ref[:, CONV_WIDTH:2 * CONV_WIDTH] = (du * av).astype(BF16)
        dabcv_ref[:, 2 * CONV_WIDTH:3 * CONV_WIDTH] = (du * ac).astype(BF16)

        head_r = lax.broadcasted_iota(jnp.int32, (GROUP_WIDTH, GROUP_WIDTH), 0) // HEAD_DIM
        head_c = lax.broadcasted_iota(jnp.int32, (GROUP_WIDTH, GROUP_WIDTH), 1) // HEAD_DIM
        same_head = (head_r == head_c).astype(BF16)
        prod = jnp.zeros((tm, GROUP_WIDTH), F32)
        dyb0s = []
        for g in range(3):
            cols = slice(g * GROUP_WIDTH, (g + 1) * GROUP_WIDTH)
            dyb0 = _dot_nt(dyb, wpb_ref[cols, :])
            dyb0_ref[:, cols] = dyb0.astype(BF16)
            dyb0s.append(dyb0)
            prod = prod + dyb0 * yb0_ref[:, cols].astype(F32)
        hi = prod.astype(BF16)
        mid = (prod - hi.astype(F32)).astype(BF16)
        lo = (prod - hi.astype(F32) - mid.astype(F32)).astype(BF16)
        delta = _dot(hi, same_head) + _dot(mid, same_head) + _dot(lo, same_head)
        lsetot = lsetot_ref[...]
        for g, (dy_ref, aux_ref) in enumerate(zip((dyl0_ref, dyl1_ref, dyl2_ref), (aux0_ref, aux1_ref, aux2_ref))):
            d = DILATIONS[g]
            _to_streams(dyb0s[g], scr, dy_ref, d, 0)
            _to_streams(lsetot, scr, aux_ref, d, 0)
            _to_streams(delta, scr, aux_ref, d, GROUP_WIDTH)

    rev = lambda i: (nt - 1 - i, 0)
    pre = lambda i: (jnp.maximum((nt - 1 - i) * hb - 1, 0), 0)
    rows = lambda width: pl.BlockSpec((tm, width), rev)
    outs = [jax.ShapeDtypeStruct((s, D_MODEL), BF16), jax.ShapeDtypeStruct((s, D_MODEL), BF16),
            jax.ShapeDtypeStruct((s, 2 * D_MODEL), BF16), jax.ShapeDtypeStruct((s, 3 * CONV_WIDTH), BF16),
            jax.ShapeDtypeStruct((s, ATTN_WIDTH), BF16)]
    outs += [jax.ShapeDtypeStruct((d, s // d, GROUP_WIDTH), BF16) for d in DILATIONS]
    outs += [jax.ShapeDtypeStruct((d, s // d, 2 * GROUP_WIDTH), F32) for d in DILATIONS]
    outs += [jax.ShapeDtypeStruct((SUBLANES, D_MODEL), F32), jax.ShapeDtypeStruct((SUBLANES, CONV_WIDTH), F32)]
    return pl.pallas_call(
        body, name="mix_bwd", grid=(nt,), out_shape=outs,
        in_specs=[rows(D_MODEL), rows(3 * CONV_WIDTH), pl.BlockSpec((2 * SUBLANES, 3 * CONV_WIDTH), pre),
                  rows(2 * D_MODEL), rows(D_MODEL), rows(D_MODEL), rows(ATTN_WIDTH), rows(GROUP_WIDTH),
                  _resident((3, CONV_WIDTH)), _resident((1, CONV_WIDTH)), _resident((2, D_MODEL)),
                  _resident((CONV_WIDTH, D_MODEL)), _resident((ATTN_WIDTH, D_MODEL)), _resident((D_MODEL, D_MODEL))],
        out_specs=[rows(D_MODEL), rows(D_MODEL), rows(2 * D_MODEL), rows(3 * CONV_WIDTH), rows(ATTN_WIDTH)]
        + [_rev_stream_block(tm, d, GROUP_WIDTH, nt) for d in DILATIONS]
        + [_rev_stream_block(tm, d, 2 * GROUP_WIDTH, nt) for d in DILATIONS]
        + [pl.BlockSpec((SUBLANES, D_MODEL), lambda i: (0, 0)), pl.BlockSpec((SUBLANES, CONV_WIDTH), lambda i: (0, 0))],
        scratch_shapes=[pltpu.VMEM((SUBLANES, CONV_WIDTH), F32),
                        pltpu.VMEM((GROUP_WIDTH // LANES * tm, LANES), F32)],
        compiler_params=_params(1))(dx1, abcv, abcv, gates, ya, yb, yb0, lsetot,
                                    conv_w, conv_b, b_gate, w_pa, w_pb, w_out)


def _attn_bwd(qkv, dy, aux, gi):
    d, length, _ = qkv.shape
    nb = length // ATTN_BLOCK
    q = ATTN_BLOCK
    gw = GROUP_WIDTH

    def body(q_ref, kp_ref, kc_ref, vp_ref, vc_ref, dy_ref, aux_ref, out_ref, dq_ref, dkv_ref):
        n = pl.program_id(1)

        @pl.when(n > 0)
        def _():
            out_ref[:, 0:gw] = dq_ref[...].astype(BF16)

        @pl.when(n == nb)
        def _():
            out_ref[:, gw:2 * gw] = dkv_ref[0].astype(BF16)
            out_ref[:, 2 * gw:3 * gw] = dkv_ref[1].astype(BF16)

        @pl.when(n < nb)
        def _():
            qs = _stack_heads(q_ref[...])
            dys = _stack_heads(dy_ref[...])
            k2 = jnp.concatenate([kp_ref[...], kc_ref[...]], axis=0)
            v2 = jnp.concatenate([vp_ref[...], vc_ref[...]], axis=0)
            lse = _per_head_col(aux_ref[:, 0:gw])
            delta = _per_head_col(aux_ref[:, gw:2 * gw])
            sc = _dot_nt(qs, k2) * ATTN_SCALE
            p = jnp.where(_band_mask(n == 0), jnp.exp(sc - lse), 0.0)
            dp = _dot_nt(dys, v2)
            ds = (p * (dp - delta) * ATTN_SCALE).astype(BF16)
            dq_ref[...] = _unstack_heads(_dot(ds, k2))
            dk2 = _dot_tn(ds, qs)
            dv2 = _dot_tn(p.astype(BF16), dys)

            @pl.when(n > 0)
            def _():
                out_ref[:, gw:2 * gw] = (dkv_ref[0] + dk2[0:q]).astype(BF16)
                out_ref[:, 2 * gw:3 * gw] = (dkv_ref[1] + dv2[0:q]).astype(BF16)

            dkv_ref[0] = dk2[q:2 * q]
            dkv_ref[1] = dv2[q:2 * q]

    last = nb - 1

    def blk(col, shift, width=gw):
        if shift:
            return pl.BlockSpec((None, q, width), lambda r, n: (r, jnp.maximum(n - 1, 0), col))
        return pl.BlockSpec((None, q, width), lambda r, n: (r, jnp.minimum(n, last), col))

    return pl.pallas_call(
        body, name=f"attn_bwd_g{gi}", grid=(d, nb + 1),
        out_shape=jax.ShapeDtypeStruct((d, length, 3 * gw), BF16),
        in_specs=[blk(0, False), blk(1, True), blk(1, False), blk(2, True), blk(2, False),
                  blk(0, False), blk(0, False, 2 * gw)],
        out_specs=pl.BlockSpec((None, q, 3 * gw), lambda r, n: (r, jnp.maximum(n - 1, 0), 0)),
        scratch_shapes=[pltpu.VMEM((q, gw), F32), pltpu.VMEM((2, q, gw), F32)],
        compiler_params=_params(2))(qkv, qkv, qkv, qkv, qkv, dy, aux)


def _matmul_tn(name, a, b, col_tile=1024, row_tile=1024, slot_major=False):
    s, k = a.shape
    n = b.shape[1]
    tk = min(row_tile, s)
    tn = col_tile
    steps = s // tk

    def body(a_ref, b_ref, o_ref, acc_ref):
        t = pl.program_id(1)

        @pl.when(t == 0)
        def _():
            acc_ref[...] = jnp.zeros_like(acc_ref)

        acc_ref[...] += _dot_tn(a_ref[...], b_ref[...])

        @pl.when(t == steps - 1)
        def _():
            o_ref[...] = acc_ref[...].astype(BF16)

    if slot_major:
        out_shape = jax.ShapeDtypeStruct((n // tn, k, tn), BF16)
        out_spec = pl.BlockSpec((None, k, tn), lambda j, t: (j, 0, 0))
    else:
        out_shape = jax.ShapeDtypeStruct((k, n), BF16)
        out_spec = pl.BlockSpec((k, tn), lambda j, t: (0, j))
    return pl.pallas_call(
        body, name=name, grid=(n // tn, steps), out_shape=out_shape,
        in_specs=[pl.BlockSpec((tk, k), lambda j, t: (t, 0)), pl.BlockSpec((tk, tn), lambda j, t: (t, j))],
        out_specs=out_spec, scratch_shapes=[pltpu.VMEM((k, tn), F32)],
        compiler_params=_params(2))(a, b)


def _local_step(x, target, g_mix, w_in, b_gate, conv_a_w, conv_a_b, w_pa, w_pb, w_out, g_ffn, w_up,
                ffn_conv_w, ffn_conv_b, w_down, g_final):
    h1, abcv, gates, qkv0, qkv1, qkv2 = _inproj_fwd(x, g_mix, w_in)
    qkvs = (qkv0, qkv1, qkv2)
    attn = [_attn_fwd(qkvs[g], g) for g in range(3)]
    x1, ya0, yb0, mrg, ya, yb, lsetot = _mix_fwd(
        x, abcv, gates, [a[0] for a in attn], [a[1] for a in attn], conv_a_w, conv_a_b, b_gate, w_pa, w_pb, w_out)
    h2, up0 = _ffn_up_fwd(x1, g_ffn, w_up)
    act, dx2, dx2b, d_g_final, loss = _ffn_act_fwd(x1, up0, target, ffn_conv_w, ffn_conv_b, w_down, g_final)

    d_up0, ffn_small = _ffn_act_bwd(dx2, up0, ffn_conv_w, ffn_conv_b, w_down)
    d_w_down = _matmul_tn("dw_down", act, dx2b, col_tile=512)
    dx1, dx1b, d_g_ffn = _norm_bwd("ffn_up_bwd", [d_up0], w_up, [[(0, 2 * D_FF, 0)]], x1, g_ffn, dx2)
    d_w_up = _matmul_tn("dw_up", h2, d_up0, col_tile=2 * D_FF // N_CHIPS, slot_major=True)

    (d_ya, d_yb, d_gates, d_abcv, d_yb0, dyl0, dyl1, dyl2, aux0, aux1, aux2, gate_small, conv_small) = _mix_bwd(
        dx1, abcv, gates, ya, yb, yb0, lsetot, conv_a_w, conv_a_b, b_gate, w_pa, w_pb, w_out)
    d_w_out = _matmul_tn("dw_out", mrg, dx1b)
    d_w_pa = _matmul_tn("dw_proj_a", ya0, d_ya, col_tile=D_MODEL // N_CHIPS, slot_major=True)
    d_w_pb = _matmul_tn("dw_proj_b", yb0, d_yb, col_tile=D_MODEL // N_CHIPS, slot_major=True)
    d_qkvs = [_attn_bwd(qkvs[g], dy, aux, g) for g, (dy, aux) in enumerate(((dyl0, aux0), (dyl1, aux1), (dyl2, aux2)))]

    dq = [d_qkvs[0][0]] + d_qkvs[1:]
    group_pieces = [[(j * GROUP_WIDTH, GROUP_WIDTH, base + g * GROUP_WIDTH) for j, base in enumerate((COL_Q, COL_K, COL_V))]
                    for g in range(3)]
    grad_x, _, d_g_mix, nat1, nat2 = _norm_bwd(
        "inproj_bwd", [d_abcv, d_gates] + dq, w_in,
        [[(0, 3 * CONV_WIDTH, COL_ABCV)], [(0, 2 * D_MODEL, COL_GATES)]] + group_pieces,
        x, g_mix, dx1, stream_dils=(1, 1, 1, 4, 16))
    d_w_abcv = _matmul_tn("dw_in_abcv", h1, d_abcv, col_tile=768)
    d_w_gates = _matmul_tn("dw_in_gates", h1, d_gates)
    d_w_groups = [_matmul_tn(f"dw_in_qkv{g}", h1, t, col_tile=768) for g, t in enumerate((dq[0], nat1, nat2))]
    gw = GROUP_WIDTH
    d_w_in = jnp.concatenate(
        [d_w_abcv] + [d_w_groups[g][:, j * gw:(j + 1) * gw] for j in range(3) for g in range(3)] + [d_w_gates], axis=1)
    return (loss, grad_x, d_w_in, d_w_pa, d_w_pb, d_w_out, d_w_up, d_w_down,
            d_g_mix, d_g_ffn, d_g_final, gate_small, conv_small, ffn_small)


N_CHIPS = 4
MESH_ID = pl.DeviceIdType.MESH
_ANY = pl.BlockSpec(memory_space=pl.ANY)
_VMEM = pl.BlockSpec(memory_space=pltpu.VMEM)


def _mesh_position():
    x, y, c = lax.axis_index("x"), lax.axis_index("y"), lax.axis_index("c")
    other_chips = [(1 - x, y), (x, 1 - y), (1 - x, 1 - y)]
    return x, y, c, other_chips


def _half_rows(c, half):
    return pl.ds(pl.multiple_of(c * half, 16), half)


def _weight_allgather(bigs, smalls):
    nb, ns = len(bigs), len(smalls)
    n_sems = 3 * (2 * nb + ns)

    def body(*refs):
        big_refs, small_refs = refs[:nb], refs[nb:nb + ns]
        big_outs, small_outs = refs[nb + ns:2 * nb + ns], refs[2 * nb + ns:2 * (nb + ns)]
        send_sems, recv_sems = refs[2 * (nb + ns):]
        x, y, c, chips = _mesh_position()
        me = 2 * x + y
        sibling = (x, y, 1 - c)

        def copy(k, src, dst, to):
            return pltpu.make_async_remote_copy(src_ref=src, dst_ref=dst, send_sem=send_sems.at[k],
                                                recv_sem=recv_sems.at[k], device_id=to, device_id_type=MESH_ID)

        halves = [r.shape[1] // 2 for r in big_refs]
        first = []
        for j, (px, py) in enumerate(chips):
            for b in range(nb):
                mine = _half_rows(c, halves[b])
                first.append(copy(3 * b + j, big_refs[b].at[0, mine], big_outs[b].at[me, mine], (px, py, c)))
            for s in range(ns):
                first.append(copy(3 * (2 * nb + s) + j, small_refs[s].at[0], small_outs[s].at[me], (px, py, c)))
        for cp in first:
            cp.start()
        passed = []
        for j, (px, py) in enumerate(chips):
            for b in range(nb):
                landed = big_outs[b].at[2 * px + py, _half_rows(c, halves[b])]
                copy(3 * b + j, landed, landed, (px, py, c)).wait_recv()
                fwd = copy(3 * (nb + b) + j, landed, landed, sibling)
                fwd.start()
                passed.append(fwd)
        for j, (px, py) in enumerate(chips):
            for s in range(ns):
                landed = small_outs[s].at[2 * px + py]
                copy(3 * (2 * nb + s) + j, landed, landed, (px, py, c)).wait_recv()
            for b in range(nb):
                from_sibling = big_outs[b].at[2 * px + py, _half_rows(1 - c, halves[b])]
                copy(3 * (nb + b) + j, from_sibling, from_sibling, sibling).wait_recv()
        for cp in first + passed:
            cp.wait_send()

    return pl.pallas_call(
        body, name="weight_allgather",
        out_shape=[jax.ShapeDtypeStruct((N_CHIPS,) + a.shape[1:], a.dtype) for a in list(bigs) + list(smalls)],
        in_specs=[_ANY] * (nb + ns), out_specs=[_ANY] * (nb + ns),
        scratch_shapes=[pltpu.SemaphoreType.DMA((n_sems,)), pltpu.SemaphoreType.DMA((n_sems,))],
    )(*bigs, *smalls)


def _sibling_swap_halves(slabs):
    na = len(slabs)

    def body(*refs):
        src_refs, out_refs = refs[:na], refs[na:2 * na]
        send_sems, recv_sems = refs[2 * na:]
        x, y, c, _ = _mesh_position()
        cps = []
        for a in range(na):
            theirs = _half_rows(1 - c, src_refs[a].shape[1] // 2)
            cps.append(pltpu.make_async_remote_copy(
                src_ref=src_refs[a].at[:, theirs, :], dst_ref=out_refs[a], send_sem=send_sems.at[a],
                recv_sem=recv_sems.at[a], device_id=(x, y, 1 - c), device_id_type=MESH_ID))
        for cp in cps:
            cp.start()
        for cp in cps:
            cp.wait()

    return pl.pallas_call(
        body, name="grad_sibling_swap",
        out_shape=[jax.ShapeDtypeStruct((a.shape[0], a.shape[1] // 2, a.shape[2]), a.dtype) for a in slabs],
        in_specs=[_ANY] * na, out_specs=[_ANY] * na,
        scratch_shapes=[pltpu.SemaphoreType.DMA((na,)), pltpu.SemaphoreType.DMA((na,))])(*slabs)


def _chip_exchange(partials):
    na = len(partials)

    def body(*refs):
        src_refs, out_refs = refs[:na], refs[na:2 * na]
        send_sems, recv_sems = refs[2 * na:]
        x, y, c, chips = _mesh_position()
        sends = []
        for j, (px, py) in enumerate(chips):
            for a in range(na):
                cp = pltpu.make_async_remote_copy(
                    src_ref=src_refs[a].at[2 * px + py], dst_ref=out_refs[a].at[j], send_sem=send_sems.at[3 * a + j],
                    recv_sem=recv_sems.at[3 * a + j], device_id=(px, py, c), device_id_type=MESH_ID)
                cp.start()
                sends.append(cp)
        for cp in sends:
            cp.wait()

    return pl.pallas_call(
        body, name="grad_chip_exchange",
        out_shape=[jax.ShapeDtypeStruct((3,) + a.shape[1:], a.dtype) for a in partials],
        in_specs=[_ANY] * na, out_specs=[_ANY] * na,
        scratch_shapes=[pltpu.SemaphoreType.DMA((3 * na,)), pltpu.SemaphoreType.DMA((3 * na,))])(*partials)


def _sibling_share(halves):
    na = len(halves)

    def body(*refs):
        out_refs = refs[na:2 * na]
        send_sems, recv_sems = refs[2 * na:]
        x, y, c, _ = _mesh_position()
        cps = []
        for a in range(na):
            mine = out_refs[a].at[0, _half_rows(c, out_refs[a].shape[1] // 2)]
            cps.append(pltpu.make_async_remote_copy(src_ref=mine, dst_ref=mine, send_sem=send_sems.at[a],
                                                    recv_sem=recv_sems.at[a], device_id=(x, y, 1 - c),
                                                    device_id_type=MESH_ID))
        for cp in cps:
            cp.start()
        for a, cp in enumerate(cps):
            cp.wait_send()
            theirs = out_refs[a].at[0, _half_rows(1 - c, out_refs[a].shape[1] // 2)]
            pltpu.make_async_remote_copy(src_ref=theirs, dst_ref=theirs, send_sem=send_sems.at[a],
                                         recv_sem=recv_sems.at[a], device_id=(x, y, 1 - c),
                                         device_id_type=MESH_ID).wait_recv()

    return pl.pallas_call(
        body, name="grad_sibling_share", out_shape=[jax.ShapeDtypeStruct(a.shape, a.dtype) for a in halves],
        in_specs=[_ANY] * na, out_specs=[_ANY] * na, input_output_aliases={a: a for a in range(na)},
        scratch_shapes=[pltpu.SemaphoreType.DMA((na,)), pltpu.SemaphoreType.DMA((na,))])(*halves)


def _add_sibling(name, slab, received, core):
    n, rows, cols = slab.shape
    half = rows // 2

    def body(core_ref, a_ref, b_ref, o_ref):
        o_ref[...] = (a_ref[...].astype(F32) + b_ref[...].astype(F32)).astype(BF16)

    grid_spec = pltpu.PrefetchScalarGridSpec(
        num_scalar_prefetch=1, grid=(n,),
        in_specs=[pl.BlockSpec((None, half, cols), lambda s, core_ref: (s, core_ref[0], 0)),
                  pl.BlockSpec((None, half, cols), lambda s, core_ref: (s, 0, 0))],
        out_specs=pl.BlockSpec((None, half, cols), lambda s, core_ref: (s, 0, 0)))
    return pl.pallas_call(body, name=name, grid_spec=grid_spec,
                          out_shape=jax.ShapeDtypeStruct((n, half, cols), BF16),
                          compiler_params=_params(1))(core, slab, received)


def _sum_chips(name, partial, received, chip_core):
    _, half, cols = partial.shape

    def body(cc_ref, own_ref, recv_ref, o_ref):
        acc = own_ref[...].astype(F32)
        for k in range(3):
            acc = acc + recv_ref[k].astype(F32)
        o_ref[...] = acc

    grid_spec = pltpu.PrefetchScalarGridSpec(
        num_scalar_prefetch=1, grid=(1,),
        in_specs=[pl.BlockSpec((None, half, cols), lambda i, cc_ref: (cc_ref[0], 0, 0)),
                  pl.BlockSpec((3, half, cols), lambda i, cc_ref: (0, 0, 0))],
        out_specs=pl.BlockSpec((None, half, cols), lambda i, cc_ref: (0, cc_ref[1], 0)))
    return pl.pallas_call(body, name=name, grid_spec=grid_spec,
                          out_shape=jax.ShapeDtypeStruct((1, 2 * half, cols), F32),
                          compiler_params=_params(1))(chip_core, partial, received)


def _adam_math(w, g, m, v):
    nm = ADAM_B1 * m + (1.0 - ADAM_B1) * g
    nv = ADAM_B2 * v + (1.0 - ADAM_B2) * jnp.square(g)
    m_hat = nm / (1.0 - ADAM_B1 ** ADAM_STEP)
    v_hat = nv / (1.0 - ADAM_B2 ** ADAM_STEP)
    delta = -ADAM_LR * (m_hat / (jnp.sqrt(v_hat) + ADAM_EPS) + ADAM_WD * w)
    return delta, nm, nv


def _adamw(name, w, g, m, v):
    _, rows, cols = w.shape
    tr = next(t for t in (512, 384, 352, 256, 128, 64, 32, 16, 8) if rows % t == 0)

    def body(w_ref, g_ref, m_ref, v_ref, d_ref, nm_ref, nv_ref):
        d_ref[...], nm_ref[...], nv_ref[...] = _adam_math(w_ref[...], g_ref[...], m_ref[...], v_ref[...])

    spec = pl.BlockSpec((None, tr, cols), lambda i: (0, i, 0))
    return pl.pallas_call(
        body, name=name, grid=(rows // tr,), out_shape=[jax.ShapeDtypeStruct(w.shape, F32)] * 3,
        in_specs=[spec] * 4, out_specs=[spec] * 3, compiler_params=_params(1))(w, g, m, v)


SMALL_PARAMS = ("norm_mix_g", "b_gate", "conv_a_w", "conv_a_b", "norm_ffn_g", "ffn_conv_w", "ffn_conv_b", "final_norm_g")


def _small_update(partials, params, moments_m, moments_v):
    na = len(partials)
    npar = len(SMALL_PARAMS)

    def body(*refs):
        in_refs = refs[:na]
        w_refs = refs[na:na + npar]
        m_refs = refs[na + npar:na + 2 * npar]
        v_refs = refs[na + 2 * npar:na + 3 * npar]
        pos = na + 3 * npar
        loss_ref = refs[pos]
        out_refs = refs[pos + 1:pos + 1 + 4 * npar]
        pos += 1 + 4 * npar
        acc_refs = refs[pos:pos + na]
        recv_refs = refs[pos + na:pos + 4 * na]
        send_sems, recv_sems = refs[pos + 4 * na:]
        x, y, c, _ = _mesh_position()
        chip = 2 * x + y
        for a in range(na):
            acc_refs[a][...] = in_refs[a][...]
        for stage, peer in enumerate(((x, y, 1 - c), (x, 1 - y, c), (1 - x, y, c))):
            cps = []
            for a in range(na):
                k = stage * na + a
                cps.append(pltpu.make_async_remote_copy(src_ref=acc_refs[a], dst_ref=recv_refs[k], send_sem=send_sems.at[k],
                                                        recv_sem=recv_sems.at[k], device_id=peer, device_id_type=MESH_ID))
            for cp in cps:
                cp.start()
            for cp in cps:
                cp.wait()
            for a in range(na):
                acc_refs[a][...] = acc_refs[a][...] + recv_refs[stage * na + a][...]

        mix, ffn, fin, gate, conv, ffnc, loss = acc_refs
        loss_ref[...] = loss[...]

        def cols(width):
            return pl.ds(pl.multiple_of(chip * width, LANES), width)

        grads = {
            "norm_mix_g": mix[...], "norm_ffn_g": ffn[...], "final_norm_g": fin[...],
            "b_gate": gate[0:2, cols(D_MODEL // N_CHIPS)],
            "conv_a_w": conv[0:3, cols(CONV_WIDTH // N_CHIPS)], "conv_a_b": conv[3:4, :],
            "ffn_conv_w": ffnc[0:3, cols(2 * D_FF // N_CHIPS)], "ffn_conv_b": ffnc[3:4, :]}
        for i, name in enumerate(SMALL_PARAMS):
            g = grads[name]
            if len(w_refs[i].shape) == 3:
                results = (g,) + _adam_math(w_refs[i][0], g, m_refs[i][0], v_refs[i][0])
                for o_ref, val in zip(out_refs[4 * i:4 * i + 4], results):
                    o_ref[0] = val
            else:
                results = (g,) + _adam_math(w_refs[i][...], g, m_refs[i][...], v_refs[i][...])
                for o_ref, val in zip(out_refs[4 * i:4 * i + 4], results):
                    o_ref[...] = val

    outs = [jax.ShapeDtypeStruct(partials[-1].shape, F32)]
    for w in params:
        outs += [jax.ShapeDtypeStruct(w.shape, F32)] * 4
    scratch = [pltpu.VMEM(p.shape, F32) for p in partials]
    scratch += [pltpu.VMEM(p.shape, F32) for _ in range(3) for p in partials]
    scratch += [pltpu.SemaphoreType.DMA((3 * na,)), pltpu.SemaphoreType.DMA((3 * na,))]
    n_in = na + 3 * npar
    return pl.pallas_call(
        body, name="small_update", out_shape=outs, in_specs=[_VMEM] * n_in, out_specs=[_VMEM] * len(outs),
        scratch_shapes=scratch)(*partials, *params, *moments_m, *moments_v)


def _gathered_columns(g):
    return jnp.transpose(g, (1, 0, 2)).reshape(g.shape[1], N_CHIPS * g.shape[2])


def _column_slabs(full):
    k, n = full.shape
    return jnp.transpose(full.reshape(k, N_CHIPS, n // N_CHIPS), (1, 0, 2))


def kernel(x, norm_mix_g, w_in, b_gate, conv_a_w, conv_a_b, w_proj_a, w_proj_b, w_out, norm_ffn_g, w_up, ffn_conv_w, ffn_conv_b, w_down, final_norm_g, loss_target, m_norm_mix_g, m_w_in, m_b_gate, m_conv_a_w, m_conv_a_b, m_w_proj_a, m_w_proj_b, m_w_out, m_norm_ffn_g, m_w_up, m_ffn_conv_w, m_ffn_conv_b, m_w_down, m_final_norm_g, v_norm_mix_g, v_w_in, v_b_gate, v_conv_a_w, v_conv_a_b, v_w_proj_a, v_w_proj_b, v_w_out, v_norm_ffn_g, v_w_up, v_ffn_conv_w, v_ffn_conv_b, v_w_down, v_final_norm_g):
    chip = (2 * lax.axis_index("x") + lax.axis_index("y")).astype(jnp.int32)
    core = lax.axis_index("c").astype(jnp.int32)
    core_arr = core.reshape(1)
    chip_core = jnp.stack([chip, core])

    big_names = ("w_in", "w_proj_a", "w_proj_b", "w_out", "w_up", "w_down")
    big_w = dict(w_in=w_in, w_proj_a=w_proj_a, w_proj_b=w_proj_b, w_out=w_out, w_up=w_up, w_down=w_down)
    big_m = dict(w_in=m_w_in, w_proj_a=m_w_proj_a, w_proj_b=m_w_proj_b, w_out=m_w_out, w_up=m_w_up, w_down=m_w_down)
    big_v = dict(w_in=v_w_in, w_proj_a=v_w_proj_a, w_proj_b=v_w_proj_b, w_out=v_w_out, w_up=v_w_up, w_down=v_w_down)
    own_big = [big_w[n].astype(BF16) for n in big_names]
    own_small = [b_gate, conv_a_w, ffn_conv_w]
    gathered = _weight_allgather(own_big, own_small)
    gathered = [lax.dynamic_update_slice(g, own, (chip, 0, 0)) for g, own in zip(gathered, own_big + own_small)]
    g_in, g_pa, g_pb, g_out, g_up, g_down, g_bgate, g_convw, g_ffnw = gathered

    (loss, grad_x, d_w_in, d_w_pa, d_w_pb, d_w_out, d_w_up, d_w_down, d_g_mix, d_g_ffn, d_g_final,
     gate_small, conv_small, ffn_small) = _local_step(
        x[0], loss_target[0], norm_mix_g, _gathered_columns(g_in), _gathered_columns(g_bgate),
        _gathered_columns(g_convw), conv_a_b, _gathered_columns(g_pa), _gathered_columns(g_pb),
        g_out.reshape(D_MODEL, D_MODEL), norm_ffn_g, _gathered_columns(g_up), _gathered_columns(g_ffnw), ffn_conv_b,
        g_down.reshape(D_FF, D_MODEL), final_norm_g.reshape(1, D_MODEL))

    slabs = [_column_slabs(d_w_in), d_w_pa, d_w_pb, d_w_out.reshape(N_CHIPS, D_MODEL // N_CHIPS, D_MODEL), d_w_up,
             d_w_down.reshape(N_CHIPS, D_FF // N_CHIPS, D_MODEL)]
    from_sibling = _sibling_swap_halves(slabs)
    chip_partials = [_add_sibling("grad_add_" + n, s, r, core_arr) for n, s, r in zip(big_names, slabs, from_sibling)]
    from_chips = _chip_exchange(chip_partials)
    my_halves = [_sum_chips("grad_sum_" + n, p, r, chip_core) for n, p, r in zip(big_names, chip_partials, from_chips)]
    big_grads = _sibling_share(my_halves)

    fin_w, fin_m, fin_v = (a.reshape(1, D_MODEL) for a in (final_norm_g, m_final_norm_g, v_final_norm_g))
    small_w = [norm_mix_g, b_gate, conv_a_w, conv_a_b, norm_ffn_g, ffn_conv_w, ffn_conv_b, fin_w]
    small_m = [m_norm_mix_g, m_b_gate, m_conv_a_w, m_conv_a_b, m_norm_ffn_g, m_ffn_conv_w, m_ffn_conv_b, fin_m]
    small_v = [v_norm_mix_g, v_b_gate, v_conv_a_w, v_conv_a_b, v_norm_ffn_g, v_ffn_conv_w, v_ffn_conv_b, fin_v]
    small_out = _small_update([d_g_mix, d_g_ffn, d_g_final, gate_small, conv_small, ffn_small, loss],
                              small_w, small_m, small_v)
    total_loss = small_out[0][0, 0]

    grads, delta, new_m, new_v = {}, {}, {}, {}
    for i, n in enumerate(SMALL_PARAMS):
        vals = small_out[1 + 4 * i:5 + 4 * i]
        if n == "final_norm_g":
            vals = [a.reshape(D_MODEL) for a in vals]
        grads[n], delta[n], new_m[n], new_v[n] = vals
    for n, g in zip(big_names, big_grads):
        grads[n] = g
        delta[n], new_m[n], new_v[n] = _adamw("adamw_" + n, big_w[n], g, big_m[n], big_v[n])

    names = ["norm_mix_g", "w_in", "b_gate", "conv_a_w", "conv_a_b", "w_proj_a", "w_proj_b", "w_out", "norm_ffn_g", "w_up",
             "ffn_conv_w", "ffn_conv_b", "w_down", "final_norm_g"]
    out = [total_loss, grad_x[None]]
    for group in (grads, delta, new_m, new_v):
        out += [group[n] for n in names]
    return tuple(out)
```

```python
import jax
import jax.numpy as jnp
from jax import lax
from jax.experimental import pallas as pl
from jax.experimental.pallas import tpu as pltpu

F32 = jnp.float32
BF16 = jnp.bfloat16

D_MODEL = 1024
CONV_WIDTH = 512
ATTN_WIDTH = 768
GROUP_WIDTH = 256
HEAD_DIM = 64
HEADS_PER_GROUP = 4
DILATIONS = (1, 4, 16)
ATTN_BLOCK = 128
D_FF = 2816
D_IN = 5888
EPS = 1e-6
NEG_INF = -1e30
ATTN_SCALE = HEAD_DIM ** -0.5

COL_ABCV = 0
COL_Q = 1536
COL_K = 2304
COL_V = 3072
COL_GATES = 3840

ADAM_LR = 0.001
ADAM_B1 = 0.9
ADAM_B2 = 0.999
ADAM_EPS = 1e-08
ADAM_WD = 0.01
ADAM_STEP = 10

LANES = 128
SUBLANES = 8
ROW_TILE = 512
VMEM_LIMIT = 56 * 1024 * 1024

_NT = (((1,), (1,)), ((), ()))
_TN = (((0,), (0,)), ((), ()))


def _params(n_axes, vmem=VMEM_LIMIT):
    return pltpu.CompilerParams(dimension_semantics=("arbitrary",) * n_axes, vmem_limit_bytes=vmem)


def _resident(shape):
    nd = len(shape)
    return pl.BlockSpec(shape, lambda *_: (0,) * nd, pipeline_mode=pl.Buffered(1))


def _rows(tm, width, col_block=0):
    return pl.BlockSpec((tm, width), lambda i: (i, col_block))


def _col_chunks(n, cmax):
    out, lo = [], 0
    while lo < n:
        size = min(cmax, n - lo)
        out.append((lo, size))
        lo += size
    return out


def _dot(a, b):
    return jnp.dot(a, b, preferred_element_type=F32)


def _dot_nt(a, b):
    return lax.dot_general(a, b, _NT, preferred_element_type=F32)


def _dot_tn(a, b):
    return lax.dot_general(a, b, _TN, preferred_element_type=F32)


def _sigmoid(x):
    return 1.0 / (1.0 + jnp.exp(-x))


def _shift_down(v, k, halo8):
    tm = v.shape[0]
    rolled = pltpu.roll(v, k, 0)
    fix = jnp.tile(pltpu.roll(halo8, k, 0), (tm // SUBLANES, 1))
    row = lax.broadcasted_iota(jnp.int32, v.shape, 0)
    return jnp.where(row < k, fix, rolled)


def _shift_up(v, k, halo8):
    tm = v.shape[0]
    rolled = pltpu.roll(v, tm - k, 0)
    fix = jnp.tile(pltpu.roll(halo8, SUBLANES - k, 0), (tm // SUBLANES, 1))
    row = lax.broadcasted_iota(jnp.int32, v.shape, 0)
    return jnp.where(row >= tm - k, fix, rolled)


def _colsum(v):
    return jnp.sum(v, axis=0, keepdims=True)


def _to_streams(val, scr, out_ref, d, col0):
    tm = val.shape[0]
    panels = val.shape[1] // LANES
    for p in range(panels):
        scr[pl.ds(p * tm, tm), :] = val[:, p * LANES:(p + 1) * LANES]
    for r in range(d):
        for p in range(panels):
            piece = scr[pl.ds(p * tm + r, tm // d, stride=d), :]
            out_ref[r, :, col0 + p * LANES: col0 + (p + 1) * LANES] = piece.astype(out_ref.dtype)


def _from_streams(in_ref, scr, d, col0, width):
    panels = width // LANES
    rows = in_ref.shape[1]
    tm = rows * d
    for r in range(d):
        for p in range(panels):
            scr[pl.ds(p * tm + r, rows, stride=d), :] = in_ref[r, :, col0 + p * LANES: col0 + (p + 1) * LANES].astype(F32)
    return jnp.concatenate([scr[pl.ds(p * tm, tm), :] for p in range(panels)], axis=1)


def _stream_block(tm, d, width):
    return pl.BlockSpec((d, tm // d, width), lambda i: (0, i, 0))


def _rev_stream_block(tm, d, width, nt):
    return pl.BlockSpec((d, tm // d, width), lambda i: (0, nt - 1 - i, 0))


N_CHIPS = 4
MESH_ID = pl.DeviceIdType.MESH
_ANY = pl.BlockSpec(memory_space=pl.ANY)
_VMEM = pl.BlockSpec(memory_space=pltpu.VMEM)


def _mesh_position():
    x, y, c = lax.axis_index("x"), lax.axis_index("y"), lax.axis_index("c")
    other_chips = [(1 - x, y), (x, 1 - y), (1 - x, 1 - y)]
    return x, y, c, other_chips


def _half_rows(c, half):
    return pl.ds(pl.multiple_of(c * half, 16), half)


def _remote_copy(k, src, dst, to, send_sems, recv_sems):
    return pltpu.make_async_remote_copy(src_ref=src, dst_ref=dst, send_sem=send_sems.at[k], recv_sem=recv_sems.at[k],
                                        device_id=to, device_id_type=MESH_ID)


def _gather_first_copies(big_refs, small_refs, big_outs, small_outs, send_sems, recv_sems):
    x, y, c, chips = _mesh_position()
    me = 2 * x + y
    nb = len(big_refs)
    cps = []
    for j, (px, py) in enumerate(chips):
        for b in range(nb):
            mine = _half_rows(c, big_refs[b].shape[1] // 2)
            cps.append(_remote_copy(3 * b + j, big_refs[b].at[0, mine], big_outs[b].at[me, mine], (px, py, c),
                                    send_sems, recv_sems))
        for s in range(len(small_refs)):
            cps.append(_remote_copy(3 * (nb + s) + j, small_refs[s].at[0], small_outs[s].at[me], (px, py, c),
                                    send_sems, recv_sems))
    return cps


def _gather_forward_copies(bufs, send_sems, recv_sems):
    x, y, c, chips = _mesh_position()
    cps = []
    for j, (px, py) in enumerate(chips):
        for b in range(len(bufs)):
            landed = bufs[b].at[2 * px + py, _half_rows(c, bufs[b].shape[1] // 2)]
            cps.append(_remote_copy(3 * b + j, landed, landed, (x, y, 1 - c), send_sems, recv_sems))
    return cps


def _chip_exchange_copies(src_refs, out_refs, send_sems, recv_sems):
    x, y, c, chips = _mesh_position()
    cps = []
    for j, (px, py) in enumerate(chips):
        for a in range(len(src_refs)):
            cps.append(_remote_copy(3 * a + j, src_refs[a].at[2 * px + py], out_refs[a].at[j], (px, py, c),
                                    send_sems, recv_sems))
    return cps


def _dma_sems(n):
    return [pltpu.SemaphoreType.DMA((n,)), pltpu.SemaphoreType.DMA((n,))]


def _inproj_fwd(x, g, w_in, big_shards, small_shards):
    s = x.shape[0]
    tm = ROW_TILE
    nt = s // tm
    nb, ns = len(big_shards), len(small_shards)
    n_fixed_in, n_fixed_out = 3, 6

    def body(*refs):
        x_ref, g_ref, w_ref = refs[:n_fixed_in]
        shard_refs = refs[n_fixed_in:n_fixed_in + nb + ns]
        pos = n_fixed_in + nb + ns
        h_ref, abcv_ref, gates_ref, qkv0_ref, qkv1_ref, qkv2_ref = refs[pos:pos + n_fixed_out]
        gathered_refs = refs[pos + n_fixed_out:pos + n_fixed_out + nb + ns]
        scr, send_sems, recv_sems = refs[pos + n_fixed_out + nb + ns:]
        i = pl.program_id(0)

        def gather_copies():
            return _gather_first_copies(shard_refs[:nb], shard_refs[nb:], gathered_refs[:nb], gathered_refs[nb:],
                                        send_sems, recv_sems)

        @pl.when(i == 0)
        def _():
            for cp in gather_copies():
                cp.start()

        xv = x_ref[...]
        r = lax.rsqrt(jnp.mean(xv * xv, axis=-1, keepdims=True) + EPS)
        h = (xv * r * g_ref[...]).astype(BF16)
        h_ref[...] = h
        for lo, size in _col_chunks(3 * CONV_WIDTH, 512):
            abcv_ref[:, lo:lo + size] = _dot(h, w_ref[:, COL_ABCV + lo: COL_ABCV + lo + size]).astype(BF16)
        for lo, size in _col_chunks(2 * D_MODEL, 512):
            gates_ref[:, lo:lo + size] = _dot(h, w_ref[:, COL_GATES + lo: COL_GATES + lo + size]).astype(BF16)
        for gi, (d, out_ref) in enumerate(zip(DILATIONS, (qkv0_ref, qkv1_ref, qkv2_ref))):
            for j, base in enumerate((COL_Q, COL_K, COL_V)):
                lo = base + gi * GROUP_WIDTH
                y = _dot(h, w_ref[:, lo:lo + GROUP_WIDTH])
                _to_streams(y, scr, out_ref, d, j * GROUP_WIDTH)

        @pl.when(i == nt - 1)
        def _():
            for cp in gather_copies():
                cp.wait()

    outs = [jax.ShapeDtypeStruct((s, D_MODEL), BF16),
            jax.ShapeDtypeStruct((s, 3 * CONV_WIDTH), BF16),
            jax.ShapeDtypeStruct((s, 2 * D_MODEL), BF16)]
    outs += [jax.ShapeDtypeStruct((d, s // d, 3 * GROUP_WIDTH), BF16) for d in DILATIONS]
    outs += [jax.ShapeDtypeStruct((N_CHIPS,) + a.shape[1:], a.dtype) for a in list(big_shards) + list(small_shards)]
    return pl.pallas_call(
        body, name="inproj_fwd", grid=(nt,), out_shape=outs,
        in_specs=[_rows(tm, D_MODEL), _resident((1, D_MODEL)), _resident((D_MODEL, D_IN))] + [_ANY] * (nb + ns),
        out_specs=[_rows(tm, D_MODEL), _rows(tm, 3 * CONV_WIDTH), _rows(tm, 2 * D_MODEL)]
        + [_stream_block(tm, d, 3 * GROUP_WIDTH) for d in DILATIONS] + [_ANY] * (nb + ns),
        scratch_shapes=[pltpu.VMEM((GROUP_WIDTH // LANES * tm, LANES), F32)] + _dma_sems(3 * (nb + ns)),
        compiler_params=_params(1))(x, g, w_in, *big_shards, *small_shards)


def _head_of_lane(shape):
    return lax.broadcasted_iota(jnp.int32, shape, 1) // HEAD_DIM


def _stack_heads(v):
    head = _head_of_lane(v.shape)
    return jnp.concatenate([jnp.where(head == h, v, jnp.zeros_like(v)) for h in range(HEADS_PER_GROUP)], axis=0)


def _unstack_heads(v):
    q = ATTN_BLOCK
    head = _head_of_lane((q, v.shape[1]))
    out = jnp.zeros((q, v.shape[1]), v.dtype)
    for h in range(HEADS_PER_GROUP):
        out = jnp.where(head == h, v[h * q:(h + 1) * q], out)
    return out


def _per_head_rows(col):
    q = ATTN_BLOCK
    head = _head_of_lane((q, GROUP_WIDTH))
    out = jnp.zeros((q, GROUP_WIDTH), col.dtype)
    for h in range(HEADS_PER_GROUP):
        out = jnp.where(head == h, col[h * q:(h + 1) * q], out)
    return out


def _per_head_col(v):
    head = _head_of_lane(v.shape)
    cols = [jnp.max(jnp.where(head == h, v, -jnp.inf), axis=1, keepdims=True) for h in range(HEADS_PER_GROUP)]
    return jnp.concatenate(cols, axis=0)


def _band_mask(first_block):
    rows = HEADS_PER_GROUP * ATTN_BLOCK
    qi = lax.broadcasted_iota(jnp.int32, (rows, 2 * ATTN_BLOCK), 0) % ATTN_BLOCK
    kj = lax.broadcasted_iota(jnp.int32, (rows, 2 * ATTN_BLOCK), 1)
    dist = qi + ATTN_BLOCK - kj
    return (dist >= 0) & (dist <= ATTN_BLOCK) & ((kj >= ATTN_BLOCK) | jnp.logical_not(first_block))


def _attn_fwd(qkv, gi, forward=()):
    d, length, _ = qkv.shape
    nb = length // ATTN_BLOCK
    q = ATTN_BLOCK
    nf = len(forward)

    def body(*refs):
        q_ref, kp_ref, kc_ref, vp_ref, vc_ref = refs[:5]
        o_ref, lse_ref = refs[5 + nf:7 + nf]
        buf_refs = refs[7 + nf:7 + 2 * nf]
        sems = refs[7 + 2 * nf:]
        n = pl.program_id(1)
        first_step = (pl.program_id(0) == 0) & (n == 0)
        last_step = (pl.program_id(0) == d - 1) & (n == nb - 1)

        if nf:
            @pl.when(first_step)
            def _():
                for cp in _gather_forward_copies(buf_refs, *sems):
                    cp.start()

        qs = _stack_heads(q_ref[...])
        k2 = jnp.concatenate([kp_ref[...], kc_ref[...]], axis=0)
        v2 = jnp.concatenate([vp_ref[...], vc_ref[...]], axis=0)
        sc = _dot_nt(qs, k2) * ATTN_SCALE
        sc = jnp.where(_band_mask(n == 0), sc, NEG_INF)
        m = jnp.max(sc, axis=1, keepdims=True)
        p = jnp.exp(sc - m)
        l = jnp.sum(p, axis=1, keepdims=True)
        of = _dot(p.astype(BF16), v2) / l
        o_ref[...] = _unstack_heads(of).astype(BF16)
        lse_ref[...] = _per_head_rows(m + jnp.log(l))

        if nf:
            @pl.when(last_step)
            def _():
                for cp in _gather_forward_copies(buf_refs, *sems):
                    cp.wait()

    def blk(col, prev):
        if prev:
            return pl.BlockSpec((None, q, GROUP_WIDTH), lambda r, n: (r, jnp.maximum(n - 1, 0), col))
        return pl.BlockSpec((None, q, GROUP_WIDTH), lambda r, n: (r, n, col))

    return pl.pallas_call(
        body, name=f"attn_fwd_g{gi}", grid=(d, nb),
        out_shape=[jax.ShapeDtypeStruct((d, length, GROUP_WIDTH), BF16),
                   jax.ShapeDtypeStruct((d, length, GROUP_WIDTH), F32)]
        + [jax.ShapeDtypeStruct(a.shape, a.dtype) for a in forward],
        in_specs=[blk(0, False), blk(1, True), blk(1, False), blk(2, True), blk(2, False)] + [_ANY] * nf,
        out_specs=[blk(0, False), blk(0, False)] + [_ANY] * nf,
        input_output_aliases={5 + a: 2 + a for a in range(nf)},
        scratch_shapes=_dma_sems(3 * nf) if nf else [],
        compiler_params=_params(2))(qkv, qkv, qkv, qkv, qkv, *forward)


def _conv_branch(ab, ac, av, halo_u, w, b):
    u = ac * av
    sh1 = _shift_down(u, 1, halo_u)
    sh2 = _shift_down(u, 2, halo_u)
    cv = w[0:1] * sh2 + w[1:2] * sh1 + w[2:3] * u + b
    return ab * cv, cv, u, sh1, sh2


def _mix_fwd(x, abcv, gates, o_list, lse_list, conv_w, conv_b, b_gate, w_pa, w_pb, w_out):
    s = x.shape[0]
    tm = ROW_TILE

    def body(x_ref, abcv_ref, gates_ref, o0_ref, o1_ref, o2_ref, l0_ref, l1_ref, l2_ref,
             cw_ref, cb_ref, bg_ref, wpa_ref, wpb_ref, wout_ref,
             x1_ref, ya0_ref, yb0_ref, mrg_ref, ya_ref, yb_ref, lsetot_ref, halo_ref, scr):
        i = pl.program_id(0)

        @pl.when(i == 0)
        def _():
            halo_ref[...] = jnp.zeros_like(halo_ref)

        ab = abcv_ref[:, 0:CONV_WIDTH].astype(F32)
        ac = abcv_ref[:, CONV_WIDTH:2 * CONV_WIDTH].astype(F32)
        av = abcv_ref[:, 2 * CONV_WIDTH:3 * CONV_WIDTH].astype(F32)
        ya0, _, u, _, _ = _conv_branch(ab, ac, av, halo_ref[...], cw_ref[...], cb_ref[...])
        halo_ref[...] = u[tm - SUBLANES:tm]
        ya0 = ya0.astype(BF16)
        ya0_ref[...] = ya0
        ya = _dot(ya0, wpa_ref[...])

        o_refs, l_refs = (o0_ref, o1_ref, o2_ref), (l0_ref, l1_ref, l2_ref)
        lses = [_from_streams(l_refs[g], scr, DILATIONS[g], 0, GROUP_WIDTH) for g in range(3)]
        top = jnp.maximum(jnp.maximum(lses[0], lses[1]), lses[2])
        lsetot = top + jnp.log(jnp.exp(lses[0] - top) + jnp.exp(lses[1] - top) + jnp.exp(lses[2] - top))
        lsetot_ref[...] = lsetot
        yb = jnp.zeros((tm, D_MODEL), F32)
        for g in range(3):
            og = _from_streams(o_refs[g], scr, DILATIONS[g], 0, GROUP_WIDTH)
            yb0 = (jnp.exp(lses[g] - lsetot) * og).astype(BF16)
            yb0_ref[:, g * GROUP_WIDTH:(g + 1) * GROUP_WIDTH] = yb0
            yb = yb + _dot(yb0, wpb_ref[g * GROUP_WIDTH:(g + 1) * GROUP_WIDTH, :])

        sa = _sigmoid(gates_ref[:, 0:D_MODEL].astype(F32) + bg_ref[0:1, :])
        sb = _sigmoid(gates_ref[:, D_MODEL:2 * D_MODEL].astype(F32) + bg_ref[1:2, :])
        ya_ref[...] = ya.astype(BF16)
        yb_ref[...] = yb.astype(BF16)
        mrg = (sa * ya + sb * yb).astype(BF16)
        mrg_ref[...] = mrg
        x1_ref[...] = x_ref[...] + _dot(mrg, wout_ref[...])

    outs = [jax.ShapeDtypeStruct((s, D_MODEL), F32),
            jax.ShapeDtypeStruct((s, CONV_WIDTH), BF16),
            jax.ShapeDtypeStruct((s, ATTN_WIDTH), BF16),
            jax.ShapeDtypeStruct((s, D_MODEL), BF16),
            jax.ShapeDtypeStruct((s, D_MODEL), BF16),
            jax.ShapeDtypeStruct((s, D_MODEL), BF16),
            jax.ShapeDtypeStruct((s, GROUP_WIDTH), F32)]
    return pl.pallas_call(
        body, name="mix_fwd", grid=(s // tm,), out_shape=outs,
        in_specs=[_rows(tm, D_MODEL), _rows(tm, 3 * CONV_WIDTH), _rows(tm, 2 * D_MODEL)]
        + [_stream_block(tm, d, GROUP_WIDTH) for d in DILATIONS] * 2
        + [_resident((3, CONV_WIDTH)), _resident((1, CONV_WIDTH)), _resident((2, D_MODEL)),
           _resident((CONV_WIDTH, D_MODEL)), _resident((ATTN_WIDTH, D_MODEL)), _resident((D_MODEL, D_MODEL))],
        out_specs=[_rows(tm, D_MODEL), _rows(tm, CONV_WIDTH), _rows(tm, ATTN_WIDTH), _rows(tm, D_MODEL),
                   _rows(tm, D_MODEL), _rows(tm, D_MODEL), _rows(tm, GROUP_WIDTH)],
        scratch_shapes=[pltpu.VMEM((SUBLANES, CONV_WIDTH), F32),
                        pltpu.VMEM((GROUP_WIDTH // LANES * tm, LANES), F32)],
        compiler_params=_params(1))(x, abcv, gates, *o_list, *lse_list, conv_w, conv_b, b_gate, w_pa, w_pb, w_out)


def _ffn_up_fwd(x1, g, w_up):
    s = x1.shape[0]
    n = w_up.shape[1]
    tm = ROW_TILE

    def body(x_ref, g_ref, w_ref, h_ref, y_ref):
        xv = x_ref[...]
        r = lax.rsqrt(jnp.mean(xv * xv, axis=-1, keepdims=True) + EPS)
        h = (xv * r * g_ref[...]).astype(BF16)
        h_ref[...] = h
        for lo, size in _col_chunks(n, 512):
            y_ref[:, lo:lo + size] = _dot(h, w_ref[:, lo:lo + size]).astype(BF16)

    return pl.pallas_call(
        body, name="ffn_up_fwd", grid=(s // tm,),
        out_shape=[jax.ShapeDtypeStruct((s, D_MODEL), BF16), jax.ShapeDtypeStruct((s, n), BF16)],
        in_specs=[_rows(tm, D_MODEL), _resident((1, D_MODEL)), _resident((D_MODEL, n))],
        out_specs=[_rows(tm, D_MODEL), _rows(tm, n)],
        compiler_params=_params(1))(x1, g, w_up)


FFN_CHUNK = 256


def _ffn_conv(up0_ref, halo16, lo, w_ref, b_ref):
    cur = up0_ref[:, lo:lo + FFN_CHUNK].astype(F32)
    halo8 = halo16[SUBLANES:2 * SUBLANES]
    sh1 = _shift_down(cur, 1, halo8)
    sh2 = _shift_down(cur, 2, halo8)
    w = w_ref[:, lo:lo + FFN_CHUNK]
    up = w[0:1] * sh2 + w[1:2] * sh1 + w[2:3] * cur + b_ref[:, lo:lo + FFN_CHUNK]
    return up, cur, sh1, sh2


def _ffn_act_fwd(x1, up0, target, conv_w, conv_b, w_down, g_final):
    s = x1.shape[0]
    tm = ROW_TILE

    def body(x1_ref, up0_ref, tgt_ref, cw_ref, cb_ref, wd_ref, gf_ref,
             act_ref, dx2_ref, dx2b_ref, dgf_ref, loss_ref, halo_ref):
        i = pl.program_id(0)

        @pl.when(i == 0)
        def _():
            halo_ref[...] = jnp.zeros_like(halo_ref)
            dgf_ref[...] = jnp.zeros_like(dgf_ref)
            loss_ref[...] = jnp.zeros_like(loss_ref)

        acc = jnp.zeros((tm, D_MODEL), F32)
        for lo, _ in _col_chunks(D_FF, FFN_CHUNK):
            gate, _, _, _ = _ffn_conv(up0_ref, halo_ref[:, lo:lo + FFN_CHUNK].astype(F32), lo, cw_ref, cb_ref)
            val, _, _, _ = _ffn_conv(up0_ref, halo_ref[:, D_FF + lo:D_FF + lo + FFN_CHUNK].astype(F32),
                                     D_FF + lo, cw_ref, cb_ref)
            act = (gate * _sigmoid(gate) * val).astype(BF16)
            act_ref[:, lo:lo + FFN_CHUNK] = act
            acc = acc + _dot(act, wd_ref[lo:lo + FFN_CHUNK, :])
        halo_ref[...] = up0_ref[tm - 2 * SUBLANES:tm, :]

        x2 = x1_ref[...] + acc
        r = lax.rsqrt(jnp.mean(x2 * x2, axis=-1, keepdims=True) + EPS)
        xn = x2 * r
        gf = gf_ref[...]
        err = xn * gf - tgt_ref[...]
        loss_ref[...] += (0.5 / D_MODEL) * jnp.sum(err * err)
        dy = err * (1.0 / D_MODEL)
        dgf_ref[...] += _colsum(dy * xn)
        dxn = dy * gf
        dx2 = r * (dxn - xn * jnp.mean(dxn * xn, axis=-1, keepdims=True))
        dx2_ref[...] = dx2
        dx2b_ref[...] = dx2.astype(BF16)

    return pl.pallas_call(
        body, name="ffn_act_fwd", grid=(s // tm,),
        out_shape=[jax.ShapeDtypeStruct((s, D_FF), BF16), jax.ShapeDtypeStruct((s, D_MODEL), F32),
                   jax.ShapeDtypeStruct((s, D_MODEL), BF16),
                   jax.ShapeDtypeStruct((1, D_MODEL), F32), jax.ShapeDtypeStruct((1, LANES), F32)],
        in_specs=[_rows(tm, D_MODEL), _rows(tm, 2 * D_FF), _rows(tm, D_MODEL),
                  _resident((3, 2 * D_FF)), _resident((1, 2 * D_FF)), _resident((D_FF, D_MODEL)),
                  _resident((1, D_MODEL))],
        out_specs=[_rows(tm, D_FF), _rows(tm, D_MODEL), _rows(tm, D_MODEL),
                   pl.BlockSpec((1, D_MODEL), lambda i: (0, 0)), pl.BlockSpec((1, LANES), lambda i: (0, 0))],
        scratch_shapes=[pltpu.VMEM((2 * SUBLANES, 2 * D_FF), BF16)],
        compiler_params=_params(1))(x1, up0, target, conv_w, conv_b, w_down, g_final)


def _ffn_act_bwd(dx2, up0, conv_w, conv_b, w_down):
    s = dx2.shape[0]
    tm = ROW_TILE
    nt = s // tm
    hb = tm // (2 * SUBLANES)

    def body(dx2_ref, up0_ref, pre_ref, cw_ref, cb_ref, wd_ref, dup0_ref, small_ref, next_ref):
        i = pl.program_id(0)

        @pl.when(i == 0)
        def _():
            next_ref[...] = jnp.zeros_like(next_ref)
            small_ref[...] = jnp.zeros_like(small_ref)

        not_first = (i < nt - 1).astype(F32)
        dx2b = dx2_ref[...].astype(BF16)
        for lo, _ in _col_chunks(D_FF, FFN_CHUNK):
            glo, vlo = lo, D_FF + lo
            gate, g0, g1, g2 = _ffn_conv(up0_ref, pre_ref[:, glo:glo + FFN_CHUNK].astype(F32) * not_first,
                                         glo, cw_ref, cb_ref)
            val, v0, v1, v2 = _ffn_conv(up0_ref, pre_ref[:, vlo:vlo + FFN_CHUNK].astype(F32) * not_first,
                                        vlo, cw_ref, cb_ref)
            dact = _dot_nt(dx2b, wd_ref[lo:lo + FFN_CHUNK, :])
            sg = _sigmoid(gate)
            dval = dact * (gate * sg)
            dgate = dact * val * (sg * (1.0 + gate * (1.0 - sg)))
            for clo, dz, z0, z1, z2 in ((glo, dgate, g0, g1, g2), (vlo, dval, v0, v1, v2)):
                small_ref[0:1, clo:clo + FFN_CHUNK] += _colsum(dz * z2)
                small_ref[1:2, clo:clo + FFN_CHUNK] += _colsum(dz * z1)
                small_ref[2:3, clo:clo + FFN_CHUNK] += _colsum(dz * z0)
                small_ref[3:4, clo:clo + FFN_CHUNK] += _colsum(dz)
                nxt = next_ref[:, clo:clo + FFN_CHUNK]
                w = cw_ref[:, clo:clo + FFN_CHUNK]
                dz0 = w[2:3] * dz + w[1:2] * _shift_up(dz, 1, nxt) + w[0:1] * _shift_up(dz, 2, nxt)
                dup0_ref[:, clo:clo + FFN_CHUNK] = dz0.astype(BF16)
                next_ref[:, clo:clo + FFN_CHUNK] = dz[0:SUBLANES]

    rev = lambda i: (nt - 1 - i, 0)
    pre = lambda i: (jnp.maximum((nt - 1 - i) * hb - 1, 0), 0)
    return pl.pallas_call(
        body, name="ffn_act_bwd", grid=(nt,),
        out_shape=[jax.ShapeDtypeStruct((s, 2 * D_FF), BF16), jax.ShapeDtypeStruct((SUBLANES, 2 * D_FF), F32)],
        in_specs=[pl.BlockSpec((tm, D_MODEL), rev), pl.BlockSpec((tm, 2 * D_FF), rev),
                  pl.BlockSpec((2 * SUBLANES, 2 * D_FF), pre),
                  _resident((3, 2 * D_FF)), _resident((1, 2 * D_FF)), _resident((D_FF, D_MODEL))],
        out_specs=[pl.BlockSpec((tm, 2 * D_FF), rev), pl.BlockSpec((SUBLANES, 2 * D_FF), lambda i: (0, 0))],
        scratch_shapes=[pltpu.VMEM((SUBLANES, 2 * D_FF), F32)],
        compiler_params=_params(1))(dx2, up0, up0, conv_w, conv_b, w_down)


def _norm_bwd(name, dys, w, pieces, xin, g, dres, stream_dils=None, exchange=()):
    s = xin.shape[0]
    tm = ROW_TILE
    nk = len(dys)
    dils = stream_dils or (1,) * nk
    widths = [dy.shape[-1] for dy in dys]
    n_total = w.shape[1]
    relayout = [k for k in range(nk) if dils[k] > 1]
    nx = len(exchange)
    nt = s // tm

    def body(*refs):
        dy_refs = refs[:nk]
        w_ref, x_ref, g_ref, dres_ref = refs[nk:nk + 4]
        part_refs = refs[nk + 4:nk + 4 + nx]
        pos = nk + 4 + nx
        dx_ref, dxb_ref, dg_ref = refs[pos:pos + 3]
        nat_refs = refs[pos + 3:pos + 3 + len(relayout)]
        pos += 3 + len(relayout)
        recv_refs = refs[pos:pos + nx]
        scr = refs[pos + nx]
        sems = refs[pos + nx + 1:]
        i = pl.program_id(0)

        @pl.when(i == 0)
        def _():
            dg_ref[...] = jnp.zeros_like(dg_ref)
            for cp in _chip_exchange_copies(part_refs, recv_refs, *sems) if nx else ():
                cp.start()

        dh = jnp.zeros((tm, D_MODEL), F32)
        for k in range(nk):
            if dils[k] > 1:
                nat_ref = nat_refs[relayout.index(k)]
                for lo, size in _col_chunks(widths[k], GROUP_WIDTH):
                    nat_ref[:, lo:lo + size] = _from_streams(dy_refs[k], scr, dils[k], lo, size).astype(BF16)
                src = nat_ref
            else:
                src = dy_refs[k]
            for first, width, wcol in pieces[k]:
                for lo, size in _col_chunks(width, 512):
                    dh = dh + _dot_nt(src[:, first + lo:first + lo + size], w_ref[:, wcol + lo:wcol + lo + size])
        xv = x_ref[...]
        r = lax.rsqrt(jnp.mean(xv * xv, axis=-1, keepdims=True) + EPS)
        xn = xv * r
        dg_ref[...] += _colsum(dh * xn)
        dxn = dh * g_ref[...]
        dx = dres_ref[...] + r * (dxn - xn * jnp.mean(dxn * xn, axis=-1, keepdims=True))
        dx_ref[...] = dx
        dxb_ref[...] = dx.astype(BF16)

        if nx:
            @pl.when(i == nt - 1)
            def _():
                for cp in _chip_exchange_copies(part_refs, recv_refs, *sems):
                    cp.wait()

    dy_specs = [(_stream_block(tm, dils[k], widths[k]) if dils[k] > 1 else _rows(tm, widths[k])) for k in range(nk)]
    outs = [jax.ShapeDtypeStruct((s, D_MODEL), F32), jax.ShapeDtypeStruct((s, D_MODEL), BF16),
            jax.ShapeDtypeStruct((1, D_MODEL), F32)]
    outs += [jax.ShapeDtypeStruct((s, widths[k]), BF16) for k in relayout]
    outs += [jax.ShapeDtypeStruct((3,) + a.shape[1:], a.dtype) for a in exchange]
    return pl.pallas_call(
        body, name=name, grid=(nt,), out_shape=outs,
        in_specs=dy_specs + [_resident((D_MODEL, n_total)), _rows(tm, D_MODEL), _resident((1, D_MODEL)),
                             _rows(tm, D_MODEL)] + [_ANY] * nx,
        out_specs=[_rows(tm, D_MODEL), _rows(tm, D_MODEL), pl.BlockSpec((1, D_MODEL), lambda i: (0, 0))]
        + [_rows(tm, widths[k]) for k in relayout] + [_ANY] * nx,
        scratch_shapes=[pltpu.VMEM((GROUP_WIDTH // LANES * tm, LANES), F32)] + (_dma_sems(3 * nx) if nx else []),
        compiler_params=_params(1))(*dys, w, xin, g, dres, *exchange)


def _mix_bwd(dx1, abcv, gates, ya, yb, yb0, lsetot, conv_w, conv_b, b_gate, w_pa, w_pb, w_out, exchange=()):
    s = dx1.shape[0]
    tm = ROW_TILE
    nt = s // tm
    hb = tm // (2 * SUBLANES)
    nx = len(exchange)

    def body(*refs):
        (dx1_ref, abcv_ref, pre_ref, gates_ref, ya_ref, yb_ref, yb0_ref, lsetot_ref,
         cw_ref, cb_ref, bg_ref, wpa_ref, wpb_ref, wout_ref) = refs[:14]
        part_refs = refs[14:14 + nx]
        (dya_ref, dyb_ref, dgates_ref, dabcv_ref, dyb0_ref, dyl0_ref, dyl1_ref, dyl2_ref, aux0_ref, aux1_ref,
         aux2_ref, sm_gate_ref, sm_conv_ref) = refs[14 + nx:27 + nx]
        recv_refs = refs[27 + nx:27 + 2 * nx]
        next_ref, scr = refs[27 + 2 * nx:29 + 2 * nx]
        sems = refs[29 + 2 * nx:]
        i = pl.program_id(0)

        @pl.when(i == 0)
        def _():
            next_ref[...] = jnp.zeros_like(next_ref)
            sm_gate_ref[...] = jnp.zeros_like(sm_gate_ref)
            sm_conv_ref[...] = jnp.zeros_like(sm_conv_ref)
            for cp in _chip_exchange_copies(part_refs, recv_refs, *sems) if nx else ():
                cp.start()

        not_first = (i < nt - 1).astype(F32)
        dm = _dot_nt(dx1_ref[...].astype(BF16), wout_ref[...])
        sa = _sigmoid(gates_ref[:, 0:D_MODEL].astype(F32) + bg_ref[0:1, :])
        sb = _sigmoid(gates_ref[:, D_MODEL:2 * D_MODEL].astype(F32) + bg_ref[1:2, :])
        dya = (dm * sa).astype(BF16)
        dyb = (dm * sb).astype(BF16)
        dya_ref[...] = dya
        dyb_ref[...] = dyb
        dga = dm * ya_ref[...].astype(F32) * (sa * (1.0 - sa))
        dgb = dm * yb_ref[...].astype(F32) * (sb * (1.0 - sb))
        dgates_ref[:, 0:D_MODEL] = dga.astype(BF16)
        dgates_ref[:, D_MODEL:2 * D_MODEL] = dgb.astype(BF16)
        sm_gate_ref[0:1, :] += _colsum(dga)
        sm_gate_ref[1:2, :] += _colsum(dgb)

        dya0 = _dot_nt(dya, wpa_ref[...])
        ab = abcv_ref[:, 0:CONV_WIDTH].astype(F32)
        ac = abcv_ref[:, CONV_WIDTH:2 * CONV_WIDTH].astype(F32)
        av = abcv_ref[:, 2 * CONV_WIDTH:3 * CONV_WIDTH].astype(F32)
        pre = pre_ref[...].astype(F32) * not_first
        halo_u = (pre[:, CONV_WIDTH:2 * CONV_WIDTH] * pre[:, 2 * CONV_WIDTH:3 * CONV_WIDTH])[SUBLANES:2 * SUBLANES]
        w = cw_ref[...]
        _, cv, u, sh1, sh2 = _conv_branch(ab, ac, av, halo_u, w, cb_ref[...])
        dcv = dya0 * ab
        sm_conv_ref[0:1, :] += _colsum(dcv * sh2)
        sm_conv_ref[1:2, :] += _colsum(dcv * sh1)
        sm_conv_ref[2:3, :] += _colsum(dcv * u)
        sm_conv_ref[3:4, :] += _colsum(dcv)
        nxt = next_ref[...]
        du = w[2:3] * dcv + w[1:2] * _shift_up(dcv, 1, nxt) + w[0:1] * _shift_up(dcv, 2, nxt)
        next_ref[...] = dcv[0:SUBLANES]
        dabcv_ref[:, 0:CONV_WIDTH] = (dya0 * cv).astype(BF16)
        dabcv_ref[:, CONV_WIDTH:2 * CONV_WIDTH] = (du * av).astype(BF16)
        dabcv_ref[:, 2 * CONV_WIDTH:3 * CONV_WIDTH] = (du * ac).astype(BF16)

        head_r = lax.broadcasted_iota(jnp.int32, (GROUP_WIDTH, GROUP_WIDTH), 0) // HEAD_DIM
        head_c = lax.broadcasted_iota(jnp.int32, (GROUP_WIDTH, GROUP_WIDTH), 1) // HEAD_DIM
        same_head = (head_r == head_c).astype(BF16)
        prod = jnp.zeros((tm, GROUP_WIDTH), F32)
        dyb0s = []
        for g in range(3):
            cols = slice(g * GROUP_WIDTH, (g + 1) * GROUP_WIDTH)
            dyb0 = _dot_nt(dyb, wpb_ref[cols, :])
            dyb0_ref[:, cols] = dyb0.astype(BF16)
            dyb0s.append(dyb0)
            prod = prod + dyb0 * yb0_ref[:, cols].astype(F32)
        hi = prod.astype(BF16)
        mid = (prod - hi.astype(F32)).astype(BF16)
        lo = (prod - hi.astype(F32) - mid.astype(F32)).astype(BF16)
        delta = _dot(hi, same_head) + _dot(mid, same_head) + _dot(lo, same_head)
        lsetot = lsetot_ref[...]
        for g, (dy_ref, aux_ref) in enumerate(zip((dyl0_ref, dyl1_ref, dyl2_ref), (aux0_ref, aux1_ref, aux2_ref))):
            d = DILATIONS[g]
            _to_streams(dyb0s[g], scr, dy_ref, d, 0)
            _to_streams(lsetot, scr, aux_ref, d, 0)
            _to_streams(delta, scr, aux_ref, d, GROUP_WIDTH)

        if nx:
            @pl.when(i == nt - 1)
            def _():
                for cp in _chip_exchange_copies(part_refs, recv_refs, *sems):
                    cp.wait()

    rev = lambda i: (nt - 1 - i, 0)
    pre = lambda i: (jnp.maximum((nt - 1 - i) * hb - 1, 0), 0)
    rows = lambda width: pl.BlockSpec((tm, width), rev)
    outs = [jax.ShapeDtypeStruct((s, D_MODEL), BF16), jax.ShapeDtypeStruct((s, D_MODEL), BF16),
            jax.ShapeDtypeStruct((s, 2 * D_MODEL), BF16), jax.ShapeDtypeStruct((s, 3 * CONV_WIDTH), BF16),
            jax.ShapeDtypeStruct((s, ATTN_WIDTH), BF16)]
    outs += [jax.ShapeDtypeStruct((d, s // d, GROUP_WIDTH), BF16) for d in DILATIONS]
    outs += [jax.ShapeDtypeStruct((d, s // d, 2 * GROUP_WIDTH), F32) for d in DILATIONS]
    outs += [jax.ShapeDtypeStruct((SUBLANES, D_MODEL), F32), jax.ShapeDtypeStruct((SUBLANES, CONV_WIDTH), F32)]
    outs += [jax.ShapeDtypeStruct((3,) + a.shape[1:], a.dtype) for a in exchange]
    return pl.pallas_call(
        body, name="mix_bwd", grid=(nt,), out_shape=outs,
        in_specs=[rows(D_MODEL), rows(3 * CONV_WIDTH), pl.BlockSpec((2 * SUBLANES, 3 * CONV_WIDTH), pre),
                  rows(2 * D_MODEL), rows(D_MODEL), rows(D_MODEL), rows(ATTN_WIDTH), rows(GROUP_WIDTH),
                  _resident((3, CONV_WIDTH)), _resident((1, CONV_WIDTH)), _resident((2, D_MODEL)),
                  _resident((CONV_WIDTH, D_MODEL)), _resident((ATTN_WIDTH, D_MODEL)), _resident((D_MODEL, D_MODEL))]
        + [_ANY] * nx,
        out_specs=[rows(D_MODEL), rows(D_MODEL), rows(2 * D_MODEL), rows(3 * CONV_WIDTH), rows(ATTN_WIDTH)]
        + [_rev_stream_block(tm, d, GROUP_WIDTH, nt) for d in DILATIONS]
        + [_rev_stream_block(tm, d, 2 * GROUP_WIDTH, nt) for d in DILATIONS]
        + [pl.BlockSpec((SUBLANES, D_MODEL), lambda i: (0, 0)), pl.BlockSpec((SUBLANES, CONV_WIDTH), lambda i: (0, 0))]
        + [_ANY] * nx,
        scratch_shapes=[pltpu.VMEM((SUBLANES, CONV_WIDTH), F32),
                        pltpu.VMEM((GROUP_WIDTH // LANES * tm, LANES), F32)] + (_dma_sems(3 * nx) if nx else []),
        compiler_params=_params(1))(dx1, abcv, abcv, gates, ya, yb, yb0, lsetot,
                                    conv_w, conv_b, b_gate, w_pa, w_pb, w_out, *exchange)


def _attn_bwd(qkv, dy, aux, gi):
    d, length, _ = qkv.shape
    nb = length // ATTN_BLOCK
    q = ATTN_BLOCK
    gw = GROUP_WIDTH

    def body(q_ref, kp_ref, kc_ref, vp_ref, vc_ref, dy_ref, aux_ref, out_ref, dq_ref, dkv_ref):
        n = pl.program_id(1)

        @pl.when(n > 0)
        def _():
            out_ref[:, 0:gw] = dq_ref[...].astype(BF16)

        @pl.when(n == nb)
        def _():
            out_ref[:, gw:2 * gw] = dkv_ref[0].astype(BF16)
            out_ref[:, 2 * gw:3 * gw] = dkv_ref[1].astype(BF16)

        @pl.when(n < nb)
        def _():
            qs = _stack_heads(q_ref[...])
            dys = _stack_heads(dy_ref[...])
            k2 = jnp.concatenate([kp_ref[...], kc_ref[...]], axis=0)
            v2 = jnp.concatenate([vp_ref[...], vc_ref[...]], axis=0)
            lse = _per_head_col(aux_ref[:, 0:gw])
            delta = _per_head_col(aux_ref[:, gw:2 * gw])
            sc = _dot_nt(qs, k2) * ATTN_SCALE
            p = jnp.where(_band_mask(n == 0), jnp.exp(sc - lse), 0.0)
            dp = _dot_nt(dys, v2)
            ds = (p * (dp - delta) * ATTN_SCALE).astype(BF16)
            dq_ref[...] = _unstack_heads(_dot(ds, k2))
            dk2 = _dot_tn(ds, qs)
            dv2 = _dot_tn(p.astype(BF16), dys)

            @pl.when(n > 0)
            def _():
                out_ref[:, gw:2 * gw] = (dkv_ref[0] + dk2[0:q]).astype(BF16)
                out_ref[:, 2 * gw:3 * gw] = (dkv_ref[1] + dv2[0:q]).astype(BF16)

            dkv_ref[0] = dk2[q:2 * q]
            dkv_ref[1] = dv2[q:2 * q]

    last = nb - 1

    def blk(col, shift, width=gw):
        if shift:
            return pl.BlockSpec((None, q, width), lambda r, n: (r, jnp.maximum(n - 1, 0), col))
        return pl.BlockSpec((None, q, width), lambda r, n: (r, jnp.minimum(n, last), col))

    return pl.pallas_call(
        body, name=f"attn_bwd_g{gi}", grid=(d, nb + 1),
        out_shape=jax.ShapeDtypeStruct((d, length, 3 * gw), BF16),
        in_specs=[blk(0, False), blk(1, True), blk(1, False), blk(2, True), blk(2, False),
                  blk(0, False), blk(0, False, 2 * gw)],
        out_specs=pl.BlockSpec((None, q, 3 * gw), lambda r, n: (r, jnp.maximum(n - 1, 0), 0)),
        scratch_shapes=[pltpu.VMEM((q, gw), F32), pltpu.VMEM((2, q, gw), F32)],
        compiler_params=_params(2))(qkv, qkv, qkv, qkv, qkv, dy, aux)


def _matmul_tn(name, a, b, col_tile=1024, row_tile=1024, slot_major=False):
    s, k = a.shape
    n = b.shape[1]
    tk = min(row_tile, s)
    tn = col_tile
    steps = s // tk

    def body(a_ref, b_ref, o_ref, acc_ref):
        t = pl.program_id(1)

        @pl.when(t == 0)
        def _():
            acc_ref[...] = jnp.zeros_like(acc_ref)

        acc_ref[...] += _dot_tn(a_ref[...], b_ref[...])

        @pl.when(t == steps - 1)
        def _():
            o_ref[...] = acc_ref[...].astype(BF16)

    if slot_major:
        out_shape = jax.ShapeDtypeStruct((n // tn, k, tn), BF16)
        out_spec = pl.BlockSpec((None, k, tn), lambda j, t: (j, 0, 0))
    else:
        out_shape = jax.ShapeDtypeStruct((k, n), BF16)
        out_spec = pl.BlockSpec((k, tn), lambda j, t: (0, j))
    return pl.pallas_call(
        body, name=name, grid=(n // tn, steps), out_shape=out_shape,
        in_specs=[pl.BlockSpec((tk, k), lambda j, t: (t, 0)), pl.BlockSpec((tk, tn), lambda j, t: (t, j))],
        out_specs=out_spec, scratch_shapes=[pltpu.VMEM((k, tn), F32)],
        compiler_params=_params(2))(a, b)


def _weight_allgather(bigs, smalls):
    nb, ns = len(bigs), len(smalls)
    n_sems = 3 * (2 * nb + ns)

    def body(*refs):
        big_refs, small_refs = refs[:nb], refs[nb:nb + ns]
        big_outs, small_outs = refs[nb + ns:2 * nb + ns], refs[2 * nb + ns:2 * (nb + ns)]
        send_sems, recv_sems = refs[2 * (nb + ns):]
        x, y, c, chips = _mesh_position()
        me = 2 * x + y
        sibling = (x, y, 1 - c)

        def copy(k, src, dst, to):
            return pltpu.make_async_remote_copy(src_ref=src, dst_ref=dst, send_sem=send_sems.at[k],
                                                recv_sem=recv_sems.at[k], device_id=to, device_id_type=MESH_ID)

        halves = [r.shape[1] // 2 for r in big_refs]
        first = []
        for j, (px, py) in enumerate(chips):
            for b in range(nb):
                mine = _half_rows(c, halves[b])
                first.append(copy(3 * b + j, big_refs[b].at[0, mine], big_outs[b].at[me, mine], (px, py, c)))
            for s in range(ns):
                first.append(copy(3 * (2 * nb + s) + j, small_refs[s].at[0], small_outs[s].at[me], (px, py, c)))
        for cp in first:
            cp.start()
        passed = []
        for j, (px, py) in enumerate(chips):
            for b in range(nb):
                landed = big_outs[b].at[2 * px + py, _half_rows(c, halves[b])]
                copy(3 * b + j, landed, landed, (px, py, c)).wait_recv()
                fwd = copy(3 * (nb + b) + j, landed, landed, sibling)
                fwd.start()
                passed.append(fwd)
        for j, (px, py) in enumerate(chips):
            for s in range(ns):
                landed = small_outs[s].at[2 * px + py]
                copy(3 * (2 * nb + s) + j, landed, landed, (px, py, c)).wait_recv()
            for b in range(nb):
                from_sibling = big_outs[b].at[2 * px + py, _half_rows(1 - c, halves[b])]
                copy(3 * (nb + b) + j, from_sibling, from_sibling, sibling).wait_recv()
        for cp in first + passed:
            cp.wait_send()

    return pl.pallas_call(
        body, name="weight_allgather",
        out_shape=[jax.ShapeDtypeStruct((N_CHIPS,) + a.shape[1:], a.dtype) for a in list(bigs) + list(smalls)],
        in_specs=[_ANY] * (nb + ns), out_specs=[_ANY] * (nb + ns),
        scratch_shapes=[pltpu.SemaphoreType.DMA((n_sems,)), pltpu.SemaphoreType.DMA((n_sems,))],
    )(*bigs, *smalls)


def _sibling_swap_halves(name, slabs):
    na = len(slabs)

    def body(*refs):
        src_refs, out_refs = refs[:na], refs[na:2 * na]
        send_sems, recv_sems = refs[2 * na:]
        x, y, c, _ = _mesh_position()
        cps = []
        for a in range(na):
            theirs = _half_rows(1 - c, src_refs[a].shape[1] // 2)
            cps.append(pltpu.make_async_remote_copy(
                src_ref=src_refs[a].at[:, theirs, :], dst_ref=out_refs[a], send_sem=send_sems.at[a],
                recv_sem=recv_sems.at[a], device_id=(x, y, 1 - c), device_id_type=MESH_ID))
        for cp in cps:
            cp.start()
        for cp in cps:
            cp.wait()

    return pl.pallas_call(
        body, name=name,
        out_shape=[jax.ShapeDtypeStruct((a.shape[0], a.shape[1] // 2, a.shape[2]), a.dtype) for a in slabs],
        in_specs=[_ANY] * na, out_specs=[_ANY] * na,
        scratch_shapes=[pltpu.SemaphoreType.DMA((na,)), pltpu.SemaphoreType.DMA((na,))])(*slabs)


def _chip_exchange(partials):
    na = len(partials)

    def body(*refs):
        cps = _chip_exchange_copies(refs[:na], refs[na:2 * na], *refs[2 * na:])
        for cp in cps:
            cp.start()
        for cp in cps:
            cp.wait()

    return pl.pallas_call(
        body, name="grad_chip_exchange",
        out_shape=[jax.ShapeDtypeStruct((3,) + a.shape[1:], a.dtype) for a in partials],
        in_specs=[_ANY] * na, out_specs=[_ANY] * na, scratch_shapes=_dma_sems(3 * na))(*partials)


def _sibling_share(halves):
    na = len(halves)

    def body(*refs):
        out_refs = refs[na:2 * na]
        send_sems, recv_sems = refs[2 * na:]
        x, y, c, _ = _mesh_position()
        cps = []
        for a in range(na):
            mine = out_refs[a].at[0, _half_rows(c, out_refs[a].shape[1] // 2)]
            cps.append(pltpu.make_async_remote_copy(src_ref=mine, dst_ref=mine, send_sem=send_sems.at[a],
                                                    recv_sem=recv_sems.at[a], device_id=(x, y, 1 - c),
                                                    device_id_type=MESH_ID))
        for cp in cps:
            cp.start()
        for a, cp in enumerate(cps):
            cp.wait_send()
            theirs = out_refs[a].at[0, _half_rows(1 - c, out_refs[a].shape[1] // 2)]
            pltpu.make_async_remote_copy(src_ref=theirs, dst_ref=theirs, send_sem=send_sems.at[a],
                                         recv_sem=recv_sems.at[a], device_id=(x, y, 1 - c),
                                         device_id_type=MESH_ID).wait_recv()

    return pl.pallas_call(
        body, name="grad_sibling_share", out_shape=[jax.ShapeDtypeStruct(a.shape, a.dtype) for a in halves],
        in_specs=[_ANY] * na, out_specs=[_ANY] * na, input_output_aliases={a: a for a in range(na)},
        scratch_shapes=[pltpu.SemaphoreType.DMA((na,)), pltpu.SemaphoreType.DMA((na,))])(*halves)


def _add_sibling(name, slab, received, core):
    n, rows, cols = slab.shape
    half = rows // 2

    def body(core_ref, a_ref, b_ref, o_ref):
        o_ref[...] = (a_ref[...].astype(F32) + b_ref[...].astype(F32)).astype(BF16)

    grid_spec = pltpu.PrefetchScalarGridSpec(
        num_scalar_prefetch=1, grid=(n,),
        in_specs=[pl.BlockSpec((None, half, cols), lambda s, core_ref: (s, core_ref[0], 0)),
                  pl.BlockSpec((None, half, cols), lambda s, core_ref: (s, 0, 0))],
        out_specs=pl.BlockSpec((None, half, cols), lambda s, core_ref: (s, 0, 0)))
    return pl.pallas_call(body, name=name, grid_spec=grid_spec,
                          out_shape=jax.ShapeDtypeStruct((n, half, cols), BF16),
                          compiler_params=_params(1))(core, slab, received)


def _sum_chips(name, partial, received, chip_core):
    _, half, cols = partial.shape

    def body(cc_ref, own_ref, recv_ref, o_ref):
        acc = own_ref[...].astype(F32)
        for k in range(3):
            acc = acc + recv_ref[k].astype(F32)
        o_ref[...] = acc

    grid_spec = pltpu.PrefetchScalarGridSpec(
        num_scalar_prefetch=1, grid=(1,),
        in_specs=[pl.BlockSpec((None, half, cols), lambda i, cc_ref: (cc_ref[0], 0, 0)),
                  pl.BlockSpec((3, half, cols), lambda i, cc_ref: (0, 0, 0))],
        out_specs=pl.BlockSpec((None, half, cols), lambda i, cc_ref: (0, cc_ref[1], 0)))
    return pl.pallas_call(body, name=name, grid_spec=grid_spec,
                          out_shape=jax.ShapeDtypeStruct((1, 2 * half, cols), F32),
                          compiler_params=_params(1))(chip_core, partial, received)


def _adam_math(w, g, m, v):
    nm = ADAM_B1 * m + (1.0 - ADAM_B1) * g
    nv = ADAM_B2 * v + (1.0 - ADAM_B2) * jnp.square(g)
    m_hat = nm / (1.0 - ADAM_B1 ** ADAM_STEP)
    v_hat = nv / (1.0 - ADAM_B2 ** ADAM_STEP)
    delta = -ADAM_LR * (m_hat / (jnp.sqrt(v_hat) + ADAM_EPS) + ADAM_WD * w)
    return delta, nm, nv


def _adamw(name, w, g, m, v):
    _, rows, cols = w.shape
    tr = next(t for t in (512, 384, 352, 256, 128, 64, 32, 16, 8) if rows % t == 0)

    def body(w_ref, g_ref, m_ref, v_ref, d_ref, nm_ref, nv_ref):
        d_ref[...], nm_ref[...], nv_ref[...] = _adam_math(w_ref[...], g_ref[...], m_ref[...], v_ref[...])

    spec = pl.BlockSpec((None, tr, cols), lambda i: (0, i, 0))
    return pl.pallas_call(
        body, name=name, grid=(rows // tr,), out_shape=[jax.ShapeDtypeStruct(w.shape, F32)] * 3,
        in_specs=[spec] * 4, out_specs=[spec] * 3, compiler_params=_params(1))(w, g, m, v)


SMALL_PARAMS = ("norm_mix_g", "b_gate", "conv_a_w", "conv_a_b", "norm_ffn_g", "ffn_conv_w", "ffn_conv_b", "final_norm_g")


def _small_update(partials, params, moments_m, moments_v):
    na = len(partials)
    npar = len(SMALL_PARAMS)

    def body(*refs):
        in_refs = refs[:na]
        w_refs = refs[na:na + npar]
        m_refs = refs[na + npar:na + 2 * npar]
        v_refs = refs[na + 2 * npar:na + 3 * npar]
        pos = na + 3 * npar
        loss_ref = refs[pos]
        out_refs = refs[pos + 1:pos + 1 + 4 * npar]
        pos += 1 + 4 * npar
        acc_refs = refs[pos:pos + na]
        recv_refs = refs[pos + na:pos + 4 * na]
        send_sems, recv_sems = refs[pos + 4 * na:]
        x, y, c, _ = _mesh_position()
        chip = 2 * x + y
        for a in range(na):
            acc_refs[a][...] = in_refs[a][...]
        for stage, peer in enumerate(((x, y, 1 - c), (x, 1 - y, c), (1 - x, y, c))):
            cps = []
            for a in range(na):
                k = stage * na + a
                cps.append(pltpu.make_async_remote_copy(src_ref=acc_refs[a], dst_ref=recv_refs[k], send_sem=send_sems.at[k],
                                                        recv_sem=recv_sems.at[k], device_id=peer, device_id_type=MESH_ID))
            for cp in cps:
                cp.start()
            for cp in cps:
                cp.wait()
            for a in range(na):
                acc_refs[a][...] = acc_refs[a][...] + recv_refs[stage * na + a][...]

        mix, ffn, fin, gate, conv, ffnc, loss = acc_refs
        loss_ref[...] = loss[...]

        def cols(width):
            return pl.ds(pl.multiple_of(chip * width, LANES), width)

        grads = {
            "norm_mix_g": mix[...], "norm_ffn_g": ffn[...], "final_norm_g": fin[...],
            "b_gate": gate[0:2, cols(D_MODEL // N_CHIPS)],
            "conv_a_w": conv[0:3, cols(CONV_WIDTH // N_CHIPS)], "conv_a_b": conv[3:4, :],
            "ffn_conv_w": ffnc[0:3, cols(2 * D_FF // N_CHIPS)], "ffn_conv_b": ffnc[3:4, :]}
        for i, name in enumerate(SMALL_PARAMS):
            g = grads[name]
            if len(w_refs[i].shape) == 3:
                results = (g,) + _adam_math(w_refs[i][0], g, m_refs[i][0], v_refs[i][0])
                for o_ref, val in zip(out_refs[4 * i:4 * i + 4], results):
                    o_ref[0] = val
            else:
                results = (g,) + _adam_math(w_refs[i][...], g, m_refs[i][...], v_refs[i][...])
                for o_ref, val in zip(out_refs[4 * i:4 * i + 4], results):
                    o_ref[...] = val

    outs = [jax.ShapeDtypeStruct(partials[-1].shape, F32)]
    for w in params:
        outs += [jax.ShapeDtypeStruct(w.shape, F32)] * 4
    scratch = [pltpu.VMEM(p.shape, F32) for p in partials]
    scratch += [pltpu.VMEM(p.shape, F32) for _ in range(3) for p in partials]
    scratch += [pltpu.SemaphoreType.DMA((3 * na,)), pltpu.SemaphoreType.DMA((3 * na,))]
    n_in = na + 3 * npar
    return pl.pallas_call(
        body, name="small_update", out_shape=outs, in_specs=[_VMEM] * n_in, out_specs=[_VMEM] * len(outs),
        scratch_shapes=scratch)(*partials, *params, *moments_m, *moments_v)


def _gathered_columns(g):
    return jnp.transpose(g, (1, 0, 2)).reshape(g.shape[1], N_CHIPS * g.shape[2])


def _column_slabs(full):
    k, n = full.shape
    return jnp.transpose(full.reshape(k, N_CHIPS, n // N_CHIPS), (1, 0, 2))


def kernel(x, norm_mix_g, w_in, b_gate, conv_a_w, conv_a_b, w_proj_a, w_proj_b, w_out, norm_ffn_g, w_up, ffn_conv_w, ffn_conv_b, w_down, final_norm_g, loss_target, m_norm_mix_g, m_w_in, m_b_gate, m_conv_a_w, m_conv_a_b, m_w_proj_a, m_w_proj_b, m_w_out, m_norm_ffn_g, m_w_up, m_ffn_conv_w, m_ffn_conv_b, m_w_down, m_final_norm_g, v_norm_mix_g, v_w_in, v_b_gate, v_conv_a_w, v_conv_a_b, v_w_proj_a, v_w_proj_b, v_w_out, v_norm_ffn_g, v_w_up, v_ffn_conv_w, v_ffn_conv_b, v_w_down, v_final_norm_g):
    chip = (2 * lax.axis_index("x") + lax.axis_index("y")).astype(jnp.int32)
    core = lax.axis_index("c").astype(jnp.int32)
    core_arr = core.reshape(1)
    chip_core = jnp.stack([chip, core])
    xs, target = x[0], loss_target[0]
    g_final = final_norm_g.reshape(1, D_MODEL)

    def own_slot(gathered, own):
        return lax.dynamic_update_slice(gathered, own, (chip, 0, 0))

    def reduce_to_shards(names, slabs, exchange_in):
        from_sibling = _sibling_swap_halves("grad_swap_" + names[0], slabs)
        partials = [_add_sibling("grad_add_" + n, s, r, core_arr) for n, s, r in zip(names, slabs, from_sibling)]
        received, rest = exchange_in(partials)
        halves = [_sum_chips("grad_sum_" + n, p, r, chip_core) for n, p, r in zip(names, partials, received)]
        return halves, rest

    w_in_b = w_in.astype(BF16)
    (g_in,) = _weight_allgather([w_in_b], [])
    w_in_full = _gathered_columns(own_slot(g_in, w_in_b))
    later_w = [w_proj_a, w_proj_b, w_out, w_up, w_down]
    later_b = [w.astype(BF16) for w in later_w]
    small_sharded = [b_gate, conv_a_w, ffn_conv_w]
    fwd = _inproj_fwd(xs, norm_mix_g, w_in_full, later_b, small_sharded)
    h1, abcv, gates, qkv0, qkv1, qkv2 = fwd[:6]
    gathered_big, gathered_small = fwd[6:11], fwd[11:14]
    qkvs = (qkv0, qkv1, qkv2)
    attn0 = _attn_fwd(qkv0, 0, forward=gathered_big)
    attn = [attn0[:2], _attn_fwd(qkv1, 1), _attn_fwd(qkv2, 2)]
    g_pa, g_pb, g_out, g_up, g_down = [own_slot(g, own) for g, own in zip(attn0[2:], later_b)]
    g_bgate, g_convw, g_ffnw = [own_slot(g, own) for g, own in zip(gathered_small, small_sharded)]
    w_pa_full, w_pb_full, w_up_full = _gathered_columns(g_pa), _gathered_columns(g_pb), _gathered_columns(g_up)
    w_out_full, w_down_full = g_out.reshape(D_MODEL, D_MODEL), g_down.reshape(D_FF, D_MODEL)
    b_gate_full, conv_w_full, ffn_w_full = (_gathered_columns(g) for g in (g_bgate, g_convw, g_ffnw))

    x1, ya0, yb0, mrg, ya, yb, lsetot = _mix_fwd(
        xs, abcv, gates, [a[0] for a in attn], [a[1] for a in attn], conv_w_full, conv_a_b, b_gate_full,
        w_pa_full, w_pb_full, w_out_full)
    h2, up0 = _ffn_up_fwd(x1, norm_ffn_g, w_up_full)
    act, dx2, dx2b, d_g_final, loss = _ffn_act_fwd(x1, up0, target, ffn_w_full, ffn_conv_b, w_down_full, g_final)

    d_up0, ffn_small = _ffn_act_bwd(dx2, up0, ffn_w_full, ffn_conv_b, w_down_full)
    d_w_down = _matmul_tn("dw_down", act, dx2b, col_tile=512)
    dx1, dx1b, d_g_ffn = _norm_bwd("ffn_up_bwd", [d_up0], w_up_full, [[(0, 2 * D_FF, 0)]], x1, norm_ffn_g, dx2)
    d_w_up = _matmul_tn("dw_up", h2, d_up0, col_tile=2 * D_FF // N_CHIPS, slot_major=True)

    def behind_mix_bwd(partials):
        res = _mix_bwd(dx1, abcv, gates, ya, yb, yb0, lsetot, conv_w_full, conv_a_b, b_gate_full,
                       w_pa_full, w_pb_full, w_out_full, exchange=partials)
        return res[13:], res[:13]

    halves_ffn, mix_res = reduce_to_shards(
        ("w_up", "w_down"), [d_w_up, d_w_down.reshape(N_CHIPS, D_FF // N_CHIPS, D_MODEL)], behind_mix_bwd)
    (d_ya, d_yb, d_gates, d_abcv, d_yb0, dyl0, dyl1, dyl2, aux0, aux1, aux2, gate_small, conv_small) = mix_res
    d_w_out = _matmul_tn("dw_out", mrg, dx1b)
    d_w_pa = _matmul_tn("dw_proj_a", ya0, d_ya, col_tile=D_MODEL // N_CHIPS, slot_major=True)
    d_w_pb = _matmul_tn("dw_proj_b", yb0, d_yb, col_tile=D_MODEL // N_CHIPS, slot_major=True)
    d_qkvs = [_attn_bwd(qkvs[g], dy, aux, g) for g, (dy, aux) in enumerate(((dyl0, aux0), (dyl1, aux1), (dyl2, aux2)))]

    dq = [d_qkvs[0][0]] + d_qkvs[1:]
    group_pieces = [[(j * GROUP_WIDTH, GROUP_WIDTH, base + g * GROUP_WIDTH) for j, base in enumerate((COL_Q, COL_K, COL_V))]
                    for g in range(3)]

    def behind_inproj_bwd(partials):
        res = _norm_bwd("inproj_bwd", [d_abcv, d_gates] + dq, w_in_full,
                        [[(0, 3 * CONV_WIDTH, COL_ABCV)], [(0, 2 * D_MODEL, COL_GATES)]] + group_pieces,
                        xs, norm_mix_g, dx1, stream_dils=(1, 1, 1, 4, 16), exchange=partials)
        return res[5:], res[:5]

    halves_mix, (grad_x, _, d_g_mix, nat1, nat2) = reduce_to_shards(
        ("w_proj_a", "w_proj_b", "w_out"),
        [d_w_pa, d_w_pb, d_w_out.reshape(N_CHIPS, D_MODEL // N_CHIPS, D_MODEL)], behind_inproj_bwd)

    d_w_abcv = _matmul_tn("dw_in_abcv", h1, d_abcv, col_tile=768)
    d_w_gates = _matmul_tn("dw_in_gates", h1, d_gates)
    d_w_groups = [_matmul_tn(f"dw_in_qkv{g}", h1, t, col_tile=768) for g, t in enumerate((dq[0], nat1, nat2))]
    gw = GROUP_WIDTH
    d_w_in = jnp.concatenate(
        [d_w_abcv] + [d_w_groups[g][:, j * gw:(j + 1) * gw] for j in range(3) for g in range(3)] + [d_w_gates], axis=1)
    halves_in, _ = reduce_to_shards(("w_in",), [_column_slabs(d_w_in)], lambda partials: (_chip_exchange(partials), None))

    big_names = ("w_in", "w_proj_a", "w_proj_b", "w_out", "w_up", "w_down")
    big_grads = _sibling_share(halves_in + halves_mix + halves_ffn)
    big_w = dict(w_in=w_in, w_proj_a=w_proj_a, w_proj_b=w_proj_b, w_out=w_out, w_up=w_up, w_down=w_down)
    big_m = dict(w_in=m_w_in, w_proj_a=m_w_proj_a, w_proj_b=m_w_proj_b, w_out=m_w_out, w_up=m_w_up, w_down=m_w_down)
    big_v = dict(w_in=v_w_in, w_proj_a=v_w_proj_a, w_proj_b=v_w_proj_b, w_out=v_w_out, w_up=v_w_up, w_down=v_w_down)

    fin_w, fin_m, fin_v = (a.reshape(1, D_MODEL) for a in (final_norm_g, m_final_norm_g, v_final_norm_g))
    small_w = [norm_mix_g, b_gate, conv_a_w, conv_a_b, norm_ffn_g, ffn_conv_w, ffn_conv_b, fin_w]
    small_m = [m_norm_mix_g, m_b_gate, m_conv_a_w, m_conv_a_b, m_norm_ffn_g, m_ffn_conv_w, m_ffn_conv_b, fin_m]
    small_v = [v_norm_mix_g, v_b_gate, v_conv_a_w, v_conv_a_b, v_norm_ffn_g, v_ffn_conv_w, v_ffn_conv_b, fin_v]
    small_out = _small_update([d_g_mix, d_g_ffn, d_g_final, gate_small, conv_small, ffn_small, loss],
                              small_w, small_m, small_v)
    total_loss = small_out[0][0, 0]

    grads, delta, new_m, new_v = {}, {}, {}, {}
    for i, n in enumerate(SMALL_PARAMS):
        vals = small_out[1 + 4 * i:5 + 4 * i]
        if n == "final_norm_g":
            vals = [a.reshape(D_MODEL) for a in vals]
        grads[n], delta[n], new_m[n], new_v[n] = vals
    for n, g in zip(big_names, big_grads):
        grads[n] = g
        delta[n], new_m[n], new_v[n] = _adamw("adamw_" + n, big_w[n], g, big_m[n], big_v[n])

    names = ["norm_mix_g", "w_in", "b_gate", "conv_a_w", "conv_a_b", "w_proj_a", "w_proj_b", "w_out", "norm_ffn_g", "w_up",
             "ffn_conv_w", "ffn_conv_b", "w_down", "final_norm_g"]
    out = [total_loss, grad_x[None]]
    for group in (grads, delta, new_m, new_v):
        out += [group[n] for n in names]
    return tuple(out)
```

```python
import jax
import jax.numpy as jnp
from jax import lax
from jax.experimental import pallas as pl
from jax.experimental.pallas import tpu as pltpu

F32 = jnp.float32
BF16 = jnp.bfloat16

D_MODEL = 1024
CONV_WIDTH = 512
ATTN_WIDTH = 768
GROUP_WIDTH = 256
HEAD_DIM = 64
HEADS_PER_GROUP = 4
DILATIONS = (1, 4, 16)
ATTN_BLOCK = 128
D_FF = 2816
D_IN = 5888
EPS = 1e-6
NEG_INF = -1e30
ATTN_SCALE = HEAD_DIM ** -0.5

COL_ABCV = 0
COL_Q = 1536
COL_K = 2304
COL_V = 3072
COL_GATES = 3840

ADAM_LR = 0.001
ADAM_B1 = 0.9
ADAM_B2 = 0.999
ADAM_EPS = 1e-08
ADAM_WD = 0.01
ADAM_STEP = 10

LANES = 128
SUBLANES = 8
ROW_TILE = 512
VMEM_LIMIT = 56 * 1024 * 1024

_NT = (((1,), (1,)), ((), ()))
_TN = (((0,), (0,)), ((), ()))


def _params(n_axes, vmem=VMEM_LIMIT):
    return pltpu.CompilerParams(dimension_semantics=("arbitrary",) * n_axes, vmem_limit_bytes=vmem)


def _resident(shape):
    nd = len(shape)
    return pl.BlockSpec(shape, lambda *_: (0,) * nd, pipeline_mode=pl.Buffered(1))


def _rows(tm, width, col_block=0):
    return pl.BlockSpec((tm, width), lambda i: (i, col_block))


def _col_chunks(n, cmax):
    out, lo = [], 0
    while lo < n:
        size = min(cmax, n - lo)
        out.append((lo, size))
        lo += size
    return out


def _dot(a, b):
    return jnp.dot(a, b, preferred_element_type=F32)


def _dot_nt(a, b):
    return lax.dot_general(a, b, _NT, preferred_element_type=F32)


def _dot_tn(a, b):
    return lax.dot_general(a, b, _TN, preferred_element_type=F32)


def _sigmoid(x):
    return 1.0 / (1.0 + jnp.exp(-x))


def _shift_down(v, k, halo8):
    tm = v.shape[0]
    rolled = pltpu.roll(v, k, 0)
    fix = jnp.tile(pltpu.roll(halo8, k, 0), (tm // SUBLANES, 1))
    row = lax.broadcasted_iota(jnp.int32, v.shape, 0)
    return jnp.where(row < k, fix, rolled)


def _shift_up(v, k, halo8):
    tm = v.shape[0]
    rolled = pltpu.roll(v, tm - k, 0)
    fix = jnp.tile(pltpu.roll(halo8, SUBLANES - k, 0), (tm // SUBLANES, 1))
    row = lax.broadcasted_iota(jnp.int32, v.shape, 0)
    return jnp.where(row >= tm - k, fix, rolled)


def _colsum(v):
    return jnp.sum(v, axis=0, keepdims=True)


def _to_streams(val, scr, out_ref, d, col0):
    tm = val.shape[0]
    panels = val.shape[1] // LANES
    for p in range(panels):
        scr[pl.ds(p * tm, tm), :] = val[:, p * LANES:(p + 1) * LANES]
    for r in range(d):
        for p in range(panels):
            piece = scr[pl.ds(p * tm + r, tm // d, stride=d), :]
            out_ref[r, :, col0 + p * LANES: col0 + (p + 1) * LANES] = piece.astype(out_ref.dtype)


def _from_streams(in_ref, scr, d, col0, width):
    panels = width // LANES
    rows = in_ref.shape[1]
    tm = rows * d
    for r in range(d):
        for p in range(panels):
            scr[pl.ds(p * tm + r, rows, stride=d), :] = in_ref[r, :, col0 + p * LANES: col0 + (p + 1) * LANES].astype(F32)
    return jnp.concatenate([scr[pl.ds(p * tm, tm), :] for p in range(panels)], axis=1)


def _stream_block(tm, d, width):
    return pl.BlockSpec((d, tm // d, width), lambda i: (0, i, 0))


def _rev_stream_block(tm, d, width, nt):
    return pl.BlockSpec((d, tm // d, width), lambda i: (0, nt - 1 - i, 0))


N_CHIPS = 4
MESH_ID = pl.DeviceIdType.MESH
_ANY = pl.BlockSpec(memory_space=pl.ANY)
_VMEM = pl.BlockSpec(memory_space=pltpu.VMEM)


def _mesh_position():
    x, y, c = lax.axis_index("x"), lax.axis_index("y"), lax.axis_index("c")
    other_chips = [(1 - x, y), (x, 1 - y), (1 - x, 1 - y)]
    return x, y, c, other_chips


def _half_rows(c, half):
    return pl.ds(pl.multiple_of(c * half, 16), half)


def _remote_copy(k, src, dst, to, send_sems, recv_sems):
    return pltpu.make_async_remote_copy(src_ref=src, dst_ref=dst, send_sem=send_sems.at[k], recv_sem=recv_sems.at[k],
                                        device_id=to, device_id_type=MESH_ID)


def _gather_first_copies(big_refs, small_refs, big_outs, small_outs, send_sems, recv_sems):
    x, y, c, chips = _mesh_position()
    me = 2 * x + y
    nb = len(big_refs)
    cps = []
    for j, (px, py) in enumerate(chips):
        for b in range(nb):
            mine = _half_rows(c, big_refs[b].shape[1] // 2)
            cps.append(_remote_copy(3 * b + j, big_refs[b].at[0, mine], big_outs[b].at[me, mine], (px, py, c),
                                    send_sems, recv_sems))
        for s in range(len(small_refs)):
            cps.append(_remote_copy(3 * (nb + s) + j, small_refs[s].at[0], small_outs[s].at[me], (px, py, c),
                                    send_sems, recv_sems))
    return cps


def _gather_forward_copies(bufs, send_sems, recv_sems):
    x, y, c, chips = _mesh_position()
    cps = []
    for j, (px, py) in enumerate(chips):
        for b in range(len(bufs)):
            landed = bufs[b].at[2 * px + py, _half_rows(c, bufs[b].shape[1] // 2)]
            cps.append(_remote_copy(3 * b + j, landed, landed, (x, y, 1 - c), send_sems, recv_sems))
    return cps


def _chip_exchange_copies(src_refs, out_refs, send_sems, recv_sems):
    x, y, c, chips = _mesh_position()
    cps = []
    for j, (px, py) in enumerate(chips):
        for a in range(len(src_refs)):
            cps.append(_remote_copy(3 * a + j, src_refs[a].at[2 * px + py], out_refs[a].at[j], (px, py, c),
                                    send_sems, recv_sems))
    return cps


def _dma_sems(n):
    return [pltpu.SemaphoreType.DMA((n,)), pltpu.SemaphoreType.DMA((n,))]


def _inproj_fwd(x, g, w_in_t, big_shards, small_shards):
    s = x.shape[0]
    tm = ROW_TILE
    nt = s // tm
    nb, ns = len(big_shards), len(small_shards)
    n_fixed_in, n_fixed_out = 3, 6

    def body(*refs):
        x_ref, g_ref, w_ref = refs[:n_fixed_in]
        shard_refs = refs[n_fixed_in:n_fixed_in + nb + ns]
        pos = n_fixed_in + nb + ns
        h_ref, abcv_ref, gates_ref, qkv0_ref, qkv1_ref, qkv2_ref = refs[pos:pos + n_fixed_out]
        gathered_refs = refs[pos + n_fixed_out:pos + n_fixed_out + nb + ns]
        scr, send_sems, recv_sems = refs[pos + n_fixed_out + nb + ns:]
        i = pl.program_id(0)

        def gather_copies():
            return _gather_first_copies(shard_refs[:nb], shard_refs[nb:], gathered_refs[:nb], gathered_refs[nb:],
                                        send_sems, recv_sems)

        @pl.when(i == 0)
        def _():
            for cp in gather_copies():
                cp.start()

        xv = x_ref[...]
        r = lax.rsqrt(jnp.mean(xv * xv, axis=-1, keepdims=True) + EPS)
        h = (xv * r * g_ref[...]).astype(BF16)
        h_ref[...] = h
        for lo, size in _col_chunks(3 * CONV_WIDTH, 512):
            abcv_ref[:, lo:lo + size] = _dot_nt(h, w_ref[COL_ABCV + lo: COL_ABCV + lo + size, :]).astype(BF16)
        for lo, size in _col_chunks(2 * D_MODEL, 512):
            gates_ref[:, lo:lo + size] = _dot_nt(h, w_ref[COL_GATES + lo: COL_GATES + lo + size, :]).astype(BF16)
        for gi, (d, out_ref) in enumerate(zip(DILATIONS, (qkv0_ref, qkv1_ref, qkv2_ref))):
            for j, base in enumerate((COL_Q, COL_K, COL_V)):
                lo = base + gi * GROUP_WIDTH
                y = _dot_nt(h, w_ref[lo:lo + GROUP_WIDTH, :])
                _to_streams(y, scr, out_ref, d, j * GROUP_WIDTH)

        @pl.when(i == nt - 1)
        def _():
            for cp in gather_copies():
                cp.wait()

    outs = [jax.ShapeDtypeStruct((s, D_MODEL), BF16),
            jax.ShapeDtypeStruct((s, 3 * CONV_WIDTH), BF16),
            jax.ShapeDtypeStruct((s, 2 * D_MODEL), BF16)]
    outs += [jax.ShapeDtypeStruct((d, s // d, 3 * GROUP_WIDTH), BF16) for d in DILATIONS]
    outs += [jax.ShapeDtypeStruct((N_CHIPS,) + a.shape[1:], a.dtype) for a in list(big_shards) + list(small_shards)]
    return pl.pallas_call(
        body, name="inproj_fwd", grid=(nt,), out_shape=outs,
        in_specs=[_rows(tm, D_MODEL), _resident((1, D_MODEL)), _resident((D_IN, D_MODEL))] + [_ANY] * (nb + ns),
        out_specs=[_rows(tm, D_MODEL), _rows(tm, 3 * CONV_WIDTH), _rows(tm, 2 * D_MODEL)]
        + [_stream_block(tm, d, 3 * GROUP_WIDTH) for d in DILATIONS] + [_ANY] * (nb + ns),
        scratch_shapes=[pltpu.VMEM((GROUP_WIDTH // LANES * tm, LANES), F32)] + _dma_sems(3 * (nb + ns)),
        compiler_params=_params(1))(x, g, w_in_t, *big_shards, *small_shards)


def _head_of_lane(shape):
    return lax.broadcasted_iota(jnp.int32, shape, 1) // HEAD_DIM


def _stack_heads(v):
    head = _head_of_lane(v.shape)
    return jnp.concatenate([jnp.where(head == h, v, jnp.zeros_like(v)) for h in range(HEADS_PER_GROUP)], axis=0)


def _unstack_heads(v):
    q = ATTN_BLOCK
    head = _head_of_lane((q, v.shape[1]))
    out = jnp.zeros((q, v.shape[1]), v.dtype)
    for h in range(HEADS_PER_GROUP):
        out = jnp.where(head == h, v[h * q:(h + 1) * q], out)
    return out


def _per_head_rows(col):
    q = ATTN_BLOCK
    head = _head_of_lane((q, GROUP_WIDTH))
    out = jnp.zeros((q, GROUP_WIDTH), col.dtype)
    for h in range(HEADS_PER_GROUP):
        out = jnp.where(head == h, col[h * q:(h + 1) * q], out)
    return out


def _per_head_col(v):
    head = _head_of_lane(v.shape)
    cols = [jnp.max(jnp.where(head == h, v, -jnp.inf), axis=1, keepdims=True) for h in range(HEADS_PER_GROUP)]
    return jnp.concatenate(cols, axis=0)


def _band_mask(first_block):
    rows = HEADS_PER_GROUP * ATTN_BLOCK
    qi = lax.broadcasted_iota(jnp.int32, (rows, 2 * ATTN_BLOCK), 0) % ATTN_BLOCK
    kj = lax.broadcasted_iota(jnp.int32, (rows, 2 * ATTN_BLOCK), 1)
    dist = qi + ATTN_BLOCK - kj
    return (dist >= 0) & (dist <= ATTN_BLOCK) & ((kj >= ATTN_BLOCK) | jnp.logical_not(first_block))


def _attn_fwd(qkv, gi, forward=()):
    d, length, _ = qkv.shape
    nb = length // ATTN_BLOCK
    q = ATTN_BLOCK
    nf = len(forward)

    def body(*refs):
        q_ref, kp_ref, kc_ref, vp_ref, vc_ref = refs[:5]
        o_ref, lse_ref = refs[5 + nf:7 + nf]
        buf_refs = refs[7 + nf:7 + 2 * nf]
        sems = refs[7 + 2 * nf:]
        n = pl.program_id(1)
        first_step = (pl.program_id(0) == 0) & (n == 0)
        last_step = (pl.program_id(0) == d - 1) & (n == nb - 1)

        if nf:
            @pl.when(first_step)
            def _():
                for cp in _gather_forward_copies(buf_refs, *sems):
                    cp.start()

        qs = _stack_heads(q_ref[...])
        k2 = jnp.concatenate([kp_ref[...], kc_ref[...]], axis=0)
        v2 = jnp.concatenate([vp_ref[...], vc_ref[...]], axis=0)
        sc = _dot_nt(qs, k2) * ATTN_SCALE
        sc = jnp.where(_band_mask(n == 0), sc, NEG_INF)
        m = jnp.max(sc, axis=1, keepdims=True)
        p = jnp.exp(sc - m)
        l = jnp.sum(p, axis=1, keepdims=True)
        of = _dot(p.astype(BF16), v2) / l
        o_ref[...] = _unstack_heads(of).astype(BF16)
        lse_ref[...] = _per_head_rows(m + jnp.log(l))

        if nf:
            @pl.when(last_step)
            def _():
                for cp in _gather_forward_copies(buf_refs, *sems):
                    cp.wait()

    def blk(col, prev):
        if prev:
            return pl.BlockSpec((None, q, GROUP_WIDTH), lambda r, n: (r, jnp.maximum(n - 1, 0), col))
        return pl.BlockSpec((None, q, GROUP_WIDTH), lambda r, n: (r, n, col))

    return pl.pallas_call(
        body, name=f"attn_fwd_g{gi}", grid=(d, nb),
        out_shape=[jax.ShapeDtypeStruct((d, length, GROUP_WIDTH), BF16),
                   jax.ShapeDtypeStruct((d, length, GROUP_WIDTH), F32)]
        + [jax.ShapeDtypeStruct(a.shape, a.dtype) for a in forward],
        in_specs=[blk(0, False), blk(1, True), blk(1, False), blk(2, True), blk(2, False)] + [_ANY] * nf,
        out_specs=[blk(0, False), blk(0, False)] + [_ANY] * nf,
        input_output_aliases={5 + a: 2 + a for a in range(nf)},
        scratch_shapes=_dma_sems(3 * nf) if nf else [],
        compiler_params=_params(2))(qkv, qkv, qkv, qkv, qkv, *forward)


def _conv_branch(ab, ac, av, halo_u, w, b):
    u = ac * av
    sh1 = _shift_down(u, 1, halo_u)
    sh2 = _shift_down(u, 2, halo_u)
    cv = w[0:1] * sh2 + w[1:2] * sh1 + w[2:3] * u + b
    return ab * cv, cv, u, sh1, sh2


def _mix_fwd(x, abcv, gates, o_list, lse_list, conv_w, conv_b, b_gate, w_pa, w_pb, w_out):
    s = x.shape[0]
    tm = ROW_TILE

    def body(x_ref, abcv_ref, gates_ref, o0_ref, o1_ref, o2_ref, l0_ref, l1_ref, l2_ref,
             cw_ref, cb_ref, bg_ref, wpa_ref, wpb_ref, wout_ref,
             x1_ref, ya0_ref, yb0_ref, mrg_ref, ya_ref, yb_ref, lsetot_ref, halo_ref, scr):
        i = pl.program_id(0)

        @pl.when(i == 0)
        def _():
            halo_ref[...] = jnp.zeros_like(halo_ref)

        ab = abcv_ref[:, 0:CONV_WIDTH].astype(F32)
        ac = abcv_ref[:, CONV_WIDTH:2 * CONV_WIDTH].astype(F32)
        av = abcv_ref[:, 2 * CONV_WIDTH:3 * CONV_WIDTH].astype(F32)
        ya0, _, u, _, _ = _conv_branch(ab, ac, av, halo_ref[...], cw_ref[...], cb_ref[...])
        halo_ref[...] = u[tm - SUBLANES:tm]
        ya0 = ya0.astype(BF16)
        ya0_ref[...] = ya0
        ya = _dot(ya0, wpa_ref[...])

        o_refs, l_refs = (o0_ref, o1_ref, o2_ref), (l0_ref, l1_ref, l2_ref)
        lses = [_from_streams(l_refs[g], scr, DILATIONS[g], 0, GROUP_WIDTH) for g in range(3)]
        top = jnp.maximum(jnp.maximum(lses[0], lses[1]), lses[2])
        lsetot = top + jnp.log(jnp.exp(lses[0] - top) + jnp.exp(lses[1] - top) + jnp.exp(lses[2] - top))
        lsetot_ref[...] = lsetot
        yb = jnp.zeros((tm, D_MODEL), F32)
        for g in range(3):
            og = _from_streams(o_refs[g], scr, DILATIONS[g], 0, GROUP_WIDTH)
            yb0 = (jnp.exp(lses[g] - lsetot) * og).astype(BF16)
            yb0_ref[:, g * GROUP_WIDTH:(g + 1) * GROUP_WIDTH] = yb0
            yb = yb + _dot(yb0, wpb_ref[g * GROUP_WIDTH:(g + 1) * GROUP_WIDTH, :])

        sa = _sigmoid(gates_ref[:, 0:D_MODEL].astype(F32) + bg_ref[0:1, :])
        sb = _sigmoid(gates_ref[:, D_MODEL:2 * D_MODEL].astype(F32) + bg_ref[1:2, :])
        ya_ref[...] = ya.astype(BF16)
        yb_ref[...] = yb.astype(BF16)
        mrg = (sa * ya + sb * yb).astype(BF16)
        mrg_ref[...] = mrg
        x1_ref[...] = x_ref[...] + _dot(mrg, wout_ref[...])

    outs = [jax.ShapeDtypeStruct((s, D_MODEL), F32),
            jax.ShapeDtypeStruct((s, CONV_WIDTH), BF16),
            jax.ShapeDtypeStruct((s, ATTN_WIDTH), BF16),
            jax.ShapeDtypeStruct((s, D_MODEL), BF16),
            jax.ShapeDtypeStruct((s, D_MODEL), BF16),
            jax.ShapeDtypeStruct((s, D_MODEL), BF16),
            jax.ShapeDtypeStruct((s, GROUP_WIDTH), F32)]
    return pl.pallas_call(
        body, name="mix_fwd", grid=(s // tm,), out_shape=outs,
        in_specs=[_rows(tm, D_MODEL), _rows(tm, 3 * CONV_WIDTH), _rows(tm, 2 * D_MODEL)]
        + [_stream_block(tm, d, GROUP_WIDTH) for d in DILATIONS] * 2
        + [_resident((3, CONV_WIDTH)), _resident((1, CONV_WIDTH)), _resident((2, D_MODEL)),
           _resident((CONV_WIDTH, D_MODEL)), _resident((ATTN_WIDTH, D_MODEL)), _resident((D_MODEL, D_MODEL))],
        out_specs=[_rows(tm, D_MODEL), _rows(tm, CONV_WIDTH), _rows(tm, ATTN_WIDTH), _rows(tm, D_MODEL),
                   _rows(tm, D_MODEL), _rows(tm, D_MODEL), _rows(tm, GROUP_WIDTH)],
        scratch_shapes=[pltpu.VMEM((SUBLANES, CONV_WIDTH), F32),
                        pltpu.VMEM((GROUP_WIDTH // LANES * tm, LANES), F32)],
        compiler_params=_params(1))(x, abcv, gates, *o_list, *lse_list, conv_w, conv_b, b_gate, w_pa, w_pb, w_out)


def _ffn_up_fwd(x1, g, w_up):
    s = x1.shape[0]
    n = w_up.shape[1]
    tm = ROW_TILE

    def body(x_ref, g_ref, w_ref, h_ref, y_ref):
        xv = x_ref[...]
        r = lax.rsqrt(jnp.mean(xv * xv, axis=-1, keepdims=True) + EPS)
        h = (xv * r * g_ref[...]).astype(BF16)
        h_ref[...] = h
        for lo, size in _col_chunks(n, 512):
            y_ref[:, lo:lo + size] = _dot(h, w_ref[:, lo:lo + size]).astype(BF16)

    return pl.pallas_call(
        body, name="ffn_up_fwd", grid=(s // tm,),
        out_shape=[jax.ShapeDtypeStruct((s, D_MODEL), BF16), jax.ShapeDtypeStruct((s, n), BF16)],
        in_specs=[_rows(tm, D_MODEL), _resident((1, D_MODEL)), _resident((D_MODEL, n))],
        out_specs=[_rows(tm, D_MODEL), _rows(tm, n)],
        compiler_params=_params(1))(x1, g, w_up)


FFN_CHUNK = 256


def _ffn_conv(up0_ref, halo16, lo, w_ref, b_ref):
    cur = up0_ref[:, lo:lo + FFN_CHUNK].astype(F32)
    halo8 = halo16[SUBLANES:2 * SUBLANES]
    sh1 = _shift_down(cur, 1, halo8)
    sh2 = _shift_down(cur, 2, halo8)
    w = w_ref[:, lo:lo + FFN_CHUNK]
    up = w[0:1] * sh2 + w[1:2] * sh1 + w[2:3] * cur + b_ref[:, lo:lo + FFN_CHUNK]
    return up, cur, sh1, sh2


def _ffn_act_fwd(x1, up0, target, conv_w, conv_b, w_down, g_final):
    s = x1.shape[0]
    tm = ROW_TILE

    def body(x1_ref, up0_ref, tgt_ref, cw_ref, cb_ref, wd_ref, gf_ref,
             act_ref, dx2_ref, dx2b_ref, dgf_ref, loss_ref, halo_ref):
        i = pl.program_id(0)

        @pl.when(i == 0)
        def _():
            halo_ref[...] = jnp.zeros_like(halo_ref)
            dgf_ref[...] = jnp.zeros_like(dgf_ref)
            loss_ref[...] = jnp.zeros_like(loss_ref)

        acc = jnp.zeros((tm, D_MODEL), F32)
        for lo, _ in _col_chunks(D_FF, FFN_CHUNK):
            gate, _, _, _ = _ffn_conv(up0_ref, halo_ref[:, lo:lo + FFN_CHUNK].astype(F32), lo, cw_ref, cb_ref)
            val, _, _, _ = _ffn_conv(up0_ref, halo_ref[:, D_FF + lo:D_FF + lo + FFN_CHUNK].astype(F32),
                                     D_FF + lo, cw_ref, cb_ref)
            act = (gate * _sigmoid(gate) * val).astype(BF16)
            act_ref[:, lo:lo + FFN_CHUNK] = act
            acc = acc + _dot(act, wd_ref[lo:lo + FFN_CHUNK, :])
        halo_ref[...] = up0_ref[tm - 2 * SUBLANES:tm, :]

        x2 = x1_ref[...] + acc
        r = lax.rsqrt(jnp.mean(x2 * x2, axis=-1, keepdims=True) + EPS)
        xn = x2 * r
        gf = gf_ref[...]
        err = xn * gf - tgt_ref[...]
        loss_ref[...] += (0.5 / D_MODEL) * jnp.sum(err * err)
        dy = err * (1.0 / D_MODEL)
        dgf_ref[...] += _colsum(dy * xn)
        dxn = dy * gf
        dx2 = r * (dxn - xn * jnp.mean(dxn * xn, axis=-1, keepdims=True))
        dx2_ref[...] = dx2
        dx2b_ref[...] = dx2.astype(BF16)

    return pl.pallas_call(
        body, name="ffn_act_fwd", grid=(s // tm,),
        out_shape=[jax.ShapeDtypeStruct((s, D_FF), BF16), jax.ShapeDtypeStruct((s, D_MODEL), F32),
                   jax.ShapeDtypeStruct((s, D_MODEL), BF16),
                   jax.ShapeDtypeStruct((1, D_MODEL), F32), jax.ShapeDtypeStruct((1, LANES), F32)],
        in_specs=[_rows(tm, D_MODEL), _rows(tm, 2 * D_FF), _rows(tm, D_MODEL),
                  _resident((3, 2 * D_FF)), _resident((1, 2 * D_FF)), _resident((D_FF, D_MODEL)),
                  _resident((1, D_MODEL))],
        out_specs=[_rows(tm, D_FF), _rows(tm, D_MODEL), _rows(tm, D_MODEL),
                   pl.BlockSpec((1, D_MODEL), lambda i: (0, 0)), pl.BlockSpec((1, LANES), lambda i: (0, 0))],
        scratch_shapes=[pltpu.VMEM((2 * SUBLANES, 2 * D_FF), BF16)],
        compiler_params=_params(1))(x1, up0, target, conv_w, conv_b, w_down, g_final)


def _ffn_act_bwd(dx2, up0, conv_w, conv_b, w_down):
    s = dx2.shape[0]
    tm = ROW_TILE
    nt = s // tm
    hb = tm // (2 * SUBLANES)

    def body(dx2_ref, up0_ref, pre_ref, cw_ref, cb_ref, wd_ref, dup0_ref, small_ref, next_ref):
        i = pl.program_id(0)

        @pl.when(i == 0)
        def _():
            next_ref[...] = jnp.zeros_like(next_ref)
            small_ref[...] = jnp.zeros_like(small_ref)

        not_first = (i < nt - 1).astype(F32)
        dx2b = dx2_ref[...].astype(BF16)
        for lo, _ in _col_chunks(D_FF, FFN_CHUNK):
            glo, vlo = lo, D_FF + lo
            gate, g0, g1, g2 = _ffn_conv(up0_ref, pre_ref[:, glo:glo + FFN_CHUNK].astype(F32) * not_first,
                                         glo, cw_ref, cb_ref)
            val, v0, v1, v2 = _ffn_conv(up0_ref, pre_ref[:, vlo:vlo + FFN_CHUNK].astype(F32) * not_first,
                                        vlo, cw_ref, cb_ref)
            dact = _dot_nt(dx2b, wd_ref[lo:lo + FFN_CHUNK, :])
            sg = _sigmoid(gate)
            dval = dact * (gate * sg)
            dgate = dact * val * (sg * (1.0 + gate * (1.0 - sg)))
            for clo, dz, z0, z1, z2 in ((glo, dgate, g0, g1, g2), (vlo, dval, v0, v1, v2)):
                small_ref[0:1, clo:clo + FFN_CHUNK] += _colsum(dz * z2)
                small_ref[1:2, clo:clo + FFN_CHUNK] += _colsum(dz * z1)
                small_ref[2:3, clo:clo + FFN_CHUNK] += _colsum(dz * z0)
                small_ref[3:4, clo:clo + FFN_CHUNK] += _colsum(dz)
                nxt = next_ref[:, clo:clo + FFN_CHUNK]
                w = cw_ref[:, clo:clo + FFN_CHUNK]
                dz0 = w[2:3] * dz + w[1:2] * _shift_up(dz, 1, nxt) + w[0:1] * _shift_up(dz, 2, nxt)
                dup0_ref[:, clo:clo + FFN_CHUNK] = dz0.astype(BF16)
                next_ref[:, clo:clo + FFN_CHUNK] = dz[0:SUBLANES]

    rev = lambda i: (nt - 1 - i, 0)
    pre = lambda i: (jnp.maximum((nt - 1 - i) * hb - 1, 0), 0)
    return pl.pallas_call(
        body, name="ffn_act_bwd", grid=(nt,),
        out_shape=[jax.ShapeDtypeStruct((s, 2 * D_FF), BF16), jax.ShapeDtypeStruct((SUBLANES, 2 * D_FF), F32)],
        in_specs=[pl.BlockSpec((tm, D_MODEL), rev), pl.BlockSpec((tm, 2 * D_FF), rev),
                  pl.BlockSpec((2 * SUBLANES, 2 * D_FF), pre),
                  _resident((3, 2 * D_FF)), _resident((1, 2 * D_FF)), _resident((D_FF, D_MODEL))],
        out_specs=[pl.BlockSpec((tm, 2 * D_FF), rev), pl.BlockSpec((SUBLANES, 2 * D_FF), lambda i: (0, 0))],
        scratch_shapes=[pltpu.VMEM((SUBLANES, 2 * D_FF), F32)],
        compiler_params=_params(1))(dx2, up0, up0, conv_w, conv_b, w_down)


def _norm_bwd(name, dys, w, pieces, xin, g, dres, stream_dils=None, w_transposed=False):
    s = xin.shape[0]
    tm = ROW_TILE
    nk = len(dys)
    dils = stream_dils or (1,) * nk
    widths = [dy.shape[-1] for dy in dys]
    relayout = [k for k in range(nk) if dils[k] > 1]
    nt = s // tm

    def body(*refs):
        dy_refs = refs[:nk]
        w_ref, x_ref, g_ref, dres_ref = refs[nk:nk + 4]
        dx_ref, dxb_ref, dg_ref = refs[nk + 4:nk + 7]
        nat_refs = refs[nk + 7:nk + 7 + len(relayout)]
        scr = refs[-1]
        i = pl.program_id(0)

        @pl.when(i == 0)
        def _():
            dg_ref[...] = jnp.zeros_like(dg_ref)

        dh = jnp.zeros((tm, D_MODEL), F32)
        for k in range(nk):
            if dils[k] > 1:
                nat_ref = nat_refs[relayout.index(k)]
                for lo, size in _col_chunks(widths[k], GROUP_WIDTH):
                    nat_ref[:, lo:lo + size] = _from_streams(dy_refs[k], scr, dils[k], lo, size).astype(BF16)
                src = nat_ref
            else:
                src = dy_refs[k]
            for first, width, wcol in pieces[k]:
                for lo, size in _col_chunks(width, 512):
                    dy = src[:, first + lo:first + lo + size]
                    if w_transposed:
                        dh = dh + _dot(dy, w_ref[wcol + lo:wcol + lo + size, :])
                    else:
                        dh = dh + _dot_nt(dy, w_ref[:, wcol + lo:wcol + lo + size])
        xv = x_ref[...]
        r = lax.rsqrt(jnp.mean(xv * xv, axis=-1, keepdims=True) + EPS)
        xn = xv * r
        dg_ref[...] += _colsum(dh * xn)
        dxn = dh * g_ref[...]
        dx = dres_ref[...] + r * (dxn - xn * jnp.mean(dxn * xn, axis=-1, keepdims=True))
        dx_ref[...] = dx
        dxb_ref[...] = dx.astype(BF16)

    dy_specs = [(_stream_block(tm, dils[k], widths[k]) if dils[k] > 1 else _rows(tm, widths[k])) for k in range(nk)]
    outs = [jax.ShapeDtypeStruct((s, D_MODEL), F32), jax.ShapeDtypeStruct((s, D_MODEL), BF16),
            jax.ShapeDtypeStruct((1, D_MODEL), F32)]
    outs += [jax.ShapeDtypeStruct((s, widths[k]), BF16) for k in relayout]
    return pl.pallas_call(
        body, name=name, grid=(nt,), out_shape=outs,
        in_specs=dy_specs + [_resident(w.shape), _rows(tm, D_MODEL), _resident((1, D_MODEL)), _rows(tm, D_MODEL)],
        out_specs=[_rows(tm, D_MODEL), _rows(tm, D_MODEL), pl.BlockSpec((1, D_MODEL), lambda i: (0, 0))]
        + [_rows(tm, widths[k]) for k in relayout],
        scratch_shapes=[pltpu.VMEM((GROUP_WIDTH // LANES * tm, LANES), F32)],
        compiler_params=_params(1))(*dys, w, xin, g, dres)


def _mix_bwd(dx1, abcv, gates, ya, yb, yb0, lsetot, conv_w, conv_b, b_gate, w_pa, w_pb, w_out, exchange=()):
    s = dx1.shape[0]
    tm = ROW_TILE
    nt = s // tm
    hb = tm // (2 * SUBLANES)
    nx = len(exchange)

    def body(*refs):
        (dx1_ref, abcv_ref, pre_ref, gates_ref, ya_ref, yb_ref, yb0_ref, lsetot_ref,
         cw_ref, cb_ref, bg_ref, wpa_ref, wpb_ref, wout_ref) = refs[:14]
        part_refs = refs[14:14 + nx]
        (dya_ref, dyb_ref, dgates_ref, dabcv_ref, dyb0_ref, dyl0_ref, dyl1_ref, dyl2_ref, aux0_ref, aux1_ref,
         aux2_ref, sm_gate_ref, sm_conv_ref) = refs[14 + nx:27 + nx]
        recv_refs = refs[27 + nx:27 + 2 * nx]
        next_ref, scr = refs[27 + 2 * nx:29 + 2 * nx]
        sems = refs[29 + 2 * nx:]
        i = pl.program_id(0)

        @pl.when(i == 0)
        def _():
            next_ref[...] = jnp.zeros_like(next_ref)
            sm_gate_ref[...] = jnp.zeros_like(sm_gate_ref)
            sm_conv_ref[...] = jnp.zeros_like(sm_conv_ref)
            for cp in _chip_exchange_copies(part_refs, recv_refs, *sems) if nx else ():
                cp.start()

        not_first = (i < nt - 1).astype(F32)
        dm = _dot_nt(dx1_ref[...].astype(BF16), wout_ref[...])
        sa = _sigmoid(gates_ref[:, 0:D_MODEL].astype(F32) + bg_ref[0:1, :])
        sb = _sigmoid(gates_ref[:, D_MODEL:2 * D_MODEL].astype(F32) + bg_ref[1:2, :])
        dya = (dm * sa).astype(BF16)
        dyb = (dm * sb).astype(BF16)
        dya_ref[...] = dya
        dyb_ref[...] = dyb
        dga = dm * ya_ref[...].astype(F32) * (sa * (1.0 - sa))
        dgb = dm * yb_ref[...].astype(F32) * (sb * (1.0 - sb))
        dgates_ref[:, 0:D_MODEL] = dga.astype(BF16)
        dgates_ref[:, D_MODEL:2 * D_MODEL] = dgb.astype(BF16)
        sm_gate_ref[0:1, :] += _colsum(dga)
        sm_gate_ref[1:2, :] += _colsum(dgb)

        dya0 = _dot_nt(dya, wpa_ref[...])
        ab = abcv_ref[:, 0:CONV_WIDTH].astype(F32)
        ac = abcv_ref[:, CONV_WIDTH:2 * CONV_WIDTH].astype(F32)
        av = abcv_ref[:, 2 * CONV_WIDTH:3 * CONV_WIDTH].astype(F32)
        pre = pre_ref[...].astype(F32) * not_first
        halo_u = (pre[:, CONV_WIDTH:2 * CONV_WIDTH] * pre[:, 2 * CONV_WIDTH:3 * CONV_WIDTH])[SUBLANES:2 * SUBLANES]
        w = cw_ref[...]
        _, cv, u, sh1, sh2 = _conv_branch(ab, ac, av, halo_u, w, cb_ref[...])
        dcv = dya0 * ab
        sm_conv_ref[0:1, :] += _colsum(dcv * sh2)
        sm_conv_ref[1:2, :] += _colsum(dcv * sh1)
        sm_conv_ref[2:3, :] += _colsum(dcv * u)
        sm_conv_ref[3:4, :] += _colsum(dcv)
        nxt = next_ref[...]
        du = w[2:3] * dcv + w[1:2] * _shift_up(dcv, 1, nxt) + w[0:1] * _shift_up(dcv, 2, nxt)
        next_ref[...] = dcv[0:SUBLANES]
        dabcv_ref[:, 0:CONV_WIDTH] = (dya0 * cv).astype(BF16)
        dabcv_ref[:, CONV_WIDTH:2 * CONV_WIDTH] = (du * av).astype(BF16)
        dabcv_ref[:, 2 * CONV_WIDTH:3 * CONV_WIDTH] = (du * ac).astype(BF16)

        head_r = lax.broadcasted_iota(jnp.int32, (GROUP_WIDTH, GROUP_WIDTH), 0) // HEAD_DIM
        head_c = lax.broadcasted_iota(jnp.int32, (GROUP_WIDTH, GROUP_WIDTH), 1) // HEAD_DIM
        same_head = (head_r == head_c).astype(BF16)
        prod = jnp.zeros((tm, GROUP_WIDTH), F32)
        dyb0s = []
        for g in range(3):
            cols = slice(g * GROUP_WIDTH, (g + 1) * GROUP_WIDTH)
            dyb0 = _dot_nt(dyb, wpb_ref[cols, :])
            dyb0_ref[:, cols] = dyb0.astype(BF16)
            dyb0s.append(dyb0)
            prod = prod + dyb0 * yb0_ref[:, cols].astype(F32)
        hi = prod.astype(BF16)
        mid = (prod - hi.astype(F32)).astype(BF16)
        lo = (prod - hi.astype(F32) - mid.astype(F32)).astype(BF16)
        delta = _dot(hi, same_head) + _dot(mid, same_head) + _dot(lo, same_head)
        lsetot = lsetot_ref[...]
        for g, (dy_ref, aux_ref) in enumerate(zip((dyl0_ref, dyl1_ref, dyl2_ref), (aux0_ref, aux1_ref, aux2_ref))):
            d = DILATIONS[g]
            _to_streams(dyb0s[g], scr, dy_ref, d, 0)
            _to_streams(lsetot, scr, aux_ref, d, 0)
            _to_streams(delta, scr, aux_ref, d, GROUP_WIDTH)

        if nx:
            @pl.when(i == nt - 1)
            def _():
                for cp in _chip_exchange_copies(part_refs, recv_refs, *sems):
                    cp.wait()

    rev = lambda i: (nt - 1 - i, 0)
    pre = lambda i: (jnp.maximum((nt - 1 - i) * hb - 1, 0), 0)
    rows = lambda width: pl.BlockSpec((tm, width), rev)
    outs = [jax.ShapeDtypeStruct((s, D_MODEL), BF16), jax.ShapeDtypeStruct((s, D_MODEL), BF16),
            jax.ShapeDtypeStruct((s, 2 * D_MODEL), BF16), jax.ShapeDtypeStruct((s, 3 * CONV_WIDTH), BF16),
            jax.ShapeDtypeStruct((s, ATTN_WIDTH), BF16)]
    outs += [jax.ShapeDtypeStruct((d, s // d, GROUP_WIDTH), BF16) for d in DILATIONS]
    outs += [jax.ShapeDtypeStruct((d, s // d, 2 * GROUP_WIDTH), F32) for d in DILATIONS]
    outs += [jax.ShapeDtypeStruct((SUBLANES, D_MODEL), F32), jax.ShapeDtypeStruct((SUBLANES, CONV_WIDTH), F32)]
    outs += [jax.ShapeDtypeStruct((3,) + a.shape[1:], a.dtype) for a in exchange]
    return pl.pallas_call(
        body, name="mix_bwd", grid=(nt,), out_shape=outs,
        in_specs=[rows(D_MODEL), rows(3 * CONV_WIDTH), pl.BlockSpec((2 * SUBLANES, 3 * CONV_WIDTH), pre),
                  rows(2 * D_MODEL), rows(D_MODEL), rows(D_MODEL), rows(ATTN_WIDTH), rows(GROUP_WIDTH),
                  _resident((3, CONV_WIDTH)), _resident((1, CONV_WIDTH)), _resident((2, D_MODEL)),
                  _resident((CONV_WIDTH, D_MODEL)), _resident((ATTN_WIDTH, D_MODEL)), _resident((D_MODEL, D_MODEL))]
        + [_ANY] * nx,
        out_specs=[rows(D_MODEL), rows(D_MODEL), rows(2 * D_MODEL), rows(3 * CONV_WIDTH), rows(ATTN_WIDTH)]
        + [_rev_stream_block(tm, d, GROUP_WIDTH, nt) for d in DILATIONS]
        + [_rev_stream_block(tm, d, 2 * GROUP_WIDTH, nt) for d in DILATIONS]
        + [pl.BlockSpec((SUBLANES, D_MODEL), lambda i: (0, 0)), pl.BlockSpec((SUBLANES, CONV_WIDTH), lambda i: (0, 0))]
        + [_ANY] * nx,
        scratch_shapes=[pltpu.VMEM((SUBLANES, CONV_WIDTH), F32),
                        pltpu.VMEM((GROUP_WIDTH // LANES * tm, LANES), F32)] + (_dma_sems(3 * nx) if nx else []),
        compiler_params=_params(1))(dx1, abcv, abcv, gates, ya, yb, yb0, lsetot,
                                    conv_w, conv_b, b_gate, w_pa, w_pb, w_out, *exchange)


def _attn_bwd(qkv, dy, aux, gi, exchange=()):
    d, length, _ = qkv.shape
    nb = length // ATTN_BLOCK
    q = ATTN_BLOCK
    gw = GROUP_WIDTH

    nx = len(exchange)

    def body(*refs):
        q_ref, kp_ref, kc_ref, vp_ref, vc_ref, dy_ref, aux_ref = refs[:7]
        part_refs = refs[7:7 + nx]
        out_ref = refs[7 + nx]
        recv_refs = refs[8 + nx:8 + 2 * nx]
        dq_ref, dkv_ref = refs[8 + 2 * nx:10 + 2 * nx]
        sems = refs[10 + 2 * nx:]
        n = pl.program_id(1)

        if nx:
            @pl.when((pl.program_id(0) == 0) & (n == 0))
            def _():
                for cp in _chip_exchange_copies(part_refs, recv_refs, *sems):
                    cp.start()

            @pl.when((pl.program_id(0) == d - 1) & (n == nb))
            def _():
                for cp in _chip_exchange_copies(part_refs, recv_refs, *sems):
                    cp.wait()

        @pl.when(n > 0)
        def _():
            out_ref[:, 0:gw] = dq_ref[...].astype(BF16)

        @pl.when(n == nb)
        def _():
            out_ref[:, gw:2 * gw] = dkv_ref[0].astype(BF16)
            out_ref[:, 2 * gw:3 * gw] = dkv_ref[1].astype(BF16)

        @pl.when(n < nb)
        def _():
            qs = _stack_heads(q_ref[...])
            dys = _stack_heads(dy_ref[...])
            k2 = jnp.concatenate([kp_ref[...], kc_ref[...]], axis=0)
            v2 = jnp.concatenate([vp_ref[...], vc_ref[...]], axis=0)
            lse = _per_head_col(aux_ref[:, 0:gw])
            delta = _per_head_col(aux_ref[:, gw:2 * gw])
            sc = _dot_nt(qs, k2) * ATTN_SCALE
            p = jnp.where(_band_mask(n == 0), jnp.exp(sc - lse), 0.0)
            dp = _dot_nt(dys, v2)
            ds = (p * (dp - delta) * ATTN_SCALE).astype(BF16)
            dq_ref[...] = _unstack_heads(_dot(ds, k2))
            dk2 = _dot_tn(ds, qs)
            dv2 = _dot_tn(p.astype(BF16), dys)

            @pl.when(n > 0)
            def _():
                out_ref[:, gw:2 * gw] = (dkv_ref[0] + dk2[0:q]).astype(BF16)
                out_ref[:, 2 * gw:3 * gw] = (dkv_ref[1] + dv2[0:q]).astype(BF16)

            dkv_ref[0] = dk2[q:2 * q]
            dkv_ref[1] = dv2[q:2 * q]

    last = nb - 1

    def blk(col, shift, width=gw):
        if shift:
            return pl.BlockSpec((None, q, width), lambda r, n: (r, jnp.maximum(n - 1, 0), col))
        return pl.BlockSpec((None, q, width), lambda r, n: (r, jnp.minimum(n, last), col))

    return pl.pallas_call(
        body, name=f"attn_bwd_g{gi}", grid=(d, nb + 1),
        out_shape=[jax.ShapeDtypeStruct((d, length, 3 * gw), BF16)]
        + [jax.ShapeDtypeStruct((3,) + a.shape[1:], a.dtype) for a in exchange],
        in_specs=[blk(0, False), blk(1, True), blk(1, False), blk(2, True), blk(2, False),
                  blk(0, False), blk(0, False, 2 * gw)] + [_ANY] * nx,
        out_specs=[pl.BlockSpec((None, q, 3 * gw), lambda r, n: (r, jnp.maximum(n - 1, 0), 0))] + [_ANY] * nx,
        scratch_shapes=[pltpu.VMEM((q, gw), F32), pltpu.VMEM((2, q, gw), F32)] + (_dma_sems(3 * nx) if nx else []),
        compiler_params=_params(2))(qkv, qkv, qkv, qkv, qkv, dy, aux, *exchange)


def _matmul_tn(name, a, b, col_tile=1024, row_tile=1024, slabs=0):
    s, k = a.shape
    n = b.shape[1]
    tk = min(row_tile, s)
    tn = col_tile
    steps = s // tk

    def body(a_ref, b_ref, o_ref, acc_ref):
        t = pl.program_id(1)

        @pl.when(t == 0)
        def _():
            acc_ref[...] = jnp.zeros_like(acc_ref)

        acc_ref[...] += _dot_tn(a_ref[...], b_ref[...])

        @pl.when(t == steps - 1)
        def _():
            if slabs:
                for q in range(per_tile):
                    o_ref[q] = acc_ref[:, q * width:(q + 1) * width].astype(BF16)
            else:
                o_ref[...] = acc_ref[...].astype(BF16)

    if slabs:
        width = n // slabs
        per_tile = tn // width
        out_shape = jax.ShapeDtypeStruct((slabs, k, width), BF16)
        out_spec = pl.BlockSpec((per_tile, k, width), lambda j, t: (j, 0, 0))
    else:
        out_shape = jax.ShapeDtypeStruct((k, n), BF16)
        out_spec = pl.BlockSpec((k, tn), lambda j, t: (0, j))
    return pl.pallas_call(
        body, name=name, grid=(n // tn, steps), out_shape=out_shape,
        in_specs=[pl.BlockSpec((tk, k), lambda j, t: (t, 0)), pl.BlockSpec((tk, tn), lambda j, t: (t, j))],
        out_specs=out_spec, scratch_shapes=[pltpu.VMEM((k, tn), F32)],
        compiler_params=_params(2))(a, b)


def _weight_allgather(bigs, smalls):
    nb, ns = len(bigs), len(smalls)
    n_sems = 3 * (2 * nb + ns)

    def body(*refs):
        big_refs, small_refs = refs[:nb], refs[nb:nb + ns]
        big_outs, small_outs = refs[nb + ns:2 * nb + ns], refs[2 * nb + ns:2 * (nb + ns)]
        send_sems, recv_sems = refs[2 * (nb + ns):]
        x, y, c, chips = _mesh_position()
        me = 2 * x + y
        sibling = (x, y, 1 - c)

        def copy(k, src, dst, to):
            return pltpu.make_async_remote_copy(src_ref=src, dst_ref=dst, send_sem=send_sems.at[k],
                                                recv_sem=recv_sems.at[k], device_id=to, device_id_type=MESH_ID)

        halves = [r.shape[1] // 2 for r in big_refs]
        first = []
        for j, (px, py) in enumerate(chips):
            for b in range(nb):
                mine = _half_rows(c, halves[b])
                first.append(copy(3 * b + j, big_refs[b].at[0, mine], big_outs[b].at[me, mine], (px, py, c)))
            for s in range(ns):
                first.append(copy(3 * (2 * nb + s) + j, small_refs[s].at[0], small_outs[s].at[me], (px, py, c)))
        for cp in first:
            cp.start()
        passed = []
        for j, (px, py) in enumerate(chips):
            for b in range(nb):
                landed = big_outs[b].at[2 * px + py, _half_rows(c, halves[b])]
                copy(3 * b + j, landed, landed, (px, py, c)).wait_recv()
                fwd = copy(3 * (nb + b) + j, landed, landed, sibling)
                fwd.start()
                passed.append(fwd)
        for j, (px, py) in enumerate(chips):
            for s in range(ns):
                landed = small_outs[s].at[2 * px + py]
                copy(3 * (2 * nb + s) + j, landed, landed, (px, py, c)).wait_recv()
            for b in range(nb):
                from_sibling = big_outs[b].at[2 * px + py, _half_rows(1 - c, halves[b])]
                copy(3 * (nb + b) + j, from_sibling, from_sibling, sibling).wait_recv()
        for cp in first + passed:
            cp.wait_send()

    return pl.pallas_call(
        body, name="weight_allgather",
        out_shape=[jax.ShapeDtypeStruct((N_CHIPS,) + a.shape[1:], a.dtype) for a in list(bigs) + list(smalls)],
        in_specs=[_ANY] * (nb + ns), out_specs=[_ANY] * (nb + ns),
        scratch_shapes=[pltpu.SemaphoreType.DMA((n_sems,)), pltpu.SemaphoreType.DMA((n_sems,))],
    )(*bigs, *smalls)


def _sibling_swap_halves(name, slabs):
    na = len(slabs)

    def body(*refs):
        src_refs, out_refs = refs[:na], refs[na:2 * na]
        send_sems, recv_sems = refs[2 * na:]
        x, y, c, _ = _mesh_position()
        cps = []
        for a in range(na):
            theirs = _half_rows(1 - c, src_refs[a].shape[1] // 2)
            cps.append(pltpu.make_async_remote_copy(
                src_ref=src_refs[a].at[:, theirs, :], dst_ref=out_refs[a], send_sem=send_sems.at[a],
                recv_sem=recv_sems.at[a], device_id=(x, y, 1 - c), device_id_type=MESH_ID))
        for cp in cps:
            cp.start()
        for cp in cps:
            cp.wait()

    return pl.pallas_call(
        body, name=name,
        out_shape=[jax.ShapeDtypeStruct((a.shape[0], a.shape[1] // 2, a.shape[2]), a.dtype) for a in slabs],
        in_specs=[_ANY] * na, out_specs=[_ANY] * na,
        scratch_shapes=[pltpu.SemaphoreType.DMA((na,)), pltpu.SemaphoreType.DMA((na,))])(*slabs)


def _chip_exchange(partials):
    na = len(partials)

    def body(*refs):
        cps = _chip_exchange_copies(refs[:na], refs[na:2 * na], *refs[2 * na:])
        for cp in cps:
            cp.start()
        for cp in cps:
            cp.wait()

    return pl.pallas_call(
        body, name="grad_chip_exchange",
        out_shape=[jax.ShapeDtypeStruct((3,) + a.shape[1:], a.dtype) for a in partials],
        in_specs=[_ANY] * na, out_specs=[_ANY] * na, scratch_shapes=_dma_sems(3 * na))(*partials)


def _sibling_share(halves):
    na = len(halves)

    def body(*refs):
        out_refs = refs[na:2 * na]
        send_sems, recv_sems = refs[2 * na:]
        x, y, c, _ = _mesh_position()
        cps = []
        for a in range(na):
            mine = out_refs[a].at[0, _half_rows(c, out_refs[a].shape[1] // 2)]
            cps.append(pltpu.make_async_remote_copy(src_ref=mine, dst_ref=mine, send_sem=send_sems.at[a],
                                                    recv_sem=recv_sems.at[a], device_id=(x, y, 1 - c),
                                                    device_id_type=MESH_ID))
        for cp in cps:
            cp.start()
        for a, cp in enumerate(cps):
            cp.wait_send()
            theirs = out_refs[a].at[0, _half_rows(1 - c, out_refs[a].shape[1] // 2)]
            pltpu.make_async_remote_copy(src_ref=theirs, dst_ref=theirs, send_sem=send_sems.at[a],
                                         recv_sem=recv_sems.at[a], device_id=(x, y, 1 - c),
                                         device_id_type=MESH_ID).wait_recv()

    return pl.pallas_call(
        body, name="grad_sibling_share", out_shape=[jax.ShapeDtypeStruct(a.shape, a.dtype) for a in halves],
        in_specs=[_ANY] * na, out_specs=[_ANY] * na, input_output_aliases={a: a for a in range(na)},
        scratch_shapes=[pltpu.SemaphoreType.DMA((na,)), pltpu.SemaphoreType.DMA((na,))])(*halves)


def _add_sibling(name, slab, received, core):
    n, rows, cols = slab.shape
    half = rows // 2

    def body(core_ref, a_ref, b_ref, o_ref):
        o_ref[...] = (a_ref[...].astype(F32) + b_ref[...].astype(F32)).astype(BF16)

    grid_spec = pltpu.PrefetchScalarGridSpec(
        num_scalar_prefetch=1, grid=(n,),
        in_specs=[pl.BlockSpec((None, half, cols), lambda s, core_ref: (s, core_ref[0], 0)),
                  pl.BlockSpec((None, half, cols), lambda s, core_ref: (s, 0, 0))],
        out_specs=pl.BlockSpec((None, half, cols), lambda s, core_ref: (s, 0, 0)))
    return pl.pallas_call(body, name=name, grid_spec=grid_spec,
                          out_shape=jax.ShapeDtypeStruct((n, half, cols), BF16),
                          compiler_params=_params(1))(core, slab, received)


def _sum_chips(name, partial, received, chip_core):
    _, half, cols = partial.shape

    def body(cc_ref, own_ref, recv_ref, o_ref):
        acc = own_ref[...].astype(F32)
        for k in range(3):
            acc = acc + recv_ref[k].astype(F32)
        o_ref[...] = acc

    grid_spec = pltpu.PrefetchScalarGridSpec(
        num_scalar_prefetch=1, grid=(1,),
        in_specs=[pl.BlockSpec((None, half, cols), lambda i, cc_ref: (cc_ref[0], 0, 0)),
                  pl.BlockSpec((3, half, cols), lambda i, cc_ref: (0, 0, 0))],
        out_specs=pl.BlockSpec((None, half, cols), lambda i, cc_ref: (0, cc_ref[1], 0)))
    return pl.pallas_call(body, name=name, grid_spec=grid_spec,
                          out_shape=jax.ShapeDtypeStruct((1, 2 * half, cols), F32),
                          compiler_params=_params(1))(chip_core, partial, received)


def _adam_math(w, g, m, v):
    nm = ADAM_B1 * m + (1.0 - ADAM_B1) * g
    nv = ADAM_B2 * v + (1.0 - ADAM_B2) * jnp.square(g)
    m_hat = nm / (1.0 - ADAM_B1 ** ADAM_STEP)
    v_hat = nv / (1.0 - ADAM_B2 ** ADAM_STEP)
    delta = -ADAM_LR * (m_hat / (jnp.sqrt(v_hat) + ADAM_EPS) + ADAM_WD * w)
    return delta, nm, nv


def _adamw(name, w, g, m, v):
    _, rows, cols = w.shape
    tr = next(t for t in (736, 512, 384, 352, 256, 128, 64, 32, 16, 8) if rows % t == 0)

    def body(w_ref, g_ref, m_ref, v_ref, d_ref, nm_ref, nv_ref):
        d_ref[...], nm_ref[...], nv_ref[...] = _adam_math(w_ref[...], g_ref[...], m_ref[...], v_ref[...])

    spec = pl.BlockSpec((None, tr, cols), lambda i: (0, i, 0))
    return pl.pallas_call(
        body, name=name, grid=(rows // tr,), out_shape=[jax.ShapeDtypeStruct(w.shape, F32)] * 3,
        in_specs=[spec] * 4, out_specs=[spec] * 3, compiler_params=_params(1))(w, g, m, v)


SMALL_PARAMS = ("norm_mix_g", "b_gate", "conv_a_w", "conv_a_b", "norm_ffn_g", "ffn_conv_w", "ffn_conv_b", "final_norm_g")


def _small_update(partials, params, moments_m, moments_v):
    na = len(partials)
    npar = len(SMALL_PARAMS)

    def body(*refs):
        in_refs = refs[:na]
        w_refs = refs[na:na + npar]
        m_refs = refs[na + npar:na + 2 * npar]
        v_refs = refs[na + 2 * npar:na + 3 * npar]
        pos = na + 3 * npar
        loss_ref = refs[pos]
        out_refs = refs[pos + 1:pos + 1 + 4 * npar]
        pos += 1 + 4 * npar
        acc_refs = refs[pos:pos + na]
        recv_refs = refs[pos + na:pos + 4 * na]
        send_sems, recv_sems = refs[pos + 4 * na:]
        x, y, c, _ = _mesh_position()
        chip = 2 * x + y
        for a in range(na):
            acc_refs[a][...] = in_refs[a][...]
        for stage, peer in enumerate(((x, y, 1 - c), (x, 1 - y, c), (1 - x, y, c))):
            cps = []
            for a in range(na):
                k = stage * na + a
                cps.append(pltpu.make_async_remote_copy(src_ref=acc_refs[a], dst_ref=recv_refs[k], send_sem=send_sems.at[k],
                                                        recv_sem=recv_sems.at[k], device_id=peer, device_id_type=MESH_ID))
            for cp in cps:
                cp.start()
            for cp in cps:
                cp.wait()
            for a in range(na):
                acc_refs[a][...] = acc_refs[a][...] + recv_refs[stage * na + a][...]

        mix, ffn, fin, gate, conv, ffnc, loss = acc_refs
        loss_ref[...] = loss[...]

        def cols(width):
            return pl.ds(pl.multiple_of(chip * width, LANES), width)

        grads = {
            "norm_mix_g": mix[...], "norm_ffn_g": ffn[...], "final_norm_g": fin[...],
            "b_gate": gate[0:2, cols(D_MODEL // N_CHIPS)],
            "conv_a_w": conv[0:3, cols(CONV_WIDTH // N_CHIPS)], "conv_a_b": conv[3:4, :],
            "ffn_conv_w": ffnc[0:3, cols(2 * D_FF // N_CHIPS)], "ffn_conv_b": ffnc[3:4, :]}
        for i, name in enumerate(SMALL_PARAMS):
            g = grads[name]
            if len(w_refs[i].shape) == 3:
                results = (g,) + _adam_math(w_refs[i][0], g, m_refs[i][0], v_refs[i][0])
                for o_ref, val in zip(out_refs[4 * i:4 * i + 4], results):
                    o_ref[0] = val
            else:
                results = (g,) + _adam_math(w_refs[i][...], g, m_refs[i][...], v_refs[i][...])
                for o_ref, val in zip(out_refs[4 * i:4 * i + 4], results):
                    o_ref[...] = val

    outs = [jax.ShapeDtypeStruct(partials[-1].shape, F32)]
    for w in params:
        outs += [jax.ShapeDtypeStruct(w.shape, F32)] * 4
    scratch = [pltpu.VMEM(p.shape, F32) for p in partials]
    scratch += [pltpu.VMEM(p.shape, F32) for _ in range(3) for p in partials]
    scratch += [pltpu.SemaphoreType.DMA((3 * na,)), pltpu.SemaphoreType.DMA((3 * na,))]
    n_in = na + 3 * npar
    return pl.pallas_call(
        body, name="small_update", out_shape=outs, in_specs=[_VMEM] * n_in, out_specs=[_VMEM] * len(outs),
        scratch_shapes=scratch)(*partials, *params, *moments_m, *moments_v)


def _gathered_columns(g):
    return jnp.transpose(g, (1, 0, 2)).reshape(g.shape[1], N_CHIPS * g.shape[2])


def _column_slabs(full):
    k, n = full.shape
    return jnp.transpose(full.reshape(k, N_CHIPS, n // N_CHIPS), (1, 0, 2))


def kernel(x, norm_mix_g, w_in, b_gate, conv_a_w, conv_a_b, w_proj_a, w_proj_b, w_out, norm_ffn_g, w_up, ffn_conv_w, ffn_conv_b, w_down, final_norm_g, loss_target, m_norm_mix_g, m_w_in, m_b_gate, m_conv_a_w, m_conv_a_b, m_w_proj_a, m_w_proj_b, m_w_out, m_norm_ffn_g, m_w_up, m_ffn_conv_w, m_ffn_conv_b, m_w_down, m_final_norm_g, v_norm_mix_g, v_w_in, v_b_gate, v_conv_a_w, v_conv_a_b, v_w_proj_a, v_w_proj_b, v_w_out, v_norm_ffn_g, v_w_up, v_ffn_conv_w, v_ffn_conv_b, v_w_down, v_final_norm_g):
    chip = (2 * lax.axis_index("x") + lax.axis_index("y")).astype(jnp.int32)
    core = lax.axis_index("c").astype(jnp.int32)
    core_arr = core.reshape(1)
    chip_core = jnp.stack([chip, core])
    xs, target = x[0], loss_target[0]
    g_final = final_norm_g.reshape(1, D_MODEL)

    def own_slot(gathered, own):
        return lax.dynamic_update_slice(gathered, own, (chip, 0, 0))

    def reduce_to_shards(names, slabs, exchange_in):
        from_sibling = _sibling_swap_halves("grad_swap_" + names[0], slabs)
        partials = [_add_sibling("grad_add_" + n, s, r, core_arr) for n, s, r in zip(names, slabs, from_sibling)]
        received, rest = exchange_in(partials)
        halves = [_sum_chips("grad_sum_" + n, p, r, chip_core) for n, p, r in zip(names, partials, received)]
        return halves, rest

    w_in_t, m_w_in_t, v_w_in_t = (jnp.swapaxes(a, 1, 2) for a in (w_in, m_w_in, v_w_in))
    w_in_tb = w_in_t.astype(BF16)
    (g_in,) = _weight_allgather([w_in_tb], [])
    w_in_full_t = own_slot(g_in, w_in_tb).reshape(D_IN, D_MODEL)
    later_w = [w_proj_a, w_proj_b, w_out, w_up, w_down]
    later_b = [w.astype(BF16) for w in later_w]
    small_sharded = [b_gate, conv_a_w, ffn_conv_w]
    fwd = _inproj_fwd(xs, norm_mix_g, w_in_full_t, later_b, small_sharded)
    h1, abcv, gates, qkv0, qkv1, qkv2 = fwd[:6]
    gathered_big, gathered_small = fwd[6:11], fwd[11:14]
    qkvs = (qkv0, qkv1, qkv2)
    attn0 = _attn_fwd(qkv0, 0, forward=gathered_big)
    attn = [attn0[:2], _attn_fwd(qkv1, 1), _attn_fwd(qkv2, 2)]
    g_pa, g_pb, g_out, g_up, g_down = [own_slot(g, own) for g, own in zip(attn0[2:], later_b)]
    g_bgate, g_convw, g_ffnw = [own_slot(g, own) for g, own in zip(gathered_small, small_sharded)]
    w_pa_full, w_pb_full, w_up_full = _gathered_columns(g_pa), _gathered_columns(g_pb), _gathered_columns(g_up)
    w_out_full, w_down_full = g_out.reshape(D_MODEL, D_MODEL), g_down.reshape(D_FF, D_MODEL)
    b_gate_full, conv_w_full, ffn_w_full = (_gathered_columns(g) for g in (g_bgate, g_convw, g_ffnw))

    x1, ya0, yb0, mrg, ya, yb, lsetot = _mix_fwd(
        xs, abcv, gates, [a[0] for a in attn], [a[1] for a in attn], conv_w_full, conv_a_b, b_gate_full,
        w_pa_full, w_pb_full, w_out_full)
    h2, up0 = _ffn_up_fwd(x1, norm_ffn_g, w_up_full)
    act, dx2, dx2b, d_g_final, loss = _ffn_act_fwd(x1, up0, target, ffn_w_full, ffn_conv_b, w_down_full, g_final)

    d_up0, ffn_small = _ffn_act_bwd(dx2, up0, ffn_w_full, ffn_conv_b, w_down_full)
    d_w_down = _matmul_tn("dw_down", act, dx2b, col_tile=512)
    dx1, dx1b, d_g_ffn = _norm_bwd("ffn_up_bwd", [d_up0], w_up_full, [[(0, 2 * D_FF, 0)]], x1, norm_ffn_g, dx2)
    d_w_up = _matmul_tn("dw_up", h2, d_up0, col_tile=2 * D_FF // N_CHIPS, slabs=N_CHIPS)

    def behind_mix_bwd(partials):
        res = _mix_bwd(dx1, abcv, gates, ya, yb, yb0, lsetot, conv_w_full, conv_a_b, b_gate_full,
                       w_pa_full, w_pb_full, w_out_full, exchange=partials)
        return res[13:], res[:13]

    halves_ffn, mix_res = reduce_to_shards(
        ("w_up", "w_down"), [d_w_up, d_w_down.reshape(N_CHIPS, D_FF // N_CHIPS, D_MODEL)], behind_mix_bwd)
    (d_ya, d_yb, d_gates, d_abcv, d_yb0, dyl0, dyl1, dyl2, aux0, aux1, aux2, gate_small, conv_small) = mix_res
    d_w_out = _matmul_tn("dw_out", mrg, dx1b)
    d_w_pa = _matmul_tn("dw_proj_a", ya0, d_ya, slabs=N_CHIPS)
    d_w_pb = _matmul_tn("dw_proj_b", yb0, d_yb, slabs=N_CHIPS)

    def behind_attn_bwd(partials):
        res = _attn_bwd(qkv0, dyl0, aux0, 0, exchange=partials)
        return res[1:], res[0]

    halves_mix, d_qkv0 = reduce_to_shards(
        ("w_proj_a", "w_proj_b", "w_out"),
        [d_w_pa, d_w_pb, d_w_out.reshape(N_CHIPS, D_MODEL // N_CHIPS, D_MODEL)], behind_attn_bwd)
    (d_qkv1,), (d_qkv2,) = _attn_bwd(qkv1, dyl1, aux1, 1), _attn_bwd(qkv2, dyl2, aux2, 2)

    dq = [d_qkv0[0], d_qkv1, d_qkv2]
    group_pieces = [[(j * GROUP_WIDTH, GROUP_WIDTH, base + g * GROUP_WIDTH) for j, base in enumerate((COL_Q, COL_K, COL_V))]
                    for g in range(3)]
    grad_x, _, d_g_mix, nat1, nat2 = _norm_bwd(
        "inproj_bwd", [d_abcv, d_gates] + dq, w_in_full_t,
        [[(0, 3 * CONV_WIDTH, COL_ABCV)], [(0, 2 * D_MODEL, COL_GATES)]] + group_pieces,
        xs, norm_mix_g, dx1, stream_dils=(1, 1, 1, 4, 16), w_transposed=True)

    d_w_abcv = _matmul_tn("dw_in_abcv", d_abcv, h1)
    d_w_gates = _matmul_tn("dw_in_gates", d_gates, h1)
    d_w_groups = [_matmul_tn(f"dw_in_qkv{g}", t, h1) for g, t in enumerate((dq[0], nat1, nat2))]
    gw = GROUP_WIDTH
    d_w_in_t = jnp.concatenate(
        [d_w_abcv] + [d_w_groups[g][j * gw:(j + 1) * gw] for j in range(3) for g in range(3)] + [d_w_gates], axis=0)
    halves_in, _ = reduce_to_shards(("w_in",), [d_w_in_t.reshape(N_CHIPS, D_IN // N_CHIPS, D_MODEL)],
                                    lambda partials: (_chip_exchange(partials), None))

    big_names = ("w_in", "w_proj_a", "w_proj_b", "w_out", "w_up", "w_down")
    big_grads = _sibling_share(halves_in + halves_mix + halves_ffn)
    big_w = dict(w_in=w_in_t, w_proj_a=w_proj_a, w_proj_b=w_proj_b, w_out=w_out, w_up=w_up, w_down=w_down)
    big_m = dict(w_in=m_w_in_t, w_proj_a=m_w_proj_a, w_proj_b=m_w_proj_b, w_out=m_w_out, w_up=m_w_up, w_down=m_w_down)
    big_v = dict(w_in=v_w_in_t, w_proj_a=v_w_proj_a, w_proj_b=v_w_proj_b, w_out=v_w_out, w_up=v_w_up, w_down=v_w_down)

    fin_w, fin_m, fin_v = (a.reshape(1, D_MODEL) for a in (final_norm_g, m_final_norm_g, v_final_norm_g))
    small_w = [norm_mix_g, b_gate, conv_a_w, conv_a_b, norm_ffn_g, ffn_conv_w, ffn_conv_b, fin_w]
    small_m = [m_norm_mix_g, m_b_gate, m_conv_a_w, m_conv_a_b, m_norm_ffn_g, m_ffn_conv_w, m_ffn_conv_b, fin_m]
    small_v = [v_norm_mix_g, v_b_gate, v_conv_a_w, v_conv_a_b, v_norm_ffn_g, v_ffn_conv_w, v_ffn_conv_b, fin_v]
    small_out = _small_update([d_g_mix, d_g_ffn, d_g_final, gate_small, conv_small, ffn_small, loss],
                              small_w, small_m, small_v)
    total_loss = small_out[0][0, 0]

    grads, delta, new_m, new_v = {}, {}, {}, {}
    for i, n in enumerate(SMALL_PARAMS):
        vals = small_out[1 + 4 * i:5 + 4 * i]
        if n == "final_norm_g":
            vals = [a.reshape(D_MODEL) for a in vals]
        grads[n], delta[n], new_m[n], new_v[n] = vals
    for n, g in zip(big_names, big_grads):
        vals = (g,) + tuple(_adamw("adamw_" + n, big_w[n], g, big_m[n], big_v[n]))
        if n == "w_in":
            vals = [jnp.swapaxes(a, 1, 2) for a in vals]
        grads[n], delta[n], new_m[n], new_v[n] = vals

    names = ["norm_mix_g", "w_in", "b_gate", "conv_a_w", "conv_a_b", "w_proj_a", "w_proj_b", "w_out", "norm_ffn_g", "w_up",
             "ffn_conv_w", "ffn_conv_b", "w_down", "final_norm_g"]
    out = [total_loss, grad_x[None]]
    for group in (grads, delta, new_m, new_v):
        out += [group[n] for n in names]
    return tuple(out)
```

```python
import jax
import jax.numpy as jnp
from jax import lax
from jax.experimental import pallas as pl
from jax.experimental.pallas import tpu as pltpu

F32 = jnp.float32
BF16 = jnp.bfloat16

D_MODEL = 1024
CONV_WIDTH = 512
ATTN_WIDTH = 768
GROUP_WIDTH = 256
HEAD_DIM = 64
HEADS_PER_GROUP = 4
DILATIONS = (1, 4, 16)
ATTN_BLOCK = 128
D_FF = 2816
D_IN = 5888
EPS = 1e-6
NEG_INF = -1e30
ATTN_SCALE = HEAD_DIM ** -0.5

COL_ABCV = 0
COL_Q = 1536
COL_K = 2304
COL_V = 3072
COL_GATES = 3840

ADAM_LR = 0.001
ADAM_B1 = 0.9
ADAM_B2 = 0.999
ADAM_EPS = 1e-08
ADAM_WD = 0.01
ADAM_STEP = 10

LANES = 128
SUBLANES = 8
ROW_TILE = 512
VMEM_LIMIT = 56 * 1024 * 1024

_NT = (((1,), (1,)), ((), ()))
_TN = (((0,), (0,)), ((), ()))


def _params(n_axes, vmem=VMEM_LIMIT):
    return pltpu.CompilerParams(dimension_semantics=("arbitrary",) * n_axes, vmem_limit_bytes=vmem)


def _resident(shape):
    nd = len(shape)
    return pl.BlockSpec(shape, lambda *_: (0,) * nd, pipeline_mode=pl.Buffered(1))


def _rows(tm, width, col_block=0):
    return pl.BlockSpec((tm, width), lambda i: (i, col_block))


def _col_chunks(n, cmax):
    out, lo = [], 0
    while lo < n:
        size = min(cmax, n - lo)
        out.append((lo, size))
        lo += size
    return out


def _dot(a, b):
    return jnp.dot(a, b, preferred_element_type=F32)


def _dot_nt(a, b):
    return lax.dot_general(a, b, _NT, preferred_element_type=F32)


def _dot_tn(a, b):
    return lax.dot_general(a, b, _TN, preferred_element_type=F32)


def _sigmoid(x):
    return 1.0 / (1.0 + jnp.exp(-x))


def _shift_down(v, k, halo8):
    tm = v.shape[0]
    rolled = pltpu.roll(v, k, 0)
    fix = jnp.tile(pltpu.roll(halo8, k, 0), (tm // SUBLANES, 1))
    row = lax.broadcasted_iota(jnp.int32, v.shape, 0)
    return jnp.where(row < k, fix, rolled)


def _shift_up(v, k, halo8):
    tm = v.shape[0]
    rolled = pltpu.roll(v, tm - k, 0)
    fix = jnp.tile(pltpu.roll(halo8, SUBLANES - k, 0), (tm // SUBLANES, 1))
    row = lax.broadcasted_iota(jnp.int32, v.shape, 0)
    return jnp.where(row >= tm - k, fix, rolled)


def _colsum(v):
    return jnp.sum(v, axis=0, keepdims=True)


def _to_streams(val, scr, out_ref, d, col0):
    tm = val.shape[0]
    panels = val.shape[1] // LANES
    for p in range(panels):
        scr[pl.ds(p * tm, tm), :] = val[:, p * LANES:(p + 1) * LANES]
    for r in range(d):
        for p in range(panels):
            piece = scr[pl.ds(p * tm + r, tm // d, stride=d), :]
            out_ref[r, :, col0 + p * LANES: col0 + (p + 1) * LANES] = piece.astype(out_ref.dtype)


def _from_streams(in_ref, scr, d, col0, width):
    panels = width // LANES
    rows = in_ref.shape[1]
    tm = rows * d
    for r in range(d):
        for p in range(panels):
            scr[pl.ds(p * tm + r, rows, stride=d), :] = in_ref[r, :, col0 + p * LANES: col0 + (p + 1) * LANES].astype(F32)
    return jnp.concatenate([scr[pl.ds(p * tm, tm), :] for p in range(panels)], axis=1)


def _stream_block(tm, d, width):
    return pl.BlockSpec((d, tm // d, width), lambda i: (0, i, 0))


def _rev_stream_block(tm, d, width, nt):
    return pl.BlockSpec((d, tm // d, width), lambda i: (0, nt - 1 - i, 0))


N_CHIPS = 4
MESH_ID = pl.DeviceIdType.MESH
_ANY = pl.BlockSpec(memory_space=pl.ANY)
_VMEM = pl.BlockSpec(memory_space=pltpu.VMEM)


def _mesh_position():
    x, y, c = lax.axis_index("x"), lax.axis_index("y"), lax.axis_index("c")
    other_chips = [(1 - x, y), (x, 1 - y), (1 - x, 1 - y)]
    return x, y, c, other_chips


def _half_rows(c, half):
    return pl.ds(pl.multiple_of(c * half, 16), half)


def _remote_copy(k, src, dst, to, send_sems, recv_sems):
    return pltpu.make_async_remote_copy(src_ref=src, dst_ref=dst, send_sem=send_sems.at[k], recv_sem=recv_sems.at[k],
                                        device_id=to, device_id_type=MESH_ID)


def _gather_first_copies(big_refs, small_refs, big_outs, small_outs, send_sems, recv_sems):
    x, y, c, chips = _mesh_position()
    me = 2 * x + y
    nb = len(big_refs)
    cps = []
    for j, (px, py) in enumerate(chips):
        for b in range(nb):
            mine = _half_rows(c, big_refs[b].shape[1] // 2)
            cps.append(_remote_copy(3 * b + j, big_refs[b].at[0, mine], big_outs[b].at[me, mine], (px, py, c),
                                    send_sems, recv_sems))
        for s in range(len(small_refs)):
            cps.append(_remote_copy(3 * (nb + s) + j, small_refs[s].at[0], small_outs[s].at[me], (px, py, c),
                                    send_sems, recv_sems))
    return cps


def _gather_forward_copies(bufs, send_sems, recv_sems):
    x, y, c, chips = _mesh_position()
    cps = []
    for j, (px, py) in enumerate(chips):
        for b in range(len(bufs)):
            landed = bufs[b].at[2 * px + py, _half_rows(c, bufs[b].shape[1] // 2)]
            cps.append(_remote_copy(3 * b + j, landed, landed, (x, y, 1 - c), send_sems, recv_sems))
    return cps


def _chip_exchange_copies(src_refs, out_refs, send_sems, recv_sems):
    x, y, c, chips = _mesh_position()
    cps = []
    for j, (px, py) in enumerate(chips):
        for a in range(len(src_refs)):
            cps.append(_remote_copy(3 * a + j, src_refs[a].at[2 * px + py], out_refs[a].at[j], (px, py, c),
                                    send_sems, recv_sems))
    return cps


def _dma_sems(n):
    return [pltpu.SemaphoreType.DMA((n,)), pltpu.SemaphoreType.DMA((n,))]


def _inproj_fwd(x, g, w_in_t, big_shards, small_shards):
    s = x.shape[0]
    tm = ROW_TILE
    nt = s // tm
    nb, ns = len(big_shards), len(small_shards)
    n_fixed_in, n_fixed_out = 3, 6

    def body(*refs):
        x_ref, g_ref, w_ref = refs[:n_fixed_in]
        shard_refs = refs[n_fixed_in:n_fixed_in + nb + ns]
        pos = n_fixed_in + nb + ns
        h_ref, abcv_ref, gates_ref, qkv0_ref, qkv1_ref, qkv2_ref = refs[pos:pos + n_fixed_out]
        gathered_refs = refs[pos + n_fixed_out:pos + n_fixed_out + nb + ns]
        scr, send_sems, recv_sems = refs[pos + n_fixed_out + nb + ns:]
        i = pl.program_id(0)

        def gather_copies():
            return _gather_first_copies(shard_refs[:nb], shard_refs[nb:], gathered_refs[:nb], gathered_refs[nb:],
                                        send_sems, recv_sems)

        @pl.when(i == 0)
        def _():
            for cp in gather_copies():
                cp.start()

        xv = x_ref[...]
        r = lax.rsqrt(jnp.mean(xv * xv, axis=-1, keepdims=True) + EPS)
        h = (xv * r * g_ref[...]).astype(BF16)
        h_ref[...] = h
        for lo, size in _col_chunks(3 * CONV_WIDTH, 512):
            abcv_ref[:, lo:lo + size] = _dot_nt(h, w_ref[COL_ABCV + lo: COL_ABCV + lo + size, :]).astype(BF16)
        for lo, size in _col_chunks(2 * D_MODEL, 512):
            gates_ref[:, lo:lo + size] = _dot_nt(h, w_ref[COL_GATES + lo: COL_GATES + lo + size, :]).astype(BF16)
        for gi, (d, out_ref) in enumerate(zip(DILATIONS, (qkv0_ref, qkv1_ref, qkv2_ref))):
            for j, base in enumerate((COL_Q, COL_K, COL_V)):
                lo = base + gi * GROUP_WIDTH
                y = _dot_nt(h, w_ref[lo:lo + GROUP_WIDTH, :])
                _to_streams(y, scr, out_ref, d, j * GROUP_WIDTH)

        @pl.when(i == nt - 1)
        def _():
            for cp in gather_copies():
                cp.wait()

    outs = [jax.ShapeDtypeStruct((s, D_MODEL), BF16),
            jax.ShapeDtypeStruct((s, 3 * CONV_WIDTH), BF16),
            jax.ShapeDtypeStruct((s, 2 * D_MODEL), BF16)]
    outs += [jax.ShapeDtypeStruct((d, s // d, 3 * GROUP_WIDTH), BF16) for d in DILATIONS]
    outs += [jax.ShapeDtypeStruct((N_CHIPS,) + a.shape[1:], a.dtype) for a in list(big_shards) + list(small_shards)]
    return pl.pallas_call(
        body, name="inproj_fwd", grid=(nt,), out_shape=outs,
        in_specs=[_rows(tm, D_MODEL), _resident((1, D_MODEL)), _resident((D_IN, D_MODEL))] + [_ANY] * (nb + ns),
        out_specs=[_rows(tm, D_MODEL), _rows(tm, 3 * CONV_WIDTH), _rows(tm, 2 * D_MODEL)]
        + [_stream_block(tm, d, 3 * GROUP_WIDTH) for d in DILATIONS] + [_ANY] * (nb + ns),
        scratch_shapes=[pltpu.VMEM((GROUP_WIDTH // LANES * tm, LANES), F32)] + _dma_sems(3 * (nb + ns)),
        compiler_params=_params(1))(x, g, w_in_t, *big_shards, *small_shards)


def _head_of_lane(shape):
    return lax.broadcasted_iota(jnp.int32, shape, 1) // HEAD_DIM


def _stack_heads(v):
    head = _head_of_lane(v.shape)
    return jnp.concatenate([jnp.where(head == h, v, jnp.zeros_like(v)) for h in range(HEADS_PER_GROUP)], axis=0)


def _unstack_heads(v):
    q = ATTN_BLOCK
    head = _head_of_lane((q, v.shape[1]))
    out = jnp.zeros((q, v.shape[1]), v.dtype)
    for h in range(HEADS_PER_GROUP):
        out = jnp.where(head == h, v[h * q:(h + 1) * q], out)
    return out


def _per_head_rows(col):
    q = ATTN_BLOCK
    head = _head_of_lane((q, GROUP_WIDTH))
    out = jnp.zeros((q, GROUP_WIDTH), col.dtype)
    for h in range(HEADS_PER_GROUP):
        out = jnp.where(head == h, col[h * q:(h + 1) * q], out)
    return out


def _per_head_col(v):
    head = _head_of_lane(v.shape)
    cols = [jnp.max(jnp.where(head == h, v, -jnp.inf), axis=1, keepdims=True) for h in range(HEADS_PER_GROUP)]
    return jnp.concatenate(cols, axis=0)


def _band_mask(first_block):
    rows = HEADS_PER_GROUP * ATTN_BLOCK
    qi = lax.broadcasted_iota(jnp.int32, (rows, 2 * ATTN_BLOCK), 0) % ATTN_BLOCK
    kj = lax.broadcasted_iota(jnp.int32, (rows, 2 * ATTN_BLOCK), 1)
    dist = qi + ATTN_BLOCK - kj
    return (dist >= 0) & (dist <= ATTN_BLOCK) & ((kj >= ATTN_BLOCK) | jnp.logical_not(first_block))


def _attn_fwd(qkv, gi, forward=()):
    d, length, _ = qkv.shape
    nb = length // ATTN_BLOCK
    q = ATTN_BLOCK
    nf = len(forward)

    def body(*refs):
        q_ref, kp_ref, kc_ref, vp_ref, vc_ref = refs[:5]
        o_ref, lse_ref = refs[5 + nf:7 + nf]
        buf_refs = refs[7 + nf:7 + 2 * nf]
        sems = refs[7 + 2 * nf:]
        n = pl.program_id(1)
        first_step = (pl.program_id(0) == 0) & (n == 0)
        last_step = (pl.program_id(0) == d - 1) & (n == nb - 1)

        if nf:
            @pl.when(first_step)
            def _():
                for cp in _gather_forward_copies(buf_refs, *sems):
                    cp.start()

        qs = _stack_heads(q_ref[...])
        k2 = jnp.concatenate([kp_ref[...], kc_ref[...]], axis=0)
        v2 = jnp.concatenate([vp_ref[...], vc_ref[...]], axis=0)
        sc = _dot_nt(qs, k2) * ATTN_SCALE
        sc = jnp.where(_band_mask(n == 0), sc, NEG_INF)
        m = jnp.max(sc, axis=1, keepdims=True)
        p = jnp.exp(sc - m)
        l = jnp.sum(p, axis=1, keepdims=True)
        of = _dot(p.astype(BF16), v2) / l
        o_ref[...] = _unstack_heads(of).astype(BF16)
        lse_ref[...] = _per_head_rows(m + jnp.log(l))

        if nf:
            @pl.when(last_step)
            def _():
                for cp in _gather_forward_copies(buf_refs, *sems):
                    cp.wait()

    def blk(col, prev):
        if prev:
            return pl.BlockSpec((None, q, GROUP_WIDTH), lambda r, n: (r, jnp.maximum(n - 1, 0), col))
        return pl.BlockSpec((None, q, GROUP_WIDTH), lambda r, n: (r, n, col))

    return pl.pallas_call(
        body, name=f"attn_fwd_g{gi}", grid=(d, nb),
        out_shape=[jax.ShapeDtypeStruct((d, length, GROUP_WIDTH), BF16),
                   jax.ShapeDtypeStruct((d, length, GROUP_WIDTH), F32)]
        + [jax.ShapeDtypeStruct(a.shape, a.dtype) for a in forward],
        in_specs=[blk(0, False), blk(1, True), blk(1, False), blk(2, True), blk(2, False)] + [_ANY] * nf,
        out_specs=[blk(0, False), blk(0, False)] + [_ANY] * nf,
        input_output_aliases={5 + a: 2 + a for a in range(nf)},
        scratch_shapes=_dma_sems(3 * nf) if nf else [],
        compiler_params=_params(2))(qkv, qkv, qkv, qkv, qkv, *forward)


def _conv_branch(ab, ac, av, halo_u, w, b):
    u = ac * av
    sh1 = _shift_down(u, 1, halo_u)
    sh2 = _shift_down(u, 2, halo_u)
    cv = w[0:1] * sh2 + w[1:2] * sh1 + w[2:3] * u + b
    return ab * cv, cv, u, sh1, sh2


def _mix_fwd(x, abcv, gates, o_list, lse_list, conv_w, conv_b, b_gate, w_pa, w_pb, w_out):
    s = x.shape[0]
    tm = ROW_TILE

    def body(x_ref, abcv_ref, gates_ref, o0_ref, o1_ref, o2_ref, l0_ref, l1_ref, l2_ref,
             cw_ref, cb_ref, bg_ref, wpa_ref, wpb_ref, wout_ref,
             x1_ref, ya0_ref, yb0_ref, mrg_ref, ya_ref, yb_ref, lsetot_ref, halo_ref, scr):
        i = pl.program_id(0)

        @pl.when(i == 0)
        def _():
            halo_ref[...] = jnp.zeros_like(halo_ref)

        ab = abcv_ref[:, 0:CONV_WIDTH].astype(F32)
        ac = abcv_ref[:, CONV_WIDTH:2 * CONV_WIDTH].astype(F32)
        av = abcv_ref[:, 2 * CONV_WIDTH:3 * CONV_WIDTH].astype(F32)
        ya0, _, u, _, _ = _conv_branch(ab, ac, av, halo_ref[...], cw_ref[...], cb_ref[...])
        halo_ref[...] = u[tm - SUBLANES:tm]
        ya0 = ya0.astype(BF16)
        ya0_ref[...] = ya0
        ya = _dot(ya0, wpa_ref[...])

        o_refs, l_refs = (o0_ref, o1_ref, o2_ref), (l0_ref, l1_ref, l2_ref)
        lses = [_from_streams(l_refs[g], scr, DILATIONS[g], 0, GROUP_WIDTH) for g in range(3)]
        top = jnp.maximum(jnp.maximum(lses[0], lses[1]), lses[2])
        lsetot = top + jnp.log(jnp.exp(lses[0] - top) + jnp.exp(lses[1] - top) + jnp.exp(lses[2] - top))
        lsetot_ref[...] = lsetot
        yb = jnp.zeros((tm, D_MODEL), F32)
        for g in range(3):
            og = _from_streams(o_refs[g], scr, DILATIONS[g], 0, GROUP_WIDTH)
            yb0 = (jnp.exp(lses[g] - lsetot) * og).astype(BF16)
            yb0_ref[:, g * GROUP_WIDTH:(g + 1) * GROUP_WIDTH] = yb0
            yb = yb + _dot(yb0, wpb_ref[g * GROUP_WIDTH:(g + 1) * GROUP_WIDTH, :])

        sa = _sigmoid(gates_ref[:, 0:D_MODEL].astype(F32) + bg_ref[0:1, :])
        sb = _sigmoid(gates_ref[:, D_MODEL:2 * D_MODEL].astype(F32) + bg_ref[1:2, :])
        ya_ref[...] = ya.astype(BF16)
        yb_ref[...] = yb.astype(BF16)
        mrg = (sa * ya + sb * yb).astype(BF16)
        mrg_ref[...] = mrg
        x1_ref[...] = x_ref[...] + _dot(mrg, wout_ref[...])

    outs = [jax.ShapeDtypeStruct((s, D_MODEL), F32),
            jax.ShapeDtypeStruct((s, CONV_WIDTH), BF16),
            jax.ShapeDtypeStruct((s, ATTN_WIDTH), BF16),
            jax.ShapeDtypeStruct((s, D_MODEL), BF16),
            jax.ShapeDtypeStruct((s, D_MODEL), BF16),
            jax.ShapeDtypeStruct((s, D_MODEL), BF16),
            jax.ShapeDtypeStruct((s, GROUP_WIDTH), F32)]
    return pl.pallas_call(
        body, name="mix_fwd", grid=(s // tm,), out_shape=outs,
        in_specs=[_rows(tm, D_MODEL), _rows(tm, 3 * CONV_WIDTH), _rows(tm, 2 * D_MODEL)]
        + [_stream_block(tm, d, GROUP_WIDTH) for d in DILATIONS] * 2
        + [_resident((3, CONV_WIDTH)), _resident((1, CONV_WIDTH)), _resident((2, D_MODEL)),
           _resident((CONV_WIDTH, D_MODEL)), _resident((ATTN_WIDTH, D_MODEL)), _resident((D_MODEL, D_MODEL))],
        out_specs=[_rows(tm, D_MODEL), _rows(tm, CONV_WIDTH), _rows(tm, ATTN_WIDTH), _rows(tm, D_MODEL),
                   _rows(tm, D_MODEL), _rows(tm, D_MODEL), _rows(tm, GROUP_WIDTH)],
        scratch_shapes=[pltpu.VMEM((SUBLANES, CONV_WIDTH), F32),
                        pltpu.VMEM((GROUP_WIDTH // LANES * tm, LANES), F32)],
        compiler_params=_params(1))(x, abcv, gates, *o_list, *lse_list, conv_w, conv_b, b_gate, w_pa, w_pb, w_out)


FFN_CHUNK = 512


def _ffn_up_fwd(x1, g, w_up, conv_w, conv_b):
    s = x1.shape[0]
    n = w_up.shape[1]
    tm = ROW_TILE

    def body(x_ref, g_ref, w_ref, cw_ref, cb_ref, h_ref, up0_ref, up_ref, halo_ref):
        @pl.when(pl.program_id(0) == 0)
        def _():
            halo_ref[...] = jnp.zeros_like(halo_ref)

        xv = x_ref[...]
        r = lax.rsqrt(jnp.mean(xv * xv, axis=-1, keepdims=True) + EPS)
        h = (xv * r * g_ref[...]).astype(BF16)
        h_ref[...] = h
        for lo, size in _col_chunks(n, FFN_CHUNK):
            cols = slice(lo, lo + size)
            y = _dot(h, w_ref[:, cols])
            up0_ref[:, cols] = y.astype(BF16)
            halo = halo_ref[:, cols]
            w = cw_ref[:, cols]
            up = w[0:1] * _shift_down(y, 2, halo) + w[1:2] * _shift_down(y, 1, halo) + w[2:3] * y + cb_ref[:, cols]
            up_ref[:, cols] = up.astype(BF16)
            halo_ref[:, cols] = y[tm - SUBLANES:tm]

    return pl.pallas_call(
        body, name="ffn_up_fwd", grid=(s // tm,),
        out_shape=[jax.ShapeDtypeStruct((s, D_MODEL), BF16), jax.ShapeDtypeStruct((s, n), BF16),
                   jax.ShapeDtypeStruct((s, n), BF16)],
        in_specs=[_rows(tm, D_MODEL), _resident((1, D_MODEL)), _resident((D_MODEL, n)), _resident((3, n)),
                  _resident((1, n))],
        out_specs=[_rows(tm, D_MODEL), _rows(tm, n), _rows(tm, n)],
        scratch_shapes=[pltpu.VMEM((SUBLANES, n), F32)],
        compiler_params=_params(1))(x1, g, w_up, conv_w, conv_b)


def _ffn_act_fwd(x1, up, target, w_down, g_final):
    s = x1.shape[0]
    tm = ROW_TILE

    def body(x1_ref, up_ref, tgt_ref, wd_ref, gf_ref, act_ref, dx2_ref, dx2b_ref, dgf_ref, loss_ref):
        @pl.when(pl.program_id(0) == 0)
        def _():
            dgf_ref[...] = jnp.zeros_like(dgf_ref)
            loss_ref[...] = jnp.zeros_like(loss_ref)

        acc = jnp.zeros((tm, D_MODEL), F32)
        for lo, size in _col_chunks(D_FF, FFN_CHUNK):
            gate = up_ref[:, lo:lo + size].astype(F32)
            val = up_ref[:, D_FF + lo:D_FF + lo + size].astype(F32)
            act = (gate * _sigmoid(gate) * val).astype(BF16)
            act_ref[:, lo:lo + size] = act
            acc = acc + _dot(act, wd_ref[lo:lo + size, :])

        x2 = x1_ref[...] + acc
        r = lax.rsqrt(jnp.mean(x2 * x2, axis=-1, keepdims=True) + EPS)
        xn = x2 * r
        gf = gf_ref[...]
        err = xn * gf - tgt_ref[...]
        loss_ref[...] += (0.5 / D_MODEL) * jnp.sum(err * err)
        dy = err * (1.0 / D_MODEL)
        dgf_ref[...] += _colsum(dy * xn)
        dxn = dy * gf
        dx2 = r * (dxn - xn * jnp.mean(dxn * xn, axis=-1, keepdims=True))
        dx2_ref[...] = dx2
        dx2b_ref[...] = dx2.astype(BF16)

    return pl.pallas_call(
        body, name="ffn_act_fwd", grid=(s // tm,),
        out_shape=[jax.ShapeDtypeStruct((s, D_FF), BF16), jax.ShapeDtypeStruct((s, D_MODEL), F32),
                   jax.ShapeDtypeStruct((s, D_MODEL), BF16),
                   jax.ShapeDtypeStruct((1, D_MODEL), F32), jax.ShapeDtypeStruct((1, LANES), F32)],
        in_specs=[_rows(tm, D_MODEL), _rows(tm, 2 * D_FF), _rows(tm, D_MODEL),
                  _resident((D_FF, D_MODEL)), _resident((1, D_MODEL))],
        out_specs=[_rows(tm, D_FF), _rows(tm, D_MODEL), _rows(tm, D_MODEL),
                   pl.BlockSpec((1, D_MODEL), lambda i: (0, 0)), pl.BlockSpec((1, LANES), lambda i: (0, 0))],
        compiler_params=_params(1))(x1, up, target, w_down, g_final)


def _ffn_act_bwd(dx2b, up, w_down):
    s = dx2b.shape[0]
    tm = ROW_TILE

    def body(dx2_ref, up_ref, wd_ref, dup_ref):
        dx2 = dx2_ref[...]
        for lo, size in _col_chunks(D_FF, FFN_CHUNK):
            gate = up_ref[:, lo:lo + size].astype(F32)
            val = up_ref[:, D_FF + lo:D_FF + lo + size].astype(F32)
            dact = _dot_nt(dx2, wd_ref[lo:lo + size, :])
            sg = _sigmoid(gate)
            dup_ref[:, lo:lo + size] = (dact * val * (sg * (1.0 + gate * (1.0 - sg)))).astype(BF16)
            dup_ref[:, D_FF + lo:D_FF + lo + size] = (dact * (gate * sg)).astype(BF16)

    return pl.pallas_call(
        body, name="ffn_act_bwd", grid=(s // tm,),
        out_shape=jax.ShapeDtypeStruct((s, 2 * D_FF), BF16),
        in_specs=[_rows(tm, D_MODEL), _rows(tm, 2 * D_FF), _resident((D_FF, D_MODEL))],
        out_specs=_rows(tm, 2 * D_FF),
        compiler_params=_params(1))(dx2b, up, w_down)


FFN_BWD_ROW_TILE = 256


def _ffn_up_bwd(d_up, up0, w_up, conv_w, x1, g, dres):
    s = x1.shape[0]
    n = w_up.shape[1]
    tm = FFN_BWD_ROW_TILE
    nt = s // tm

    def body(dup_ref, up0_ref, w_ref, cw_ref, x_ref, g_ref, dres_ref,
             dx_ref, dxb_ref, dg_ref, dup0_ref, small_ref, next_ref):
        @pl.when(pl.program_id(0) == 0)
        def _():
            next_ref[...] = jnp.zeros_like(next_ref)
            small_ref[...] = jnp.zeros_like(small_ref)
            dg_ref[...] = jnp.zeros_like(dg_ref)

        dh = jnp.zeros((tm, D_MODEL), F32)
        for lo, size in _col_chunks(n, FFN_CHUNK):
            cols = slice(lo, lo + size)
            dz = dup_ref[:, cols].astype(F32)
            x0 = up0_ref[:, cols].astype(F32)
            nxt = next_ref[:, cols]
            dz1 = _shift_up(dz, 1, nxt)
            dz2 = _shift_up(dz, 2, nxt)
            next_ref[:, cols] = dz[0:SUBLANES]
            small_ref[0:1, cols] += _colsum(dz2 * x0)
            small_ref[1:2, cols] += _colsum(dz1 * x0)
            small_ref[2:3, cols] += _colsum(dz * x0)
            small_ref[3:4, cols] += _colsum(dz)
            w = cw_ref[:, cols]
            dup0 = (w[2:3] * dz + w[1:2] * dz1 + w[0:1] * dz2).astype(BF16)
            dup0_ref[:, cols] = dup0
            dh = dh + _dot_nt(dup0, w_ref[:, cols])
        xv = x_ref[...]
        r = lax.rsqrt(jnp.mean(xv * xv, axis=-1, keepdims=True) + EPS)
        xn = xv * r
        dg_ref[...] += _colsum(dh * xn)
        dxn = dh * g_ref[...]
        dx = dres_ref[...] + r * (dxn - xn * jnp.mean(dxn * xn, axis=-1, keepdims=True))
        dx_ref[...] = dx
        dxb_ref[...] = dx.astype(BF16)

    rows = lambda width: pl.BlockSpec((tm, width), lambda i: (nt - 1 - i, 0))
    return pl.pallas_call(
        body, name="ffn_up_bwd", grid=(nt,),
        out_shape=[jax.ShapeDtypeStruct((s, D_MODEL), F32), jax.ShapeDtypeStruct((s, D_MODEL), BF16),
                   jax.ShapeDtypeStruct((1, D_MODEL), F32), jax.ShapeDtypeStruct((s, n), BF16),
                   jax.ShapeDtypeStruct((SUBLANES, n), F32)],
        in_specs=[rows(n), rows(n), _resident((D_MODEL, n)), _resident((3, n)), rows(D_MODEL),
                  _resident((1, D_MODEL)), rows(D_MODEL)],
        out_specs=[rows(D_MODEL), rows(D_MODEL), pl.BlockSpec((1, D_MODEL), lambda i: (0, 0)), rows(n),
                   pl.BlockSpec((SUBLANES, n), lambda i: (0, 0))],
        scratch_shapes=[pltpu.VMEM((SUBLANES, n), F32)],
        compiler_params=_params(1))(d_up, up0, w_up, conv_w, x1, g, dres)


def _norm_bwd(name, dys, w, pieces, xin, g, dres, stream_dils=None, w_transposed=False):
    s = xin.shape[0]
    tm = ROW_TILE
    nk = len(dys)
    dils = stream_dils or (1,) * nk
    widths = [dy.shape[-1] for dy in dys]
    relayout = [k for k in range(nk) if dils[k] > 1]
    nt = s // tm

    def body(*refs):
        dy_refs = refs[:nk]
        w_ref, x_ref, g_ref, dres_ref = refs[nk:nk + 4]
        dx_ref, dxb_ref, dg_ref = refs[nk + 4:nk + 7]
        nat_refs = refs[nk + 7:nk + 7 + len(relayout)]
        scr = refs[-1]
        i = pl.program_id(0)

        @pl.when(i == 0)
        def _():
            dg_ref[...] = jnp.zeros_like(dg_ref)

        dh = jnp.zeros((tm, D_MODEL), F32)
        for k in range(nk):
            if dils[k] > 1:
                nat_ref = nat_refs[relayout.index(k)]
                for lo, size in _col_chunks(widths[k], GROUP_WIDTH):
                    nat_ref[:, lo:lo + size] = _from_streams(dy_refs[k], scr, dils[k], lo, size).astype(BF16)
                src = nat_ref
            else:
                src = dy_refs[k]
            for first, width, wcol in pieces[k]:
                for lo, size in _col_chunks(width, 512):
                    dy = src[:, first + lo:first + lo + size]
                    if w_transposed:
                        dh = dh + _dot(dy, w_ref[wcol + lo:wcol + lo + size, :])
                    else:
                        dh = dh + _dot_nt(dy, w_ref[:, wcol + lo:wcol + lo + size])
        xv = x_ref[...]
        r = lax.rsqrt(jnp.mean(xv * xv, axis=-1, keepdims=True) + EPS)
        xn = xv * r
        dg_ref[...] += _colsum(dh * xn)
        dxn = dh * g_ref[...]
        dx = dres_ref[...] + r * (dxn - xn * jnp.mean(dxn * xn, axis=-1, keepdims=True))
        dx_ref[...] = dx
        dxb_ref[...] = dx.astype(BF16)

    dy_specs = [(_stream_block(tm, dils[k], widths[k]) if dils[k] > 1 else _rows(tm, widths[k])) for k in range(nk)]
    outs = [jax.ShapeDtypeStruct((s, D_MODEL), F32), jax.ShapeDtypeStruct((s, D_MODEL), BF16),
            jax.ShapeDtypeStruct((1, D_MODEL), F32)]
    outs += [jax.ShapeDtypeStruct((s, widths[k]), BF16) for k in relayout]
    return pl.pallas_call(
        body, name=name, grid=(nt,), out_shape=outs,
        in_specs=dy_specs + [_resident(w.shape), _rows(tm, D_MODEL), _resident((1, D_MODEL)), _rows(tm, D_MODEL)],
        out_specs=[_rows(tm, D_MODEL), _rows(tm, D_MODEL), pl.BlockSpec((1, D_MODEL), lambda i: (0, 0))]
        + [_rows(tm, widths[k]) for k in relayout],
        scratch_shapes=[pltpu.VMEM((GROUP_WIDTH // LANES * tm, LANES), F32)],
        compiler_params=_params(1))(*dys, w, xin, g, dres)


def _mix_bwd(dx1, abcv, gates, ya, yb, yb0, lsetot, conv_w, conv_b, b_gate, w_pa, w_pb, w_out, exchange=()):
    s = dx1.shape[0]
    tm = ROW_TILE
    nt = s // tm
    hb = tm // (2 * SUBLANES)
    nx = len(exchange)

    def body(*refs):
        (dx1_ref, abcv_ref, pre_ref, gates_ref, ya_ref, yb_ref, yb0_ref, lsetot_ref,
         cw_ref, cb_ref, bg_ref, wpa_ref, wpb_ref, wout_ref) = refs[:14]
        part_refs = refs[14:14 + nx]
        (dya_ref, dyb_ref, dgates_ref, dabcv_ref, dyb0_ref, dyl0_ref, dyl1_ref, dyl2_ref, aux0_ref, aux1_ref,
         aux2_ref, sm_gate_ref, sm_conv_ref) = refs[14 + nx:27 + nx]
        recv_refs = refs[27 + nx:27 + 2 * nx]
        next_ref, scr = refs[27 + 2 * nx:29 + 2 * nx]
        sems = refs[29 + 2 * nx:]
        i = pl.program_id(0)

        @pl.when(i == 0)
        def _():
            next_ref[...] = jnp.zeros_like(next_ref)
            sm_gate_ref[...] = jnp.zeros_like(sm_gate_ref)
            sm_conv_ref[...] = jnp.zeros_like(sm_conv_ref)
            for cp in _chip_exchange_copies(part_refs, recv_refs, *sems) if nx else ():
                cp.start()

        not_first = (i < nt - 1).astype(F32)
        dm = _dot_nt(dx1_ref[...].astype(BF16), wout_ref[...])
        sa = _sigmoid(gates_ref[:, 0:D_MODEL].astype(F32) + bg_ref[0:1, :])
        sb = _sigmoid(gates_ref[:, D_MODEL:2 * D_MODEL].astype(F32) + bg_ref[1:2, :])
        dya = (dm * sa).astype(BF16)
        dyb = (dm * sb).astype(BF16)
        dya_ref[...] = dya
        dyb_ref[...] = dyb
        dga = dm * ya_ref[...].astype(F32) * (sa * (1.0 - sa))
        dgb = dm * yb_ref[...].astype(F32) * (sb * (1.0 - sb))
        dgates_ref[:, 0:D_MODEL] = dga.astype(BF16)
        dgates_ref[:, D_MODEL:2 * D_MODEL] = dgb.astype(BF16)
        sm_gate_ref[0:1, :] += _colsum(dga)
        sm_gate_ref[1:2, :] += _colsum(dgb)

        dya0 = _dot_nt(dya, wpa_ref[...])
        ab = abcv_ref[:, 0:CONV_WIDTH].astype(F32)
        ac = abcv_ref[:, CONV_WIDTH:2 * CONV_WIDTH].astype(F32)
        av = abcv_ref[:, 2 * CONV_WIDTH:3 * CONV_WIDTH].astype(F32)
        pre = pre_ref[...].astype(F32) * not_first
        halo_u = (pre[:, CONV_WIDTH:2 * CONV_WIDTH] * pre[:, 2 * CONV_WIDTH:3 * CONV_WIDTH])[SUBLANES:2 * SUBLANES]
        w = cw_ref[...]
        _, cv, u, sh1, sh2 = _conv_branch(ab, ac, av, halo_u, w, cb_ref[...])
        dcv = dya0 * ab
        sm_conv_ref[0:1, :] += _colsum(dcv * sh2)
        sm_conv_ref[1:2, :] += _colsum(dcv * sh1)
        sm_conv_ref[2:3, :] += _colsum(dcv * u)
        sm_conv_ref[3:4, :] += _colsum(dcv)
        nxt = next_ref[...]
        du = w[2:3] * dcv + w[1:2] * _shift_up(dcv, 1, nxt) + w[0:1] * _shift_up(dcv, 2, nxt)
        next_ref[...] = dcv[0:SUBLANES]
        dabcv_ref[:, 0:CONV_WIDTH] = (dya0 * cv).astype(BF16)
        dabcv_ref[:, CONV_WIDTH:2 * CONV_WIDTH] = (du * av).astype(BF16)
        dabcv_ref[:, 2 * CONV_WIDTH:3 * CONV_WIDTH] = (du * ac).astype(BF16)

        head_r = lax.broadcasted_iota(jnp.int32, (GROUP_WIDTH, GROUP_WIDTH), 0) // HEAD_DIM
        head_c = lax.broadcasted_iota(jnp.int32, (GROUP_WIDTH, GROUP_WIDTH), 1) // HEAD_DIM
        same_head = (head_r == head_c).astype(BF16)
        prod = jnp.zeros((tm, GROUP_WIDTH), F32)
        dyb0s = []
        for g in range(3):
            cols = slice(g * GROUP_WIDTH, (g + 1) * GROUP_WIDTH)
            dyb0 = _dot_nt(dyb, wpb_ref[cols, :])
            dyb0_ref[:, cols] = dyb0.astype(BF16)
            dyb0s.append(dyb0)
            prod = prod + dyb0 * yb0_ref[:, cols].astype(F32)
        hi = prod.astype(BF16)
        mid = (prod - hi.astype(F32)).astype(BF16)
        lo = (prod - hi.astype(F32) - mid.astype(F32)).astype(BF16)
        delta = _dot(hi, same_head) + _dot(mid, same_head) + _dot(lo, same_head)
        lsetot = lsetot_ref[...]
        for g, (dy_ref, aux_ref) in enumerate(zip((dyl0_ref, dyl1_ref, dyl2_ref), (aux0_ref, aux1_ref, aux2_ref))):
            d = DILATIONS[g]
            _to_streams(dyb0s[g], scr, dy_ref, d, 0)
            _to_streams(lsetot, scr, aux_ref, d, 0)
            _to_streams(delta, scr, aux_ref, d, GROUP_WIDTH)

        if nx:
            @pl.when(i == nt - 1)
            def _():
                for cp in _chip_exchange_copies(part_refs, recv_refs, *sems):
                    cp.wait()

    rev = lambda i: (nt - 1 - i, 0)
    pre = lambda i: (jnp.maximum((nt - 1 - i) * hb - 1, 0), 0)
    rows = lambda width: pl.BlockSpec((tm, width), rev)
    outs = [jax.ShapeDtypeStruct((s, D_MODEL), BF16), jax.ShapeDtypeStruct((s, D_MODEL), BF16),
            jax.ShapeDtypeStruct((s, 2 * D_MODEL), BF16), jax.ShapeDtypeStruct((s, 3 * CONV_WIDTH), BF16),
            jax.ShapeDtypeStruct((s, ATTN_WIDTH), BF16)]
    outs += [jax.ShapeDtypeStruct((d, s // d, GROUP_WIDTH), BF16) for d in DILATIONS]
    outs += [jax.ShapeDtypeStruct((d, s // d, 2 * GROUP_WIDTH), F32) for d in DILATIONS]
    outs += [jax.ShapeDtypeStruct((SUBLANES, D_MODEL), F32), jax.ShapeDtypeStruct((SUBLANES, CONV_WIDTH), F32)]
    outs += [jax.ShapeDtypeStruct((3,) + a.shape[1:], a.dtype) for a in exchange]
    return pl.pallas_call(
        body, name="mix_bwd", grid=(nt,), out_shape=outs,
        in_specs=[rows(D_MODEL), rows(3 * CONV_WIDTH), pl.BlockSpec((2 * SUBLANES, 3 * CONV_WIDTH), pre),
                  rows(2 * D_MODEL), rows(D_MODEL), rows(D_MODEL), rows(ATTN_WIDTH), rows(GROUP_WIDTH),
                  _resident((3, CONV_WIDTH)), _resident((1, CONV_WIDTH)), _resident((2, D_MODEL)),
                  _resident((CONV_WIDTH, D_MODEL)), _resident((ATTN_WIDTH, D_MODEL)), _resident((D_MODEL, D_MODEL))]
        + [_ANY] * nx,
        out_specs=[rows(D_MODEL), rows(D_MODEL), rows(2 * D_MODEL), rows(3 * CONV_WIDTH), rows(ATTN_WIDTH)]
        + [_rev_stream_block(tm, d, GROUP_WIDTH, nt) for d in DILATIONS]
        + [_rev_stream_block(tm, d, 2 * GROUP_WIDTH, nt) for d in DILATIONS]
        + [pl.BlockSpec((SUBLANES, D_MODEL), lambda i: (0, 0)), pl.BlockSpec((SUBLANES, CONV_WIDTH), lambda i: (0, 0))]
        + [_ANY] * nx,
        scratch_shapes=[pltpu.VMEM((SUBLANES, CONV_WIDTH), F32),
                        pltpu.VMEM((GROUP_WIDTH // LANES * tm, LANES), F32)] + (_dma_sems(3 * nx) if nx else []),
        compiler_params=_params(1))(dx1, abcv, abcv, gates, ya, yb, yb0, lsetot,
                                    conv_w, conv_b, b_gate, w_pa, w_pb, w_out, *exchange)


def _attn_bwd(qkv, dy, aux, gi, exchange=()):
    d, length, _ = qkv.shape
    nb = length // ATTN_BLOCK
    q = ATTN_BLOCK
    gw = GROUP_WIDTH

    nx = len(exchange)

    def body(*refs):
        q_ref, kp_ref, kc_ref, vp_ref, vc_ref, dy_ref, aux_ref = refs[:7]
        part_refs = refs[7:7 + nx]
        out_ref = refs[7 + nx]
        recv_refs = refs[8 + nx:8 + 2 * nx]
        dq_ref, dkv_ref = refs[8 + 2 * nx:10 + 2 * nx]
        sems = refs[10 + 2 * nx:]
        n = pl.program_id(1)

        if nx:
            @pl.when((pl.program_id(0) == 0) & (n == 0))
            def _():
                for cp in _chip_exchange_copies(part_refs, recv_refs, *sems):
                    cp.start()

            @pl.when((pl.program_id(0) == d - 1) & (n == nb))
            def _():
                for cp in _chip_exchange_copies(part_refs, recv_refs, *sems):
                    cp.wait()

        @pl.when(n > 0)
        def _():
            out_ref[:, 0:gw] = dq_ref[...].astype(BF16)

        @pl.when(n == nb)
        def _():
            out_ref[:, gw:2 * gw] = dkv_ref[0].astype(BF16)
            out_ref[:, 2 * gw:3 * gw] = dkv_ref[1].astype(BF16)

        @pl.when(n < nb)
        def _():
            qs = _stack_heads(q_ref[...])
            dys = _stack_heads(dy_ref[...])
            k2 = jnp.concatenate([kp_ref[...], kc_ref[...]], axis=0)
            v2 = jnp.concatenate([vp_ref[...], vc_ref[...]], axis=0)
            lse = _per_head_col(aux_ref[:, 0:gw])
            delta = _per_head_col(aux_ref[:, gw:2 * gw])
            sc = _dot_nt(qs, k2) * ATTN_SCALE
            p = jnp.where(_band_mask(n == 0), jnp.exp(sc - lse), 0.0)
            dp = _dot_nt(dys, v2)
            ds = (p * (dp - delta) * ATTN_SCALE).astype(BF16)
            dq_ref[...] = _unstack_heads(_dot(ds, k2))
            dk2 = _dot_tn(ds, qs)
            dv2 = _dot_tn(p.astype(BF16), dys)

            @pl.when(n > 0)
            def _():
                out_ref[:, gw:2 * gw] = (dkv_ref[0] + dk2[0:q]).astype(BF16)
                out_ref[:, 2 * gw:3 * gw] = (dkv_ref[1] + dv2[0:q]).astype(BF16)

            dkv_ref[0] = dk2[q:2 * q]
            dkv_ref[1] = dv2[q:2 * q]

    last = nb - 1

    def blk(col, shift, width=gw):
        if shift:
            return pl.BlockSpec((None, q, width), lambda r, n: (r, jnp.maximum(n - 1, 0), col))
        return pl.BlockSpec((None, q, width), lambda r, n: (r, jnp.minimum(n, last), col))

    return pl.pallas_call(
        body, name=f"attn_bwd_g{gi}", grid=(d, nb + 1),
        out_shape=[jax.ShapeDtypeStruct((d, length, 3 * gw), BF16)]
        + [jax.ShapeDtypeStruct((3,) + a.shape[1:], a.dtype) for a in exchange],
        in_specs=[blk(0, False), blk(1, True), blk(1, False), blk(2, True), blk(2, False),
                  blk(0, False), blk(0, False, 2 * gw)] + [_ANY] * nx,
        out_specs=[pl.BlockSpec((None, q, 3 * gw), lambda r, n: (r, jnp.maximum(n - 1, 0), 0))] + [_ANY] * nx,
        scratch_shapes=[pltpu.VMEM((q, gw), F32), pltpu.VMEM((2, q, gw), F32)] + (_dma_sems(3 * nx) if nx else []),
        compiler_params=_params(2))(qkv, qkv, qkv, qkv, qkv, dy, aux, *exchange)


def _matmul_tn(name, a, b, col_tile=1024, row_tile=1024, slabs=0):
    s, k = a.shape
    n = b.shape[1]
    tk = min(row_tile, s)
    tn = col_tile
    steps = s // tk

    def body(a_ref, b_ref, o_ref, acc_ref):
        t = pl.program_id(1)

        @pl.when(t == 0)
        def _():
            acc_ref[...] = jnp.zeros_like(acc_ref)

        acc_ref[...] += _dot_tn(a_ref[...], b_ref[...])

        @pl.when(t == steps - 1)
        def _():
            if slabs:
                for q in range(per_tile):
                    o_ref[q] = acc_ref[:, q * width:(q + 1) * width].astype(BF16)
            else:
                o_ref[...] = acc_ref[...].astype(BF16)

    if slabs:
        width = n // slabs
        per_tile = tn // width
        out_shape = jax.ShapeDtypeStruct((slabs, k, width), BF16)
        out_spec = pl.BlockSpec((per_tile, k, width), lambda j, t: (j, 0, 0))
    else:
        out_shape = jax.ShapeDtypeStruct((k, n), BF16)
        out_spec = pl.BlockSpec((k, tn), lambda j, t: (0, j))
    return pl.pallas_call(
        body, name=name, grid=(n // tn, steps), out_shape=out_shape,
        in_specs=[pl.BlockSpec((tk, k), lambda j, t: (t, 0)), pl.BlockSpec((tk, tn), lambda j, t: (t, j))],
        out_specs=out_spec, scratch_shapes=[pltpu.VMEM((k, tn), F32)],
        compiler_params=_params(2))(a, b)


def _weight_allgather(bigs, smalls):
    nb, ns = len(bigs), len(smalls)
    n_sems = 3 * (2 * nb + ns)

    def body(*refs):
        big_refs, small_refs = refs[:nb], refs[nb:nb + ns]
        big_outs, small_outs = refs[nb + ns:2 * nb + ns], refs[2 * nb + ns:2 * (nb + ns)]
        send_sems, recv_sems = refs[2 * (nb + ns):]
        x, y, c, chips = _mesh_position()
        me = 2 * x + y
        sibling = (x, y, 1 - c)

        def copy(k, src, dst, to):
            return pltpu.make_async_remote_copy(src_ref=src, dst_ref=dst, send_sem=send_sems.at[k],
                                                recv_sem=recv_sems.at[k], device_id=to, device_id_type=MESH_ID)

        halves = [r.shape[1] // 2 for r in big_refs]
        first = []
        for j, (px, py) in enumerate(chips):
            for b in range(nb):
                mine = _half_rows(c, halves[b])
                first.append(copy(3 * b + j, big_refs[b].at[0, mine], big_outs[b].at[me, mine], (px, py, c)))
            for s in range(ns):
                first.append(copy(3 * (2 * nb + s) + j, small_refs[s].at[0], small_outs[s].at[me], (px, py, c)))
        for cp in first:
            cp.start()
        passed = []
        for j, (px, py) in enumerate(chips):
            for b in range(nb):
                landed = big_outs[b].at[2 * px + py, _half_rows(c, halves[b])]
                copy(3 * b + j, landed, landed, (px, py, c)).wait_recv()
                fwd = copy(3 * (nb + b) + j, landed, landed, sibling)
                fwd.start()
                passed.append(fwd)
        for j, (px, py) in enumerate(chips):
            for s in range(ns):
                landed = small_outs[s].at[2 * px + py]
                copy(3 * (2 * nb + s) + j, landed, landed, (px, py, c)).wait_recv()
            for b in range(nb):
                from_sibling = big_outs[b].at[2 * px + py, _half_rows(1 - c, halves[b])]
                copy(3 * (nb + b) + j, from_sibling, from_sibling, sibling).wait_recv()
        for cp in first + passed:
            cp.wait_send()

    return pl.pallas_call(
        body, name="weight_allgather",
        out_shape=[jax.ShapeDtypeStruct((N_CHIPS,) + a.shape[1:], a.dtype) for a in list(bigs) + list(smalls)],
        in_specs=[_ANY] * (nb + ns), out_specs=[_ANY] * (nb + ns),
        scratch_shapes=[pltpu.SemaphoreType.DMA((n_sems,)), pltpu.SemaphoreType.DMA((n_sems,))],
    )(*bigs, *smalls)


def _sibling_swap_halves(name, slabs):
    na = len(slabs)

    def body(*refs):
        src_refs, out_refs = refs[:na], refs[na:2 * na]
        send_sems, recv_sems = refs[2 * na:]
        x, y, c, _ = _mesh_position()
        cps = []
        for a in range(na):
            theirs = _half_rows(1 - c, src_refs[a].shape[1] // 2)
            cps.append(pltpu.make_async_remote_copy(
                src_ref=src_refs[a].at[:, theirs, :], dst_ref=out_refs[a], send_sem=send_sems.at[a],
                recv_sem=recv_sems.at[a], device_id=(x, y, 1 - c), device_id_type=MESH_ID))
        for cp in cps:
            cp.start()
        for cp in cps:
            cp.wait()

    return pl.pallas_call(
        body, name=name,
        out_shape=[jax.ShapeDtypeStruct((a.shape[0], a.shape[1] // 2, a.shape[2]), a.dtype) for a in slabs],
        in_specs=[_ANY] * na, out_specs=[_ANY] * na,
        scratch_shapes=[pltpu.SemaphoreType.DMA((na,)), pltpu.SemaphoreType.DMA((na,))])(*slabs)


def _chip_exchange(partials):
    na = len(partials)

    def body(*refs):
        cps = _chip_exchange_copies(refs[:na], refs[na:2 * na], *refs[2 * na:])
        for cp in cps:
            cp.start()
        for cp in cps:
            cp.wait()

    return pl.pallas_call(
        body, name="grad_chip_exchange",
        out_shape=[jax.ShapeDtypeStruct((3,) + a.shape[1:], a.dtype) for a in partials],
        in_specs=[_ANY] * na, out_specs=[_ANY] * na, scratch_shapes=_dma_sems(3 * na))(*partials)


def _sibling_share(halves):
    na = len(halves)

    def body(*refs):
        out_refs = refs[na:2 * na]
        send_sems, recv_sems = refs[2 * na:]
        x, y, c, _ = _mesh_position()
        cps = []
        for a in range(na):
            mine = out_refs[a].at[0, _half_rows(c, out_refs[a].shape[1] // 2)]
            cps.append(pltpu.make_async_remote_copy(src_ref=mine, dst_ref=mine, send_sem=send_sems.at[a],
                                                    recv_sem=recv_sems.at[a], device_id=(x, y, 1 - c),
                                                    device_id_type=MESH_ID))
        for cp in cps:
            cp.start()
        for a, cp in enumerate(cps):
            cp.wait_send()
            theirs = out_refs[a].at[0, _half_rows(1 - c, out_refs[a].shape[1] // 2)]
            pltpu.make_async_remote_copy(src_ref=theirs, dst_ref=theirs, send_sem=send_sems.at[a],
                                         recv_sem=recv_sems.at[a], device_id=(x, y, 1 - c),
                                         device_id_type=MESH_ID).wait_recv()

    return pl.pallas_call(
        body, name="grad_sibling_share", out_shape=[jax.ShapeDtypeStruct(a.shape, a.dtype) for a in halves],
        in_specs=[_ANY] * na, out_specs=[_ANY] * na, input_output_aliases={a: a for a in range(na)},
        scratch_shapes=[pltpu.SemaphoreType.DMA((na,)), pltpu.SemaphoreType.DMA((na,))])(*halves)


def _add_sibling(name, slab, received, core):
    n, rows, cols = slab.shape
    half = rows // 2

    def body(core_ref, a_ref, b_ref, o_ref):
        o_ref[...] = (a_ref[...].astype(F32) + b_ref[...].astype(F32)).astype(BF16)

    grid_spec = pltpu.PrefetchScalarGridSpec(
        num_scalar_prefetch=1, grid=(n,),
        in_specs=[pl.BlockSpec((None, half, cols), lambda s, core_ref: (s, core_ref[0], 0)),
                  pl.BlockSpec((None, half, cols), lambda s, core_ref: (s, 0, 0))],
        out_specs=pl.BlockSpec((None, half, cols), lambda s, core_ref: (s, 0, 0)))
    return pl.pallas_call(body, name=name, grid_spec=grid_spec,
                          out_shape=jax.ShapeDtypeStruct((n, half, cols), BF16),
                          compiler_params=_params(1))(core, slab, received)


def _sum_chips(name, partial, received, chip_core):
    _, half, cols = partial.shape

    def body(cc_ref, own_ref, recv_ref, o_ref):
        acc = own_ref[...].astype(F32)
        for k in range(3):
            acc = acc + recv_ref[k].astype(F32)
        o_ref[...] = acc

    grid_spec = pltpu.PrefetchScalarGridSpec(
        num_scalar_prefetch=1, grid=(1,),
        in_specs=[pl.BlockSpec((None, half, cols), lambda i, cc_ref: (cc_ref[0], 0, 0)),
                  pl.BlockSpec((3, half, cols), lambda i, cc_ref: (0, 0, 0))],
        out_specs=pl.BlockSpec((None, half, cols), lambda i, cc_ref: (0, cc_ref[1], 0)))
    return pl.pallas_call(body, name=name, grid_spec=grid_spec,
                          out_shape=jax.ShapeDtypeStruct((1, 2 * half, cols), F32),
                          compiler_params=_params(1))(chip_core, partial, received)


def _adam_math(w, g, m, v):
    nm = ADAM_B1 * m + (1.0 - ADAM_B1) * g
    nv = ADAM_B2 * v + (1.0 - ADAM_B2) * jnp.square(g)
    m_hat = nm / (1.0 - ADAM_B1 ** ADAM_STEP)
    v_hat = nv / (1.0 - ADAM_B2 ** ADAM_STEP)
    delta = -ADAM_LR * (m_hat / (jnp.sqrt(v_hat) + ADAM_EPS) + ADAM_WD * w)
    return delta, nm, nv


def _adamw(name, w, g, m, v):
    _, rows, cols = w.shape
    tr = next(t for t in (736, 512, 384, 352, 256, 128, 64, 32, 16, 8) if rows % t == 0)

    def body(w_ref, g_ref, m_ref, v_ref, d_ref, nm_ref, nv_ref):
        d_ref[...], nm_ref[...], nv_ref[...] = _adam_math(w_ref[...], g_ref[...], m_ref[...], v_ref[...])

    spec = pl.BlockSpec((None, tr, cols), lambda i: (0, i, 0))
    return pl.pallas_call(
        body, name=name, grid=(rows // tr,), out_shape=[jax.ShapeDtypeStruct(w.shape, F32)] * 3,
        in_specs=[spec] * 4, out_specs=[spec] * 3, compiler_params=_params(1))(w, g, m, v)


SMALL_PARAMS = ("norm_mix_g", "b_gate", "conv_a_w", "conv_a_b", "norm_ffn_g", "ffn_conv_w", "ffn_conv_b", "final_norm_g")


def _small_update(partials, params, moments_m, moments_v):
    na = len(partials)
    npar = len(SMALL_PARAMS)

    def body(*refs):
        in_refs = refs[:na]
        w_refs = refs[na:na + npar]
        m_refs = refs[na + npar:na + 2 * npar]
        v_refs = refs[na + 2 * npar:na + 3 * npar]
        pos = na + 3 * npar
        loss_ref = refs[pos]
        out_refs = refs[pos + 1:pos + 1 + 4 * npar]
        pos += 1 + 4 * npar
        acc_refs = refs[pos:pos + na]
        recv_refs = refs[pos + na:pos + 4 * na]
        send_sems, recv_sems = refs[pos + 4 * na:]
        x, y, c, _ = _mesh_position()
        chip = 2 * x + y
        for a in range(na):
            acc_refs[a][...] = in_refs[a][...]
        for stage, peer in enumerate(((x, y, 1 - c), (x, 1 - y, c), (1 - x, y, c))):
            cps = []
            for a in range(na):
                k = stage * na + a
                cps.append(pltpu.make_async_remote_copy(src_ref=acc_refs[a], dst_ref=recv_refs[k], send_sem=send_sems.at[k],
                                                        recv_sem=recv_sems.at[k], device_id=peer, device_id_type=MESH_ID))
            for cp in cps:
                cp.start()
            for cp in cps:
                cp.wait()
            for a in range(na):
                acc_refs[a][...] = acc_refs[a][...] + recv_refs[stage * na + a][...]

        mix, ffn, fin, gate, conv, ffnc, loss = acc_refs
        loss_ref[...] = loss[...]

        def cols(width):
            return pl.ds(pl.multiple_of(chip * width, LANES), width)

        grads = {
            "norm_mix_g": mix[...], "norm_ffn_g": ffn[...], "final_norm_g": fin[...],
            "b_gate": gate[0:2, cols(D_MODEL // N_CHIPS)],
            "conv_a_w": conv[0:3, cols(CONV_WIDTH // N_CHIPS)], "conv_a_b": conv[3:4, :],
            "ffn_conv_w": ffnc[0:3, cols(2 * D_FF // N_CHIPS)], "ffn_conv_b": ffnc[3:4, :]}
        for i, name in enumerate(SMALL_PARAMS):
            g = grads[name]
            if len(w_refs[i].shape) == 3:
                results = (g,) + _adam_math(w_refs[i][0], g, m_refs[i][0], v_refs[i][0])
                for o_ref, val in zip(out_refs[4 * i:4 * i + 4], results):
                    o_ref[0] = val
            else:
                results = (g,) + _adam_math(w_refs[i][...], g, m_refs[i][...], v_refs[i][...])
                for o_ref, val in zip(out_refs[4 * i:4 * i + 4], results):
                    o_ref[...] = val

    outs = [jax.ShapeDtypeStruct(partials[-1].shape, F32)]
    for w in params:
        outs += [jax.ShapeDtypeStruct(w.shape, F32)] * 4
    scratch = [pltpu.VMEM(p.shape, F32) for p in partials]
    scratch += [pltpu.VMEM(p.shape, F32) for _ in range(3) for p in partials]
    scratch += [pltpu.SemaphoreType.DMA((3 * na,)), pltpu.SemaphoreType.DMA((3 * na,))]
    n_in = na + 3 * npar
    return pl.pallas_call(
        body, name="small_update", out_shape=outs, in_specs=[_VMEM] * n_in, out_specs=[_VMEM] * len(outs),
        scratch_shapes=scratch)(*partials, *params, *moments_m, *moments_v)


def _gathered_columns(g):
    return jnp.transpose(g, (1, 0, 2)).reshape(g.shape[1], N_CHIPS * g.shape[2])


def _column_slabs(full):
    k, n = full.shape
    return jnp.transpose(full.reshape(k, N_CHIPS, n // N_CHIPS), (1, 0, 2))


def kernel(x, norm_mix_g, w_in, b_gate, conv_a_w, conv_a_b, w_proj_a, w_proj_b, w_out, norm_ffn_g, w_up, ffn_conv_w, ffn_conv_b, w_down, final_norm_g, loss_target, m_norm_mix_g, m_w_in, m_b_gate, m_conv_a_w, m_conv_a_b, m_w_proj_a, m_w_proj_b, m_w_out, m_norm_ffn_g, m_w_up, m_ffn_conv_w, m_ffn_conv_b, m_w_down, m_final_norm_g, v_norm_mix_g, v_w_in, v_b_gate, v_conv_a_w, v_conv_a_b, v_w_proj_a, v_w_proj_b, v_w_out, v_norm_ffn_g, v_w_up, v_ffn_conv_w, v_ffn_conv_b, v_w_down, v_final_norm_g):
    chip = (2 * lax.axis_index("x") + lax.axis_index("y")).astype(jnp.int32)
    core = lax.axis_index("c").astype(jnp.int32)
    core_arr = core.reshape(1)
    chip_core = jnp.stack([chip, core])
    xs, target = x[0], loss_target[0]
    g_final = final_norm_g.reshape(1, D_MODEL)

    def own_slot(gathered, own):
        return lax.dynamic_update_slice(gathered, own, (chip, 0, 0))

    def reduce_to_shards(names, slabs, exchange_in):
        from_sibling = _sibling_swap_halves("grad_swap_" + names[0], slabs)
        partials = [_add_sibling("grad_add_" + n, s, r, core_arr) for n, s, r in zip(names, slabs, from_sibling)]
        received, rest = exchange_in(partials)
        halves = [_sum_chips("grad_sum_" + n, p, r, chip_core) for n, p, r in zip(names, partials, received)]
        return halves, rest

    w_in_t, m_w_in_t, v_w_in_t = (jnp.swapaxes(a, 1, 2) for a in (w_in, m_w_in, v_w_in))
    w_in_tb = w_in_t.astype(BF16)
    (g_in,) = _weight_allgather([w_in_tb], [])
    w_in_full_t = own_slot(g_in, w_in_tb).reshape(D_IN, D_MODEL)
    later_w = [w_proj_a, w_proj_b, w_out, w_up, w_down]
    later_b = [w.astype(BF16) for w in later_w]
    small_sharded = [b_gate, conv_a_w, ffn_conv_w]
    fwd = _inproj_fwd(xs, norm_mix_g, w_in_full_t, later_b, small_sharded)
    h1, abcv, gates, qkv0, qkv1, qkv2 = fwd[:6]
    gathered_big, gathered_small = fwd[6:11], fwd[11:14]
    qkvs = (qkv0, qkv1, qkv2)
    attn0 = _attn_fwd(qkv0, 0, forward=gathered_big)
    attn = [attn0[:2], _attn_fwd(qkv1, 1), _attn_fwd(qkv2, 2)]
    g_pa, g_pb, g_out, g_up, g_down = [own_slot(g, own) for g, own in zip(attn0[2:], later_b)]
    g_bgate, g_convw, g_ffnw = [own_slot(g, own) for g, own in zip(gathered_small, small_sharded)]
    w_pa_full, w_pb_full, w_up_full = _gathered_columns(g_pa), _gathered_columns(g_pb), _gathered_columns(g_up)
    w_out_full, w_down_full = g_out.reshape(D_MODEL, D_MODEL), g_down.reshape(D_FF, D_MODEL)
    b_gate_full, conv_w_full, ffn_w_full = (_gathered_columns(g) for g in (g_bgate, g_convw, g_ffnw))

    x1, ya0, yb0, mrg, ya, yb, lsetot = _mix_fwd(
        xs, abcv, gates, [a[0] for a in attn], [a[1] for a in attn], conv_w_full, conv_a_b, b_gate_full,
        w_pa_full, w_pb_full, w_out_full)
    h2, up0, up = _ffn_up_fwd(x1, norm_ffn_g, w_up_full, ffn_w_full, ffn_conv_b)
    act, dx2, dx2b, d_g_final, loss = _ffn_act_fwd(x1, up, target, w_down_full, g_final)

    d_up = _ffn_act_bwd(dx2b, up, w_down_full)
    d_w_down = _matmul_tn("dw_down", act, dx2b, col_tile=512)
    dx1, dx1b, d_g_ffn, d_up0, ffn_small = _ffn_up_bwd(d_up, up0, w_up_full, ffn_w_full, x1, norm_ffn_g, dx2)
    d_w_up = _matmul_tn("dw_up", h2, d_up0, col_tile=2 * D_FF // N_CHIPS, slabs=N_CHIPS)

    def behind_mix_bwd(partials):
        res = _mix_bwd(dx1, abcv, gates, ya, yb, yb0, lsetot, conv_w_full, conv_a_b, b_gate_full,
                       w_pa_full, w_pb_full, w_out_full, exchange=partials)
        return res[13:], res[:13]

    halves_ffn, mix_res = reduce_to_shards(
        ("w_up", "w_down"), [d_w_up, d_w_down.reshape(N_CHIPS, D_FF // N_CHIPS, D_MODEL)], behind_mix_bwd)
    (d_ya, d_yb, d_gates, d_abcv, d_yb0, dyl0, dyl1, dyl2, aux0, aux1, aux2, gate_small, conv_small) = mix_res
    d_w_out = _matmul_tn("dw_out", mrg, dx1b)
    d_w_pa = _matmul_tn("dw_proj_a", ya0, d_ya, slabs=N_CHIPS)
    d_w_pb = _matmul_tn("dw_proj_b", yb0, d_yb, slabs=N_CHIPS)

    def behind_attn_bwd(partials):
        res = _attn_bwd(qkv0, dyl0, aux0, 0, exchange=partials)
        return res[1:], res[0]

    halves_mix, d_qkv0 = reduce_to_shards(
        ("w_proj_a", "w_proj_b", "w_out"),
        [d_w_pa, d_w_pb, d_w_out.reshape(N_CHIPS, D_MODEL // N_CHIPS, D_MODEL)], behind_attn_bwd)
    (d_qkv1,), (d_qkv2,) = _attn_bwd(qkv1, dyl1, aux1, 1), _attn_bwd(qkv2, dyl2, aux2, 2)

    dq = [d_qkv0[0], d_qkv1, d_qkv2]
    group_pieces = [[(j * GROUP_WIDTH, GROUP_WIDTH, base + g * GROUP_WIDTH) for j, base in enumerate((COL_Q, COL_K, COL_V))]
                    for g in range(3)]
    grad_x, _, d_g_mix, nat1, nat2 = _norm_bwd(
        "inproj_bwd", [d_abcv, d_gates] + dq, w_in_full_t,
        [[(0, 3 * CONV_WIDTH, COL_ABCV)], [(0, 2 * D_MODEL, COL_GATES)]] + group_pieces,
        xs, norm_mix_g, dx1, stream_dils=(1, 1, 1, 4, 16), w_transposed=True)

    d_w_abcv = _matmul_tn("dw_in_abcv", d_abcv, h1)
    d_w_gates = _matmul_tn("dw_in_gates", d_gates, h1)
    d_w_groups = [_matmul_tn(f"dw_in_qkv{g}", t, h1) for g, t in enumerate((dq[0], nat1, nat2))]
    gw = GROUP_WIDTH
    d_w_in_t = jnp.concatenate(
        [d_w_abcv] + [d_w_groups[g][j * gw:(j + 1) * gw] for j in range(3) for g in range(3)] + [d_w_gates], axis=0)
    halves_in, _ = reduce_to_shards(("w_in",), [d_w_in_t.reshape(N_CHIPS, D_IN // N_CHIPS, D_MODEL)],
                                    lambda partials: (_chip_exchange(partials), None))

    big_names = ("w_in", "w_proj_a", "w_proj_b", "w_out", "w_up", "w_down")
    big_grads = _sibling_share(halves_in + halves_mix + halves_ffn)
    big_w = dict(w_in=w_in_t, w_proj_a=w_proj_a, w_proj_b=w_proj_b, w_out=w_out, w_up=w_up, w_down=w_down)
    big_m = dict(w_in=m_w_in_t, w_proj_a=m_w_proj_a, w_proj_b=m_w_proj_b, w_out=m_w_out, w_up=m_w_up, w_down=m_w_down)
    big_v = dict(w_in=v_w_in_t, w_proj_a=v_w_proj_a, w_proj_b=v_w_proj_b, w_out=v_w_out, w_up=v_w_up, w_down=v_w_down)

    fin_w, fin_m, fin_v = (a.reshape(1, D_MODEL) for a in (final_norm_g, m_final_norm_g, v_final_norm_g))
    small_w = [norm_mix_g, b_gate, conv_a_w, conv_a_b, norm_ffn_g, ffn_conv_w, ffn_conv_b, fin_w]
    small_m = [m_norm_mix_g, m_b_gate, m_conv_a_w, m_conv_a_b, m_norm_ffn_g, m_ffn_conv_w, m_ffn_conv_b, fin_m]
    small_v = [v_norm_mix_g, v_b_gate, v_conv_a_w, v_conv_a_b, v_norm_ffn_g, v_ffn_conv_w, v_ffn_conv_b, fin_v]
    small_out = _small_update([d_g_mix, d_g_ffn, d_g_final, gate_small, conv_small, ffn_small, loss],
                              small_w, small_m, small_v)
    total_loss = small_out[0][0, 0]

    grads, delta, new_m, new_v = {}, {}, {}, {}
    for i, n in enumerate(SMALL_PARAMS):
        vals = small_out[1 + 4 * i:5 + 4 * i]
        if n == "final_norm_g":
            vals = [a.reshape(D_MODEL) for a in vals]
        grads[n], delta[n], new_m[n], new_v[n] = vals
    for n, g in zip(big_names, big_grads):
        vals = (g,) + tuple(_adamw("adamw_" + n, big_w[n], g, big_m[n], big_v[n]))
        if n == "w_in":
            vals = [jnp.swapaxes(a, 1, 2) for a in vals]
        grads[n], delta[n], new_m[n], new_v[n] = vals

    names = ["norm_mix_g", "w_in", "b_gate", "conv_a_w", "conv_a_b", "w_proj_a", "w_proj_b", "w_out", "norm_ffn_g", "w_up",
             "ffn_conv_w", "ffn_conv_b", "w_down", "final_norm_g"]
    out = [total_loss, grad_x[None]]
    for group in (grads, delta, new_m, new_v):
        out += [group[n] for n in names]
    return tuple(out)
```

```python
import jax
import jax.numpy as jnp
from jax import lax
from jax.experimental import pallas as pl
from jax.experimental.pallas import tpu as pltpu

F32 = jnp.float32
BF16 = jnp.bfloat16

D_MODEL = 1024
CONV_WIDTH = 512
ATTN_WIDTH = 768
GROUP_WIDTH = 256
HEAD_DIM = 64
HEADS_PER_GROUP = 4
DILATIONS = (1, 4, 16)
ATTN_BLOCK = 128
D_FF = 2816
D_IN = 5888
EPS = 1e-6
NEG_INF = -1e30
ATTN_SCALE = HEAD_DIM ** -0.5

COL_ABCV = 0
COL_Q = 1536
COL_K = 2304
COL_V = 3072
COL_GATES = 3840

ADAM_LR = 0.001
ADAM_B1 = 0.9
ADAM_B2 = 0.999
ADAM_EPS = 1e-08
ADAM_WD = 0.01
ADAM_STEP = 10

LANES = 128
SUBLANES = 8
ROW_TILE = 512
VMEM_LIMIT = 56 * 1024 * 1024

_NT = (((1,), (1,)), ((), ()))
_TN = (((0,), (0,)), ((), ()))


def _params(n_axes, vmem=VMEM_LIMIT):
    return pltpu.CompilerParams(dimension_semantics=("arbitrary",) * n_axes, vmem_limit_bytes=vmem)


def _resident(shape):
    nd = len(shape)
    return pl.BlockSpec(shape, lambda *_: (0,) * nd, pipeline_mode=pl.Buffered(1))


def _rows(tm, width, col_block=0):
    return pl.BlockSpec((tm, width), lambda i: (i, col_block))


def _col_chunks(n, cmax):
    out, lo = [], 0
    while lo < n:
        size = min(cmax, n - lo)
        out.append((lo, size))
        lo += size
    return out


def _dot(a, b):
    return jnp.dot(a, b, preferred_element_type=F32)


def _dot_nt(a, b):
    return lax.dot_general(a, b, _NT, preferred_element_type=F32)


def _dot_tn(a, b):
    return lax.dot_general(a, b, _TN, preferred_element_type=F32)


def _sigmoid(x):
    return 1.0 / (1.0 + jnp.exp(-x))


def _shift_down(v, k, halo8):
    tm = v.shape[0]
    rolled = pltpu.roll(v, k, 0)
    fix = jnp.tile(pltpu.roll(halo8, k, 0), (tm // SUBLANES, 1))
    row = lax.broadcasted_iota(jnp.int32, v.shape, 0)
    return jnp.where(row < k, fix, rolled)


def _shift_up(v, k, halo8):
    tm = v.shape[0]
    rolled = pltpu.roll(v, tm - k, 0)
    fix = jnp.tile(pltpu.roll(halo8, SUBLANES - k, 0), (tm // SUBLANES, 1))
    row = lax.broadcasted_iota(jnp.int32, v.shape, 0)
    return jnp.where(row >= tm - k, fix, rolled)


def _colsum(v):
    return jnp.sum(v, axis=0, keepdims=True)


def _to_streams(val, scr, out_ref, d, col0):
    tm = val.shape[0]
    panels = val.shape[1] // LANES
    for p in range(panels):
        scr[pl.ds(p * tm, tm), :] = val[:, p * LANES:(p + 1) * LANES]
    for r in range(d):
        for p in range(panels):
            piece = scr[pl.ds(p * tm + r, tm // d, stride=d), :]
            out_ref[r, :, col0 + p * LANES: col0 + (p + 1) * LANES] = piece.astype(out_ref.dtype)


def _from_streams(in_ref, scr, d, col0, width):
    panels = width // LANES
    rows = in_ref.shape[1]
    tm = rows * d
    for r in range(d):
        for p in range(panels):
            scr[pl.ds(p * tm + r, rows, stride=d), :] = in_ref[r, :, col0 + p * LANES: col0 + (p + 1) * LANES].astype(F32)
    return jnp.concatenate([scr[pl.ds(p * tm, tm), :] for p in range(panels)], axis=1)


def _stream_block(tm, d, width):
    return pl.BlockSpec((d, tm // d, width), lambda i: (0, i, 0))


def _rev_stream_block(tm, d, width, nt):
    return pl.BlockSpec((d, tm // d, width), lambda i: (0, nt - 1 - i, 0))


N_CHIPS = 4
MESH_ID = pl.DeviceIdType.MESH
_ANY = pl.BlockSpec(memory_space=pl.ANY)
_VMEM = pl.BlockSpec(memory_space=pltpu.VMEM)


def _mesh_position():
    x, y, c = lax.axis_index("x"), lax.axis_index("y"), lax.axis_index("c")
    other_chips = [(1 - x, y), (x, 1 - y), (1 - x, 1 - y)]
    return x, y, c, other_chips


def _half_rows(c, half):
    return pl.ds(pl.multiple_of(c * half, 16), half)


def _remote_copy(k, src, dst, to, send_sems, recv_sems):
    return pltpu.make_async_remote_copy(src_ref=src, dst_ref=dst, send_sem=send_sems.at[k], recv_sem=recv_sems.at[k],
                                        device_id=to, device_id_type=MESH_ID)


def _gather_first_copies(big_refs, small_refs, big_outs, small_outs, send_sems, recv_sems):
    x, y, c, chips = _mesh_position()
    me = 2 * x + y
    nb = len(big_refs)
    cps = []
    for j, (px, py) in enumerate(chips):
        for b in range(nb):
            mine = _half_rows(c, big_refs[b].shape[1] // 2)
            cps.append(_remote_copy(3 * b + j, big_refs[b].at[0, mine], big_outs[b].at[me, mine], (px, py, c),
                                    send_sems, recv_sems))
        for s in range(len(small_refs)):
            cps.append(_remote_copy(3 * (nb + s) + j, small_refs[s].at[0], small_outs[s].at[me], (px, py, c),
                                    send_sems, recv_sems))
    return cps


def _gather_forward_copies(bufs, send_sems, recv_sems):
    x, y, c, chips = _mesh_position()
    cps = []
    for j, (px, py) in enumerate(chips):
        for b in range(len(bufs)):
            landed = bufs[b].at[2 * px + py, _half_rows(c, bufs[b].shape[1] // 2)]
            cps.append(_remote_copy(3 * b + j, landed, landed, (x, y, 1 - c), send_sems, recv_sems))
    return cps


def _chip_exchange_copies(src_refs, out_refs, send_sems, recv_sems):
    x, y, c, chips = _mesh_position()
    cps = []
    for j, (px, py) in enumerate(chips):
        for a in range(len(src_refs)):
            cps.append(_remote_copy(3 * a + j, src_refs[a].at[2 * px + py], out_refs[a].at[j], (px, py, c),
                                    send_sems, recv_sems))
    return cps


def _dma_sems(n):
    return [pltpu.SemaphoreType.DMA((n,)), pltpu.SemaphoreType.DMA((n,))]


def _inproj_fwd(x, g, w_in_t, big_shards, small_shards):
    s = x.shape[0]
    tm = ROW_TILE
    nt = s // tm
    nb, ns = len(big_shards), len(small_shards)
    n_fixed_in, n_fixed_out = 3, 6

    def body(*refs):
        x_ref, g_ref, w_ref = refs[:n_fixed_in]
        shard_refs = refs[n_fixed_in:n_fixed_in + nb + ns]
        pos = n_fixed_in + nb + ns
        h_ref, abcv_ref, gates_ref, qkv0_ref, qkv1_ref, qkv2_ref = refs[pos:pos + n_fixed_out]
        gathered_refs = refs[pos + n_fixed_out:pos + n_fixed_out + nb + ns]
        scr, send_sems, recv_sems = refs[pos + n_fixed_out + nb + ns:]
        i = pl.program_id(0)

        def gather_copies():
            return _gather_first_copies(shard_refs[:nb], shard_refs[nb:], gathered_refs[:nb], gathered_refs[nb:],
                                        send_sems, recv_sems)

        @pl.when(i == 0)
        def _():
            for cp in gather_copies():
                cp.start()

        xv = x_ref[...]
        r = lax.rsqrt(jnp.mean(xv * xv, axis=-1, keepdims=True) + EPS)
        h = (xv * r * g_ref[...]).astype(BF16)
        h_ref[...] = h
        for lo, size in _col_chunks(3 * CONV_WIDTH, 512):
            abcv_ref[:, lo:lo + size] = _dot_nt(h, w_ref[COL_ABCV + lo: COL_ABCV + lo + size, :]).astype(BF16)
        for lo, size in _col_chunks(2 * D_MODEL, 512):
            gates_ref[:, lo:lo + size] = _dot_nt(h, w_ref[COL_GATES + lo: COL_GATES + lo + size, :]).astype(BF16)
        for gi, (d, out_ref) in enumerate(zip(DILATIONS, (qkv0_ref, qkv1_ref, qkv2_ref))):
            for j, base in enumerate((COL_Q, COL_K, COL_V)):
                lo = base + gi * GROUP_WIDTH
                y = _dot_nt(h, w_ref[lo:lo + GROUP_WIDTH, :])
                _to_streams(y, scr, out_ref, d, j * GROUP_WIDTH)

        @pl.when(i == nt - 1)
        def _():
            for cp in gather_copies():
                cp.wait()

    outs = [jax.ShapeDtypeStruct((s, D_MODEL), BF16),
            jax.ShapeDtypeStruct((s, 3 * CONV_WIDTH), BF16),
            jax.ShapeDtypeStruct((s, 2 * D_MODEL), BF16)]
    outs += [jax.ShapeDtypeStruct((d, s // d, 3 * GROUP_WIDTH), BF16) for d in DILATIONS]
    outs += [jax.ShapeDtypeStruct((N_CHIPS,) + a.shape[1:], a.dtype) for a in list(big_shards) + list(small_shards)]
    return pl.pallas_call(
        body, name="inproj_fwd", grid=(nt,), out_shape=outs,
        in_specs=[_rows(tm, D_MODEL), _resident((1, D_MODEL)), _resident((D_IN, D_MODEL))] + [_ANY] * (nb + ns),
        out_specs=[_rows(tm, D_MODEL), _rows(tm, 3 * CONV_WIDTH), _rows(tm, 2 * D_MODEL)]
        + [_stream_block(tm, d, 3 * GROUP_WIDTH) for d in DILATIONS] + [_ANY] * (nb + ns),
        scratch_shapes=[pltpu.VMEM((GROUP_WIDTH // LANES * tm, LANES), F32)] + _dma_sems(3 * (nb + ns)),
        compiler_params=_params(1))(x, g, w_in_t, *big_shards, *small_shards)


def _head_of_lane(shape):
    return lax.broadcasted_iota(jnp.int32, shape, 1) // HEAD_DIM


def _stack_heads(v):
    head = _head_of_lane(v.shape)
    return jnp.concatenate([jnp.where(head == h, v, jnp.zeros_like(v)) for h in range(HEADS_PER_GROUP)], axis=0)


def _unstack_heads(v):
    q = ATTN_BLOCK
    head = _head_of_lane((q, v.shape[1]))
    out = jnp.zeros((q, v.shape[1]), v.dtype)
    for h in range(HEADS_PER_GROUP):
        out = jnp.where(head == h, v[h * q:(h + 1) * q], out)
    return out


def _per_head_rows(col):
    q = ATTN_BLOCK
    head = _head_of_lane((q, GROUP_WIDTH))
    out = jnp.zeros((q, GROUP_WIDTH), col.dtype)
    for h in range(HEADS_PER_GROUP):
        out = jnp.where(head == h, col[h * q:(h + 1) * q], out)
    return out


def _per_head_col(v):
    head = _head_of_lane(v.shape)
    cols = [jnp.max(jnp.where(head == h, v, -jnp.inf), axis=1, keepdims=True) for h in range(HEADS_PER_GROUP)]
    return jnp.concatenate(cols, axis=0)


ATTN_BLOCKS_PER_STEP = 4


def _band_bias(first_block):
    rows = HEADS_PER_GROUP * ATTN_BLOCK
    qi = lax.broadcasted_iota(jnp.int32, (rows, 2 * ATTN_BLOCK), 0) % ATTN_BLOCK
    kj = lax.broadcasted_iota(jnp.int32, (rows, 2 * ATTN_BLOCK), 1)
    dist = qi + ATTN_BLOCK - kj
    valid = (dist >= 0) & (dist <= ATTN_BLOCK)
    if first_block:
        valid = valid & (kj >= ATTN_BLOCK)
    return jnp.where(valid, 0.0, NEG_INF).astype(F32)


def _store_band_biases(bias_ref):
    bias_ref[0] = _band_bias(False)
    bias_ref[1] = _band_bias(True)


def _attn_block_specs(g, nb, clamp_last=False):
    q = ATTN_BLOCK
    last = nb // g - 1

    def cur(col, width=GROUP_WIDTH):
        if clamp_last:
            return pl.BlockSpec((None, g * q, width), lambda r, n: (r, jnp.minimum(n, last), col))
        return pl.BlockSpec((None, g * q, width), lambda r, n: (r, n, col))

    def prev(col):
        if clamp_last:
            return pl.BlockSpec((None, q, GROUP_WIDTH), lambda r, n: (r, jnp.clip(n * g - 1, 0, nb - 1), col))
        return pl.BlockSpec((None, q, GROUP_WIDTH), lambda r, n: (r, jnp.maximum(n * g - 1, 0), col))

    return cur, prev


def _attn_fwd(qkv, gi, forward=()):
    d, length, _ = qkv.shape
    nb = length // ATTN_BLOCK
    q = ATTN_BLOCK
    g = min(ATTN_BLOCKS_PER_STEP, nb)
    ns = nb // g
    nf = len(forward)

    def body(*refs):
        q_ref, kp_ref, kc_ref, vp_ref, vc_ref = refs[:5]
        o_ref, lse_ref = refs[5 + nf:7 + nf]
        buf_refs = refs[7 + nf:7 + 2 * nf]
        bias_ref = refs[7 + 2 * nf]
        sems = refs[8 + 2 * nf:]
        n = pl.program_id(1)
        first_step = (pl.program_id(0) == 0) & (n == 0)
        last_step = (pl.program_id(0) == d - 1) & (n == ns - 1)

        @pl.when(first_step)
        def _():
            _store_band_biases(bias_ref)
            for cp in _gather_forward_copies(buf_refs, *sems) if nf else ():
                cp.start()

        kfull = jnp.concatenate([kp_ref[...], kc_ref[...]], axis=0)
        vfull = jnp.concatenate([vp_ref[...], vc_ref[...]], axis=0)
        for j in range(g):
            qs = _stack_heads(q_ref[j * q:(j + 1) * q, :])
            k2 = kfull[j * q:(j + 2) * q]
            v2 = vfull[j * q:(j + 2) * q]
            bias = jnp.where(n == 0, bias_ref[1], bias_ref[0]) if j == 0 else bias_ref[0]
            sc = _dot_nt(qs, k2) * ATTN_SCALE + bias
            m = jnp.max(sc, axis=1, keepdims=True)
            p = jnp.exp(sc - m)
            l = jnp.sum(p, axis=1, keepdims=True)
            of = _dot(p.astype(BF16), v2) / l
            o_ref[j * q:(j + 1) * q, :] = _unstack_heads(of).astype(BF16)
            lse_ref[j * q:(j + 1) * q, :] = _per_head_rows(m + jnp.log(l))

        if nf:
            @pl.when(last_step)
            def _():
                for cp in _gather_forward_copies(buf_refs, *sems):
                    cp.wait()

    cur, prev = _attn_block_specs(g, nb)
    return pl.pallas_call(
        body, name=f"attn_fwd_g{gi}", grid=(d, ns),
        out_shape=[jax.ShapeDtypeStruct((d, length, GROUP_WIDTH), BF16),
                   jax.ShapeDtypeStruct((d, length, GROUP_WIDTH), F32)]
        + [jax.ShapeDtypeStruct(a.shape, a.dtype) for a in forward],
        in_specs=[cur(0), prev(1), cur(1), prev(2), cur(2)] + [_ANY] * nf,
        out_specs=[cur(0), cur(0)] + [_ANY] * nf,
        input_output_aliases={5 + a: 2 + a for a in range(nf)},
        scratch_shapes=[pltpu.VMEM((2, HEADS_PER_GROUP * q, 2 * q), F32)] + (_dma_sems(3 * nf) if nf else []),
        compiler_params=_params(2))(qkv, qkv, qkv, qkv, qkv, *forward)


def _conv_branch(ab, ac, av, halo_u, w, b):
    u = ac * av
    sh1 = _shift_down(u, 1, halo_u)
    sh2 = _shift_down(u, 2, halo_u)
    cv = w[0:1] * sh2 + w[1:2] * sh1 + w[2:3] * u + b
    return ab * cv, cv, u, sh1, sh2


def _mix_fwd(x, abcv, gates, o_list, lse_list, conv_w, conv_b, b_gate, w_pa, w_pb, w_out):
    s = x.shape[0]
    tm = ROW_TILE

    def body(x_ref, abcv_ref, gates_ref, o0_ref, o1_ref, o2_ref, l0_ref, l1_ref, l2_ref,
             cw_ref, cb_ref, bg_ref, wpa_ref, wpb_ref, wout_ref,
             x1_ref, ya0_ref, yb0_ref, mrg_ref, ya_ref, yb_ref, lsetot_ref, halo_ref, scr):
        i = pl.program_id(0)

        @pl.when(i == 0)
        def _():
            halo_ref[...] = jnp.zeros_like(halo_ref)

        ab = abcv_ref[:, 0:CONV_WIDTH].astype(F32)
        ac = abcv_ref[:, CONV_WIDTH:2 * CONV_WIDTH].astype(F32)
        av = abcv_ref[:, 2 * CONV_WIDTH:3 * CONV_WIDTH].astype(F32)
        ya0, _, u, _, _ = _conv_branch(ab, ac, av, halo_ref[...], cw_ref[...], cb_ref[...])
        halo_ref[...] = u[tm - SUBLANES:tm]
        ya0 = ya0.astype(BF16)
        ya0_ref[...] = ya0
        ya = _dot(ya0, wpa_ref[...])

        o_refs, l_refs = (o0_ref, o1_ref, o2_ref), (l0_ref, l1_ref, l2_ref)
        lses = [_from_streams(l_refs[g], scr, DILATIONS[g], 0, GROUP_WIDTH) for g in range(3)]
        top = jnp.maximum(jnp.maximum(lses[0], lses[1]), lses[2])
        lsetot = top + jnp.log(jnp.exp(lses[0] - top) + jnp.exp(lses[1] - top) + jnp.exp(lses[2] - top))
        lsetot_ref[...] = lsetot
        yb = jnp.zeros((tm, D_MODEL), F32)
        for g in range(3):
            og = _from_streams(o_refs[g], scr, DILATIONS[g], 0, GROUP_WIDTH)
            yb0 = (jnp.exp(lses[g] - lsetot) * og).astype(BF16)
            yb0_ref[:, g * GROUP_WIDTH:(g + 1) * GROUP_WIDTH] = yb0
            yb = yb + _dot(yb0, wpb_ref[g * GROUP_WIDTH:(g + 1) * GROUP_WIDTH, :])

        sa = _sigmoid(gates_ref[:, 0:D_MODEL].astype(F32) + bg_ref[0:1, :])
        sb = _sigmoid(gates_ref[:, D_MODEL:2 * D_MODEL].astype(F32) + bg_ref[1:2, :])
        ya_ref[...] = ya.astype(BF16)
        yb_ref[...] = yb.astype(BF16)
        mrg = (sa * ya + sb * yb).astype(BF16)
        mrg_ref[...] = mrg
        x1_ref[...] = x_ref[...] + _dot(mrg, wout_ref[...])

    outs = [jax.ShapeDtypeStruct((s, D_MODEL), F32),
            jax.ShapeDtypeStruct((s, CONV_WIDTH), BF16),
            jax.ShapeDtypeStruct((s, ATTN_WIDTH), BF16),
            jax.ShapeDtypeStruct((s, D_MODEL), BF16),
            jax.ShapeDtypeStruct((s, D_MODEL), BF16),
            jax.ShapeDtypeStruct((s, D_MODEL), BF16),
            jax.ShapeDtypeStruct((s, GROUP_WIDTH), F32)]
    return pl.pallas_call(
        body, name="mix_fwd", grid=(s // tm,), out_shape=outs,
        in_specs=[_rows(tm, D_MODEL), _rows(tm, 3 * CONV_WIDTH), _rows(tm, 2 * D_MODEL)]
        + [_stream_block(tm, d, GROUP_WIDTH) for d in DILATIONS] * 2
        + [_resident((3, CONV_WIDTH)), _resident((1, CONV_WIDTH)), _resident((2, D_MODEL)),
           _resident((CONV_WIDTH, D_MODEL)), _resident((ATTN_WIDTH, D_MODEL)), _resident((D_MODEL, D_MODEL))],
        out_specs=[_rows(tm, D_MODEL), _rows(tm, CONV_WIDTH), _rows(tm, ATTN_WIDTH), _rows(tm, D_MODEL),
                   _rows(tm, D_MODEL), _rows(tm, D_MODEL), _rows(tm, GROUP_WIDTH)],
        scratch_shapes=[pltpu.VMEM((SUBLANES, CONV_WIDTH), F32),
                        pltpu.VMEM((GROUP_WIDTH // LANES * tm, LANES), F32)],
        compiler_params=_params(1))(x, abcv, gates, *o_list, *lse_list, conv_w, conv_b, b_gate, w_pa, w_pb, w_out)


FFN_CHUNK = 512


def _ffn_up_fwd(x1, g, w_up, conv_w, conv_b):
    s = x1.shape[0]
    n = w_up.shape[1]
    tm = ROW_TILE

    def body(x_ref, g_ref, w_ref, cw_ref, cb_ref, h_ref, up0_ref, up_ref, halo_ref):
        @pl.when(pl.program_id(0) == 0)
        def _():
            halo_ref[...] = jnp.zeros_like(halo_ref)

        xv = x_ref[...]
        r = lax.rsqrt(jnp.mean(xv * xv, axis=-1, keepdims=True) + EPS)
        h = (xv * r * g_ref[...]).astype(BF16)
        h_ref[...] = h
        for lo, size in _col_chunks(n, FFN_CHUNK):
            cols = slice(lo, lo + size)
            y = _dot(h, w_ref[:, cols])
            up0_ref[:, cols] = y.astype(BF16)
            halo = halo_ref[:, cols]
            w = cw_ref[:, cols]
            up = w[0:1] * _shift_down(y, 2, halo) + w[1:2] * _shift_down(y, 1, halo) + w[2:3] * y + cb_ref[:, cols]
            up_ref[:, cols] = up.astype(BF16)
            halo_ref[:, cols] = y[tm - SUBLANES:tm]

    return pl.pallas_call(
        body, name="ffn_up_fwd", grid=(s // tm,),
        out_shape=[jax.ShapeDtypeStruct((s, D_MODEL), BF16), jax.ShapeDtypeStruct((s, n), BF16),
                   jax.ShapeDtypeStruct((s, n), BF16)],
        in_specs=[_rows(tm, D_MODEL), _resident((1, D_MODEL)), _resident((D_MODEL, n)), _resident((3, n)),
                  _resident((1, n))],
        out_specs=[_rows(tm, D_MODEL), _rows(tm, n), _rows(tm, n)],
        scratch_shapes=[pltpu.VMEM((SUBLANES, n), F32)],
        compiler_params=_params(1))(x1, g, w_up, conv_w, conv_b)


def _ffn_act_fwd(x1, up, target, w_down, g_final):
    s = x1.shape[0]
    tm = ROW_TILE

    def body(x1_ref, up_ref, tgt_ref, wd_ref, gf_ref, act_ref, dx2_ref, dx2b_ref, dgf_ref, loss_ref):
        @pl.when(pl.program_id(0) == 0)
        def _():
            dgf_ref[...] = jnp.zeros_like(dgf_ref)
            loss_ref[...] = jnp.zeros_like(loss_ref)

        acc = jnp.zeros((tm, D_MODEL), F32)
        for lo, size in _col_chunks(D_FF, FFN_CHUNK):
            gate = up_ref[:, lo:lo + size].astype(F32)
            val = up_ref[:, D_FF + lo:D_FF + lo + size].astype(F32)
            act = (gate * _sigmoid(gate) * val).astype(BF16)
            act_ref[:, lo:lo + size] = act
            acc = acc + _dot(act, wd_ref[lo:lo + size, :])

        x2 = x1_ref[...] + acc
        r = lax.rsqrt(jnp.mean(x2 * x2, axis=-1, keepdims=True) + EPS)
        xn = x2 * r
        gf = gf_ref[...]
        err = xn * gf - tgt_ref[...]
        loss_ref[...] += (0.5 / D_MODEL) * jnp.sum(err * err)
        dy = err * (1.0 / D_MODEL)
        dgf_ref[...] += _colsum(dy * xn)
        dxn = dy * gf
        dx2 = r * (dxn - xn * jnp.mean(dxn * xn, axis=-1, keepdims=True))
        dx2_ref[...] = dx2
        dx2b_ref[...] = dx2.astype(BF16)

    return pl.pallas_call(
        body, name="ffn_act_fwd", grid=(s // tm,),
        out_shape=[jax.ShapeDtypeStruct((s, D_FF), BF16), jax.ShapeDtypeStruct((s, D_MODEL), F32),
                   jax.ShapeDtypeStruct((s, D_MODEL), BF16),
                   jax.ShapeDtypeStruct((1, D_MODEL), F32), jax.ShapeDtypeStruct((1, LANES), F32)],
        in_specs=[_rows(tm, D_MODEL), _rows(tm, 2 * D_FF), _rows(tm, D_MODEL),
                  _resident((D_FF, D_MODEL)), _resident((1, D_MODEL))],
        out_specs=[_rows(tm, D_FF), _rows(tm, D_MODEL), _rows(tm, D_MODEL),
                   pl.BlockSpec((1, D_MODEL), lambda i: (0, 0)), pl.BlockSpec((1, LANES), lambda i: (0, 0))],
        compiler_params=_params(1))(x1, up, target, w_down, g_final)


def _ffn_act_bwd(dx2b, up, w_down):
    s = dx2b.shape[0]
    tm = ROW_TILE

    def body(dx2_ref, up_ref, wd_ref, dup_ref):
        dx2 = dx2_ref[...]
        for lo, size in _col_chunks(D_FF, FFN_CHUNK):
            gate = up_ref[:, lo:lo + size].astype(F32)
            val = up_ref[:, D_FF + lo:D_FF + lo + size].astype(F32)
            dact = _dot_nt(dx2, wd_ref[lo:lo + size, :])
            sg = _sigmoid(gate)
            dup_ref[:, lo:lo + size] = (dact * val * (sg * (1.0 + gate * (1.0 - sg)))).astype(BF16)
            dup_ref[:, D_FF + lo:D_FF + lo + size] = (dact * (gate * sg)).astype(BF16)

    return pl.pallas_call(
        body, name="ffn_act_bwd", grid=(s // tm,),
        out_shape=jax.ShapeDtypeStruct((s, 2 * D_FF), BF16),
        in_specs=[_rows(tm, D_MODEL), _rows(tm, 2 * D_FF), _resident((D_FF, D_MODEL))],
        out_specs=_rows(tm, 2 * D_FF),
        compiler_params=_params(1))(dx2b, up, w_down)


FFN_BWD_ROW_TILE = 256


def _ffn_up_bwd(d_up, up0, w_up, conv_w, x1, g, dres):
    s = x1.shape[0]
    n = w_up.shape[1]
    tm = FFN_BWD_ROW_TILE
    nt = s // tm

    def body(dup_ref, up0_ref, w_ref, cw_ref, x_ref, g_ref, dres_ref,
             dx_ref, dxb_ref, dg_ref, dup0_ref, small_ref, next_ref):
        @pl.when(pl.program_id(0) == 0)
        def _():
            next_ref[...] = jnp.zeros_like(next_ref)
            small_ref[...] = jnp.zeros_like(small_ref)
            dg_ref[...] = jnp.zeros_like(dg_ref)

        dh = jnp.zeros((tm, D_MODEL), F32)
        for lo, size in _col_chunks(n, FFN_CHUNK):
            cols = slice(lo, lo + size)
            dz = dup_ref[:, cols].astype(F32)
            x0 = up0_ref[:, cols].astype(F32)
            nxt = next_ref[:, cols]
            dz1 = _shift_up(dz, 1, nxt)
            dz2 = _shift_up(dz, 2, nxt)
            next_ref[:, cols] = dz[0:SUBLANES]
            small_ref[0:1, cols] += _colsum(dz2 * x0)
            small_ref[1:2, cols] += _colsum(dz1 * x0)
            small_ref[2:3, cols] += _colsum(dz * x0)
            small_ref[3:4, cols] += _colsum(dz)
            w = cw_ref[:, cols]
            dup0 = (w[2:3] * dz + w[1:2] * dz1 + w[0:1] * dz2).astype(BF16)
            dup0_ref[:, cols] = dup0
            dh = dh + _dot_nt(dup0, w_ref[:, cols])
        xv = x_ref[...]
        r = lax.rsqrt(jnp.mean(xv * xv, axis=-1, keepdims=True) + EPS)
        xn = xv * r
        dg_ref[...] += _colsum(dh * xn)
        dxn = dh * g_ref[...]
        dx = dres_ref[...] + r * (dxn - xn * jnp.mean(dxn * xn, axis=-1, keepdims=True))
        dx_ref[...] = dx
        dxb_ref[...] = dx.astype(BF16)

    rows = lambda width: pl.BlockSpec((tm, width), lambda i: (nt - 1 - i, 0))
    return pl.pallas_call(
        body, name="ffn_up_bwd", grid=(nt,),
        out_shape=[jax.ShapeDtypeStruct((s, D_MODEL), F32), jax.ShapeDtypeStruct((s, D_MODEL), BF16),
                   jax.ShapeDtypeStruct((1, D_MODEL), F32), jax.ShapeDtypeStruct((s, n), BF16),
                   jax.ShapeDtypeStruct((SUBLANES, n), F32)],
        in_specs=[rows(n), rows(n), _resident((D_MODEL, n)), _resident((3, n)), rows(D_MODEL),
                  _resident((1, D_MODEL)), rows(D_MODEL)],
        out_specs=[rows(D_MODEL), rows(D_MODEL), pl.BlockSpec((1, D_MODEL), lambda i: (0, 0)), rows(n),
                   pl.BlockSpec((SUBLANES, n), lambda i: (0, 0))],
        scratch_shapes=[pltpu.VMEM((SUBLANES, n), F32)],
        compiler_params=_params(1))(d_up, up0, w_up, conv_w, x1, g, dres)


def _norm_bwd(name, dys, w, pieces, xin, g, dres, stream_dils=None, w_transposed=False):
    s = xin.shape[0]
    tm = ROW_TILE
    nk = len(dys)
    dils = stream_dils or (1,) * nk
    widths = [dy.shape[-1] for dy in dys]
    relayout = [k for k in range(nk) if dils[k] > 1]
    nt = s // tm

    def body(*refs):
        dy_refs = refs[:nk]
        w_ref, x_ref, g_ref, dres_ref = refs[nk:nk + 4]
        dx_ref, dxb_ref, dg_ref = refs[nk + 4:nk + 7]
        nat_refs = refs[nk + 7:nk + 7 + len(relayout)]
        scr = refs[-1]
        i = pl.program_id(0)

        @pl.when(i == 0)
        def _():
            dg_ref[...] = jnp.zeros_like(dg_ref)

        dh = jnp.zeros((tm, D_MODEL), F32)
        for k in range(nk):
            if dils[k] > 1:
                nat_ref = nat_refs[relayout.index(k)]
                for lo, size in _col_chunks(widths[k], GROUP_WIDTH):
                    nat_ref[:, lo:lo + size] = _from_streams(dy_refs[k], scr, dils[k], lo, size).astype(BF16)
                src = nat_ref
            else:
                src = dy_refs[k]
            for first, width, wcol in pieces[k]:
                for lo, size in _col_chunks(width, 512):
                    dy = src[:, first + lo:first + lo + size]
                    if w_transposed:
                        dh = dh + _dot(dy, w_ref[wcol + lo:wcol + lo + size, :])
                    else:
                        dh = dh + _dot_nt(dy, w_ref[:, wcol + lo:wcol + lo + size])
        xv = x_ref[...]
        r = lax.rsqrt(jnp.mean(xv * xv, axis=-1, keepdims=True) + EPS)
        xn = xv * r
        dg_ref[...] += _colsum(dh * xn)
        dxn = dh * g_ref[...]
        dx = dres_ref[...] + r * (dxn - xn * jnp.mean(dxn * xn, axis=-1, keepdims=True))
        dx_ref[...] = dx
        dxb_ref[...] = dx.astype(BF16)

    dy_specs = [(_stream_block(tm, dils[k], widths[k]) if dils[k] > 1 else _rows(tm, widths[k])) for k in range(nk)]
    outs = [jax.ShapeDtypeStruct((s, D_MODEL), F32), jax.ShapeDtypeStruct((s, D_MODEL), BF16),
            jax.ShapeDtypeStruct((1, D_MODEL), F32)]
    outs += [jax.ShapeDtypeStruct((s, widths[k]), BF16) for k in relayout]
    return pl.pallas_call(
        body, name=name, grid=(nt,), out_shape=outs,
        in_specs=dy_specs + [_resident(w.shape), _rows(tm, D_MODEL), _resident((1, D_MODEL)), _rows(tm, D_MODEL)],
        out_specs=[_rows(tm, D_MODEL), _rows(tm, D_MODEL), pl.BlockSpec((1, D_MODEL), lambda i: (0, 0))]
        + [_rows(tm, widths[k]) for k in relayout],
        scratch_shapes=[pltpu.VMEM((GROUP_WIDTH // LANES * tm, LANES), F32)],
        compiler_params=_params(1))(*dys, w, xin, g, dres)


def _mix_bwd(dx1, abcv, gates, ya, yb, yb0, lsetot, conv_w, conv_b, b_gate, w_pa, w_pb, w_out, exchange=()):
    s = dx1.shape[0]
    tm = ROW_TILE
    nt = s // tm
    hb = tm // (2 * SUBLANES)
    nx = len(exchange)

    def body(*refs):
        (dx1_ref, abcv_ref, pre_ref, gates_ref, ya_ref, yb_ref, yb0_ref, lsetot_ref,
         cw_ref, cb_ref, bg_ref, wpa_ref, wpb_ref, wout_ref) = refs[:14]
        part_refs = refs[14:14 + nx]
        (dya_ref, dyb_ref, dgates_ref, dabcv_ref, dyb0_ref, dyl0_ref, dyl1_ref, dyl2_ref, aux0_ref, aux1_ref,
         aux2_ref, sm_gate_ref, sm_conv_ref) = refs[14 + nx:27 + nx]
        recv_refs = refs[27 + nx:27 + 2 * nx]
        next_ref, scr = refs[27 + 2 * nx:29 + 2 * nx]
        sems = refs[29 + 2 * nx:]
        i = pl.program_id(0)

        @pl.when(i == 0)
        def _():
            next_ref[...] = jnp.zeros_like(next_ref)
            sm_gate_ref[...] = jnp.zeros_like(sm_gate_ref)
            sm_conv_ref[...] = jnp.zeros_like(sm_conv_ref)
            for cp in _chip_exchange_copies(part_refs, recv_refs, *sems) if nx else ():
                cp.start()

        not_first = (i < nt - 1).astype(F32)
        dm = _dot_nt(dx1_ref[...].astype(BF16), wout_ref[...])
        sa = _sigmoid(gates_ref[:, 0:D_MODEL].astype(F32) + bg_ref[0:1, :])
        sb = _sigmoid(gates_ref[:, D_MODEL:2 * D_MODEL].astype(F32) + bg_ref[1:2, :])
        dya = (dm * sa).astype(BF16)
        dyb = (dm * sb).astype(BF16)
        dya_ref[...] = dya
        dyb_ref[...] = dyb
        dga = dm * ya_ref[...].astype(F32) * (sa * (1.0 - sa))
        dgb = dm * yb_ref[...].astype(F32) * (sb * (1.0 - sb))
        dgates_ref[:, 0:D_MODEL] = dga.astype(BF16)
        dgates_ref[:, D_MODEL:2 * D_MODEL] = dgb.astype(BF16)
        sm_gate_ref[0:1, :] += _colsum(dga)
        sm_gate_ref[1:2, :] += _colsum(dgb)

        dya0 = _dot_nt(dya, wpa_ref[...])
        ab = abcv_ref[:, 0:CONV_WIDTH].astype(F32)
        ac = abcv_ref[:, CONV_WIDTH:2 * CONV_WIDTH].astype(F32)
        av = abcv_ref[:, 2 * CONV_WIDTH:3 * CONV_WIDTH].astype(F32)
        pre = pre_ref[...].astype(F32) * not_first
        halo_u = (pre[:, CONV_WIDTH:2 * CONV_WIDTH] * pre[:, 2 * CONV_WIDTH:3 * CONV_WIDTH])[SUBLANES:2 * SUBLANES]
        w = cw_ref[...]
        _, cv, u, sh1, sh2 = _conv_branch(ab, ac, av, halo_u, w, cb_ref[...])
        dcv = dya0 * ab
        sm_conv_ref[0:1, :] += _colsum(dcv * sh2)
        sm_conv_ref[1:2, :] += _colsum(dcv * sh1)
        sm_conv_ref[2:3, :] += _colsum(dcv * u)
        sm_conv_ref[3:4, :] += _colsum(dcv)
        nxt = next_ref[...]
        du = w[2:3] * dcv + w[1:2] * _shift_up(dcv, 1, nxt) + w[0:1] * _shift_up(dcv, 2, nxt)
        next_ref[...] = dcv[0:SUBLANES]
        dabcv_ref[:, 0:CONV_WIDTH] = (dya0 * cv).astype(BF16)
        dabcv_ref[:, CONV_WIDTH:2 * CONV_WIDTH] = (du * av).astype(BF16)
        dabcv_ref[:, 2 * CONV_WIDTH:3 * CONV_WIDTH] = (du * ac).astype(BF16)

        head_r = lax.broadcasted_iota(jnp.int32, (GROUP_WIDTH, GROUP_WIDTH), 0) // HEAD_DIM
        head_c = lax.broadcasted_iota(jnp.int32, (GROUP_WIDTH, GROUP_WIDTH), 1) // HEAD_DIM
        same_head = (head_r == head_c).astype(BF16)
        prod = jnp.zeros((tm, GROUP_WIDTH), F32)
        dyb0s = []
        for g in range(3):
            cols = slice(g * GROUP_WIDTH, (g + 1) * GROUP_WIDTH)
            dyb0 = _dot_nt(dyb, wpb_ref[cols, :])
            dyb0_ref[:, cols] = dyb0.astype(BF16)
            dyb0s.append(dyb0)
            prod = prod + dyb0 * yb0_ref[:, cols].astype(F32)
        hi = prod.astype(BF16)
        mid = (prod - hi.astype(F32)).astype(BF16)
        lo = (prod - hi.astype(F32) - mid.astype(F32)).astype(BF16)
        delta = _dot(hi, same_head) + _dot(mid, same_head) + _dot(lo, same_head)
        lsetot = lsetot_ref[...]
        for g, (dy_ref, aux_ref) in enumerate(zip((dyl0_ref, dyl1_ref, dyl2_ref), (aux0_ref, aux1_ref, aux2_ref))):
            d = DILATIONS[g]
            _to_streams(dyb0s[g], scr, dy_ref, d, 0)
            _to_streams(lsetot, scr, aux_ref, d, 0)
            _to_streams(delta, scr, aux_ref, d, GROUP_WIDTH)

        if nx:
            @pl.when(i == nt - 1)
            def _():
                for cp in _chip_exchange_copies(part_refs, recv_refs, *sems):
                    cp.wait()

    rev = lambda i: (nt - 1 - i, 0)
    pre = lambda i: (jnp.maximum((nt - 1 - i) * hb - 1, 0), 0)
    rows = lambda width: pl.BlockSpec((tm, width), rev)
    outs = [jax.ShapeDtypeStruct((s, D_MODEL), BF16), jax.ShapeDtypeStruct((s, D_MODEL), BF16),
            jax.ShapeDtypeStruct((s, 2 * D_MODEL), BF16), jax.ShapeDtypeStruct((s, 3 * CONV_WIDTH), BF16),
            jax.ShapeDtypeStruct((s, ATTN_WIDTH), BF16)]
    outs += [jax.ShapeDtypeStruct((d, s // d, GROUP_WIDTH), BF16) for d in DILATIONS]
    outs += [jax.ShapeDtypeStruct((d, s // d, 2 * GROUP_WIDTH), F32) for d in DILATIONS]
    outs += [jax.ShapeDtypeStruct((SUBLANES, D_MODEL), F32), jax.ShapeDtypeStruct((SUBLANES, CONV_WIDTH), F32)]
    outs += [jax.ShapeDtypeStruct((3,) + a.shape[1:], a.dtype) for a in exchange]
    return pl.pallas_call(
        body, name="mix_bwd", grid=(nt,), out_shape=outs,
        in_specs=[rows(D_MODEL), rows(3 * CONV_WIDTH), pl.BlockSpec((2 * SUBLANES, 3 * CONV_WIDTH), pre),
                  rows(2 * D_MODEL), rows(D_MODEL), rows(D_MODEL), rows(ATTN_WIDTH), rows(GROUP_WIDTH),
                  _resident((3, CONV_WIDTH)), _resident((1, CONV_WIDTH)), _resident((2, D_MODEL)),
                  _resident((CONV_WIDTH, D_MODEL)), _resident((ATTN_WIDTH, D_MODEL)), _resident((D_MODEL, D_MODEL))]
        + [_ANY] * nx,
        out_specs=[rows(D_MODEL), rows(D_MODEL), rows(2 * D_MODEL), rows(3 * CONV_WIDTH), rows(ATTN_WIDTH)]
        + [_rev_stream_block(tm, d, GROUP_WIDTH, nt) for d in DILATIONS]
        + [_rev_stream_block(tm, d, 2 * GROUP_WIDTH, nt) for d in DILATIONS]
        + [pl.BlockSpec((SUBLANES, D_MODEL), lambda i: (0, 0)), pl.BlockSpec((SUBLANES, CONV_WIDTH), lambda i: (0, 0))]
        + [_ANY] * nx,
        scratch_shapes=[pltpu.VMEM((SUBLANES, CONV_WIDTH), F32),
                        pltpu.VMEM((GROUP_WIDTH // LANES * tm, LANES), F32)] + (_dma_sems(3 * nx) if nx else []),
        compiler_params=_params(1))(dx1, abcv, abcv, gates, ya, yb, yb0, lsetot,
                                    conv_w, conv_b, b_gate, w_pa, w_pb, w_out, *exchange)


def _attn_bwd(qkv, dy, aux, gi, exchange=()):
    d, length, _ = qkv.shape
    nb = length // ATTN_BLOCK
    q = ATTN_BLOCK
    gw = GROUP_WIDTH
    g = min(ATTN_BLOCKS_PER_STEP, nb)
    assert g >= 2 and nb % g == 0
    ns = nb // g
    tail = (g - 1) * q
    nx = len(exchange)

    def body(*refs):
        q_ref, kp_ref, kc_ref, vp_ref, vc_ref, dy_ref, aux_ref = refs[:7]
        part_refs = refs[7:7 + nx]
        out_ref = refs[7 + nx]
        recv_refs = refs[8 + nx:8 + 2 * nx]
        dq_ref, dkv_ref, bias_ref = refs[8 + 2 * nx:11 + 2 * nx]
        sems = refs[11 + 2 * nx:]
        n = pl.program_id(1)

        @pl.when((pl.program_id(0) == 0) & (n == 0))
        def _():
            _store_band_biases(bias_ref)
            for cp in _chip_exchange_copies(part_refs, recv_refs, *sems) if nx else ():
                cp.start()

        if nx:
            @pl.when((pl.program_id(0) == d - 1) & (n == ns))
            def _():
                for cp in _chip_exchange_copies(part_refs, recv_refs, *sems):
                    cp.wait()

        @pl.when(n > 0)
        def _():
            out_ref[:, 0:gw] = dq_ref[...].astype(BF16)
            out_ref[0:tail, gw:2 * gw] = dkv_ref[0, 0:tail].astype(BF16)
            out_ref[0:tail, 2 * gw:3 * gw] = dkv_ref[1, 0:tail].astype(BF16)

        @pl.when(n == ns)
        def _():
            out_ref[tail:g * q, gw:2 * gw] = dkv_ref[0, tail:g * q].astype(BF16)
            out_ref[tail:g * q, 2 * gw:3 * gw] = dkv_ref[1, tail:g * q].astype(BF16)

        @pl.when(n < ns)
        def _():
            kfull = jnp.concatenate([kp_ref[...], kc_ref[...]], axis=0)
            vfull = jnp.concatenate([vp_ref[...], vc_ref[...]], axis=0)
            for j in range(g):
                rows = slice(j * q, (j + 1) * q)
                qs = _stack_heads(q_ref[rows, :])
                dys = _stack_heads(dy_ref[rows, :])
                k2 = kfull[j * q:(j + 2) * q]
                v2 = vfull[j * q:(j + 2) * q]
                lse = _per_head_col(aux_ref[rows, 0:gw])
                delta = _per_head_col(aux_ref[rows, gw:2 * gw])
                bias = jnp.where(n == 0, bias_ref[1], bias_ref[0]) if j == 0 else bias_ref[0]
                p = jnp.exp(_dot_nt(qs, k2) * ATTN_SCALE + bias - lse)
                dp = _dot_nt(dys, v2)
                ds = (p * (dp - delta) * ATTN_SCALE).astype(BF16)
                dq_j = _unstack_heads(_dot(ds, k2))
                dk2 = _dot_tn(ds, qs)
                dv2 = _dot_tn(p.astype(BF16), dys)
                if j == 0:
                    @pl.when(n > 0)
                    def _():
                        out_ref[tail:g * q, gw:2 * gw] = (dkv_ref[0, tail:g * q] + dk2[0:q]).astype(BF16)
                        out_ref[tail:g * q, 2 * gw:3 * gw] = (dkv_ref[1, tail:g * q] + dv2[0:q]).astype(BF16)
                else:
                    dkv_ref[0, (j - 1) * q:j * q] += dk2[0:q]
                    dkv_ref[1, (j - 1) * q:j * q] += dv2[0:q]
                dkv_ref[0, rows] = dk2[q:2 * q]
                dkv_ref[1, rows] = dv2[q:2 * q]
                dq_ref[rows, :] = dq_j

    cur, prev = _attn_block_specs(g, nb, clamp_last=True)
    return pl.pallas_call(
        body, name=f"attn_bwd_g{gi}", grid=(d, ns + 1),
        out_shape=[jax.ShapeDtypeStruct((d, length, 3 * gw), BF16)]
        + [jax.ShapeDtypeStruct((3,) + a.shape[1:], a.dtype) for a in exchange],
        in_specs=[cur(0), prev(1), cur(1), prev(2), cur(2), cur(0), cur(0, 2 * gw)] + [_ANY] * nx,
        out_specs=[pl.BlockSpec((None, g * q, 3 * gw), lambda r, n: (r, jnp.maximum(n - 1, 0), 0))] + [_ANY] * nx,
        scratch_shapes=[pltpu.VMEM((g * q, gw), F32), pltpu.VMEM((2, g * q, gw), F32),
                        pltpu.VMEM((2, HEADS_PER_GROUP * q, 2 * q), F32)] + (_dma_sems(3 * nx) if nx else []),
        compiler_params=_params(2))(qkv, qkv, qkv, qkv, qkv, dy, aux, *exchange)


def _matmul_tn(name, a, b, col_tile=1024, row_tile=1024, slabs=0):
    s, k = a.shape
    n = b.shape[1]
    tk = min(row_tile, s)
    tn = col_tile
    steps = s // tk

    def body(a_ref, b_ref, o_ref, acc_ref):
        t = pl.program_id(1)

        @pl.when(t == 0)
        def _():
            acc_ref[...] = jnp.zeros_like(acc_ref)

        acc_ref[...] += _dot_tn(a_ref[...], b_ref[...])

        @pl.when(t == steps - 1)
        def _():
            if slabs:
                for q in range(per_tile):
                    o_ref[q] = acc_ref[:, q * width:(q + 1) * width].astype(BF16)
            else:
                o_ref[...] = acc_ref[...].astype(BF16)

    if slabs:
        width = n // slabs
        per_tile = tn // width
        out_shape = jax.ShapeDtypeStruct((slabs, k, width), BF16)
        out_spec = pl.BlockSpec((per_tile, k, width), lambda j, t: (j, 0, 0))
    else:
        out_shape = jax.ShapeDtypeStruct((k, n), BF16)
        out_spec = pl.BlockSpec((k, tn), lambda j, t: (0, j))
    return pl.pallas_call(
        body, name=name, grid=(n // tn, steps), out_shape=out_shape,
        in_specs=[pl.BlockSpec((tk, k), lambda j, t: (t, 0)), pl.BlockSpec((tk, tn), lambda j, t: (t, j))],
        out_specs=out_spec, scratch_shapes=[pltpu.VMEM((k, tn), F32)],
        compiler_params=_params(2))(a, b)


def _weight_allgather(bigs, smalls):
    nb, ns = len(bigs), len(smalls)
    n_sems = 3 * (2 * nb + ns)

    def body(*refs):
        big_refs, small_refs = refs[:nb], refs[nb:nb + ns]
        big_outs, small_outs = refs[nb + ns:2 * nb + ns], refs[2 * nb + ns:2 * (nb + ns)]
        send_sems, recv_sems = refs[2 * (nb + ns):]
        x, y, c, chips = _mesh_position()
        me = 2 * x + y
        sibling = (x, y, 1 - c)

        def copy(k, src, dst, to):
            return pltpu.make_async_remote_copy(src_ref=src, dst_ref=dst, send_sem=send_sems.at[k],
                                                recv_sem=recv_sems.at[k], device_id=to, device_id_type=MESH_ID)

        halves = [r.shape[1] // 2 for r in big_refs]
        first = []
        for j, (px, py) in enumerate(chips):
            for b in range(nb):
                mine = _half_rows(c, halves[b])
                first.append(copy(3 * b + j, big_refs[b].at[0, mine], big_outs[b].at[me, mine], (px, py, c)))
            for s in range(ns):
                first.append(copy(3 * (2 * nb + s) + j, small_refs[s].at[0], small_outs[s].at[me], (px, py, c)))
        for cp in first:
            cp.start()
        passed = []
        for j, (px, py) in enumerate(chips):
            for b in range(nb):
                landed = big_outs[b].at[2 * px + py, _half_rows(c, halves[b])]
                copy(3 * b + j, landed, landed, (px, py, c)).wait_recv()
                fwd = copy(3 * (nb + b) + j, landed, landed, sibling)
                fwd.start()
                passed.append(fwd)
        for j, (px, py) in enumerate(chips):
            for s in range(ns):
                landed = small_outs[s].at[2 * px + py]
                copy(3 * (2 * nb + s) + j, landed, landed, (px, py, c)).wait_recv()
            for b in range(nb):
                from_sibling = big_outs[b].at[2 * px + py, _half_rows(1 - c, halves[b])]
                copy(3 * (nb + b) + j, from_sibling, from_sibling, sibling).wait_recv()
        for cp in first + passed:
            cp.wait_send()

    return pl.pallas_call(
        body, name="weight_allgather",
        out_shape=[jax.ShapeDtypeStruct((N_CHIPS,) + a.shape[1:], a.dtype) for a in list(bigs) + list(smalls)],
        in_specs=[_ANY] * (nb + ns), out_specs=[_ANY] * (nb + ns),
        scratch_shapes=[pltpu.SemaphoreType.DMA((n_sems,)), pltpu.SemaphoreType.DMA((n_sems,))],
    )(*bigs, *smalls)


def _sibling_swap_halves(name, slabs):
    na = len(slabs)

    def body(*refs):
        src_refs, out_refs = refs[:na], refs[na:2 * na]
        send_sems, recv_sems = refs[2 * na:]
        x, y, c, _ = _mesh_position()
        cps = []
        for a in range(na):
            theirs = _half_rows(1 - c, src_refs[a].shape[1] // 2)
            cps.append(pltpu.make_async_remote_copy(
                src_ref=src_refs[a].at[:, theirs, :], dst_ref=out_refs[a], send_sem=send_sems.at[a],
                recv_sem=recv_sems.at[a], device_id=(x, y, 1 - c), device_id_type=MESH_ID))
        for cp in cps:
            cp.start()
        for cp in cps:
            cp.wait()

    return pl.pallas_call(
        body, name=name,
        out_shape=[jax.ShapeDtypeStruct((a.shape[0], a.shape[1] // 2, a.shape[2]), a.dtype) for a in slabs],
        in_specs=[_ANY] * na, out_specs=[_ANY] * na,
        scratch_shapes=[pltpu.SemaphoreType.DMA((na,)), pltpu.SemaphoreType.DMA((na,))])(*slabs)


def _chip_exchange(partials):
    na = len(partials)

    def body(*refs):
        cps = _chip_exchange_copies(refs[:na], refs[na:2 * na], *refs[2 * na:])
        for cp in cps:
            cp.start()
        for cp in cps:
            cp.wait()

    return pl.pallas_call(
        body, name="grad_chip_exchange",
        out_shape=[jax.ShapeDtypeStruct((3,) + a.shape[1:], a.dtype) for a in partials],
        in_specs=[_ANY] * na, out_specs=[_ANY] * na, scratch_shapes=_dma_sems(3 * na))(*partials)


def _sibling_share(halves):
    na = len(halves)

    def body(*refs):
        out_refs = refs[na:2 * na]
        send_sems, recv_sems = refs[2 * na:]
        x, y, c, _ = _mesh_position()
        cps = []
        for a in range(na):
            mine = out_refs[a].at[0, _half_rows(c, out_refs[a].shape[1] // 2)]
            cps.append(pltpu.make_async_remote_copy(src_ref=mine, dst_ref=mine, send_sem=send_sems.at[a],
                                                    recv_sem=recv_sems.at[a], device_id=(x, y, 1 - c),
                                                    device_id_type=MESH_ID))
        for cp in cps:
            cp.start()
        for a, cp in enumerate(cps):
            cp.wait_send()
            theirs = out_refs[a].at[0, _half_rows(1 - c, out_refs[a].shape[1] // 2)]
            pltpu.make_async_remote_copy(src_ref=theirs, dst_ref=theirs, send_sem=send_sems.at[a],
                                         recv_sem=recv_sems.at[a], device_id=(x, y, 1 - c),
                                         device_id_type=MESH_ID).wait_recv()

    return pl.pallas_call(
        body, name="grad_sibling_share", out_shape=[jax.ShapeDtypeStruct(a.shape, a.dtype) for a in halves],
        in_specs=[_ANY] * na, out_specs=[_ANY] * na, input_output_aliases={a: a for a in range(na)},
        scratch_shapes=[pltpu.SemaphoreType.DMA((na,)), pltpu.SemaphoreType.DMA((na,))])(*halves)


def _add_sibling(name, slab, received, core):
    n, rows, cols = slab.shape
    half = rows // 2

    def body(core_ref, a_ref, b_ref, o_ref):
        o_ref[...] = (a_ref[...].astype(F32) + b_ref[...].astype(F32)).astype(BF16)

    grid_spec = pltpu.PrefetchScalarGridSpec(
        num_scalar_prefetch=1, grid=(n,),
        in_specs=[pl.BlockSpec((None, half, cols), lambda s, core_ref: (s, core_ref[0], 0)),
                  pl.BlockSpec((None, half, cols), lambda s, core_ref: (s, 0, 0))],
        out_specs=pl.BlockSpec((None, half, cols), lambda s, core_ref: (s, 0, 0)))
    return pl.pallas_call(body, name=name, grid_spec=grid_spec,
                          out_shape=jax.ShapeDtypeStruct((n, half, cols), BF16),
                          compiler_params=_params(1))(core, slab, received)


def _sum_chips(name, partial, received, chip_core):
    _, half, cols = partial.shape

    def body(cc_ref, own_ref, recv_ref, o_ref):
        acc = own_ref[...].astype(F32)
        for k in range(3):
            acc = acc + recv_ref[k].astype(F32)
        o_ref[...] = acc

    grid_spec = pltpu.PrefetchScalarGridSpec(
        num_scalar_prefetch=1, grid=(1,),
        in_specs=[pl.BlockSpec((None, half, cols), lambda i, cc_ref: (cc_ref[0], 0, 0)),
                  pl.BlockSpec((3, half, cols), lambda i, cc_ref: (0, 0, 0))],
        out_specs=pl.BlockSpec((None, half, cols), lambda i, cc_ref: (0, cc_ref[1], 0)))
    return pl.pallas_call(body, name=name, grid_spec=grid_spec,
                          out_shape=jax.ShapeDtypeStruct((1, 2 * half, cols), F32),
                          compiler_params=_params(1))(chip_core, partial, received)


def _adam_math(w, g, m, v):
    nm = ADAM_B1 * m + (1.0 - ADAM_B1) * g
    nv = ADAM_B2 * v + (1.0 - ADAM_B2) * jnp.square(g)
    m_hat = nm / (1.0 - ADAM_B1 ** ADAM_STEP)
    v_hat = nv / (1.0 - ADAM_B2 ** ADAM_STEP)
    delta = -ADAM_LR * (m_hat / (jnp.sqrt(v_hat) + ADAM_EPS) + ADAM_WD * w)
    return delta, nm, nv


def _adamw(name, w, g, m, v):
    _, rows, cols = w.shape
    tr = next(t for t in (736, 512, 384, 352, 256, 128, 64, 32, 16, 8) if rows % t == 0)

    def body(w_ref, g_ref, m_ref, v_ref, d_ref, nm_ref, nv_ref):
        d_ref[...], nm_ref[...], nv_ref[...] = _adam_math(w_ref[...], g_ref[...], m_ref[...], v_ref[...])

    spec = pl.BlockSpec((None, tr, cols), lambda i: (0, i, 0))
    return pl.pallas_call(
        body, name=name, grid=(rows // tr,), out_shape=[jax.ShapeDtypeStruct(w.shape, F32)] * 3,
        in_specs=[spec] * 4, out_specs=[spec] * 3, compiler_params=_params(1))(w, g, m, v)


SMALL_PARAMS = ("norm_mix_g", "b_gate", "conv_a_w", "conv_a_b", "norm_ffn_g", "ffn_conv_w", "ffn_conv_b", "final_norm_g")


def _small_update(partials, params, moments_m, moments_v):
    na = len(partials)
    npar = len(SMALL_PARAMS)

    def body(*refs):
        in_refs = refs[:na]
        w_refs = refs[na:na + npar]
        m_refs = refs[na + npar:na + 2 * npar]
        v_refs = refs[na + 2 * npar:na + 3 * npar]
        pos = na + 3 * npar
        loss_ref = refs[pos]
        out_refs = refs[pos + 1:pos + 1 + 4 * npar]
        pos += 1 + 4 * npar
        acc_refs = refs[pos:pos + na]
        recv_refs = refs[pos + na:pos + 4 * na]
        send_sems, recv_sems = refs[pos + 4 * na:]
        x, y, c, _ = _mesh_position()
        chip = 2 * x + y
        for a in range(na):
            acc_refs[a][...] = in_refs[a][...]
        for stage, peer in enumerate(((x, y, 1 - c), (x, 1 - y, c), (1 - x, y, c))):
            cps = []
            for a in range(na):
                k = stage * na + a
                cps.append(pltpu.make_async_remote_copy(src_ref=acc_refs[a], dst_ref=recv_refs[k], send_sem=send_sems.at[k],
                                                        recv_sem=recv_sems.at[k], device_id=peer, device_id_type=MESH_ID))
            for cp in cps:
                cp.start()
            for cp in cps:
                cp.wait()
            for a in range(na):
                acc_refs[a][...] = acc_refs[a][...] + recv_refs[stage * na + a][...]

        mix, ffn, fin, gate, conv, ffnc, loss = acc_refs
        loss_ref[...] = loss[...]

        def cols(width):
            return pl.ds(pl.multiple_of(chip * width, LANES), width)

        grads = {
            "norm_mix_g": mix[...], "norm_ffn_g": ffn[...], "final_norm_g": fin[...],
            "b_gate": gate[0:2, cols(D_MODEL // N_CHIPS)],
            "conv_a_w": conv[0:3, cols(CONV_WIDTH // N_CHIPS)], "conv_a_b": conv[3:4, :],
            "ffn_conv_w": ffnc[0:3, cols(2 * D_FF // N_CHIPS)], "ffn_conv_b": ffnc[3:4, :]}
        for i, name in enumerate(SMALL_PARAMS):
            g = grads[name]
            if len(w_refs[i].shape) == 3:
                results = (g,) + _adam_math(w_refs[i][0], g, m_refs[i][0], v_refs[i][0])
                for o_ref, val in zip(out_refs[4 * i:4 * i + 4], results):
                    o_ref[0] = val
            else:
                results = (g,) + _adam_math(w_refs[i][...], g, m_refs[i][...], v_refs[i][...])
                for o_ref, val in zip(out_refs[4 * i:4 * i + 4], results):
                    o_ref[...] = val

    outs = [jax.ShapeDtypeStruct(partials[-1].shape, F32)]
    for w in params:
        outs += [jax.ShapeDtypeStruct(w.shape, F32)] * 4
    scratch = [pltpu.VMEM(p.shape, F32) for p in partials]
    scratch += [pltpu.VMEM(p.shape, F32) for _ in range(3) for p in partials]
    scratch += [pltpu.SemaphoreType.DMA((3 * na,)), pltpu.SemaphoreType.DMA((3 * na,))]
    n_in = na + 3 * npar
    return pl.pallas_call(
        body, name="small_update", out_shape=outs, in_specs=[_VMEM] * n_in, out_specs=[_VMEM] * len(outs),
        scratch_shapes=scratch)(*partials, *params, *moments_m, *moments_v)


def _gathered_columns(g):
    return jnp.transpose(g, (1, 0, 2)).reshape(g.shape[1], N_CHIPS * g.shape[2])


def _column_slabs(full):
    k, n = full.shape
    return jnp.transpose(full.reshape(k, N_CHIPS, n // N_CHIPS), (1, 0, 2))


def kernel(x, norm_mix_g, w_in, b_gate, conv_a_w, conv_a_b, w_proj_a, w_proj_b, w_out, norm_ffn_g, w_up, ffn_conv_w, ffn_conv_b, w_down, final_norm_g, loss_target, m_norm_mix_g, m_w_in, m_b_gate, m_conv_a_w, m_conv_a_b, m_w_proj_a, m_w_proj_b, m_w_out, m_norm_ffn_g, m_w_up, m_ffn_conv_w, m_ffn_conv_b, m_w_down, m_final_norm_g, v_norm_mix_g, v_w_in, v_b_gate, v_conv_a_w, v_conv_a_b, v_w_proj_a, v_w_proj_b, v_w_out, v_norm_ffn_g, v_w_up, v_ffn_conv_w, v_ffn_conv_b, v_w_down, v_final_norm_g):
    chip = (2 * lax.axis_index("x") + lax.axis_index("y")).astype(jnp.int32)
    core = lax.axis_index("c").astype(jnp.int32)
    core_arr = core.reshape(1)
    chip_core = jnp.stack([chip, core])
    xs, target = x[0], loss_target[0]
    g_final = final_norm_g.reshape(1, D_MODEL)

    def own_slot(gathered, own):
        return lax.dynamic_update_slice(gathered, own, (chip, 0, 0))

    def reduce_to_shards(names, slabs, exchange_in):
        from_sibling = _sibling_swap_halves("grad_swap_" + names[0], slabs)
        partials = [_add_sibling("grad_add_" + n, s, r, core_arr) for n, s, r in zip(names, slabs, from_sibling)]
        received, rest = exchange_in(partials)
        halves = [_sum_chips("grad_sum_" + n, p, r, chip_core) for n, p, r in zip(names, partials, received)]
        return halves, rest

    w_in_t, m_w_in_t, v_w_in_t = (jnp.swapaxes(a, 1, 2) for a in (w_in, m_w_in, v_w_in))
    w_in_tb = w_in_t.astype(BF16)
    (g_in,) = _weight_allgather([w_in_tb], [])
    w_in_full_t = own_slot(g_in, w_in_tb).reshape(D_IN, D_MODEL)
    later_w = [w_proj_a, w_proj_b, w_out, w_up, w_down]
    later_b = [w.astype(BF16) for w in later_w]
    small_sharded = [b_gate, conv_a_w, ffn_conv_w]
    fwd = _inproj_fwd(xs, norm_mix_g, w_in_full_t, later_b, small_sharded)
    h1, abcv, gates, qkv0, qkv1, qkv2 = fwd[:6]
    gathered_big, gathered_small = fwd[6:11], fwd[11:14]
    qkvs = (qkv0, qkv1, qkv2)
    attn0 = _attn_fwd(qkv0, 0, forward=gathered_big)
    attn = [attn0[:2], _attn_fwd(qkv1, 1), _attn_fwd(qkv2, 2)]
    g_pa, g_pb, g_out, g_up, g_down = [own_slot(g, own) for g, own in zip(attn0[2:], later_b)]
    g_bgate, g_convw, g_ffnw = [own_slot(g, own) for g, own in zip(gathered_small, small_sharded)]
    w_pa_full, w_pb_full, w_up_full = _gathered_columns(g_pa), _gathered_columns(g_pb), _gathered_columns(g_up)
    w_out_full, w_down_full = g_out.reshape(D_MODEL, D_MODEL), g_down.reshape(D_FF, D_MODEL)
    b_gate_full, conv_w_full, ffn_w_full = (_gathered_columns(g) for g in (g_bgate, g_convw, g_ffnw))

    x1, ya0, yb0, mrg, ya, yb, lsetot = _mix_fwd(
        xs, abcv, gates, [a[0] for a in attn], [a[1] for a in attn], conv_w_full, conv_a_b, b_gate_full,
        w_pa_full, w_pb_full, w_out_full)
    h2, up0, up = _ffn_up_fwd(x1, norm_ffn_g, w_up_full, ffn_w_full, ffn_conv_b)
    act, dx2, dx2b, d_g_final, loss = _ffn_act_fwd(x1, up, target, w_down_full, g_final)

    d_up = _ffn_act_bwd(dx2b, up, w_down_full)
    d_w_down = _matmul_tn("dw_down", act, dx2b, col_tile=512)
    dx1, dx1b, d_g_ffn, d_up0, ffn_small = _ffn_up_bwd(d_up, up0, w_up_full, ffn_w_full, x1, norm_ffn_g, dx2)
    d_w_up = _matmul_tn("dw_up", h2, d_up0, col_tile=2 * D_FF // N_CHIPS, slabs=N_CHIPS)

    def behind_mix_bwd(partials):
        res = _mix_bwd(dx1, abcv, gates, ya, yb, yb0, lsetot, conv_w_full, conv_a_b, b_gate_full,
                       w_pa_full, w_pb_full, w_out_full, exchange=partials)
        return res[13:], res[:13]

    halves_ffn, mix_res = reduce_to_shards(
        ("w_up", "w_down"), [d_w_up, d_w_down.reshape(N_CHIPS, D_FF // N_CHIPS, D_MODEL)], behind_mix_bwd)
    (d_ya, d_yb, d_gates, d_abcv, d_yb0, dyl0, dyl1, dyl2, aux0, aux1, aux2, gate_small, conv_small) = mix_res
    d_w_out = _matmul_tn("dw_out", mrg, dx1b)
    d_w_pa = _matmul_tn("dw_proj_a", ya0, d_ya, slabs=N_CHIPS)
    d_w_pb = _matmul_tn("dw_proj_b", yb0, d_yb, slabs=N_CHIPS)

    def behind_attn_bwd(partials):
        res = _attn_bwd(qkv0, dyl0, aux0, 0, exchange=partials)
        return res[1:], res[0]

    halves_mix, d_qkv0 = reduce_to_shards(
        ("w_proj_a", "w_proj_b", "w_out"),
        [d_w_pa, d_w_pb, d_w_out.reshape(N_CHIPS, D_MODEL // N_CHIPS, D_MODEL)], behind_attn_bwd)
    (d_qkv1,), (d_qkv2,) = _attn_bwd(qkv1, dyl1, aux1, 1), _attn_bwd(qkv2, dyl2, aux2, 2)

    dq = [d_qkv0[0], d_qkv1, d_qkv2]
    group_pieces = [[(j * GROUP_WIDTH, GROUP_WIDTH, base + g * GROUP_WIDTH) for j, base in enumerate((COL_Q, COL_K, COL_V))]
                    for g in range(3)]
    grad_x, _, d_g_mix, nat1, nat2 = _norm_bwd(
        "inproj_bwd", [d_abcv, d_gates] + dq, w_in_full_t,
        [[(0, 3 * CONV_WIDTH, COL_ABCV)], [(0, 2 * D_MODEL, COL_GATES)]] + group_pieces,
        xs, norm_mix_g, dx1, stream_dils=(1, 1, 1, 4, 16), w_transposed=True)

    d_w_abcv = _matmul_tn("dw_in_abcv", d_abcv, h1)
    d_w_gates = _matmul_tn("dw_in_gates", d_gates, h1)
    d_w_groups = [_matmul_tn(f"dw_in_qkv{g}", t, h1) for g, t in enumerate((dq[0], nat1, nat2))]
    gw = GROUP_WIDTH
    d_w_in_t = jnp.concatenate(
        [d_w_abcv] + [d_w_groups[g][j * gw:(j + 1) * gw] for j in range(3) for g in range(3)] + [d_w_gates], axis=0)
    halves_in, _ = reduce_to_shards(("w_in",), [d_w_in_t.reshape(N_CHIPS, D_IN // N_CHIPS, D_MODEL)],
                                    lambda partials: (_chip_exchange(partials), None))

    big_names = ("w_in", "w_proj_a", "w_proj_b", "w_out", "w_up", "w_down")
    big_grads = _sibling_share(halves_in + halves_mix + halves_ffn)
    big_w = dict(w_in=w_in_t, w_proj_a=w_proj_a, w_proj_b=w_proj_b, w_out=w_out, w_up=w_up, w_down=w_down)
    big_m = dict(w_in=m_w_in_t, w_proj_a=m_w_proj_a, w_proj_b=m_w_proj_b, w_out=m_w_out, w_up=m_w_up, w_down=m_w_down)
    big_v = dict(w_in=v_w_in_t, w_proj_a=v_w_proj_a, w_proj_b=v_w_proj_b, w_out=v_w_out, w_up=v_w_up, w_down=v_w_down)

    fin_w, fin_m, fin_v = (a.reshape(1, D_MODEL) for a in (final_norm_g, m_final_norm_g, v_final_norm_g))
    small_w = [norm_mix_g, b_gate, conv_a_w, conv_a_b, norm_ffn_g, ffn_conv_w, ffn_conv_b, fin_w]
    small_m = [m_norm_mix_g, m_b_gate, m_conv_a_w, m_conv_a_b, m_norm_ffn_g, m_ffn_conv_w, m_ffn_conv_b, fin_m]
    small_v = [v_norm_mix_g, v_b_gate, v_conv_a_w, v_conv_a_b, v_norm_ffn_g, v_ffn_conv_w, v_ffn_conv_b, fin_v]
    small_out = _small_update([d_g_mix, d_g_ffn, d_g_final, gate_small, conv_small, ffn_small, loss],
                              small_w, small_m, small_v)
    total_loss = small_out[0][0, 0]

    grads, delta, new_m, new_v = {}, {}, {}, {}
    for i, n in enumerate(SMALL_PARAMS):
        vals = small_out[1 + 4 * i:5 + 4 * i]
        if n == "final_norm_g":
            vals = [a.reshape(D_MODEL) for a in vals]
        grads[n], delta[n], new_m[n], new_v[n] = vals
    for n, g in zip(big_names, big_grads):
        vals = (g,) + tuple(_adamw("adamw_" + n, big_w[n], g, big_m[n], big_v[n]))
        if n == "w_in":
            vals = [jnp.swapaxes(a, 1, 2) for a in vals]
        grads[n], delta[n], new_m[n], new_v[n] = vals

    names = ["norm_mix_g", "w_in", "b_gate", "conv_a_w", "conv_a_b", "w_proj_a", "w_proj_b", "w_out", "norm_ffn_g", "w_up",
             "ffn_conv_w", "ffn_conv_b", "w_down", "final_norm_g"]
    out = [total_loss, grad_x[None]]
    for group in (grads, delta, new_m, new_v):
        out += [group[n] for n in names]
    return tuple(out)
```

```python
import jax
import jax.numpy as jnp
from jax import lax
from jax.experimental import pallas as pl
from jax.experimental.pallas import tpu as pltpu

F32 = jnp.float32
BF16 = jnp.bfloat16

D_MODEL = 1024
CONV_WIDTH = 512
ATTN_WIDTH = 768
GROUP_WIDTH = 256
HEAD_DIM = 64
HEADS_PER_GROUP = 4
DILATIONS = (1, 4, 16)
ATTN_BLOCK = 128
D_FF = 2816
D_IN = 5888
EPS = 1e-6
NEG_INF = -1e30
ATTN_SCALE = HEAD_DIM ** -0.5

COL_ABCV = 0
COL_Q = 1536
COL_K = 2304
COL_V = 3072
COL_GATES = 3840

ADAM_LR = 0.001
ADAM_B1 = 0.9
ADAM_B2 = 0.999
ADAM_EPS = 1e-08
ADAM_WD = 0.01
ADAM_STEP = 10

LANES = 128
SUBLANES = 8
ROW_TILE = 512
VMEM_LIMIT = 56 * 1024 * 1024

_NT = (((1,), (1,)), ((), ()))
_TN = (((0,), (0,)), ((), ()))


def _params(n_axes, vmem=VMEM_LIMIT):
    return pltpu.CompilerParams(dimension_semantics=("arbitrary",) * n_axes, vmem_limit_bytes=vmem)


def _resident(shape):
    nd = len(shape)
    return pl.BlockSpec(shape, lambda *_: (0,) * nd, pipeline_mode=pl.Buffered(1))


def _rows(tm, width, col_block=0):
    return pl.BlockSpec((tm, width), lambda i: (i, col_block))


def _col_chunks(n, cmax):
    out, lo = [], 0
    while lo < n:
        size = min(cmax, n - lo)
        out.append((lo, size))
        lo += size
    return out


def _dot(a, b):
    return jnp.dot(a, b, preferred_element_type=F32)


def _dot_nt(a, b):
    return lax.dot_general(a, b, _NT, preferred_element_type=F32)


def _dot_tn(a, b):
    return lax.dot_general(a, b, _TN, preferred_element_type=F32)


def _sigmoid(x):
    return 1.0 / (1.0 + jnp.exp(-x))


def _shift_down(v, k, halo8):
    tm = v.shape[0]
    rolled = pltpu.roll(v, k, 0)
    fix = jnp.tile(pltpu.roll(halo8, k, 0), (tm // SUBLANES, 1))
    row = lax.broadcasted_iota(jnp.int32, v.shape, 0)
    return jnp.where(row < k, fix, rolled)


def _shift_up(v, k, halo8):
    tm = v.shape[0]
    rolled = pltpu.roll(v, tm - k, 0)
    fix = jnp.tile(pltpu.roll(halo8, SUBLANES - k, 0), (tm // SUBLANES, 1))
    row = lax.broadcasted_iota(jnp.int32, v.shape, 0)
    return jnp.where(row >= tm - k, fix, rolled)


def _colsum(v):
    return jnp.sum(v, axis=0, keepdims=True)


def _to_streams(val, scr, out_ref, d, col0):
    tm = val.shape[0]
    panels = val.shape[1] // LANES
    for p in range(panels):
        scr[pl.ds(p * tm, tm), :] = val[:, p * LANES:(p + 1) * LANES]
    for r in range(d):
        for p in range(panels):
            piece = scr[pl.ds(p * tm + r, tm // d, stride=d), :]
            out_ref[r, :, col0 + p * LANES: col0 + (p + 1) * LANES] = piece.astype(out_ref.dtype)


def _from_streams(in_ref, scr, d, col0, width):
    panels = width // LANES
    rows = in_ref.shape[1]
    tm = rows * d
    for r in range(d):
        for p in range(panels):
            scr[pl.ds(p * tm + r, rows, stride=d), :] = in_ref[r, :, col0 + p * LANES: col0 + (p + 1) * LANES].astype(F32)
    return jnp.concatenate([scr[pl.ds(p * tm, tm), :] for p in range(panels)], axis=1)


def _stream_block(tm, d, width):
    return pl.BlockSpec((d, tm // d, width), lambda i: (0, i, 0))


def _rev_stream_block(tm, d, width, nt):
    return pl.BlockSpec((d, tm // d, width), lambda i: (0, nt - 1 - i, 0))


N_CHIPS = 4
MESH_ID = pl.DeviceIdType.MESH
_ANY = pl.BlockSpec(memory_space=pl.ANY)
_VMEM = pl.BlockSpec(memory_space=pltpu.VMEM)


def _mesh_position():
    x, y, c = lax.axis_index("x"), lax.axis_index("y"), lax.axis_index("c")
    other_chips = [(1 - x, y), (x, 1 - y), (1 - x, 1 - y)]
    return x, y, c, other_chips


def _half_rows(c, half):
    return pl.ds(pl.multiple_of(c * half, 16), half)


def _remote_copy(k, src, dst, to, send_sems, recv_sems):
    return pltpu.make_async_remote_copy(src_ref=src, dst_ref=dst, send_sem=send_sems.at[k], recv_sem=recv_sems.at[k],
                                        device_id=to, device_id_type=MESH_ID)


def _gather_first_copies(big_refs, small_refs, big_outs, small_outs, send_sems, recv_sems):
    x, y, c, chips = _mesh_position()
    me = 2 * x + y
    nb = len(big_refs)
    cps = []
    for j, (px, py) in enumerate(chips):
        for b in range(nb):
            mine = _half_rows(c, big_refs[b].shape[1] // 2)
            cps.append(_remote_copy(3 * b + j, big_refs[b].at[0, mine], big_outs[b].at[me, mine], (px, py, c),
                                    send_sems, recv_sems))
        for s in range(len(small_refs)):
            cps.append(_remote_copy(3 * (nb + s) + j, small_refs[s].at[0], small_outs[s].at[me], (px, py, c),
                                    send_sems, recv_sems))
    return cps


def _gather_forward_copies(bufs, send_sems, recv_sems):
    x, y, c, chips = _mesh_position()
    cps = []
    for j, (px, py) in enumerate(chips):
        for b in range(len(bufs)):
            landed = bufs[b].at[2 * px + py, _half_rows(c, bufs[b].shape[1] // 2)]
            cps.append(_remote_copy(3 * b + j, landed, landed, (x, y, 1 - c), send_sems, recv_sems))
    return cps


def _chip_exchange_copies(src_refs, out_refs, send_sems, recv_sems):
    x, y, c, chips = _mesh_position()
    cps = []
    for j, (px, py) in enumerate(chips):
        for a in range(len(src_refs)):
            cps.append(_remote_copy(3 * a + j, src_refs[a].at[2 * px + py], out_refs[a].at[j], (px, py, c),
                                    send_sems, recv_sems))
    return cps


def _dma_sems(n):
    return [pltpu.SemaphoreType.DMA((n,)), pltpu.SemaphoreType.DMA((n,))]


def _inproj_fwd(x, g, w_in_t, big_shards, small_shards):
    s = x.shape[0]
    tm = ROW_TILE
    nt = s // tm
    nb, ns = len(big_shards), len(small_shards)
    n_fixed_in, n_fixed_out = 3, 6

    def body(*refs):
        x_ref, g_ref, w_ref = refs[:n_fixed_in]
        shard_refs = refs[n_fixed_in:n_fixed_in + nb + ns]
        pos = n_fixed_in + nb + ns
        h_ref, abcv_ref, gates_ref, qkv0_ref, qkv1_ref, qkv2_ref = refs[pos:pos + n_fixed_out]
        gathered_refs = refs[pos + n_fixed_out:pos + n_fixed_out + nb + ns]
        scr, send_sems, recv_sems = refs[pos + n_fixed_out + nb + ns:]
        i = pl.program_id(0)

        def gather_copies():
            return _gather_first_copies(shard_refs[:nb], shard_refs[nb:], gathered_refs[:nb], gathered_refs[nb:],
                                        send_sems, recv_sems)

        @pl.when(i == 0)
        def _():
            for cp in gather_copies():
                cp.start()

        xv = x_ref[...]
        r = lax.rsqrt(jnp.mean(xv * xv, axis=-1, keepdims=True) + EPS)
        h = (xv * r * g_ref[...]).astype(BF16)
        h_ref[...] = h
        for lo, size in _col_chunks(3 * CONV_WIDTH, 512):
            abcv_ref[:, lo:lo + size] = _dot_nt(h, w_ref[COL_ABCV + lo: COL_ABCV + lo + size, :]).astype(BF16)
        for lo, size in _col_chunks(2 * D_MODEL, 512):
            gates_ref[:, lo:lo + size] = _dot_nt(h, w_ref[COL_GATES + lo: COL_GATES + lo + size, :]).astype(BF16)
        for gi, (d, out_ref) in enumerate(zip(DILATIONS, (qkv0_ref, qkv1_ref, qkv2_ref))):
            for j, base in enumerate((COL_Q, COL_K, COL_V)):
                lo = base + gi * GROUP_WIDTH
                y = _dot_nt(h, w_ref[lo:lo + GROUP_WIDTH, :])
                _to_streams(y, scr, out_ref, d, j * GROUP_WIDTH)

        @pl.when(i == nt - 1)
        def _():
            for cp in gather_copies():
                cp.wait()

    outs = [jax.ShapeDtypeStruct((s, D_MODEL), BF16),
            jax.ShapeDtypeStruct((s, 3 * CONV_WIDTH), BF16),
            jax.ShapeDtypeStruct((s, 2 * D_MODEL), BF16)]
    outs += [jax.ShapeDtypeStruct((d, s // d, 3 * GROUP_WIDTH), BF16) for d in DILATIONS]
    outs += [jax.ShapeDtypeStruct((N_CHIPS,) + a.shape[1:], a.dtype) for a in list(big_shards) + list(small_shards)]
    return pl.pallas_call(
        body, name="inproj_fwd", grid=(nt,), out_shape=outs,
        in_specs=[_rows(tm, D_MODEL), _resident((1, D_MODEL)), _resident((D_IN, D_MODEL))] + [_ANY] * (nb + ns),
        out_specs=[_rows(tm, D_MODEL), _rows(tm, 3 * CONV_WIDTH), _rows(tm, 2 * D_MODEL)]
        + [_stream_block(tm, d, 3 * GROUP_WIDTH) for d in DILATIONS] + [_ANY] * (nb + ns),
        scratch_shapes=[pltpu.VMEM((GROUP_WIDTH // LANES * tm, LANES), F32)] + _dma_sems(3 * (nb + ns)),
        compiler_params=_params(1))(x, g, w_in_t, *big_shards, *small_shards)


def _head_of_lane(shape):
    return lax.broadcasted_iota(jnp.int32, shape, 1) // HEAD_DIM


def _stack_heads(v):
    head = _head_of_lane(v.shape)
    return jnp.concatenate([jnp.where(head == h, v, jnp.zeros_like(v)) for h in range(HEADS_PER_GROUP)], axis=0)


def _unstack_heads(v):
    q = ATTN_BLOCK
    head = _head_of_lane((q, v.shape[1]))
    out = jnp.zeros((q, v.shape[1]), v.dtype)
    for h in range(HEADS_PER_GROUP):
        out = jnp.where(head == h, v[h * q:(h + 1) * q], out)
    return out


def _per_head_rows(col):
    q = ATTN_BLOCK
    head = _head_of_lane((q, GROUP_WIDTH))
    out = jnp.zeros((q, GROUP_WIDTH), col.dtype)
    for h in range(HEADS_PER_GROUP):
        out = jnp.where(head == h, col[h * q:(h + 1) * q], out)
    return out


def _per_head_col(v):
    head = _head_of_lane(v.shape)
    cols = [jnp.max(jnp.where(head == h, v, -jnp.inf), axis=1, keepdims=True) for h in range(HEADS_PER_GROUP)]
    return jnp.concatenate(cols, axis=0)


ATTN_BLOCKS_PER_STEP = 4


def _band_bias(first_block):
    rows = HEADS_PER_GROUP * ATTN_BLOCK
    qi = lax.broadcasted_iota(jnp.int32, (rows, 2 * ATTN_BLOCK), 0) % ATTN_BLOCK
    kj = lax.broadcasted_iota(jnp.int32, (rows, 2 * ATTN_BLOCK), 1)
    dist = qi + ATTN_BLOCK - kj
    valid = (dist >= 0) & (dist <= ATTN_BLOCK)
    if first_block:
        valid = valid & (kj >= ATTN_BLOCK)
    return jnp.where(valid, 0.0, NEG_INF).astype(F32)


def _store_band_biases(bias_ref):
    bias_ref[0] = _band_bias(False)
    bias_ref[1] = _band_bias(True)


def _attn_block_specs(g, nb, clamp_last=False):
    q = ATTN_BLOCK
    last = nb // g - 1

    def cur(col, width=GROUP_WIDTH):
        if clamp_last:
            return pl.BlockSpec((None, g * q, width), lambda r, n: (r, jnp.minimum(n, last), col))
        return pl.BlockSpec((None, g * q, width), lambda r, n: (r, n, col))

    def prev(col):
        if clamp_last:
            return pl.BlockSpec((None, q, GROUP_WIDTH), lambda r, n: (r, jnp.clip(n * g - 1, 0, nb - 1), col))
        return pl.BlockSpec((None, q, GROUP_WIDTH), lambda r, n: (r, jnp.maximum(n * g - 1, 0), col))

    return cur, prev


def _attn_fwd(qkv, gi, forward=()):
    d, length, _ = qkv.shape
    nb = length // ATTN_BLOCK
    q = ATTN_BLOCK
    g = min(ATTN_BLOCKS_PER_STEP, nb)
    ns = nb // g
    nf = len(forward)

    def body(*refs):
        q_ref, kp_ref, kc_ref, vp_ref, vc_ref = refs[:5]
        o_ref, lse_ref = refs[5 + nf:7 + nf]
        buf_refs = refs[7 + nf:7 + 2 * nf]
        bias_ref = refs[7 + 2 * nf]
        sems = refs[8 + 2 * nf:]
        n = pl.program_id(1)
        first_step = (pl.program_id(0) == 0) & (n == 0)
        last_step = (pl.program_id(0) == d - 1) & (n == ns - 1)

        @pl.when(first_step)
        def _():
            _store_band_biases(bias_ref)
            for cp in _gather_forward_copies(buf_refs, *sems) if nf else ():
                cp.start()

        kfull = jnp.concatenate([kp_ref[...], kc_ref[...]], axis=0)
        vfull = jnp.concatenate([vp_ref[...], vc_ref[...]], axis=0)
        for j in range(g):
            qs = _stack_heads(q_ref[j * q:(j + 1) * q, :])
            k2 = kfull[j * q:(j + 2) * q]
            v2 = vfull[j * q:(j + 2) * q]
            bias = jnp.where(n == 0, bias_ref[1], bias_ref[0]) if j == 0 else bias_ref[0]
            sc = _dot_nt(qs, k2) * ATTN_SCALE + bias
            m = jnp.max(sc, axis=1, keepdims=True)
            p = jnp.exp(sc - m)
            l = jnp.sum(p, axis=1, keepdims=True)
            of = _dot(p.astype(BF16), v2) / l
            o_ref[j * q:(j + 1) * q, :] = _unstack_heads(of).astype(BF16)
            lse_ref[j * q:(j + 1) * q, :] = _per_head_rows(m + jnp.log(l))

        if nf:
            @pl.when(last_step)
            def _():
                for cp in _gather_forward_copies(buf_refs, *sems):
                    cp.wait()

    cur, prev = _attn_block_specs(g, nb)
    return pl.pallas_call(
        body, name=f"attn_fwd_g{gi}", grid=(d, ns),
        out_shape=[jax.ShapeDtypeStruct((d, length, GROUP_WIDTH), BF16),
                   jax.ShapeDtypeStruct((d, length, GROUP_WIDTH), F32)]
        + [jax.ShapeDtypeStruct(a.shape, a.dtype) for a in forward],
        in_specs=[cur(0), prev(1), cur(1), prev(2), cur(2)] + [_ANY] * nf,
        out_specs=[cur(0), cur(0)] + [_ANY] * nf,
        input_output_aliases={5 + a: 2 + a for a in range(nf)},
        scratch_shapes=[pltpu.VMEM((2, HEADS_PER_GROUP * q, 2 * q), F32)] + (_dma_sems(3 * nf) if nf else []),
        compiler_params=_params(2))(qkv, qkv, qkv, qkv, qkv, *forward)


def _conv_branch(ab, ac, av, halo_u, w, b):
    u = ac * av
    sh1 = _shift_down(u, 1, halo_u)
    sh2 = _shift_down(u, 2, halo_u)
    cv = w[0:1] * sh2 + w[1:2] * sh1 + w[2:3] * u + b
    return ab * cv, cv, u, sh1, sh2


def _mix_fwd(x, abcv, gates, o_list, lse_list, conv_w, conv_b, b_gate, w_pa, w_pb, w_out):
    s = x.shape[0]
    tm = ROW_TILE

    def body(x_ref, abcv_ref, gates_ref, o0_ref, o1_ref, o2_ref, l0_ref, l1_ref, l2_ref,
             cw_ref, cb_ref, bg_ref, wpa_ref, wpb_ref, wout_ref,
             x1_ref, ya0_ref, yb0_ref, mrg_ref, ya_ref, yb_ref, lsetot_ref, halo_ref, scr):
        i = pl.program_id(0)

        @pl.when(i == 0)
        def _():
            halo_ref[...] = jnp.zeros_like(halo_ref)

        ab = abcv_ref[:, 0:CONV_WIDTH].astype(F32)
        ac = abcv_ref[:, CONV_WIDTH:2 * CONV_WIDTH].astype(F32)
        av = abcv_ref[:, 2 * CONV_WIDTH:3 * CONV_WIDTH].astype(F32)
        ya0, _, u, _, _ = _conv_branch(ab, ac, av, halo_ref[...], cw_ref[...], cb_ref[...])
        halo_ref[...] = u[tm - SUBLANES:tm]
        ya0 = ya0.astype(BF16)
        ya0_ref[...] = ya0
        ya = _dot(ya0, wpa_ref[...])

        o_refs, l_refs = (o0_ref, o1_ref, o2_ref), (l0_ref, l1_ref, l2_ref)
        lses = [_from_streams(l_refs[g], scr, DILATIONS[g], 0, GROUP_WIDTH) for g in range(3)]
        top = jnp.maximum(jnp.maximum(lses[0], lses[1]), lses[2])
        lsetot = top + jnp.log(jnp.exp(lses[0] - top) + jnp.exp(lses[1] - top) + jnp.exp(lses[2] - top))
        lsetot_ref[...] = lsetot
        yb = jnp.zeros((tm, D_MODEL), F32)
        for g in range(3):
            og = _from_streams(o_refs[g], scr, DILATIONS[g], 0, GROUP_WIDTH)
            yb0 = (jnp.exp(lses[g] - lsetot) * og).astype(BF16)
            yb0_ref[:, g * GROUP_WIDTH:(g + 1) * GROUP_WIDTH] = yb0
            yb = yb + _dot(yb0, wpb_ref[g * GROUP_WIDTH:(g + 1) * GROUP_WIDTH, :])

        sa = _sigmoid(gates_ref[:, 0:D_MODEL].astype(F32) + bg_ref[0:1, :])
        sb = _sigmoid(gates_ref[:, D_MODEL:2 * D_MODEL].astype(F32) + bg_ref[1:2, :])
        ya_ref[...] = ya.astype(BF16)
        yb_ref[...] = yb.astype(BF16)
        mrg = (sa * ya + sb * yb).astype(BF16)
        mrg_ref[...] = mrg
        x1_ref[...] = x_ref[...] + _dot(mrg, wout_ref[...])

    outs = [jax.ShapeDtypeStruct((s, D_MODEL), F32),
            jax.ShapeDtypeStruct((s, CONV_WIDTH), BF16),
            jax.ShapeDtypeStruct((s, ATTN_WIDTH), BF16),
            jax.ShapeDtypeStruct((s, D_MODEL), BF16),
            jax.ShapeDtypeStruct((s, D_MODEL), BF16),
            jax.ShapeDtypeStruct((s, D_MODEL), BF16),
            jax.ShapeDtypeStruct((s, GROUP_WIDTH), F32)]
    return pl.pallas_call(
        body, name="mix_fwd", grid=(s // tm,), out_shape=outs,
        in_specs=[_rows(tm, D_MODEL), _rows(tm, 3 * CONV_WIDTH), _rows(tm, 2 * D_MODEL)]
        + [_stream_block(tm, d, GROUP_WIDTH) for d in DILATIONS] * 2
        + [_resident((3, CONV_WIDTH)), _resident((1, CONV_WIDTH)), _resident((2, D_MODEL)),
           _resident((CONV_WIDTH, D_MODEL)), _resident((ATTN_WIDTH, D_MODEL)), _resident((D_MODEL, D_MODEL))],
        out_specs=[_rows(tm, D_MODEL), _rows(tm, CONV_WIDTH), _rows(tm, ATTN_WIDTH), _rows(tm, D_MODEL),
                   _rows(tm, D_MODEL), _rows(tm, D_MODEL), _rows(tm, GROUP_WIDTH)],
        scratch_shapes=[pltpu.VMEM((SUBLANES, CONV_WIDTH), F32),
                        pltpu.VMEM((GROUP_WIDTH // LANES * tm, LANES), F32)],
        compiler_params=_params(1))(x, abcv, gates, *o_list, *lse_list, conv_w, conv_b, b_gate, w_pa, w_pb, w_out)


FFN_CHUNK = 512


def _ffn_up_fwd(x1, g, w_up, conv_w, conv_b):
    s = x1.shape[0]
    n = w_up.shape[1]
    tm = ROW_TILE

    def body(x_ref, g_ref, w_ref, cw_ref, cb_ref, h_ref, up0_ref, up_ref, halo_ref):
        @pl.when(pl.program_id(0) == 0)
        def _():
            halo_ref[...] = jnp.zeros_like(halo_ref)

        xv = x_ref[...]
        r = lax.rsqrt(jnp.mean(xv * xv, axis=-1, keepdims=True) + EPS)
        h = (xv * r * g_ref[...]).astype(BF16)
        h_ref[...] = h
        for lo, size in _col_chunks(n, FFN_CHUNK):
            cols = slice(lo, lo + size)
            y = _dot(h, w_ref[:, cols])
            up0_ref[:, cols] = y.astype(BF16)
            halo = halo_ref[:, cols]
            w = cw_ref[:, cols]
            up = w[0:1] * _shift_down(y, 2, halo) + w[1:2] * _shift_down(y, 1, halo) + w[2:3] * y + cb_ref[:, cols]
            up_ref[:, cols] = up.astype(BF16)
            halo_ref[:, cols] = y[tm - SUBLANES:tm]

    return pl.pallas_call(
        body, name="ffn_up_fwd", grid=(s // tm,),
        out_shape=[jax.ShapeDtypeStruct((s, D_MODEL), BF16), jax.ShapeDtypeStruct((s, n), BF16),
                   jax.ShapeDtypeStruct((s, n), BF16)],
        in_specs=[_rows(tm, D_MODEL), _resident((1, D_MODEL)), _resident((D_MODEL, n)), _resident((3, n)),
                  _resident((1, n))],
        out_specs=[_rows(tm, D_MODEL), _rows(tm, n), _rows(tm, n)],
        scratch_shapes=[pltpu.VMEM((SUBLANES, n), F32)],
        compiler_params=_params(1))(x1, g, w_up, conv_w, conv_b)


def _ffn_act_fwd(x1, up, target, w_down, g_final):
    s = x1.shape[0]
    tm = ROW_TILE

    def body(x1_ref, up_ref, tgt_ref, wd_ref, gf_ref, act_ref, dx2_ref, dx2b_ref, dgf_ref, loss_ref):
        @pl.when(pl.program_id(0) == 0)
        def _():
            dgf_ref[...] = jnp.zeros_like(dgf_ref)
            loss_ref[...] = jnp.zeros_like(loss_ref)

        acc = jnp.zeros((tm, D_MODEL), F32)
        for lo, size in _col_chunks(D_FF, FFN_CHUNK):
            gate = up_ref[:, lo:lo + size].astype(F32)
            val = up_ref[:, D_FF + lo:D_FF + lo + size].astype(F32)
            act = (gate * _sigmoid(gate) * val).astype(BF16)
            act_ref[:, lo:lo + size] = act
            acc = acc + _dot(act, wd_ref[lo:lo + size, :])

        x2 = x1_ref[...] + acc
        r = lax.rsqrt(jnp.mean(x2 * x2, axis=-1, keepdims=True) + EPS)
        xn = x2 * r
        gf = gf_ref[...]
        err = xn * gf - tgt_ref[...]
        loss_ref[...] += (0.5 / D_MODEL) * jnp.sum(err * err)
        dy = err * (1.0 / D_MODEL)
        dgf_ref[...] += _colsum(dy * xn)
        dxn = dy * gf
        dx2 = r * (dxn - xn * jnp.mean(dxn * xn, axis=-1, keepdims=True))
        dx2_ref[...] = dx2
        dx2b_ref[...] = dx2.astype(BF16)

    return pl.pallas_call(
        body, name="ffn_act_fwd", grid=(s // tm,),
        out_shape=[jax.ShapeDtypeStruct((s, D_FF), BF16), jax.ShapeDtypeStruct((s, D_MODEL), F32),
                   jax.ShapeDtypeStruct((s, D_MODEL), BF16),
                   jax.ShapeDtypeStruct((1, D_MODEL), F32), jax.ShapeDtypeStruct((1, LANES), F32)],
        in_specs=[_rows(tm, D_MODEL), _rows(tm, 2 * D_FF), _rows(tm, D_MODEL),
                  _resident((D_FF, D_MODEL)), _resident((1, D_MODEL))],
        out_specs=[_rows(tm, D_FF), _rows(tm, D_MODEL), _rows(tm, D_MODEL),
                   pl.BlockSpec((1, D_MODEL), lambda i: (0, 0)), pl.BlockSpec((1, LANES), lambda i: (0, 0))],
        compiler_params=_params(1))(x1, up, target, w_down, g_final)


def _ffn_act_bwd(dx2b, up, w_down):
    s = dx2b.shape[0]
    tm = ROW_TILE

    def body(dx2_ref, up_ref, wd_ref, dup_ref):
        dx2 = dx2_ref[...]
        for lo, size in _col_chunks(D_FF, FFN_CHUNK):
            gate = up_ref[:, lo:lo + size].astype(F32)
            val = up_ref[:, D_FF + lo:D_FF + lo + size].astype(F32)
            dact = _dot_nt(dx2, wd_ref[lo:lo + size, :])
            sg = _sigmoid(gate)
            dup_ref[:, lo:lo + size] = (dact * val * (sg * (1.0 + gate * (1.0 - sg)))).astype(BF16)
            dup_ref[:, D_FF + lo:D_FF + lo + size] = (dact * (gate * sg)).astype(BF16)

    return pl.pallas_call(
        body, name="ffn_act_bwd", grid=(s // tm,),
        out_shape=jax.ShapeDtypeStruct((s, 2 * D_FF), BF16),
        in_specs=[_rows(tm, D_MODEL), _rows(tm, 2 * D_FF), _resident((D_FF, D_MODEL))],
        out_specs=_rows(tm, 2 * D_FF),
        compiler_params=_params(1))(dx2b, up, w_down)


FFN_BWD_ROW_TILE = 256


def _ffn_up_bwd(d_up, up0, w_up, conv_w, x1, g, dres):
    s = x1.shape[0]
    n = w_up.shape[1]
    tm = FFN_BWD_ROW_TILE
    nt = s // tm

    def body(dup_ref, up0_ref, w_ref, cw_ref, x_ref, g_ref, dres_ref,
             dx_ref, dxb_ref, dg_ref, dup0_ref, small_ref, next_ref):
        @pl.when(pl.program_id(0) == 0)
        def _():
            next_ref[...] = jnp.zeros_like(next_ref)
            small_ref[...] = jnp.zeros_like(small_ref)
            dg_ref[...] = jnp.zeros_like(dg_ref)

        dh = jnp.zeros((tm, D_MODEL), F32)
        for lo, size in _col_chunks(n, FFN_CHUNK):
            cols = slice(lo, lo + size)
            dz = dup_ref[:, cols].astype(F32)
            x0 = up0_ref[:, cols].astype(F32)
            nxt = next_ref[:, cols]
            dz1 = _shift_up(dz, 1, nxt)
            dz2 = _shift_up(dz, 2, nxt)
            next_ref[:, cols] = dz[0:SUBLANES]
            small_ref[0:1, cols] += _colsum(dz2 * x0)
            small_ref[1:2, cols] += _colsum(dz1 * x0)
            small_ref[2:3, cols] += _colsum(dz * x0)
            small_ref[3:4, cols] += _colsum(dz)
            w = cw_ref[:, cols]
            dup0 = (w[2:3] * dz + w[1:2] * dz1 + w[0:1] * dz2).astype(BF16)
            dup0_ref[:, cols] = dup0
            dh = dh + _dot_nt(dup0, w_ref[:, cols])
        xv = x_ref[...]
        r = lax.rsqrt(jnp.mean(xv * xv, axis=-1, keepdims=True) + EPS)
        xn = xv * r
        dg_ref[...] += _colsum(dh * xn)
        dxn = dh * g_ref[...]
        dx = dres_ref[...] + r * (dxn - xn * jnp.mean(dxn * xn, axis=-1, keepdims=True))
        dx_ref[...] = dx
        dxb_ref[...] = dx.astype(BF16)

    rows = lambda width: pl.BlockSpec((tm, width), lambda i: (nt - 1 - i, 0))
    return pl.pallas_call(
        body, name="ffn_up_bwd", grid=(nt,),
        out_shape=[jax.ShapeDtypeStruct((s, D_MODEL), F32), jax.ShapeDtypeStruct((s, D_MODEL), BF16),
                   jax.ShapeDtypeStruct((1, D_MODEL), F32), jax.ShapeDtypeStruct((s, n), BF16),
                   jax.ShapeDtypeStruct((SUBLANES, n), F32)],
        in_specs=[rows(n), rows(n), _resident((D_MODEL, n)), _resident((3, n)), rows(D_MODEL),
                  _resident((1, D_MODEL)), rows(D_MODEL)],
        out_specs=[rows(D_MODEL), rows(D_MODEL), pl.BlockSpec((1, D_MODEL), lambda i: (0, 0)), rows(n),
                   pl.BlockSpec((SUBLANES, n), lambda i: (0, 0))],
        scratch_shapes=[pltpu.VMEM((SUBLANES, n), F32)],
        compiler_params=_params(1))(d_up, up0, w_up, conv_w, x1, g, dres)


def _norm_bwd(name, dys, w, pieces, xin, g, dres, stream_dils=None, w_transposed=False):
    s = xin.shape[0]
    tm = ROW_TILE
    nk = len(dys)
    dils = stream_dils or (1,) * nk
    widths = [dy.shape[-1] for dy in dys]
    relayout = [k for k in range(nk) if dils[k] > 1]
    nt = s // tm

    def body(*refs):
        dy_refs = refs[:nk]
        w_ref, x_ref, g_ref, dres_ref = refs[nk:nk + 4]
        dx_ref, dxb_ref, dg_ref = refs[nk + 4:nk + 7]
        nat_refs = refs[nk + 7:nk + 7 + len(relayout)]
        scr = refs[-1]
        i = pl.program_id(0)

        @pl.when(i == 0)
        def _():
            dg_ref[...] = jnp.zeros_like(dg_ref)

        dh = jnp.zeros((tm, D_MODEL), F32)
        for k in range(nk):
            if dils[k] > 1:
                nat_ref = nat_refs[relayout.index(k)]
                for lo, size in _col_chunks(widths[k], GROUP_WIDTH):
                    nat_ref[:, lo:lo + size] = _from_streams(dy_refs[k], scr, dils[k], lo, size).astype(BF16)
                src = nat_ref
            else:
                src = dy_refs[k]
            for first, width, wcol in pieces[k]:
                for lo, size in _col_chunks(width, 512):
                    dy = src[:, first + lo:first + lo + size]
                    if w_transposed:
                        dh = dh + _dot(dy, w_ref[wcol + lo:wcol + lo + size, :])
                    else:
                        dh = dh + _dot_nt(dy, w_ref[:, wcol + lo:wcol + lo + size])
        xv = x_ref[...]
        r = lax.rsqrt(jnp.mean(xv * xv, axis=-1, keepdims=True) + EPS)
        xn = xv * r
        dg_ref[...] += _colsum(dh * xn)
        dxn = dh * g_ref[...]
        dx = dres_ref[...] + r * (dxn - xn * jnp.mean(dxn * xn, axis=-1, keepdims=True))
        dx_ref[...] = dx
        dxb_ref[...] = dx.astype(BF16)

    dy_specs = [(_stream_block(tm, dils[k], widths[k]) if dils[k] > 1 else _rows(tm, widths[k])) for k in range(nk)]
    outs = [jax.ShapeDtypeStruct((s, D_MODEL), F32), jax.ShapeDtypeStruct((s, D_MODEL), BF16),
            jax.ShapeDtypeStruct((1, D_MODEL), F32)]
    outs += [jax.ShapeDtypeStruct((s, widths[k]), BF16) for k in relayout]
    return pl.pallas_call(
        body, name=name, grid=(nt,), out_shape=outs,
        in_specs=dy_specs + [_resident(w.shape), _rows(tm, D_MODEL), _resident((1, D_MODEL)), _rows(tm, D_MODEL)],
        out_specs=[_rows(tm, D_MODEL), _rows(tm, D_MODEL), pl.BlockSpec((1, D_MODEL), lambda i: (0, 0))]
        + [_rows(tm, widths[k]) for k in relayout],
        scratch_shapes=[pltpu.VMEM((GROUP_WIDTH // LANES * tm, LANES), F32)],
        compiler_params=_params(1))(*dys, w, xin, g, dres)


def _mix_bwd(dx1, abcv, gates, ya, yb, yb0, lsetot, conv_w, conv_b, b_gate, w_pa, w_pb, w_out, exchange=()):
    s = dx1.shape[0]
    tm = ROW_TILE
    nt = s // tm
    hb = tm // (2 * SUBLANES)
    nx = len(exchange)

    def body(*refs):
        (dx1_ref, abcv_ref, pre_ref, gates_ref, ya_ref, yb_ref, yb0_ref, lsetot_ref,
         cw_ref, cb_ref, bg_ref, wpa_ref, wpb_ref, wout_ref) = refs[:14]
        part_refs = refs[14:14 + nx]
        (dya_ref, dyb_ref, dgates_ref, dabcv_ref, dyb0_ref, dyl0_ref, dyl1_ref, dyl2_ref, aux0_ref, aux1_ref,
         aux2_ref, sm_gate_ref, sm_conv_ref) = refs[14 + nx:27 + nx]
        recv_refs = refs[27 + nx:27 + 2 * nx]
        next_ref, scr = refs[27 + 2 * nx:29 + 2 * nx]
        sems = refs[29 + 2 * nx:]
        i = pl.program_id(0)

        @pl.when(i == 0)
        def _():
            next_ref[...] = jnp.zeros_like(next_ref)
            sm_gate_ref[...] = jnp.zeros_like(sm_gate_ref)
            sm_conv_ref[...] = jnp.zeros_like(sm_conv_ref)
            for cp in _chip_exchange_copies(part_refs, recv_refs, *sems) if nx else ():
                cp.start()

        not_first = (i < nt - 1).astype(F32)
        dm = _dot_nt(dx1_ref[...].astype(BF16), wout_ref[...])
        sa = _sigmoid(gates_ref[:, 0:D_MODEL].astype(F32) + bg_ref[0:1, :])
        sb = _sigmoid(gates_ref[:, D_MODEL:2 * D_MODEL].astype(F32) + bg_ref[1:2, :])
        dya = (dm * sa).astype(BF16)
        dyb = (dm * sb).astype(BF16)
        dya_ref[...] = dya
        dyb_ref[...] = dyb
        dga = dm * ya_ref[...].astype(F32) * (sa * (1.0 - sa))
        dgb = dm * yb_ref[...].astype(F32) * (sb * (1.0 - sb))
        dgates_ref[:, 0:D_MODEL] = dga.astype(BF16)
        dgates_ref[:, D_MODEL:2 * D_MODEL] = dgb.astype(BF16)
        sm_gate_ref[0:1, :] += _colsum(dga)
        sm_gate_ref[1:2, :] += _colsum(dgb)

        dya0 = _dot_nt(dya, wpa_ref[...])
        ab = abcv_ref[:, 0:CONV_WIDTH].astype(F32)
        ac = abcv_ref[:, CONV_WIDTH:2 * CONV_WIDTH].astype(F32)
        av = abcv_ref[:, 2 * CONV_WIDTH:3 * CONV_WIDTH].astype(F32)
        pre = pre_ref[...].astype(F32) * not_first
        halo_u = (pre[:, CONV_WIDTH:2 * CONV_WIDTH] * pre[:, 2 * CONV_WIDTH:3 * CONV_WIDTH])[SUBLANES:2 * SUBLANES]
        w = cw_ref[...]
        _, cv, u, sh1, sh2 = _conv_branch(ab, ac, av, halo_u, w, cb_ref[...])
        dcv = dya0 * ab
        sm_conv_ref[0:1, :] += _colsum(dcv * sh2)
        sm_conv_ref[1:2, :] += _colsum(dcv * sh1)
        sm_conv_ref[2:3, :] += _colsum(dcv * u)
        sm_conv_ref[3:4, :] += _colsum(dcv)
        nxt = next_ref[...]
        du = w[2:3] * dcv + w[1:2] * _shift_up(dcv, 1, nxt) + w[0:1] * _shift_up(dcv, 2, nxt)
        next_ref[...] = dcv[0:SUBLANES]
        dabcv_ref[:, 0:CONV_WIDTH] = (dya0 * cv).astype(BF16)
        dabcv_ref[:, CONV_WIDTH:2 * CONV_WIDTH] = (du * av).astype(BF16)
        dabcv_ref[:, 2 * CONV_WIDTH:3 * CONV_WIDTH] = (du * ac).astype(BF16)

        head_r = lax.broadcasted_iota(jnp.int32, (GROUP_WIDTH, GROUP_WIDTH), 0) // HEAD_DIM
        head_c = lax.broadcasted_iota(jnp.int32, (GROUP_WIDTH, GROUP_WIDTH), 1) // HEAD_DIM
        same_head = (head_r == head_c).astype(BF16)
        prod = jnp.zeros((tm, GROUP_WIDTH), F32)
        dyb0s = []
        for g in range(3):
            cols = slice(g * GROUP_WIDTH, (g + 1) * GROUP_WIDTH)
            dyb0 = _dot_nt(dyb, wpb_ref[cols, :])
            dyb0_ref[:, cols] = dyb0.astype(BF16)
            dyb0s.append(dyb0)
            prod = prod + dyb0 * yb0_ref[:, cols].astype(F32)
        hi = prod.astype(BF16)
        mid = (prod - hi.astype(F32)).astype(BF16)
        lo = (prod - hi.astype(F32) - mid.astype(F32)).astype(BF16)
        delta = _dot(hi, same_head) + _dot(mid, same_head) + _dot(lo, same_head)
        lsetot = lsetot_ref[...]
        for g, (dy_ref, aux_ref) in enumerate(zip((dyl0_ref, dyl1_ref, dyl2_ref), (aux0_ref, aux1_ref, aux2_ref))):
            d = DILATIONS[g]
            _to_streams(dyb0s[g], scr, dy_ref, d, 0)
            _to_streams(lsetot, scr, aux_ref, d, 0)
            _to_streams(delta, scr, aux_ref, d, GROUP_WIDTH)

        if nx:
            @pl.when(i == nt - 1)
            def _():
                for cp in _chip_exchange_copies(part_refs, recv_refs, *sems):
                    cp.wait()

    rev = lambda i: (nt - 1 - i, 0)
    pre = lambda i: (jnp.maximum((nt - 1 - i) * hb - 1, 0), 0)
    rows = lambda width: pl.BlockSpec((tm, width), rev)
    outs = [jax.ShapeDtypeStruct((s, D_MODEL), BF16), jax.ShapeDtypeStruct((s, D_MODEL), BF16),
            jax.ShapeDtypeStruct((s, 2 * D_MODEL), BF16), jax.ShapeDtypeStruct((s, 3 * CONV_WIDTH), BF16),
            jax.ShapeDtypeStruct((s, ATTN_WIDTH), BF16)]
    outs += [jax.ShapeDtypeStruct((d, s // d, GROUP_WIDTH), BF16) for d in DILATIONS]
    outs += [jax.ShapeDtypeStruct((d, s // d, 2 * GROUP_WIDTH), F32) for d in DILATIONS]
    outs += [jax.ShapeDtypeStruct((SUBLANES, D_MODEL), F32), jax.ShapeDtypeStruct((SUBLANES, CONV_WIDTH), F32)]
    outs += [jax.ShapeDtypeStruct((3,) + a.shape[1:], a.dtype) for a in exchange]
    return pl.pallas_call(
        body, name="mix_bwd", grid=(nt,), out_shape=outs,
        in_specs=[rows(D_MODEL), rows(3 * CONV_WIDTH), pl.BlockSpec((2 * SUBLANES, 3 * CONV_WIDTH), pre),
                  rows(2 * D_MODEL), rows(D_MODEL), rows(D_MODEL), rows(ATTN_WIDTH), rows(GROUP_WIDTH),
                  _resident((3, CONV_WIDTH)), _resident((1, CONV_WIDTH)), _resident((2, D_MODEL)),
                  _resident((CONV_WIDTH, D_MODEL)), _resident((ATTN_WIDTH, D_MODEL)), _resident((D_MODEL, D_MODEL))]
        + [_ANY] * nx,
        out_specs=[rows(D_MODEL), rows(D_MODEL), rows(2 * D_MODEL), rows(3 * CONV_WIDTH), rows(ATTN_WIDTH)]
        + [_rev_stream_block(tm, d, GROUP_WIDTH, nt) for d in DILATIONS]
        + [_rev_stream_block(tm, d, 2 * GROUP_WIDTH, nt) for d in DILATIONS]
        + [pl.BlockSpec((SUBLANES, D_MODEL), lambda i: (0, 0)), pl.BlockSpec((SUBLANES, CONV_WIDTH), lambda i: (0, 0))]
        + [_ANY] * nx,
        scratch_shapes=[pltpu.VMEM((SUBLANES, CONV_WIDTH), F32),
                        pltpu.VMEM((GROUP_WIDTH // LANES * tm, LANES), F32)] + (_dma_sems(3 * nx) if nx else []),
        compiler_params=_params(1))(dx1, abcv, abcv, gates, ya, yb, yb0, lsetot,
                                    conv_w, conv_b, b_gate, w_pa, w_pb, w_out, *exchange)


def _attn_bwd(qkv, dy, aux, gi, exchange=()):
    d, length, _ = qkv.shape
    nb = length // ATTN_BLOCK
    q = ATTN_BLOCK
    gw = GROUP_WIDTH
    g = min(ATTN_BLOCKS_PER_STEP, nb)
    assert g >= 2 and nb % g == 0
    ns = nb // g
    lag = 1 if ns > 1 else 0
    tail = (g - 1) * q
    nx = len(exchange)

    def body(*refs):
        q_ref, kp_ref, kc_ref, vp_ref, vc_ref, dy_ref, aux_ref = refs[:7]
        part_refs = refs[7:7 + nx]
        out_ref = refs[7 + nx]
        recv_refs = refs[8 + nx:8 + 2 * nx]
        dq_ref, dkv_ref, bias_ref = refs[8 + 2 * nx:11 + 2 * nx]
        sems = refs[11 + 2 * nx:]
        n = pl.program_id(1)

        @pl.when((pl.program_id(0) == 0) & (n == 0))
        def _():
            _store_band_biases(bias_ref)
            for cp in _chip_exchange_copies(part_refs, recv_refs, *sems) if nx else ():
                cp.start()

        if nx:
            @pl.when((pl.program_id(0) == d - 1) & (n == ns - 1 + lag))
            def _():
                for cp in _chip_exchange_copies(part_refs, recv_refs, *sems):
                    cp.wait()

        def emit(rows):
            out_ref[rows, gw:2 * gw] = dkv_ref[0, rows].astype(BF16)
            out_ref[rows, 2 * gw:3 * gw] = dkv_ref[1, rows].astype(BF16)

        if lag:
            @pl.when(n > 0)
            def _():
                out_ref[:, 0:gw] = dq_ref[...].astype(BF16)
                emit(slice(0, tail))

            @pl.when(n == ns)
            def _():
                emit(slice(tail, g * q))

        @pl.when(n < ns)
        def _():
            kfull = jnp.concatenate([kp_ref[...], kc_ref[...]], axis=0)
            vfull = jnp.concatenate([vp_ref[...], vc_ref[...]], axis=0)
            for j in range(g):
                rows = slice(j * q, (j + 1) * q)
                qs = _stack_heads(q_ref[rows, :])
                dys = _stack_heads(dy_ref[rows, :])
                k2 = kfull[j * q:(j + 2) * q]
                v2 = vfull[j * q:(j + 2) * q]
                lse = _per_head_col(aux_ref[rows, 0:gw])
                delta = _per_head_col(aux_ref[rows, gw:2 * gw])
                bias = jnp.where(n == 0, bias_ref[1], bias_ref[0]) if j == 0 else bias_ref[0]
                p = jnp.exp(_dot_nt(qs, k2) * ATTN_SCALE + bias - lse)
                dp = _dot_nt(dys, v2)
                ds = (p * (dp - delta) * ATTN_SCALE).astype(BF16)
                dq_j = _unstack_heads(_dot(ds, k2))
                dk2 = _dot_tn(ds, qs)
                dv2 = _dot_tn(p.astype(BF16), dys)
                if j == 0:
                    @pl.when(n > 0)
                    def _():
                        out_ref[tail:g * q, gw:2 * gw] = (dkv_ref[0, tail:g * q] + dk2[0:q]).astype(BF16)
                        out_ref[tail:g * q, 2 * gw:3 * gw] = (dkv_ref[1, tail:g * q] + dv2[0:q]).astype(BF16)
                else:
                    dkv_ref[0, (j - 1) * q:j * q] += dk2[0:q]
                    dkv_ref[1, (j - 1) * q:j * q] += dv2[0:q]
                dkv_ref[0, rows] = dk2[q:2 * q]
                dkv_ref[1, rows] = dv2[q:2 * q]
                dq_ref[rows, :] = dq_j
            if not lag:
                out_ref[:, 0:gw] = dq_ref[...].astype(BF16)
                emit(slice(0, g * q))

    cur, prev = _attn_block_specs(g, nb, clamp_last=True)
    return pl.pallas_call(
        body, name=f"attn_bwd_g{gi}", grid=(d, ns + lag),
        out_shape=[jax.ShapeDtypeStruct((d, length, 3 * gw), BF16)]
        + [jax.ShapeDtypeStruct((3,) + a.shape[1:], a.dtype) for a in exchange],
        in_specs=[cur(0), prev(1), cur(1), prev(2), cur(2), cur(0), cur(0, 2 * gw)] + [_ANY] * nx,
        out_specs=[pl.BlockSpec((None, g * q, 3 * gw), lambda r, n: (r, jnp.maximum(n - lag, 0), 0))] + [_ANY] * nx,
        scratch_shapes=[pltpu.VMEM((g * q, gw), F32), pltpu.VMEM((2, g * q, gw), F32),
                        pltpu.VMEM((2, HEADS_PER_GROUP * q, 2 * q), F32)] + (_dma_sems(3 * nx) if nx else []),
        compiler_params=_params(2))(qkv, qkv, qkv, qkv, qkv, dy, aux, *exchange)


def _matmul_tn(name, a, b, col_tile=1024, row_tile=2048, slabs=0):
    s, k = a.shape
    n = b.shape[1]
    tk = min(row_tile, s)
    tn = col_tile
    steps = s // tk

    def body(a_ref, b_ref, o_ref, acc_ref):
        t = pl.program_id(1)

        @pl.when(t == 0)
        def _():
            acc_ref[...] = jnp.zeros_like(acc_ref)

        acc_ref[...] += _dot_tn(a_ref[...], b_ref[...])

        @pl.when(t == steps - 1)
        def _():
            if slabs:
                for q in range(per_tile):
                    o_ref[q] = acc_ref[:, q * width:(q + 1) * width].astype(BF16)
            else:
                o_ref[...] = acc_ref[...].astype(BF16)

    if slabs:
        width = n // slabs
        per_tile = tn // width
        out_shape = jax.ShapeDtypeStruct((slabs, k, width), BF16)
        out_spec = pl.BlockSpec((per_tile, k, width), lambda j, t: (j, 0, 0))
    else:
        out_shape = jax.ShapeDtypeStruct((k, n), BF16)
        out_spec = pl.BlockSpec((k, tn), lambda j, t: (0, j))
    return pl.pallas_call(
        body, name=name, grid=(n // tn, steps), out_shape=out_shape,
        in_specs=[pl.BlockSpec((tk, k), lambda j, t: (t, 0)), pl.BlockSpec((tk, tn), lambda j, t: (t, j))],
        out_specs=out_spec, scratch_shapes=[pltpu.VMEM((k, tn), F32)],
        compiler_params=_params(2))(a, b)


def _weight_allgather(bigs, smalls):
    nb, ns = len(bigs), len(smalls)
    n_sems = 3 * (2 * nb + ns)

    def body(*refs):
        big_refs, small_refs = refs[:nb], refs[nb:nb + ns]
        big_outs, small_outs = refs[nb + ns:2 * nb + ns], refs[2 * nb + ns:2 * (nb + ns)]
        send_sems, recv_sems = refs[2 * (nb + ns):]
        x, y, c, chips = _mesh_position()
        me = 2 * x + y
        sibling = (x, y, 1 - c)

        def copy(k, src, dst, to):
            return pltpu.make_async_remote_copy(src_ref=src, dst_ref=dst, send_sem=send_sems.at[k],
                                                recv_sem=recv_sems.at[k], device_id=to, device_id_type=MESH_ID)

        halves = [r.shape[1] // 2 for r in big_refs]
        first = []
        for j, (px, py) in enumerate(chips):
            for b in range(nb):
                mine = _half_rows(c, halves[b])
                first.append(copy(3 * b + j, big_refs[b].at[0, mine], big_outs[b].at[me, mine], (px, py, c)))
            for s in range(ns):
                first.append(copy(3 * (2 * nb + s) + j, small_refs[s].at[0], small_outs[s].at[me], (px, py, c)))
        for cp in first:
            cp.start()
        passed = []
        for j, (px, py) in enumerate(chips):
            for b in range(nb):
                landed = big_outs[b].at[2 * px + py, _half_rows(c, halves[b])]
                copy(3 * b + j, landed, landed, (px, py, c)).wait_recv()
                fwd = copy(3 * (nb + b) + j, landed, landed, sibling)
                fwd.start()
                passed.append(fwd)
        for j, (px, py) in enumerate(chips):
            for s in range(ns):
                landed = small_outs[s].at[2 * px + py]
                copy(3 * (2 * nb + s) + j, landed, landed, (px, py, c)).wait_recv()
            for b in range(nb):
                from_sibling = big_outs[b].at[2 * px + py, _half_rows(1 - c, halves[b])]
                copy(3 * (nb + b) + j, from_sibling, from_sibling, sibling).wait_recv()
        for cp in first + passed:
            cp.wait_send()

    return pl.pallas_call(
        body, name="weight_allgather",
        out_shape=[jax.ShapeDtypeStruct((N_CHIPS,) + a.shape[1:], a.dtype) for a in list(bigs) + list(smalls)],
        in_specs=[_ANY] * (nb + ns), out_specs=[_ANY] * (nb + ns),
        scratch_shapes=[pltpu.SemaphoreType.DMA((n_sems,)), pltpu.SemaphoreType.DMA((n_sems,))],
    )(*bigs, *smalls)


def _sibling_swap_halves(name, slabs):
    na = len(slabs)

    def body(*refs):
        src_refs, out_refs = refs[:na], refs[na:2 * na]
        send_sems, recv_sems = refs[2 * na:]
        x, y, c, _ = _mesh_position()
        cps = []
        for a in range(na):
            theirs = _half_rows(1 - c, src_refs[a].shape[1] // 2)
            cps.append(pltpu.make_async_remote_copy(
                src_ref=src_refs[a].at[:, theirs, :], dst_ref=out_refs[a], send_sem=send_sems.at[a],
                recv_sem=recv_sems.at[a], device_id=(x, y, 1 - c), device_id_type=MESH_ID))
        for cp in cps:
            cp.start()
        for cp in cps:
            cp.wait()

    return pl.pallas_call(
        body, name=name,
        out_shape=[jax.ShapeDtypeStruct((a.shape[0], a.shape[1] // 2, a.shape[2]), a.dtype) for a in slabs],
        in_specs=[_ANY] * na, out_specs=[_ANY] * na,
        scratch_shapes=[pltpu.SemaphoreType.DMA((na,)), pltpu.SemaphoreType.DMA((na,))])(*slabs)


def _chip_exchange(partials):
    na = len(partials)

    def body(*refs):
        cps = _chip_exchange_copies(refs[:na], refs[na:2 * na], *refs[2 * na:])
        for cp in cps:
            cp.start()
        for cp in cps:
            cp.wait()

    return pl.pallas_call(
        body, name="grad_chip_exchange",
        out_shape=[jax.ShapeDtypeStruct((3,) + a.shape[1:], a.dtype) for a in partials],
        in_specs=[_ANY] * na, out_specs=[_ANY] * na, scratch_shapes=_dma_sems(3 * na))(*partials)


def _sibling_share(halves):
    na = len(halves)

    def body(*refs):
        out_refs = refs[na:2 * na]
        send_sems, recv_sems = refs[2 * na:]
        x, y, c, _ = _mesh_position()
        cps = []
        for a in range(na):
            mine = out_refs[a].at[0, _half_rows(c, out_refs[a].shape[1] // 2)]
            cps.append(pltpu.make_async_remote_copy(src_ref=mine, dst_ref=mine, send_sem=send_sems.at[a],
                                                    recv_sem=recv_sems.at[a], device_id=(x, y, 1 - c),
                                                    device_id_type=MESH_ID))
        for cp in cps:
            cp.start()
        for a, cp in enumerate(cps):
            cp.wait_send()
            theirs = out_refs[a].at[0, _half_rows(1 - c, out_refs[a].shape[1] // 2)]
            pltpu.make_async_remote_copy(src_ref=theirs, dst_ref=theirs, send_sem=send_sems.at[a],
                                         recv_sem=recv_sems.at[a], device_id=(x, y, 1 - c),
                                         device_id_type=MESH_ID).wait_recv()

    return pl.pallas_call(
        body, name="grad_sibling_share", out_shape=[jax.ShapeDtypeStruct(a.shape, a.dtype) for a in halves],
        in_specs=[_ANY] * na, out_specs=[_ANY] * na, input_output_aliases={a: a for a in range(na)},
        scratch_shapes=[pltpu.SemaphoreType.DMA((na,)), pltpu.SemaphoreType.DMA((na,))])(*halves)


def _add_sibling(name, slab, received, core):
    n, rows, cols = slab.shape
    half = rows // 2

    def body(core_ref, a_ref, b_ref, o_ref):
        o_ref[...] = (a_ref[...].astype(F32) + b_ref[...].astype(F32)).astype(BF16)

    grid_spec = pltpu.PrefetchScalarGridSpec(
        num_scalar_prefetch=1, grid=(n,),
        in_specs=[pl.BlockSpec((None, half, cols), lambda s, core_ref: (s, core_ref[0], 0)),
                  pl.BlockSpec((None, half, cols), lambda s, core_ref: (s, 0, 0))],
        out_specs=pl.BlockSpec((None, half, cols), lambda s, core_ref: (s, 0, 0)))
    return pl.pallas_call(body, name=name, grid_spec=grid_spec,
                          out_shape=jax.ShapeDtypeStruct((n, half, cols), BF16),
                          compiler_params=_params(1))(core, slab, received)


def _sum_chips(name, partial, received, chip_core):
    _, half, cols = partial.shape

    def body(cc_ref, own_ref, recv_ref, o_ref):
        acc = own_ref[...].astype(F32)
        for k in range(3):
            acc = acc + recv_ref[k].astype(F32)
        o_ref[...] = acc

    grid_spec = pltpu.PrefetchScalarGridSpec(
        num_scalar_prefetch=1, grid=(1,),
        in_specs=[pl.BlockSpec((None, half, cols), lambda i, cc_ref: (cc_ref[0], 0, 0)),
                  pl.BlockSpec((3, half, cols), lambda i, cc_ref: (0, 0, 0))],
        out_specs=pl.BlockSpec((None, half, cols), lambda i, cc_ref: (0, cc_ref[1], 0)))
    return pl.pallas_call(body, name=name, grid_spec=grid_spec,
                          out_shape=jax.ShapeDtypeStruct((1, 2 * half, cols), F32),
                          compiler_params=_params(1))(chip_core, partial, received)


def _adam_math(w, g, m, v):
    nm = ADAM_B1 * m + (1.0 - ADAM_B1) * g
    nv = ADAM_B2 * v + (1.0 - ADAM_B2) * jnp.square(g)
    m_hat = nm / (1.0 - ADAM_B1 ** ADAM_STEP)
    v_hat = nv / (1.0 - ADAM_B2 ** ADAM_STEP)
    delta = -ADAM_LR * (m_hat / (jnp.sqrt(v_hat) + ADAM_EPS) + ADAM_WD * w)
    return delta, nm, nv


def _adamw(name, w, g, m, v):
    _, rows, cols = w.shape
    tr = next(t for t in (736, 512, 384, 352, 256, 128, 64, 32, 16, 8) if rows % t == 0)

    def body(w_ref, g_ref, m_ref, v_ref, d_ref, nm_ref, nv_ref):
        d_ref[...], nm_ref[...], nv_ref[...] = _adam_math(w_ref[...], g_ref[...], m_ref[...], v_ref[...])

    spec = pl.BlockSpec((None, tr, cols), lambda i: (0, i, 0))
    return pl.pallas_call(
        body, name=name, grid=(rows // tr,), out_shape=[jax.ShapeDtypeStruct(w.shape, F32)] * 3,
        in_specs=[spec] * 4, out_specs=[spec] * 3, compiler_params=_params(1))(w, g, m, v)


SMALL_PARAMS = ("norm_mix_g", "b_gate", "conv_a_w", "conv_a_b", "norm_ffn_g", "ffn_conv_w", "ffn_conv_b", "final_norm_g")


def _small_update(partials, params, moments_m, moments_v):
    na = len(partials)
    npar = len(SMALL_PARAMS)

    def body(*refs):
        in_refs = refs[:na]
        w_refs = refs[na:na + npar]
        m_refs = refs[na + npar:na + 2 * npar]
        v_refs = refs[na + 2 * npar:na + 3 * npar]
        pos = na + 3 * npar
        loss_ref = refs[pos]
        out_refs = refs[pos + 1:pos + 1 + 4 * npar]
        pos += 1 + 4 * npar
        acc_refs = refs[pos:pos + na]
        recv_refs = refs[pos + na:pos + 4 * na]
        send_sems, recv_sems = refs[pos + 4 * na:]
        x, y, c, _ = _mesh_position()
        chip = 2 * x + y
        for a in range(na):
            acc_refs[a][...] = in_refs[a][...]
        for stage, peer in enumerate(((x, y, 1 - c), (x, 1 - y, c), (1 - x, y, c))):
            cps = []
            for a in range(na):
                k = stage * na + a
                cps.append(pltpu.make_async_remote_copy(src_ref=acc_refs[a], dst_ref=recv_refs[k], send_sem=send_sems.at[k],
                                                        recv_sem=recv_sems.at[k], device_id=peer, device_id_type=MESH_ID))
            for cp in cps:
                cp.start()
            for cp in cps:
                cp.wait()
            for a in range(na):
                acc_refs[a][...] = acc_refs[a][...] + recv_refs[stage * na + a][...]

        mix, ffn, fin, gate, conv, ffnc, loss = acc_refs
        loss_ref[...] = loss[...]

        def cols(width):
            return pl.ds(pl.multiple_of(chip * width, LANES), width)

        grads = {
            "norm_mix_g": mix[...], "norm_ffn_g": ffn[...], "final_norm_g": fin[...],
            "b_gate": gate[0:2, cols(D_MODEL // N_CHIPS)],
            "conv_a_w": conv[0:3, cols(CONV_WIDTH // N_CHIPS)], "conv_a_b": conv[3:4, :],
            "ffn_conv_w": ffnc[0:3, cols(2 * D_FF // N_CHIPS)], "ffn_conv_b": ffnc[3:4, :]}
        for i, name in enumerate(SMALL_PARAMS):
            g = grads[name]
            if len(w_refs[i].shape) == 3:
                results = (g,) + _adam_math(w_refs[i][0], g, m_refs[i][0], v_refs[i][0])
                for o_ref, val in zip(out_refs[4 * i:4 * i + 4], results):
                    o_ref[0] = val
            else:
                results = (g,) + _adam_math(w_refs[i][...], g, m_refs[i][...], v_refs[i][...])
                for o_ref, val in zip(out_refs[4 * i:4 * i + 4], results):
                    o_ref[...] = val

    outs = [jax.ShapeDtypeStruct(partials[-1].shape, F32)]
    for w in params:
        outs += [jax.ShapeDtypeStruct(w.shape, F32)] * 4
    scratch = [pltpu.VMEM(p.shape, F32) for p in partials]
    scratch += [pltpu.VMEM(p.shape, F32) for _ in range(3) for p in partials]
    scratch += [pltpu.SemaphoreType.DMA((3 * na,)), pltpu.SemaphoreType.DMA((3 * na,))]
    n_in = na + 3 * npar
    return pl.pallas_call(
        body, name="small_update", out_shape=outs, in_specs=[_VMEM] * n_in, out_specs=[_VMEM] * len(outs),
        scratch_shapes=scratch)(*partials, *params, *moments_m, *moments_v)


def _gathered_columns(g):
    return jnp.transpose(g, (1, 0, 2)).reshape(g.shape[1], N_CHIPS * g.shape[2])


def _column_slabs(full):
    k, n = full.shape
    return jnp.transpose(full.reshape(k, N_CHIPS, n // N_CHIPS), (1, 0, 2))


def kernel(x, norm_mix_g, w_in, b_gate, conv_a_w, conv_a_b, w_proj_a, w_proj_b, w_out, norm_ffn_g, w_up, ffn_conv_w, ffn_conv_b, w_down, final_norm_g, loss_target, m_norm_mix_g, m_w_in, m_b_gate, m_conv_a_w, m_conv_a_b, m_w_proj_a, m_w_proj_b, m_w_out, m_norm_ffn_g, m_w_up, m_ffn_conv_w, m_ffn_conv_b, m_w_down, m_final_norm_g, v_norm_mix_g, v_w_in, v_b_gate, v_conv_a_w, v_conv_a_b, v_w_proj_a, v_w_proj_b, v_w_out, v_norm_ffn_g, v_w_up, v_ffn_conv_w, v_ffn_conv_b, v_w_down, v_final_norm_g):
    chip = (2 * lax.axis_index("x") + lax.axis_index("y")).astype(jnp.int32)
    core = lax.axis_index("c").astype(jnp.int32)
    core_arr = core.reshape(1)
    chip_core = jnp.stack([chip, core])
    xs, target = x[0], loss_target[0]
    g_final = final_norm_g.reshape(1, D_MODEL)

    def own_slot(gathered, own):
        return lax.dynamic_update_slice(gathered, own, (chip, 0, 0))

    def reduce_to_shards(names, slabs, exchange_in):
        from_sibling = _sibling_swap_halves("grad_swap_" + names[0], slabs)
        partials = [_add_sibling("grad_add_" + n, s, r, core_arr) for n, s, r in zip(names, slabs, from_sibling)]
        received, rest = exchange_in(partials)
        halves = [_sum_chips("grad_sum_" + n, p, r, chip_core) for n, p, r in zip(names, partials, received)]
        return halves, rest

    w_in_t, m_w_in_t, v_w_in_t = (jnp.swapaxes(a, 1, 2) for a in (w_in, m_w_in, v_w_in))
    w_in_tb = w_in_t.astype(BF16)
    (g_in,) = _weight_allgather([w_in_tb], [])
    w_in_full_t = own_slot(g_in, w_in_tb).reshape(D_IN, D_MODEL)
    later_w = [w_proj_a, w_proj_b, w_out, w_up, w_down]
    later_b = [w.astype(BF16) for w in later_w]
    small_sharded = [b_gate, conv_a_w, ffn_conv_w]
    fwd = _inproj_fwd(xs, norm_mix_g, w_in_full_t, later_b, small_sharded)
    h1, abcv, gates, qkv0, qkv1, qkv2 = fwd[:6]
    gathered_big, gathered_small = fwd[6:11], fwd[11:14]
    qkvs = (qkv0, qkv1, qkv2)
    attn0 = _attn_fwd(qkv0, 0, forward=gathered_big)
    attn = [attn0[:2], _attn_fwd(qkv1, 1), _attn_fwd(qkv2, 2)]
    g_pa, g_pb, g_out, g_up, g_down = [own_slot(g, own) for g, own in zip(attn0[2:], later_b)]
    g_bgate, g_convw, g_ffnw = [own_slot(g, own) for g, own in zip(gathered_small, small_sharded)]
    w_pa_full, w_pb_full, w_up_full = _gathered_columns(g_pa), _gathered_columns(g_pb), _gathered_columns(g_up)
    w_out_full, w_down_full = g_out.reshape(D_MODEL, D_MODEL), g_down.reshape(D_FF, D_MODEL)
    b_gate_full, conv_w_full, ffn_w_full = (_gathered_columns(g) for g in (g_bgate, g_convw, g_ffnw))

    x1, ya0, yb0, mrg, ya, yb, lsetot = _mix_fwd(
        xs, abcv, gates, [a[0] for a in attn], [a[1] for a in attn], conv_w_full, conv_a_b, b_gate_full,
        w_pa_full, w_pb_full, w_out_full)
    h2, up0, up = _ffn_up_fwd(x1, norm_ffn_g, w_up_full, ffn_w_full, ffn_conv_b)
    act, dx2, dx2b, d_g_final, loss = _ffn_act_fwd(x1, up, target, w_down_full, g_final)

    d_up = _ffn_act_bwd(dx2b, up, w_down_full)
    d_w_down = _matmul_tn("dw_down", act, dx2b, col_tile=512)
    dx1, dx1b, d_g_ffn, d_up0, ffn_small = _ffn_up_bwd(d_up, up0, w_up_full, ffn_w_full, x1, norm_ffn_g, dx2)
    d_w_up = _matmul_tn("dw_up", h2, d_up0, col_tile=2 * D_FF // N_CHIPS, slabs=N_CHIPS)

    def behind_mix_bwd(partials):
        res = _mix_bwd(dx1, abcv, gates, ya, yb, yb0, lsetot, conv_w_full, conv_a_b, b_gate_full,
                       w_pa_full, w_pb_full, w_out_full, exchange=partials)
        return res[13:], res[:13]

    halves_ffn, mix_res = reduce_to_shards(
        ("w_up", "w_down"), [d_w_up, d_w_down.reshape(N_CHIPS, D_FF // N_CHIPS, D_MODEL)], behind_mix_bwd)
    (d_ya, d_yb, d_gates, d_abcv, d_yb0, dyl0, dyl1, dyl2, aux0, aux1, aux2, gate_small, conv_small) = mix_res
    d_w_out = _matmul_tn("dw_out", mrg, dx1b)
    d_w_pa = _matmul_tn("dw_proj_a", ya0, d_ya, slabs=N_CHIPS)
    d_w_pb = _matmul_tn("dw_proj_b", yb0, d_yb, slabs=N_CHIPS)

    def behind_attn_bwd(partials):
        res = _attn_bwd(qkv0, dyl0, aux0, 0, exchange=partials)
        return res[1:], res[0]

    halves_mix, d_qkv0 = reduce_to_shards(
        ("w_proj_a", "w_proj_b", "w_out"),
        [d_w_pa, d_w_pb, d_w_out.reshape(N_CHIPS, D_MODEL // N_CHIPS, D_MODEL)], behind_attn_bwd)
    (d_qkv1,), (d_qkv2,) = _attn_bwd(qkv1, dyl1, aux1, 1), _attn_bwd(qkv2, dyl2, aux2, 2)

    dq = [d_qkv0[0], d_qkv1, d_qkv2]
    group_pieces = [[(j * GROUP_WIDTH, GROUP_WIDTH, base + g * GROUP_WIDTH) for j, base in enumerate((COL_Q, COL_K, COL_V))]
                    for g in range(3)]
    grad_x, _, d_g_mix, nat1, nat2 = _norm_bwd(
        "inproj_bwd", [d_abcv, d_gates] + dq, w_in_full_t,
        [[(0, 3 * CONV_WIDTH, COL_ABCV)], [(0, 2 * D_MODEL, COL_GATES)]] + group_pieces,
        xs, norm_mix_g, dx1, stream_dils=(1, 1, 1, 4, 16), w_transposed=True)

    d_w_abcv = _matmul_tn("dw_in_abcv", d_abcv, h1)
    d_w_gates = _matmul_tn("dw_in_gates", d_gates, h1)
    d_w_groups = [_matmul_tn(f"dw_in_qkv{g}", t, h1) for g, t in enumerate((dq[0], nat1, nat2))]
    gw = GROUP_WIDTH
    d_w_in_t = jnp.concatenate(
        [d_w_abcv] + [d_w_groups[g][j * gw:(j + 1) * gw] for j in range(3) for g in range(3)] + [d_w_gates], axis=0)
    halves_in, _ = reduce_to_shards(("w_in",), [d_w_in_t.reshape(N_CHIPS, D_IN // N_CHIPS, D_MODEL)],
                                    lambda partials: (_chip_exchange(partials), None))

    big_names = ("w_in", "w_proj_a", "w_proj_b", "w_out", "w_up", "w_down")
    big_grads = _sibling_share(halves_in + halves_mix + halves_ffn)
    big_w = dict(w_in=w_in_t, w_proj_a=w_proj_a, w_proj_b=w_proj_b, w_out=w_out, w_up=w_up, w_down=w_down)
    big_m = dict(w_in=m_w_in_t, w_proj_a=m_w_proj_a, w_proj_b=m_w_proj_b, w_out=m_w_out, w_up=m_w_up, w_down=m_w_down)
    big_v = dict(w_in=v_w_in_t, w_proj_a=v_w_proj_a, w_proj_b=v_w_proj_b, w_out=v_w_out, w_up=v_w_up, w_down=v_w_down)

    fin_w, fin_m, fin_v = (a.reshape(1, D_MODEL) for a in (final_norm_g, m_final_norm_g, v_final_norm_g))
    small_w = [norm_mix_g, b_gate, conv_a_w, conv_a_b, norm_ffn_g, ffn_conv_w, ffn_conv_b, fin_w]
    small_m = [m_norm_mix_g, m_b_gate, m_conv_a_w, m_conv_a_b, m_norm_ffn_g, m_ffn_conv_w, m_ffn_conv_b, fin_m]
    small_v = [v_norm_mix_g, v_b_gate, v_conv_a_w, v_conv_a_b, v_norm_ffn_g, v_ffn_conv_w, v_ffn_conv_b, fin_v]
    small_out = _small_update([d_g_mix, d_g_ffn, d_g_final, gate_small, conv_small, ffn_small, loss],
                              small_w, small_m, small_v)
    total_loss = small_out[0][0, 0]

    grads, delta, new_m, new_v = {}, {}, {}, {}
    for i, n in enumerate(SMALL_PARAMS):
        vals = small_out[1 + 4 * i:5 + 4 * i]
        if n == "final_norm_g":
            vals = [a.reshape(D_MODEL) for a in vals]
        grads[n], delta[n], new_m[n], new_v[n] = vals
    for n, g in zip(big_names, big_grads):
        vals = (g,) + tuple(_adamw("adamw_" + n, big_w[n], g, big_m[n], big_v[n]))
        if n == "w_in":
            vals = [jnp.swapaxes(a, 1, 2) for a in vals]
        grads[n], delta[n], new_m[n], new_v[n] = vals

    names = ["norm_mix_g", "w_in", "b_gate", "conv_a_w", "conv_a_b", "w_proj_a", "w_proj_b", "w_out", "norm_ffn_g", "w_up",
             "ffn_conv_w", "ffn_conv_b", "w_down", "final_norm_g"]
    out = [total_loss, grad_x[None]]
    for group in (grads, delta, new_m, new_v):
        out += [group[n] for n in names]
    return tuple(out)
```

```python
import jax
import jax.numpy as jnp
from jax import lax
from jax.experimental import pallas as pl
from jax.experimental.pallas import tpu as pltpu

F32 = jnp.float32
BF16 = jnp.bfloat16

D_MODEL = 1024
CONV_WIDTH = 512
ATTN_WIDTH = 768
GROUP_WIDTH = 256
HEAD_DIM = 64
HEADS_PER_GROUP = 4
DILATIONS = (1, 4, 16)
ATTN_BLOCK = 128
D_FF = 2816
D_IN = 5888
EPS = 1e-6
NEG_INF = -1e30
ATTN_SCALE = HEAD_DIM ** -0.5

COL_ABCV = 0
COL_Q = 1536
COL_K = 2304
COL_V = 3072
COL_GATES = 3840

ADAM_LR = 0.001
ADAM_B1 = 0.9
ADAM_B2 = 0.999
ADAM_EPS = 1e-08
ADAM_WD = 0.01
ADAM_STEP = 10

LANES = 128
SUBLANES = 8
ROW_TILE = 512
VMEM_LIMIT = 56 * 1024 * 1024

_NT = (((1,), (1,)), ((), ()))
_TN = (((0,), (0,)), ((), ()))


def _params(n_axes, vmem=VMEM_LIMIT):
    return pltpu.CompilerParams(dimension_semantics=("arbitrary",) * n_axes, vmem_limit_bytes=vmem)


def _resident(shape):
    nd = len(shape)
    return pl.BlockSpec(shape, lambda *_: (0,) * nd, pipeline_mode=pl.Buffered(1))


def _rows(tm, width, col_block=0):
    return pl.BlockSpec((tm, width), lambda i: (i, col_block))


def _col_chunks(n, cmax):
    out, lo = [], 0
    while lo < n:
        size = min(cmax, n - lo)
        out.append((lo, size))
        lo += size
    return out


def _dot(a, b):
    return jnp.dot(a, b, preferred_element_type=F32)


def _dot_nt(a, b):
    return lax.dot_general(a, b, _NT, preferred_element_type=F32)


def _dot_tn(a, b):
    return lax.dot_general(a, b, _TN, preferred_element_type=F32)


def _sigmoid(x):
    return 1.0 / (1.0 + jnp.exp(-x))


def _shift_down(v, k, halo8):
    tm = v.shape[0]
    rolled = pltpu.roll(v, k, 0)
    fix = jnp.tile(pltpu.roll(halo8, k, 0), (tm // SUBLANES, 1))
    row = lax.broadcasted_iota(jnp.int32, v.shape, 0)
    return jnp.where(row < k, fix, rolled)


def _shift_up(v, k, halo8):
    tm = v.shape[0]
    rolled = pltpu.roll(v, tm - k, 0)
    fix = jnp.tile(pltpu.roll(halo8, SUBLANES - k, 0), (tm // SUBLANES, 1))
    row = lax.broadcasted_iota(jnp.int32, v.shape, 0)
    return jnp.where(row >= tm - k, fix, rolled)


def _colsum(v):
    return jnp.sum(v, axis=0, keepdims=True)


def _to_streams(val, scr, out_ref, d, col0):
    tm = val.shape[0]
    panels = val.shape[1] // LANES
    for p in range(panels):
        scr[pl.ds(p * tm, tm), :] = val[:, p * LANES:(p + 1) * LANES]
    for r in range(d):
        for p in range(panels):
            piece = scr[pl.ds(p * tm + r, tm // d, stride=d), :]
            out_ref[r, :, col0 + p * LANES: col0 + (p + 1) * LANES] = piece.astype(out_ref.dtype)


def _from_streams(in_ref, scr, d, col0, width):
    panels = width // LANES
    rows = in_ref.shape[1]
    tm = rows * d
    for r in range(d):
        for p in range(panels):
            scr[pl.ds(p * tm + r, rows, stride=d), :] = in_ref[r, :, col0 + p * LANES: col0 + (p + 1) * LANES].astype(F32)
    return jnp.concatenate([scr[pl.ds(p * tm, tm), :] for p in range(panels)], axis=1)


def _stream_block(tm, d, width):
    return pl.BlockSpec((d, tm // d, width), lambda i: (0, i, 0))


def _rev_stream_block(tm, d, width, nt):
    return pl.BlockSpec((d, tm // d, width), lambda i: (0, nt - 1 - i, 0))


N_CHIPS = 4
MESH_ID = pl.DeviceIdType.MESH
_ANY = pl.BlockSpec(memory_space=pl.ANY)
_VMEM = pl.BlockSpec(memory_space=pltpu.VMEM)


def _mesh_position():
    x, y, c = lax.axis_index("x"), lax.axis_index("y"), lax.axis_index("c")
    other_chips = [(1 - x, y), (x, 1 - y), (1 - x, 1 - y)]
    return x, y, c, other_chips


def _half_rows(c, half):
    return pl.ds(pl.multiple_of(c * half, 16), half)


def _remote_copy(k, src, dst, to, send_sems, recv_sems):
    return pltpu.make_async_remote_copy(src_ref=src, dst_ref=dst, send_sem=send_sems.at[k], recv_sem=recv_sems.at[k],
                                        device_id=to, device_id_type=MESH_ID)


def _gather_first_copies(big_refs, small_refs, big_outs, small_outs, send_sems, recv_sems):
    x, y, c, chips = _mesh_position()
    me = 2 * x + y
    nb = len(big_refs)
    cps = []
    for j, (px, py) in enumerate(chips):
        for b in range(nb):
            mine = _half_rows(c, big_refs[b].shape[1] // 2)
            cps.append(_remote_copy(3 * b + j, big_refs[b].at[0, mine], big_outs[b].at[me, mine], (px, py, c),
                                    send_sems, recv_sems))
        for s in range(len(small_refs)):
            cps.append(_remote_copy(3 * (nb + s) + j, small_refs[s].at[0], small_outs[s].at[me], (px, py, c),
                                    send_sems, recv_sems))
    return cps


def _gather_forward_copies(bufs, send_sems, recv_sems):
    x, y, c, chips = _mesh_position()
    cps = []
    for j, (px, py) in enumerate(chips):
        for b in range(len(bufs)):
            landed = bufs[b].at[2 * px + py, _half_rows(c, bufs[b].shape[1] // 2)]
            cps.append(_remote_copy(3 * b + j, landed, landed, (x, y, 1 - c), send_sems, recv_sems))
    return cps


def _chip_exchange_copies(src_refs, out_refs, send_sems, recv_sems):
    x, y, c, chips = _mesh_position()
    cps = []
    for j, (px, py) in enumerate(chips):
        for a in range(len(src_refs)):
            cps.append(_remote_copy(3 * a + j, src_refs[a].at[2 * px + py], out_refs[a].at[j], (px, py, c),
                                    send_sems, recv_sems))
    return cps


def _dma_sems(n):
    return [pltpu.SemaphoreType.DMA((n,)), pltpu.SemaphoreType.DMA((n,))]


def _inproj_fwd(x, g, w_in_t, big_shards, small_shards):
    s = x.shape[0]
    tm = ROW_TILE
    nt = s // tm
    nb, ns = len(big_shards), len(small_shards)
    n_fixed_in, n_fixed_out = 3, 8

    def body(*refs):
        x_ref, g_ref, w_ref = refs[:n_fixed_in]
        shard_refs = refs[n_fixed_in:n_fixed_in + nb + ns]
        pos = n_fixed_in + nb + ns
        h_ref, abcv_ref, gates_ref, qkv0_ref, qkv1_ref, qkv2_ref, hs1_ref, hs2_ref = refs[pos:pos + n_fixed_out]
        gathered_refs = refs[pos + n_fixed_out:pos + n_fixed_out + nb + ns]
        scr, send_sems, recv_sems = refs[pos + n_fixed_out + nb + ns:]
        i = pl.program_id(0)

        def gather_copies():
            return _gather_first_copies(shard_refs[:nb], shard_refs[nb:], gathered_refs[:nb], gathered_refs[nb:],
                                        send_sems, recv_sems)

        @pl.when(i == 0)
        def _():
            for cp in gather_copies():
                cp.start()

        xv = x_ref[...]
        r = lax.rsqrt(jnp.mean(xv * xv, axis=-1, keepdims=True) + EPS)
        hf = xv * r * g_ref[...]
        h = hf.astype(BF16)
        h_ref[...] = h
        for d, hs_ref in zip(DILATIONS[1:], (hs1_ref, hs2_ref)):
            for lo, size in _col_chunks(D_MODEL, GROUP_WIDTH):
                _to_streams(hf[:, lo:lo + size], scr, hs_ref, d, lo)
        for lo, size in _col_chunks(3 * CONV_WIDTH, 512):
            abcv_ref[:, lo:lo + size] = _dot_nt(h, w_ref[COL_ABCV + lo: COL_ABCV + lo + size, :]).astype(BF16)
        for lo, size in _col_chunks(2 * D_MODEL, 512):
            gates_ref[:, lo:lo + size] = _dot_nt(h, w_ref[COL_GATES + lo: COL_GATES + lo + size, :]).astype(BF16)
        for gi, (d, out_ref) in enumerate(zip(DILATIONS, (qkv0_ref, qkv1_ref, qkv2_ref))):
            for j, base in enumerate((COL_Q, COL_K, COL_V)):
                lo = base + gi * GROUP_WIDTH
                y = _dot_nt(h, w_ref[lo:lo + GROUP_WIDTH, :])
                _to_streams(y, scr, out_ref, d, j * GROUP_WIDTH)

        @pl.when(i == nt - 1)
        def _():
            for cp in gather_copies():
                cp.wait()

    outs = [jax.ShapeDtypeStruct((s, D_MODEL), BF16),
            jax.ShapeDtypeStruct((s, 3 * CONV_WIDTH), BF16),
            jax.ShapeDtypeStruct((s, 2 * D_MODEL), BF16)]
    outs += [jax.ShapeDtypeStruct((d, s // d, 3 * GROUP_WIDTH), BF16) for d in DILATIONS]
    outs += [jax.ShapeDtypeStruct((d, s // d, D_MODEL), BF16) for d in DILATIONS[1:]]
    outs += [jax.ShapeDtypeStruct((N_CHIPS,) + a.shape[1:], a.dtype) for a in list(big_shards) + list(small_shards)]
    return pl.pallas_call(
        body, name="inproj_fwd", grid=(nt,), out_shape=outs,
        in_specs=[_rows(tm, D_MODEL), _resident((1, D_MODEL)), _resident((D_IN, D_MODEL))] + [_ANY] * (nb + ns),
        out_specs=[_rows(tm, D_MODEL), _rows(tm, 3 * CONV_WIDTH), _rows(tm, 2 * D_MODEL)]
        + [_stream_block(tm, d, 3 * GROUP_WIDTH) for d in DILATIONS]
        + [_stream_block(tm, d, D_MODEL) for d in DILATIONS[1:]] + [_ANY] * (nb + ns),
        scratch_shapes=[pltpu.VMEM((GROUP_WIDTH // LANES * tm, LANES), F32)] + _dma_sems(3 * (nb + ns)),
        compiler_params=_params(1))(x, g, w_in_t, *big_shards, *small_shards)


def _head_of_lane(shape):
    return lax.broadcasted_iota(jnp.int32, shape, 1) // HEAD_DIM


def _stack_heads(v):
    head = _head_of_lane(v.shape)
    return jnp.concatenate([jnp.where(head == h, v, jnp.zeros_like(v)) for h in range(HEADS_PER_GROUP)], axis=0)


def _unstack_heads(v):
    q = ATTN_BLOCK
    head = _head_of_lane((q, v.shape[1]))
    out = jnp.zeros((q, v.shape[1]), v.dtype)
    for h in range(HEADS_PER_GROUP):
        out = jnp.where(head == h, v[h * q:(h + 1) * q], out)
    return out


def _per_head_rows(col):
    q = ATTN_BLOCK
    head = _head_of_lane((q, GROUP_WIDTH))
    out = jnp.zeros((q, GROUP_WIDTH), col.dtype)
    for h in range(HEADS_PER_GROUP):
        out = jnp.where(head == h, col[h * q:(h + 1) * q], out)
    return out


def _per_head_col(v):
    head = _head_of_lane(v.shape)
    cols = [jnp.max(jnp.where(head == h, v, -jnp.inf), axis=1, keepdims=True) for h in range(HEADS_PER_GROUP)]
    return jnp.concatenate(cols, axis=0)


ATTN_BLOCKS_PER_STEP = 4


def _band_bias(first_block):
    rows = HEADS_PER_GROUP * ATTN_BLOCK
    qi = lax.broadcasted_iota(jnp.int32, (rows, 2 * ATTN_BLOCK), 0) % ATTN_BLOCK
    kj = lax.broadcasted_iota(jnp.int32, (rows, 2 * ATTN_BLOCK), 1)
    dist = qi + ATTN_BLOCK - kj
    valid = (dist >= 0) & (dist <= ATTN_BLOCK)
    if first_block:
        valid = valid & (kj >= ATTN_BLOCK)
    return jnp.where(valid, 0.0, NEG_INF).astype(F32)


def _store_band_biases(bias_ref):
    bias_ref[0] = _band_bias(False)
    bias_ref[1] = _band_bias(True)


def _attn_block_specs(g, nb, clamp_last=False):
    q = ATTN_BLOCK
    last = nb // g - 1

    def cur(col, width=GROUP_WIDTH):
        if clamp_last:
            return pl.BlockSpec((None, g * q, width), lambda r, n: (r, jnp.minimum(n, last), col))
        return pl.BlockSpec((None, g * q, width), lambda r, n: (r, n, col))

    def prev(col):
        if clamp_last:
            return pl.BlockSpec((None, q, GROUP_WIDTH), lambda r, n: (r, jnp.clip(n * g - 1, 0, nb - 1), col))
        return pl.BlockSpec((None, q, GROUP_WIDTH), lambda r, n: (r, jnp.maximum(n * g - 1, 0), col))

    return cur, prev


def _attn_fwd(qkv, gi, forward=()):
    d, length, _ = qkv.shape
    nb = length // ATTN_BLOCK
    q = ATTN_BLOCK
    g = min(ATTN_BLOCKS_PER_STEP, nb)
    ns = nb // g
    nf = len(forward)

    def body(*refs):
        q_ref, kp_ref, kc_ref, vp_ref, vc_ref = refs[:5]
        o_ref, lse_ref = refs[5 + nf:7 + nf]
        buf_refs = refs[7 + nf:7 + 2 * nf]
        bias_ref = refs[7 + 2 * nf]
        sems = refs[8 + 2 * nf:]
        n = pl.program_id(1)
        first_step = (pl.program_id(0) == 0) & (n == 0)
        last_step = (pl.program_id(0) == d - 1) & (n == ns - 1)

        @pl.when(first_step)
        def _():
            _store_band_biases(bias_ref)
            for cp in _gather_forward_copies(buf_refs, *sems) if nf else ():
                cp.start()

        kfull = jnp.concatenate([kp_ref[...], kc_ref[...]], axis=0)
        vfull = jnp.concatenate([vp_ref[...], vc_ref[...]], axis=0)
        for j in range(g):
            qs = _stack_heads(q_ref[j * q:(j + 1) * q, :])
            k2 = kfull[j * q:(j + 2) * q]
            v2 = vfull[j * q:(j + 2) * q]
            bias = jnp.where(n == 0, bias_ref[1], bias_ref[0]) if j == 0 else bias_ref[0]
            sc = _dot_nt(qs, k2) * ATTN_SCALE + bias
            m = jnp.max(sc, axis=1, keepdims=True)
            p = jnp.exp(sc - m)
            l = jnp.sum(p, axis=1, keepdims=True)
            of = _dot(p.astype(BF16), v2) / l
            o_ref[j * q:(j + 1) * q, :] = _unstack_heads(of).astype(BF16)
            lse_ref[j * q:(j + 1) * q, :] = _per_head_rows(m + jnp.log(l))

        if nf:
            @pl.when(last_step)
            def _():
                for cp in _gather_forward_copies(buf_refs, *sems):
                    cp.wait()

    cur, prev = _attn_block_specs(g, nb)
    return pl.pallas_call(
        body, name=f"attn_fwd_g{gi}", grid=(d, ns),
        out_shape=[jax.ShapeDtypeStruct((d, length, GROUP_WIDTH), BF16),
                   jax.ShapeDtypeStruct((d, length, GROUP_WIDTH), F32)]
        + [jax.ShapeDtypeStruct(a.shape, a.dtype) for a in forward],
        in_specs=[cur(0), prev(1), cur(1), prev(2), cur(2)] + [_ANY] * nf,
        out_specs=[cur(0), cur(0)] + [_ANY] * nf,
        input_output_aliases={5 + a: 2 + a for a in range(nf)},
        scratch_shapes=[pltpu.VMEM((2, HEADS_PER_GROUP * q, 2 * q), F32)] + (_dma_sems(3 * nf) if nf else []),
        compiler_params=_params(2))(qkv, qkv, qkv, qkv, qkv, *forward)


def _conv_branch(ab, ac, av, halo_u, w, b):
    u = ac * av
    sh1 = _shift_down(u, 1, halo_u)
    sh2 = _shift_down(u, 2, halo_u)
    cv = w[0:1] * sh2 + w[1:2] * sh1 + w[2:3] * u + b
    return ab * cv, cv, u, sh1, sh2


def _mix_fwd(x, abcv, gates, o_list, lse_list, conv_w, conv_b, b_gate, w_pa, w_pb, w_out):
    s = x.shape[0]
    tm = ROW_TILE

    def body(x_ref, abcv_ref, gates_ref, o0_ref, o1_ref, o2_ref, l0_ref, l1_ref, l2_ref,
             cw_ref, cb_ref, bg_ref, wpa_ref, wpb_ref, wout_ref,
             x1_ref, ya0_ref, yb0_ref, mrg_ref, ya_ref, yb_ref, lsetot_ref, halo_ref, scr):
        i = pl.program_id(0)

        @pl.when(i == 0)
        def _():
            halo_ref[...] = jnp.zeros_like(halo_ref)

        ab = abcv_ref[:, 0:CONV_WIDTH].astype(F32)
        ac = abcv_ref[:, CONV_WIDTH:2 * CONV_WIDTH].astype(F32)
        av = abcv_ref[:, 2 * CONV_WIDTH:3 * CONV_WIDTH].astype(F32)
        ya0, _, u, _, _ = _conv_branch(ab, ac, av, halo_ref[...], cw_ref[...], cb_ref[...])
        halo_ref[...] = u[tm - SUBLANES:tm]
        ya0 = ya0.astype(BF16)
        ya0_ref[...] = ya0
        ya = _dot(ya0, wpa_ref[...])

        o_refs, l_refs = (o0_ref, o1_ref, o2_ref), (l0_ref, l1_ref, l2_ref)
        lses = [_from_streams(l_refs[g], scr, DILATIONS[g], 0, GROUP_WIDTH) for g in range(3)]
        top = jnp.maximum(jnp.maximum(lses[0], lses[1]), lses[2])
        lsetot = top + jnp.log(jnp.exp(lses[0] - top) + jnp.exp(lses[1] - top) + jnp.exp(lses[2] - top))
        lsetot_ref[...] = lsetot
        yb = jnp.zeros((tm, D_MODEL), F32)
        for g in range(3):
            og = _from_streams(o_refs[g], scr, DILATIONS[g], 0, GROUP_WIDTH)
            yb0 = (jnp.exp(lses[g] - lsetot) * og).astype(BF16)
            yb0_ref[:, g * GROUP_WIDTH:(g + 1) * GROUP_WIDTH] = yb0
            yb = yb + _dot(yb0, wpb_ref[g * GROUP_WIDTH:(g + 1) * GROUP_WIDTH, :])

        sa = _sigmoid(gates_ref[:, 0:D_MODEL].astype(F32) + bg_ref[0:1, :])
        sb = _sigmoid(gates_ref[:, D_MODEL:2 * D_MODEL].astype(F32) + bg_ref[1:2, :])
        ya_ref[...] = ya.astype(BF16)
        yb_ref[...] = yb.astype(BF16)
        mrg = (sa * ya + sb * yb).astype(BF16)
        mrg_ref[...] = mrg
        x1_ref[...] = x_ref[...] + _dot(mrg, wout_ref[...])

    outs = [jax.ShapeDtypeStruct((s, D_MODEL), F32),
            jax.ShapeDtypeStruct((s, CONV_WIDTH), BF16),
            jax.ShapeDtypeStruct((s, ATTN_WIDTH), BF16),
            jax.ShapeDtypeStruct((s, D_MODEL), BF16),
            jax.ShapeDtypeStruct((s, D_MODEL), BF16),
            jax.ShapeDtypeStruct((s, D_MODEL), BF16),
            jax.ShapeDtypeStruct((s, GROUP_WIDTH), F32)]
    return pl.pallas_call(
        body, name="mix_fwd", grid=(s // tm,), out_shape=outs,
        in_specs=[_rows(tm, D_MODEL), _rows(tm, 3 * CONV_WIDTH), _rows(tm, 2 * D_MODEL)]
        + [_stream_block(tm, d, GROUP_WIDTH) for d in DILATIONS] * 2
        + [_resident((3, CONV_WIDTH)), _resident((1, CONV_WIDTH)), _resident((2, D_MODEL)),
           _resident((CONV_WIDTH, D_MODEL)), _resident((ATTN_WIDTH, D_MODEL)), _resident((D_MODEL, D_MODEL))],
        out_specs=[_rows(tm, D_MODEL), _rows(tm, CONV_WIDTH), _rows(tm, ATTN_WIDTH), _rows(tm, D_MODEL),
                   _rows(tm, D_MODEL), _rows(tm, D_MODEL), _rows(tm, GROUP_WIDTH)],
        scratch_shapes=[pltpu.VMEM((SUBLANES, CONV_WIDTH), F32),
                        pltpu.VMEM((GROUP_WIDTH // LANES * tm, LANES), F32)],
        compiler_params=_params(1))(x, abcv, gates, *o_list, *lse_list, conv_w, conv_b, b_gate, w_pa, w_pb, w_out)


FFN_CHUNK = 512


def _ffn_up_fwd(x1, g, w_up, conv_w, conv_b):
    s = x1.shape[0]
    n = w_up.shape[1]
    tm = ROW_TILE

    def body(x_ref, g_ref, w_ref, cw_ref, cb_ref, h_ref, up0_ref, up_ref, halo_ref):
        @pl.when(pl.program_id(0) == 0)
        def _():
            halo_ref[...] = jnp.zeros_like(halo_ref)

        xv = x_ref[...]
        r = lax.rsqrt(jnp.mean(xv * xv, axis=-1, keepdims=True) + EPS)
        h = (xv * r * g_ref[...]).astype(BF16)
        h_ref[...] = h
        for lo, size in _col_chunks(n, FFN_CHUNK):
            cols = slice(lo, lo + size)
            y = _dot(h, w_ref[:, cols])
            up0_ref[:, cols] = y.astype(BF16)
            halo = halo_ref[:, cols]
            w = cw_ref[:, cols]
            up = w[0:1] * _shift_down(y, 2, halo) + w[1:2] * _shift_down(y, 1, halo) + w[2:3] * y + cb_ref[:, cols]
            up_ref[:, cols] = up.astype(BF16)
            halo_ref[:, cols] = y[tm - SUBLANES:tm]

    return pl.pallas_call(
        body, name="ffn_up_fwd", grid=(s // tm,),
        out_shape=[jax.ShapeDtypeStruct((s, D_MODEL), BF16), jax.ShapeDtypeStruct((s, n), BF16),
                   jax.ShapeDtypeStruct((s, n), BF16)],
        in_specs=[_rows(tm, D_MODEL), _resident((1, D_MODEL)), _resident((D_MODEL, n)), _resident((3, n)),
                  _resident((1, n))],
        out_specs=[_rows(tm, D_MODEL), _rows(tm, n), _rows(tm, n)],
        scratch_shapes=[pltpu.VMEM((SUBLANES, n), F32)],
        compiler_params=_params(1))(x1, g, w_up, conv_w, conv_b)


def _ffn_act_fwd(x1, up, target, w_down, g_final):
    s = x1.shape[0]
    tm = ROW_TILE

    def body(x1_ref, up_ref, tgt_ref, wd_ref, gf_ref, act_ref, dx2_ref, dx2b_ref, dgf_ref, loss_ref):
        @pl.when(pl.program_id(0) == 0)
        def _():
            dgf_ref[...] = jnp.zeros_like(dgf_ref)
            loss_ref[...] = jnp.zeros_like(loss_ref)

        acc = jnp.zeros((tm, D_MODEL), F32)
        for lo, size in _col_chunks(D_FF, FFN_CHUNK):
            gate = up_ref[:, lo:lo + size].astype(F32)
            val = up_ref[:, D_FF + lo:D_FF + lo + size].astype(F32)
            act = (gate * _sigmoid(gate) * val).astype(BF16)
            act_ref[:, lo:lo + size] = act
            acc = acc + _dot(act, wd_ref[lo:lo + size, :])

        x2 = x1_ref[...] + acc
        r = lax.rsqrt(jnp.mean(x2 * x2, axis=-1, keepdims=True) + EPS)
        xn = x2 * r
        gf = gf_ref[...]
        err = xn * gf - tgt_ref[...]
        loss_ref[...] += (0.5 / D_MODEL) * jnp.sum(err * err)
        dy = err * (1.0 / D_MODEL)
        dgf_ref[...] += _colsum(dy * xn)
        dxn = dy * gf
        dx2 = r * (dxn - xn * jnp.mean(dxn * xn, axis=-1, keepdims=True))
        dx2_ref[...] = dx2
        dx2b_ref[...] = dx2.astype(BF16)

    return pl.pallas_call(
        body, name="ffn_act_fwd", grid=(s // tm,),
        out_shape=[jax.ShapeDtypeStruct((s, D_FF), BF16), jax.ShapeDtypeStruct((s, D_MODEL), F32),
                   jax.ShapeDtypeStruct((s, D_MODEL), BF16),
                   jax.ShapeDtypeStruct((1, D_MODEL), F32), jax.ShapeDtypeStruct((1, LANES), F32)],
        in_specs=[_rows(tm, D_MODEL), _rows(tm, 2 * D_FF), _rows(tm, D_MODEL),
                  _resident((D_FF, D_MODEL)), _resident((1, D_MODEL))],
        out_specs=[_rows(tm, D_FF), _rows(tm, D_MODEL), _rows(tm, D_MODEL),
                   pl.BlockSpec((1, D_MODEL), lambda i: (0, 0)), pl.BlockSpec((1, LANES), lambda i: (0, 0))],
        compiler_params=_params(1))(x1, up, target, w_down, g_final)


def _ffn_act_bwd(dx2b, up, w_down):
    s = dx2b.shape[0]
    tm = ROW_TILE

    def body(dx2_ref, up_ref, wd_ref, dup_ref):
        dx2 = dx2_ref[...]
        for lo, size in _col_chunks(D_FF, FFN_CHUNK):
            gate = up_ref[:, lo:lo + size].astype(F32)
            val = up_ref[:, D_FF + lo:D_FF + lo + size].astype(F32)
            dact = _dot_nt(dx2, wd_ref[lo:lo + size, :])
            sg = _sigmoid(gate)
            dup_ref[:, lo:lo + size] = (dact * val * (sg * (1.0 + gate * (1.0 - sg)))).astype(BF16)
            dup_ref[:, D_FF + lo:D_FF + lo + size] = (dact * (gate * sg)).astype(BF16)

    return pl.pallas_call(
        body, name="ffn_act_bwd", grid=(s // tm,),
        out_shape=jax.ShapeDtypeStruct((s, 2 * D_FF), BF16),
        in_specs=[_rows(tm, D_MODEL), _rows(tm, 2 * D_FF), _resident((D_FF, D_MODEL))],
        out_specs=_rows(tm, 2 * D_FF),
        compiler_params=_params(1))(dx2b, up, w_down)


FFN_BWD_ROW_TILE = 256


def _ffn_up_bwd(d_up, up0, w_up, conv_w, x1, g, dres):
    s = x1.shape[0]
    n = w_up.shape[1]
    tm = FFN_BWD_ROW_TILE
    nt = s // tm

    def body(dup_ref, up0_ref, w_ref, cw_ref, x_ref, g_ref, dres_ref,
             dx_ref, dxb_ref, dg_ref, dup0_ref, small_ref, next_ref):
        @pl.when(pl.program_id(0) == 0)
        def _():
            next_ref[...] = jnp.zeros_like(next_ref)
            small_ref[...] = jnp.zeros_like(small_ref)
            dg_ref[...] = jnp.zeros_like(dg_ref)

        dh = jnp.zeros((tm, D_MODEL), F32)
        for lo, size in _col_chunks(n, FFN_CHUNK):
            cols = slice(lo, lo + size)
            dz = dup_ref[:, cols].astype(F32)
            x0 = up0_ref[:, cols].astype(F32)
            nxt = next_ref[:, cols]
            dz1 = _shift_up(dz, 1, nxt)
            dz2 = _shift_up(dz, 2, nxt)
            next_ref[:, cols] = dz[0:SUBLANES]
            small_ref[0:1, cols] += _colsum(dz2 * x0)
            small_ref[1:2, cols] += _colsum(dz1 * x0)
            small_ref[2:3, cols] += _colsum(dz * x0)
            small_ref[3:4, cols] += _colsum(dz)
            w = cw_ref[:, cols]
            dup0 = (w[2:3] * dz + w[1:2] * dz1 + w[0:1] * dz2).astype(BF16)
            dup0_ref[:, cols] = dup0
            dh = dh + _dot_nt(dup0, w_ref[:, cols])
        xv = x_ref[...]
        r = lax.rsqrt(jnp.mean(xv * xv, axis=-1, keepdims=True) + EPS)
        xn = xv * r
        dg_ref[...] += _colsum(dh * xn)
        dxn = dh * g_ref[...]
        dx = dres_ref[...] + r * (dxn - xn * jnp.mean(dxn * xn, axis=-1, keepdims=True))
        dx_ref[...] = dx
        dxb_ref[...] = dx.astype(BF16)

    rows = lambda width: pl.BlockSpec((tm, width), lambda i: (nt - 1 - i, 0))
    return pl.pallas_call(
        body, name="ffn_up_bwd", grid=(nt,),
        out_shape=[jax.ShapeDtypeStruct((s, D_MODEL), F32), jax.ShapeDtypeStruct((s, D_MODEL), BF16),
                   jax.ShapeDtypeStruct((1, D_MODEL), F32), jax.ShapeDtypeStruct((s, n), BF16),
                   jax.ShapeDtypeStruct((SUBLANES, n), F32)],
        in_specs=[rows(n), rows(n), _resident((D_MODEL, n)), _resident((3, n)), rows(D_MODEL),
                  _resident((1, D_MODEL)), rows(D_MODEL)],
        out_specs=[rows(D_MODEL), rows(D_MODEL), pl.BlockSpec((1, D_MODEL), lambda i: (0, 0)), rows(n),
                   pl.BlockSpec((SUBLANES, n), lambda i: (0, 0))],
        scratch_shapes=[pltpu.VMEM((SUBLANES, n), F32)],
        compiler_params=_params(1))(d_up, up0, w_up, conv_w, x1, g, dres)


def _inproj_bwd(d_abcv, d_gates, d_qkvs, w_in_t, x, g, dres):
    s = x.shape[0]
    tm = ROW_TILE
    gw = GROUP_WIDTH

    def body(dabcv_ref, dgates_ref, dq0_ref, dq1_ref, dq2_ref, w_ref, x_ref, g_ref, dres_ref, dx_ref, dg_ref, scr):
        @pl.when(pl.program_id(0) == 0)
        def _():
            dg_ref[...] = jnp.zeros_like(dg_ref)

        dh = jnp.zeros((tm, D_MODEL), F32)
        for src, width, wrow in ((dabcv_ref, 3 * CONV_WIDTH, COL_ABCV), (dgates_ref, 2 * D_MODEL, COL_GATES)):
            for lo, size in _col_chunks(width, 512):
                dh = dh + _dot(src[:, lo:lo + size], w_ref[wrow + lo:wrow + lo + size, :])
        for gi, (d, dq_ref) in enumerate(zip(DILATIONS, (dq0_ref, dq1_ref, dq2_ref))):
            for j, base in enumerate((COL_Q, COL_K, COL_V)):
                dy = _from_streams(dq_ref, scr, d, j * gw, gw).astype(BF16)
                wrow = base + gi * gw
                dh = dh + _dot(dy, w_ref[wrow:wrow + gw, :])
        xv = x_ref[...]
        r = lax.rsqrt(jnp.mean(xv * xv, axis=-1, keepdims=True) + EPS)
        xn = xv * r
        dg_ref[...] += _colsum(dh * xn)
        dxn = dh * g_ref[...]
        dx_ref[...] = dres_ref[...] + r * (dxn - xn * jnp.mean(dxn * xn, axis=-1, keepdims=True))

    return pl.pallas_call(
        body, name="inproj_bwd", grid=(s // tm,),
        out_shape=[jax.ShapeDtypeStruct((s, D_MODEL), F32), jax.ShapeDtypeStruct((1, D_MODEL), F32)],
        in_specs=[_rows(tm, 3 * CONV_WIDTH), _rows(tm, 2 * D_MODEL)]
        + [_stream_block(tm, d, 3 * gw) for d in DILATIONS]
        + [_resident((D_IN, D_MODEL)), _rows(tm, D_MODEL), _resident((1, D_MODEL)), _rows(tm, D_MODEL)],
        out_specs=[_rows(tm, D_MODEL), pl.BlockSpec((1, D_MODEL), lambda i: (0, 0))],
        scratch_shapes=[pltpu.VMEM((gw // LANES * tm, LANES), F32)],
        compiler_params=_params(1))(d_abcv, d_gates, *d_qkvs, w_in_t, x, g, dres)


def _mix_bwd(dx1, abcv, gates, ya, yb, yb0, lsetot, conv_w, conv_b, b_gate, w_pa, w_pb, w_out, exchange=()):
    s = dx1.shape[0]
    tm = ROW_TILE
    nt = s // tm
    hb = tm // (2 * SUBLANES)
    nx = len(exchange)

    def body(*refs):
        (dx1_ref, abcv_ref, pre_ref, gates_ref, ya_ref, yb_ref, yb0_ref, lsetot_ref,
         cw_ref, cb_ref, bg_ref, wpa_ref, wpb_ref, wout_ref) = refs[:14]
        part_refs = refs[14:14 + nx]
        (dya_ref, dyb_ref, dgates_ref, dabcv_ref, dyb0_ref, dyl0_ref, dyl1_ref, dyl2_ref, aux0_ref, aux1_ref,
         aux2_ref, sm_gate_ref, sm_conv_ref) = refs[14 + nx:27 + nx]
        recv_refs = refs[27 + nx:27 + 2 * nx]
        next_ref, scr = refs[27 + 2 * nx:29 + 2 * nx]
        sems = refs[29 + 2 * nx:]
        i = pl.program_id(0)

        @pl.when(i == 0)
        def _():
            next_ref[...] = jnp.zeros_like(next_ref)
            sm_gate_ref[...] = jnp.zeros_like(sm_gate_ref)
            sm_conv_ref[...] = jnp.zeros_like(sm_conv_ref)
            for cp in _chip_exchange_copies(part_refs, recv_refs, *sems) if nx else ():
                cp.start()

        not_first = (i < nt - 1).astype(F32)
        dm = _dot_nt(dx1_ref[...].astype(BF16), wout_ref[...])
        sa = _sigmoid(gates_ref[:, 0:D_MODEL].astype(F32) + bg_ref[0:1, :])
        sb = _sigmoid(gates_ref[:, D_MODEL:2 * D_MODEL].astype(F32) + bg_ref[1:2, :])
        dya = (dm * sa).astype(BF16)
        dyb = (dm * sb).astype(BF16)
        dya_ref[...] = dya
        dyb_ref[...] = dyb
        dga = dm * ya_ref[...].astype(F32) * (sa * (1.0 - sa))
        dgb = dm * yb_ref[...].astype(F32) * (sb * (1.0 - sb))
        dgates_ref[:, 0:D_MODEL] = dga.astype(BF16)
        dgates_ref[:, D_MODEL:2 * D_MODEL] = dgb.astype(BF16)
        sm_gate_ref[0:1, :] += _colsum(dga)
        sm_gate_ref[1:2, :] += _colsum(dgb)

        dya0 = _dot_nt(dya, wpa_ref[...])
        ab = abcv_ref[:, 0:CONV_WIDTH].astype(F32)
        ac = abcv_ref[:, CONV_WIDTH:2 * CONV_WIDTH].astype(F32)
        av = abcv_ref[:, 2 * CONV_WIDTH:3 * CONV_WIDTH].astype(F32)
        pre = pre_ref[...].astype(F32) * not_first
        halo_u = (pre[:, CONV_WIDTH:2 * CONV_WIDTH] * pre[:, 2 * CONV_WIDTH:3 * CONV_WIDTH])[SUBLANES:2 * SUBLANES]
        w = cw_ref[...]
        _, cv, u, sh1, sh2 = _conv_branch(ab, ac, av, halo_u, w, cb_ref[...])
        dcv = dya0 * ab
        sm_conv_ref[0:1, :] += _colsum(dcv * sh2)
        sm_conv_ref[1:2, :] += _colsum(dcv * sh1)
        sm_conv_ref[2:3, :] += _colsum(dcv * u)
        sm_conv_ref[3:4, :] += _colsum(dcv)
        nxt = next_ref[...]
        du = w[2:3] * dcv + w[1:2] * _shift_up(dcv, 1, nxt) + w[0:1] * _shift_up(dcv, 2, nxt)
        next_ref[...] = dcv[0:SUBLANES]
        dabcv_ref[:, 0:CONV_WIDTH] = (dya0 * cv).astype(BF16)
        dabcv_ref[:, CONV_WIDTH:2 * CONV_WIDTH] = (du * av).astype(BF16)
        dabcv_ref[:, 2 * CONV_WIDTH:3 * CONV_WIDTH] = (du * ac).astype(BF16)

        head_r = lax.broadcasted_iota(jnp.int32, (GROUP_WIDTH, GROUP_WIDTH), 0) // HEAD_DIM
        head_c = lax.broadcasted_iota(jnp.int32, (GROUP_WIDTH, GROUP_WIDTH), 1) // HEAD_DIM
        same_head = (head_r == head_c).astype(BF16)
        prod = jnp.zeros((tm, GROUP_WIDTH), F32)
        dyb0s = []
        for g in range(3):
            cols = slice(g * GROUP_WIDTH, (g + 1) * GROUP_WIDTH)
            dyb0 = _dot_nt(dyb, wpb_ref[cols, :])
            dyb0_ref[:, cols] = dyb0.astype(BF16)
            dyb0s.append(dyb0)
            prod = prod + dyb0 * yb0_ref[:, cols].astype(F32)
        hi = prod.astype(BF16)
        mid = (prod - hi.astype(F32)).astype(BF16)
        lo = (prod - hi.astype(F32) - mid.astype(F32)).astype(BF16)
        delta = _dot(hi, same_head) + _dot(mid, same_head) + _dot(lo, same_head)
        lsetot = lsetot_ref[...]
        for g, (dy_ref, aux_ref) in enumerate(zip((dyl0_ref, dyl1_ref, dyl2_ref), (aux0_ref, aux1_ref, aux2_ref))):
            d = DILATIONS[g]
            _to_streams(dyb0s[g], scr, dy_ref, d, 0)
            _to_streams(lsetot, scr, aux_ref, d, 0)
            _to_streams(delta, scr, aux_ref, d, GROUP_WIDTH)

        if nx:
            @pl.when(i == nt - 1)
            def _():
                for cp in _chip_exchange_copies(part_refs, recv_refs, *sems):
                    cp.wait()

    rev = lambda i: (nt - 1 - i, 0)
    pre = lambda i: (jnp.maximum((nt - 1 - i) * hb - 1, 0), 0)
    rows = lambda width: pl.BlockSpec((tm, width), rev)
    outs = [jax.ShapeDtypeStruct((s, D_MODEL), BF16), jax.ShapeDtypeStruct((s, D_MODEL), BF16),
            jax.ShapeDtypeStruct((s, 2 * D_MODEL), BF16), jax.ShapeDtypeStruct((s, 3 * CONV_WIDTH), BF16),
            jax.ShapeDtypeStruct((s, ATTN_WIDTH), BF16)]
    outs += [jax.ShapeDtypeStruct((d, s // d, GROUP_WIDTH), BF16) for d in DILATIONS]
    outs += [jax.ShapeDtypeStruct((d, s // d, 2 * GROUP_WIDTH), F32) for d in DILATIONS]
    outs += [jax.ShapeDtypeStruct((SUBLANES, D_MODEL), F32), jax.ShapeDtypeStruct((SUBLANES, CONV_WIDTH), F32)]
    outs += [jax.ShapeDtypeStruct((3,) + a.shape[1:], a.dtype) for a in exchange]
    return pl.pallas_call(
        body, name="mix_bwd", grid=(nt,), out_shape=outs,
        in_specs=[rows(D_MODEL), rows(3 * CONV_WIDTH), pl.BlockSpec((2 * SUBLANES, 3 * CONV_WIDTH), pre),
                  rows(2 * D_MODEL), rows(D_MODEL), rows(D_MODEL), rows(ATTN_WIDTH), rows(GROUP_WIDTH),
                  _resident((3, CONV_WIDTH)), _resident((1, CONV_WIDTH)), _resident((2, D_MODEL)),
                  _resident((CONV_WIDTH, D_MODEL)), _resident((ATTN_WIDTH, D_MODEL)), _resident((D_MODEL, D_MODEL))]
        + [_ANY] * nx,
        out_specs=[rows(D_MODEL), rows(D_MODEL), rows(2 * D_MODEL), rows(3 * CONV_WIDTH), rows(ATTN_WIDTH)]
        + [_rev_stream_block(tm, d, GROUP_WIDTH, nt) for d in DILATIONS]
        + [_rev_stream_block(tm, d, 2 * GROUP_WIDTH, nt) for d in DILATIONS]
        + [pl.BlockSpec((SUBLANES, D_MODEL), lambda i: (0, 0)), pl.BlockSpec((SUBLANES, CONV_WIDTH), lambda i: (0, 0))]
        + [_ANY] * nx,
        scratch_shapes=[pltpu.VMEM((SUBLANES, CONV_WIDTH), F32),
                        pltpu.VMEM((GROUP_WIDTH // LANES * tm, LANES), F32)] + (_dma_sems(3 * nx) if nx else []),
        compiler_params=_params(1))(dx1, abcv, abcv, gates, ya, yb, yb0, lsetot,
                                    conv_w, conv_b, b_gate, w_pa, w_pb, w_out, *exchange)


def _attn_bwd(qkv, dy, aux, gi, exchange=()):
    d, length, _ = qkv.shape
    nb = length // ATTN_BLOCK
    q = ATTN_BLOCK
    gw = GROUP_WIDTH
    g = min(ATTN_BLOCKS_PER_STEP, nb)
    assert g >= 2 and nb % g == 0
    ns = nb // g
    lag = 1 if ns > 1 else 0
    tail = (g - 1) * q
    nx = len(exchange)

    def body(*refs):
        q_ref, kp_ref, kc_ref, vp_ref, vc_ref, dy_ref, aux_ref = refs[:7]
        part_refs = refs[7:7 + nx]
        out_ref = refs[7 + nx]
        recv_refs = refs[8 + nx:8 + 2 * nx]
        dq_ref, dkv_ref, bias_ref = refs[8 + 2 * nx:11 + 2 * nx]
        sems = refs[11 + 2 * nx:]
        n = pl.program_id(1)

        @pl.when((pl.program_id(0) == 0) & (n == 0))
        def _():
            _store_band_biases(bias_ref)
            for cp in _chip_exchange_copies(part_refs, recv_refs, *sems) if nx else ():
                cp.start()

        if nx:
            @pl.when((pl.program_id(0) == d - 1) & (n == ns - 1 + lag))
            def _():
                for cp in _chip_exchange_copies(part_refs, recv_refs, *sems):
                    cp.wait()

        def emit(rows):
            out_ref[rows, gw:2 * gw] = dkv_ref[0, rows].astype(BF16)
            out_ref[rows, 2 * gw:3 * gw] = dkv_ref[1, rows].astype(BF16)

        if lag:
            @pl.when(n > 0)
            def _():
                out_ref[:, 0:gw] = dq_ref[...].astype(BF16)
                emit(slice(0, tail))

            @pl.when(n == ns)
            def _():
                emit(slice(tail, g * q))

        @pl.when(n < ns)
        def _():
            kfull = jnp.concatenate([kp_ref[...], kc_ref[...]], axis=0)
            vfull = jnp.concatenate([vp_ref[...], vc_ref[...]], axis=0)
            for j in range(g):
                rows = slice(j * q, (j + 1) * q)
                qs = _stack_heads(q_ref[rows, :])
                dys = _stack_heads(dy_ref[rows, :])
                k2 = kfull[j * q:(j + 2) * q]
                v2 = vfull[j * q:(j + 2) * q]
                lse = _per_head_col(aux_ref[rows, 0:gw])
                delta = _per_head_col(aux_ref[rows, gw:2 * gw])
                bias = jnp.where(n == 0, bias_ref[1], bias_ref[0]) if j == 0 else bias_ref[0]
                p = jnp.exp(_dot_nt(qs, k2) * ATTN_SCALE + bias - lse)
                dp = _dot_nt(dys, v2)
                ds = (p * (dp - delta) * ATTN_SCALE).astype(BF16)
                dq_j = _unstack_heads(_dot(ds, k2))
                dk2 = _dot_tn(ds, qs)
                dv2 = _dot_tn(p.astype(BF16), dys)
                if j == 0:
                    @pl.when(n > 0)
                    def _():
                        out_ref[tail:g * q, gw:2 * gw] = (dkv_ref[0, tail:g * q] + dk2[0:q]).astype(BF16)
                        out_ref[tail:g * q, 2 * gw:3 * gw] = (dkv_ref[1, tail:g * q] + dv2[0:q]).astype(BF16)
                else:
                    dkv_ref[0, (j - 1) * q:j * q] += dk2[0:q]
                    dkv_ref[1, (j - 1) * q:j * q] += dv2[0:q]
                dkv_ref[0, rows] = dk2[q:2 * q]
                dkv_ref[1, rows] = dv2[q:2 * q]
                dq_ref[rows, :] = dq_j
            if not lag:
                out_ref[:, 0:gw] = dq_ref[...].astype(BF16)
                emit(slice(0, g * q))

    cur, prev = _attn_block_specs(g, nb, clamp_last=True)
    return pl.pallas_call(
        body, name=f"attn_bwd_g{gi}", grid=(d, ns + lag),
        out_shape=[jax.ShapeDtypeStruct((d, length, 3 * gw), BF16)]
        + [jax.ShapeDtypeStruct((3,) + a.shape[1:], a.dtype) for a in exchange],
        in_specs=[cur(0), prev(1), cur(1), prev(2), cur(2), cur(0), cur(0, 2 * gw)] + [_ANY] * nx,
        out_specs=[pl.BlockSpec((None, g * q, 3 * gw), lambda r, n: (r, jnp.maximum(n - lag, 0), 0))] + [_ANY] * nx,
        scratch_shapes=[pltpu.VMEM((g * q, gw), F32), pltpu.VMEM((2, g * q, gw), F32),
                        pltpu.VMEM((2, HEADS_PER_GROUP * q, 2 * q), F32)] + (_dma_sems(3 * nx) if nx else []),
        compiler_params=_params(2))(qkv, qkv, qkv, qkv, qkv, dy, aux, *exchange)


def _matmul_tn(name, a, b, col_tile=1024, row_tile=2048, slabs=0):
    s, k = a.shape
    n = b.shape[1]
    tk = min(row_tile, s)
    tn = col_tile
    steps = s // tk

    def body(a_ref, b_ref, o_ref, acc_ref):
        t = pl.program_id(1)

        @pl.when(t == 0)
        def _():
            acc_ref[...] = jnp.zeros_like(acc_ref)

        acc_ref[...] += _dot_tn(a_ref[...], b_ref[...])

        @pl.when(t == steps - 1)
        def _():
            if slabs:
                for q in range(per_tile):
                    o_ref[q] = acc_ref[:, q * width:(q + 1) * width].astype(BF16)
            else:
                o_ref[...] = acc_ref[...].astype(BF16)

    if slabs:
        width = n // slabs
        per_tile = tn // width
        out_shape = jax.ShapeDtypeStruct((slabs, k, width), BF16)
        out_spec = pl.BlockSpec((per_tile, k, width), lambda j, t: (j, 0, 0))
    else:
        out_shape = jax.ShapeDtypeStruct((k, n), BF16)
        out_spec = pl.BlockSpec((k, tn), lambda j, t: (0, j))
    return pl.pallas_call(
        body, name=name, grid=(n // tn, steps), out_shape=out_shape,
        in_specs=[pl.BlockSpec((tk, k), lambda j, t: (t, 0)), pl.BlockSpec((tk, tn), lambda j, t: (t, j))],
        out_specs=out_spec, scratch_shapes=[pltpu.VMEM((k, tn), F32)],
        compiler_params=_params(2))(a, b)


def _weight_allgather(bigs, smalls):
    nb, ns = len(bigs), len(smalls)
    n_sems = 3 * (2 * nb + ns)

    def body(*refs):
        big_refs, small_refs = refs[:nb], refs[nb:nb + ns]
        big_outs, small_outs = refs[nb + ns:2 * nb + ns], refs[2 * nb + ns:2 * (nb + ns)]
        send_sems, recv_sems = refs[2 * (nb + ns):]
        x, y, c, chips = _mesh_position()
        me = 2 * x + y
        sibling = (x, y, 1 - c)

        def copy(k, src, dst, to):
            return pltpu.make_async_remote_copy(src_ref=src, dst_ref=dst, send_sem=send_sems.at[k],
                                                recv_sem=recv_sems.at[k], device_id=to, device_id_type=MESH_ID)

        halves = [r.shape[1] // 2 for r in big_refs]
        first = []
        for j, (px, py) in enumerate(chips):
            for b in range(nb):
                mine = _half_rows(c, halves[b])
                first.append(copy(3 * b + j, big_refs[b].at[0, mine], big_outs[b].at[me, mine], (px, py, c)))
            for s in range(ns):
                first.append(copy(3 * (2 * nb + s) + j, small_refs[s].at[0], small_outs[s].at[me], (px, py, c)))
        for cp in first:
            cp.start()
        passed = []
        for j, (px, py) in enumerate(chips):
            for b in range(nb):
                landed = big_outs[b].at[2 * px + py, _half_rows(c, halves[b])]
                copy(3 * b + j, landed, landed, (px, py, c)).wait_recv()
                fwd = copy(3 * (nb + b) + j, landed, landed, sibling)
                fwd.start()
                passed.append(fwd)
        for j, (px, py) in enumerate(chips):
            for s in range(ns):
                landed = small_outs[s].at[2 * px + py]
                copy(3 * (2 * nb + s) + j, landed, landed, (px, py, c)).wait_recv()
            for b in range(nb):
                from_sibling = big_outs[b].at[2 * px + py, _half_rows(1 - c, halves[b])]
                copy(3 * (nb + b) + j, from_sibling, from_sibling, sibling).wait_recv()
        for cp in first + passed:
            cp.wait_send()

    return pl.pallas_call(
        body, name="weight_allgather",
        out_shape=[jax.ShapeDtypeStruct((N_CHIPS,) + a.shape[1:], a.dtype) for a in list(bigs) + list(smalls)],
        in_specs=[_ANY] * (nb + ns), out_specs=[_ANY] * (nb + ns),
        scratch_shapes=[pltpu.SemaphoreType.DMA((n_sems,)), pltpu.SemaphoreType.DMA((n_sems,))],
    )(*bigs, *smalls)


def _sibling_swap_halves(name, slabs):
    na = len(slabs)

    def body(*refs):
        src_refs, out_refs = refs[:na], refs[na:2 * na]
        send_sems, recv_sems = refs[2 * na:]
        x, y, c, _ = _mesh_position()
        cps = []
        for a in range(na):
            theirs = _half_rows(1 - c, src_refs[a].shape[1] // 2)
            cps.append(pltpu.make_async_remote_copy(
                src_ref=src_refs[a].at[:, theirs, :], dst_ref=out_refs[a], send_sem=send_sems.at[a],
                recv_sem=recv_sems.at[a], device_id=(x, y, 1 - c), device_id_type=MESH_ID))
        for cp in cps:
            cp.start()
        for cp in cps:
            cp.wait()

    return pl.pallas_call(
        body, name=name,
        out_shape=[jax.ShapeDtypeStruct((a.shape[0], a.shape[1] // 2, a.shape[2]), a.dtype) for a in slabs],
        in_specs=[_ANY] * na, out_specs=[_ANY] * na,
        scratch_shapes=[pltpu.SemaphoreType.DMA((na,)), pltpu.SemaphoreType.DMA((na,))])(*slabs)


_HBM = pl.BlockSpec(memory_space=pltpu.HBM)
_SEM = pl.BlockSpec(memory_space=pltpu.SEMAPHORE)
_DATAFLOW = pltpu.SideEffectType.DATAFLOW_SIDE_EFFECTING


def _chip_exchange_start(partial):
    _, rows, cols = partial.shape
    landing = jax.ShapeDtypeStruct((3, rows, cols), partial.dtype)

    def body(src_ref, land_ref, send_sems, recv_sems, src_thru, land_thru, token):
        for cp in _chip_exchange_copies([src_ref], [land_ref], send_sems, recv_sems):
            cp.start()
        token[...] = jnp.zeros_like(token)

    return pl.pallas_call(
        body, name="grad_exchange_start",
        out_shape=(pltpu.SemaphoreType.DMA((3,)), pltpu.SemaphoreType.DMA((3,)),
                   pltpu.HBM(partial.shape, partial.dtype), pltpu.HBM(landing.shape, landing.dtype),
                   jax.ShapeDtypeStruct((SUBLANES, LANES), F32)),
        in_specs=(_HBM, _HBM), out_specs=(_SEM, _SEM, _HBM, _HBM, _VMEM), input_output_aliases={0: 2, 1: 3},
        compiler_params=pltpu.CompilerParams(has_side_effects=_DATAFLOW),
    )(pltpu.with_memory_space_constraint(partial, pltpu.HBM),
      pltpu.with_memory_space_constraint(lax.empty(landing.shape, landing.dtype), pltpu.HBM))


def _chip_exchange_wait(send_sems, recv_sems, src_thru, land_thru, after):
    def body(src_ref, land_ref, send_sems, recv_sems, after_ref, src_out, land_out):
        for cp in _chip_exchange_copies([src_ref], [land_ref], send_sems, recv_sems):
            cp.wait_send()
            cp.wait_recv()

    return pl.pallas_call(
        body, name="grad_exchange_wait",
        out_shape=(pltpu.HBM(src_thru.shape, src_thru.dtype), pltpu.HBM(land_thru.shape, land_thru.dtype)),
        in_specs=(_HBM, _HBM, _SEM, _SEM, _ANY), out_specs=(_HBM, _HBM), input_output_aliases={0: 0, 1: 1},
        compiler_params=pltpu.CompilerParams(has_side_effects=_DATAFLOW),
    )(src_thru, land_thru, send_sems, recv_sems, after)


def _sibling_share(halves):
    na = len(halves)

    def body(*refs):
        out_refs = refs[na:2 * na]
        send_sems, recv_sems = refs[2 * na:]
        x, y, c, _ = _mesh_position()
        cps = []
        for a in range(na):
            mine = out_refs[a].at[0, _half_rows(c, out_refs[a].shape[1] // 2)]
            cps.append(pltpu.make_async_remote_copy(src_ref=mine, dst_ref=mine, send_sem=send_sems.at[a],
                                                    recv_sem=recv_sems.at[a], device_id=(x, y, 1 - c),
                                                    device_id_type=MESH_ID))
        for cp in cps:
            cp.start()
        for a, cp in enumerate(cps):
            cp.wait_send()
            theirs = out_refs[a].at[0, _half_rows(1 - c, out_refs[a].shape[1] // 2)]
            pltpu.make_async_remote_copy(src_ref=theirs, dst_ref=theirs, send_sem=send_sems.at[a],
                                         recv_sem=recv_sems.at[a], device_id=(x, y, 1 - c),
                                         device_id_type=MESH_ID).wait_recv()

    return pl.pallas_call(
        body, name="grad_sibling_share", out_shape=[jax.ShapeDtypeStruct(a.shape, a.dtype) for a in halves],
        in_specs=[_ANY] * na, out_specs=[_ANY] * na, input_output_aliases={a: a for a in range(na)},
        scratch_shapes=[pltpu.SemaphoreType.DMA((na,)), pltpu.SemaphoreType.DMA((na,))])(*halves)


def _add_sibling(name, slab, received, core):
    n, rows, cols = slab.shape
    half = rows // 2

    def body(core_ref, a_ref, b_ref, o_ref):
        o_ref[...] = (a_ref[...].astype(F32) + b_ref[...].astype(F32)).astype(BF16)

    grid_spec = pltpu.PrefetchScalarGridSpec(
        num_scalar_prefetch=1, grid=(n,),
        in_specs=[pl.BlockSpec((None, half, cols), lambda s, core_ref: (s, core_ref[0], 0)),
                  pl.BlockSpec((None, half, cols), lambda s, core_ref: (s, 0, 0))],
        out_specs=pl.BlockSpec((None, half, cols), lambda s, core_ref: (s, 0, 0)))
    return pl.pallas_call(body, name=name, grid_spec=grid_spec,
                          out_shape=jax.ShapeDtypeStruct((n, half, cols), BF16),
                          compiler_params=_params(1))(core, slab, received)


def _sum_chips(name, partial, received, chip_core):
    _, half, cols = partial.shape

    def body(cc_ref, own_ref, recv_ref, o_ref):
        acc = own_ref[...].astype(F32)
        for k in range(3):
            acc = acc + recv_ref[k].astype(F32)
        o_ref[...] = acc

    grid_spec = pltpu.PrefetchScalarGridSpec(
        num_scalar_prefetch=1, grid=(1,),
        in_specs=[pl.BlockSpec((None, half, cols), lambda i, cc_ref: (cc_ref[0], 0, 0)),
                  pl.BlockSpec((3, half, cols), lambda i, cc_ref: (0, 0, 0))],
        out_specs=pl.BlockSpec((None, half, cols), lambda i, cc_ref: (0, cc_ref[1], 0)))
    return pl.pallas_call(body, name=name, grid_spec=grid_spec,
                          out_shape=jax.ShapeDtypeStruct((1, 2 * half, cols), F32),
                          compiler_params=_params(1))(chip_core, partial, received)


def _adam_math(w, g, m, v):
    nm = ADAM_B1 * m + (1.0 - ADAM_B1) * g
    nv = ADAM_B2 * v + (1.0 - ADAM_B2) * jnp.square(g)
    m_hat = nm / (1.0 - ADAM_B1 ** ADAM_STEP)
    v_hat = nv / (1.0 - ADAM_B2 ** ADAM_STEP)
    delta = -ADAM_LR * (m_hat / (jnp.sqrt(v_hat) + ADAM_EPS) + ADAM_WD * w)
    return delta, nm, nv


def _adamw(name, w, g, m, v):
    _, rows, cols = w.shape
    tr = next(t for t in (736, 512, 384, 352, 256, 128, 64, 32, 16, 8) if rows % t == 0)

    def body(w_ref, g_ref, m_ref, v_ref, d_ref, nm_ref, nv_ref):
        d_ref[...], nm_ref[...], nv_ref[...] = _adam_math(w_ref[...], g_ref[...], m_ref[...], v_ref[...])

    spec = pl.BlockSpec((None, tr, cols), lambda i: (0, i, 0))
    return pl.pallas_call(
        body, name=name, grid=(rows // tr,), out_shape=[jax.ShapeDtypeStruct(w.shape, F32)] * 3,
        in_specs=[spec] * 4, out_specs=[spec] * 3, compiler_params=_params(1))(w, g, m, v)


SMALL_PARAMS = ("norm_mix_g", "b_gate", "conv_a_w", "conv_a_b", "norm_ffn_g", "ffn_conv_w", "ffn_conv_b", "final_norm_g")


def _small_update(partials, params, moments_m, moments_v):
    na = len(partials)
    npar = len(SMALL_PARAMS)

    def body(*refs):
        in_refs = refs[:na]
        w_refs = refs[na:na + npar]
        m_refs = refs[na + npar:na + 2 * npar]
        v_refs = refs[na + 2 * npar:na + 3 * npar]
        pos = na + 3 * npar
        loss_ref = refs[pos]
        out_refs = refs[pos + 1:pos + 1 + 4 * npar]
        pos += 1 + 4 * npar
        acc_refs = refs[pos:pos + na]
        recv_refs = refs[pos + na:pos + 4 * na]
        send_sems, recv_sems = refs[pos + 4 * na:]
        x, y, c, _ = _mesh_position()
        chip = 2 * x + y
        for a in range(na):
            acc_refs[a][...] = in_refs[a][...]
        for stage, peer in enumerate(((x, y, 1 - c), (x, 1 - y, c), (1 - x, y, c))):
            cps = []
            for a in range(na):
                k = stage * na + a
                cps.append(pltpu.make_async_remote_copy(src_ref=acc_refs[a], dst_ref=recv_refs[k], send_sem=send_sems.at[k],
                                                        recv_sem=recv_sems.at[k], device_id=peer, device_id_type=MESH_ID))
            for cp in cps:
                cp.start()
            for cp in cps:
                cp.wait()
            for a in range(na):
                acc_refs[a][...] = acc_refs[a][...] + recv_refs[stage * na + a][...]

        mix, ffn, fin, gate, conv, ffnc, loss = acc_refs
        loss_ref[...] = loss[...]

        def cols(width):
            return pl.ds(pl.multiple_of(chip * width, LANES), width)

        grads = {
            "norm_mix_g": mix[...], "norm_ffn_g": ffn[...], "final_norm_g": fin[...],
            "b_gate": gate[0:2, cols(D_MODEL // N_CHIPS)],
            "conv_a_w": conv[0:3, cols(CONV_WIDTH // N_CHIPS)], "conv_a_b": conv[3:4, :],
            "ffn_conv_w": ffnc[0:3, cols(2 * D_FF // N_CHIPS)], "ffn_conv_b": ffnc[3:4, :]}
        for i, name in enumerate(SMALL_PARAMS):
            g = grads[name]
            if len(w_refs[i].shape) == 3:
                results = (g,) + _adam_math(w_refs[i][0], g, m_refs[i][0], v_refs[i][0])
                for o_ref, val in zip(out_refs[4 * i:4 * i + 4], results):
                    o_ref[0] = val
            else:
                results = (g,) + _adam_math(w_refs[i][...], g, m_refs[i][...], v_refs[i][...])
                for o_ref, val in zip(out_refs[4 * i:4 * i + 4], results):
                    o_ref[...] = val

    outs = [jax.ShapeDtypeStruct(partials[-1].shape, F32)]
    for w in params:
        outs += [jax.ShapeDtypeStruct(w.shape, F32)] * 4
    scratch = [pltpu.VMEM(p.shape, F32) for p in partials]
    scratch += [pltpu.VMEM(p.shape, F32) for _ in range(3) for p in partials]
    scratch += [pltpu.SemaphoreType.DMA((3 * na,)), pltpu.SemaphoreType.DMA((3 * na,))]
    n_in = na + 3 * npar
    return pl.pallas_call(
        body, name="small_update", out_shape=outs, in_specs=[_VMEM] * n_in, out_specs=[_VMEM] * len(outs),
        scratch_shapes=scratch)(*partials, *params, *moments_m, *moments_v)


def _gathered_columns(g):
    return jnp.transpose(g, (1, 0, 2)).reshape(g.shape[1], N_CHIPS * g.shape[2])


def _column_slabs(full):
    k, n = full.shape
    return jnp.transpose(full.reshape(k, N_CHIPS, n // N_CHIPS), (1, 0, 2))


def kernel(x, norm_mix_g, w_in, b_gate, conv_a_w, conv_a_b, w_proj_a, w_proj_b, w_out, norm_ffn_g, w_up, ffn_conv_w, ffn_conv_b, w_down, final_norm_g, loss_target, m_norm_mix_g, m_w_in, m_b_gate, m_conv_a_w, m_conv_a_b, m_w_proj_a, m_w_proj_b, m_w_out, m_norm_ffn_g, m_w_up, m_ffn_conv_w, m_ffn_conv_b, m_w_down, m_final_norm_g, v_norm_mix_g, v_w_in, v_b_gate, v_conv_a_w, v_conv_a_b, v_w_proj_a, v_w_proj_b, v_w_out, v_norm_ffn_g, v_w_up, v_ffn_conv_w, v_ffn_conv_b, v_w_down, v_final_norm_g):
    chip = (2 * lax.axis_index("x") + lax.axis_index("y")).astype(jnp.int32)
    core = lax.axis_index("c").astype(jnp.int32)
    core_arr = core.reshape(1)
    chip_core = jnp.stack([chip, core])
    xs, target = x[0], loss_target[0]
    g_final = final_norm_g.reshape(1, D_MODEL)

    def own_slot(gathered, own):
        return lax.dynamic_update_slice(gathered, own, (chip, 0, 0))

    def reduce_to_shards(names, slabs, exchange_in):
        from_sibling = _sibling_swap_halves("grad_swap_" + names[0], slabs)
        partials = [_add_sibling("grad_add_" + n, s, r, core_arr) for n, s, r in zip(names, slabs, from_sibling)]
        received, rest = exchange_in(partials)
        halves = [_sum_chips("grad_sum_" + n, p, r, chip_core) for n, p, r in zip(names, partials, received)]
        return halves, rest

    w_in_t, m_w_in_t, v_w_in_t = (jnp.swapaxes(a, 1, 2) for a in (w_in, m_w_in, v_w_in))
    w_in_tb = w_in_t.astype(BF16)
    (g_in,) = _weight_allgather([w_in_tb], [])
    w_in_full_t = own_slot(g_in, w_in_tb).reshape(D_IN, D_MODEL)
    later_w = [w_proj_a, w_proj_b, w_out, w_up, w_down]
    later_b = [w.astype(BF16) for w in later_w]
    small_sharded = [b_gate, conv_a_w, ffn_conv_w]
    fwd = _inproj_fwd(xs, norm_mix_g, w_in_full_t, later_b, small_sharded)
    h1, abcv, gates, qkv0, qkv1, qkv2, h1_streams4, h1_streams16 = fwd[:8]
    gathered_big, gathered_small = fwd[8:13], fwd[13:16]
    qkvs = (qkv0, qkv1, qkv2)
    attn0 = _attn_fwd(qkv0, 0, forward=gathered_big)
    attn = [attn0[:2], _attn_fwd(qkv1, 1), _attn_fwd(qkv2, 2)]
    g_pa, g_pb, g_out, g_up, g_down = [own_slot(g, own) for g, own in zip(attn0[2:], later_b)]
    g_bgate, g_convw, g_ffnw = [own_slot(g, own) for g, own in zip(gathered_small, small_sharded)]
    w_pa_full, w_pb_full, w_up_full = _gathered_columns(g_pa), _gathered_columns(g_pb), _gathered_columns(g_up)
    w_out_full, w_down_full = g_out.reshape(D_MODEL, D_MODEL), g_down.reshape(D_FF, D_MODEL)
    b_gate_full, conv_w_full, ffn_w_full = (_gathered_columns(g) for g in (g_bgate, g_convw, g_ffnw))

    x1, ya0, yb0, mrg, ya, yb, lsetot = _mix_fwd(
        xs, abcv, gates, [a[0] for a in attn], [a[1] for a in attn], conv_w_full, conv_a_b, b_gate_full,
        w_pa_full, w_pb_full, w_out_full)
    h2, up0, up = _ffn_up_fwd(x1, norm_ffn_g, w_up_full, ffn_w_full, ffn_conv_b)
    act, dx2, dx2b, d_g_final, loss = _ffn_act_fwd(x1, up, target, w_down_full, g_final)

    d_up = _ffn_act_bwd(dx2b, up, w_down_full)
    d_w_down = _matmul_tn("dw_down", act, dx2b, col_tile=512)
    dx1, dx1b, d_g_ffn, d_up0, ffn_small = _ffn_up_bwd(d_up, up0, w_up_full, ffn_w_full, x1, norm_ffn_g, dx2)
    d_w_up = _matmul_tn("dw_up", h2, d_up0, col_tile=2 * D_FF // N_CHIPS, slabs=N_CHIPS)

    def behind_mix_bwd(partials):
        res = _mix_bwd(dx1, abcv, gates, ya, yb, yb0, lsetot, conv_w_full, conv_a_b, b_gate_full,
                       w_pa_full, w_pb_full, w_out_full, exchange=partials)
        return res[13:], res[:13]

    halves_ffn, mix_res = reduce_to_shards(
        ("w_up", "w_down"), [d_w_up, d_w_down.reshape(N_CHIPS, D_FF // N_CHIPS, D_MODEL)], behind_mix_bwd)
    (d_ya, d_yb, d_gates, d_abcv, d_yb0, dyl0, dyl1, dyl2, aux0, aux1, aux2, gate_small, conv_small) = mix_res
    d_w_out = _matmul_tn("dw_out", mrg, dx1b)
    d_w_pa = _matmul_tn("dw_proj_a", ya0, d_ya, slabs=N_CHIPS)
    d_w_pb = _matmul_tn("dw_proj_b", yb0, d_yb, slabs=N_CHIPS)

    def behind_attn_bwd(partials):
        res = _attn_bwd(qkv0, dyl0, aux0, 0, exchange=partials)
        return res[1:], res[0]

    halves_mix, d_qkv0 = reduce_to_shards(
        ("w_proj_a", "w_proj_b", "w_out"),
        [d_w_pa, d_w_pb, d_w_out.reshape(N_CHIPS, D_MODEL // N_CHIPS, D_MODEL)], behind_attn_bwd)
    (d_qkv1,), (d_qkv2,) = _attn_bwd(qkv1, dyl1, aux1, 1), _attn_bwd(qkv2, dyl2, aux2, 2)

    dq = [d_qkv0, d_qkv1, d_qkv2]
    seq = xs.shape[0]
    d_w_abcv = _matmul_tn("dw_in_abcv", d_abcv, h1)
    d_w_gates = _matmul_tn("dw_in_gates", d_gates, h1)
    d_w_groups = [_matmul_tn(f"dw_in_qkv{g}", t.reshape(seq, 3 * GROUP_WIDTH), h.reshape(seq, D_MODEL))
                  for g, (t, h) in enumerate(zip(dq, (h1, h1_streams4, h1_streams16)))]
    gw = GROUP_WIDTH
    d_w_in_t = jnp.concatenate(
        [d_w_abcv] + [d_w_groups[g][j * gw:(j + 1) * gw] for j in range(3) for g in range(3)] + [d_w_gates], axis=0)

    slab_in = d_w_in_t.reshape(N_CHIPS, D_IN // N_CHIPS, D_MODEL)
    (from_sibling_in,) = _sibling_swap_halves("grad_swap_w_in", [slab_in])
    partial_in = _add_sibling("grad_add_w_in", slab_in, from_sibling_in, core_arr)
    send_sems, recv_sems, partial_thru, landing_thru, token = _chip_exchange_start(partial_in)
    g_mix_after_start = norm_mix_g + token[0:1, 0:1]
    grad_x, d_g_mix = _inproj_bwd(d_abcv, d_gates, dq, w_in_full_t, xs, g_mix_after_start, dx1)
    partial_in, received_in = _chip_exchange_wait(send_sems, recv_sems, partial_thru, landing_thru, d_g_mix)
    halves_in = [_sum_chips("grad_sum_w_in", partial_in, received_in, chip_core)]

    big_names = ("w_in", "w_proj_a", "w_proj_b", "w_out", "w_up", "w_down")
    big_grads = _sibling_share(halves_in + halves_mix + halves_ffn)
    big_w = dict(w_in=w_in_t, w_proj_a=w_proj_a, w_proj_b=w_proj_b, w_out=w_out, w_up=w_up, w_down=w_down)
    big_m = dict(w_in=m_w_in_t, w_proj_a=m_w_proj_a, w_proj_b=m_w_proj_b, w_out=m_w_out, w_up=m_w_up, w_down=m_w_down)
    big_v = dict(w_in=v_w_in_t, w_proj_a=v_w_proj_a, w_proj_b=v_w_proj_b, w_out=v_w_out, w_up=v_w_up, w_down=v_w_down)

    fin_w, fin_m, fin_v = (a.reshape(1, D_MODEL) for a in (final_norm_g, m_final_norm_g, v_final_norm_g))
    small_w = [norm_mix_g, b_gate, conv_a_w, conv_a_b, norm_ffn_g, ffn_conv_w, ffn_conv_b, fin_w]
    small_m = [m_norm_mix_g, m_b_gate, m_conv_a_w, m_conv_a_b, m_norm_ffn_g, m_ffn_conv_w, m_ffn_conv_b, fin_m]
    small_v = [v_norm_mix_g, v_b_gate, v_conv_a_w, v_conv_a_b, v_norm_ffn_g, v_ffn_conv_w, v_ffn_conv_b, fin_v]
    small_out = _small_update([d_g_mix, d_g_ffn, d_g_final, gate_small, conv_small, ffn_small, loss],
                              small_w, small_m, small_v)
    total_loss = small_out[0][0, 0]

    grads, delta, new_m, new_v = {}, {}, {}, {}
    for i, n in enumerate(SMALL_PARAMS):
        vals = small_out[1 + 4 * i:5 + 4 * i]
        if n == "final_norm_g":
            vals = [a.reshape(D_MODEL) for a in vals]
        grads[n], delta[n], new_m[n], new_v[n] = vals
    for n, g in zip(big_names, big_grads):
        vals = (g,) + tuple(_adamw("adamw_" + n, big_w[n], g, big_m[n], big_v[n]))
        if n == "w_in":
            vals = [jnp.swapaxes(a, 1, 2) for a in vals]
        grads[n], delta[n], new_m[n], new_v[n] = vals

    names = ["norm_mix_g", "w_in", "b_gate", "conv_a_w", "conv_a_b", "w_proj_a", "w_proj_b", "w_out", "norm_ffn_g", "w_up",
             "ffn_conv_w", "ffn_conv_b", "w_down", "final_norm_g"]
    out = [total_loss, grad_x[None]]
    for group in (grads, delta, new_m, new_v):
        out += [group[n] for n in names]
    return tuple(out)
```

```python
import jax
import jax.numpy as jnp
from jax import lax
from jax.experimental import pallas as pl
from jax.experimental.pallas import tpu as pltpu

F32 = jnp.float32
BF16 = jnp.bfloat16

D_MODEL = 1024
CONV_WIDTH = 512
ATTN_WIDTH = 768
GROUP_WIDTH = 256
HEAD_DIM = 64
HEADS_PER_GROUP = 4
DILATIONS = (1, 4, 16)
ATTN_BLOCK = 128
D_FF = 2816
D_IN = 5888
EPS = 1e-6
NEG_INF = -1e30
ATTN_SCALE = HEAD_DIM ** -0.5

COL_ABCV = 0
COL_Q = 1536
COL_K = 2304
COL_V = 3072
COL_GATES = 3840

ADAM_LR = 0.001
ADAM_B1 = 0.9
ADAM_B2 = 0.999
ADAM_EPS = 1e-08
ADAM_WD = 0.01
ADAM_STEP = 10

LANES = 128
SUBLANES = 8
ROW_TILE = 512
VMEM_LIMIT = 56 * 1024 * 1024

_NT = (((1,), (1,)), ((), ()))
_TN = (((0,), (0,)), ((), ()))


def _params(n_axes, vmem=VMEM_LIMIT):
    return pltpu.CompilerParams(dimension_semantics=("arbitrary",) * n_axes, vmem_limit_bytes=vmem)


def _resident(shape):
    nd = len(shape)
    return pl.BlockSpec(shape, lambda *_: (0,) * nd, pipeline_mode=pl.Buffered(1))


def _rows(tm, width, col_block=0):
    return pl.BlockSpec((tm, width), lambda i: (i, col_block))


def _col_chunks(n, cmax):
    out, lo = [], 0
    while lo < n:
        size = min(cmax, n - lo)
        out.append((lo, size))
        lo += size
    return out


def _dot(a, b):
    return jnp.dot(a, b, preferred_element_type=F32)


def _dot_nt(a, b):
    return lax.dot_general(a, b, _NT, preferred_element_type=F32)


def _dot_tn(a, b):
    return lax.dot_general(a, b, _TN, preferred_element_type=F32)


def _sigmoid(x):
    return 1.0 / (1.0 + jnp.exp(-x))


def _shift_down(v, k, halo8):
    tm = v.shape[0]
    rolled = pltpu.roll(v, k, 0)
    fix = jnp.tile(pltpu.roll(halo8, k, 0), (tm // SUBLANES, 1))
    row = lax.broadcasted_iota(jnp.int32, v.shape, 0)
    return jnp.where(row < k, fix, rolled)


def _shift_up(v, k, halo8):
    tm = v.shape[0]
    rolled = pltpu.roll(v, tm - k, 0)
    fix = jnp.tile(pltpu.roll(halo8, SUBLANES - k, 0), (tm // SUBLANES, 1))
    row = lax.broadcasted_iota(jnp.int32, v.shape, 0)
    return jnp.where(row >= tm - k, fix, rolled)


def _colsum(v):
    return jnp.sum(v, axis=0, keepdims=True)


def _to_streams(val, scr, out_ref, d, col0):
    tm = val.shape[0]
    panels = val.shape[1] // LANES
    if d == 1:
        out_ref[0, :, col0:col0 + val.shape[1]] = val.astype(out_ref.dtype)
        return
    for p in range(panels):
        scr[pl.ds(p * tm, tm), :] = val[:, p * LANES:(p + 1) * LANES]
    for r in range(d):
        for p in range(panels):
            piece = scr[pl.ds(p * tm + r, tm // d, stride=d), :]
            out_ref[r, :, col0 + p * LANES: col0 + (p + 1) * LANES] = piece.astype(out_ref.dtype)


def _from_streams(in_ref, scr, d, col0, width):
    panels = width // LANES
    rows = in_ref.shape[1]
    tm = rows * d
    if d == 1:
        return in_ref[0, :, col0:col0 + width].astype(F32)
    for r in range(d):
        for p in range(panels):
            scr[pl.ds(p * tm + r, rows, stride=d), :] = in_ref[r, :, col0 + p * LANES: col0 + (p + 1) * LANES].astype(F32)
    return jnp.concatenate([scr[pl.ds(p * tm, tm), :] for p in range(panels)], axis=1)


def _stream_block(tm, d, width):
    return pl.BlockSpec((d, tm // d, width), lambda i: (0, i, 0))


def _rev_stream_block(tm, d, width, nt):
    return pl.BlockSpec((d, tm // d, width), lambda i: (0, nt - 1 - i, 0))


N_CHIPS = 4
MESH_ID = pl.DeviceIdType.MESH
_ANY = pl.BlockSpec(memory_space=pl.ANY)
_VMEM = pl.BlockSpec(memory_space=pltpu.VMEM)


def _mesh_position():
    x, y, c = lax.axis_index("x"), lax.axis_index("y"), lax.axis_index("c")
    other_chips = [(1 - x, y), (x, 1 - y), (1 - x, 1 - y)]
    return x, y, c, other_chips


def _half_rows(c, half):
    return pl.ds(pl.multiple_of(c * half, 16), half)


def _remote_copy(k, src, dst, to, send_sems, recv_sems):
    return pltpu.make_async_remote_copy(src_ref=src, dst_ref=dst, send_sem=send_sems.at[k], recv_sem=recv_sems.at[k],
                                        device_id=to, device_id_type=MESH_ID)


def _gather_first_copies(big_refs, small_refs, big_outs, small_outs, send_sems, recv_sems):
    x, y, c, chips = _mesh_position()
    me = 2 * x + y
    nb = len(big_refs)
    cps = []
    for j, (px, py) in enumerate(chips):
        for b in range(nb):
            mine = _half_rows(c, big_refs[b].shape[1] // 2)
            cps.append(_remote_copy(3 * b + j, big_refs[b].at[0, mine], big_outs[b].at[me, mine], (px, py, c),
                                    send_sems, recv_sems))
        for s in range(len(small_refs)):
            cps.append(_remote_copy(3 * (nb + s) + j, small_refs[s].at[0], small_outs[s].at[me], (px, py, c),
                                    send_sems, recv_sems))
    return cps


def _gather_forward_copies(bufs, send_sems, recv_sems):
    x, y, c, chips = _mesh_position()
    cps = []
    for j, (px, py) in enumerate(chips):
        for b in range(len(bufs)):
            landed = bufs[b].at[2 * px + py, _half_rows(c, bufs[b].shape[1] // 2)]
            cps.append(_remote_copy(3 * b + j, landed, landed, (x, y, 1 - c), send_sems, recv_sems))
    return cps


def _chip_exchange_copies(src_refs, out_refs, send_sems, recv_sems):
    x, y, c, chips = _mesh_position()
    cps = []
    for j, (px, py) in enumerate(chips):
        for a in range(len(src_refs)):
            cps.append(_remote_copy(3 * a + j, src_refs[a].at[2 * px + py], out_refs[a].at[j], (px, py, c),
                                    send_sems, recv_sems))
    return cps


def _dma_sems(n):
    return [pltpu.SemaphoreType.DMA((n,)), pltpu.SemaphoreType.DMA((n,))]


def _inproj_fwd(x, g, w_in_t, big_shards, small_shards):
    s = x.shape[0]
    tm = ROW_TILE
    nt = s // tm
    nb, ns = len(big_shards), len(small_shards)
    n_fixed_in, n_fixed_out = 3, 8

    def body(*refs):
        x_ref, g_ref, w_ref = refs[:n_fixed_in]
        shard_refs = refs[n_fixed_in:n_fixed_in + nb + ns]
        pos = n_fixed_in + nb + ns
        h_ref, abcv_ref, gates_ref, qkv0_ref, qkv1_ref, qkv2_ref, hs1_ref, hs2_ref = refs[pos:pos + n_fixed_out]
        gathered_refs = refs[pos + n_fixed_out:pos + n_fixed_out + nb + ns]
        scr, send_sems, recv_sems = refs[pos + n_fixed_out + nb + ns:]
        i = pl.program_id(0)

        def gather_copies():
            return _gather_first_copies(shard_refs[:nb], shard_refs[nb:], gathered_refs[:nb], gathered_refs[nb:],
                                        send_sems, recv_sems)

        @pl.when(i == 0)
        def _():
            for cp in gather_copies():
                cp.start()

        xv = x_ref[...]
        r = lax.rsqrt(jnp.mean(xv * xv, axis=-1, keepdims=True) + EPS)
        hf = xv * r * g_ref[...]
        h = hf.astype(BF16)
        h_ref[...] = h
        for d, hs_ref in zip(DILATIONS[1:], (hs1_ref, hs2_ref)):
            for lo, size in _col_chunks(D_MODEL, GROUP_WIDTH):
                _to_streams(hf[:, lo:lo + size], scr, hs_ref, d, lo)
        for lo, size in _col_chunks(3 * CONV_WIDTH, 512):
            abcv_ref[:, lo:lo + size] = _dot_nt(h, w_ref[COL_ABCV + lo: COL_ABCV + lo + size, :]).astype(BF16)
        for lo, size in _col_chunks(2 * D_MODEL, 512):
            gates_ref[:, lo:lo + size] = _dot_nt(h, w_ref[COL_GATES + lo: COL_GATES + lo + size, :]).astype(BF16)
        for gi, (d, out_ref) in enumerate(zip(DILATIONS, (qkv0_ref, qkv1_ref, qkv2_ref))):
            for j, base in enumerate((COL_Q, COL_K, COL_V)):
                lo = base + gi * GROUP_WIDTH
                y = _dot_nt(h, w_ref[lo:lo + GROUP_WIDTH, :])
                _to_streams(y, scr, out_ref, d, j * GROUP_WIDTH)

        @pl.when(i == nt - 1)
        def _():
            for cp in gather_copies():
                cp.wait()

    outs = [jax.ShapeDtypeStruct((s, D_MODEL), BF16),
            jax.ShapeDtypeStruct((s, 3 * CONV_WIDTH), BF16),
            jax.ShapeDtypeStruct((s, 2 * D_MODEL), BF16)]
    outs += [jax.ShapeDtypeStruct((d, s // d, 3 * GROUP_WIDTH), BF16) for d in DILATIONS]
    outs += [jax.ShapeDtypeStruct((d, s // d, D_MODEL), BF16) for d in DILATIONS[1:]]
    outs += [jax.ShapeDtypeStruct((N_CHIPS,) + a.shape[1:], a.dtype) for a in list(big_shards) + list(small_shards)]
    return pl.pallas_call(
        body, name="inproj_fwd", grid=(nt,), out_shape=outs,
        in_specs=[_rows(tm, D_MODEL), _resident((1, D_MODEL)), _resident((D_IN, D_MODEL))] + [_ANY] * (nb + ns),
        out_specs=[_rows(tm, D_MODEL), _rows(tm, 3 * CONV_WIDTH), _rows(tm, 2 * D_MODEL)]
        + [_stream_block(tm, d, 3 * GROUP_WIDTH) for d in DILATIONS]
        + [_stream_block(tm, d, D_MODEL) for d in DILATIONS[1:]] + [_ANY] * (nb + ns),
        scratch_shapes=[pltpu.VMEM((GROUP_WIDTH // LANES * tm, LANES), F32)] + _dma_sems(3 * (nb + ns)),
        compiler_params=_params(1))(x, g, w_in_t, *big_shards, *small_shards)


def _head_of_lane(shape):
    return lax.broadcasted_iota(jnp.int32, shape, 1) // HEAD_DIM


def _stack_heads(v):
    head = _head_of_lane(v.shape)
    return jnp.concatenate([jnp.where(head == h, v, jnp.zeros_like(v)) for h in range(HEADS_PER_GROUP)], axis=0)


def _unstack_heads(v):
    q = ATTN_BLOCK
    head = _head_of_lane((q, v.shape[1]))
    out = jnp.zeros((q, v.shape[1]), v.dtype)
    for h in range(HEADS_PER_GROUP):
        out = jnp.where(head == h, v[h * q:(h + 1) * q], out)
    return out


def _per_head_rows(col):
    q = ATTN_BLOCK
    head = _head_of_lane((q, GROUP_WIDTH))
    out = jnp.zeros((q, GROUP_WIDTH), col.dtype)
    for h in range(HEADS_PER_GROUP):
        out = jnp.where(head == h, col[h * q:(h + 1) * q], out)
    return out


def _compact_heads(v):
    lane = lax.broadcasted_iota(jnp.int32, (v.shape[0], LANES), 1)
    return jnp.where((lane & 32) == 0, v[:, 0:LANES], v[:, LANES:2 * LANES])


def _compact_head_col(v):
    lane = lax.broadcasted_iota(jnp.int32, v.shape, 1)
    head = ((lane >> 6) & 1) + 2 * ((lane >> 5) & 1)
    cols = [jnp.max(jnp.where(head == h, v, -jnp.inf), axis=1, keepdims=True) for h in range(HEADS_PER_GROUP)]
    return jnp.concatenate(cols, axis=0)


ATTN_BLOCKS_PER_STEP = 4


def _band_bias(first_block):
    rows = HEADS_PER_GROUP * ATTN_BLOCK
    qi = lax.broadcasted_iota(jnp.int32, (rows, 2 * ATTN_BLOCK), 0) % ATTN_BLOCK
    kj = lax.broadcasted_iota(jnp.int32, (rows, 2 * ATTN_BLOCK), 1)
    dist = qi + ATTN_BLOCK - kj
    valid = (dist >= 0) & (dist <= ATTN_BLOCK)
    if first_block:
        valid = valid & (kj >= ATTN_BLOCK)
    return jnp.where(valid, 0.0, NEG_INF).astype(F32)


def _store_band_biases(bias_ref):
    bias_ref[0] = _band_bias(False)
    bias_ref[1] = _band_bias(True)


def _attn_block_specs(g, nb, clamp_last=False):
    q = ATTN_BLOCK
    last = nb // g - 1

    def cur(col, width=GROUP_WIDTH):
        if clamp_last:
            return pl.BlockSpec((None, g * q, width), lambda r, n: (r, jnp.minimum(n, last), col))
        return pl.BlockSpec((None, g * q, width), lambda r, n: (r, n, col))

    def prev(col):
        if clamp_last:
            return pl.BlockSpec((None, q, GROUP_WIDTH), lambda r, n: (r, jnp.clip(n * g - 1, 0, nb - 1), col))
        return pl.BlockSpec((None, q, GROUP_WIDTH), lambda r, n: (r, jnp.maximum(n * g - 1, 0), col))

    return cur, prev


def _attn_fwd(qkv, gi, forward=()):
    d, length, _ = qkv.shape
    nb = length // ATTN_BLOCK
    q = ATTN_BLOCK
    g = min(ATTN_BLOCKS_PER_STEP, nb)
    ns = nb // g
    nf = len(forward)

    def body(*refs):
        q_ref, kp_ref, kc_ref, vp_ref, vc_ref = refs[:5]
        o_ref, lse_ref = refs[5 + nf:7 + nf]
        buf_refs = refs[7 + nf:7 + 2 * nf]
        bias_ref = refs[7 + 2 * nf]
        sems = refs[8 + 2 * nf:]
        n = pl.program_id(1)
        first_step = (pl.program_id(0) == 0) & (n == 0)
        last_step = (pl.program_id(0) == d - 1) & (n == ns - 1)

        @pl.when(first_step)
        def _():
            _store_band_biases(bias_ref)
            for cp in _gather_forward_copies(buf_refs, *sems) if nf else ():
                cp.start()

        kfull = jnp.concatenate([kp_ref[...], kc_ref[...]], axis=0)
        vfull = jnp.concatenate([vp_ref[...], vc_ref[...]], axis=0)
        for j in range(g):
            qs = _stack_heads(q_ref[j * q:(j + 1) * q, :])
            k2 = kfull[j * q:(j + 2) * q]
            v2 = vfull[j * q:(j + 2) * q]
            bias = jnp.where(n == 0, bias_ref[1], bias_ref[0]) if j == 0 else bias_ref[0]
            sc = _dot_nt(qs, k2) * ATTN_SCALE + bias
            m = jnp.max(sc, axis=1, keepdims=True)
            p = jnp.exp(sc - m)
            l = jnp.sum(p, axis=1, keepdims=True)
            of = _dot(p.astype(BF16), v2) / l
            o_ref[j * q:(j + 1) * q, :] = _unstack_heads(of).astype(BF16)
            lse_ref[j * q:(j + 1) * q, :] = _per_head_rows(m + jnp.log(l))

        if nf:
            @pl.when(last_step)
            def _():
                for cp in _gather_forward_copies(buf_refs, *sems):
                    cp.wait()

    cur, prev = _attn_block_specs(g, nb)
    return pl.pallas_call(
        body, name=f"attn_fwd_g{gi}", grid=(d, ns),
        out_shape=[jax.ShapeDtypeStruct((d, length, GROUP_WIDTH), BF16),
                   jax.ShapeDtypeStruct((d, length, GROUP_WIDTH), F32)]
        + [jax.ShapeDtypeStruct(a.shape, a.dtype) for a in forward],
        in_specs=[cur(0), prev(1), cur(1), prev(2), cur(2)] + [_ANY] * nf,
        out_specs=[cur(0), cur(0)] + [_ANY] * nf,
        input_output_aliases={5 + a: 2 + a for a in range(nf)},
        scratch_shapes=[pltpu.VMEM((2, HEADS_PER_GROUP * q, 2 * q), F32)] + (_dma_sems(3 * nf) if nf else []),
        compiler_params=_params(2))(qkv, qkv, qkv, qkv, qkv, *forward)


def _conv_branch(ab, ac, av, halo_u, w, b):
    u = ac * av
    sh1 = _shift_down(u, 1, halo_u)
    sh2 = _shift_down(u, 2, halo_u)
    cv = w[0:1] * sh2 + w[1:2] * sh1 + w[2:3] * u + b
    return ab * cv, cv, u, sh1, sh2


def _mix_fwd(x, abcv, gates, o_list, lse_list, conv_w, conv_b, b_gate, w_pa, w_pb, w_out):
    s = x.shape[0]
    tm = ROW_TILE

    def body(x_ref, abcv_ref, gates_ref, o0_ref, o1_ref, o2_ref, l0_ref, l1_ref, l2_ref,
             cw_ref, cb_ref, bg_ref, wpa_ref, wpb_ref, wout_ref,
             x1_ref, ya0_ref, yb0_ref, mrg_ref, ya_ref, yb_ref, lsetot_ref, halo_ref, scr):
        i = pl.program_id(0)

        @pl.when(i == 0)
        def _():
            halo_ref[...] = jnp.zeros_like(halo_ref)

        ab = abcv_ref[:, 0:CONV_WIDTH].astype(F32)
        ac = abcv_ref[:, CONV_WIDTH:2 * CONV_WIDTH].astype(F32)
        av = abcv_ref[:, 2 * CONV_WIDTH:3 * CONV_WIDTH].astype(F32)
        ya0, _, u, _, _ = _conv_branch(ab, ac, av, halo_ref[...], cw_ref[...], cb_ref[...])
        halo_ref[...] = u[tm - SUBLANES:tm]
        ya0 = ya0.astype(BF16)
        ya0_ref[...] = ya0
        ya = _dot(ya0, wpa_ref[...])

        o_refs, l_refs = (o0_ref, o1_ref, o2_ref), (l0_ref, l1_ref, l2_ref)
        lses = [_from_streams(l_refs[g], scr, DILATIONS[g], 0, GROUP_WIDTH) for g in range(3)]
        top = jnp.maximum(jnp.maximum(lses[0], lses[1]), lses[2])
        lsetot = top + jnp.log(jnp.exp(lses[0] - top) + jnp.exp(lses[1] - top) + jnp.exp(lses[2] - top))
        lsetot_ref[...] = lsetot
        yb = jnp.zeros((tm, D_MODEL), F32)
        for g in range(3):
            og = _from_streams(o_refs[g], scr, DILATIONS[g], 0, GROUP_WIDTH)
            yb0 = (jnp.exp(lses[g] - lsetot) * og).astype(BF16)
            yb0_ref[:, g * GROUP_WIDTH:(g + 1) * GROUP_WIDTH] = yb0
            yb = yb + _dot(yb0, wpb_ref[g * GROUP_WIDTH:(g + 1) * GROUP_WIDTH, :])

        sa = _sigmoid(gates_ref[:, 0:D_MODEL].astype(F32) + bg_ref[0:1, :])
        sb = _sigmoid(gates_ref[:, D_MODEL:2 * D_MODEL].astype(F32) + bg_ref[1:2, :])
        ya_ref[...] = ya.astype(BF16)
        yb_ref[...] = yb.astype(BF16)
        mrg = (sa * ya + sb * yb).astype(BF16)
        mrg_ref[...] = mrg
        x1_ref[...] = x_ref[...] + _dot(mrg, wout_ref[...])

    outs = [jax.ShapeDtypeStruct((s, D_MODEL), F32),
            jax.ShapeDtypeStruct((s, CONV_WIDTH), BF16),
            jax.ShapeDtypeStruct((s, ATTN_WIDTH), BF16),
            jax.ShapeDtypeStruct((s, D_MODEL), BF16),
            jax.ShapeDtypeStruct((s, D_MODEL), BF16),
            jax.ShapeDtypeStruct((s, D_MODEL), BF16),
            jax.ShapeDtypeStruct((s, GROUP_WIDTH), F32)]
    return pl.pallas_call(
        body, name="mix_fwd", grid=(s // tm,), out_shape=outs,
        in_specs=[_rows(tm, D_MODEL), _rows(tm, 3 * CONV_WIDTH), _rows(tm, 2 * D_MODEL)]
        + [_stream_block(tm, d, GROUP_WIDTH) for d in DILATIONS] * 2
        + [_resident((3, CONV_WIDTH)), _resident((1, CONV_WIDTH)), _resident((2, D_MODEL)),
           _resident((CONV_WIDTH, D_MODEL)), _resident((ATTN_WIDTH, D_MODEL)), _resident((D_MODEL, D_MODEL))],
        out_specs=[_rows(tm, D_MODEL), _rows(tm, CONV_WIDTH), _rows(tm, ATTN_WIDTH), _rows(tm, D_MODEL),
                   _rows(tm, D_MODEL), _rows(tm, D_MODEL), _rows(tm, GROUP_WIDTH)],
        scratch_shapes=[pltpu.VMEM((SUBLANES, CONV_WIDTH), F32),
                        pltpu.VMEM((GROUP_WIDTH // LANES * tm, LANES), F32)],
        compiler_params=_params(1))(x, abcv, gates, *o_list, *lse_list, conv_w, conv_b, b_gate, w_pa, w_pb, w_out)


FFN_CHUNK = 512


def _ffn_up_fwd(x1, g, w_up, conv_w, conv_b):
    s = x1.shape[0]
    n = w_up.shape[1]
    tm = ROW_TILE

    def body(x_ref, g_ref, w_ref, cw_ref, cb_ref, h_ref, up0_ref, up_ref, halo_ref):
        @pl.when(pl.program_id(0) == 0)
        def _():
            halo_ref[...] = jnp.zeros_like(halo_ref)

        xv = x_ref[...]
        r = lax.rsqrt(jnp.mean(xv * xv, axis=-1, keepdims=True) + EPS)
        h = (xv * r * g_ref[...]).astype(BF16)
        h_ref[...] = h
        for lo, size in _col_chunks(n, FFN_CHUNK):
            cols = slice(lo, lo + size)
            y = _dot(h, w_ref[:, cols])
            up0_ref[:, cols] = y.astype(BF16)
            halo = halo_ref[:, cols]
            w = cw_ref[:, cols]
            up = w[0:1] * _shift_down(y, 2, halo) + w[1:2] * _shift_down(y, 1, halo) + w[2:3] * y + cb_ref[:, cols]
            up_ref[:, cols] = up.astype(BF16)
            halo_ref[:, cols] = y[tm - SUBLANES:tm]

    return pl.pallas_call(
        body, name="ffn_up_fwd", grid=(s // tm,),
        out_shape=[jax.ShapeDtypeStruct((s, D_MODEL), BF16), jax.ShapeDtypeStruct((s, n), BF16),
                   jax.ShapeDtypeStruct((s, n), BF16)],
        in_specs=[_rows(tm, D_MODEL), _resident((1, D_MODEL)), _resident((D_MODEL, n)), _resident((3, n)),
                  _resident((1, n))],
        out_specs=[_rows(tm, D_MODEL), _rows(tm, n), _rows(tm, n)],
        scratch_shapes=[pltpu.VMEM((SUBLANES, n), F32)],
        compiler_params=_params(1))(x1, g, w_up, conv_w, conv_b)


def _ffn_act_fwd(x1, up, target, w_down, g_final):
    s = x1.shape[0]
    tm = ROW_TILE

    def body(x1_ref, up_ref, tgt_ref, wd_ref, gf_ref, act_ref, dx2_ref, dx2b_ref, dgf_ref, loss_ref):
        @pl.when(pl.program_id(0) == 0)
        def _():
            dgf_ref[...] = jnp.zeros_like(dgf_ref)
            loss_ref[...] = jnp.zeros_like(loss_ref)

        acc = jnp.zeros((tm, D_MODEL), F32)
        for lo, size in _col_chunks(D_FF, FFN_CHUNK):
            gate = up_ref[:, lo:lo + size].astype(F32)
            val = up_ref[:, D_FF + lo:D_FF + lo + size].astype(F32)
            act = (gate * _sigmoid(gate) * val).astype(BF16)
            act_ref[:, lo:lo + size] = act
            acc = acc + _dot(act, wd_ref[lo:lo + size, :])

        x2 = x1_ref[...] + acc
        r = lax.rsqrt(jnp.mean(x2 * x2, axis=-1, keepdims=True) + EPS)
        xn = x2 * r
        gf = gf_ref[...]
        err = xn * gf - tgt_ref[...]
        loss_ref[...] += (0.5 / D_MODEL) * jnp.sum(err * err)
        dy = err * (1.0 / D_MODEL)
        dgf_ref[...] += _colsum(dy * xn)
        dxn = dy * gf
        dx2 = r * (dxn - xn * jnp.mean(dxn * xn, axis=-1, keepdims=True))
        dx2_ref[...] = dx2
        dx2b_ref[...] = dx2.astype(BF16)

    return pl.pallas_call(
        body, name="ffn_act_fwd", grid=(s // tm,),
        out_shape=[jax.ShapeDtypeStruct((s, D_FF), BF16), jax.ShapeDtypeStruct((s, D_MODEL), F32),
                   jax.ShapeDtypeStruct((s, D_MODEL), BF16),
                   jax.ShapeDtypeStruct((1, D_MODEL), F32), jax.ShapeDtypeStruct((1, LANES), F32)],
        in_specs=[_rows(tm, D_MODEL), _rows(tm, 2 * D_FF), _rows(tm, D_MODEL),
                  _resident((D_FF, D_MODEL)), _resident((1, D_MODEL))],
        out_specs=[_rows(tm, D_FF), _rows(tm, D_MODEL), _rows(tm, D_MODEL),
                   pl.BlockSpec((1, D_MODEL), lambda i: (0, 0)), pl.BlockSpec((1, LANES), lambda i: (0, 0))],
        compiler_params=_params(1))(x1, up, target, w_down, g_final)


def _ffn_act_bwd(dx2b, up, w_down):
    s = dx2b.shape[0]
    tm = ROW_TILE

    def body(dx2_ref, up_ref, wd_ref, dup_ref):
        dx2 = dx2_ref[...]
        for lo, size in _col_chunks(D_FF, FFN_CHUNK):
            gate = up_ref[:, lo:lo + size].astype(F32)
            val = up_ref[:, D_FF + lo:D_FF + lo + size].astype(F32)
            dact = _dot_nt(dx2, wd_ref[lo:lo + size, :])
            sg = _sigmoid(gate)
            dup_ref[:, lo:lo + size] = (dact * val * (sg * (1.0 + gate * (1.0 - sg)))).astype(BF16)
            dup_ref[:, D_FF + lo:D_FF + lo + size] = (dact * (gate * sg)).astype(BF16)

    return pl.pallas_call(
        body, name="ffn_act_bwd", grid=(s // tm,),
        out_shape=jax.ShapeDtypeStruct((s, 2 * D_FF), BF16),
        in_specs=[_rows(tm, D_MODEL), _rows(tm, 2 * D_FF), _resident((D_FF, D_MODEL))],
        out_specs=_rows(tm, 2 * D_FF),
        compiler_params=_params(1))(dx2b, up, w_down)


FFN_BWD_ROW_TILE = 256


def _ffn_up_bwd(d_up, up0, w_up, conv_w, x1, g, dres):
    s = x1.shape[0]
    n = w_up.shape[1]
    tm = FFN_BWD_ROW_TILE
    nt = s // tm

    def body(dup_ref, up0_ref, w_ref, cw_ref, x_ref, g_ref, dres_ref,
             dx_ref, dxb_ref, dg_ref, dup0_ref, small_ref, next_ref):
        @pl.when(pl.program_id(0) == 0)
        def _():
            next_ref[...] = jnp.zeros_like(next_ref)
            small_ref[...] = jnp.zeros_like(small_ref)
            dg_ref[...] = jnp.zeros_like(dg_ref)

        dh = jnp.zeros((tm, D_MODEL), F32)
        for lo, size in _col_chunks(n, FFN_CHUNK):
            cols = slice(lo, lo + size)
            dz = dup_ref[:, cols].astype(F32)
            x0 = up0_ref[:, cols].astype(F32)
            nxt = next_ref[:, cols]
            dz1 = _shift_up(dz, 1, nxt)
            dz2 = _shift_up(dz, 2, nxt)
            next_ref[:, cols] = dz[0:SUBLANES]
            small_ref[0:1, cols] += _colsum(dz2 * x0)
            small_ref[1:2, cols] += _colsum(dz1 * x0)
            small_ref[2:3, cols] += _colsum(dz * x0)
            small_ref[3:4, cols] += _colsum(dz)
            w = cw_ref[:, cols]
            dup0 = (w[2:3] * dz + w[1:2] * dz1 + w[0:1] * dz2).astype(BF16)
            dup0_ref[:, cols] = dup0
            dh = dh + _dot_nt(dup0, w_ref[:, cols])
        xv = x_ref[...]
        r = lax.rsqrt(jnp.mean(xv * xv, axis=-1, keepdims=True) + EPS)
        xn = xv * r
        dg_ref[...] += _colsum(dh * xn)
        dxn = dh * g_ref[...]
        dx = dres_ref[...] + r * (dxn - xn * jnp.mean(dxn * xn, axis=-1, keepdims=True))
        dx_ref[...] = dx
        dxb_ref[...] = dx.astype(BF16)

    rows = lambda width: pl.BlockSpec((tm, width), lambda i: (nt - 1 - i, 0))
    return pl.pallas_call(
        body, name="ffn_up_bwd", grid=(nt,),
        out_shape=[jax.ShapeDtypeStruct((s, D_MODEL), F32), jax.ShapeDtypeStruct((s, D_MODEL), BF16),
                   jax.ShapeDtypeStruct((1, D_MODEL), F32), jax.ShapeDtypeStruct((s, n), BF16),
                   jax.ShapeDtypeStruct((SUBLANES, n), F32)],
        in_specs=[rows(n), rows(n), _resident((D_MODEL, n)), _resident((3, n)), rows(D_MODEL),
                  _resident((1, D_MODEL)), rows(D_MODEL)],
        out_specs=[rows(D_MODEL), rows(D_MODEL), pl.BlockSpec((1, D_MODEL), lambda i: (0, 0)), rows(n),
                   pl.BlockSpec((SUBLANES, n), lambda i: (0, 0))],
        scratch_shapes=[pltpu.VMEM((SUBLANES, n), F32)],
        compiler_params=_params(1))(d_up, up0, w_up, conv_w, x1, g, dres)


def _inproj_bwd(d_abcv, d_gates, d_qkvs, w_in_t, x, g, dres):
    s = x.shape[0]
    tm = ROW_TILE
    gw = GROUP_WIDTH

    def body(dabcv_ref, dgates_ref, dq0_ref, dq1_ref, dq2_ref, w_ref, x_ref, g_ref, dres_ref, dx_ref, dg_ref, scr):
        @pl.when(pl.program_id(0) == 0)
        def _():
            dg_ref[...] = jnp.zeros_like(dg_ref)

        dh = jnp.zeros((tm, D_MODEL), F32)
        for src, width, wrow in ((dabcv_ref, 3 * CONV_WIDTH, COL_ABCV), (dgates_ref, 2 * D_MODEL, COL_GATES)):
            for lo, size in _col_chunks(width, 512):
                dh = dh + _dot(src[:, lo:lo + size], w_ref[wrow + lo:wrow + lo + size, :])
        for gi, (d, dq_ref) in enumerate(zip(DILATIONS, (dq0_ref, dq1_ref, dq2_ref))):
            for j, base in enumerate((COL_Q, COL_K, COL_V)):
                dy = _from_streams(dq_ref, scr, d, j * gw, gw).astype(BF16)
                wrow = base + gi * gw
                dh = dh + _dot(dy, w_ref[wrow:wrow + gw, :])
        xv = x_ref[...]
        r = lax.rsqrt(jnp.mean(xv * xv, axis=-1, keepdims=True) + EPS)
        xn = xv * r
        dg_ref[...] += _colsum(dh * xn)
        dxn = dh * g_ref[...]
        dx_ref[...] = dres_ref[...] + r * (dxn - xn * jnp.mean(dxn * xn, axis=-1, keepdims=True))

    return pl.pallas_call(
        body, name="inproj_bwd", grid=(s // tm,),
        out_shape=[jax.ShapeDtypeStruct((s, D_MODEL), F32), jax.ShapeDtypeStruct((1, D_MODEL), F32)],
        in_specs=[_rows(tm, 3 * CONV_WIDTH), _rows(tm, 2 * D_MODEL)]
        + [_stream_block(tm, d, 3 * gw) for d in DILATIONS]
        + [_resident((D_IN, D_MODEL)), _rows(tm, D_MODEL), _resident((1, D_MODEL)), _rows(tm, D_MODEL)],
        out_specs=[_rows(tm, D_MODEL), pl.BlockSpec((1, D_MODEL), lambda i: (0, 0))],
        scratch_shapes=[pltpu.VMEM((gw // LANES * tm, LANES), F32)],
        compiler_params=_params(1))(d_abcv, d_gates, *d_qkvs, w_in_t, x, g, dres)


def _mix_bwd(dx1, abcv, gates, ya, yb, yb0, lsetot, conv_w, conv_b, b_gate, w_pa, w_pb, w_out, exchange=()):
    s = dx1.shape[0]
    tm = ROW_TILE
    nt = s // tm
    hb = tm // (2 * SUBLANES)
    nx = len(exchange)

    def body(*refs):
        (dx1_ref, abcv_ref, pre_ref, gates_ref, ya_ref, yb_ref, yb0_ref, lsetot_ref,
         cw_ref, cb_ref, bg_ref, wpa_ref, wpb_ref, wout_ref) = refs[:14]
        part_refs = refs[14:14 + nx]
        (dya_ref, dyb_ref, dgates_ref, dabcv_ref, dyb0_ref, dyl0_ref, dyl1_ref, dyl2_ref, aux0_ref, aux1_ref,
         aux2_ref, sm_gate_ref, sm_conv_ref) = refs[14 + nx:27 + nx]
        recv_refs = refs[27 + nx:27 + 2 * nx]
        next_ref, scr = refs[27 + 2 * nx:29 + 2 * nx]
        sems = refs[29 + 2 * nx:]
        i = pl.program_id(0)

        @pl.when(i == 0)
        def _():
            next_ref[...] = jnp.zeros_like(next_ref)
            sm_gate_ref[...] = jnp.zeros_like(sm_gate_ref)
            sm_conv_ref[...] = jnp.zeros_like(sm_conv_ref)
            for cp in _chip_exchange_copies(part_refs, recv_refs, *sems) if nx else ():
                cp.start()

        not_first = (i < nt - 1).astype(F32)
        dm = _dot_nt(dx1_ref[...].astype(BF16), wout_ref[...])
        sa = _sigmoid(gates_ref[:, 0:D_MODEL].astype(F32) + bg_ref[0:1, :])
        sb = _sigmoid(gates_ref[:, D_MODEL:2 * D_MODEL].astype(F32) + bg_ref[1:2, :])
        dya = (dm * sa).astype(BF16)
        dyb = (dm * sb).astype(BF16)
        dya_ref[...] = dya
        dyb_ref[...] = dyb
        dga = dm * ya_ref[...].astype(F32) * (sa * (1.0 - sa))
        dgb = dm * yb_ref[...].astype(F32) * (sb * (1.0 - sb))
        dgates_ref[:, 0:D_MODEL] = dga.astype(BF16)
        dgates_ref[:, D_MODEL:2 * D_MODEL] = dgb.astype(BF16)
        sm_gate_ref[0:1, :] += _colsum(dga)
        sm_gate_ref[1:2, :] += _colsum(dgb)

        dya0 = _dot_nt(dya, wpa_ref[...])
        ab = abcv_ref[:, 0:CONV_WIDTH].astype(F32)
        ac = abcv_ref[:, CONV_WIDTH:2 * CONV_WIDTH].astype(F32)
        av = abcv_ref[:, 2 * CONV_WIDTH:3 * CONV_WIDTH].astype(F32)
        pre = pre_ref[...].astype(F32) * not_first
        halo_u = (pre[:, CONV_WIDTH:2 * CONV_WIDTH] * pre[:, 2 * CONV_WIDTH:3 * CONV_WIDTH])[SUBLANES:2 * SUBLANES]
        w = cw_ref[...]
        _, cv, u, sh1, sh2 = _conv_branch(ab, ac, av, halo_u, w, cb_ref[...])
        dcv = dya0 * ab
        sm_conv_ref[0:1, :] += _colsum(dcv * sh2)
        sm_conv_ref[1:2, :] += _colsum(dcv * sh1)
        sm_conv_ref[2:3, :] += _colsum(dcv * u)
        sm_conv_ref[3:4, :] += _colsum(dcv)
        nxt = next_ref[...]
        du = w[2:3] * dcv + w[1:2] * _shift_up(dcv, 1, nxt) + w[0:1] * _shift_up(dcv, 2, nxt)
        next_ref[...] = dcv[0:SUBLANES]
        dabcv_ref[:, 0:CONV_WIDTH] = (dya0 * cv).astype(BF16)
        dabcv_ref[:, CONV_WIDTH:2 * CONV_WIDTH] = (du * av).astype(BF16)
        dabcv_ref[:, 2 * CONV_WIDTH:3 * CONV_WIDTH] = (du * ac).astype(BF16)

        head_r = lax.broadcasted_iota(jnp.int32, (GROUP_WIDTH, GROUP_WIDTH), 0) // HEAD_DIM
        head_c = lax.broadcasted_iota(jnp.int32, (GROUP_WIDTH, GROUP_WIDTH), 1) // HEAD_DIM
        same_head = (head_r == head_c).astype(BF16)
        prod = jnp.zeros((tm, GROUP_WIDTH), F32)
        dyb0s = []
        for g in range(3):
            cols = slice(g * GROUP_WIDTH, (g + 1) * GROUP_WIDTH)
            dyb0 = _dot_nt(dyb, wpb_ref[cols, :])
            dyb0_ref[:, cols] = dyb0.astype(BF16)
            dyb0s.append(dyb0)
            prod = prod + dyb0 * yb0_ref[:, cols].astype(F32)
        hi = prod.astype(BF16)
        mid = (prod - hi.astype(F32)).astype(BF16)
        lo = (prod - hi.astype(F32) - mid.astype(F32)).astype(BF16)
        delta = _dot(hi, same_head) + _dot(mid, same_head) + _dot(lo, same_head)
        lse_c = _compact_heads(lsetot_ref[...])
        delta_c = _compact_heads(delta)
        for g, (dy_ref, aux_ref) in enumerate(zip((dyl0_ref, dyl1_ref, dyl2_ref), (aux0_ref, aux1_ref, aux2_ref))):
            d = DILATIONS[g]
            _to_streams(dyb0s[g], scr, dy_ref, d, 0)
            _to_streams(lse_c, scr, aux_ref, d, 0)
            _to_streams(delta_c, scr, aux_ref, d, LANES)

        if nx:
            @pl.when(i == nt - 1)
            def _():
                for cp in _chip_exchange_copies(part_refs, recv_refs, *sems):
                    cp.wait()

    rev = lambda i: (nt - 1 - i, 0)
    pre = lambda i: (jnp.maximum((nt - 1 - i) * hb - 1, 0), 0)
    rows = lambda width: pl.BlockSpec((tm, width), rev)
    outs = [jax.ShapeDtypeStruct((s, D_MODEL), BF16), jax.ShapeDtypeStruct((s, D_MODEL), BF16),
            jax.ShapeDtypeStruct((s, 2 * D_MODEL), BF16), jax.ShapeDtypeStruct((s, 3 * CONV_WIDTH), BF16),
            jax.ShapeDtypeStruct((s, ATTN_WIDTH), BF16)]
    outs += [jax.ShapeDtypeStruct((d, s // d, GROUP_WIDTH), BF16) for d in DILATIONS]
    outs += [jax.ShapeDtypeStruct((d, s // d, 2 * LANES), F32) for d in DILATIONS]
    outs += [jax.ShapeDtypeStruct((SUBLANES, D_MODEL), F32), jax.ShapeDtypeStruct((SUBLANES, CONV_WIDTH), F32)]
    outs += [jax.ShapeDtypeStruct((3,) + a.shape[1:], a.dtype) for a in exchange]
    return pl.pallas_call(
        body, name="mix_bwd", grid=(nt,), out_shape=outs,
        in_specs=[rows(D_MODEL), rows(3 * CONV_WIDTH), pl.BlockSpec((2 * SUBLANES, 3 * CONV_WIDTH), pre),
                  rows(2 * D_MODEL), rows(D_MODEL), rows(D_MODEL), rows(ATTN_WIDTH), rows(GROUP_WIDTH),
                  _resident((3, CONV_WIDTH)), _resident((1, CONV_WIDTH)), _resident((2, D_MODEL)),
                  _resident((CONV_WIDTH, D_MODEL)), _resident((ATTN_WIDTH, D_MODEL)), _resident((D_MODEL, D_MODEL))]
        + [_ANY] * nx,
        out_specs=[rows(D_MODEL), rows(D_MODEL), rows(2 * D_MODEL), rows(3 * CONV_WIDTH), rows(ATTN_WIDTH)]
        + [_rev_stream_block(tm, d, GROUP_WIDTH, nt) for d in DILATIONS]
        + [_rev_stream_block(tm, d, 2 * LANES, nt) for d in DILATIONS]
        + [pl.BlockSpec((SUBLANES, D_MODEL), lambda i: (0, 0)), pl.BlockSpec((SUBLANES, CONV_WIDTH), lambda i: (0, 0))]
        + [_ANY] * nx,
        scratch_shapes=[pltpu.VMEM((SUBLANES, CONV_WIDTH), F32),
                        pltpu.VMEM((GROUP_WIDTH // LANES * tm, LANES), F32)] + (_dma_sems(3 * nx) if nx else []),
        compiler_params=_params(1))(dx1, abcv, abcv, gates, ya, yb, yb0, lsetot,
                                    conv_w, conv_b, b_gate, w_pa, w_pb, w_out, *exchange)


def _attn_bwd(qkv, dy, aux, gi, exchange=()):
    d, length, _ = qkv.shape
    nb = length // ATTN_BLOCK
    q = ATTN_BLOCK
    gw = GROUP_WIDTH
    g = min(ATTN_BLOCKS_PER_STEP, nb)
    assert g >= 2 and nb % g == 0
    ns = nb // g
    lag = 1 if ns > 1 else 0
    tail = (g - 1) * q
    nx = len(exchange)

    def body(*refs):
        q_ref, kp_ref, kc_ref, vp_ref, vc_ref, dy_ref, aux_ref = refs[:7]
        part_refs = refs[7:7 + nx]
        out_ref = refs[7 + nx]
        recv_refs = refs[8 + nx:8 + 2 * nx]
        dq_ref, dkv_ref, bias_ref = refs[8 + 2 * nx:11 + 2 * nx]
        sems = refs[11 + 2 * nx:]
        n = pl.program_id(1)

        @pl.when((pl.program_id(0) == 0) & (n == 0))
        def _():
            _store_band_biases(bias_ref)
            for cp in _chip_exchange_copies(part_refs, recv_refs, *sems) if nx else ():
                cp.start()

        if nx:
            @pl.when((pl.program_id(0) == d - 1) & (n == ns - 1 + lag))
            def _():
                for cp in _chip_exchange_copies(part_refs, recv_refs, *sems):
                    cp.wait()

        def emit(rows):
            out_ref[rows, gw:2 * gw] = dkv_ref[0, rows].astype(BF16)
            out_ref[rows, 2 * gw:3 * gw] = dkv_ref[1, rows].astype(BF16)

        if lag:
            @pl.when(n > 0)
            def _():
                out_ref[:, 0:gw] = dq_ref[...].astype(BF16)
                emit(slice(0, tail))

            @pl.when(n == ns)
            def _():
                emit(slice(tail, g * q))

        @pl.when(n < ns)
        def _():
            kfull = jnp.concatenate([kp_ref[...], kc_ref[...]], axis=0)
            vfull = jnp.concatenate([vp_ref[...], vc_ref[...]], axis=0)
            for j in range(g):
                rows = slice(j * q, (j + 1) * q)
                qs = _stack_heads(q_ref[rows, :])
                dys = _stack_heads(dy_ref[rows, :])
                k2 = kfull[j * q:(j + 2) * q]
                v2 = vfull[j * q:(j + 2) * q]
                lse = _compact_head_col(aux_ref[rows, 0:LANES])
                delta = _compact_head_col(aux_ref[rows, LANES:2 * LANES])
                bias = jnp.where(n == 0, bias_ref[1], bias_ref[0]) if j == 0 else bias_ref[0]
                p = jnp.exp(_dot_nt(qs, k2) * ATTN_SCALE + bias - lse)
                dp = _dot_nt(dys, v2)
                ds = (p * (dp - delta) * ATTN_SCALE).astype(BF16)
                dq_j = _unstack_heads(_dot(ds, k2))
                dk2 = _dot_tn(ds, qs)
                dv2 = _dot_tn(p.astype(BF16), dys)
                if j == 0:
                    @pl.when(n > 0)
                    def _():
                        out_ref[tail:g * q, gw:2 * gw] = (dkv_ref[0, tail:g * q] + dk2[0:q]).astype(BF16)
                        out_ref[tail:g * q, 2 * gw:3 * gw] = (dkv_ref[1, tail:g * q] + dv2[0:q]).astype(BF16)
                else:
                    dkv_ref[0, (j - 1) * q:j * q] += dk2[0:q]
                    dkv_ref[1, (j - 1) * q:j * q] += dv2[0:q]
                dkv_ref[0, rows] = dk2[q:2 * q]
                dkv_ref[1, rows] = dv2[q:2 * q]
                dq_ref[rows, :] = dq_j
            if not lag:
                out_ref[:, 0:gw] = dq_ref[...].astype(BF16)
                emit(slice(0, g * q))

    cur, prev = _attn_block_specs(g, nb, clamp_last=True)
    return pl.pallas_call(
        body, name=f"attn_bwd_g{gi}", grid=(d, ns + lag),
        out_shape=[jax.ShapeDtypeStruct((d, length, 3 * gw), BF16)]
        + [jax.ShapeDtypeStruct((3,) + a.shape[1:], a.dtype) for a in exchange],
        in_specs=[cur(0), prev(1), cur(1), prev(2), cur(2), cur(0), cur(0, 2 * LANES)] + [_ANY] * nx,
        out_specs=[pl.BlockSpec((None, g * q, 3 * gw), lambda r, n: (r, jnp.maximum(n - lag, 0), 0))] + [_ANY] * nx,
        scratch_shapes=[pltpu.VMEM((g * q, gw), F32), pltpu.VMEM((2, g * q, gw), F32),
                        pltpu.VMEM((2, HEADS_PER_GROUP * q, 2 * q), F32)] + (_dma_sems(3 * nx) if nx else []),
        compiler_params=_params(2))(qkv, qkv, qkv, qkv, qkv, dy, aux, *exchange)


def _matmul_tn(name, a, b, col_tile=1024, row_tile=2048, slabs=0):
    s, k = a.shape
    n = b.shape[1]
    tk = min(row_tile, s)
    tn = col_tile
    steps = s // tk

    def body(a_ref, b_ref, o_ref, acc_ref):
        t = pl.program_id(1)

        @pl.when(t == 0)
        def _():
            acc_ref[...] = jnp.zeros_like(acc_ref)

        acc_ref[...] += _dot_tn(a_ref[...], b_ref[...])

        @pl.when(t == steps - 1)
        def _():
            if slabs:
                for q in range(per_tile):
                    o_ref[q] = acc_ref[:, q * width:(q + 1) * width].astype(BF16)
            else:
                o_ref[...] = acc_ref[...].astype(BF16)

    if slabs:
        width = n // slabs
        per_tile = tn // width
        out_shape = jax.ShapeDtypeStruct((slabs, k, width), BF16)
        out_spec = pl.BlockSpec((per_tile, k, width), lambda j, t: (j, 0, 0))
    else:
        out_shape = jax.ShapeDtypeStruct((k, n), BF16)
        out_spec = pl.BlockSpec((k, tn), lambda j, t: (0, j))
    return pl.pallas_call(
        body, name=name, grid=(n // tn, steps), out_shape=out_shape,
        in_specs=[pl.BlockSpec((tk, k), lambda j, t: (t, 0)), pl.BlockSpec((tk, tn), lambda j, t: (t, j))],
        out_specs=out_spec, scratch_shapes=[pltpu.VMEM((k, tn), F32)],
        compiler_params=_params(2))(a, b)


def _weight_allgather(bigs, smalls):
    nb, ns = len(bigs), len(smalls)
    n_sems = 3 * (2 * nb + ns)

    def body(*refs):
        big_refs, small_refs = refs[:nb], refs[nb:nb + ns]
        big_outs, small_outs = refs[nb + ns:2 * nb + ns], refs[2 * nb + ns:2 * (nb + ns)]
        send_sems, recv_sems = refs[2 * (nb + ns):]
        x, y, c, chips = _mesh_position()
        me = 2 * x + y
        sibling = (x, y, 1 - c)

        def copy(k, src, dst, to):
            return pltpu.make_async_remote_copy(src_ref=src, dst_ref=dst, send_sem=send_sems.at[k],
                                                recv_sem=recv_sems.at[k], device_id=to, device_id_type=MESH_ID)

        halves = [r.shape[1] // 2 for r in big_refs]
        first = []
        for j, (px, py) in enumerate(chips):
            for b in range(nb):
                mine = _half_rows(c, halves[b])
                first.append(copy(3 * b + j, big_refs[b].at[0, mine], big_outs[b].at[me, mine], (px, py, c)))
            for s in range(ns):
                first.append(copy(3 * (2 * nb + s) + j, small_refs[s].at[0], small_outs[s].at[me], (px, py, c)))
        for cp in first:
            cp.start()
        passed = []
        for j, (px, py) in enumerate(chips):
            for b in range(nb):
                landed = big_outs[b].at[2 * px + py, _half_rows(c, halves[b])]
                copy(3 * b + j, landed, landed, (px, py, c)).wait_recv()
                fwd = copy(3 * (nb + b) + j, landed, landed, sibling)
                fwd.start()
                passed.append(fwd)
        for j, (px, py) in enumerate(chips):
            for s in range(ns):
                landed = small_outs[s].at[2 * px + py]
                copy(3 * (2 * nb + s) + j, landed, landed, (px, py, c)).wait_recv()
            for b in range(nb):
                from_sibling = big_outs[b].at[2 * px + py, _half_rows(1 - c, halves[b])]
                copy(3 * (nb + b) + j, from_sibling, from_sibling, sibling).wait_recv()
        for cp in first + passed:
            cp.wait_send()

    return pl.pallas_call(
        body, name="weight_allgather",
        out_shape=[jax.ShapeDtypeStruct((N_CHIPS,) + a.shape[1:], a.dtype) for a in list(bigs) + list(smalls)],
        in_specs=[_ANY] * (nb + ns), out_specs=[_ANY] * (nb + ns),
        scratch_shapes=[pltpu.SemaphoreType.DMA((n_sems,)), pltpu.SemaphoreType.DMA((n_sems,))],
    )(*bigs, *smalls)


def _sibling_swap_halves(name, slabs):
    na = len(slabs)

    def body(*refs):
        src_refs, out_refs = refs[:na], refs[na:2 * na]
        send_sems, recv_sems = refs[2 * na:]
        x, y, c, _ = _mesh_position()
        cps = []
        for a in range(na):
            theirs = _half_rows(1 - c, src_refs[a].shape[1] // 2)
            cps.append(pltpu.make_async_remote_copy(
                src_ref=src_refs[a].at[:, theirs, :], dst_ref=out_refs[a], send_sem=send_sems.at[a],
                recv_sem=recv_sems.at[a], device_id=(x, y, 1 - c), device_id_type=MESH_ID))
        for cp in cps:
            cp.start()
        for cp in cps:
            cp.wait()

    return pl.pallas_call(
        body, name=name,
        out_shape=[jax.ShapeDtypeStruct((a.shape[0], a.shape[1] // 2, a.shape[2]), a.dtype) for a in slabs],
        in_specs=[_ANY] * na, out_specs=[_ANY] * na,
        scratch_shapes=[pltpu.SemaphoreType.DMA((na,)), pltpu.SemaphoreType.DMA((na,))])(*slabs)


_HBM = pl.BlockSpec(memory_space=pltpu.HBM)
_SEM = pl.BlockSpec(memory_space=pltpu.SEMAPHORE)
_DATAFLOW = pltpu.SideEffectType.DATAFLOW_SIDE_EFFECTING


def _chip_exchange_start(partial):
    _, rows, cols = partial.shape
    landing = jax.ShapeDtypeStruct((3, rows, cols), partial.dtype)

    def body(src_ref, land_ref, send_sems, recv_sems, src_thru, land_thru, token):
        for cp in _chip_exchange_copies([src_ref], [land_ref], send_sems, recv_sems):
            cp.start()
        token[...] = jnp.zeros_like(token)

    return pl.pallas_call(
        body, name="grad_exchange_start",
        out_shape=(pltpu.SemaphoreType.DMA((3,)), pltpu.SemaphoreType.DMA((3,)),
                   pltpu.HBM(partial.shape, partial.dtype), pltpu.HBM(landing.shape, landing.dtype),
                   jax.ShapeDtypeStruct((SUBLANES, LANES), F32)),
        in_specs=(_HBM, _HBM), out_specs=(_SEM, _SEM, _HBM, _HBM, _VMEM), input_output_aliases={0: 2, 1: 3},
        compiler_params=pltpu.CompilerParams(has_side_effects=_DATAFLOW),
    )(pltpu.with_memory_space_constraint(partial, pltpu.HBM),
      pltpu.with_memory_space_constraint(lax.empty(landing.shape, landing.dtype), pltpu.HBM))


def _chip_exchange_wait(send_sems, recv_sems, src_thru, land_thru, after):
    def body(src_ref, land_ref, send_sems, recv_sems, after_ref, src_out, land_out):
        for cp in _chip_exchange_copies([src_ref], [land_ref], send_sems, recv_sems):
            cp.wait_send()
            cp.wait_recv()

    return pl.pallas_call(
        body, name="grad_exchange_wait",
        out_shape=(pltpu.HBM(src_thru.shape, src_thru.dtype), pltpu.HBM(land_thru.shape, land_thru.dtype)),
        in_specs=(_HBM, _HBM, _SEM, _SEM, _ANY), out_specs=(_HBM, _HBM), input_output_aliases={0: 0, 1: 1},
        compiler_params=pltpu.CompilerParams(has_side_effects=_DATAFLOW),
    )(src_thru, land_thru, send_sems, recv_sems, after)


def _sibling_share(halves):
    na = len(halves)

    def body(*refs):
        out_refs = refs[na:2 * na]
        send_sems, recv_sems = refs[2 * na:]
        x, y, c, _ = _mesh_position()
        cps = []
        for a in range(na):
            mine = out_refs[a].at[0, _half_rows(c, out_refs[a].shape[1] // 2)]
            cps.append(pltpu.make_async_remote_copy(src_ref=mine, dst_ref=mine, send_sem=send_sems.at[a],
                                                    recv_sem=recv_sems.at[a], device_id=(x, y, 1 - c),
                                                    device_id_type=MESH_ID))
        for cp in cps:
            cp.start()
        for a, cp in enumerate(cps):
            cp.wait_send()
            theirs = out_refs[a].at[0, _half_rows(1 - c, out_refs[a].shape[1] // 2)]
            pltpu.make_async_remote_copy(src_ref=theirs, dst_ref=theirs, send_sem=send_sems.at[a],
                                         recv_sem=recv_sems.at[a], device_id=(x, y, 1 - c),
                                         device_id_type=MESH_ID).wait_recv()

    return pl.pallas_call(
        body, name="grad_sibling_share", out_shape=[jax.ShapeDtypeStruct(a.shape, a.dtype) for a in halves],
        in_specs=[_ANY] * na, out_specs=[_ANY] * na, input_output_aliases={a: a for a in range(na)},
        scratch_shapes=[pltpu.SemaphoreType.DMA((na,)), pltpu.SemaphoreType.DMA((na,))])(*halves)


def _add_sibling(name, slab, received, core):
    n, rows, cols = slab.shape
    half = rows // 2

    def body(core_ref, a_ref, b_ref, o_ref):
        o_ref[...] = (a_ref[...].astype(F32) + b_ref[...].astype(F32)).astype(BF16)

    grid_spec = pltpu.PrefetchScalarGridSpec(
        num_scalar_prefetch=1, grid=(n,),
        in_specs=[pl.BlockSpec((None, half, cols), lambda s, core_ref: (s, core_ref[0], 0)),
                  pl.BlockSpec((None, half, cols), lambda s, core_ref: (s, 0, 0))],
        out_specs=pl.BlockSpec((None, half, cols), lambda s, core_ref: (s, 0, 0)))
    return pl.pallas_call(body, name=name, grid_spec=grid_spec,
                          out_shape=jax.ShapeDtypeStruct((n, half, cols), BF16),
                          compiler_params=_params(1))(core, slab, received)


def _sum_chips(name, partial, received, chip_core):
    _, half, cols = partial.shape

    def body(cc_ref, own_ref, recv_ref, o_ref):
        acc = own_ref[...].astype(F32)
        for k in range(3):
            acc = acc + recv_ref[k].astype(F32)
        o_ref[...] = acc

    grid_spec = pltpu.PrefetchScalarGridSpec(
        num_scalar_prefetch=1, grid=(1,),
        in_specs=[pl.BlockSpec((None, half, cols), lambda i, cc_ref: (cc_ref[0], 0, 0)),
                  pl.BlockSpec((3, half, cols), lambda i, cc_ref: (0, 0, 0))],
        out_specs=pl.BlockSpec((None, half, cols), lambda i, cc_ref: (0, cc_ref[1], 0)))
    return pl.pallas_call(body, name=name, grid_spec=grid_spec,
                          out_shape=jax.ShapeDtypeStruct((1, 2 * half, cols), F32),
                          compiler_params=_params(1))(chip_core, partial, received)


def _adam_math(w, g, m, v):
    nm = ADAM_B1 * m + (1.0 - ADAM_B1) * g
    nv = ADAM_B2 * v + (1.0 - ADAM_B2) * jnp.square(g)
    m_hat = nm / (1.0 - ADAM_B1 ** ADAM_STEP)
    v_hat = nv / (1.0 - ADAM_B2 ** ADAM_STEP)
    delta = -ADAM_LR * (m_hat / (jnp.sqrt(v_hat) + ADAM_EPS) + ADAM_WD * w)
    return delta, nm, nv


def _adamw(name, w, g, m, v):
    _, rows, cols = w.shape
    tr = next(t for t in (736, 512, 384, 352, 256, 128, 64, 32, 16, 8) if rows % t == 0)

    def body(w_ref, g_ref, m_ref, v_ref, d_ref, nm_ref, nv_ref):
        d_ref[...], nm_ref[...], nv_ref[...] = _adam_math(w_ref[...], g_ref[...], m_ref[...], v_ref[...])

    spec = pl.BlockSpec((None, tr, cols), lambda i: (0, i, 0))
    return pl.pallas_call(
        body, name=name, grid=(rows // tr,), out_shape=[jax.ShapeDtypeStruct(w.shape, F32)] * 3,
        in_specs=[spec] * 4, out_specs=[spec] * 3, compiler_params=_params(1))(w, g, m, v)


SMALL_PARAMS = ("norm_mix_g", "b_gate", "conv_a_w", "conv_a_b", "norm_ffn_g", "ffn_conv_w", "ffn_conv_b", "final_norm_g")


def _small_update(partials, params, moments_m, moments_v):
    na = len(partials)
    npar = len(SMALL_PARAMS)

    def body(*refs):
        in_refs = refs[:na]
        w_refs = refs[na:na + npar]
        m_refs = refs[na + npar:na + 2 * npar]
        v_refs = refs[na + 2 * npar:na + 3 * npar]
        pos = na + 3 * npar
        loss_ref = refs[pos]
        out_refs = refs[pos + 1:pos + 1 + 4 * npar]
        pos += 1 + 4 * npar
        acc_refs = refs[pos:pos + na]
        recv_refs = refs[pos + na:pos + 4 * na]
        send_sems, recv_sems = refs[pos + 4 * na:]
        x, y, c, _ = _mesh_position()
        chip = 2 * x + y
        for a in range(na):
            acc_refs[a][...] = in_refs[a][...]
        for stage, peer in enumerate(((x, y, 1 - c), (x, 1 - y, c), (1 - x, y, c))):
            cps = []
            for a in range(na):
                k = stage * na + a
                cps.append(pltpu.make_async_remote_copy(src_ref=acc_refs[a], dst_ref=recv_refs[k], send_sem=send_sems.at[k],
                                                        recv_sem=recv_sems.at[k], device_id=peer, device_id_type=MESH_ID))
            for cp in cps:
                cp.start()
            for cp in cps:
                cp.wait()
            for a in range(na):
                acc_refs[a][...] = acc_refs[a][...] + recv_refs[stage * na + a][...]

        mix, ffn, fin, gate, conv, ffnc, loss = acc_refs
        loss_ref[...] = loss[...]

        def cols(width):
            return pl.ds(pl.multiple_of(chip * width, LANES), width)

        grads = {
            "norm_mix_g": mix[...], "norm_ffn_g": ffn[...], "final_norm_g": fin[...],
            "b_gate": gate[0:2, cols(D_MODEL // N_CHIPS)],
            "conv_a_w": conv[0:3, cols(CONV_WIDTH // N_CHIPS)], "conv_a_b": conv[3:4, :],
            "ffn_conv_w": ffnc[0:3, cols(2 * D_FF // N_CHIPS)], "ffn_conv_b": ffnc[3:4, :]}
        for i, name in enumerate(SMALL_PARAMS):
            g = grads[name]
            if len(w_refs[i].shape) == 3:
                results = (g,) + _adam_math(w_refs[i][0], g, m_refs[i][0], v_refs[i][0])
                for o_ref, val in zip(out_refs[4 * i:4 * i + 4], results):
                    o_ref[0] = val
            else:
                results = (g,) + _adam_math(w_refs[i][...], g, m_refs[i][...], v_refs[i][...])
                for o_ref, val in zip(out_refs[4 * i:4 * i + 4], results):
                    o_ref[...] = val

    outs = [jax.ShapeDtypeStruct(partials[-1].shape, F32)]
    for w in params:
        outs += [jax.ShapeDtypeStruct(w.shape, F32)] * 4
    scratch = [pltpu.VMEM(p.shape, F32) for p in partials]
    scratch += [pltpu.VMEM(p.shape, F32) for _ in range(3) for p in partials]
    scratch += [pltpu.SemaphoreType.DMA((3 * na,)), pltpu.SemaphoreType.DMA((3 * na,))]
    n_in = na + 3 * npar
    return pl.pallas_call(
        body, name="small_update", out_shape=outs, in_specs=[_VMEM] * n_in, out_specs=[_VMEM] * len(outs),
        scratch_shapes=scratch)(*partials, *params, *moments_m, *moments_v)


def _gathered_columns(g):
    return jnp.transpose(g, (1, 0, 2)).reshape(g.shape[1], N_CHIPS * g.shape[2])


def _column_slabs(full):
    k, n = full.shape
    return jnp.transpose(full.reshape(k, N_CHIPS, n // N_CHIPS), (1, 0, 2))


def kernel(x, norm_mix_g, w_in, b_gate, conv_a_w, conv_a_b, w_proj_a, w_proj_b, w_out, norm_ffn_g, w_up, ffn_conv_w, ffn_conv_b, w_down, final_norm_g, loss_target, m_norm_mix_g, m_w_in, m_b_gate, m_conv_a_w, m_conv_a_b, m_w_proj_a, m_w_proj_b, m_w_out, m_norm_ffn_g, m_w_up, m_ffn_conv_w, m_ffn_conv_b, m_w_down, m_final_norm_g, v_norm_mix_g, v_w_in, v_b_gate, v_conv_a_w, v_conv_a_b, v_w_proj_a, v_w_proj_b, v_w_out, v_norm_ffn_g, v_w_up, v_ffn_conv_w, v_ffn_conv_b, v_w_down, v_final_norm_g):
    chip = (2 * lax.axis_index("x") + lax.axis_index("y")).astype(jnp.int32)
    core = lax.axis_index("c").astype(jnp.int32)
    core_arr = core.reshape(1)
    chip_core = jnp.stack([chip, core])
    xs, target = x[0], loss_target[0]
    g_final = final_norm_g.reshape(1, D_MODEL)

    def own_slot(gathered, own):
        return lax.dynamic_update_slice(gathered, own, (chip, 0, 0))

    def reduce_to_shards(names, slabs, exchange_in):
        from_sibling = _sibling_swap_halves("grad_swap_" + names[0], slabs)
        partials = [_add_sibling("grad_add_" + n, s, r, core_arr) for n, s, r in zip(names, slabs, from_sibling)]
        received, rest = exchange_in(partials)
        halves = [_sum_chips("grad_sum_" + n, p, r, chip_core) for n, p, r in zip(names, partials, received)]
        return halves, rest

    w_in_t, m_w_in_t, v_w_in_t = (jnp.swapaxes(a, 1, 2) for a in (w_in, m_w_in, v_w_in))
    w_in_tb = w_in_t.astype(BF16)
    (g_in,) = _weight_allgather([w_in_tb], [])
    w_in_full_t = own_slot(g_in, w_in_tb).reshape(D_IN, D_MODEL)
    later_w = [w_proj_a, w_proj_b, w_out, w_up, w_down]
    later_b = [w.astype(BF16) for w in later_w]
    small_sharded = [b_gate, conv_a_w, ffn_conv_w]
    fwd = _inproj_fwd(xs, norm_mix_g, w_in_full_t, later_b, small_sharded)
    h1, abcv, gates, qkv0, qkv1, qkv2, h1_streams4, h1_streams16 = fwd[:8]
    gathered_big, gathered_small = fwd[8:13], fwd[13:16]
    qkvs = (qkv0, qkv1, qkv2)
    attn0 = _attn_fwd(qkv0, 0, forward=gathered_big)
    attn = [attn0[:2], _attn_fwd(qkv1, 1), _attn_fwd(qkv2, 2)]
    g_pa, g_pb, g_out, g_up, g_down = [own_slot(g, own) for g, own in zip(attn0[2:], later_b)]
    g_bgate, g_convw, g_ffnw = [own_slot(g, own) for g, own in zip(gathered_small, small_sharded)]
    w_pa_full, w_pb_full, w_up_full = _gathered_columns(g_pa), _gathered_columns(g_pb), _gathered_columns(g_up)
    w_out_full, w_down_full = g_out.reshape(D_MODEL, D_MODEL), g_down.reshape(D_FF, D_MODEL)
    b_gate_full, conv_w_full, ffn_w_full = (_gathered_columns(g) for g in (g_bgate, g_convw, g_ffnw))

    x1, ya0, yb0, mrg, ya, yb, lsetot = _mix_fwd(
        xs, abcv, gates, [a[0] for a in attn], [a[1] for a in attn], conv_w_full, conv_a_b, b_gate_full,
        w_pa_full, w_pb_full, w_out_full)
    h2, up0, up = _ffn_up_fwd(x1, norm_ffn_g, w_up_full, ffn_w_full, ffn_conv_b)
    act, dx2, dx2b, d_g_final, loss = _ffn_act_fwd(x1, up, target, w_down_full, g_final)

    d_up = _ffn_act_bwd(dx2b, up, w_down_full)
    d_w_down = _matmul_tn("dw_down", act, dx2b, col_tile=512)
    dx1, dx1b, d_g_ffn, d_up0, ffn_small = _ffn_up_bwd(d_up, up0, w_up_full, ffn_w_full, x1, norm_ffn_g, dx2)
    d_w_up = _matmul_tn("dw_up", h2, d_up0, col_tile=2 * D_FF // N_CHIPS, slabs=N_CHIPS)

    def behind_mix_bwd(partials):
        res = _mix_bwd(dx1, abcv, gates, ya, yb, yb0, lsetot, conv_w_full, conv_a_b, b_gate_full,
                       w_pa_full, w_pb_full, w_out_full, exchange=partials)
        return res[13:], res[:13]

    halves_ffn, mix_res = reduce_to_shards(
        ("w_up", "w_down"), [d_w_up, d_w_down.reshape(N_CHIPS, D_FF // N_CHIPS, D_MODEL)], behind_mix_bwd)
    (d_ya, d_yb, d_gates, d_abcv, d_yb0, dyl0, dyl1, dyl2, aux0, aux1, aux2, gate_small, conv_small) = mix_res
    d_w_out = _matmul_tn("dw_out", mrg, dx1b)
    d_w_pa = _matmul_tn("dw_proj_a", ya0, d_ya, slabs=N_CHIPS)
    d_w_pb = _matmul_tn("dw_proj_b", yb0, d_yb, slabs=N_CHIPS)

    def behind_attn_bwd(partials):
        res = _attn_bwd(qkv0, dyl0, aux0, 0, exchange=partials)
        return res[1:], res[0]

    halves_mix, d_qkv0 = reduce_to_shards(
        ("w_proj_a", "w_proj_b", "w_out"),
        [d_w_pa, d_w_pb, d_w_out.reshape(N_CHIPS, D_MODEL // N_CHIPS, D_MODEL)], behind_attn_bwd)
    (d_qkv1,), (d_qkv2,) = _attn_bwd(qkv1, dyl1, aux1, 1), _attn_bwd(qkv2, dyl2, aux2, 2)

    dq = [d_qkv0, d_qkv1, d_qkv2]
    seq = xs.shape[0]
    d_w_abcv = _matmul_tn("dw_in_abcv", d_abcv, h1)
    d_w_gates = _matmul_tn("dw_in_gates", d_gates, h1)
    d_w_groups = [_matmul_tn(f"dw_in_qkv{g}", t.reshape(seq, 3 * GROUP_WIDTH), h.reshape(seq, D_MODEL))
                  for g, (t, h) in enumerate(zip(dq, (h1, h1_streams4, h1_streams16)))]
    gw = GROUP_WIDTH
    d_w_in_t = jnp.concatenate(
        [d_w_abcv] + [d_w_groups[g][j * gw:(j + 1) * gw] for j in range(3) for g in range(3)] + [d_w_gates], axis=0)

    slab_in = d_w_in_t.reshape(N_CHIPS, D_IN // N_CHIPS, D_MODEL)
    (from_sibling_in,) = _sibling_swap_halves("grad_swap_w_in", [slab_in])
    partial_in = _add_sibling("grad_add_w_in", slab_in, from_sibling_in, core_arr)
    send_sems, recv_sems, partial_thru, landing_thru, token = _chip_exchange_start(partial_in)
    g_mix_after_start = norm_mix_g + token[0:1, 0:1]
    grad_x, d_g_mix = _inproj_bwd(d_abcv, d_gates, dq, w_in_full_t, xs, g_mix_after_start, dx1)
    partial_in, received_in = _chip_exchange_wait(send_sems, recv_sems, partial_thru, landing_thru, d_g_mix)
    halves_in = [_sum_chips("grad_sum_w_in", partial_in, received_in, chip_core)]

    big_names = ("w_in", "w_proj_a", "w_proj_b", "w_out", "w_up", "w_down")
    big_grads = _sibling_share(halves_in + halves_mix + halves_ffn)
    big_w = dict(w_in=w_in_t, w_proj_a=w_proj_a, w_proj_b=w_proj_b, w_out=w_out, w_up=w_up, w_down=w_down)
    big_m = dict(w_in=m_w_in_t, w_proj_a=m_w_proj_a, w_proj_b=m_w_proj_b, w_out=m_w_out, w_up=m_w_up, w_down=m_w_down)
    big_v = dict(w_in=v_w_in_t, w_proj_a=v_w_proj_a, w_proj_b=v_w_proj_b, w_out=v_w_out, w_up=v_w_up, w_down=v_w_down)

    fin_w, fin_m, fin_v = (a.reshape(1, D_MODEL) for a in (final_norm_g, m_final_norm_g, v_final_norm_g))
    small_w = [norm_mix_g, b_gate, conv_a_w, conv_a_b, norm_ffn_g, ffn_conv_w, ffn_conv_b, fin_w]
    small_m = [m_norm_mix_g, m_b_gate, m_conv_a_w, m_conv_a_b, m_norm_ffn_g, m_ffn_conv_w, m_ffn_conv_b, fin_m]
    small_v = [v_norm_mix_g, v_b_gate, v_conv_a_w, v_conv_a_b, v_norm_ffn_g, v_ffn_conv_w, v_ffn_conv_b, fin_v]
    small_out = _small_update([d_g_mix, d_g_ffn, d_g_final, gate_small, conv_small, ffn_small, loss],
                              small_w, small_m, small_v)
    total_loss = small_out[0][0, 0]

    grads, delta, new_m, new_v = {}, {}, {}, {}
    for i, n in enumerate(SMALL_PARAMS):
        vals = small_out[1 + 4 * i:5 + 4 * i]
        if n == "final_norm_g":
            vals = [a.reshape(D_MODEL) for a in vals]
        grads[n], delta[n], new_m[n], new_v[n] = vals
    for n, g in zip(big_names, big_grads):
        vals = (g,) + tuple(_adamw("adamw_" + n, big_w[n], g, big_m[n], big_v[n]))
        if n == "w_in":
            vals = [jnp.swapaxes(a, 1, 2) for a in vals]
        grads[n], delta[n], new_m[n], new_v[n] = vals

    names = ["norm_mix_g", "w_in", "b_gate", "conv_a_w", "conv_a_b", "w_proj_a", "w_proj_b", "w_out", "norm_ffn_g", "w_up",
             "ffn_conv_w", "ffn_conv_b", "w_down", "final_norm_g"]
    out = [total_loss, grad_x[None]]
    for group in (grads, delta, new_m, new_v):
        out += [group[n] for n in names]
    return tuple(out)
```

```python
import jax
import jax.numpy as jnp
from jax import lax
from jax.experimental import pallas as pl
from jax.experimental.pallas import tpu as pltpu

F32 = jnp.float32
BF16 = jnp.bfloat16

D_MODEL = 1024
CONV_WIDTH = 512
ATTN_WIDTH = 768
GROUP_WIDTH = 256
HEAD_DIM = 64
HEADS_PER_GROUP = 4
DILATIONS = (1, 4, 16)
ATTN_BLOCK = 128
D_FF = 2816
D_IN = 5888
EPS = 1e-6
NEG_INF = -1e30
ATTN_SCALE = HEAD_DIM ** -0.5

COL_ABCV = 0
COL_Q = 1536
COL_K = 2304
COL_V = 3072
COL_GATES = 3840

ADAM_LR = 0.001
ADAM_B1 = 0.9
ADAM_B2 = 0.999
ADAM_EPS = 1e-08
ADAM_WD = 0.01
ADAM_STEP = 10

LANES = 128
SUBLANES = 8
ROW_TILE = 512
VMEM_LIMIT = 56 * 1024 * 1024

_NT = (((1,), (1,)), ((), ()))
_TN = (((0,), (0,)), ((), ()))


def _params(n_axes, vmem=VMEM_LIMIT):
    return pltpu.CompilerParams(dimension_semantics=("arbitrary",) * n_axes, vmem_limit_bytes=vmem)


def _resident(shape):
    nd = len(shape)
    return pl.BlockSpec(shape, lambda *_: (0,) * nd, pipeline_mode=pl.Buffered(1))


def _rows(tm, width, col_block=0):
    return pl.BlockSpec((tm, width), lambda i: (i, col_block))


def _col_chunks(n, cmax):
    out, lo = [], 0
    while lo < n:
        size = min(cmax, n - lo)
        out.append((lo, size))
        lo += size
    return out


def _dot(a, b):
    return jnp.dot(a, b, preferred_element_type=F32)


def _dot_nt(a, b):
    return lax.dot_general(a, b, _NT, preferred_element_type=F32)


def _dot_tn(a, b):
    return lax.dot_general(a, b, _TN, preferred_element_type=F32)


def _sigmoid(x):
    return 0.5 * jnp.tanh(0.5 * x) + 0.5


def _silu(x):
    hx = 0.5 * x
    return hx + hx * jnp.tanh(hx)


def _shift_down(v, k, halo8):
    tm = v.shape[0]
    rolled = pltpu.roll(v, k, 0)
    fix = jnp.tile(pltpu.roll(halo8, k, 0), (tm // SUBLANES, 1))
    row = lax.broadcasted_iota(jnp.int32, v.shape, 0)
    return jnp.where(row < k, fix, rolled)


def _shift_up(v, k, halo8):
    tm = v.shape[0]
    rolled = pltpu.roll(v, tm - k, 0)
    fix = jnp.tile(pltpu.roll(halo8, SUBLANES - k, 0), (tm // SUBLANES, 1))
    row = lax.broadcasted_iota(jnp.int32, v.shape, 0)
    return jnp.where(row >= tm - k, fix, rolled)


def _colsum(v):
    return jnp.sum(v, axis=0, keepdims=True)


def _to_streams(val, scr, out_ref, d, col0):
    tm = val.shape[0]
    panels = val.shape[1] // LANES
    if d == 1:
        out_ref[0, :, col0:col0 + val.shape[1]] = val.astype(out_ref.dtype)
        return
    for p in range(panels):
        scr[pl.ds(p * tm, tm), :] = val[:, p * LANES:(p + 1) * LANES]
    for r in range(d):
        for p in range(panels):
            piece = scr[pl.ds(p * tm + r, tm // d, stride=d), :]
            out_ref[r, :, col0 + p * LANES: col0 + (p + 1) * LANES] = piece.astype(out_ref.dtype)


def _from_streams(in_ref, scr, d, col0, width):
    panels = width // LANES
    rows = in_ref.shape[1]
    tm = rows * d
    if d == 1:
        return in_ref[0, :, col0:col0 + width].astype(F32)
    for r in range(d):
        for p in range(panels):
            scr[pl.ds(p * tm + r, rows, stride=d), :] = in_ref[r, :, col0 + p * LANES: col0 + (p + 1) * LANES].astype(F32)
    return jnp.concatenate([scr[pl.ds(p * tm, tm), :] for p in range(panels)], axis=1)


def _stream_block(tm, d, width):
    return pl.BlockSpec((d, tm // d, width), lambda i: (0, i, 0))


def _rev_stream_block(tm, d, width, nt):
    return pl.BlockSpec((d, tm // d, width), lambda i: (0, nt - 1 - i, 0))


N_CHIPS = 4
MESH_ID = pl.DeviceIdType.MESH
_ANY = pl.BlockSpec(memory_space=pl.ANY)
_VMEM = pl.BlockSpec(memory_space=pltpu.VMEM)


def _mesh_position():
    x, y, c = lax.axis_index("x"), lax.axis_index("y"), lax.axis_index("c")
    other_chips = [(1 - x, y), (x, 1 - y), (1 - x, 1 - y)]
    return x, y, c, other_chips


def _half_rows(c, half):
    return pl.ds(pl.multiple_of(c * half, 16), half)


def _remote_copy(k, src, dst, to, send_sems, recv_sems):
    return pltpu.make_async_remote_copy(src_ref=src, dst_ref=dst, send_sem=send_sems.at[k], recv_sem=recv_sems.at[k],
                                        device_id=to, device_id_type=MESH_ID)


def _gather_first_copies(big_refs, small_refs, big_outs, small_outs, send_sems, recv_sems):
    x, y, c, chips = _mesh_position()
    me = 2 * x + y
    nb = len(big_refs)
    cps = []
    for j, (px, py) in enumerate(chips):
        for b in range(nb):
            mine = _half_rows(c, big_refs[b].shape[1] // 2)
            cps.append(_remote_copy(3 * b + j, big_refs[b].at[0, mine], big_outs[b].at[me, mine], (px, py, c),
                                    send_sems, recv_sems))
        for s in range(len(small_refs)):
            cps.append(_remote_copy(3 * (nb + s) + j, small_refs[s].at[0], small_outs[s].at[me], (px, py, c),
                                    send_sems, recv_sems))
    return cps


def _gather_forward_copies(bufs, send_sems, recv_sems):
    x, y, c, chips = _mesh_position()
    cps = []
    for j, (px, py) in enumerate(chips):
        for b in range(len(bufs)):
            landed = bufs[b].at[2 * px + py, _half_rows(c, bufs[b].shape[1] // 2)]
            cps.append(_remote_copy(3 * b + j, landed, landed, (x, y, 1 - c), send_sems, recv_sems))
    return cps


def _chip_exchange_copies(src_refs, out_refs, send_sems, recv_sems):
    x, y, c, chips = _mesh_position()
    cps = []
    for j, (px, py) in enumerate(chips):
        for a in range(len(src_refs)):
            cps.append(_remote_copy(3 * a + j, src_refs[a].at[2 * px + py], out_refs[a].at[j], (px, py, c),
                                    send_sems, recv_sems))
    return cps


def _dma_sems(n):
    return [pltpu.SemaphoreType.DMA((n,)), pltpu.SemaphoreType.DMA((n,))]


def _norm_fwd(x, g):
    s = x.shape[0]
    tm = ROW_TILE

    def body(x_ref, g_ref, h_ref, hs1_ref, hs2_ref, scr):
        xv = x_ref[...]
        r = lax.rsqrt(jnp.mean(xv * xv, axis=-1, keepdims=True) + EPS)
        hf = xv * r * g_ref[...]
        h_ref[...] = hf.astype(BF16)
        for d, hs_ref in zip(DILATIONS[1:], (hs1_ref, hs2_ref)):
            for lo, size in _col_chunks(D_MODEL, GROUP_WIDTH):
                _to_streams(hf[:, lo:lo + size], scr, hs_ref, d, lo)

    return pl.pallas_call(
        body, name="norm_fwd", grid=(s // tm,),
        out_shape=[jax.ShapeDtypeStruct((s, D_MODEL), BF16)]
        + [jax.ShapeDtypeStruct((d, s // d, D_MODEL), BF16) for d in DILATIONS[1:]],
        in_specs=[_rows(tm, D_MODEL), _resident((1, D_MODEL))],
        out_specs=[_rows(tm, D_MODEL)] + [_stream_block(tm, d, D_MODEL) for d in DILATIONS[1:]],
        scratch_shapes=[pltpu.VMEM((GROUP_WIDTH // LANES * tm, LANES), F32)],
        compiler_params=_params(1))(x, g)


def _inproj_fwd(h1, w_in_t, big_shards, small_shards):
    s = h1.shape[0]
    tm = ROW_TILE
    nt = s // tm
    nb, ns = len(big_shards), len(small_shards)
    n_fixed_in, n_fixed_out = 2, 5

    def body(*refs):
        h_ref, w_ref = refs[:n_fixed_in]
        shard_refs = refs[n_fixed_in:n_fixed_in + nb + ns]
        pos = n_fixed_in + nb + ns
        abcv_ref, gates_ref, qkv0_ref, qkv1_ref, qkv2_ref = refs[pos:pos + n_fixed_out]
        gathered_refs = refs[pos + n_fixed_out:pos + n_fixed_out + nb + ns]
        scr, send_sems, recv_sems = refs[pos + n_fixed_out + nb + ns:]
        i = pl.program_id(0)

        def gather_copies():
            return _gather_first_copies(shard_refs[:nb], shard_refs[nb:], gathered_refs[:nb], gathered_refs[nb:],
                                        send_sems, recv_sems)

        @pl.when(i == 0)
        def _():
            for cp in gather_copies():
                cp.start()

        h = h_ref[...]
        for lo, size in _col_chunks(3 * CONV_WIDTH, 512):
            abcv_ref[:, lo:lo + size] = _dot_nt(h, w_ref[COL_ABCV + lo: COL_ABCV + lo + size, :]).astype(BF16)
        for lo, size in _col_chunks(2 * D_MODEL, 512):
            gates_ref[:, lo:lo + size] = _dot_nt(h, w_ref[COL_GATES + lo: COL_GATES + lo + size, :]).astype(BF16)
        for gi, (d, out_ref) in enumerate(zip(DILATIONS, (qkv0_ref, qkv1_ref, qkv2_ref))):
            for j, base in enumerate((COL_Q, COL_K, COL_V)):
                lo = base + gi * GROUP_WIDTH
                y = _dot_nt(h, w_ref[lo:lo + GROUP_WIDTH, :])
                _to_streams(y, scr, out_ref, d, j * GROUP_WIDTH)

        @pl.when(i == nt - 1)
        def _():
            for cp in gather_copies():
                cp.wait()

    outs = [jax.ShapeDtypeStruct((s, 3 * CONV_WIDTH), BF16), jax.ShapeDtypeStruct((s, 2 * D_MODEL), BF16)]
    outs += [jax.ShapeDtypeStruct((d, s // d, 3 * GROUP_WIDTH), BF16) for d in DILATIONS]
    outs += [jax.ShapeDtypeStruct((N_CHIPS,) + a.shape[1:], a.dtype) for a in list(big_shards) + list(small_shards)]
    return pl.pallas_call(
        body, name="inproj_fwd", grid=(nt,), out_shape=outs,
        in_specs=[_rows(tm, D_MODEL), _resident((D_IN, D_MODEL))] + [_ANY] * (nb + ns),
        out_specs=[_rows(tm, 3 * CONV_WIDTH), _rows(tm, 2 * D_MODEL)]
        + [_stream_block(tm, d, 3 * GROUP_WIDTH) for d in DILATIONS] + [_ANY] * (nb + ns),
        scratch_shapes=[pltpu.VMEM((GROUP_WIDTH // LANES * tm, LANES), F32)] + _dma_sems(3 * (nb + ns)),
        compiler_params=_params(1))(h1, w_in_t, *big_shards, *small_shards)


def _head_of_lane(shape):
    return lax.broadcasted_iota(jnp.int32, shape, 1) // HEAD_DIM


def _stack_heads(v):
    head = _head_of_lane(v.shape)
    return jnp.concatenate([jnp.where(head == h, v, jnp.zeros_like(v)) for h in range(HEADS_PER_GROUP)], axis=0)


def _unstack_heads(v):
    q = ATTN_BLOCK
    head = _head_of_lane((q, v.shape[1]))
    out = jnp.zeros((q, v.shape[1]), v.dtype)
    for h in range(HEADS_PER_GROUP):
        out = jnp.where(head == h, v[h * q:(h + 1) * q], out)
    return out


def _per_head_rows(col):
    q = ATTN_BLOCK
    head = _head_of_lane((q, GROUP_WIDTH))
    out = jnp.zeros((q, GROUP_WIDTH), col.dtype)
    for h in range(HEADS_PER_GROUP):
        out = jnp.where(head == h, col[h * q:(h + 1) * q], out)
    return out


def _compact_heads(v):
    lane = lax.broadcasted_iota(jnp.int32, (v.shape[0], LANES), 1)
    return jnp.where((lane & 32) == 0, v[:, 0:LANES], v[:, LANES:2 * LANES])


def _compact_head_col(v):
    lane = lax.broadcasted_iota(jnp.int32, v.shape, 1)
    head = ((lane >> 6) & 1) + 2 * ((lane >> 5) & 1)
    cols = [jnp.max(jnp.where(head == h, v, -jnp.inf), axis=1, keepdims=True) for h in range(HEADS_PER_GROUP)]
    return jnp.concatenate(cols, axis=0)


ATTN_BLOCKS_PER_STEP = 4


def _band_bias(first_block):
    rows = HEADS_PER_GROUP * ATTN_BLOCK
    qi = lax.broadcasted_iota(jnp.int32, (rows, 2 * ATTN_BLOCK), 0) % ATTN_BLOCK
    kj = lax.broadcasted_iota(jnp.int32, (rows, 2 * ATTN_BLOCK), 1)
    dist = qi + ATTN_BLOCK - kj
    valid = (dist >= 0) & (dist <= ATTN_BLOCK)
    if first_block:
        valid = valid & (kj >= ATTN_BLOCK)
    return jnp.where(valid, 0.0, NEG_INF).astype(F32)


def _store_band_biases(bias_ref):
    bias_ref[0] = _band_bias(False)
    bias_ref[1] = _band_bias(True)


def _attn_block_specs(g, nb, clamp_last=False):
    q = ATTN_BLOCK
    last = nb // g - 1

    def cur(col, width=GROUP_WIDTH):
        if clamp_last:
            return pl.BlockSpec((None, g * q, width), lambda r, n: (r, jnp.minimum(n, last), col))
        return pl.BlockSpec((None, g * q, width), lambda r, n: (r, n, col))

    def prev(col):
        if clamp_last:
            return pl.BlockSpec((None, q, GROUP_WIDTH), lambda r, n: (r, jnp.clip(n * g - 1, 0, nb - 1), col))
        return pl.BlockSpec((None, q, GROUP_WIDTH), lambda r, n: (r, jnp.maximum(n * g - 1, 0), col))

    return cur, prev


def _attn_fwd(qkv, gi, forward=()):
    d, length, _ = qkv.shape
    nb = length // ATTN_BLOCK
    q = ATTN_BLOCK
    g = min(ATTN_BLOCKS_PER_STEP, nb)
    ns = nb // g
    nf = len(forward)

    def body(*refs):
        q_ref, kp_ref, kc_ref, vp_ref, vc_ref = refs[:5]
        o_ref, lse_ref = refs[5 + nf:7 + nf]
        buf_refs = refs[7 + nf:7 + 2 * nf]
        bias_ref = refs[7 + 2 * nf]
        sems = refs[8 + 2 * nf:]
        n = pl.program_id(1)
        first_step = (pl.program_id(0) == 0) & (n == 0)
        last_step = (pl.program_id(0) == d - 1) & (n == ns - 1)

        @pl.when(first_step)
        def _():
            _store_band_biases(bias_ref)
            for cp in _gather_forward_copies(buf_refs, *sems) if nf else ():
                cp.start()

        kfull = jnp.concatenate([kp_ref[...], kc_ref[...]], axis=0)
        vfull = jnp.concatenate([vp_ref[...], vc_ref[...]], axis=0)
        for j in range(g):
            qs = _stack_heads(q_ref[j * q:(j + 1) * q, :])
            k2 = kfull[j * q:(j + 2) * q]
            v2 = vfull[j * q:(j + 2) * q]
            bias = jnp.where(n == 0, bias_ref[1], bias_ref[0]) if j == 0 else bias_ref[0]
            sc = _dot_nt(qs, k2) * ATTN_SCALE + bias
            m = jnp.max(sc, axis=1, keepdims=True)
            p = jnp.exp(sc - m)
            l = jnp.sum(p, axis=1, keepdims=True)
            of = _dot(p.astype(BF16), v2) / l
            o_ref[j * q:(j + 1) * q, :] = _unstack_heads(of).astype(BF16)
            lse_ref[j * q:(j + 1) * q, :] = _per_head_rows(m + jnp.log(l))

        if nf:
            @pl.when(last_step)
            def _():
                for cp in _gather_forward_copies(buf_refs, *sems):
                    cp.wait()

    cur, prev = _attn_block_specs(g, nb)
    return pl.pallas_call(
        body, name=f"attn_fwd_g{gi}", grid=(d, ns),
        out_shape=[jax.ShapeDtypeStruct((d, length, GROUP_WIDTH), BF16),
                   jax.ShapeDtypeStruct((d, length, GROUP_WIDTH), F32)]
        + [jax.ShapeDtypeStruct(a.shape, a.dtype) for a in forward],
        in_specs=[cur(0), prev(1), cur(1), prev(2), cur(2)] + [_ANY] * nf,
        out_specs=[cur(0), cur(0)] + [_ANY] * nf,
        input_output_aliases={5 + a: 2 + a for a in range(nf)},
        scratch_shapes=[pltpu.VMEM((2, HEADS_PER_GROUP * q, 2 * q), F32)] + (_dma_sems(3 * nf) if nf else []),
        compiler_params=_params(2))(qkv, qkv, qkv, qkv, qkv, *forward)


def _conv_branch(ab, ac, av, halo_u, w, b):
    u = ac * av
    sh1 = _shift_down(u, 1, halo_u)
    sh2 = _shift_down(u, 2, halo_u)
    cv = w[0:1] * sh2 + w[1:2] * sh1 + w[2:3] * u + b
    return ab * cv, cv, u, sh1, sh2


def _mix_fwd(x, abcv, gates, o_list, lse_list, conv_w, conv_b, b_gate, w_pa, w_pb, w_out):
    s = x.shape[0]
    tm = ROW_TILE

    def body(x_ref, abcv_ref, gates_ref, o0_ref, o1_ref, o2_ref, l0_ref, l1_ref, l2_ref,
             cw_ref, cb_ref, bg_ref, wpa_ref, wpb_ref, wout_ref,
             x1_ref, ya0_ref, yb0_ref, mrg_ref, ya_ref, yb_ref, lsetot_ref, halo_ref, scr):
        i = pl.program_id(0)

        @pl.when(i == 0)
        def _():
            halo_ref[...] = jnp.zeros_like(halo_ref)

        ab = abcv_ref[:, 0:CONV_WIDTH].astype(F32)
        ac = abcv_ref[:, CONV_WIDTH:2 * CONV_WIDTH].astype(F32)
        av = abcv_ref[:, 2 * CONV_WIDTH:3 * CONV_WIDTH].astype(F32)
        ya0, _, u, _, _ = _conv_branch(ab, ac, av, halo_ref[...], cw_ref[...], cb_ref[...])
        halo_ref[...] = u[tm - SUBLANES:tm]
        ya0 = ya0.astype(BF16)
        ya0_ref[...] = ya0
        ya = _dot(ya0, wpa_ref[...])

        o_refs, l_refs = (o0_ref, o1_ref, o2_ref), (l0_ref, l1_ref, l2_ref)
        lses = [_from_streams(l_refs[g], scr, DILATIONS[g], 0, GROUP_WIDTH) for g in range(3)]
        top = jnp.maximum(jnp.maximum(lses[0], lses[1]), lses[2])
        lsetot = top + jnp.log(jnp.exp(lses[0] - top) + jnp.exp(lses[1] - top) + jnp.exp(lses[2] - top))
        lsetot_ref[...] = lsetot
        yb = jnp.zeros((tm, D_MODEL), F32)
        for g in range(3):
            og = _from_streams(o_refs[g], scr, DILATIONS[g], 0, GROUP_WIDTH)
            yb0 = (jnp.exp(lses[g] - lsetot) * og).astype(BF16)
            yb0_ref[:, g * GROUP_WIDTH:(g + 1) * GROUP_WIDTH] = yb0
            yb = yb + _dot(yb0, wpb_ref[g * GROUP_WIDTH:(g + 1) * GROUP_WIDTH, :])

        sa = _sigmoid(gates_ref[:, 0:D_MODEL].astype(F32) + bg_ref[0:1, :])
        sb = _sigmoid(gates_ref[:, D_MODEL:2 * D_MODEL].astype(F32) + bg_ref[1:2, :])
        ya_ref[...] = ya.astype(BF16)
        yb_ref[...] = yb.astype(BF16)
        mrg = (sa * ya + sb * yb).astype(BF16)
        mrg_ref[...] = mrg
        x1_ref[...] = x_ref[...] + _dot(mrg, wout_ref[...])

    outs = [jax.ShapeDtypeStruct((s, D_MODEL), F32),
            jax.ShapeDtypeStruct((s, CONV_WIDTH), BF16),
            jax.ShapeDtypeStruct((s, ATTN_WIDTH), BF16),
            jax.ShapeDtypeStruct((s, D_MODEL), BF16),
            jax.ShapeDtypeStruct((s, D_MODEL), BF16),
            jax.ShapeDtypeStruct((s, D_MODEL), BF16),
            jax.ShapeDtypeStruct((s, GROUP_WIDTH), F32)]
    return pl.pallas_call(
        body, name="mix_fwd", grid=(s // tm,), out_shape=outs,
        in_specs=[_rows(tm, D_MODEL), _rows(tm, 3 * CONV_WIDTH), _rows(tm, 2 * D_MODEL)]
        + [_stream_block(tm, d, GROUP_WIDTH) for d in DILATIONS] * 2
        + [_resident((3, CONV_WIDTH)), _resident((1, CONV_WIDTH)), _resident((2, D_MODEL)),
           _resident((CONV_WIDTH, D_MODEL)), _resident((ATTN_WIDTH, D_MODEL)), _resident((D_MODEL, D_MODEL))],
        out_specs=[_rows(tm, D_MODEL), _rows(tm, CONV_WIDTH), _rows(tm, ATTN_WIDTH), _rows(tm, D_MODEL),
                   _rows(tm, D_MODEL), _rows(tm, D_MODEL), _rows(tm, GROUP_WIDTH)],
        scratch_shapes=[pltpu.VMEM((SUBLANES, CONV_WIDTH), F32),
                        pltpu.VMEM((GROUP_WIDTH // LANES * tm, LANES), F32)],
        compiler_params=_params(1))(x, abcv, gates, *o_list, *lse_list, conv_w, conv_b, b_gate, w_pa, w_pb, w_out)


FFN_CHUNK = 512


def _ffn_up_fwd(x1, g, w_up, conv_w, conv_b):
    s = x1.shape[0]
    n = w_up.shape[1]
    tm = ROW_TILE

    def body(x_ref, g_ref, w_ref, cw_ref, cb_ref, h_ref, up0_ref, up_ref, halo_ref):
        @pl.when(pl.program_id(0) == 0)
        def _():
            halo_ref[...] = jnp.zeros_like(halo_ref)

        xv = x_ref[...]
        r = lax.rsqrt(jnp.mean(xv * xv, axis=-1, keepdims=True) + EPS)
        h = (xv * r * g_ref[...]).astype(BF16)
        h_ref[...] = h
        for lo, size in _col_chunks(n, FFN_CHUNK):
            cols = slice(lo, lo + size)
            y = _dot(h, w_ref[:, cols])
            up0_ref[:, cols] = y.astype(BF16)
            halo = halo_ref[:, cols]
            w = cw_ref[:, cols]
            up = w[0:1] * _shift_down(y, 2, halo) + w[1:2] * _shift_down(y, 1, halo) + w[2:3] * y + cb_ref[:, cols]
            up_ref[:, cols] = up.astype(BF16)
            halo_ref[:, cols] = y[tm - SUBLANES:tm]

    return pl.pallas_call(
        body, name="ffn_up_fwd", grid=(s // tm,),
        out_shape=[jax.ShapeDtypeStruct((s, D_MODEL), BF16), jax.ShapeDtypeStruct((s, n), BF16),
                   jax.ShapeDtypeStruct((s, n), BF16)],
        in_specs=[_rows(tm, D_MODEL), _resident((1, D_MODEL)), _resident((D_MODEL, n)), _resident((3, n)),
                  _resident((1, n))],
        out_specs=[_rows(tm, D_MODEL), _rows(tm, n), _rows(tm, n)],
        scratch_shapes=[pltpu.VMEM((SUBLANES, n), F32)],
        compiler_params=_params(1))(x1, g, w_up, conv_w, conv_b)


def _ffn_act_fwd(x1, up, target, w_down, g_final):
    s = x1.shape[0]
    tm = ROW_TILE

    def body(x1_ref, up_ref, tgt_ref, wd_ref, gf_ref, act_ref, dx2_ref, dx2b_ref, dgf_ref, loss_ref):
        @pl.when(pl.program_id(0) == 0)
        def _():
            dgf_ref[...] = jnp.zeros_like(dgf_ref)
            loss_ref[...] = jnp.zeros_like(loss_ref)

        acc = jnp.zeros((tm, D_MODEL), F32)
        for lo, size in _col_chunks(D_FF, FFN_CHUNK):
            gate = up_ref[:, lo:lo + size].astype(F32)
            val = up_ref[:, D_FF + lo:D_FF + lo + size].astype(F32)
            act = (_silu(gate) * val).astype(BF16)
            act_ref[:, lo:lo + size] = act
            acc = acc + _dot(act, wd_ref[lo:lo + size, :])

        x2 = x1_ref[...] + acc
        r = lax.rsqrt(jnp.mean(x2 * x2, axis=-1, keepdims=True) + EPS)
        xn = x2 * r
        gf = gf_ref[...]
        err = xn * gf - tgt_ref[...]
        loss_ref[...] += (0.5 / D_MODEL) * jnp.sum(err * err)
        dy = err * (1.0 / D_MODEL)
        dgf_ref[...] += _colsum(dy * xn)
        dxn = dy * gf
        dx2 = r * (dxn - xn * jnp.mean(dxn * xn, axis=-1, keepdims=True))
        dx2_ref[...] = dx2
        dx2b_ref[...] = dx2.astype(BF16)

    return pl.pallas_call(
        body, name="ffn_act_fwd", grid=(s // tm,),
        out_shape=[jax.ShapeDtypeStruct((s, D_FF), BF16), jax.ShapeDtypeStruct((s, D_MODEL), F32),
                   jax.ShapeDtypeStruct((s, D_MODEL), BF16),
                   jax.ShapeDtypeStruct((1, D_MODEL), F32), jax.ShapeDtypeStruct((1, LANES), F32)],
        in_specs=[_rows(tm, D_MODEL), _rows(tm, 2 * D_FF), _rows(tm, D_MODEL),
                  _resident((D_FF, D_MODEL)), _resident((1, D_MODEL))],
        out_specs=[_rows(tm, D_FF), _rows(tm, D_MODEL), _rows(tm, D_MODEL),
                   pl.BlockSpec((1, D_MODEL), lambda i: (0, 0)), pl.BlockSpec((1, LANES), lambda i: (0, 0))],
        compiler_params=_params(1))(x1, up, target, w_down, g_final)


def _ffn_act_bwd(dx2b, up, w_down):
    s = dx2b.shape[0]
    tm = ROW_TILE

    def body(dx2_ref, up_ref, wd_ref, dup_ref):
        dx2 = dx2_ref[...]
        for lo, size in _col_chunks(D_FF, FFN_CHUNK):
            gate = up_ref[:, lo:lo + size].astype(F32)
            val = up_ref[:, D_FF + lo:D_FF + lo + size].astype(F32)
            dact = _dot_nt(dx2, wd_ref[lo:lo + size, :])
            sg = _sigmoid(gate)
            dup_ref[:, lo:lo + size] = (dact * val * (sg * (1.0 + gate * (1.0 - sg)))).astype(BF16)
            dup_ref[:, D_FF + lo:D_FF + lo + size] = (dact * (gate * sg)).astype(BF16)

    return pl.pallas_call(
        body, name="ffn_act_bwd", grid=(s // tm,),
        out_shape=jax.ShapeDtypeStruct((s, 2 * D_FF), BF16),
        in_specs=[_rows(tm, D_MODEL), _rows(tm, 2 * D_FF), _resident((D_FF, D_MODEL))],
        out_specs=_rows(tm, 2 * D_FF),
        compiler_params=_params(1))(dx2b, up, w_down)


FFN_BWD_ROW_TILE = 256


def _ffn_up_bwd(d_up, up0, w_up, conv_w, x1, g, dres):
    s = x1.shape[0]
    n = w_up.shape[1]
    tm = FFN_BWD_ROW_TILE
    nt = s // tm

    def body(dup_ref, up0_ref, w_ref, cw_ref, x_ref, g_ref, dres_ref,
             dx_ref, dxb_ref, dg_ref, dup0_ref, small_ref, next_ref):
        @pl.when(pl.program_id(0) == 0)
        def _():
            next_ref[...] = jnp.zeros_like(next_ref)
            small_ref[...] = jnp.zeros_like(small_ref)
            dg_ref[...] = jnp.zeros_like(dg_ref)

        dh = jnp.zeros((tm, D_MODEL), F32)
        for lo, size in _col_chunks(n, FFN_CHUNK):
            cols = slice(lo, lo + size)
            dz = dup_ref[:, cols].astype(F32)
            x0 = up0_ref[:, cols].astype(F32)
            nxt = next_ref[:, cols]
            dz1 = _shift_up(dz, 1, nxt)
            dz2 = _shift_up(dz, 2, nxt)
            next_ref[:, cols] = dz[0:SUBLANES]
            small_ref[0:1, cols] += _colsum(dz2 * x0)
            small_ref[1:2, cols] += _colsum(dz1 * x0)
            small_ref[2:3, cols] += _colsum(dz * x0)
            small_ref[3:4, cols] += _colsum(dz)
            w = cw_ref[:, cols]
            dup0 = (w[2:3] * dz + w[1:2] * dz1 + w[0:1] * dz2).astype(BF16)
            dup0_ref[:, cols] = dup0
            dh = dh + _dot_nt(dup0, w_ref[:, cols])
        xv = x_ref[...]
        r = lax.rsqrt(jnp.mean(xv * xv, axis=-1, keepdims=True) + EPS)
        xn = xv * r
        dg_ref[...] += _colsum(dh * xn)
        dxn = dh * g_ref[...]
        dx = dres_ref[...] + r * (dxn - xn * jnp.mean(dxn * xn, axis=-1, keepdims=True))
        dx_ref[...] = dx
        dxb_ref[...] = dx.astype(BF16)

    rows = lambda width: pl.BlockSpec((tm, width), lambda i: (nt - 1 - i, 0))
    return pl.pallas_call(
        body, name="ffn_up_bwd", grid=(nt,),
        out_shape=[jax.ShapeDtypeStruct((s, D_MODEL), F32), jax.ShapeDtypeStruct((s, D_MODEL), BF16),
                   jax.ShapeDtypeStruct((1, D_MODEL), F32), jax.ShapeDtypeStruct((s, n), BF16),
                   jax.ShapeDtypeStruct((SUBLANES, n), F32)],
        in_specs=[rows(n), rows(n), _resident((D_MODEL, n)), _resident((3, n)), rows(D_MODEL),
                  _resident((1, D_MODEL)), rows(D_MODEL)],
        out_specs=[rows(D_MODEL), rows(D_MODEL), pl.BlockSpec((1, D_MODEL), lambda i: (0, 0)), rows(n),
                   pl.BlockSpec((SUBLANES, n), lambda i: (0, 0))],
        scratch_shapes=[pltpu.VMEM((SUBLANES, n), F32)],
        compiler_params=_params(1))(d_up, up0, w_up, conv_w, x1, g, dres)


def _inproj_bwd(d_abcv, d_gates, d_qkvs, w_in_t, x, g, dres):
    s = x.shape[0]
    tm = ROW_TILE
    gw = GROUP_WIDTH

    def body(dabcv_ref, dgates_ref, dq0_ref, dq1_ref, dq2_ref, w_ref, x_ref, g_ref, dres_ref, dx_ref, dg_ref, scr):
        @pl.when(pl.program_id(0) == 0)
        def _():
            dg_ref[...] = jnp.zeros_like(dg_ref)

        dh = jnp.zeros((tm, D_MODEL), F32)
        for src, width, wrow in ((dabcv_ref, 3 * CONV_WIDTH, COL_ABCV), (dgates_ref, 2 * D_MODEL, COL_GATES)):
            for lo, size in _col_chunks(width, 512):
                dh = dh + _dot(src[:, lo:lo + size], w_ref[wrow + lo:wrow + lo + size, :])
        for gi, (d, dq_ref) in enumerate(zip(DILATIONS, (dq0_ref, dq1_ref, dq2_ref))):
            for j, base in enumerate((COL_Q, COL_K, COL_V)):
                dy = _from_streams(dq_ref, scr, d, j * gw, gw).astype(BF16)
                wrow = base + gi * gw
                dh = dh + _dot(dy, w_ref[wrow:wrow + gw, :])
        xv = x_ref[...]
        r = lax.rsqrt(jnp.mean(xv * xv, axis=-1, keepdims=True) + EPS)
        xn = xv * r
        dg_ref[...] += _colsum(dh * xn)
        dxn = dh * g_ref[...]
        dx_ref[...] = dres_ref[...] + r * (dxn - xn * jnp.mean(dxn * xn, axis=-1, keepdims=True))

    return pl.pallas_call(
        body, name="inproj_bwd", grid=(s // tm,),
        out_shape=[jax.ShapeDtypeStruct((s, D_MODEL), F32), jax.ShapeDtypeStruct((1, D_MODEL), F32)],
        in_specs=[_rows(tm, 3 * CONV_WIDTH), _rows(tm, 2 * D_MODEL)]
        + [_stream_block(tm, d, 3 * gw) for d in DILATIONS]
        + [_resident((D_IN, D_MODEL)), _rows(tm, D_MODEL), _resident((1, D_MODEL)), _rows(tm, D_MODEL)],
        out_specs=[_rows(tm, D_MODEL), pl.BlockSpec((1, D_MODEL), lambda i: (0, 0))],
        scratch_shapes=[pltpu.VMEM((gw // LANES * tm, LANES), F32)],
        compiler_params=_params(1))(d_abcv, d_gates, *d_qkvs, w_in_t, x, g, dres)


def _mix_bwd(dx1, abcv, gates, ya, yb, yb0, lsetot, conv_w, conv_b, b_gate, w_pa, w_pb, w_out, exchange=()):
    s = dx1.shape[0]
    tm = ROW_TILE
    nt = s // tm
    hb = tm // (2 * SUBLANES)
    nx = len(exchange)

    def body(*refs):
        (dx1_ref, abcv_ref, pre_ref, gates_ref, ya_ref, yb_ref, yb0_ref, lsetot_ref,
         cw_ref, cb_ref, bg_ref, wpa_ref, wpb_ref, wout_ref) = refs[:14]
        part_refs = refs[14:14 + nx]
        (dya_ref, dyb_ref, dgates_ref, dabcv_ref, dyb0_ref, dyl0_ref, dyl1_ref, dyl2_ref, aux0_ref, aux1_ref,
         aux2_ref, sm_gate_ref, sm_conv_ref) = refs[14 + nx:27 + nx]
        recv_refs = refs[27 + nx:27 + 2 * nx]
        next_ref, scr = refs[27 + 2 * nx:29 + 2 * nx]
        sems = refs[29 + 2 * nx:]
        i = pl.program_id(0)

        @pl.when(i == 0)
        def _():
            next_ref[...] = jnp.zeros_like(next_ref)
            sm_gate_ref[...] = jnp.zeros_like(sm_gate_ref)
            sm_conv_ref[...] = jnp.zeros_like(sm_conv_ref)
            for cp in _chip_exchange_copies(part_refs, recv_refs, *sems) if nx else ():
                cp.start()

        not_first = (i < nt - 1).astype(F32)
        dm = _dot_nt(dx1_ref[...].astype(BF16), wout_ref[...])
        sa = _sigmoid(gates_ref[:, 0:D_MODEL].astype(F32) + bg_ref[0:1, :])
        sb = _sigmoid(gates_ref[:, D_MODEL:2 * D_MODEL].astype(F32) + bg_ref[1:2, :])
        dya = (dm * sa).astype(BF16)
        dyb = (dm * sb).astype(BF16)
        dya_ref[...] = dya
        dyb_ref[...] = dyb
        dga = dm * ya_ref[...].astype(F32) * (sa * (1.0 - sa))
        dgb = dm * yb_ref[...].astype(F32) * (sb * (1.0 - sb))
        dgates_ref[:, 0:D_MODEL] = dga.astype(BF16)
        dgates_ref[:, D_MODEL:2 * D_MODEL] = dgb.astype(BF16)
        sm_gate_ref[0:1, :] += _colsum(dga)
        sm_gate_ref[1:2, :] += _colsum(dgb)

        dya0 = _dot_nt(dya, wpa_ref[...])
        ab = abcv_ref[:, 0:CONV_WIDTH].astype(F32)
        ac = abcv_ref[:, CONV_WIDTH:2 * CONV_WIDTH].astype(F32)
        av = abcv_ref[:, 2 * CONV_WIDTH:3 * CONV_WIDTH].astype(F32)
        pre = pre_ref[...].astype(F32) * not_first
        halo_u = (pre[:, CONV_WIDTH:2 * CONV_WIDTH] * pre[:, 2 * CONV_WIDTH:3 * CONV_WIDTH])[SUBLANES:2 * SUBLANES]
        w = cw_ref[...]
        _, cv, u, sh1, sh2 = _conv_branch(ab, ac, av, halo_u, w, cb_ref[...])
        dcv = dya0 * ab
        sm_conv_ref[0:1, :] += _colsum(dcv * sh2)
        sm_conv_ref[1:2, :] += _colsum(dcv * sh1)
        sm_conv_ref[2:3, :] += _colsum(dcv * u)
        sm_conv_ref[3:4, :] += _colsum(dcv)
        nxt = next_ref[...]
        du = w[2:3] * dcv + w[1:2] * _shift_up(dcv, 1, nxt) + w[0:1] * _shift_up(dcv, 2, nxt)
        next_ref[...] = dcv[0:SUBLANES]
        dabcv_ref[:, 0:CONV_WIDTH] = (dya0 * cv).astype(BF16)
        dabcv_ref[:, CONV_WIDTH:2 * CONV_WIDTH] = (du * av).astype(BF16)
        dabcv_ref[:, 2 * CONV_WIDTH:3 * CONV_WIDTH] = (du * ac).astype(BF16)

        head_r = lax.broadcasted_iota(jnp.int32, (GROUP_WIDTH, GROUP_WIDTH), 0) // HEAD_DIM
        head_c = lax.broadcasted_iota(jnp.int32, (GROUP_WIDTH, GROUP_WIDTH), 1) // HEAD_DIM
        same_head = (head_r == head_c).astype(BF16)
        prod = jnp.zeros((tm, GROUP_WIDTH), F32)
        dyb0s = []
        for g in range(3):
            cols = slice(g * GROUP_WIDTH, (g + 1) * GROUP_WIDTH)
            dyb0 = _dot_nt(dyb, wpb_ref[cols, :])
            dyb0_ref[:, cols] = dyb0.astype(BF16)
            dyb0s.append(dyb0)
            prod = prod + dyb0 * yb0_ref[:, cols].astype(F32)
        hi = prod.astype(BF16)
        mid = (prod - hi.astype(F32)).astype(BF16)
        lo = (prod - hi.astype(F32) - mid.astype(F32)).astype(BF16)
        delta = _dot(hi, same_head) + _dot(mid, same_head) + _dot(lo, same_head)
        lse_c = _compact_heads(lsetot_ref[...])
        delta_c = _compact_heads(delta)
        for g, (dy_ref, aux_ref) in enumerate(zip((dyl0_ref, dyl1_ref, dyl2_ref), (aux0_ref, aux1_ref, aux2_ref))):
            d = DILATIONS[g]
            _to_streams(dyb0s[g], scr, dy_ref, d, 0)
            _to_streams(lse_c, scr, aux_ref, d, 0)
            _to_streams(delta_c, scr, aux_ref, d, LANES)

        if nx:
            @pl.when(i == nt - 1)
            def _():
                for cp in _chip_exchange_copies(part_refs, recv_refs, *sems):
                    cp.wait()

    rev = lambda i: (nt - 1 - i, 0)
    pre = lambda i: (jnp.maximum((nt - 1 - i) * hb - 1, 0), 0)
    rows = lambda width: pl.BlockSpec((tm, width), rev)
    outs = [jax.ShapeDtypeStruct((s, D_MODEL), BF16), jax.ShapeDtypeStruct((s, D_MODEL), BF16),
            jax.ShapeDtypeStruct((s, 2 * D_MODEL), BF16), jax.ShapeDtypeStruct((s, 3 * CONV_WIDTH), BF16),
            jax.ShapeDtypeStruct((s, ATTN_WIDTH), BF16)]
    outs += [jax.ShapeDtypeStruct((d, s // d, GROUP_WIDTH), BF16) for d in DILATIONS]
    outs += [jax.ShapeDtypeStruct((d, s // d, 2 * LANES), F32) for d in DILATIONS]
    outs += [jax.ShapeDtypeStruct((SUBLANES, D_MODEL), F32), jax.ShapeDtypeStruct((SUBLANES, CONV_WIDTH), F32)]
    outs += [jax.ShapeDtypeStruct((3,) + a.shape[1:], a.dtype) for a in exchange]
    return pl.pallas_call(
        body, name="mix_bwd", grid=(nt,), out_shape=outs,
        in_specs=[rows(D_MODEL), rows(3 * CONV_WIDTH), pl.BlockSpec((2 * SUBLANES, 3 * CONV_WIDTH), pre),
                  rows(2 * D_MODEL), rows(D_MODEL), rows(D_MODEL), rows(ATTN_WIDTH), rows(GROUP_WIDTH),
                  _resident((3, CONV_WIDTH)), _resident((1, CONV_WIDTH)), _resident((2, D_MODEL)),
                  _resident((CONV_WIDTH, D_MODEL)), _resident((ATTN_WIDTH, D_MODEL)), _resident((D_MODEL, D_MODEL))]
        + [_ANY] * nx,
        out_specs=[rows(D_MODEL), rows(D_MODEL), rows(2 * D_MODEL), rows(3 * CONV_WIDTH), rows(ATTN_WIDTH)]
        + [_rev_stream_block(tm, d, GROUP_WIDTH, nt) for d in DILATIONS]
        + [_rev_stream_block(tm, d, 2 * LANES, nt) for d in DILATIONS]
        + [pl.BlockSpec((SUBLANES, D_MODEL), lambda i: (0, 0)), pl.BlockSpec((SUBLANES, CONV_WIDTH), lambda i: (0, 0))]
        + [_ANY] * nx,
        scratch_shapes=[pltpu.VMEM((SUBLANES, CONV_WIDTH), F32),
                        pltpu.VMEM((GROUP_WIDTH // LANES * tm, LANES), F32)] + (_dma_sems(3 * nx) if nx else []),
        compiler_params=_params(1))(dx1, abcv, abcv, gates, ya, yb, yb0, lsetot,
                                    conv_w, conv_b, b_gate, w_pa, w_pb, w_out, *exchange)


def _attn_bwd(qkv, dy, aux, gi, exchange=()):
    d, length, _ = qkv.shape
    nb = length // ATTN_BLOCK
    q = ATTN_BLOCK
    gw = GROUP_WIDTH
    g = min(ATTN_BLOCKS_PER_STEP, nb)
    assert g >= 2 and nb % g == 0
    ns = nb // g
    lag = 1 if ns > 1 else 0
    tail = (g - 1) * q
    nx = len(exchange)

    def body(*refs):
        q_ref, kp_ref, kc_ref, vp_ref, vc_ref, dy_ref, aux_ref = refs[:7]
        part_refs = refs[7:7 + nx]
        out_ref = refs[7 + nx]
        recv_refs = refs[8 + nx:8 + 2 * nx]
        dq_ref, dkv_ref, bias_ref = refs[8 + 2 * nx:11 + 2 * nx]
        sems = refs[11 + 2 * nx:]
        n = pl.program_id(1)

        @pl.when((pl.program_id(0) == 0) & (n == 0))
        def _():
            _store_band_biases(bias_ref)
            for cp in _chip_exchange_copies(part_refs, recv_refs, *sems) if nx else ():
                cp.start()

        if nx:
            @pl.when((pl.program_id(0) == d - 1) & (n == ns - 1 + lag))
            def _():
                for cp in _chip_exchange_copies(part_refs, recv_refs, *sems):
                    cp.wait()

        def emit(rows):
            out_ref[rows, gw:2 * gw] = dkv_ref[0, rows].astype(BF16)
            out_ref[rows, 2 * gw:3 * gw] = dkv_ref[1, rows].astype(BF16)

        if lag:
            @pl.when(n > 0)
            def _():
                out_ref[:, 0:gw] = dq_ref[...].astype(BF16)
                emit(slice(0, tail))

            @pl.when(n == ns)
            def _():
                emit(slice(tail, g * q))

        @pl.when(n < ns)
        def _():
            kfull = jnp.concatenate([kp_ref[...], kc_ref[...]], axis=0)
            vfull = jnp.concatenate([vp_ref[...], vc_ref[...]], axis=0)
            for j in range(g):
                rows = slice(j * q, (j + 1) * q)
                qs = _stack_heads(q_ref[rows, :])
                dys = _stack_heads(dy_ref[rows, :])
                k2 = kfull[j * q:(j + 2) * q]
                v2 = vfull[j * q:(j + 2) * q]
                lse = _compact_head_col(aux_ref[rows, 0:LANES])
                delta = _compact_head_col(aux_ref[rows, LANES:2 * LANES])
                bias = jnp.where(n == 0, bias_ref[1], bias_ref[0]) if j == 0 else bias_ref[0]
                p = jnp.exp(_dot_nt(qs, k2) * ATTN_SCALE + bias - lse)
                dp = _dot_nt(dys, v2)
                ds = (p * (dp - delta) * ATTN_SCALE).astype(BF16)
                dq_j = _unstack_heads(_dot(ds, k2))
                dk2 = _dot_tn(ds, qs)
                dv2 = _dot_tn(p.astype(BF16), dys)
                if j == 0:
                    @pl.when(n > 0)
                    def _():
                        out_ref[tail:g * q, gw:2 * gw] = (dkv_ref[0, tail:g * q] + dk2[0:q]).astype(BF16)
                        out_ref[tail:g * q, 2 * gw:3 * gw] = (dkv_ref[1, tail:g * q] + dv2[0:q]).astype(BF16)
                else:
                    dkv_ref[0, (j - 1) * q:j * q] += dk2[0:q]
                    dkv_ref[1, (j - 1) * q:j * q] += dv2[0:q]
                dkv_ref[0, rows] = dk2[q:2 * q]
                dkv_ref[1, rows] = dv2[q:2 * q]
                dq_ref[rows, :] = dq_j
            if not lag:
                out_ref[:, 0:gw] = dq_ref[...].astype(BF16)
                emit(slice(0, g * q))

    cur, prev = _attn_block_specs(g, nb, clamp_last=True)
    return pl.pallas_call(
        body, name=f"attn_bwd_g{gi}", grid=(d, ns + lag),
        out_shape=[jax.ShapeDtypeStruct((d, length, 3 * gw), BF16)]
        + [jax.ShapeDtypeStruct((3,) + a.shape[1:], a.dtype) for a in exchange],
        in_specs=[cur(0), prev(1), cur(1), prev(2), cur(2), cur(0), cur(0, 2 * LANES)] + [_ANY] * nx,
        out_specs=[pl.BlockSpec((None, g * q, 3 * gw), lambda r, n: (r, jnp.maximum(n - lag, 0), 0))] + [_ANY] * nx,
        scratch_shapes=[pltpu.VMEM((g * q, gw), F32), pltpu.VMEM((2, g * q, gw), F32),
                        pltpu.VMEM((2, HEADS_PER_GROUP * q, 2 * q), F32)] + (_dma_sems(3 * nx) if nx else []),
        compiler_params=_params(2))(qkv, qkv, qkv, qkv, qkv, dy, aux, *exchange)


def _matmul_tn(name, a, b, col_tile=1024, row_tile=2048, slabs=0):
    s, k = a.shape
    n = b.shape[1]
    tk = min(row_tile, s)
    tn = col_tile
    steps = s // tk

    def body(a_ref, b_ref, o_ref, acc_ref):
        t = pl.program_id(1)

        @pl.when(t == 0)
        def _():
            acc_ref[...] = jnp.zeros_like(acc_ref)

        acc_ref[...] += _dot_tn(a_ref[...], b_ref[...])

        @pl.when(t == steps - 1)
        def _():
            if slabs:
                for q in range(per_tile):
                    o_ref[q] = acc_ref[:, q * width:(q + 1) * width].astype(BF16)
            else:
                o_ref[...] = acc_ref[...].astype(BF16)

    if slabs:
        width = n // slabs
        per_tile = tn // width
        out_shape = jax.ShapeDtypeStruct((slabs, k, width), BF16)
        out_spec = pl.BlockSpec((per_tile, k, width), lambda j, t: (j, 0, 0))
    else:
        out_shape = jax.ShapeDtypeStruct((k, n), BF16)
        out_spec = pl.BlockSpec((k, tn), lambda j, t: (0, j))
    return pl.pallas_call(
        body, name=name, grid=(n // tn, steps), out_shape=out_shape,
        in_specs=[pl.BlockSpec((tk, k), lambda j, t: (t, 0)), pl.BlockSpec((tk, tn), lambda j, t: (t, j))],
        out_specs=out_spec, scratch_shapes=[pltpu.VMEM((k, tn), F32)],
        compiler_params=_params(2))(a, b)


def _sibling_swap_halves(name, slabs):
    na = len(slabs)

    def body(*refs):
        src_refs, out_refs = refs[:na], refs[na:2 * na]
        send_sems, recv_sems = refs[2 * na:]
        x, y, c, _ = _mesh_position()
        cps = []
        for a in range(na):
            theirs = _half_rows(1 - c, src_refs[a].shape[1] // 2)
            cps.append(pltpu.make_async_remote_copy(
                src_ref=src_refs[a].at[:, theirs, :], dst_ref=out_refs[a], send_sem=send_sems.at[a],
                recv_sem=recv_sems.at[a], device_id=(x, y, 1 - c), device_id_type=MESH_ID))
        for cp in cps:
            cp.start()
        for cp in cps:
            cp.wait()

    return pl.pallas_call(
        body, name=name,
        out_shape=[jax.ShapeDtypeStruct((a.shape[0], a.shape[1] // 2, a.shape[2]), a.dtype) for a in slabs],
        in_specs=[_ANY] * na, out_specs=[_ANY] * na,
        scratch_shapes=[pltpu.SemaphoreType.DMA((na,)), pltpu.SemaphoreType.DMA((na,))])(*slabs)


_HBM = pl.BlockSpec(memory_space=pltpu.HBM)
_SEM = pl.BlockSpec(memory_space=pltpu.SEMAPHORE)
_DATAFLOW = pltpu.SideEffectType.DATAFLOW_SIDE_EFFECTING


def _gather_start(shard):
    gathered = jax.ShapeDtypeStruct((N_CHIPS,) + shard.shape[1:], shard.dtype)

    def body(src_ref, buf_ref, send_sems, recv_sems, src_thru, buf_thru, token):
        for cp in _gather_first_copies([src_ref], [], [buf_ref], [], send_sems, recv_sems):
            cp.start()
        token[...] = jnp.zeros_like(token)

    return pl.pallas_call(
        body, name="gather_start",
        out_shape=(pltpu.SemaphoreType.DMA((3,)), pltpu.SemaphoreType.DMA((3,)),
                   pltpu.HBM(shard.shape, shard.dtype), pltpu.HBM(gathered.shape, gathered.dtype),
                   jax.ShapeDtypeStruct((SUBLANES, LANES), F32)),
        in_specs=(_HBM, _HBM), out_specs=(_SEM, _SEM, _HBM, _HBM, _VMEM), input_output_aliases={0: 2, 1: 3},
        compiler_params=pltpu.CompilerParams(has_side_effects=_DATAFLOW),
    )(pltpu.with_memory_space_constraint(shard, pltpu.HBM),
      pltpu.with_memory_space_constraint(lax.empty(gathered.shape, gathered.dtype), pltpu.HBM))


def _gather_forward(send_sems, recv_sems, shard_thru, buf_thru, after):
    def body(src_ref, buf_ref, send_sems, recv_sems, after_ref, fwd_send, fwd_recv, buf_out):
        for cp in _gather_first_copies([src_ref], [], [buf_ref], [], send_sems, recv_sems):
            cp.wait_send()
            cp.wait_recv()
        for cp in _gather_forward_copies([buf_ref], fwd_send, fwd_recv):
            cp.start()

    return pl.pallas_call(
        body, name="gather_forward",
        out_shape=(pltpu.SemaphoreType.DMA((3,)), pltpu.SemaphoreType.DMA((3,)),
                   pltpu.HBM(buf_thru.shape, buf_thru.dtype)),
        in_specs=(_HBM, _HBM, _SEM, _SEM, _ANY), out_specs=(_SEM, _SEM, _HBM), input_output_aliases={1: 2},
        compiler_params=pltpu.CompilerParams(has_side_effects=_DATAFLOW),
    )(shard_thru, buf_thru, send_sems, recv_sems, after)


def _gather_wait(fwd_send, fwd_recv, buf_thru):
    def body(buf_ref, fwd_send, fwd_recv, buf_out):
        for cp in _gather_forward_copies([buf_ref], fwd_send, fwd_recv):
            cp.wait_send()
            cp.wait_recv()

    return pl.pallas_call(
        body, name="gather_wait", out_shape=pltpu.HBM(buf_thru.shape, buf_thru.dtype),
        in_specs=(_HBM, _SEM, _SEM), out_specs=_HBM, input_output_aliases={0: 0},
        compiler_params=pltpu.CompilerParams(has_side_effects=_DATAFLOW),
    )(buf_thru, fwd_send, fwd_recv)


def _chip_exchange_start(partial):
    _, rows, cols = partial.shape
    landing = jax.ShapeDtypeStruct((3, rows, cols), partial.dtype)

    def body(src_ref, land_ref, send_sems, recv_sems, src_thru, land_thru, token):
        for cp in _chip_exchange_copies([src_ref], [land_ref], send_sems, recv_sems):
            cp.start()
        token[...] = jnp.zeros_like(token)

    return pl.pallas_call(
        body, name="grad_exchange_start",
        out_shape=(pltpu.SemaphoreType.DMA((3,)), pltpu.SemaphoreType.DMA((3,)),
                   pltpu.HBM(partial.shape, partial.dtype), pltpu.HBM(landing.shape, landing.dtype),
                   jax.ShapeDtypeStruct((SUBLANES, LANES), F32)),
        in_specs=(_HBM, _HBM), out_specs=(_SEM, _SEM, _HBM, _HBM, _VMEM), input_output_aliases={0: 2, 1: 3},
        compiler_params=pltpu.CompilerParams(has_side_effects=_DATAFLOW),
    )(pltpu.with_memory_space_constraint(partial, pltpu.HBM),
      pltpu.with_memory_space_constraint(lax.empty(landing.shape, landing.dtype), pltpu.HBM))


def _chip_exchange_wait(send_sems, recv_sems, src_thru, land_thru, after):
    def body(src_ref, land_ref, send_sems, recv_sems, after_ref, src_out, land_out):
        for cp in _chip_exchange_copies([src_ref], [land_ref], send_sems, recv_sems):
            cp.wait_send()
            cp.wait_recv()

    return pl.pallas_call(
        body, name="grad_exchange_wait",
        out_shape=(pltpu.HBM(src_thru.shape, src_thru.dtype), pltpu.HBM(land_thru.shape, land_thru.dtype)),
        in_specs=(_HBM, _HBM, _SEM, _SEM, _ANY), out_specs=(_HBM, _HBM), input_output_aliases={0: 0, 1: 1},
        compiler_params=pltpu.CompilerParams(has_side_effects=_DATAFLOW),
    )(src_thru, land_thru, send_sems, recv_sems, after)


def _sibling_share(halves):
    na = len(halves)

    def body(*refs):
        out_refs = refs[na:2 * na]
        send_sems, recv_sems = refs[2 * na:]
        x, y, c, _ = _mesh_position()
        cps = []
        for a in range(na):
            mine = out_refs[a].at[0, _half_rows(c, out_refs[a].shape[1] // 2)]
            cps.append(pltpu.make_async_remote_copy(src_ref=mine, dst_ref=mine, send_sem=send_sems.at[a],
                                                    recv_sem=recv_sems.at[a], device_id=(x, y, 1 - c),
                                                    device_id_type=MESH_ID))
        for cp in cps:
            cp.start()
        for a, cp in enumerate(cps):
            cp.wait_send()
            theirs = out_refs[a].at[0, _half_rows(1 - c, out_refs[a].shape[1] // 2)]
            pltpu.make_async_remote_copy(src_ref=theirs, dst_ref=theirs, send_sem=send_sems.at[a],
                                         recv_sem=recv_sems.at[a], device_id=(x, y, 1 - c),
                                         device_id_type=MESH_ID).wait_recv()

    return pl.pallas_call(
        body, name="grad_sibling_share", out_shape=[jax.ShapeDtypeStruct(a.shape, a.dtype) for a in halves],
        in_specs=[_ANY] * na, out_specs=[_ANY] * na, input_output_aliases={a: a for a in range(na)},
        scratch_shapes=[pltpu.SemaphoreType.DMA((na,)), pltpu.SemaphoreType.DMA((na,))])(*halves)


def _add_sibling(name, slab, received, core):
    n, rows, cols = slab.shape
    half = rows // 2

    def body(core_ref, a_ref, b_ref, o_ref):
        o_ref[...] = (a_ref[...].astype(F32) + b_ref[...].astype(F32)).astype(BF16)

    grid_spec = pltpu.PrefetchScalarGridSpec(
        num_scalar_prefetch=1, grid=(n,),
        in_specs=[pl.BlockSpec((None, half, cols), lambda s, core_ref: (s, core_ref[0], 0)),
                  pl.BlockSpec((None, half, cols), lambda s, core_ref: (s, 0, 0))],
        out_specs=pl.BlockSpec((None, half, cols), lambda s, core_ref: (s, 0, 0)))
    return pl.pallas_call(body, name=name, grid_spec=grid_spec,
                          out_shape=jax.ShapeDtypeStruct((n, half, cols), BF16),
                          compiler_params=_params(1))(core, slab, received)


def _sum_chips(name, partial, received, chip_core):
    _, half, cols = partial.shape

    def body(cc_ref, own_ref, recv_ref, o_ref):
        acc = own_ref[...].astype(F32)
        for k in range(3):
            acc = acc + recv_ref[k].astype(F32)
        o_ref[...] = acc

    grid_spec = pltpu.PrefetchScalarGridSpec(
        num_scalar_prefetch=1, grid=(1,),
        in_specs=[pl.BlockSpec((None, half, cols), lambda i, cc_ref: (cc_ref[0], 0, 0)),
                  pl.BlockSpec((3, half, cols), lambda i, cc_ref: (0, 0, 0))],
        out_specs=pl.BlockSpec((None, half, cols), lambda i, cc_ref: (0, cc_ref[1], 0)))
    return pl.pallas_call(body, name=name, grid_spec=grid_spec,
                          out_shape=jax.ShapeDtypeStruct((1, 2 * half, cols), F32),
                          compiler_params=_params(1))(chip_core, partial, received)


def _adam_math(w, g, m, v):
    nm = ADAM_B1 * m + (1.0 - ADAM_B1) * g
    nv = ADAM_B2 * v + (1.0 - ADAM_B2) * jnp.square(g)
    m_hat = nm / (1.0 - ADAM_B1 ** ADAM_STEP)
    v_hat = nv / (1.0 - ADAM_B2 ** ADAM_STEP)
    delta = -ADAM_LR * (m_hat / (jnp.sqrt(v_hat) + ADAM_EPS) + ADAM_WD * w)
    return delta, nm, nv


def _adamw(name, w, g, m, v):
    _, rows, cols = w.shape
    tr = next(t for t in (736, 512, 384, 352, 256, 128, 64, 32, 16, 8) if rows % t == 0)

    def body(w_ref, g_ref, m_ref, v_ref, d_ref, nm_ref, nv_ref):
        d_ref[...], nm_ref[...], nv_ref[...] = _adam_math(w_ref[...], g_ref[...], m_ref[...], v_ref[...])

    spec = pl.BlockSpec((None, tr, cols), lambda i: (0, i, 0))
    return pl.pallas_call(
        body, name=name, grid=(rows // tr,), out_shape=[jax.ShapeDtypeStruct(w.shape, F32)] * 3,
        in_specs=[spec] * 4, out_specs=[spec] * 3, compiler_params=_params(1))(w, g, m, v)


SMALL_PARAMS = ("norm_mix_g", "b_gate", "conv_a_w", "conv_a_b", "norm_ffn_g", "ffn_conv_w", "ffn_conv_b", "final_norm_g")


def _small_update(partials, params, moments_m, moments_v):
    na = len(partials)
    npar = len(SMALL_PARAMS)

    def body(*refs):
        in_refs = refs[:na]
        w_refs = refs[na:na + npar]
        m_refs = refs[na + npar:na + 2 * npar]
        v_refs = refs[na + 2 * npar:na + 3 * npar]
        pos = na + 3 * npar
        loss_ref = refs[pos]
        out_refs = refs[pos + 1:pos + 1 + 4 * npar]
        pos += 1 + 4 * npar
        acc_refs = refs[pos:pos + na]
        recv_refs = refs[pos + na:pos + 4 * na]
        send_sems, recv_sems = refs[pos + 4 * na:]
        x, y, c, _ = _mesh_position()
        chip = 2 * x + y
        for a in range(na):
            acc_refs[a][...] = in_refs[a][...]
        for stage, peer in enumerate(((x, y, 1 - c), (x, 1 - y, c), (1 - x, y, c))):
            cps = []
            for a in range(na):
                k = stage * na + a
                cps.append(pltpu.make_async_remote_copy(src_ref=acc_refs[a], dst_ref=recv_refs[k], send_sem=send_sems.at[k],
                                                        recv_sem=recv_sems.at[k], device_id=peer, device_id_type=MESH_ID))
            for cp in cps:
                cp.start()
            for cp in cps:
                cp.wait()
            for a in range(na):
                acc_refs[a][...] = acc_refs[a][...] + recv_refs[stage * na + a][...]

        mix, ffn, fin, gate, conv, ffnc, loss = acc_refs
        loss_ref[...] = loss[...]

        def cols(width):
            return pl.ds(pl.multiple_of(chip * width, LANES), width)

        grads = {
            "norm_mix_g": mix[...], "norm_ffn_g": ffn[...], "final_norm_g": fin[...],
            "b_gate": gate[0:2, cols(D_MODEL // N_CHIPS)],
            "conv_a_w": conv[0:3, cols(CONV_WIDTH // N_CHIPS)], "conv_a_b": conv[3:4, :],
            "ffn_conv_w": ffnc[0:3, cols(2 * D_FF // N_CHIPS)], "ffn_conv_b": ffnc[3:4, :]}
        for i, name in enumerate(SMALL_PARAMS):
            g = grads[name]
            if len(w_refs[i].shape) == 3:
                results = (g,) + _adam_math(w_refs[i][0], g, m_refs[i][0], v_refs[i][0])
                for o_ref, val in zip(out_refs[4 * i:4 * i + 4], results):
                    o_ref[0] = val
            else:
                results = (g,) + _adam_math(w_refs[i][...], g, m_refs[i][...], v_refs[i][...])
                for o_ref, val in zip(out_refs[4 * i:4 * i + 4], results):
                    o_ref[...] = val

    outs = [jax.ShapeDtypeStruct(partials[-1].shape, F32)]
    for w in params:
        outs += [jax.ShapeDtypeStruct(w.shape, F32)] * 4
    scratch = [pltpu.VMEM(p.shape, F32) for p in partials]
    scratch += [pltpu.VMEM(p.shape, F32) for _ in range(3) for p in partials]
    scratch += [pltpu.SemaphoreType.DMA((3 * na,)), pltpu.SemaphoreType.DMA((3 * na,))]
    n_in = na + 3 * npar
    return pl.pallas_call(
        body, name="small_update", out_shape=outs, in_specs=[_VMEM] * n_in, out_specs=[_VMEM] * len(outs),
        scratch_shapes=scratch)(*partials, *params, *moments_m, *moments_v)


def _gathered_columns(g):
    return jnp.transpose(g, (1, 0, 2)).reshape(g.shape[1], N_CHIPS * g.shape[2])


def _column_slabs(full):
    k, n = full.shape
    return jnp.transpose(full.reshape(k, N_CHIPS, n // N_CHIPS), (1, 0, 2))


def kernel(x, norm_mix_g, w_in, b_gate, conv_a_w, conv_a_b, w_proj_a, w_proj_b, w_out, norm_ffn_g, w_up, ffn_conv_w, ffn_conv_b, w_down, final_norm_g, loss_target, m_norm_mix_g, m_w_in, m_b_gate, m_conv_a_w, m_conv_a_b, m_w_proj_a, m_w_proj_b, m_w_out, m_norm_ffn_g, m_w_up, m_ffn_conv_w, m_ffn_conv_b, m_w_down, m_final_norm_g, v_norm_mix_g, v_w_in, v_b_gate, v_conv_a_w, v_conv_a_b, v_w_proj_a, v_w_proj_b, v_w_out, v_norm_ffn_g, v_w_up, v_ffn_conv_w, v_ffn_conv_b, v_w_down, v_final_norm_g):
    chip = (2 * lax.axis_index("x") + lax.axis_index("y")).astype(jnp.int32)
    core = lax.axis_index("c").astype(jnp.int32)
    core_arr = core.reshape(1)
    chip_core = jnp.stack([chip, core])
    xs, target = x[0], loss_target[0]
    g_final = final_norm_g.reshape(1, D_MODEL)

    def own_slot(gathered, own):
        return lax.dynamic_update_slice(gathered, own, (chip, 0, 0))

    def reduce_to_shards(names, slabs, exchange_in):
        from_sibling = _sibling_swap_halves("grad_swap_" + names[0], slabs)
        partials = [_add_sibling("grad_add_" + n, s, r, core_arr) for n, s, r in zip(names, slabs, from_sibling)]
        received, rest = exchange_in(partials)
        halves = [_sum_chips("grad_sum_" + n, p, r, chip_core) for n, p, r in zip(names, partials, received)]
        return halves, rest

    w_in_t, m_w_in_t, v_w_in_t = (jnp.swapaxes(a, 1, 2) for a in (w_in, m_w_in, v_w_in))
    w_in_tb = w_in_t.astype(BF16)
    send1, recv1, shard_thru, g_in, token = _gather_start(w_in_tb)
    h1, h1_streams4, h1_streams16 = _norm_fwd(xs, norm_mix_g + token[0:1, 0:1])
    send2, recv2, g_in = _gather_forward(send1, recv1, shard_thru, g_in, h1)
    g_in = _gather_wait(send2, recv2, g_in)
    w_in_full_t = own_slot(g_in, w_in_tb).reshape(D_IN, D_MODEL)
    later_w = [w_proj_a, w_proj_b, w_out, w_up, w_down]
    later_b = [w.astype(BF16) for w in later_w]
    small_sharded = [b_gate, conv_a_w, ffn_conv_w]
    fwd = _inproj_fwd(h1, w_in_full_t, later_b, small_sharded)
    abcv, gates, qkv0, qkv1, qkv2 = fwd[:5]
    gathered_big, gathered_small = fwd[5:10], fwd[10:13]
    qkvs = (qkv0, qkv1, qkv2)
    attn0 = _attn_fwd(qkv0, 0, forward=gathered_big)
    attn = [attn0[:2], _attn_fwd(qkv1, 1), _attn_fwd(qkv2, 2)]
    g_pa, g_pb, g_out, g_up, g_down = [own_slot(g, own) for g, own in zip(attn0[2:], later_b)]
    g_bgate, g_convw, g_ffnw = [own_slot(g, own) for g, own in zip(gathered_small, small_sharded)]
    w_pa_full, w_pb_full, w_up_full = _gathered_columns(g_pa), _gathered_columns(g_pb), _gathered_columns(g_up)
    w_out_full, w_down_full = g_out.reshape(D_MODEL, D_MODEL), g_down.reshape(D_FF, D_MODEL)
    b_gate_full, conv_w_full, ffn_w_full = (_gathered_columns(g) for g in (g_bgate, g_convw, g_ffnw))

    x1, ya0, yb0, mrg, ya, yb, lsetot = _mix_fwd(
        xs, abcv, gates, [a[0] for a in attn], [a[1] for a in attn], conv_w_full, conv_a_b, b_gate_full,
        w_pa_full, w_pb_full, w_out_full)
    h2, up0, up = _ffn_up_fwd(x1, norm_ffn_g, w_up_full, ffn_w_full, ffn_conv_b)
    act, dx2, dx2b, d_g_final, loss = _ffn_act_fwd(x1, up, target, w_down_full, g_final)

    d_up = _ffn_act_bwd(dx2b, up, w_down_full)
    d_w_down = _matmul_tn("dw_down", act, dx2b, col_tile=512)
    dx1, dx1b, d_g_ffn, d_up0, ffn_small = _ffn_up_bwd(d_up, up0, w_up_full, ffn_w_full, x1, norm_ffn_g, dx2)
    d_w_up = _matmul_tn("dw_up", h2, d_up0, col_tile=2 * D_FF // N_CHIPS, slabs=N_CHIPS)

    def behind_mix_bwd(partials):
        res = _mix_bwd(dx1, abcv, gates, ya, yb, yb0, lsetot, conv_w_full, conv_a_b, b_gate_full,
                       w_pa_full, w_pb_full, w_out_full, exchange=partials)
        return res[13:], res[:13]

    halves_ffn, mix_res = reduce_to_shards(
        ("w_up", "w_down"), [d_w_up, d_w_down.reshape(N_CHIPS, D_FF // N_CHIPS, D_MODEL)], behind_mix_bwd)
    (d_ya, d_yb, d_gates, d_abcv, d_yb0, dyl0, dyl1, dyl2, aux0, aux1, aux2, gate_small, conv_small) = mix_res
    d_w_out = _matmul_tn("dw_out", mrg, dx1b)
    d_w_pa = _matmul_tn("dw_proj_a", ya0, d_ya, slabs=N_CHIPS)
    d_w_pb = _matmul_tn("dw_proj_b", yb0, d_yb, slabs=N_CHIPS)

    def behind_attn_bwd(partials):
        res = _attn_bwd(qkv0, dyl0, aux0, 0, exchange=partials)
        return res[1:], res[0]

    halves_mix, d_qkv0 = reduce_to_shards(
        ("w_proj_a", "w_proj_b", "w_out"),
        [d_w_pa, d_w_pb, d_w_out.reshape(N_CHIPS, D_MODEL // N_CHIPS, D_MODEL)], behind_attn_bwd)
    (d_qkv1,), (d_qkv2,) = _attn_bwd(qkv1, dyl1, aux1, 1), _attn_bwd(qkv2, dyl2, aux2, 2)

    dq = [d_qkv0, d_qkv1, d_qkv2]
    seq = xs.shape[0]
    d_w_abcv = _matmul_tn("dw_in_abcv", d_abcv, h1)
    d_w_gates = _matmul_tn("dw_in_gates", d_gates, h1)
    d_w_groups = [_matmul_tn(f"dw_in_qkv{g}", t.reshape(seq, 3 * GROUP_WIDTH), h.reshape(seq, D_MODEL))
                  for g, (t, h) in enumerate(zip(dq, (h1, h1_streams4, h1_streams16)))]
    gw = GROUP_WIDTH
    d_w_in_t = jnp.concatenate(
        [d_w_abcv] + [d_w_groups[g][j * gw:(j + 1) * gw] for j in range(3) for g in range(3)] + [d_w_gates], axis=0)

    slab_in = d_w_in_t.reshape(N_CHIPS, D_IN // N_CHIPS, D_MODEL)
    (from_sibling_in,) = _sibling_swap_halves("grad_swap_w_in", [slab_in])
    partial_in = _add_sibling("grad_add_w_in", slab_in, from_sibling_in, core_arr)
    send_sems, recv_sems, partial_thru, landing_thru, token = _chip_exchange_start(partial_in)
    g_mix_after_start = norm_mix_g + token[0:1, 0:1]
    grad_x, d_g_mix = _inproj_bwd(d_abcv, d_gates, dq, w_in_full_t, xs, g_mix_after_start, dx1)
    partial_in, received_in = _chip_exchange_wait(send_sems, recv_sems, partial_thru, landing_thru, d_g_mix)
    halves_in = [_sum_chips("grad_sum_w_in", partial_in, received_in, chip_core)]

    big_names = ("w_in", "w_proj_a", "w_proj_b", "w_out", "w_up", "w_down")
    big_grads = _sibling_share(halves_in + halves_mix + halves_ffn)
    big_w = dict(w_in=w_in_t, w_proj_a=w_proj_a, w_proj_b=w_proj_b, w_out=w_out, w_up=w_up, w_down=w_down)
    big_m = dict(w_in=m_w_in_t, w_proj_a=m_w_proj_a, w_proj_b=m_w_proj_b, w_out=m_w_out, w_up=m_w_up, w_down=m_w_down)
    big_v = dict(w_in=v_w_in_t, w_proj_a=v_w_proj_a, w_proj_b=v_w_proj_b, w_out=v_w_out, w_up=v_w_up, w_down=v_w_down)

    fin_w, fin_m, fin_v = (a.reshape(1, D_MODEL) for a in (final_norm_g, m_final_norm_g, v_final_norm_g))
    small_w = [norm_mix_g, b_gate, conv_a_w, conv_a_b, norm_ffn_g, ffn_conv_w, ffn_conv_b, fin_w]
    small_m = [m_norm_mix_g, m_b_gate, m_conv_a_w, m_conv_a_b, m_norm_ffn_g, m_ffn_conv_w, m_ffn_conv_b, fin_m]
    small_v = [v_norm_mix_g, v_b_gate, v_conv_a_w, v_conv_a_b, v_norm_ffn_g, v_ffn_conv_w, v_ffn_conv_b, fin_v]
    small_out = _small_update([d_g_mix, d_g_ffn, d_g_final, gate_small, conv_small, ffn_small, loss],
                              small_w, small_m, small_v)
    total_loss = small_out[0][0, 0]

    grads, delta, new_m, new_v = {}, {}, {}, {}
    for i, n in enumerate(SMALL_PARAMS):
        vals = small_out[1 + 4 * i:5 + 4 * i]
        if n == "final_norm_g":
            vals = [a.reshape(D_MODEL) for a in vals]
        grads[n], delta[n], new_m[n], new_v[n] = vals
    for n, g in zip(big_names, big_grads):
        vals = (g,) + tuple(_adamw("adamw_" + n, big_w[n], g, big_m[n], big_v[n]))
        if n == "w_in":
            vals = [jnp.swapaxes(a, 1, 2) for a in vals]
        grads[n], delta[n], new_m[n], new_v[n] = vals

    names = ["norm_mix_g", "w_in", "b_gate", "conv_a_w", "conv_a_b", "w_proj_a", "w_proj_b", "w_out", "norm_ffn_g", "w_up",
             "ffn_conv_w", "ffn_conv_b", "w_down", "final_norm_g"]
    out = [total_loss, grad_x[None]]
    for group in (grads, delta, new_m, new_v):
        out += [group[n] for n in names]
    return tuple(out)
```

```python
import jax
import jax.numpy as jnp
from jax import lax
from jax.experimental import pallas as pl
from jax.experimental.pallas import tpu as pltpu

F32 = jnp.float32
BF16 = jnp.bfloat16

D_MODEL = 1024
CONV_WIDTH = 512
ATTN_WIDTH = 768
GROUP_WIDTH = 256
HEAD_DIM = 64
HEADS_PER_GROUP = 4
DILATIONS = (1, 4, 16)
ATTN_BLOCK = 128
D_FF = 2816
D_IN = 5888
EPS = 1e-6
NEG_INF = -1e30
ATTN_SCALE = HEAD_DIM ** -0.5

COL_ABCV = 0
COL_Q = 1536
COL_K = 2304
COL_V = 3072
COL_GATES = 3840

ADAM_LR = 0.001
ADAM_B1 = 0.9
ADAM_B2 = 0.999
ADAM_EPS = 1e-08
ADAM_WD = 0.01
ADAM_STEP = 10

LANES = 128
SUBLANES = 8
ROW_TILE = 512
VMEM_LIMIT = 56 * 1024 * 1024

_NT = (((1,), (1,)), ((), ()))
_TN = (((0,), (0,)), ((), ()))


def _params(n_axes, vmem=VMEM_LIMIT):
    return pltpu.CompilerParams(dimension_semantics=("arbitrary",) * n_axes, vmem_limit_bytes=vmem)


def _resident(shape):
    nd = len(shape)
    return pl.BlockSpec(shape, lambda *_: (0,) * nd, pipeline_mode=pl.Buffered(1))


def _rows(tm, width, col_block=0):
    return pl.BlockSpec((tm, width), lambda i: (i, col_block))


def _col_chunks(n, cmax):
    out, lo = [], 0
    while lo < n:
        size = min(cmax, n - lo)
        out.append((lo, size))
        lo += size
    return out


def _dot(a, b):
    return jnp.dot(a, b, preferred_element_type=F32)


def _dot_nt(a, b):
    return lax.dot_general(a, b, _NT, preferred_element_type=F32)


def _dot_tn(a, b):
    return lax.dot_general(a, b, _TN, preferred_element_type=F32)


def _sigmoid(x):
    return 0.5 * jnp.tanh(0.5 * x) + 0.5


def _silu(x):
    hx = 0.5 * x
    return hx + hx * jnp.tanh(hx)


def _shift_down(v, k, halo8):
    tm = v.shape[0]
    rolled = pltpu.roll(v, k, 0)
    fix = jnp.tile(pltpu.roll(halo8, k, 0), (tm // SUBLANES, 1))
    row = lax.broadcasted_iota(jnp.int32, v.shape, 0)
    return jnp.where(row < k, fix, rolled)


def _shift_up(v, k, halo8):
    tm = v.shape[0]
    rolled = pltpu.roll(v, tm - k, 0)
    fix = jnp.tile(pltpu.roll(halo8, SUBLANES - k, 0), (tm // SUBLANES, 1))
    row = lax.broadcasted_iota(jnp.int32, v.shape, 0)
    return jnp.where(row >= tm - k, fix, rolled)


def _colsum(v):
    return jnp.sum(v, axis=0, keepdims=True)


def _to_streams(val, scr, out_ref, d, col0):
    tm = val.shape[0]
    panels = val.shape[1] // LANES
    if d == 1:
        out_ref[0, :, col0:col0 + val.shape[1]] = val.astype(out_ref.dtype)
        return
    for p in range(panels):
        scr[pl.ds(p * tm, tm), :] = val[:, p * LANES:(p + 1) * LANES]
    for r in range(d):
        for p in range(panels):
            piece = scr[pl.ds(p * tm + r, tm // d, stride=d), :]
            out_ref[r, :, col0 + p * LANES: col0 + (p + 1) * LANES] = piece.astype(out_ref.dtype)


def _from_streams(in_ref, scr, d, col0, width):
    panels = width // LANES
    rows = in_ref.shape[1]
    tm = rows * d
    if d == 1:
        return in_ref[0, :, col0:col0 + width].astype(F32)
    for r in range(d):
        for p in range(panels):
            scr[pl.ds(p * tm + r, rows, stride=d), :] = in_ref[r, :, col0 + p * LANES: col0 + (p + 1) * LANES].astype(F32)
    return jnp.concatenate([scr[pl.ds(p * tm, tm), :] for p in range(panels)], axis=1)


def _stream_block(tm, d, width):
    return pl.BlockSpec((d, tm // d, width), lambda i: (0, i, 0))


def _rev_stream_block(tm, d, width, nt):
    return pl.BlockSpec((d, tm // d, width), lambda i: (0, nt - 1 - i, 0))


N_CHIPS = 4
MESH_ID = pl.DeviceIdType.MESH
_ANY = pl.BlockSpec(memory_space=pl.ANY)
_VMEM = pl.BlockSpec(memory_space=pltpu.VMEM)


def _mesh_position():
    x, y, c = lax.axis_index("x"), lax.axis_index("y"), lax.axis_index("c")
    other_chips = [(1 - x, y), (x, 1 - y), (1 - x, 1 - y)]
    return x, y, c, other_chips


def _half_rows(c, half):
    return pl.ds(pl.multiple_of(c * half, 16), half)


def _remote_copy(k, src, dst, to, send_sems, recv_sems):
    return pltpu.make_async_remote_copy(src_ref=src, dst_ref=dst, send_sem=send_sems.at[k], recv_sem=recv_sems.at[k],
                                        device_id=to, device_id_type=MESH_ID)


def _gather_first_copies(big_refs, small_refs, big_outs, small_outs, send_sems, recv_sems):
    x, y, c, chips = _mesh_position()
    me = 2 * x + y
    nb = len(big_refs)
    cps = []
    for j, (px, py) in enumerate(chips):
        for b in range(nb):
            mine = _half_rows(c, big_refs[b].shape[1] // 2)
            cps.append(_remote_copy(3 * b + j, big_refs[b].at[0, mine], big_outs[b].at[me, mine], (px, py, c),
                                    send_sems, recv_sems))
        for s in range(len(small_refs)):
            cps.append(_remote_copy(3 * (nb + s) + j, small_refs[s].at[0], small_outs[s].at[me], (px, py, c),
                                    send_sems, recv_sems))
    return cps


def _gather_forward_copies(bufs, send_sems, recv_sems):
    x, y, c, chips = _mesh_position()
    cps = []
    for j, (px, py) in enumerate(chips):
        for b in range(len(bufs)):
            landed = bufs[b].at[2 * px + py, _half_rows(c, bufs[b].shape[1] // 2)]
            cps.append(_remote_copy(3 * b + j, landed, landed, (x, y, 1 - c), send_sems, recv_sems))
    return cps


def _chip_exchange_copies(src_refs, out_refs, send_sems, recv_sems):
    x, y, c, chips = _mesh_position()
    cps = []
    for j, (px, py) in enumerate(chips):
        for a in range(len(src_refs)):
            cps.append(_remote_copy(3 * a + j, src_refs[a].at[2 * px + py], out_refs[a].at[j], (px, py, c),
                                    send_sems, recv_sems))
    return cps


def _sibling_swap_copies(src_refs, out_refs, send_sems, recv_sems):
    x, y, c, _ = _mesh_position()
    cps = []
    for a in range(len(src_refs)):
        theirs = _half_rows(1 - c, src_refs[a].shape[1] // 2)
        cps.append(_remote_copy(a, src_refs[a].at[:, theirs, :], out_refs[a], (x, y, 1 - c), send_sems, recv_sems))
    return cps


def _swap_shapes(slabs):
    return [jax.ShapeDtypeStruct((a.shape[0], a.shape[1] // 2, a.shape[2]), a.dtype) for a in slabs]


def _dma_sems(n):
    return [pltpu.SemaphoreType.DMA((n,)), pltpu.SemaphoreType.DMA((n,))]


def _norm_fwd(x, g):
    s = x.shape[0]
    tm = ROW_TILE

    def body(x_ref, g_ref, h_ref, hs1_ref, hs2_ref, scr):
        xv = x_ref[...]
        r = lax.rsqrt(jnp.mean(xv * xv, axis=-1, keepdims=True) + EPS)
        hf = xv * r * g_ref[...]
        h_ref[...] = hf.astype(BF16)
        for d, hs_ref in zip(DILATIONS[1:], (hs1_ref, hs2_ref)):
            for lo, size in _col_chunks(D_MODEL, GROUP_WIDTH):
                _to_streams(hf[:, lo:lo + size], scr, hs_ref, d, lo)

    return pl.pallas_call(
        body, name="norm_fwd", grid=(s // tm,),
        out_shape=[jax.ShapeDtypeStruct((s, D_MODEL), BF16)]
        + [jax.ShapeDtypeStruct((d, s // d, D_MODEL), BF16) for d in DILATIONS[1:]],
        in_specs=[_rows(tm, D_MODEL), _resident((1, D_MODEL))],
        out_specs=[_rows(tm, D_MODEL)] + [_stream_block(tm, d, D_MODEL) for d in DILATIONS[1:]],
        scratch_shapes=[pltpu.VMEM((GROUP_WIDTH // LANES * tm, LANES), F32)],
        compiler_params=_params(1))(x, g)


def _inproj_fwd(h1, w_in_t, big_shards, small_shards):
    s = h1.shape[0]
    tm = ROW_TILE
    nt = s // tm
    nb, ns = len(big_shards), len(small_shards)
    n_fixed_in, n_fixed_out = 2, 5

    def body(*refs):
        h_ref, w_ref = refs[:n_fixed_in]
        shard_refs = refs[n_fixed_in:n_fixed_in + nb + ns]
        pos = n_fixed_in + nb + ns
        abcv_ref, gates_ref, qkv0_ref, qkv1_ref, qkv2_ref = refs[pos:pos + n_fixed_out]
        gathered_refs = refs[pos + n_fixed_out:pos + n_fixed_out + nb + ns]
        scr, send_sems, recv_sems = refs[pos + n_fixed_out + nb + ns:]
        i = pl.program_id(0)

        def gather_copies():
            return _gather_first_copies(shard_refs[:nb], shard_refs[nb:], gathered_refs[:nb], gathered_refs[nb:],
                                        send_sems, recv_sems)

        @pl.when(i == 0)
        def _():
            for cp in gather_copies():
                cp.start()

        h = h_ref[...]
        for lo, size in _col_chunks(3 * CONV_WIDTH, 512):
            abcv_ref[:, lo:lo + size] = _dot_nt(h, w_ref[COL_ABCV + lo: COL_ABCV + lo + size, :]).astype(BF16)
        for lo, size in _col_chunks(2 * D_MODEL, 512):
            gates_ref[:, lo:lo + size] = _dot_nt(h, w_ref[COL_GATES + lo: COL_GATES + lo + size, :]).astype(BF16)
        for gi, (d, out_ref) in enumerate(zip(DILATIONS, (qkv0_ref, qkv1_ref, qkv2_ref))):
            for j, base in enumerate((COL_Q, COL_K, COL_V)):
                lo = base + gi * GROUP_WIDTH
                y = _dot_nt(h, w_ref[lo:lo + GROUP_WIDTH, :])
                _to_streams(y, scr, out_ref, d, j * GROUP_WIDTH)

        @pl.when(i == nt - 1)
        def _():
            for cp in gather_copies():
                cp.wait()

    outs = [jax.ShapeDtypeStruct((s, 3 * CONV_WIDTH), BF16), jax.ShapeDtypeStruct((s, 2 * D_MODEL), BF16)]
    outs += [jax.ShapeDtypeStruct((d, s // d, 3 * GROUP_WIDTH), BF16) for d in DILATIONS]
    outs += [jax.ShapeDtypeStruct((N_CHIPS,) + a.shape[1:], a.dtype) for a in list(big_shards) + list(small_shards)]
    return pl.pallas_call(
        body, name="inproj_fwd", grid=(nt,), out_shape=outs,
        in_specs=[_rows(tm, D_MODEL), _resident((D_IN, D_MODEL))] + [_ANY] * (nb + ns),
        out_specs=[_rows(tm, 3 * CONV_WIDTH), _rows(tm, 2 * D_MODEL)]
        + [_stream_block(tm, d, 3 * GROUP_WIDTH) for d in DILATIONS] + [_ANY] * (nb + ns),
        scratch_shapes=[pltpu.VMEM((GROUP_WIDTH // LANES * tm, LANES), F32)] + _dma_sems(3 * (nb + ns)),
        compiler_params=_params(1))(h1, w_in_t, *big_shards, *small_shards)


def _head_of_lane(shape):
    return lax.broadcasted_iota(jnp.int32, shape, 1) // HEAD_DIM


def _stack_heads(v):
    head = _head_of_lane(v.shape)
    return jnp.concatenate([jnp.where(head == h, v, jnp.zeros_like(v)) for h in range(HEADS_PER_GROUP)], axis=0)


def _unstack_heads(v):
    q = ATTN_BLOCK
    head = _head_of_lane((q, v.shape[1]))
    out = jnp.zeros((q, v.shape[1]), v.dtype)
    for h in range(HEADS_PER_GROUP):
        out = jnp.where(head == h, v[h * q:(h + 1) * q], out)
    return out


def _per_head_rows(col):
    q = ATTN_BLOCK
    head = _head_of_lane((q, GROUP_WIDTH))
    out = jnp.zeros((q, GROUP_WIDTH), col.dtype)
    for h in range(HEADS_PER_GROUP):
        out = jnp.where(head == h, col[h * q:(h + 1) * q], out)
    return out


def _compact_heads(v):
    lane = lax.broadcasted_iota(jnp.int32, (v.shape[0], LANES), 1)
    return jnp.where((lane & 32) == 0, v[:, 0:LANES], v[:, LANES:2 * LANES])


def _compact_head_col(v):
    lane = lax.broadcasted_iota(jnp.int32, v.shape, 1)
    head = ((lane >> 6) & 1) + 2 * ((lane >> 5) & 1)
    cols = [jnp.max(jnp.where(head == h, v, -jnp.inf), axis=1, keepdims=True) for h in range(HEADS_PER_GROUP)]
    return jnp.concatenate(cols, axis=0)


ATTN_BLOCKS_PER_STEP = 4


def _band_bias(first_block):
    rows = HEADS_PER_GROUP * ATTN_BLOCK
    qi = lax.broadcasted_iota(jnp.int32, (rows, 2 * ATTN_BLOCK), 0) % ATTN_BLOCK
    kj = lax.broadcasted_iota(jnp.int32, (rows, 2 * ATTN_BLOCK), 1)
    dist = qi + ATTN_BLOCK - kj
    valid = (dist >= 0) & (dist <= ATTN_BLOCK)
    if first_block:
        valid = valid & (kj >= ATTN_BLOCK)
    return jnp.where(valid, 0.0, NEG_INF).astype(F32)


def _store_band_biases(bias_ref):
    bias_ref[0] = _band_bias(False)
    bias_ref[1] = _band_bias(True)


def _attn_block_specs(g, nb, clamp_last=False):
    q = ATTN_BLOCK
    last = nb // g - 1

    def cur(col, width=GROUP_WIDTH):
        if clamp_last:
            return pl.BlockSpec((None, g * q, width), lambda r, n: (r, jnp.minimum(n, last), col))
        return pl.BlockSpec((None, g * q, width), lambda r, n: (r, n, col))

    def prev(col):
        if clamp_last:
            return pl.BlockSpec((None, q, GROUP_WIDTH), lambda r, n: (r, jnp.clip(n * g - 1, 0, nb - 1), col))
        return pl.BlockSpec((None, q, GROUP_WIDTH), lambda r, n: (r, jnp.maximum(n * g - 1, 0), col))

    return cur, prev


def _attn_fwd(qkv, gi, forward=()):
    d, length, _ = qkv.shape
    nb = length // ATTN_BLOCK
    q = ATTN_BLOCK
    g = min(ATTN_BLOCKS_PER_STEP, nb)
    ns = nb // g
    nf = len(forward)

    def body(*refs):
        q_ref, kp_ref, kc_ref, vp_ref, vc_ref = refs[:5]
        o_ref, lse_ref = refs[5 + nf:7 + nf]
        buf_refs = refs[7 + nf:7 + 2 * nf]
        bias_ref = refs[7 + 2 * nf]
        sems = refs[8 + 2 * nf:]
        n = pl.program_id(1)
        first_step = (pl.program_id(0) == 0) & (n == 0)
        last_step = (pl.program_id(0) == d - 1) & (n == ns - 1)

        @pl.when(first_step)
        def _():
            _store_band_biases(bias_ref)
            for cp in _gather_forward_copies(buf_refs, *sems) if nf else ():
                cp.start()

        kfull = jnp.concatenate([kp_ref[...], kc_ref[...]], axis=0)
        vfull = jnp.concatenate([vp_ref[...], vc_ref[...]], axis=0)
        for j in range(g):
            qs = _stack_heads(q_ref[j * q:(j + 1) * q, :])
            k2 = kfull[j * q:(j + 2) * q]
            v2 = vfull[j * q:(j + 2) * q]
            bias = jnp.where(n == 0, bias_ref[1], bias_ref[0]) if j == 0 else bias_ref[0]
            sc = _dot_nt(qs, k2) * ATTN_SCALE + bias
            m = jnp.max(sc, axis=1, keepdims=True)
            p = jnp.exp(sc - m)
            l = jnp.sum(p, axis=1, keepdims=True)
            of = _dot(p.astype(BF16), v2) / l
            o_ref[j * q:(j + 1) * q, :] = _unstack_heads(of).astype(BF16)
            lse_ref[j * q:(j + 1) * q, :] = _per_head_rows(m + jnp.log(l))

        if nf:
            @pl.when(last_step)
            def _():
                for cp in _gather_forward_copies(buf_refs, *sems):
                    cp.wait()

    cur, prev = _attn_block_specs(g, nb)
    return pl.pallas_call(
        body, name=f"attn_fwd_g{gi}", grid=(d, ns),
        out_shape=[jax.ShapeDtypeStruct((d, length, GROUP_WIDTH), BF16),
                   jax.ShapeDtypeStruct((d, length, GROUP_WIDTH), F32)]
        + [jax.ShapeDtypeStruct(a.shape, a.dtype) for a in forward],
        in_specs=[cur(0), prev(1), cur(1), prev(2), cur(2)] + [_ANY] * nf,
        out_specs=[cur(0), cur(0)] + [_ANY] * nf,
        input_output_aliases={5 + a: 2 + a for a in range(nf)},
        scratch_shapes=[pltpu.VMEM((2, HEADS_PER_GROUP * q, 2 * q), F32)] + (_dma_sems(3 * nf) if nf else []),
        compiler_params=_params(2))(qkv, qkv, qkv, qkv, qkv, *forward)


def _conv_branch(ab, ac, av, halo_u, w, b):
    u = ac * av
    sh1 = _shift_down(u, 1, halo_u)
    sh2 = _shift_down(u, 2, halo_u)
    cv = w[0:1] * sh2 + w[1:2] * sh1 + w[2:3] * u + b
    return ab * cv, cv, u, sh1, sh2


def _mix_fwd(x, abcv, gates, o_list, lse_list, conv_w, conv_b, b_gate, w_pa, w_pb, w_out):
    s = x.shape[0]
    tm = ROW_TILE

    def body(x_ref, abcv_ref, gates_ref, o0_ref, o1_ref, o2_ref, l0_ref, l1_ref, l2_ref,
             cw_ref, cb_ref, bg_ref, wpa_ref, wpb_ref, wout_ref,
             x1_ref, ya0_ref, yb0_ref, mrg_ref, ya_ref, yb_ref, lsetot_ref, halo_ref, scr):
        i = pl.program_id(0)

        @pl.when(i == 0)
        def _():
            halo_ref[...] = jnp.zeros_like(halo_ref)

        ab = abcv_ref[:, 0:CONV_WIDTH].astype(F32)
        ac = abcv_ref[:, CONV_WIDTH:2 * CONV_WIDTH].astype(F32)
        av = abcv_ref[:, 2 * CONV_WIDTH:3 * CONV_WIDTH].astype(F32)
        ya0, _, u, _, _ = _conv_branch(ab, ac, av, halo_ref[...], cw_ref[...], cb_ref[...])
        halo_ref[...] = u[tm - SUBLANES:tm]
        ya0 = ya0.astype(BF16)
        ya0_ref[...] = ya0
        ya = _dot(ya0, wpa_ref[...])

        o_refs, l_refs = (o0_ref, o1_ref, o2_ref), (l0_ref, l1_ref, l2_ref)
        lses = [_from_streams(l_refs[g], scr, DILATIONS[g], 0, GROUP_WIDTH) for g in range(3)]
        top = jnp.maximum(jnp.maximum(lses[0], lses[1]), lses[2])
        lsetot = top + jnp.log(jnp.exp(lses[0] - top) + jnp.exp(lses[1] - top) + jnp.exp(lses[2] - top))
        lsetot_ref[...] = lsetot
        yb = jnp.zeros((tm, D_MODEL), F32)
        for g in range(3):
            og = _from_streams(o_refs[g], scr, DILATIONS[g], 0, GROUP_WIDTH)
            yb0 = (jnp.exp(lses[g] - lsetot) * og).astype(BF16)
            yb0_ref[:, g * GROUP_WIDTH:(g + 1) * GROUP_WIDTH] = yb0
            yb = yb + _dot(yb0, wpb_ref[g * GROUP_WIDTH:(g + 1) * GROUP_WIDTH, :])

        sa = _sigmoid(gates_ref[:, 0:D_MODEL].astype(F32) + bg_ref[0:1, :])
        sb = _sigmoid(gates_ref[:, D_MODEL:2 * D_MODEL].astype(F32) + bg_ref[1:2, :])
        ya_ref[...] = ya.astype(BF16)
        yb_ref[...] = yb.astype(BF16)
        mrg = (sa * ya + sb * yb).astype(BF16)
        mrg_ref[...] = mrg
        x1_ref[...] = x_ref[...] + _dot(mrg, wout_ref[...])

    outs = [jax.ShapeDtypeStruct((s, D_MODEL), F32),
            jax.ShapeDtypeStruct((s, CONV_WIDTH), BF16),
            jax.ShapeDtypeStruct((s, ATTN_WIDTH), BF16),
            jax.ShapeDtypeStruct((s, D_MODEL), BF16),
            jax.ShapeDtypeStruct((s, D_MODEL), BF16),
            jax.ShapeDtypeStruct((s, D_MODEL), BF16),
            jax.ShapeDtypeStruct((s, GROUP_WIDTH), F32)]
    return pl.pallas_call(
        body, name="mix_fwd", grid=(s // tm,), out_shape=outs,
        in_specs=[_rows(tm, D_MODEL), _rows(tm, 3 * CONV_WIDTH), _rows(tm, 2 * D_MODEL)]
        + [_stream_block(tm, d, GROUP_WIDTH) for d in DILATIONS] * 2
        + [_resident((3, CONV_WIDTH)), _resident((1, CONV_WIDTH)), _resident((2, D_MODEL)),
           _resident((CONV_WIDTH, D_MODEL)), _resident((ATTN_WIDTH, D_MODEL)), _resident((D_MODEL, D_MODEL))],
        out_specs=[_rows(tm, D_MODEL), _rows(tm, CONV_WIDTH), _rows(tm, ATTN_WIDTH), _rows(tm, D_MODEL),
                   _rows(tm, D_MODEL), _rows(tm, D_MODEL), _rows(tm, GROUP_WIDTH)],
        scratch_shapes=[pltpu.VMEM((SUBLANES, CONV_WIDTH), F32),
                        pltpu.VMEM((GROUP_WIDTH // LANES * tm, LANES), F32)],
        compiler_params=_params(1))(x, abcv, gates, *o_list, *lse_list, conv_w, conv_b, b_gate, w_pa, w_pb, w_out)


FFN_CHUNK = 512


def _ffn_up_fwd(x1, g, w_up, conv_w, conv_b):
    s = x1.shape[0]
    n = w_up.shape[1]
    tm = ROW_TILE

    def body(x_ref, g_ref, w_ref, cw_ref, cb_ref, h_ref, up0_ref, up_ref, halo_ref):
        @pl.when(pl.program_id(0) == 0)
        def _():
            halo_ref[...] = jnp.zeros_like(halo_ref)

        xv = x_ref[...]
        r = lax.rsqrt(jnp.mean(xv * xv, axis=-1, keepdims=True) + EPS)
        h = (xv * r * g_ref[...]).astype(BF16)
        h_ref[...] = h
        for lo, size in _col_chunks(n, FFN_CHUNK):
            cols = slice(lo, lo + size)
            y = _dot(h, w_ref[:, cols])
            up0_ref[:, cols] = y.astype(BF16)
            halo = halo_ref[:, cols]
            w = cw_ref[:, cols]
            up = w[0:1] * _shift_down(y, 2, halo) + w[1:2] * _shift_down(y, 1, halo) + w[2:3] * y + cb_ref[:, cols]
            up_ref[:, cols] = up.astype(BF16)
            halo_ref[:, cols] = y[tm - SUBLANES:tm]

    return pl.pallas_call(
        body, name="ffn_up_fwd", grid=(s // tm,),
        out_shape=[jax.ShapeDtypeStruct((s, D_MODEL), BF16), jax.ShapeDtypeStruct((s, n), BF16),
                   jax.ShapeDtypeStruct((s, n), BF16)],
        in_specs=[_rows(tm, D_MODEL), _resident((1, D_MODEL)), _resident((D_MODEL, n)), _resident((3, n)),
                  _resident((1, n))],
        out_specs=[_rows(tm, D_MODEL), _rows(tm, n), _rows(tm, n)],
        scratch_shapes=[pltpu.VMEM((SUBLANES, n), F32)],
        compiler_params=_params(1))(x1, g, w_up, conv_w, conv_b)


def _ffn_act_fwd(x1, up, target, w_down, g_final):
    s = x1.shape[0]
    tm = ROW_TILE

    def body(x1_ref, up_ref, tgt_ref, wd_ref, gf_ref, act_ref, dx2_ref, dx2b_ref, dgf_ref, loss_ref):
        @pl.when(pl.program_id(0) == 0)
        def _():
            dgf_ref[...] = jnp.zeros_like(dgf_ref)
            loss_ref[...] = jnp.zeros_like(loss_ref)

        acc = jnp.zeros((tm, D_MODEL), F32)
        for lo, size in _col_chunks(D_FF, FFN_CHUNK):
            gate = up_ref[:, lo:lo + size].astype(F32)
            val = up_ref[:, D_FF + lo:D_FF + lo + size].astype(F32)
            act = (_silu(gate) * val).astype(BF16)
            act_ref[:, lo:lo + size] = act
            acc = acc + _dot(act, wd_ref[lo:lo + size, :])

        x2 = x1_ref[...] + acc
        r = lax.rsqrt(jnp.mean(x2 * x2, axis=-1, keepdims=True) + EPS)
        xn = x2 * r
        gf = gf_ref[...]
        err = xn * gf - tgt_ref[...]
        loss_ref[...] += (0.5 / D_MODEL) * jnp.sum(err * err)
        dy = err * (1.0 / D_MODEL)
        dgf_ref[...] += _colsum(dy * xn)
        dxn = dy * gf
        dx2 = r * (dxn - xn * jnp.mean(dxn * xn, axis=-1, keepdims=True))
        dx2_ref[...] = dx2
        dx2b_ref[...] = dx2.astype(BF16)

    return pl.pallas_call(
        body, name="ffn_act_fwd", grid=(s // tm,),
        out_shape=[jax.ShapeDtypeStruct((s, D_FF), BF16), jax.ShapeDtypeStruct((s, D_MODEL), F32),
                   jax.ShapeDtypeStruct((s, D_MODEL), BF16),
                   jax.ShapeDtypeStruct((1, D_MODEL), F32), jax.ShapeDtypeStruct((1, LANES), F32)],
        in_specs=[_rows(tm, D_MODEL), _rows(tm, 2 * D_FF), _rows(tm, D_MODEL),
                  _resident((D_FF, D_MODEL)), _resident((1, D_MODEL))],
        out_specs=[_rows(tm, D_FF), _rows(tm, D_MODEL), _rows(tm, D_MODEL),
                   pl.BlockSpec((1, D_MODEL), lambda i: (0, 0)), pl.BlockSpec((1, LANES), lambda i: (0, 0))],
        compiler_params=_params(1))(x1, up, target, w_down, g_final)


def _ffn_act_bwd(dx2b, up, w_down):
    s = dx2b.shape[0]
    tm = ROW_TILE

    def body(dx2_ref, up_ref, wd_ref, dup_ref):
        dx2 = dx2_ref[...]
        for lo, size in _col_chunks(D_FF, FFN_CHUNK):
            gate = up_ref[:, lo:lo + size].astype(F32)
            val = up_ref[:, D_FF + lo:D_FF + lo + size].astype(F32)
            dact = _dot_nt(dx2, wd_ref[lo:lo + size, :])
            sg = _sigmoid(gate)
            dup_ref[:, lo:lo + size] = (dact * val * (sg * (1.0 + gate * (1.0 - sg)))).astype(BF16)
            dup_ref[:, D_FF + lo:D_FF + lo + size] = (dact * (gate * sg)).astype(BF16)

    return pl.pallas_call(
        body, name="ffn_act_bwd", grid=(s // tm,),
        out_shape=jax.ShapeDtypeStruct((s, 2 * D_FF), BF16),
        in_specs=[_rows(tm, D_MODEL), _rows(tm, 2 * D_FF), _resident((D_FF, D_MODEL))],
        out_specs=_rows(tm, 2 * D_FF),
        compiler_params=_params(1))(dx2b, up, w_down)


FFN_BWD_ROW_TILE = 256


def _ffn_up_bwd(d_up, up0, w_up, conv_w, x1, g, dres, swap=()):
    s = x1.shape[0]
    n = w_up.shape[1]
    tm = FFN_BWD_ROW_TILE
    nt = s // tm
    nw = len(swap)

    def body(*refs):
        dup_ref, up0_ref, w_ref, cw_ref, x_ref, g_ref, dres_ref = refs[:7]
        slab_refs = refs[7:7 + nw]
        dx_ref, dxb_ref, dg_ref, dup0_ref, small_ref = refs[7 + nw:12 + nw]
        swapped_refs = refs[12 + nw:12 + 2 * nw]
        next_ref = refs[12 + 2 * nw]
        sems = refs[13 + 2 * nw:]

        @pl.when(pl.program_id(0) == 0)
        def _():
            next_ref[...] = jnp.zeros_like(next_ref)
            small_ref[...] = jnp.zeros_like(small_ref)
            dg_ref[...] = jnp.zeros_like(dg_ref)
            for cp in _sibling_swap_copies(slab_refs, swapped_refs, *sems) if nw else ():
                cp.start()

        dh = jnp.zeros((tm, D_MODEL), F32)
        for lo, size in _col_chunks(n, FFN_CHUNK):
            cols = slice(lo, lo + size)
            dz = dup_ref[:, cols].astype(F32)
            x0 = up0_ref[:, cols].astype(F32)
            nxt = next_ref[:, cols]
            dz1 = _shift_up(dz, 1, nxt)
            dz2 = _shift_up(dz, 2, nxt)
            next_ref[:, cols] = dz[0:SUBLANES]
            small_ref[0:1, cols] += _colsum(dz2 * x0)
            small_ref[1:2, cols] += _colsum(dz1 * x0)
            small_ref[2:3, cols] += _colsum(dz * x0)
            small_ref[3:4, cols] += _colsum(dz)
            w = cw_ref[:, cols]
            dup0 = (w[2:3] * dz + w[1:2] * dz1 + w[0:1] * dz2).astype(BF16)
            dup0_ref[:, cols] = dup0
            dh = dh + _dot_nt(dup0, w_ref[:, cols])
        xv = x_ref[...]
        r = lax.rsqrt(jnp.mean(xv * xv, axis=-1, keepdims=True) + EPS)
        xn = xv * r
        dg_ref[...] += _colsum(dh * xn)
        dxn = dh * g_ref[...]
        dx = dres_ref[...] + r * (dxn - xn * jnp.mean(dxn * xn, axis=-1, keepdims=True))
        dx_ref[...] = dx
        dxb_ref[...] = dx.astype(BF16)

        if nw:
            @pl.when(pl.program_id(0) == nt - 1)
            def _():
                for cp in _sibling_swap_copies(slab_refs, swapped_refs, *sems):
                    cp.wait()

    rows = lambda width: pl.BlockSpec((tm, width), lambda i: (nt - 1 - i, 0))
    return pl.pallas_call(
        body, name="ffn_up_bwd", grid=(nt,),
        out_shape=[jax.ShapeDtypeStruct((s, D_MODEL), F32), jax.ShapeDtypeStruct((s, D_MODEL), BF16),
                   jax.ShapeDtypeStruct((1, D_MODEL), F32), jax.ShapeDtypeStruct((s, n), BF16),
                   jax.ShapeDtypeStruct((SUBLANES, n), F32)] + _swap_shapes(swap),
        in_specs=[rows(n), rows(n), _resident((D_MODEL, n)), _resident((3, n)), rows(D_MODEL),
                  _resident((1, D_MODEL)), rows(D_MODEL)] + [_ANY] * nw,
        out_specs=[rows(D_MODEL), rows(D_MODEL), pl.BlockSpec((1, D_MODEL), lambda i: (0, 0)), rows(n),
                   pl.BlockSpec((SUBLANES, n), lambda i: (0, 0))] + [_ANY] * nw,
        scratch_shapes=[pltpu.VMEM((SUBLANES, n), F32)] + (_dma_sems(nw) if nw else []),
        compiler_params=_params(1))(d_up, up0, w_up, conv_w, x1, g, dres, *swap)


def _inproj_bwd(d_abcv, d_gates, d_qkvs, w_in_t, x, g, dres):
    s = x.shape[0]
    tm = ROW_TILE
    gw = GROUP_WIDTH

    def body(dabcv_ref, dgates_ref, dq0_ref, dq1_ref, dq2_ref, w_ref, x_ref, g_ref, dres_ref, dx_ref, dg_ref, scr):
        @pl.when(pl.program_id(0) == 0)
        def _():
            dg_ref[...] = jnp.zeros_like(dg_ref)

        dh = jnp.zeros((tm, D_MODEL), F32)
        for src, width, wrow in ((dabcv_ref, 3 * CONV_WIDTH, COL_ABCV), (dgates_ref, 2 * D_MODEL, COL_GATES)):
            for lo, size in _col_chunks(width, 512):
                dh = dh + _dot(src[:, lo:lo + size], w_ref[wrow + lo:wrow + lo + size, :])
        for gi, (d, dq_ref) in enumerate(zip(DILATIONS, (dq0_ref, dq1_ref, dq2_ref))):
            for j, base in enumerate((COL_Q, COL_K, COL_V)):
                dy = _from_streams(dq_ref, scr, d, j * gw, gw).astype(BF16)
                wrow = base + gi * gw
                dh = dh + _dot(dy, w_ref[wrow:wrow + gw, :])
        xv = x_ref[...]
        r = lax.rsqrt(jnp.mean(xv * xv, axis=-1, keepdims=True) + EPS)
        xn = xv * r
        dg_ref[...] += _colsum(dh * xn)
        dxn = dh * g_ref[...]
        dx_ref[...] = dres_ref[...] + r * (dxn - xn * jnp.mean(dxn * xn, axis=-1, keepdims=True))

    return pl.pallas_call(
        body, name="inproj_bwd", grid=(s // tm,),
        out_shape=[jax.ShapeDtypeStruct((s, D_MODEL), F32), jax.ShapeDtypeStruct((1, D_MODEL), F32)],
        in_specs=[_rows(tm, 3 * CONV_WIDTH), _rows(tm, 2 * D_MODEL)]
        + [_stream_block(tm, d, 3 * gw) for d in DILATIONS]
        + [_resident((D_IN, D_MODEL)), _rows(tm, D_MODEL), _resident((1, D_MODEL)), _rows(tm, D_MODEL)],
        out_specs=[_rows(tm, D_MODEL), pl.BlockSpec((1, D_MODEL), lambda i: (0, 0))],
        scratch_shapes=[pltpu.VMEM((gw // LANES * tm, LANES), F32)],
        compiler_params=_params(1))(d_abcv, d_gates, *d_qkvs, w_in_t, x, g, dres)


def _mix_bwd(dx1, abcv, gates, ya, yb, yb0, lsetot, conv_w, conv_b, b_gate, w_pa, w_pb, w_out, exchange=()):
    s = dx1.shape[0]
    tm = ROW_TILE
    nt = s // tm
    hb = tm // (2 * SUBLANES)
    nx = len(exchange)

    def body(*refs):
        (dx1_ref, abcv_ref, pre_ref, gates_ref, ya_ref, yb_ref, yb0_ref, lsetot_ref,
         cw_ref, cb_ref, bg_ref, wpa_ref, wpb_ref, wout_ref) = refs[:14]
        part_refs = refs[14:14 + nx]
        (dya_ref, dyb_ref, dgates_ref, dabcv_ref, dyb0_ref, dyl0_ref, dyl1_ref, dyl2_ref, aux0_ref, aux1_ref,
         aux2_ref, sm_gate_ref, sm_conv_ref) = refs[14 + nx:27 + nx]
        recv_refs = refs[27 + nx:27 + 2 * nx]
        next_ref, scr = refs[27 + 2 * nx:29 + 2 * nx]
        sems = refs[29 + 2 * nx:]
        i = pl.program_id(0)

        @pl.when(i == 0)
        def _():
            next_ref[...] = jnp.zeros_like(next_ref)
            sm_gate_ref[...] = jnp.zeros_like(sm_gate_ref)
            sm_conv_ref[...] = jnp.zeros_like(sm_conv_ref)
            for cp in _chip_exchange_copies(part_refs, recv_refs, *sems) if nx else ():
                cp.start()

        not_first = (i < nt - 1).astype(F32)
        dm = _dot_nt(dx1_ref[...].astype(BF16), wout_ref[...])
        sa = _sigmoid(gates_ref[:, 0:D_MODEL].astype(F32) + bg_ref[0:1, :])
        sb = _sigmoid(gates_ref[:, D_MODEL:2 * D_MODEL].astype(F32) + bg_ref[1:2, :])
        dya = (dm * sa).astype(BF16)
        dyb = (dm * sb).astype(BF16)
        dya_ref[...] = dya
        dyb_ref[...] = dyb
        dga = dm * ya_ref[...].astype(F32) * (sa * (1.0 - sa))
        dgb = dm * yb_ref[...].astype(F32) * (sb * (1.0 - sb))
        dgates_ref[:, 0:D_MODEL] = dga.astype(BF16)
        dgates_ref[:, D_MODEL:2 * D_MODEL] = dgb.astype(BF16)
        sm_gate_ref[0:1, :] += _colsum(dga)
        sm_gate_ref[1:2, :] += _colsum(dgb)

        dya0 = _dot_nt(dya, wpa_ref[...])
        ab = abcv_ref[:, 0:CONV_WIDTH].astype(F32)
        ac = abcv_ref[:, CONV_WIDTH:2 * CONV_WIDTH].astype(F32)
        av = abcv_ref[:, 2 * CONV_WIDTH:3 * CONV_WIDTH].astype(F32)
        pre = pre_ref[...].astype(F32) * not_first
        halo_u = (pre[:, CONV_WIDTH:2 * CONV_WIDTH] * pre[:, 2 * CONV_WIDTH:3 * CONV_WIDTH])[SUBLANES:2 * SUBLANES]
        w = cw_ref[...]
        _, cv, u, sh1, sh2 = _conv_branch(ab, ac, av, halo_u, w, cb_ref[...])
        dcv = dya0 * ab
        sm_conv_ref[0:1, :] += _colsum(dcv * sh2)
        sm_conv_ref[1:2, :] += _colsum(dcv * sh1)
        sm_conv_ref[2:3, :] += _colsum(dcv * u)
        sm_conv_ref[3:4, :] += _colsum(dcv)
        nxt = next_ref[...]
        du = w[2:3] * dcv + w[1:2] * _shift_up(dcv, 1, nxt) + w[0:1] * _shift_up(dcv, 2, nxt)
        next_ref[...] = dcv[0:SUBLANES]
        dabcv_ref[:, 0:CONV_WIDTH] = (dya0 * cv).astype(BF16)
        dabcv_ref[:, CONV_WIDTH:2 * CONV_WIDTH] = (du * av).astype(BF16)
        dabcv_ref[:, 2 * CONV_WIDTH:3 * CONV_WIDTH] = (du * ac).astype(BF16)

        head_r = lax.broadcasted_iota(jnp.int32, (GROUP_WIDTH, GROUP_WIDTH), 0) // HEAD_DIM
        head_c = lax.broadcasted_iota(jnp.int32, (GROUP_WIDTH, GROUP_WIDTH), 1) // HEAD_DIM
        same_head = (head_r == head_c).astype(BF16)
        prod = jnp.zeros((tm, GROUP_WIDTH), F32)
        dyb0s = []
        for g in range(3):
            cols = slice(g * GROUP_WIDTH, (g + 1) * GROUP_WIDTH)
            dyb0 = _dot_nt(dyb, wpb_ref[cols, :])
            dyb0_ref[:, cols] = dyb0.astype(BF16)
            dyb0s.append(dyb0)
            prod = prod + dyb0 * yb0_ref[:, cols].astype(F32)
        hi = prod.astype(BF16)
        mid = (prod - hi.astype(F32)).astype(BF16)
        lo = (prod - hi.astype(F32) - mid.astype(F32)).astype(BF16)
        delta = _dot(hi, same_head) + _dot(mid, same_head) + _dot(lo, same_head)
        lse_c = _compact_heads(lsetot_ref[...])
        delta_c = _compact_heads(delta)
        for g, (dy_ref, aux_ref) in enumerate(zip((dyl0_ref, dyl1_ref, dyl2_ref), (aux0_ref, aux1_ref, aux2_ref))):
            d = DILATIONS[g]
            _to_streams(dyb0s[g], scr, dy_ref, d, 0)
            _to_streams(lse_c, scr, aux_ref, d, 0)
            _to_streams(delta_c, scr, aux_ref, d, LANES)

        if nx:
            @pl.when(i == nt - 1)
            def _():
                for cp in _chip_exchange_copies(part_refs, recv_refs, *sems):
                    cp.wait()

    rev = lambda i: (nt - 1 - i, 0)
    pre = lambda i: (jnp.maximum((nt - 1 - i) * hb - 1, 0), 0)
    rows = lambda width: pl.BlockSpec((tm, width), rev)
    outs = [jax.ShapeDtypeStruct((s, D_MODEL), BF16), jax.ShapeDtypeStruct((s, D_MODEL), BF16),
            jax.ShapeDtypeStruct((s, 2 * D_MODEL), BF16), jax.ShapeDtypeStruct((s, 3 * CONV_WIDTH), BF16),
            jax.ShapeDtypeStruct((s, ATTN_WIDTH), BF16)]
    outs += [jax.ShapeDtypeStruct((d, s // d, GROUP_WIDTH), BF16) for d in DILATIONS]
    outs += [jax.ShapeDtypeStruct((d, s // d, 2 * LANES), F32) for d in DILATIONS]
    outs += [jax.ShapeDtypeStruct((SUBLANES, D_MODEL), F32), jax.ShapeDtypeStruct((SUBLANES, CONV_WIDTH), F32)]
    outs += [jax.ShapeDtypeStruct((3,) + a.shape[1:], a.dtype) for a in exchange]
    return pl.pallas_call(
        body, name="mix_bwd", grid=(nt,), out_shape=outs,
        in_specs=[rows(D_MODEL), rows(3 * CONV_WIDTH), pl.BlockSpec((2 * SUBLANES, 3 * CONV_WIDTH), pre),
                  rows(2 * D_MODEL), rows(D_MODEL), rows(D_MODEL), rows(ATTN_WIDTH), rows(GROUP_WIDTH),
                  _resident((3, CONV_WIDTH)), _resident((1, CONV_WIDTH)), _resident((2, D_MODEL)),
                  _resident((CONV_WIDTH, D_MODEL)), _resident((ATTN_WIDTH, D_MODEL)), _resident((D_MODEL, D_MODEL))]
        + [_ANY] * nx,
        out_specs=[rows(D_MODEL), rows(D_MODEL), rows(2 * D_MODEL), rows(3 * CONV_WIDTH), rows(ATTN_WIDTH)]
        + [_rev_stream_block(tm, d, GROUP_WIDTH, nt) for d in DILATIONS]
        + [_rev_stream_block(tm, d, 2 * LANES, nt) for d in DILATIONS]
        + [pl.BlockSpec((SUBLANES, D_MODEL), lambda i: (0, 0)), pl.BlockSpec((SUBLANES, CONV_WIDTH), lambda i: (0, 0))]
        + [_ANY] * nx,
        scratch_shapes=[pltpu.VMEM((SUBLANES, CONV_WIDTH), F32),
                        pltpu.VMEM((GROUP_WIDTH // LANES * tm, LANES), F32)] + (_dma_sems(3 * nx) if nx else []),
        compiler_params=_params(1))(dx1, abcv, abcv, gates, ya, yb, yb0, lsetot,
                                    conv_w, conv_b, b_gate, w_pa, w_pb, w_out, *exchange)


def _attn_bwd(qkv, dy, aux, gi, exchange=(), swap=()):
    d, length, _ = qkv.shape
    nb = length // ATTN_BLOCK
    q = ATTN_BLOCK
    gw = GROUP_WIDTH
    g = min(ATTN_BLOCKS_PER_STEP, nb)
    assert g >= 2 and nb % g == 0
    ns = nb // g
    lag = 1 if ns > 1 else 0
    tail = (g - 1) * q
    nx, nw = len(exchange), len(swap)

    def body(*refs):
        q_ref, kp_ref, kc_ref, vp_ref, vc_ref, dy_ref, aux_ref = refs[:7]
        part_refs = refs[7:7 + nx]
        slab_refs = refs[7 + nx:7 + nx + nw]
        pos = 7 + nx + nw
        out_ref = refs[pos]
        recv_refs = refs[pos + 1:pos + 1 + nx]
        swapped_refs = refs[pos + 1 + nx:pos + 1 + nx + nw]
        pos += 1 + nx + nw
        dq_ref, dkv_ref, bias_ref = refs[pos:pos + 3]
        sems = refs[pos + 3:]
        n = pl.program_id(1)

        def copies():
            cps = _chip_exchange_copies(part_refs, recv_refs, sems[0], sems[1]) if nx else []
            return cps + (_sibling_swap_copies(slab_refs, swapped_refs, sems[-2], sems[-1]) if nw else [])

        @pl.when((pl.program_id(0) == 0) & (n == 0))
        def _():
            _store_band_biases(bias_ref)
            for cp in copies():
                cp.start()

        if nx or nw:
            @pl.when((pl.program_id(0) == d - 1) & (n == ns - 1 + lag))
            def _():
                for cp in copies():
                    cp.wait()

        def emit(rows):
            out_ref[rows, gw:2 * gw] = dkv_ref[0, rows].astype(BF16)
            out_ref[rows, 2 * gw:3 * gw] = dkv_ref[1, rows].astype(BF16)

        if lag:
            @pl.when(n > 0)
            def _():
                out_ref[:, 0:gw] = dq_ref[...].astype(BF16)
                emit(slice(0, tail))

            @pl.when(n == ns)
            def _():
                emit(slice(tail, g * q))

        @pl.when(n < ns)
        def _():
            kfull = jnp.concatenate([kp_ref[...], kc_ref[...]], axis=0)
            vfull = jnp.concatenate([vp_ref[...], vc_ref[...]], axis=0)
            for j in range(g):
                rows = slice(j * q, (j + 1) * q)
                qs = _stack_heads(q_ref[rows, :])
                dys = _stack_heads(dy_ref[rows, :])
                k2 = kfull[j * q:(j + 2) * q]
                v2 = vfull[j * q:(j + 2) * q]
                lse = _compact_head_col(aux_ref[rows, 0:LANES])
                delta = _compact_head_col(aux_ref[rows, LANES:2 * LANES])
                bias = jnp.where(n == 0, bias_ref[1], bias_ref[0]) if j == 0 else bias_ref[0]
                p = jnp.exp(_dot_nt(qs, k2) * ATTN_SCALE + bias - lse)
                dp = _dot_nt(dys, v2)
                ds = (p * (dp - delta) * ATTN_SCALE).astype(BF16)
                dq_j = _unstack_heads(_dot(ds, k2))
                dk2 = _dot_tn(ds, qs)
                dv2 = _dot_tn(p.astype(BF16), dys)
                if j == 0:
                    @pl.when(n > 0)
                    def _():
                        out_ref[tail:g * q, gw:2 * gw] = (dkv_ref[0, tail:g * q] + dk2[0:q]).astype(BF16)
                        out_ref[tail:g * q, 2 * gw:3 * gw] = (dkv_ref[1, tail:g * q] + dv2[0:q]).astype(BF16)
                else:
                    dkv_ref[0, (j - 1) * q:j * q] += dk2[0:q]
                    dkv_ref[1, (j - 1) * q:j * q] += dv2[0:q]
                dkv_ref[0, rows] = dk2[q:2 * q]
                dkv_ref[1, rows] = dv2[q:2 * q]
                dq_ref[rows, :] = dq_j
            if not lag:
                out_ref[:, 0:gw] = dq_ref[...].astype(BF16)
                emit(slice(0, g * q))

    cur, prev = _attn_block_specs(g, nb, clamp_last=True)
    return pl.pallas_call(
        body, name=f"attn_bwd_g{gi}", grid=(d, ns + lag),
        out_shape=[jax.ShapeDtypeStruct((d, length, 3 * gw), BF16)]
        + [jax.ShapeDtypeStruct((3,) + a.shape[1:], a.dtype) for a in exchange] + _swap_shapes(swap),
        in_specs=[cur(0), prev(1), cur(1), prev(2), cur(2), cur(0), cur(0, 2 * LANES)] + [_ANY] * (nx + nw),
        out_specs=[pl.BlockSpec((None, g * q, 3 * gw), lambda r, n: (r, jnp.maximum(n - lag, 0), 0))]
        + [_ANY] * (nx + nw),
        scratch_shapes=[pltpu.VMEM((g * q, gw), F32), pltpu.VMEM((2, g * q, gw), F32),
                        pltpu.VMEM((2, HEADS_PER_GROUP * q, 2 * q), F32)]
        + (_dma_sems(3 * nx) if nx else []) + (_dma_sems(nw) if nw else []),
        compiler_params=_params(2))(qkv, qkv, qkv, qkv, qkv, dy, aux, *exchange, *swap)


def _matmul_tn(name, a, b, col_tile=1024, row_tile=2048, slabs=0, swap=()):
    s, k = a.shape
    n = b.shape[1]
    tk = min(row_tile, s)
    tn = col_tile
    steps = s // tk
    nw = len(swap)

    def body(*refs):
        a_ref, b_ref = refs[:2]
        slab_refs = refs[2:2 + nw]
        o_ref = refs[2 + nw]
        swapped_refs = refs[3 + nw:3 + 2 * nw]
        acc_ref = refs[3 + 2 * nw]
        sems = refs[4 + 2 * nw:]
        t = pl.program_id(1)

        if nw:
            @pl.when((pl.program_id(0) == 0) & (t == 0))
            def _():
                for cp in _sibling_swap_copies(slab_refs, swapped_refs, *sems):
                    cp.start()

            @pl.when((pl.program_id(0) == n // tn - 1) & (t == steps - 1))
            def _():
                for cp in _sibling_swap_copies(slab_refs, swapped_refs, *sems):
                    cp.wait()

        @pl.when(t == 0)
        def _():
            acc_ref[...] = jnp.zeros_like(acc_ref)

        acc_ref[...] += _dot_tn(a_ref[...], b_ref[...])

        @pl.when(t == steps - 1)
        def _():
            if slabs:
                for q in range(per_tile):
                    o_ref[q] = acc_ref[:, q * width:(q + 1) * width].astype(BF16)
            else:
                o_ref[...] = acc_ref[...].astype(BF16)

    if slabs:
        width = n // slabs
        per_tile = tn // width
        out_shape = jax.ShapeDtypeStruct((slabs, k, width), BF16)
        out_spec = pl.BlockSpec((per_tile, k, width), lambda j, t: (j, 0, 0))
    else:
        out_shape = jax.ShapeDtypeStruct((k, n), BF16)
        out_spec = pl.BlockSpec((k, tn), lambda j, t: (0, j))
    res = pl.pallas_call(
        body, name=name, grid=(n // tn, steps), out_shape=[out_shape] + _swap_shapes(swap),
        in_specs=[pl.BlockSpec((tk, k), lambda j, t: (t, 0)), pl.BlockSpec((tk, tn), lambda j, t: (t, j))] + [_ANY] * nw,
        out_specs=[out_spec] + [_ANY] * nw,
        scratch_shapes=[pltpu.VMEM((k, tn), F32)] + (_dma_sems(nw) if nw else []),
        compiler_params=_params(2))(a, b, *swap)
    return res if nw else res[0]


def _sibling_swap_halves(name, slabs):
    na = len(slabs)

    def body(*refs):
        src_refs, out_refs = refs[:na], refs[na:2 * na]
        send_sems, recv_sems = refs[2 * na:]
        x, y, c, _ = _mesh_position()
        cps = []
        for a in range(na):
            theirs = _half_rows(1 - c, src_refs[a].shape[1] // 2)
            cps.append(pltpu.make_async_remote_copy(
                src_ref=src_refs[a].at[:, theirs, :], dst_ref=out_refs[a], send_sem=send_sems.at[a],
                recv_sem=recv_sems.at[a], device_id=(x, y, 1 - c), device_id_type=MESH_ID))
        for cp in cps:
            cp.start()
        for cp in cps:
            cp.wait()

    return pl.pallas_call(
        body, name=name,
        out_shape=[jax.ShapeDtypeStruct((a.shape[0], a.shape[1] // 2, a.shape[2]), a.dtype) for a in slabs],
        in_specs=[_ANY] * na, out_specs=[_ANY] * na,
        scratch_shapes=[pltpu.SemaphoreType.DMA((na,)), pltpu.SemaphoreType.DMA((na,))])(*slabs)


_HBM = pl.BlockSpec(memory_space=pltpu.HBM)
_SEM = pl.BlockSpec(memory_space=pltpu.SEMAPHORE)
_DATAFLOW = pltpu.SideEffectType.DATAFLOW_SIDE_EFFECTING


def _gather_start(shard):
    gathered = jax.ShapeDtypeStruct((N_CHIPS,) + shard.shape[1:], shard.dtype)

    def body(src_ref, buf_ref, send_sems, recv_sems, src_thru, buf_thru, token):
        for cp in _gather_first_copies([src_ref], [], [buf_ref], [], send_sems, recv_sems):
            cp.start()
        token[...] = jnp.zeros_like(token)

    return pl.pallas_call(
        body, name="gather_start",
        out_shape=(pltpu.SemaphoreType.DMA((3,)), pltpu.SemaphoreType.DMA((3,)),
                   pltpu.HBM(shard.shape, shard.dtype), pltpu.HBM(gathered.shape, gathered.dtype),
                   jax.ShapeDtypeStruct((SUBLANES, LANES), F32)),
        in_specs=(_HBM, _HBM), out_specs=(_SEM, _SEM, _HBM, _HBM, _VMEM), input_output_aliases={0: 2, 1: 3},
        compiler_params=pltpu.CompilerParams(has_side_effects=_DATAFLOW),
    )(pltpu.with_memory_space_constraint(shard, pltpu.HBM),
      pltpu.with_memory_space_constraint(lax.empty(gathered.shape, gathered.dtype), pltpu.HBM))


def _gather_forward(send_sems, recv_sems, shard_thru, buf_thru, after):
    def body(src_ref, buf_ref, send_sems, recv_sems, after_ref, fwd_send, fwd_recv, buf_out):
        for cp in _gather_first_copies([src_ref], [], [buf_ref], [], send_sems, recv_sems):
            cp.wait_send()
            cp.wait_recv()
        for cp in _gather_forward_copies([buf_ref], fwd_send, fwd_recv):
            cp.start()

    return pl.pallas_call(
        body, name="gather_forward",
        out_shape=(pltpu.SemaphoreType.DMA((3,)), pltpu.SemaphoreType.DMA((3,)),
                   pltpu.HBM(buf_thru.shape, buf_thru.dtype)),
        in_specs=(_HBM, _HBM, _SEM, _SEM, _ANY), out_specs=(_SEM, _SEM, _HBM), input_output_aliases={1: 2},
        compiler_params=pltpu.CompilerParams(has_side_effects=_DATAFLOW),
    )(shard_thru, buf_thru, send_sems, recv_sems, after)


def _gather_wait(fwd_send, fwd_recv, buf_thru):
    def body(buf_ref, fwd_send, fwd_recv, buf_out):
        for cp in _gather_forward_copies([buf_ref], fwd_send, fwd_recv):
            cp.wait_send()
            cp.wait_recv()

    return pl.pallas_call(
        body, name="gather_wait", out_shape=pltpu.HBM(buf_thru.shape, buf_thru.dtype),
        in_specs=(_HBM, _SEM, _SEM), out_specs=_HBM, input_output_aliases={0: 0},
        compiler_params=pltpu.CompilerParams(has_side_effects=_DATAFLOW),
    )(buf_thru, fwd_send, fwd_recv)


def _chip_exchange_start(partial):
    _, rows, cols = partial.shape
    landing = jax.ShapeDtypeStruct((3, rows, cols), partial.dtype)

    def body(src_ref, land_ref, send_sems, recv_sems, src_thru, land_thru, token):
        for cp in _chip_exchange_copies([src_ref], [land_ref], send_sems, recv_sems):
            cp.start()
        token[...] = jnp.zeros_like(token)

    return pl.pallas_call(
        body, name="grad_exchange_start",
        out_shape=(pltpu.SemaphoreType.DMA((3,)), pltpu.SemaphoreType.DMA((3,)),
                   pltpu.HBM(partial.shape, partial.dtype), pltpu.HBM(landing.shape, landing.dtype),
                   jax.ShapeDtypeStruct((SUBLANES, LANES), F32)),
        in_specs=(_HBM, _HBM), out_specs=(_SEM, _SEM, _HBM, _HBM, _VMEM), input_output_aliases={0: 2, 1: 3},
        compiler_params=pltpu.CompilerParams(has_side_effects=_DATAFLOW),
    )(pltpu.with_memory_space_constraint(partial, pltpu.HBM),
      pltpu.with_memory_space_constraint(lax.empty(landing.shape, landing.dtype), pltpu.HBM))


def _chip_exchange_wait(send_sems, recv_sems, src_thru, land_thru, after):
    def body(src_ref, land_ref, send_sems, recv_sems, after_ref, src_out, land_out):
        for cp in _chip_exchange_copies([src_ref], [land_ref], send_sems, recv_sems):
            cp.wait_send()
            cp.wait_recv()

    return pl.pallas_call(
        body, name="grad_exchange_wait",
        out_shape=(pltpu.HBM(src_thru.shape, src_thru.dtype), pltpu.HBM(land_thru.shape, land_thru.dtype)),
        in_specs=(_HBM, _HBM, _SEM, _SEM, _ANY), out_specs=(_HBM, _HBM), input_output_aliases={0: 0, 1: 1},
        compiler_params=pltpu.CompilerParams(has_side_effects=_DATAFLOW),
    )(src_thru, land_thru, send_sems, recv_sems, after)


def _sibling_share(halves):
    na = len(halves)

    def body(*refs):
        out_refs = refs[na:2 * na]
        send_sems, recv_sems = refs[2 * na:]
        x, y, c, _ = _mesh_position()
        cps = []
        for a in range(na):
            mine = out_refs[a].at[0, _half_rows(c, out_refs[a].shape[1] // 2)]
            cps.append(pltpu.make_async_remote_copy(src_ref=mine, dst_ref=mine, send_sem=send_sems.at[a],
                                                    recv_sem=recv_sems.at[a], device_id=(x, y, 1 - c),
                                                    device_id_type=MESH_ID))
        for cp in cps:
            cp.start()
        for a, cp in enumerate(cps):
            cp.wait_send()
            theirs = out_refs[a].at[0, _half_rows(1 - c, out_refs[a].shape[1] // 2)]
            pltpu.make_async_remote_copy(src_ref=theirs, dst_ref=theirs, send_sem=send_sems.at[a],
                                         recv_sem=recv_sems.at[a], device_id=(x, y, 1 - c),
                                         device_id_type=MESH_ID).wait_recv()

    return pl.pallas_call(
        body, name="grad_sibling_share", out_shape=[jax.ShapeDtypeStruct(a.shape, a.dtype) for a in halves],
        in_specs=[_ANY] * na, out_specs=[_ANY] * na, input_output_aliases={a: a for a in range(na)},
        scratch_shapes=[pltpu.SemaphoreType.DMA((na,)), pltpu.SemaphoreType.DMA((na,))])(*halves)


def _add_sibling(name, slab, received, core):
    n, rows, cols = slab.shape
    half = rows // 2

    def body(core_ref, a_ref, b_ref, o_ref):
        o_ref[...] = (a_ref[...].astype(F32) + b_ref[...].astype(F32)).astype(BF16)

    grid_spec = pltpu.PrefetchScalarGridSpec(
        num_scalar_prefetch=1, grid=(n,),
        in_specs=[pl.BlockSpec((None, half, cols), lambda s, core_ref: (s, core_ref[0], 0)),
                  pl.BlockSpec((None, half, cols), lambda s, core_ref: (s, 0, 0))],
        out_specs=pl.BlockSpec((None, half, cols), lambda s, core_ref: (s, 0, 0)))
    return pl.pallas_call(body, name=name, grid_spec=grid_spec,
                          out_shape=jax.ShapeDtypeStruct((n, half, cols), BF16),
                          compiler_params=_params(1))(core, slab, received)


def _sum_chips(name, partial, received, chip_core):
    _, half, cols = partial.shape

    def body(cc_ref, own_ref, recv_ref, o_ref):
        acc = own_ref[...].astype(F32)
        for k in range(3):
            acc = acc + recv_ref[k].astype(F32)
        o_ref[...] = acc

    grid_spec = pltpu.PrefetchScalarGridSpec(
        num_scalar_prefetch=1, grid=(1,),
        in_specs=[pl.BlockSpec((None, half, cols), lambda i, cc_ref: (cc_ref[0], 0, 0)),
                  pl.BlockSpec((3, half, cols), lambda i, cc_ref: (0, 0, 0))],
        out_specs=pl.BlockSpec((None, half, cols), lambda i, cc_ref: (0, cc_ref[1], 0)))
    return pl.pallas_call(body, name=name, grid_spec=grid_spec,
                          out_shape=jax.ShapeDtypeStruct((1, 2 * half, cols), F32),
                          compiler_params=_params(1))(chip_core, partial, received)


def _adam_math(w, g, m, v):
    nm = ADAM_B1 * m + (1.0 - ADAM_B1) * g
    nv = ADAM_B2 * v + (1.0 - ADAM_B2) * jnp.square(g)
    m_hat = nm / (1.0 - ADAM_B1 ** ADAM_STEP)
    v_hat = nv / (1.0 - ADAM_B2 ** ADAM_STEP)
    delta = -ADAM_LR * (m_hat / (jnp.sqrt(v_hat) + ADAM_EPS) + ADAM_WD * w)
    return delta, nm, nv


def _adamw(name, w, g, m, v):
    _, rows, cols = w.shape
    tr = next(t for t in (736, 512, 384, 352, 256, 128, 64, 32, 16, 8) if rows % t == 0)

    def body(w_ref, g_ref, m_ref, v_ref, d_ref, nm_ref, nv_ref):
        d_ref[...], nm_ref[...], nv_ref[...] = _adam_math(w_ref[...], g_ref[...], m_ref[...], v_ref[...])

    spec = pl.BlockSpec((None, tr, cols), lambda i: (0, i, 0))
    return pl.pallas_call(
        body, name=name, grid=(rows // tr,), out_shape=[jax.ShapeDtypeStruct(w.shape, F32)] * 3,
        in_specs=[spec] * 4, out_specs=[spec] * 3, compiler_params=_params(1))(w, g, m, v)


SMALL_PARAMS = ("norm_mix_g", "b_gate", "conv_a_w", "conv_a_b", "norm_ffn_g", "ffn_conv_w", "ffn_conv_b", "final_norm_g")


def _small_update(partials, params, moments_m, moments_v):
    na = len(partials)
    npar = len(SMALL_PARAMS)

    def body(*refs):
        in_refs = refs[:na]
        w_refs = refs[na:na + npar]
        m_refs = refs[na + npar:na + 2 * npar]
        v_refs = refs[na + 2 * npar:na + 3 * npar]
        pos = na + 3 * npar
        loss_ref = refs[pos]
        out_refs = refs[pos + 1:pos + 1 + 4 * npar]
        pos += 1 + 4 * npar
        acc_refs = refs[pos:pos + na]
        recv_refs = refs[pos + na:pos + 4 * na]
        send_sems, recv_sems = refs[pos + 4 * na:]
        x, y, c, _ = _mesh_position()
        chip = 2 * x + y
        for a in range(na):
            acc_refs[a][...] = in_refs[a][...]
        for stage, peer in enumerate(((x, y, 1 - c), (x, 1 - y, c), (1 - x, y, c))):
            cps = []
            for a in range(na):
                k = stage * na + a
                cps.append(pltpu.make_async_remote_copy(src_ref=acc_refs[a], dst_ref=recv_refs[k], send_sem=send_sems.at[k],
                                                        recv_sem=recv_sems.at[k], device_id=peer, device_id_type=MESH_ID))
            for cp in cps:
                cp.start()
            for cp in cps:
                cp.wait()
            for a in range(na):
                acc_refs[a][...] = acc_refs[a][...] + recv_refs[stage * na + a][...]

        mix, ffn, fin, gate, conv, ffnc, loss = acc_refs
        loss_ref[...] = loss[...]

        def cols(width):
            return pl.ds(pl.multiple_of(chip * width, LANES), width)

        grads = {
            "norm_mix_g": mix[...], "norm_ffn_g": ffn[...], "final_norm_g": fin[...],
            "b_gate": gate[0:2, cols(D_MODEL // N_CHIPS)],
            "conv_a_w": conv[0:3, cols(CONV_WIDTH // N_CHIPS)], "conv_a_b": conv[3:4, :],
            "ffn_conv_w": ffnc[0:3, cols(2 * D_FF // N_CHIPS)], "ffn_conv_b": ffnc[3:4, :]}
        for i, name in enumerate(SMALL_PARAMS):
            g = grads[name]
            if len(w_refs[i].shape) == 3:
                results = (g,) + _adam_math(w_refs[i][0], g, m_refs[i][0], v_refs[i][0])
                for o_ref, val in zip(out_refs[4 * i:4 * i + 4], results):
                    o_ref[0] = val
            else:
                results = (g,) + _adam_math(w_refs[i][...], g, m_refs[i][...], v_refs[i][...])
                for o_ref, val in zip(out_refs[4 * i:4 * i + 4], results):
                    o_ref[...] = val

    outs = [jax.ShapeDtypeStruct(partials[-1].shape, F32)]
    for w in params:
        outs += [jax.ShapeDtypeStruct(w.shape, F32)] * 4
    scratch = [pltpu.VMEM(p.shape, F32) for p in partials]
    scratch += [pltpu.VMEM(p.shape, F32) for _ in range(3) for p in partials]
    scratch += [pltpu.SemaphoreType.DMA((3 * na,)), pltpu.SemaphoreType.DMA((3 * na,))]
    n_in = na + 3 * npar
    return pl.pallas_call(
        body, name="small_update", out_shape=outs, in_specs=[_VMEM] * n_in, out_specs=[_VMEM] * len(outs),
        scratch_shapes=scratch)(*partials, *params, *moments_m, *moments_v)


def _gathered_columns(g):
    return jnp.transpose(g, (1, 0, 2)).reshape(g.shape[1], N_CHIPS * g.shape[2])


def _column_slabs(full):
    k, n = full.shape
    return jnp.transpose(full.reshape(k, N_CHIPS, n // N_CHIPS), (1, 0, 2))


def kernel(x, norm_mix_g, w_in, b_gate, conv_a_w, conv_a_b, w_proj_a, w_proj_b, w_out, norm_ffn_g, w_up, ffn_conv_w, ffn_conv_b, w_down, final_norm_g, loss_target, m_norm_mix_g, m_w_in, m_b_gate, m_conv_a_w, m_conv_a_b, m_w_proj_a, m_w_proj_b, m_w_out, m_norm_ffn_g, m_w_up, m_ffn_conv_w, m_ffn_conv_b, m_w_down, m_final_norm_g, v_norm_mix_g, v_w_in, v_b_gate, v_conv_a_w, v_conv_a_b, v_w_proj_a, v_w_proj_b, v_w_out, v_norm_ffn_g, v_w_up, v_ffn_conv_w, v_ffn_conv_b, v_w_down, v_final_norm_g):
    chip = (2 * lax.axis_index("x") + lax.axis_index("y")).astype(jnp.int32)
    core = lax.axis_index("c").astype(jnp.int32)
    core_arr = core.reshape(1)
    chip_core = jnp.stack([chip, core])
    xs, target = x[0], loss_target[0]
    g_final = final_norm_g.reshape(1, D_MODEL)

    def own_slot(gathered, own):
        return lax.dynamic_update_slice(gathered, own, (chip, 0, 0))

    w_in_t, m_w_in_t, v_w_in_t = (jnp.swapaxes(a, 1, 2) for a in (w_in, m_w_in, v_w_in))
    w_in_tb = w_in_t.astype(BF16)
    send1, recv1, shard_thru, g_in, token = _gather_start(w_in_tb)
    h1, h1_streams4, h1_streams16 = _norm_fwd(xs, norm_mix_g + token[0:1, 0:1])
    send2, recv2, g_in = _gather_forward(send1, recv1, shard_thru, g_in, h1)
    g_in = _gather_wait(send2, recv2, g_in)
    w_in_full_t = own_slot(g_in, w_in_tb).reshape(D_IN, D_MODEL)
    later_w = [w_proj_a, w_proj_b, w_out, w_up, w_down]
    later_b = [w.astype(BF16) for w in later_w]
    small_sharded = [b_gate, conv_a_w, ffn_conv_w]
    fwd = _inproj_fwd(h1, w_in_full_t, later_b, small_sharded)
    abcv, gates, qkv0, qkv1, qkv2 = fwd[:5]
    gathered_big, gathered_small = fwd[5:10], fwd[10:13]
    qkvs = (qkv0, qkv1, qkv2)
    attn0 = _attn_fwd(qkv0, 0, forward=gathered_big)
    attn = [attn0[:2], _attn_fwd(qkv1, 1), _attn_fwd(qkv2, 2)]
    g_pa, g_pb, g_out, g_up, g_down = [own_slot(g, own) for g, own in zip(attn0[2:], later_b)]
    g_bgate, g_convw, g_ffnw = [own_slot(g, own) for g, own in zip(gathered_small, small_sharded)]
    w_pa_full, w_pb_full, w_up_full = _gathered_columns(g_pa), _gathered_columns(g_pb), _gathered_columns(g_up)
    w_out_full, w_down_full = g_out.reshape(D_MODEL, D_MODEL), g_down.reshape(D_FF, D_MODEL)
    b_gate_full, conv_w_full, ffn_w_full = (_gathered_columns(g) for g in (g_bgate, g_convw, g_ffnw))

    x1, ya0, yb0, mrg, ya, yb, lsetot = _mix_fwd(
        xs, abcv, gates, [a[0] for a in attn], [a[1] for a in attn], conv_w_full, conv_a_b, b_gate_full,
        w_pa_full, w_pb_full, w_out_full)
    h2, up0, up = _ffn_up_fwd(x1, norm_ffn_g, w_up_full, ffn_w_full, ffn_conv_b)
    act, dx2, dx2b, d_g_final, loss = _ffn_act_fwd(x1, up, target, w_down_full, g_final)

    def add_sibling(names, slabs, swapped):
        return [_add_sibling("grad_add_" + n, s, r, core_arr) for n, s, r in zip(names, slabs, swapped)]

    def sum_chips(names, partials, received):
        return [_sum_chips("grad_sum_" + n, p, r, chip_core) for n, p, r in zip(names, partials, received)]

    d_up = _ffn_act_bwd(dx2b, up, w_down_full)
    slab_down = _matmul_tn("dw_down", act, dx2b, col_tile=512).reshape(N_CHIPS, D_FF // N_CHIPS, D_MODEL)
    dx1, dx1b, d_g_ffn, d_up0, ffn_small, swapped_down = _ffn_up_bwd(
        d_up, up0, w_up_full, ffn_w_full, x1, norm_ffn_g, dx2, swap=[slab_down])
    (partial_down,) = add_sibling(["w_down"], [slab_down], [swapped_down])
    slab_up = _matmul_tn("dw_up", h2, d_up0, col_tile=2 * D_FF // N_CHIPS, slabs=N_CHIPS)
    d_w_out, swapped_up = _matmul_tn("dw_out", mrg, dx1b, swap=[slab_up])
    (partial_up,) = add_sibling(["w_up"], [slab_up], [swapped_up])

    mix_res = _mix_bwd(dx1, abcv, gates, ya, yb, yb0, lsetot, conv_w_full, conv_a_b, b_gate_full,
                       w_pa_full, w_pb_full, w_out_full, exchange=[partial_up, partial_down])
    (d_ya, d_yb, d_gates, d_abcv, d_yb0, dyl0, dyl1, dyl2, aux0, aux1, aux2, gate_small, conv_small) = mix_res[:13]
    halves_ffn = sum_chips(["w_up", "w_down"], [partial_up, partial_down], mix_res[13:])

    mix_names = ["w_proj_a", "w_proj_b", "w_out"]
    slabs_mix = [_matmul_tn("dw_proj_a", ya0, d_ya, slabs=N_CHIPS), _matmul_tn("dw_proj_b", yb0, d_yb, slabs=N_CHIPS),
                 d_w_out.reshape(N_CHIPS, D_MODEL // N_CHIPS, D_MODEL)]
    res0 = _attn_bwd(qkv0, dyl0, aux0, 0, swap=slabs_mix)
    d_qkv0, partials_mix = res0[0], add_sibling(mix_names, slabs_mix, res0[1:])
    res1 = _attn_bwd(qkv1, dyl1, aux1, 1, exchange=partials_mix)
    d_qkv1, halves_mix = res1[0], sum_chips(mix_names, partials_mix, res1[1:])
    (d_qkv2,) = _attn_bwd(qkv2, dyl2, aux2, 2)

    dq = [d_qkv0, d_qkv1, d_qkv2]
    seq = xs.shape[0]
    d_w_abcv = _matmul_tn("dw_in_abcv", d_abcv, h1)
    d_w_gates = _matmul_tn("dw_in_gates", d_gates, h1)
    d_w_groups = [_matmul_tn(f"dw_in_qkv{g}", t.reshape(seq, 3 * GROUP_WIDTH), h.reshape(seq, D_MODEL))
                  for g, (t, h) in enumerate(zip(dq, (h1, h1_streams4, h1_streams16)))]
    gw = GROUP_WIDTH
    d_w_in_t = jnp.concatenate(
        [d_w_abcv] + [d_w_groups[g][j * gw:(j + 1) * gw] for j in range(3) for g in range(3)] + [d_w_gates], axis=0)

    slab_in = d_w_in_t.reshape(N_CHIPS, D_IN // N_CHIPS, D_MODEL)
    (from_sibling_in,) = _sibling_swap_halves("grad_swap_w_in", [slab_in])
    partial_in = _add_sibling("grad_add_w_in", slab_in, from_sibling_in, core_arr)
    send_sems, recv_sems, partial_thru, landing_thru, token = _chip_exchange_start(partial_in)
    g_mix_after_start = norm_mix_g + token[0:1, 0:1]
    grad_x, d_g_mix = _inproj_bwd(d_abcv, d_gates, dq, w_in_full_t, xs, g_mix_after_start, dx1)
    partial_in, received_in = _chip_exchange_wait(send_sems, recv_sems, partial_thru, landing_thru, d_g_mix)
    halves_in = sum_chips(["w_in"], [partial_in], [received_in])

    big_names = ("w_in", "w_proj_a", "w_proj_b", "w_out", "w_up", "w_down")
    big_grads = _sibling_share(halves_in + halves_mix + halves_ffn)
    big_w = dict(w_in=w_in_t, w_proj_a=w_proj_a, w_proj_b=w_proj_b, w_out=w_out, w_up=w_up, w_down=w_down)
    big_m = dict(w_in=m_w_in_t, w_proj_a=m_w_proj_a, w_proj_b=m_w_proj_b, w_out=m_w_out, w_up=m_w_up, w_down=m_w_down)
    big_v = dict(w_in=v_w_in_t, w_proj_a=v_w_proj_a, w_proj_b=v_w_proj_b, w_out=v_w_out, w_up=v_w_up, w_down=v_w_down)

    fin_w, fin_m, fin_v = (a.reshape(1, D_MODEL) for a in (final_norm_g, m_final_norm_g, v_final_norm_g))
    small_w = [norm_mix_g, b_gate, conv_a_w, conv_a_b, norm_ffn_g, ffn_conv_w, ffn_conv_b, fin_w]
    small_m = [m_norm_mix_g, m_b_gate, m_conv_a_w, m_conv_a_b, m_norm_ffn_g, m_ffn_conv_w, m_ffn_conv_b, fin_m]
    small_v = [v_norm_mix_g, v_b_gate, v_conv_a_w, v_conv_a_b, v_norm_ffn_g, v_ffn_conv_w, v_ffn_conv_b, fin_v]
    small_out = _small_update([d_g_mix, d_g_ffn, d_g_final, gate_small, conv_small, ffn_small, loss],
                              small_w, small_m, small_v)
    total_loss = small_out[0][0, 0]

    grads, delta, new_m, new_v = {}, {}, {}, {}
    for i, n in enumerate(SMALL_PARAMS):
        vals = small_out[1 + 4 * i:5 + 4 * i]
        if n == "final_norm_g":
            vals = [a.reshape(D_MODEL) for a in vals]
        grads[n], delta[n], new_m[n], new_v[n] = vals
    for n, g in zip(big_names, big_grads):
        vals = (g,) + tuple(_adamw("adamw_" + n, big_w[n], g, big_m[n], big_v[n]))
        if n == "w_in":
            vals = [jnp.swapaxes(a, 1, 2) for a in vals]
        grads[n], delta[n], new_m[n], new_v[n] = vals

    names = ["norm_mix_g", "w_in", "b_gate", "conv_a_w", "conv_a_b", "w_proj_a", "w_proj_b", "w_out", "norm_ffn_g", "w_up",
             "ffn_conv_w", "ffn_conv_b", "w_down", "final_norm_g"]
    out = [total_loss, grad_x[None]]
    for group in (grads, delta, new_m, new_v):
        out += [group[n] for n in names]
    return tuple(out)
```

```python
import jax
import jax.numpy as jnp
from jax import lax
from jax.experimental import pallas as pl
from jax.experimental.pallas import tpu as pltpu

F32 = jnp.float32
BF16 = jnp.bfloat16

D_MODEL = 1024
CONV_WIDTH = 512
ATTN_WIDTH = 768
GROUP_WIDTH = 256
HEAD_DIM = 64
HEADS_PER_GROUP = 4
DILATIONS = (1, 4, 16)
ATTN_BLOCK = 128
D_FF = 2816
D_IN = 5888
EPS = 1e-6
NEG_INF = -1e30
ATTN_SCALE = HEAD_DIM ** -0.5

COL_ABCV = 0
COL_Q = 1536
COL_K = 2304
COL_V = 3072
COL_GATES = 3840

ADAM_LR = 0.001
ADAM_B1 = 0.9
ADAM_B2 = 0.999
ADAM_EPS = 1e-08
ADAM_WD = 0.01
ADAM_STEP = 10

LANES = 128
SUBLANES = 8
ROW_TILE = 512
VMEM_LIMIT = 56 * 1024 * 1024

_NT = (((1,), (1,)), ((), ()))
_TN = (((0,), (0,)), ((), ()))


def _params(n_axes, vmem=VMEM_LIMIT):
    return pltpu.CompilerParams(dimension_semantics=("arbitrary",) * n_axes, vmem_limit_bytes=vmem)


def _resident(shape):
    nd = len(shape)
    return pl.BlockSpec(shape, lambda *_: (0,) * nd, pipeline_mode=pl.Buffered(1))


def _rows(tm, width, col_block=0):
    return pl.BlockSpec((tm, width), lambda i: (i, col_block))


def _col_chunks(n, cmax):
    out, lo = [], 0
    while lo < n:
        size = min(cmax, n - lo)
        out.append((lo, size))
        lo += size
    return out


def _dot(a, b):
    return jnp.dot(a, b, preferred_element_type=F32)


def _dot_nt(a, b):
    return lax.dot_general(a, b, _NT, preferred_element_type=F32)


def _dot_tn(a, b):
    return lax.dot_general(a, b, _TN, preferred_element_type=F32)


def _sigmoid(x):
    return 0.5 * jnp.tanh(0.5 * x) + 0.5


def _silu(x):
    hx = 0.5 * x
    return hx + hx * jnp.tanh(hx)


def _shift_down(v, k, halo8):
    tm = v.shape[0]
    rolled = pltpu.roll(v, k, 0)
    fix = jnp.tile(pltpu.roll(halo8, k, 0), (tm // SUBLANES, 1))
    row = lax.broadcasted_iota(jnp.int32, v.shape, 0)
    return jnp.where(row < k, fix, rolled)


def _shift_up(v, k, halo8):
    tm = v.shape[0]
    rolled = pltpu.roll(v, tm - k, 0)
    fix = jnp.tile(pltpu.roll(halo8, SUBLANES - k, 0), (tm // SUBLANES, 1))
    row = lax.broadcasted_iota(jnp.int32, v.shape, 0)
    return jnp.where(row >= tm - k, fix, rolled)


def _colsum(v):
    return jnp.sum(v, axis=0, keepdims=True)


def _to_streams(val, scr, out_ref, d, col0):
    tm = val.shape[0]
    panels = val.shape[1] // LANES
    if d == 1:
        out_ref[0, :, col0:col0 + val.shape[1]] = val.astype(out_ref.dtype)
        return
    for p in range(panels):
        scr[pl.ds(p * tm, tm), :] = val[:, p * LANES:(p + 1) * LANES]
    for r in range(d):
        for p in range(panels):
            piece = scr[pl.ds(p * tm + r, tm // d, stride=d), :]
            out_ref[r, :, col0 + p * LANES: col0 + (p + 1) * LANES] = piece.astype(out_ref.dtype)


def _from_streams(in_ref, scr, d, col0, width):
    panels = width // LANES
    rows = in_ref.shape[1]
    tm = rows * d
    if d == 1:
        return in_ref[0, :, col0:col0 + width].astype(F32)
    for r in range(d):
        for p in range(panels):
            scr[pl.ds(p * tm + r, rows, stride=d), :] = in_ref[r, :, col0 + p * LANES: col0 + (p + 1) * LANES].astype(F32)
    return jnp.concatenate([scr[pl.ds(p * tm, tm), :] for p in range(panels)], axis=1)


def _stream_block(tm, d, width):
    return pl.BlockSpec((d, tm // d, width), lambda i: (0, i, 0))


def _rev_stream_block(tm, d, width, nt):
    return pl.BlockSpec((d, tm // d, width), lambda i: (0, nt - 1 - i, 0))


N_CHIPS = 4
MESH_ID = pl.DeviceIdType.MESH
_ANY = pl.BlockSpec(memory_space=pl.ANY)
_VMEM = pl.BlockSpec(memory_space=pltpu.VMEM)


def _mesh_position():
    x, y, c = lax.axis_index("x"), lax.axis_index("y"), lax.axis_index("c")
    other_chips = [(1 - x, y), (x, 1 - y), (1 - x, 1 - y)]
    return x, y, c, other_chips


def _half_rows(c, half):
    return pl.ds(pl.multiple_of(c * half, 16), half)


def _remote_copy(k, src, dst, to, send_sems, recv_sems):
    return pltpu.make_async_remote_copy(src_ref=src, dst_ref=dst, send_sem=send_sems.at[k], recv_sem=recv_sems.at[k],
                                        device_id=to, device_id_type=MESH_ID)


def _gather_first_copies(big_refs, small_refs, big_outs, small_outs, send_sems, recv_sems):
    x, y, c, chips = _mesh_position()
    me = 2 * x + y
    nb = len(big_refs)
    cps = []
    for j, (px, py) in enumerate(chips):
        for b in range(nb):
            mine = _half_rows(c, big_refs[b].shape[1] // 2)
            cps.append(_remote_copy(3 * b + j, big_refs[b].at[0, mine], big_outs[b].at[me, mine], (px, py, c),
                                    send_sems, recv_sems))
        for s in range(len(small_refs)):
            cps.append(_remote_copy(3 * (nb + s) + j, small_refs[s].at[0], small_outs[s].at[me], (px, py, c),
                                    send_sems, recv_sems))
    return cps


def _gather_forward_copies(bufs, send_sems, recv_sems):
    x, y, c, chips = _mesh_position()
    cps = []
    for j, (px, py) in enumerate(chips):
        for b in range(len(bufs)):
            landed = bufs[b].at[2 * px + py, _half_rows(c, bufs[b].shape[1] // 2)]
            cps.append(_remote_copy(3 * b + j, landed, landed, (x, y, 1 - c), send_sems, recv_sems))
    return cps


def _chip_exchange_copies(src_refs, out_refs, send_sems, recv_sems):
    x, y, c, chips = _mesh_position()
    cps = []
    for j, (px, py) in enumerate(chips):
        for a in range(len(src_refs)):
            cps.append(_remote_copy(3 * a + j, src_refs[a].at[2 * px + py], out_refs[a].at[j], (px, py, c),
                                    send_sems, recv_sems))
    return cps


def _sibling_swap_copies(src_refs, out_refs, send_sems, recv_sems):
    x, y, c, _ = _mesh_position()
    cps = []
    for a in range(len(src_refs)):
        theirs = _half_rows(1 - c, src_refs[a].shape[1] // 2)
        cps.append(_remote_copy(a, src_refs[a].at[:, theirs, :], out_refs[a], (x, y, 1 - c), send_sems, recv_sems))
    return cps


def _swap_shapes(slabs):
    return [jax.ShapeDtypeStruct((a.shape[0], a.shape[1] // 2, a.shape[2]), a.dtype) for a in slabs]


def _dma_sems(n):
    return [pltpu.SemaphoreType.DMA((n,)), pltpu.SemaphoreType.DMA((n,))]


def _norm_fwd(x, g):
    s = x.shape[0]
    tm = ROW_TILE

    def body(x_ref, g_ref, h_ref, hs1_ref, hs2_ref, scr):
        xv = x_ref[...]
        r = lax.rsqrt(jnp.mean(xv * xv, axis=-1, keepdims=True) + EPS)
        hf = xv * r * g_ref[...]
        h_ref[...] = hf.astype(BF16)
        for d, hs_ref in zip(DILATIONS[1:], (hs1_ref, hs2_ref)):
            for lo, size in _col_chunks(D_MODEL, GROUP_WIDTH):
                _to_streams(hf[:, lo:lo + size], scr, hs_ref, d, lo)

    return pl.pallas_call(
        body, name="norm_fwd", grid=(s // tm,),
        out_shape=[jax.ShapeDtypeStruct((s, D_MODEL), BF16)]
        + [jax.ShapeDtypeStruct((d, s // d, D_MODEL), BF16) for d in DILATIONS[1:]],
        in_specs=[_rows(tm, D_MODEL), _resident((1, D_MODEL))],
        out_specs=[_rows(tm, D_MODEL)] + [_stream_block(tm, d, D_MODEL) for d in DILATIONS[1:]],
        scratch_shapes=[pltpu.VMEM((GROUP_WIDTH // LANES * tm, LANES), F32)],
        compiler_params=_params(1))(x, g)


def _inproj_fwd(h1, w_in_t, big_shards, small_shards):
    s = h1.shape[0]
    tm = ROW_TILE
    nt = s // tm
    nb, ns = len(big_shards), len(small_shards)
    n_fixed_in, n_fixed_out = 2, 5

    def body(*refs):
        h_ref, w_ref = refs[:n_fixed_in]
        shard_refs = refs[n_fixed_in:n_fixed_in + nb + ns]
        pos = n_fixed_in + nb + ns
        abcv_ref, gates_ref, qkv0_ref, qkv1_ref, qkv2_ref = refs[pos:pos + n_fixed_out]
        gathered_refs = refs[pos + n_fixed_out:pos + n_fixed_out + nb + ns]
        scr, send_sems, recv_sems = refs[pos + n_fixed_out + nb + ns:]
        i = pl.program_id(0)

        def gather_copies():
            return _gather_first_copies(shard_refs[:nb], shard_refs[nb:], gathered_refs[:nb], gathered_refs[nb:],
                                        send_sems, recv_sems)

        @pl.when(i == 0)
        def _():
            for cp in gather_copies():
                cp.start()

        h = h_ref[...]
        for lo, size in _col_chunks(3 * CONV_WIDTH, 512):
            abcv_ref[:, lo:lo + size] = _dot_nt(h, w_ref[COL_ABCV + lo: COL_ABCV + lo + size, :]).astype(BF16)
        for lo, size in _col_chunks(2 * D_MODEL, 512):
            gates_ref[:, lo:lo + size] = _dot_nt(h, w_ref[COL_GATES + lo: COL_GATES + lo + size, :]).astype(BF16)
        for gi, (d, out_ref) in enumerate(zip(DILATIONS, (qkv0_ref, qkv1_ref, qkv2_ref))):
            for j, base in enumerate((COL_Q, COL_K, COL_V)):
                lo = base + gi * GROUP_WIDTH
                y = _dot_nt(h, w_ref[lo:lo + GROUP_WIDTH, :])
                if j == 0:
                    y = y * ATTN_SCALE
                _to_streams(y, scr, out_ref, d, j * GROUP_WIDTH)

        @pl.when(i == nt - 1)
        def _():
            for cp in gather_copies():
                cp.wait()

    outs = [jax.ShapeDtypeStruct((s, 3 * CONV_WIDTH), BF16), jax.ShapeDtypeStruct((s, 2 * D_MODEL), BF16)]
    outs += [jax.ShapeDtypeStruct((d, s // d, 3 * GROUP_WIDTH), BF16) for d in DILATIONS]
    outs += [jax.ShapeDtypeStruct((N_CHIPS,) + a.shape[1:], a.dtype) for a in list(big_shards) + list(small_shards)]
    return pl.pallas_call(
        body, name="inproj_fwd", grid=(nt,), out_shape=outs,
        in_specs=[_rows(tm, D_MODEL), _resident((D_IN, D_MODEL))] + [_ANY] * (nb + ns),
        out_specs=[_rows(tm, 3 * CONV_WIDTH), _rows(tm, 2 * D_MODEL)]
        + [_stream_block(tm, d, 3 * GROUP_WIDTH) for d in DILATIONS] + [_ANY] * (nb + ns),
        scratch_shapes=[pltpu.VMEM((GROUP_WIDTH // LANES * tm, LANES), F32)] + _dma_sems(3 * (nb + ns)),
        compiler_params=_params(1))(h1, w_in_t, *big_shards, *small_shards)


def _head_of_lane(shape):
    return lax.broadcasted_iota(jnp.int32, shape, 1) // HEAD_DIM


def _stack_heads(v):
    head = _head_of_lane(v.shape)
    return jnp.concatenate([jnp.where(head == h, v, jnp.zeros_like(v)) for h in range(HEADS_PER_GROUP)], axis=0)


def _unstack_heads(v):
    q = ATTN_BLOCK
    head = _head_of_lane((q, v.shape[1]))
    out = jnp.zeros((q, v.shape[1]), v.dtype)
    for h in range(HEADS_PER_GROUP):
        out = jnp.where(head == h, v[h * q:(h + 1) * q], out)
    return out


def _per_head_rows(col):
    q = ATTN_BLOCK
    head = _head_of_lane((q, GROUP_WIDTH))
    out = jnp.zeros((q, GROUP_WIDTH), col.dtype)
    for h in range(HEADS_PER_GROUP):
        out = jnp.where(head == h, col[h * q:(h + 1) * q], out)
    return out


def _compact_heads(v):
    lane = lax.broadcasted_iota(jnp.int32, (v.shape[0], LANES), 1)
    return jnp.where((lane & 32) == 0, v[:, 0:LANES], v[:, LANES:2 * LANES])


def _compact_head_col(v):
    lane = lax.broadcasted_iota(jnp.int32, v.shape, 1)
    head = ((lane >> 6) & 1) + 2 * ((lane >> 5) & 1)
    cols = [jnp.max(jnp.where(head == h, v, -jnp.inf), axis=1, keepdims=True) for h in range(HEADS_PER_GROUP)]
    return jnp.concatenate(cols, axis=0)


ATTN_BLOCKS_PER_STEP = 4


def _band_bias(first_block):
    rows = HEADS_PER_GROUP * ATTN_BLOCK
    qi = lax.broadcasted_iota(jnp.int32, (rows, 2 * ATTN_BLOCK), 0) % ATTN_BLOCK
    kj = lax.broadcasted_iota(jnp.int32, (rows, 2 * ATTN_BLOCK), 1)
    dist = qi + ATTN_BLOCK - kj
    valid = (dist >= 0) & (dist <= ATTN_BLOCK)
    if first_block:
        valid = valid & (kj >= ATTN_BLOCK)
    return jnp.where(valid, 0.0, NEG_INF).astype(F32)


def _store_band_biases(bias_ref):
    bias_ref[0] = _band_bias(False)
    bias_ref[1] = _band_bias(True)


def _attn_block_specs(g, nb, clamp_last=False):
    q = ATTN_BLOCK
    last = nb // g - 1

    def cur(col, width=GROUP_WIDTH):
        if clamp_last:
            return pl.BlockSpec((None, g * q, width), lambda r, n: (r, jnp.minimum(n, last), col))
        return pl.BlockSpec((None, g * q, width), lambda r, n: (r, n, col))

    def prev(col):
        if clamp_last:
            return pl.BlockSpec((None, q, GROUP_WIDTH), lambda r, n: (r, jnp.clip(n * g - 1, 0, nb - 1), col))
        return pl.BlockSpec((None, q, GROUP_WIDTH), lambda r, n: (r, jnp.maximum(n * g - 1, 0), col))

    return cur, prev


def _attn_fwd(qkv, gi, forward=()):
    d, length, _ = qkv.shape
    nb = length // ATTN_BLOCK
    q = ATTN_BLOCK
    g = min(ATTN_BLOCKS_PER_STEP, nb)
    ns = nb // g
    nf = len(forward)

    def body(*refs):
        q_ref, kp_ref, kc_ref, vp_ref, vc_ref = refs[:5]
        o_ref, lse_ref = refs[5 + nf:7 + nf]
        buf_refs = refs[7 + nf:7 + 2 * nf]
        bias_ref = refs[7 + 2 * nf]
        sems = refs[8 + 2 * nf:]
        n = pl.program_id(1)
        first_step = (pl.program_id(0) == 0) & (n == 0)
        last_step = (pl.program_id(0) == d - 1) & (n == ns - 1)

        @pl.when(first_step)
        def _():
            _store_band_biases(bias_ref)
            for cp in _gather_forward_copies(buf_refs, *sems) if nf else ():
                cp.start()

        kfull = jnp.concatenate([kp_ref[...], kc_ref[...]], axis=0)
        vfull = jnp.concatenate([vp_ref[...], vc_ref[...]], axis=0)
        for j in range(g):
            qs = _stack_heads(q_ref[j * q:(j + 1) * q, :])
            k2 = kfull[j * q:(j + 2) * q]
            v2 = vfull[j * q:(j + 2) * q]
            bias = jnp.where(n == 0, bias_ref[1], bias_ref[0]) if j == 0 else bias_ref[0]
            sc = _dot_nt(qs, k2) + bias
            m = jnp.max(sc, axis=1, keepdims=True)
            p = jnp.exp(sc - m)
            l = jnp.sum(p, axis=1, keepdims=True)
            of = _dot(p.astype(BF16), v2) / l
            o_ref[j * q:(j + 1) * q, :] = _unstack_heads(of).astype(BF16)
            lse_ref[j * q:(j + 1) * q, :] = _per_head_rows(m + jnp.log(l))

        if nf:
            @pl.when(last_step)
            def _():
                for cp in _gather_forward_copies(buf_refs, *sems):
                    cp.wait()

    cur, prev = _attn_block_specs(g, nb)
    return pl.pallas_call(
        body, name=f"attn_fwd_g{gi}", grid=(d, ns),
        out_shape=[jax.ShapeDtypeStruct((d, length, GROUP_WIDTH), BF16),
                   jax.ShapeDtypeStruct((d, length, GROUP_WIDTH), F32)]
        + [jax.ShapeDtypeStruct(a.shape, a.dtype) for a in forward],
        in_specs=[cur(0), prev(1), cur(1), prev(2), cur(2)] + [_ANY] * nf,
        out_specs=[cur(0), cur(0)] + [_ANY] * nf,
        input_output_aliases={5 + a: 2 + a for a in range(nf)},
        scratch_shapes=[pltpu.VMEM((2, HEADS_PER_GROUP * q, 2 * q), F32)] + (_dma_sems(3 * nf) if nf else []),
        compiler_params=_params(2))(qkv, qkv, qkv, qkv, qkv, *forward)


def _conv_branch(ab, ac, av, halo_u, w, b):
    u = ac * av
    sh1 = _shift_down(u, 1, halo_u)
    sh2 = _shift_down(u, 2, halo_u)
    cv = w[0:1] * sh2 + w[1:2] * sh1 + w[2:3] * u + b
    return ab * cv, cv, u, sh1, sh2


def _mix_fwd(x, abcv, gates, o_list, lse_list, conv_w, conv_b, b_gate, w_pa, w_pb, w_out):
    s = x.shape[0]
    tm = ROW_TILE

    def body(x_ref, abcv_ref, gates_ref, o0_ref, o1_ref, o2_ref, l0_ref, l1_ref, l2_ref,
             cw_ref, cb_ref, bg_ref, wpa_ref, wpb_ref, wout_ref,
             x1_ref, ya0_ref, yb0_ref, mrg_ref, ya_ref, yb_ref, lsetot_ref, halo_ref, scr):
        i = pl.program_id(0)

        @pl.when(i == 0)
        def _():
            halo_ref[...] = jnp.zeros_like(halo_ref)

        ab = abcv_ref[:, 0:CONV_WIDTH].astype(F32)
        ac = abcv_ref[:, CONV_WIDTH:2 * CONV_WIDTH].astype(F32)
        av = abcv_ref[:, 2 * CONV_WIDTH:3 * CONV_WIDTH].astype(F32)
        ya0, _, u, _, _ = _conv_branch(ab, ac, av, halo_ref[...], cw_ref[...], cb_ref[...])
        halo_ref[...] = u[tm - SUBLANES:tm]
        ya0 = ya0.astype(BF16)
        ya0_ref[...] = ya0
        ya = _dot(ya0, wpa_ref[...])

        o_refs, l_refs = (o0_ref, o1_ref, o2_ref), (l0_ref, l1_ref, l2_ref)
        lses = [_from_streams(l_refs[g], scr, DILATIONS[g], 0, GROUP_WIDTH) for g in range(3)]
        top = jnp.maximum(jnp.maximum(lses[0], lses[1]), lses[2])
        weights = [jnp.exp(lse - top) for lse in lses]
        total = weights[0] + weights[1] + weights[2]
        lsetot_ref[...] = top + jnp.log(total)
        inv_total = 1.0 / total
        yb = jnp.zeros((tm, D_MODEL), F32)
        for g in range(3):
            og = _from_streams(o_refs[g], scr, DILATIONS[g], 0, GROUP_WIDTH)
            yb0 = (weights[g] * inv_total * og).astype(BF16)
            yb0_ref[:, g * GROUP_WIDTH:(g + 1) * GROUP_WIDTH] = yb0
            yb = yb + _dot(yb0, wpb_ref[g * GROUP_WIDTH:(g + 1) * GROUP_WIDTH, :])

        sa = _sigmoid(gates_ref[:, 0:D_MODEL].astype(F32) + bg_ref[0:1, :])
        sb = _sigmoid(gates_ref[:, D_MODEL:2 * D_MODEL].astype(F32) + bg_ref[1:2, :])
        ya_ref[...] = ya.astype(BF16)
        yb_ref[...] = yb.astype(BF16)
        mrg = (sa * ya + sb * yb).astype(BF16)
        mrg_ref[...] = mrg
        x1_ref[...] = x_ref[...] + _dot(mrg, wout_ref[...])

    outs = [jax.ShapeDtypeStruct((s, D_MODEL), F32),
            jax.ShapeDtypeStruct((s, CONV_WIDTH), BF16),
            jax.ShapeDtypeStruct((s, ATTN_WIDTH), BF16),
            jax.ShapeDtypeStruct((s, D_MODEL), BF16),
            jax.ShapeDtypeStruct((s, D_MODEL), BF16),
            jax.ShapeDtypeStruct((s, D_MODEL), BF16),
            jax.ShapeDtypeStruct((s, GROUP_WIDTH), F32)]
    return pl.pallas_call(
        body, name="mix_fwd", grid=(s // tm,), out_shape=outs,
        in_specs=[_rows(tm, D_MODEL), _rows(tm, 3 * CONV_WIDTH), _rows(tm, 2 * D_MODEL)]
        + [_stream_block(tm, d, GROUP_WIDTH) for d in DILATIONS] * 2
        + [_resident((3, CONV_WIDTH)), _resident((1, CONV_WIDTH)), _resident((2, D_MODEL)),
           _resident((CONV_WIDTH, D_MODEL)), _resident((ATTN_WIDTH, D_MODEL)), _resident((D_MODEL, D_MODEL))],
        out_specs=[_rows(tm, D_MODEL), _rows(tm, CONV_WIDTH), _rows(tm, ATTN_WIDTH), _rows(tm, D_MODEL),
                   _rows(tm, D_MODEL), _rows(tm, D_MODEL), _rows(tm, GROUP_WIDTH)],
        scratch_shapes=[pltpu.VMEM((SUBLANES, CONV_WIDTH), F32),
                        pltpu.VMEM((GROUP_WIDTH // LANES * tm, LANES), F32)],
        compiler_params=_params(1))(x, abcv, gates, *o_list, *lse_list, conv_w, conv_b, b_gate, w_pa, w_pb, w_out)


FFN_CHUNK = 512
FFN_UP_ROW_TILE = 256


def _ffn_up_fwd(x1, g, w_up, conv_w, conv_b):
    s = x1.shape[0]
    n = w_up.shape[1]
    tm = FFN_UP_ROW_TILE

    def body(x_ref, g_ref, w_ref, cw_ref, cb_ref, h_ref, up0_ref, up_ref, halo_ref):
        @pl.when(pl.program_id(0) == 0)
        def _():
            halo_ref[...] = jnp.zeros_like(halo_ref)

        xv = x_ref[...]
        r = lax.rsqrt(jnp.mean(xv * xv, axis=-1, keepdims=True) + EPS)
        h = (xv * r * g_ref[...]).astype(BF16)
        h_ref[...] = h
        for lo, size in _col_chunks(n, FFN_CHUNK):
            cols = slice(lo, lo + size)
            y = _dot(h, w_ref[:, cols])
            up0_ref[:, cols] = y.astype(BF16)
            halo = halo_ref[:, cols]
            w = cw_ref[:, cols]
            up = w[0:1] * _shift_down(y, 2, halo) + w[1:2] * _shift_down(y, 1, halo) + w[2:3] * y + cb_ref[:, cols]
            up_ref[:, cols] = up.astype(BF16)
            halo_ref[:, cols] = y[tm - SUBLANES:tm]

    return pl.pallas_call(
        body, name="ffn_up_fwd", grid=(s // tm,),
        out_shape=[jax.ShapeDtypeStruct((s, D_MODEL), BF16), jax.ShapeDtypeStruct((s, n), BF16),
                   jax.ShapeDtypeStruct((s, n), BF16)],
        in_specs=[_rows(tm, D_MODEL), _resident((1, D_MODEL)), _resident((D_MODEL, n)), _resident((3, n)),
                  _resident((1, n))],
        out_specs=[_rows(tm, D_MODEL), _rows(tm, n), _rows(tm, n)],
        scratch_shapes=[pltpu.VMEM((SUBLANES, n), F32)],
        compiler_params=_params(1))(x1, g, w_up, conv_w, conv_b)


def _ffn_act_fwd(x1, up, target, w_down, g_final):
    s = x1.shape[0]
    tm = ROW_TILE

    def body(x1_ref, up_ref, tgt_ref, wd_ref, gf_ref, act_ref, dx2_ref, dx2b_ref, dgf_ref, loss_ref):
        @pl.when(pl.program_id(0) == 0)
        def _():
            dgf_ref[...] = jnp.zeros_like(dgf_ref)
            loss_ref[...] = jnp.zeros_like(loss_ref)

        acc = jnp.zeros((tm, D_MODEL), F32)
        for lo, size in _col_chunks(D_FF, FFN_CHUNK):
            gate = up_ref[:, lo:lo + size].astype(F32)
            val = up_ref[:, D_FF + lo:D_FF + lo + size].astype(F32)
            act = (_silu(gate) * val).astype(BF16)
            act_ref[:, lo:lo + size] = act
            acc = acc + _dot(act, wd_ref[lo:lo + size, :])

        x2 = x1_ref[...] + acc
        r = lax.rsqrt(jnp.mean(x2 * x2, axis=-1, keepdims=True) + EPS)
        xn = x2 * r
        gf = gf_ref[...]
        err = xn * gf - tgt_ref[...]
        loss_ref[...] += (0.5 / D_MODEL) * jnp.sum(err * err)
        dy = err * (1.0 / D_MODEL)
        dgf_ref[...] += _colsum(dy * xn)
        dxn = dy * gf
        dx2 = r * (dxn - xn * jnp.mean(dxn * xn, axis=-1, keepdims=True))
        dx2_ref[...] = dx2
        dx2b_ref[...] = dx2.astype(BF16)

    return pl.pallas_call(
        body, name="ffn_act_fwd", grid=(s // tm,),
        out_shape=[jax.ShapeDtypeStruct((s, D_FF), BF16), jax.ShapeDtypeStruct((s, D_MODEL), F32),
                   jax.ShapeDtypeStruct((s, D_MODEL), BF16),
                   jax.ShapeDtypeStruct((1, D_MODEL), F32), jax.ShapeDtypeStruct((1, LANES), F32)],
        in_specs=[_rows(tm, D_MODEL), _rows(tm, 2 * D_FF), _rows(tm, D_MODEL),
                  _resident((D_FF, D_MODEL)), _resident((1, D_MODEL))],
        out_specs=[_rows(tm, D_FF), _rows(tm, D_MODEL), _rows(tm, D_MODEL),
                   pl.BlockSpec((1, D_MODEL), lambda i: (0, 0)), pl.BlockSpec((1, LANES), lambda i: (0, 0))],
        compiler_params=_params(1))(x1, up, target, w_down, g_final)


def _ffn_act_bwd(dx2b, up, w_down):
    s = dx2b.shape[0]
    tm = ROW_TILE

    def body(dx2_ref, up_ref, wd_ref, dup_ref):
        dx2 = dx2_ref[...]
        for lo, size in _col_chunks(D_FF, FFN_CHUNK):
            gate = up_ref[:, lo:lo + size].astype(F32)
            val = up_ref[:, D_FF + lo:D_FF + lo + size].astype(F32)
            dact = _dot_nt(dx2, wd_ref[lo:lo + size, :])
            sg = _sigmoid(gate)
            dup_ref[:, lo:lo + size] = (dact * val * (sg * (1.0 + gate * (1.0 - sg)))).astype(BF16)
            dup_ref[:, D_FF + lo:D_FF + lo + size] = (dact * (gate * sg)).astype(BF16)

    return pl.pallas_call(
        body, name="ffn_act_bwd", grid=(s // tm,),
        out_shape=jax.ShapeDtypeStruct((s, 2 * D_FF), BF16),
        in_specs=[_rows(tm, D_MODEL), _rows(tm, 2 * D_FF), _resident((D_FF, D_MODEL))],
        out_specs=_rows(tm, 2 * D_FF),
        compiler_params=_params(1))(dx2b, up, w_down)


def _ffn_up_bwd(d_up, up0, w_up, conv_w, x1, g, dres, swap=()):
    s = x1.shape[0]
    n = w_up.shape[1]
    tm = FFN_UP_ROW_TILE
    nt = s // tm
    nw = len(swap)

    def body(*refs):
        dup_ref, up0_ref, w_ref, cw_ref, x_ref, g_ref, dres_ref = refs[:7]
        slab_refs = refs[7:7 + nw]
        dx_ref, dxb_ref, dg_ref, dup0_ref, small_ref = refs[7 + nw:12 + nw]
        swapped_refs = refs[12 + nw:12 + 2 * nw]
        next_ref = refs[12 + 2 * nw]
        sems = refs[13 + 2 * nw:]

        @pl.when(pl.program_id(0) == 0)
        def _():
            next_ref[...] = jnp.zeros_like(next_ref)
            small_ref[...] = jnp.zeros_like(small_ref)
            dg_ref[...] = jnp.zeros_like(dg_ref)
            for cp in _sibling_swap_copies(slab_refs, swapped_refs, *sems) if nw else ():
                cp.start()

        dh = jnp.zeros((tm, D_MODEL), F32)
        for lo, size in _col_chunks(n, FFN_CHUNK):
            cols = slice(lo, lo + size)
            dz = dup_ref[:, cols].astype(F32)
            x0 = up0_ref[:, cols].astype(F32)
            nxt = next_ref[:, cols]
            dz1 = _shift_up(dz, 1, nxt)
            dz2 = _shift_up(dz, 2, nxt)
            next_ref[:, cols] = dz[0:SUBLANES]
            small_ref[0:1, cols] += _colsum(dz2 * x0)
            small_ref[1:2, cols] += _colsum(dz1 * x0)
            small_ref[2:3, cols] += _colsum(dz * x0)
            small_ref[3:4, cols] += _colsum(dz)
            w = cw_ref[:, cols]
            dup0 = (w[2:3] * dz + w[1:2] * dz1 + w[0:1] * dz2).astype(BF16)
            dup0_ref[:, cols] = dup0
            dh = dh + _dot_nt(dup0, w_ref[:, cols])
        xv = x_ref[...]
        r = lax.rsqrt(jnp.mean(xv * xv, axis=-1, keepdims=True) + EPS)
        xn = xv * r
        dg_ref[...] += _colsum(dh * xn)
        dxn = dh * g_ref[...]
        dx = dres_ref[...] + r * (dxn - xn * jnp.mean(dxn * xn, axis=-1, keepdims=True))
        dx_ref[...] = dx
        dxb_ref[...] = dx.astype(BF16)

        if nw:
            @pl.when(pl.program_id(0) == nt - 1)
            def _():
                for cp in _sibling_swap_copies(slab_refs, swapped_refs, *sems):
                    cp.wait()

    rows = lambda width: pl.BlockSpec((tm, width), lambda i: (nt - 1 - i, 0))
    return pl.pallas_call(
        body, name="ffn_up_bwd", grid=(nt,),
        out_shape=[jax.ShapeDtypeStruct((s, D_MODEL), F32), jax.ShapeDtypeStruct((s, D_MODEL), BF16),
                   jax.ShapeDtypeStruct((1, D_MODEL), F32), jax.ShapeDtypeStruct((s, n), BF16),
                   jax.ShapeDtypeStruct((SUBLANES, n), F32)] + _swap_shapes(swap),
        in_specs=[rows(n), rows(n), _resident((D_MODEL, n)), _resident((3, n)), rows(D_MODEL),
                  _resident((1, D_MODEL)), rows(D_MODEL)] + [_ANY] * nw,
        out_specs=[rows(D_MODEL), rows(D_MODEL), pl.BlockSpec((1, D_MODEL), lambda i: (0, 0)), rows(n),
                   pl.BlockSpec((SUBLANES, n), lambda i: (0, 0))] + [_ANY] * nw,
        scratch_shapes=[pltpu.VMEM((SUBLANES, n), F32)] + (_dma_sems(nw) if nw else []),
        compiler_params=_params(1))(d_up, up0, w_up, conv_w, x1, g, dres, *swap)


def _inproj_bwd(d_abcv, d_gates, d_qkvs, w_in_t, x, g, dres):
    s = x.shape[0]
    tm = ROW_TILE
    gw = GROUP_WIDTH

    def body(dabcv_ref, dgates_ref, dq0_ref, dq1_ref, dq2_ref, w_ref, x_ref, g_ref, dres_ref, dx_ref, dg_ref, scr):
        @pl.when(pl.program_id(0) == 0)
        def _():
            dg_ref[...] = jnp.zeros_like(dg_ref)

        dh = jnp.zeros((tm, D_MODEL), F32)
        for src, width, wrow in ((dabcv_ref, 3 * CONV_WIDTH, COL_ABCV), (dgates_ref, 2 * D_MODEL, COL_GATES)):
            for lo, size in _col_chunks(width, 512):
                dh = dh + _dot(src[:, lo:lo + size], w_ref[wrow + lo:wrow + lo + size, :])
        for gi, (d, dq_ref) in enumerate(zip(DILATIONS, (dq0_ref, dq1_ref, dq2_ref))):
            for j, base in enumerate((COL_Q, COL_K, COL_V)):
                dy = _from_streams(dq_ref, scr, d, j * gw, gw).astype(BF16)
                wrow = base + gi * gw
                dh = dh + _dot(dy, w_ref[wrow:wrow + gw, :])
        xv = x_ref[...]
        r = lax.rsqrt(jnp.mean(xv * xv, axis=-1, keepdims=True) + EPS)
        xn = xv * r
        dg_ref[...] += _colsum(dh * xn)
        dxn = dh * g_ref[...]
        dx_ref[...] = dres_ref[...] + r * (dxn - xn * jnp.mean(dxn * xn, axis=-1, keepdims=True))

    return pl.pallas_call(
        body, name="inproj_bwd", grid=(s // tm,),
        out_shape=[jax.ShapeDtypeStruct((s, D_MODEL), F32), jax.ShapeDtypeStruct((1, D_MODEL), F32)],
        in_specs=[_rows(tm, 3 * CONV_WIDTH), _rows(tm, 2 * D_MODEL)]
        + [_stream_block(tm, d, 3 * gw) for d in DILATIONS]
        + [_resident((D_IN, D_MODEL)), _rows(tm, D_MODEL), _resident((1, D_MODEL)), _rows(tm, D_MODEL)],
        out_specs=[_rows(tm, D_MODEL), pl.BlockSpec((1, D_MODEL), lambda i: (0, 0))],
        scratch_shapes=[pltpu.VMEM((gw // LANES * tm, LANES), F32)],
        compiler_params=_params(1))(d_abcv, d_gates, *d_qkvs, w_in_t, x, g, dres)


def _mix_bwd(dx1, abcv, gates, ya, yb, yb0, lsetot, conv_w, conv_b, b_gate, w_pa, w_pb, w_out, exchange=()):
    s = dx1.shape[0]
    tm = ROW_TILE
    nt = s // tm
    hb = tm // (2 * SUBLANES)
    nx = len(exchange)

    def body(*refs):
        (dx1_ref, abcv_ref, pre_ref, gates_ref, ya_ref, yb_ref, yb0_ref, lsetot_ref,
         cw_ref, cb_ref, bg_ref, wpa_ref, wpb_ref, wout_ref) = refs[:14]
        part_refs = refs[14:14 + nx]
        (dya_ref, dyb_ref, dgates_ref, dabcv_ref, dyb0_ref, dyl0_ref, dyl1_ref, dyl2_ref, aux0_ref, aux1_ref,
         aux2_ref, sm_gate_ref, sm_conv_ref) = refs[14 + nx:27 + nx]
        recv_refs = refs[27 + nx:27 + 2 * nx]
        next_ref, scr = refs[27 + 2 * nx:29 + 2 * nx]
        sems = refs[29 + 2 * nx:]
        i = pl.program_id(0)

        @pl.when(i == 0)
        def _():
            next_ref[...] = jnp.zeros_like(next_ref)
            sm_gate_ref[...] = jnp.zeros_like(sm_gate_ref)
            sm_conv_ref[...] = jnp.zeros_like(sm_conv_ref)
            for cp in _chip_exchange_copies(part_refs, recv_refs, *sems) if nx else ():
                cp.start()

        not_first = (i < nt - 1).astype(F32)
        dm = _dot_nt(dx1_ref[...].astype(BF16), wout_ref[...])
        sa = _sigmoid(gates_ref[:, 0:D_MODEL].astype(F32) + bg_ref[0:1, :])
        sb = _sigmoid(gates_ref[:, D_MODEL:2 * D_MODEL].astype(F32) + bg_ref[1:2, :])
        dya = (dm * sa).astype(BF16)
        dyb = (dm * sb).astype(BF16)
        dya_ref[...] = dya
        dyb_ref[...] = dyb
        dga = dm * ya_ref[...].astype(F32) * (sa * (1.0 - sa))
        dgb = dm * yb_ref[...].astype(F32) * (sb * (1.0 - sb))
        dgates_ref[:, 0:D_MODEL] = dga.astype(BF16)
        dgates_ref[:, D_MODEL:2 * D_MODEL] = dgb.astype(BF16)
        sm_gate_ref[0:1, :] += _colsum(dga)
        sm_gate_ref[1:2, :] += _colsum(dgb)

        dya0 = _dot_nt(dya, wpa_ref[...])
        ab = abcv_ref[:, 0:CONV_WIDTH].astype(F32)
        ac = abcv_ref[:, CONV_WIDTH:2 * CONV_WIDTH].astype(F32)
        av = abcv_ref[:, 2 * CONV_WIDTH:3 * CONV_WIDTH].astype(F32)
        pre = pre_ref[...].astype(F32) * not_first
        halo_u = (pre[:, CONV_WIDTH:2 * CONV_WIDTH] * pre[:, 2 * CONV_WIDTH:3 * CONV_WIDTH])[SUBLANES:2 * SUBLANES]
        w = cw_ref[...]
        _, cv, u, sh1, sh2 = _conv_branch(ab, ac, av, halo_u, w, cb_ref[...])
        dcv = dya0 * ab
        sm_conv_ref[0:1, :] += _colsum(dcv * sh2)
        sm_conv_ref[1:2, :] += _colsum(dcv * sh1)
        sm_conv_ref[2:3, :] += _colsum(dcv * u)
        sm_conv_ref[3:4, :] += _colsum(dcv)
        nxt = next_ref[...]
        du = w[2:3] * dcv + w[1:2] * _shift_up(dcv, 1, nxt) + w[0:1] * _shift_up(dcv, 2, nxt)
        next_ref[...] = dcv[0:SUBLANES]
        dabcv_ref[:, 0:CONV_WIDTH] = (dya0 * cv).astype(BF16)
        dabcv_ref[:, CONV_WIDTH:2 * CONV_WIDTH] = (du * av).astype(BF16)
        dabcv_ref[:, 2 * CONV_WIDTH:3 * CONV_WIDTH] = (du * ac).astype(BF16)

        head_r = lax.broadcasted_iota(jnp.int32, (GROUP_WIDTH, GROUP_WIDTH), 0) // HEAD_DIM
        head_c = lax.broadcasted_iota(jnp.int32, (GROUP_WIDTH, GROUP_WIDTH), 1) // HEAD_DIM
        same_head = (head_r == head_c).astype(BF16)
        prod = jnp.zeros((tm, GROUP_WIDTH), F32)
        dyb0s = []
        for g in range(3):
            cols = slice(g * GROUP_WIDTH, (g + 1) * GROUP_WIDTH)
            dyb0 = _dot_nt(dyb, wpb_ref[cols, :])
            dyb0_ref[:, cols] = dyb0.astype(BF16)
            dyb0s.append(dyb0)
            prod = prod + dyb0 * yb0_ref[:, cols].astype(F32)
        hi = prod.astype(BF16)
        mid = (prod - hi.astype(F32)).astype(BF16)
        lo = (prod - hi.astype(F32) - mid.astype(F32)).astype(BF16)
        delta = _dot(hi, same_head) + _dot(mid, same_head) + _dot(lo, same_head)
        lse_c = _compact_heads(lsetot_ref[...])
        delta_c = _compact_heads(delta)
        for g, (dy_ref, aux_ref) in enumerate(zip((dyl0_ref, dyl1_ref, dyl2_ref), (aux0_ref, aux1_ref, aux2_ref))):
            d = DILATIONS[g]
            _to_streams(dyb0s[g], scr, dy_ref, d, 0)
            _to_streams(lse_c, scr, aux_ref, d, 0)
            _to_streams(delta_c, scr, aux_ref, d, LANES)

        if nx:
            @pl.when(i == nt - 1)
            def _():
                for cp in _chip_exchange_copies(part_refs, recv_refs, *sems):
                    cp.wait()

    rev = lambda i: (nt - 1 - i, 0)
    pre = lambda i: (jnp.maximum((nt - 1 - i) * hb - 1, 0), 0)
    rows = lambda width: pl.BlockSpec((tm, width), rev)
    outs = [jax.ShapeDtypeStruct((s, D_MODEL), BF16), jax.ShapeDtypeStruct((s, D_MODEL), BF16),
            jax.ShapeDtypeStruct((s, 2 * D_MODEL), BF16), jax.ShapeDtypeStruct((s, 3 * CONV_WIDTH), BF16),
            jax.ShapeDtypeStruct((s, ATTN_WIDTH), BF16)]
    outs += [jax.ShapeDtypeStruct((d, s // d, GROUP_WIDTH), BF16) for d in DILATIONS]
    outs += [jax.ShapeDtypeStruct((d, s // d, 2 * LANES), F32) for d in DILATIONS]
    outs += [jax.ShapeDtypeStruct((SUBLANES, D_MODEL), F32), jax.ShapeDtypeStruct((SUBLANES, CONV_WIDTH), F32)]
    outs += [jax.ShapeDtypeStruct((3,) + a.shape[1:], a.dtype) for a in exchange]
    return pl.pallas_call(
        body, name="mix_bwd", grid=(nt,), out_shape=outs,
        in_specs=[rows(D_MODEL), rows(3 * CONV_WIDTH), pl.BlockSpec((2 * SUBLANES, 3 * CONV_WIDTH), pre),
                  rows(2 * D_MODEL), rows(D_MODEL), rows(D_MODEL), rows(ATTN_WIDTH), rows(GROUP_WIDTH),
                  _resident((3, CONV_WIDTH)), _resident((1, CONV_WIDTH)), _resident((2, D_MODEL)),
                  _resident((CONV_WIDTH, D_MODEL)), _resident((ATTN_WIDTH, D_MODEL)), _resident((D_MODEL, D_MODEL))]
        + [_ANY] * nx,
        out_specs=[rows(D_MODEL), rows(D_MODEL), rows(2 * D_MODEL), rows(3 * CONV_WIDTH), rows(ATTN_WIDTH)]
        + [_rev_stream_block(tm, d, GROUP_WIDTH, nt) for d in DILATIONS]
        + [_rev_stream_block(tm, d, 2 * LANES, nt) for d in DILATIONS]
        + [pl.BlockSpec((SUBLANES, D_MODEL), lambda i: (0, 0)), pl.BlockSpec((SUBLANES, CONV_WIDTH), lambda i: (0, 0))]
        + [_ANY] * nx,
        scratch_shapes=[pltpu.VMEM((SUBLANES, CONV_WIDTH), F32),
                        pltpu.VMEM((GROUP_WIDTH // LANES * tm, LANES), F32)] + (_dma_sems(3 * nx) if nx else []),
        compiler_params=_params(1))(dx1, abcv, abcv, gates, ya, yb, yb0, lsetot,
                                    conv_w, conv_b, b_gate, w_pa, w_pb, w_out, *exchange)


def _attn_bwd(qkv, dy, aux, gi, exchange=(), swap=()):
    d, length, _ = qkv.shape
    nb = length // ATTN_BLOCK
    q = ATTN_BLOCK
    gw = GROUP_WIDTH
    g = min(ATTN_BLOCKS_PER_STEP, nb)
    assert g >= 2 and nb % g == 0
    ns = nb // g
    lag = 1 if ns > 1 else 0
    tail = (g - 1) * q
    nx, nw = len(exchange), len(swap)

    def body(*refs):
        q_ref, kp_ref, kc_ref, vp_ref, vc_ref, dy_ref, aux_ref = refs[:7]
        part_refs = refs[7:7 + nx]
        slab_refs = refs[7 + nx:7 + nx + nw]
        pos = 7 + nx + nw
        out_ref = refs[pos]
        recv_refs = refs[pos + 1:pos + 1 + nx]
        swapped_refs = refs[pos + 1 + nx:pos + 1 + nx + nw]
        pos += 1 + nx + nw
        dq_ref, dkv_ref, bias_ref = refs[pos:pos + 3]
        sems = refs[pos + 3:]
        n = pl.program_id(1)

        def copies():
            cps = _chip_exchange_copies(part_refs, recv_refs, sems[0], sems[1]) if nx else []
            return cps + (_sibling_swap_copies(slab_refs, swapped_refs, sems[-2], sems[-1]) if nw else [])

        @pl.when((pl.program_id(0) == 0) & (n == 0))
        def _():
            _store_band_biases(bias_ref)
            for cp in copies():
                cp.start()

        if nx or nw:
            @pl.when((pl.program_id(0) == d - 1) & (n == ns - 1 + lag))
            def _():
                for cp in copies():
                    cp.wait()

        def emit(rows):
            out_ref[rows, gw:2 * gw] = dkv_ref[0, rows].astype(BF16)
            out_ref[rows, 2 * gw:3 * gw] = dkv_ref[1, rows].astype(BF16)

        if lag:
            @pl.when(n > 0)
            def _():
                out_ref[:, 0:gw] = dq_ref[...].astype(BF16)
                emit(slice(0, tail))

            @pl.when(n == ns)
            def _():
                emit(slice(tail, g * q))

        @pl.when(n < ns)
        def _():
            kfull = jnp.concatenate([kp_ref[...], kc_ref[...]], axis=0)
            vfull = jnp.concatenate([vp_ref[...], vc_ref[...]], axis=0)
            for j in range(g):
                rows = slice(j * q, (j + 1) * q)
                qs = _stack_heads(q_ref[rows, :])
                dys = _stack_heads(dy_ref[rows, :])
                k2 = kfull[j * q:(j + 2) * q]
                v2 = vfull[j * q:(j + 2) * q]
                lse = _compact_head_col(aux_ref[rows, 0:LANES])
                delta = _compact_head_col(aux_ref[rows, LANES:2 * LANES])
                bias = jnp.where(n == 0, bias_ref[1], bias_ref[0]) if j == 0 else bias_ref[0]
                p = jnp.exp(_dot_nt(qs, k2) + bias - lse)
                dp = _dot_nt(dys, v2)
                ds = (p * (dp - delta)).astype(BF16)
                dq_j = _unstack_heads(_dot(ds, k2)) * ATTN_SCALE
                dk2 = _dot_tn(ds, qs)
                dv2 = _dot_tn(p.astype(BF16), dys)
                if j == 0:
                    @pl.when(n > 0)
                    def _():
                        out_ref[tail:g * q, gw:2 * gw] = (dkv_ref[0, tail:g * q] + dk2[0:q]).astype(BF16)
                        out_ref[tail:g * q, 2 * gw:3 * gw] = (dkv_ref[1, tail:g * q] + dv2[0:q]).astype(BF16)
                else:
                    dkv_ref[0, (j - 1) * q:j * q] += dk2[0:q]
                    dkv_ref[1, (j - 1) * q:j * q] += dv2[0:q]
                dkv_ref[0, rows] = dk2[q:2 * q]
                dkv_ref[1, rows] = dv2[q:2 * q]
                dq_ref[rows, :] = dq_j
            if not lag:
                out_ref[:, 0:gw] = dq_ref[...].astype(BF16)
                emit(slice(0, g * q))

    cur, prev = _attn_block_specs(g, nb, clamp_last=True)
    return pl.pallas_call(
        body, name=f"attn_bwd_g{gi}", grid=(d, ns + lag),
        out_shape=[jax.ShapeDtypeStruct((d, length, 3 * gw), BF16)]
        + [jax.ShapeDtypeStruct((3,) + a.shape[1:], a.dtype) for a in exchange] + _swap_shapes(swap),
        in_specs=[cur(0), prev(1), cur(1), prev(2), cur(2), cur(0), cur(0, 2 * LANES)] + [_ANY] * (nx + nw),
        out_specs=[pl.BlockSpec((None, g * q, 3 * gw), lambda r, n: (r, jnp.maximum(n - lag, 0), 0))]
        + [_ANY] * (nx + nw),
        scratch_shapes=[pltpu.VMEM((g * q, gw), F32), pltpu.VMEM((2, g * q, gw), F32),
                        pltpu.VMEM((2, HEADS_PER_GROUP * q, 2 * q), F32)]
        + (_dma_sems(3 * nx) if nx else []) + (_dma_sems(nw) if nw else []),
        compiler_params=_params(2))(qkv, qkv, qkv, qkv, qkv, dy, aux, *exchange, *swap)


def _matmul_tn(name, a, b, col_tile=1024, row_tile=2048, slabs=0, swap=()):
    s, k = a.shape
    n = b.shape[1]
    tk = min(row_tile, s)
    tn = col_tile
    steps = s // tk
    nw = len(swap)

    def body(*refs):
        a_ref, b_ref = refs[:2]
        slab_refs = refs[2:2 + nw]
        o_ref = refs[2 + nw]
        swapped_refs = refs[3 + nw:3 + 2 * nw]
        acc_ref = refs[3 + 2 * nw]
        sems = refs[4 + 2 * nw:]
        t = pl.program_id(1)

        if nw:
            @pl.when((pl.program_id(0) == 0) & (t == 0))
            def _():
                for cp in _sibling_swap_copies(slab_refs, swapped_refs, *sems):
                    cp.start()

            @pl.when((pl.program_id(0) == n // tn - 1) & (t == steps - 1))
            def _():
                for cp in _sibling_swap_copies(slab_refs, swapped_refs, *sems):
                    cp.wait()

        @pl.when(t == 0)
        def _():
            acc_ref[...] = jnp.zeros_like(acc_ref)

        acc_ref[...] += _dot_tn(a_ref[...], b_ref[...])

        @pl.when(t == steps - 1)
        def _():
            if slabs:
                for q in range(per_tile):
                    o_ref[q] = acc_ref[:, q * width:(q + 1) * width].astype(BF16)
            else:
                o_ref[...] = acc_ref[...].astype(BF16)

    if slabs:
        width = n // slabs
        per_tile = tn // width
        out_shape = jax.ShapeDtypeStruct((slabs, k, width), BF16)
        out_spec = pl.BlockSpec((per_tile, k, width), lambda j, t: (j, 0, 0))
    else:
        out_shape = jax.ShapeDtypeStruct((k, n), BF16)
        out_spec = pl.BlockSpec((k, tn), lambda j, t: (0, j))
    res = pl.pallas_call(
        body, name=name, grid=(n // tn, steps), out_shape=[out_shape] + _swap_shapes(swap),
        in_specs=[pl.BlockSpec((tk, k), lambda j, t: (t, 0)), pl.BlockSpec((tk, tn), lambda j, t: (t, j))] + [_ANY] * nw,
        out_specs=[out_spec] + [_ANY] * nw,
        scratch_shapes=[pltpu.VMEM((k, tn), F32)] + (_dma_sems(nw) if nw else []),
        compiler_params=_params(2))(a, b, *swap)
    return res if nw else res[0]


def _sibling_swap_halves(name, slabs):
    na = len(slabs)

    def body(*refs):
        src_refs, out_refs = refs[:na], refs[na:2 * na]
        send_sems, recv_sems = refs[2 * na:]
        x, y, c, _ = _mesh_position()
        cps = []
        for a in range(na):
            theirs = _half_rows(1 - c, src_refs[a].shape[1] // 2)
            cps.append(pltpu.make_async_remote_copy(
                src_ref=src_refs[a].at[:, theirs, :], dst_ref=out_refs[a], send_sem=send_sems.at[a],
                recv_sem=recv_sems.at[a], device_id=(x, y, 1 - c), device_id_type=MESH_ID))
        for cp in cps:
            cp.start()
        for cp in cps:
            cp.wait()

    return pl.pallas_call(
        body, name=name,
        out_shape=[jax.ShapeDtypeStruct((a.shape[0], a.shape[1] // 2, a.shape[2]), a.dtype) for a in slabs],
        in_specs=[_ANY] * na, out_specs=[_ANY] * na,
        scratch_shapes=[pltpu.SemaphoreType.DMA((na,)), pltpu.SemaphoreType.DMA((na,))])(*slabs)


_HBM = pl.BlockSpec(memory_space=pltpu.HBM)
_SEM = pl.BlockSpec(memory_space=pltpu.SEMAPHORE)
_DATAFLOW = pltpu.SideEffectType.DATAFLOW_SIDE_EFFECTING


def _gather_start(shard):
    gathered = jax.ShapeDtypeStruct((N_CHIPS,) + shard.shape[1:], shard.dtype)

    def body(src_ref, buf_ref, send_sems, recv_sems, src_thru, buf_thru, token):
        for cp in _gather_first_copies([src_ref], [], [buf_ref], [], send_sems, recv_sems):
            cp.start()
        token[...] = jnp.zeros_like(token)

    return pl.pallas_call(
        body, name="gather_start",
        out_shape=(pltpu.SemaphoreType.DMA((3,)), pltpu.SemaphoreType.DMA((3,)),
                   pltpu.HBM(shard.shape, shard.dtype), pltpu.HBM(gathered.shape, gathered.dtype),
                   jax.ShapeDtypeStruct((SUBLANES, LANES), F32)),
        in_specs=(_HBM, _HBM), out_specs=(_SEM, _SEM, _HBM, _HBM, _VMEM), input_output_aliases={0: 2, 1: 3},
        compiler_params=pltpu.CompilerParams(has_side_effects=_DATAFLOW),
    )(pltpu.with_memory_space_constraint(shard, pltpu.HBM),
      pltpu.with_memory_space_constraint(lax.empty(gathered.shape, gathered.dtype), pltpu.HBM))


def _gather_forward(send_sems, recv_sems, shard_thru, buf_thru, after):
    def body(src_ref, buf_ref, send_sems, recv_sems, after_ref, fwd_send, fwd_recv, buf_out):
        for cp in _gather_first_copies([src_ref], [], [buf_ref], [], send_sems, recv_sems):
            cp.wait_send()
            cp.wait_recv()
        for cp in _gather_forward_copies([buf_ref], fwd_send, fwd_recv):
            cp.start()

    return pl.pallas_call(
        body, name="gather_forward",
        out_shape=(pltpu.SemaphoreType.DMA((3,)), pltpu.SemaphoreType.DMA((3,)),
                   pltpu.HBM(buf_thru.shape, buf_thru.dtype)),
        in_specs=(_HBM, _HBM, _SEM, _SEM, _ANY), out_specs=(_SEM, _SEM, _HBM), input_output_aliases={1: 2},
        compiler_params=pltpu.CompilerParams(has_side_effects=_DATAFLOW),
    )(shard_thru, buf_thru, send_sems, recv_sems, after)


def _gather_wait(fwd_send, fwd_recv, buf_thru):
    def body(buf_ref, fwd_send, fwd_recv, buf_out):
        for cp in _gather_forward_copies([buf_ref], fwd_send, fwd_recv):
            cp.wait_send()
            cp.wait_recv()

    return pl.pallas_call(
        body, name="gather_wait", out_shape=pltpu.HBM(buf_thru.shape, buf_thru.dtype),
        in_specs=(_HBM, _SEM, _SEM), out_specs=_HBM, input_output_aliases={0: 0},
        compiler_params=pltpu.CompilerParams(has_side_effects=_DATAFLOW),
    )(buf_thru, fwd_send, fwd_recv)


def _chip_exchange_start(partial):
    _, rows, cols = partial.shape
    landing = jax.ShapeDtypeStruct((3, rows, cols), partial.dtype)

    def body(src_ref, land_ref, send_sems, recv_sems, src_thru, land_thru, token):
        for cp in _chip_exchange_copies([src_ref], [land_ref], send_sems, recv_sems):
            cp.start()
        token[...] = jnp.zeros_like(token)

    return pl.pallas_call(
        body, name="grad_exchange_start",
        out_shape=(pltpu.SemaphoreType.DMA((3,)), pltpu.SemaphoreType.DMA((3,)),
                   pltpu.HBM(partial.shape, partial.dtype), pltpu.HBM(landing.shape, landing.dtype),
                   jax.ShapeDtypeStruct((SUBLANES, LANES), F32)),
        in_specs=(_HBM, _HBM), out_specs=(_SEM, _SEM, _HBM, _HBM, _VMEM), input_output_aliases={0: 2, 1: 3},
        compiler_params=pltpu.CompilerParams(has_side_effects=_DATAFLOW),
    )(pltpu.with_memory_space_constraint(partial, pltpu.HBM),
      pltpu.with_memory_space_constraint(lax.empty(landing.shape, landing.dtype), pltpu.HBM))


def _chip_exchange_wait(send_sems, recv_sems, src_thru, land_thru, after):
    def body(src_ref, land_ref, send_sems, recv_sems, after_ref, src_out, land_out):
        for cp in _chip_exchange_copies([src_ref], [land_ref], send_sems, recv_sems):
            cp.wait_send()
            cp.wait_recv()

    return pl.pallas_call(
        body, name="grad_exchange_wait",
        out_shape=(pltpu.HBM(src_thru.shape, src_thru.dtype), pltpu.HBM(land_thru.shape, land_thru.dtype)),
        in_specs=(_HBM, _HBM, _SEM, _SEM, _ANY), out_specs=(_HBM, _HBM), input_output_aliases={0: 0, 1: 1},
        compiler_params=pltpu.CompilerParams(has_side_effects=_DATAFLOW),
    )(src_thru, land_thru, send_sems, recv_sems, after)


def _sibling_share(halves):
    na = len(halves)

    def body(*refs):
        out_refs = refs[na:2 * na]
        send_sems, recv_sems = refs[2 * na:]
        x, y, c, _ = _mesh_position()
        cps = []
        for a in range(na):
            mine = out_refs[a].at[0, _half_rows(c, out_refs[a].shape[1] // 2)]
            cps.append(pltpu.make_async_remote_copy(src_ref=mine, dst_ref=mine, send_sem=send_sems.at[a],
                                                    recv_sem=recv_sems.at[a], device_id=(x, y, 1 - c),
                                                    device_id_type=MESH_ID))
        for cp in cps:
            cp.start()
        for a, cp in enumerate(cps):
            cp.wait_send()
            theirs = out_refs[a].at[0, _half_rows(1 - c, out_refs[a].shape[1] // 2)]
            pltpu.make_async_remote_copy(src_ref=theirs, dst_ref=theirs, send_sem=send_sems.at[a],
                                         recv_sem=recv_sems.at[a], device_id=(x, y, 1 - c),
                                         device_id_type=MESH_ID).wait_recv()

    return pl.pallas_call(
        body, name="grad_sibling_share", out_shape=[jax.ShapeDtypeStruct(a.shape, a.dtype) for a in halves],
        in_specs=[_ANY] * na, out_specs=[_ANY] * na, input_output_aliases={a: a for a in range(na)},
        scratch_shapes=[pltpu.SemaphoreType.DMA((na,)), pltpu.SemaphoreType.DMA((na,))])(*halves)


def _add_sibling(name, slab, received, core):
    n, rows, cols = slab.shape
    half = rows // 2

    def body(core_ref, a_ref, b_ref, o_ref):
        o_ref[...] = (a_ref[...].astype(F32) + b_ref[...].astype(F32)).astype(BF16)

    grid_spec = pltpu.PrefetchScalarGridSpec(
        num_scalar_prefetch=1, grid=(n,),
        in_specs=[pl.BlockSpec((None, half, cols), lambda s, core_ref: (s, core_ref[0], 0)),
                  pl.BlockSpec((None, half, cols), lambda s, core_ref: (s, 0, 0))],
        out_specs=pl.BlockSpec((None, half, cols), lambda s, core_ref: (s, 0, 0)))
    return pl.pallas_call(body, name=name, grid_spec=grid_spec,
                          out_shape=jax.ShapeDtypeStruct((n, half, cols), BF16),
                          compiler_params=_params(1))(core, slab, received)


def _sum_chips(name, partial, received, chip_core):
    _, half, cols = partial.shape

    def body(cc_ref, own_ref, recv_ref, o_ref):
        acc = own_ref[...].astype(F32)
        for k in range(3):
            acc = acc + recv_ref[k].astype(F32)
        o_ref[...] = acc

    grid_spec = pltpu.PrefetchScalarGridSpec(
        num_scalar_prefetch=1, grid=(1,),
        in_specs=[pl.BlockSpec((None, half, cols), lambda i, cc_ref: (cc_ref[0], 0, 0)),
                  pl.BlockSpec((3, half, cols), lambda i, cc_ref: (0, 0, 0))],
        out_specs=pl.BlockSpec((None, half, cols), lambda i, cc_ref: (0, cc_ref[1], 0)))
    return pl.pallas_call(body, name=name, grid_spec=grid_spec,
                          out_shape=jax.ShapeDtypeStruct((1, 2 * half, cols), F32),
                          compiler_params=_params(1))(chip_core, partial, received)


def _adam_math(w, g, m, v):
    nm = ADAM_B1 * m + (1.0 - ADAM_B1) * g
    nv = ADAM_B2 * v + (1.0 - ADAM_B2) * jnp.square(g)
    m_hat = nm / (1.0 - ADAM_B1 ** ADAM_STEP)
    v_hat = nv / (1.0 - ADAM_B2 ** ADAM_STEP)
    delta = -ADAM_LR * (m_hat / (jnp.sqrt(v_hat) + ADAM_EPS) + ADAM_WD * w)
    return delta, nm, nv


def _adamw(name, w, g, m, v):
    _, rows, cols = w.shape
    tr = next(t for t in (736, 512, 384, 352, 256, 128, 64, 32, 16, 8) if rows % t == 0)

    def body(w_ref, g_ref, m_ref, v_ref, d_ref, nm_ref, nv_ref):
        d_ref[...], nm_ref[...], nv_ref[...] = _adam_math(w_ref[...], g_ref[...], m_ref[...], v_ref[...])

    spec = pl.BlockSpec((None, tr, cols), lambda i: (0, i, 0))
    return pl.pallas_call(
        body, name=name, grid=(rows // tr,), out_shape=[jax.ShapeDtypeStruct(w.shape, F32)] * 3,
        in_specs=[spec] * 4, out_specs=[spec] * 3, compiler_params=_params(1))(w, g, m, v)


SMALL_PARAMS = ("norm_mix_g", "b_gate", "conv_a_w", "conv_a_b", "norm_ffn_g", "ffn_conv_w", "ffn_conv_b", "final_norm_g")


def _small_update(partials, params, moments_m, moments_v):
    na = len(partials)
    npar = len(SMALL_PARAMS)

    def body(*refs):
        in_refs = refs[:na]
        w_refs = refs[na:na + npar]
        m_refs = refs[na + npar:na + 2 * npar]
        v_refs = refs[na + 2 * npar:na + 3 * npar]
        pos = na + 3 * npar
        loss_ref = refs[pos]
        out_refs = refs[pos + 1:pos + 1 + 4 * npar]
        pos += 1 + 4 * npar
        acc_refs = refs[pos:pos + na]
        recv_refs = refs[pos + na:pos + 4 * na]
        send_sems, recv_sems = refs[pos + 4 * na:]
        x, y, c, _ = _mesh_position()
        chip = 2 * x + y
        for a in range(na):
            acc_refs[a][...] = in_refs[a][...]
        for stage, peer in enumerate(((x, y, 1 - c), (x, 1 - y, c), (1 - x, y, c))):
            cps = []
            for a in range(na):
                k = stage * na + a
                cps.append(pltpu.make_async_remote_copy(src_ref=acc_refs[a], dst_ref=recv_refs[k], send_sem=send_sems.at[k],
                                                        recv_sem=recv_sems.at[k], device_id=peer, device_id_type=MESH_ID))
            for cp in cps:
                cp.start()
            for cp in cps:
                cp.wait()
            for a in range(na):
                acc_refs[a][...] = acc_refs[a][...] + recv_refs[stage * na + a][...]

        mix, ffn, fin, gate, conv, ffnc, loss = acc_refs
        loss_ref[...] = loss[...]

        def cols(width):
            return pl.ds(pl.multiple_of(chip * width, LANES), width)

        grads = {
            "norm_mix_g": mix[...], "norm_ffn_g": ffn[...], "final_norm_g": fin[...],
            "b_gate": gate[0:2, cols(D_MODEL // N_CHIPS)],
            "conv_a_w": conv[0:3, cols(CONV_WIDTH // N_CHIPS)], "conv_a_b": conv[3:4, :],
            "ffn_conv_w": ffnc[0:3, cols(2 * D_FF // N_CHIPS)], "ffn_conv_b": ffnc[3:4, :]}
        for i, name in enumerate(SMALL_PARAMS):
            g = grads[name]
            if len(w_refs[i].shape) == 3:
                results = (g,) + _adam_math(w_refs[i][0], g, m_refs[i][0], v_refs[i][0])
                for o_ref, val in zip(out_refs[4 * i:4 * i + 4], results):
                    o_ref[0] = val
            else:
                results = (g,) + _adam_math(w_refs[i][...], g, m_refs[i][...], v_refs[i][...])
                for o_ref, val in zip(out_refs[4 * i:4 * i + 4], results):
                    o_ref[...] = val

    outs = [jax.ShapeDtypeStruct(partials[-1].shape, F32)]
    for w in params:
        outs += [jax.ShapeDtypeStruct(w.shape, F32)] * 4
    scratch = [pltpu.VMEM(p.shape, F32) for p in partials]
    scratch += [pltpu.VMEM(p.shape, F32) for _ in range(3) for p in partials]
    scratch += [pltpu.SemaphoreType.DMA((3 * na,)), pltpu.SemaphoreType.DMA((3 * na,))]
    n_in = na + 3 * npar
    return pl.pallas_call(
        body, name="small_update", out_shape=outs, in_specs=[_VMEM] * n_in, out_specs=[_VMEM] * len(outs),
        scratch_shapes=scratch)(*partials, *params, *moments_m, *moments_v)


def _gathered_columns(g):
    return jnp.transpose(g, (1, 0, 2)).reshape(g.shape[1], N_CHIPS * g.shape[2])


def _column_slabs(full):
    k, n = full.shape
    return jnp.transpose(full.reshape(k, N_CHIPS, n // N_CHIPS), (1, 0, 2))


def kernel(x, norm_mix_g, w_in, b_gate, conv_a_w, conv_a_b, w_proj_a, w_proj_b, w_out, norm_ffn_g, w_up, ffn_conv_w, ffn_conv_b, w_down, final_norm_g, loss_target, m_norm_mix_g, m_w_in, m_b_gate, m_conv_a_w, m_conv_a_b, m_w_proj_a, m_w_proj_b, m_w_out, m_norm_ffn_g, m_w_up, m_ffn_conv_w, m_ffn_conv_b, m_w_down, m_final_norm_g, v_norm_mix_g, v_w_in, v_b_gate, v_conv_a_w, v_conv_a_b, v_w_proj_a, v_w_proj_b, v_w_out, v_norm_ffn_g, v_w_up, v_ffn_conv_w, v_ffn_conv_b, v_w_down, v_final_norm_g):
    chip = (2 * lax.axis_index("x") + lax.axis_index("y")).astype(jnp.int32)
    core = lax.axis_index("c").astype(jnp.int32)
    core_arr = core.reshape(1)
    chip_core = jnp.stack([chip, core])
    xs, target = x[0], loss_target[0]
    g_final = final_norm_g.reshape(1, D_MODEL)

    def own_slot(gathered, own):
        return lax.dynamic_update_slice(gathered, own, (chip, 0, 0))

    w_in_t, m_w_in_t, v_w_in_t = (jnp.swapaxes(a, 1, 2) for a in (w_in, m_w_in, v_w_in))
    w_in_tb = w_in_t.astype(BF16)
    send1, recv1, shard_thru, g_in, token = _gather_start(w_in_tb)
    h1, h1_streams4, h1_streams16 = _norm_fwd(xs, norm_mix_g + token[0:1, 0:1])
    send2, recv2, g_in = _gather_forward(send1, recv1, shard_thru, g_in, h1)
    g_in = _gather_wait(send2, recv2, g_in)
    w_in_full_t = own_slot(g_in, w_in_tb).reshape(D_IN, D_MODEL)
    later_w = [w_proj_a, w_proj_b, w_out, w_up, w_down]
    later_b = [w.astype(BF16) for w in later_w]
    small_sharded = [b_gate, conv_a_w, ffn_conv_w]
    fwd = _inproj_fwd(h1, w_in_full_t, later_b, small_sharded)
    abcv, gates, qkv0, qkv1, qkv2 = fwd[:5]
    gathered_big, gathered_small = fwd[5:10], fwd[10:13]
    qkvs = (qkv0, qkv1, qkv2)
    attn0 = _attn_fwd(qkv0, 0, forward=gathered_big)
    attn = [attn0[:2], _attn_fwd(qkv1, 1), _attn_fwd(qkv2, 2)]
    g_pa, g_pb, g_out, g_up, g_down = [own_slot(g, own) for g, own in zip(attn0[2:], later_b)]
    g_bgate, g_convw, g_ffnw = [own_slot(g, own) for g, own in zip(gathered_small, small_sharded)]
    w_pa_full, w_pb_full, w_up_full = _gathered_columns(g_pa), _gathered_columns(g_pb), _gathered_columns(g_up)
    w_out_full, w_down_full = g_out.reshape(D_MODEL, D_MODEL), g_down.reshape(D_FF, D_MODEL)
    b_gate_full, conv_w_full, ffn_w_full = (_gathered_columns(g) for g in (g_bgate, g_convw, g_ffnw))

    x1, ya0, yb0, mrg, ya, yb, lsetot = _mix_fwd(
        xs, abcv, gates, [a[0] for a in attn], [a[1] for a in attn], conv_w_full, conv_a_b, b_gate_full,
        w_pa_full, w_pb_full, w_out_full)
    h2, up0, up = _ffn_up_fwd(x1, norm_ffn_g, w_up_full, ffn_w_full, ffn_conv_b)
    act, dx2, dx2b, d_g_final, loss = _ffn_act_fwd(x1, up, target, w_down_full, g_final)

    def add_sibling(names, slabs, swapped):
        return [_add_sibling("grad_add_" + n, s, r, core_arr) for n, s, r in zip(names, slabs, swapped)]

    def sum_chips(names, partials, received):
        return [_sum_chips("grad_sum_" + n, p, r, chip_core) for n, p, r in zip(names, partials, received)]

    d_up = _ffn_act_bwd(dx2b, up, w_down_full)
    slab_down = _matmul_tn("dw_down", act, dx2b, col_tile=512).reshape(N_CHIPS, D_FF // N_CHIPS, D_MODEL)
    dx1, dx1b, d_g_ffn, d_up0, ffn_small, swapped_down = _ffn_up_bwd(
        d_up, up0, w_up_full, ffn_w_full, x1, norm_ffn_g, dx2, swap=[slab_down])
    (partial_down,) = add_sibling(["w_down"], [slab_down], [swapped_down])
    slab_up = _matmul_tn("dw_up", h2, d_up0, col_tile=2 * D_FF // N_CHIPS, slabs=N_CHIPS)
    d_w_out, swapped_up = _matmul_tn("dw_out", mrg, dx1b, swap=[slab_up])
    (partial_up,) = add_sibling(["w_up"], [slab_up], [swapped_up])

    mix_res = _mix_bwd(dx1, abcv, gates, ya, yb, yb0, lsetot, conv_w_full, conv_a_b, b_gate_full,
                       w_pa_full, w_pb_full, w_out_full, exchange=[partial_up, partial_down])
    (d_ya, d_yb, d_gates, d_abcv, d_yb0, dyl0, dyl1, dyl2, aux0, aux1, aux2, gate_small, conv_small) = mix_res[:13]
    halves_ffn = sum_chips(["w_up", "w_down"], [partial_up, partial_down], mix_res[13:])

    mix_names = ["w_proj_a", "w_proj_b", "w_out"]
    slabs_mix = [_matmul_tn("dw_proj_a", ya0, d_ya, slabs=N_CHIPS), _matmul_tn("dw_proj_b", yb0, d_yb, slabs=N_CHIPS),
                 d_w_out.reshape(N_CHIPS, D_MODEL // N_CHIPS, D_MODEL)]
    res0 = _attn_bwd(qkv0, dyl0, aux0, 0, swap=slabs_mix)
    d_qkv0, partials_mix = res0[0], add_sibling(mix_names, slabs_mix, res0[1:])
    res1 = _attn_bwd(qkv1, dyl1, aux1, 1, exchange=partials_mix)
    d_qkv1, halves_mix = res1[0], sum_chips(mix_names, partials_mix, res1[1:])
    (d_qkv2,) = _attn_bwd(qkv2, dyl2, aux2, 2)

    dq = [d_qkv0, d_qkv1, d_qkv2]
    seq = xs.shape[0]
    d_w_abcv = _matmul_tn("dw_in_abcv", d_abcv, h1)
    d_w_gates = _matmul_tn("dw_in_gates", d_gates, h1)
    d_w_groups = [_matmul_tn(f"dw_in_qkv{g}", t.reshape(seq, 3 * GROUP_WIDTH), h.reshape(seq, D_MODEL))
                  for g, (t, h) in enumerate(zip(dq, (h1, h1_streams4, h1_streams16)))]
    gw = GROUP_WIDTH
    d_w_in_t = jnp.concatenate(
        [d_w_abcv] + [d_w_groups[g][j * gw:(j + 1) * gw] for j in range(3) for g in range(3)] + [d_w_gates], axis=0)

    slab_in = d_w_in_t.reshape(N_CHIPS, D_IN // N_CHIPS, D_MODEL)
    (from_sibling_in,) = _sibling_swap_halves("grad_swap_w_in", [slab_in])
    partial_in = _add_sibling("grad_add_w_in", slab_in, from_sibling_in, core_arr)
    send_sems, recv_sems, partial_thru, landing_thru, token = _chip_exchange_start(partial_in)
    g_mix_after_start = norm_mix_g + token[0:1, 0:1]
    grad_x, d_g_mix = _inproj_bwd(d_abcv, d_gates, dq, w_in_full_t, xs, g_mix_after_start, dx1)
    partial_in, received_in = _chip_exchange_wait(send_sems, recv_sems, partial_thru, landing_thru, d_g_mix)
    halves_in = sum_chips(["w_in"], [partial_in], [received_in])

    big_names = ("w_in", "w_proj_a", "w_proj_b", "w_out", "w_up", "w_down")
    big_grads = _sibling_share(halves_in + halves_mix + halves_ffn)
    big_w = dict(w_in=w_in_t, w_proj_a=w_proj_a, w_proj_b=w_proj_b, w_out=w_out, w_up=w_up, w_down=w_down)
    big_m = dict(w_in=m_w_in_t, w_proj_a=m_w_proj_a, w_proj_b=m_w_proj_b, w_out=m_w_out, w_up=m_w_up, w_down=m_w_down)
    big_v = dict(w_in=v_w_in_t, w_proj_a=v_w_proj_a, w_proj_b=v_w_proj_b, w_out=v_w_out, w_up=v_w_up, w_down=v_w_down)

    fin_w, fin_m, fin_v = (a.reshape(1, D_MODEL) for a in (final_norm_g, m_final_norm_g, v_final_norm_g))
    small_w = [norm_mix_g, b_gate, conv_a_w, conv_a_b, norm_ffn_g, ffn_conv_w, ffn_conv_b, fin_w]
    small_m = [m_norm_mix_g, m_b_gate, m_conv_a_w, m_conv_a_b, m_norm_ffn_g, m_ffn_conv_w, m_ffn_conv_b, fin_m]
    small_v = [v_norm_mix_g, v_b_gate, v_conv_a_w, v_conv_a_b, v_norm_ffn_g, v_ffn_conv_w, v_ffn_conv_b, fin_v]
    small_out = _small_update([d_g_mix, d_g_ffn, d_g_final, gate_small, conv_small, ffn_small, loss],
                              small_w, small_m, small_v)
    total_loss = small_out[0][0, 0]

    grads, delta, new_m, new_v = {}, {}, {}, {}
    for i, n in enumerate(SMALL_PARAMS):
        vals = small_out[1 + 4 * i:5 + 4 * i]
        if n == "final_norm_g":
            vals = [a.reshape(D_MODEL) for a in vals]
        grads[n], delta[n], new_m[n], new_v[n] = vals
    for n, g in zip(big_names, big_grads):
        vals = (g,) + tuple(_adamw("adamw_" + n, big_w[n], g, big_m[n], big_v[n]))
        if n == "w_in":
            vals = [jnp.swapaxes(a, 1, 2) for a in vals]
        grads[n], delta[n], new_m[n], new_v[n] = vals

    names = ["norm_mix_g", "w_in", "b_gate", "conv_a_w", "conv_a_b", "w_proj_a", "w_proj_b", "w_out", "norm_ffn_g", "w_up",
             "ffn_conv_w", "ffn_conv_b", "w_down", "final_norm_g"]
    out = [total_loss, grad_x[None]]
    for group in (grads, delta, new_m, new_v):
        out += [group[n] for n in names]
    return tuple(out)
```

```python
import jax
import jax.numpy as jnp
from jax import lax
from jax.experimental import pallas as pl
from jax.experimental.pallas import tpu as pltpu

F32 = jnp.float32
BF16 = jnp.bfloat16

D_MODEL = 1024
CONV_WIDTH = 512
ATTN_WIDTH = 768
GROUP_WIDTH = 256
HEAD_DIM = 64
HEADS_PER_GROUP = 4
DILATIONS = (1, 4, 16)
ATTN_BLOCK = 128
D_FF = 2816
D_IN = 5888
EPS = 1e-6
NEG_INF = -1e30
ATTN_SCALE = HEAD_DIM ** -0.5

COL_ABCV = 0
COL_Q = 1536
COL_K = 2304
COL_V = 3072
COL_GATES = 3840

ADAM_LR = 0.001
ADAM_B1 = 0.9
ADAM_B2 = 0.999
ADAM_EPS = 1e-08
ADAM_WD = 0.01
ADAM_STEP = 10

LANES = 128
SUBLANES = 8
BF16_ROWS = 16
ROW_TILE = 512
VMEM_LIMIT = 56 * 1024 * 1024

_NT = (((1,), (1,)), ((), ()))
_TN = (((0,), (0,)), ((), ()))


def _params(n_axes, vmem=VMEM_LIMIT):
    return pltpu.CompilerParams(dimension_semantics=("arbitrary",) * n_axes, vmem_limit_bytes=vmem)


def _resident(shape):
    nd = len(shape)
    return pl.BlockSpec(shape, lambda *_: (0,) * nd, pipeline_mode=pl.Buffered(1))


def _rows(tm, width, col_block=0):
    return pl.BlockSpec((tm, width), lambda i: (i, col_block))


def _col_chunks(n, cmax):
    out, lo = [], 0
    while lo < n:
        size = min(cmax, n - lo)
        out.append((lo, size))
        lo += size
    return out


def _dot(a, b):
    return jnp.dot(a, b, preferred_element_type=F32)


def _dot_nt(a, b):
    return lax.dot_general(a, b, _NT, preferred_element_type=F32)


def _dot_tn(a, b):
    return lax.dot_general(a, b, _TN, preferred_element_type=F32)


def _sigmoid(x):
    return 0.5 * jnp.tanh(0.5 * x) + 0.5


def _silu(x):
    hx = 0.5 * x
    return hx + hx * jnp.tanh(hx)


def _shift_down(v, k, halo8):
    tm = v.shape[0]
    rolled = pltpu.roll(v, k, 0)
    fix = jnp.tile(pltpu.roll(halo8, k, 0), (tm // SUBLANES, 1))
    row = lax.broadcasted_iota(jnp.int32, v.shape, 0)
    return jnp.where(row < k, fix, rolled)


def _shift_up(v, k, halo8):
    tm = v.shape[0]
    rolled = pltpu.roll(v, tm - k, 0)
    fix = jnp.tile(pltpu.roll(halo8, SUBLANES - k, 0), (tm // SUBLANES, 1))
    row = lax.broadcasted_iota(jnp.int32, v.shape, 0)
    return jnp.where(row >= tm - k, fix, rolled)


def _colsum(v):
    return jnp.sum(v, axis=0, keepdims=True)


def _to_streams(val, scr, out_ref, d, col0):
    tm = val.shape[0]
    panels = val.shape[1] // LANES
    if d == 1:
        out_ref[0, :, col0:col0 + val.shape[1]] = val.astype(out_ref.dtype)
        return
    for p in range(panels):
        scr[pl.ds(p * tm, tm), :] = val[:, p * LANES:(p + 1) * LANES]
    for r in range(d):
        for p in range(panels):
            piece = scr[pl.ds(p * tm + r, tm // d, stride=d), :]
            out_ref[r, :, col0 + p * LANES: col0 + (p + 1) * LANES] = piece.astype(out_ref.dtype)


def _from_streams(in_ref, scr, d, col0, width):
    panels = width // LANES
    rows = in_ref.shape[1]
    tm = rows * d
    if d == 1:
        return in_ref[0, :, col0:col0 + width].astype(F32)
    for r in range(d):
        for p in range(panels):
            scr[pl.ds(p * tm + r, rows, stride=d), :] = in_ref[r, :, col0 + p * LANES: col0 + (p + 1) * LANES].astype(F32)
    return jnp.concatenate([scr[pl.ds(p * tm, tm), :] for p in range(panels)], axis=1)


def _stream_block(tm, d, width):
    return pl.BlockSpec((d, tm // d, width), lambda i: (0, i, 0))


def _rev_stream_block(tm, d, width, nt):
    return pl.BlockSpec((d, tm // d, width), lambda i: (0, nt - 1 - i, 0))


N_CHIPS = 4
MESH_ID = pl.DeviceIdType.MESH
_ANY = pl.BlockSpec(memory_space=pl.ANY)
_VMEM = pl.BlockSpec(memory_space=pltpu.VMEM)


def _mesh_position():
    x, y, c = lax.axis_index("x"), lax.axis_index("y"), lax.axis_index("c")
    other_chips = [(1 - x, y), (x, 1 - y), (1 - x, 1 - y)]
    return x, y, c, other_chips


def _half_rows(c, half):
    return pl.ds(pl.multiple_of(c * half, BF16_ROWS), half)


def _remote_copy(k, src, dst, to, send_sems, recv_sems):
    return pltpu.make_async_remote_copy(src_ref=src, dst_ref=dst, send_sem=send_sems.at[k], recv_sem=recv_sems.at[k],
                                        device_id=to, device_id_type=MESH_ID)


def _gather_first_copies(big_refs, small_refs, big_outs, small_outs, send_sems, recv_sems):
    x, y, c, chips = _mesh_position()
    me = 2 * x + y
    nb = len(big_refs)
    cps = []
    for j, (px, py) in enumerate(chips):
        for b in range(nb):
            mine = _half_rows(c, big_refs[b].shape[1] // 2)
            cps.append(_remote_copy(3 * b + j, big_refs[b].at[0, mine], big_outs[b].at[me, mine], (px, py, c),
                                    send_sems, recv_sems))
        for s in range(len(small_refs)):
            cps.append(_remote_copy(3 * (nb + s) + j, small_refs[s].at[0], small_outs[s].at[me], (px, py, c),
                                    send_sems, recv_sems))
    return cps


def _gather_forward_copies(bufs, send_sems, recv_sems):
    x, y, c, chips = _mesh_position()
    cps = []
    for j, (px, py) in enumerate(chips):
        for b in range(len(bufs)):
            landed = bufs[b].at[2 * px + py, _half_rows(c, bufs[b].shape[1] // 2)]
            cps.append(_remote_copy(3 * b + j, landed, landed, (x, y, 1 - c), send_sems, recv_sems))
    return cps


def _chip_exchange_copies(src_refs, out_refs, send_sems, recv_sems):
    x, y, c, chips = _mesh_position()
    cps = []
    for j, (px, py) in enumerate(chips):
        for a in range(len(src_refs)):
            cps.append(_remote_copy(3 * a + j, src_refs[a].at[2 * px + py], out_refs[a].at[j], (px, py, c),
                                    send_sems, recv_sems))
    return cps


def _sibling_swap_copies(src_refs, out_refs, send_sems, recv_sems):
    x, y, c, _ = _mesh_position()
    cps = []
    for a in range(len(src_refs)):
        theirs = _half_rows(1 - c, src_refs[a].shape[1] // 2)
        cps.append(_remote_copy(a, src_refs[a].at[:, theirs, :], out_refs[a], (x, y, 1 - c), send_sems, recv_sems))
    return cps


def _swap_shapes(slabs):
    return [jax.ShapeDtypeStruct((a.shape[0], a.shape[1] // 2, a.shape[2]), a.dtype) for a in slabs]


def _dma_sems(n):
    return [pltpu.SemaphoreType.DMA((n,)), pltpu.SemaphoreType.DMA((n,))]


def _norm_fwd(x, g):
    s = x.shape[0]
    tm = ROW_TILE

    def body(x_ref, g_ref, h_ref, hs1_ref, hs2_ref, scr):
        xv = x_ref[...]
        r = lax.rsqrt(jnp.mean(xv * xv, axis=-1, keepdims=True) + EPS)
        hf = xv * r * g_ref[...]
        h_ref[...] = hf.astype(BF16)
        for d, hs_ref in zip(DILATIONS[1:], (hs1_ref, hs2_ref)):
            for lo, size in _col_chunks(D_MODEL, GROUP_WIDTH):
                _to_streams(hf[:, lo:lo + size], scr, hs_ref, d, lo)

    return pl.pallas_call(
        body, name="norm_fwd", grid=(s // tm,),
        out_shape=[jax.ShapeDtypeStruct((s, D_MODEL), BF16)]
        + [jax.ShapeDtypeStruct((d, s // d, D_MODEL), BF16) for d in DILATIONS[1:]],
        in_specs=[_rows(tm, D_MODEL), _resident((1, D_MODEL))],
        out_specs=[_rows(tm, D_MODEL)] + [_stream_block(tm, d, D_MODEL) for d in DILATIONS[1:]],
        scratch_shapes=[pltpu.VMEM((GROUP_WIDTH // LANES * tm, LANES), F32)],
        compiler_params=_params(1))(x, g)


def _inproj_fwd(h1, w_in_t, big_shards, small_shards):
    s = h1.shape[0]
    tm = ROW_TILE
    nt = s // tm
    nb, ns = len(big_shards), len(small_shards)
    n_fixed_in, n_fixed_out = 2, 5

    def body(*refs):
        h_ref, w_ref = refs[:n_fixed_in]
        shard_refs = refs[n_fixed_in:n_fixed_in + nb + ns]
        pos = n_fixed_in + nb + ns
        abcv_ref, gates_ref, qkv0_ref, qkv1_ref, qkv2_ref = refs[pos:pos + n_fixed_out]
        gathered_refs = refs[pos + n_fixed_out:pos + n_fixed_out + nb + ns]
        scr, send_sems, recv_sems = refs[pos + n_fixed_out + nb + ns:]
        i = pl.program_id(0)

        def gather_copies():
            return _gather_first_copies(shard_refs[:nb], shard_refs[nb:], gathered_refs[:nb], gathered_refs[nb:],
                                        send_sems, recv_sems)

        @pl.when(i == 0)
        def _():
            for cp in gather_copies():
                cp.start()

        h = h_ref[...]
        for lo, size in _col_chunks(3 * CONV_WIDTH, 512):
            abcv_ref[:, lo:lo + size] = _dot_nt(h, w_ref[COL_ABCV + lo: COL_ABCV + lo + size, :]).astype(BF16)
        for lo, size in _col_chunks(2 * D_MODEL, 512):
            gates_ref[:, lo:lo + size] = _dot_nt(h, w_ref[COL_GATES + lo: COL_GATES + lo + size, :]).astype(BF16)
        for gi, (d, out_ref) in enumerate(zip(DILATIONS, (qkv0_ref, qkv1_ref, qkv2_ref))):
            for j, base in enumerate((COL_Q, COL_K, COL_V)):
                lo = base + gi * GROUP_WIDTH
                y = _dot_nt(h, w_ref[lo:lo + GROUP_WIDTH, :])
                if j == 0:
                    y = y * ATTN_SCALE
                _to_streams(y, scr, out_ref, d, j * GROUP_WIDTH)

        @pl.when(i == nt - 1)
        def _():
            for cp in gather_copies():
                cp.wait()

    outs = [jax.ShapeDtypeStruct((s, 3 * CONV_WIDTH), BF16), jax.ShapeDtypeStruct((s, 2 * D_MODEL), BF16)]
    outs += [jax.ShapeDtypeStruct((d, s // d, 3 * GROUP_WIDTH), BF16) for d in DILATIONS]
    outs += [jax.ShapeDtypeStruct((N_CHIPS,) + a.shape[1:], a.dtype) for a in list(big_shards) + list(small_shards)]
    return pl.pallas_call(
        body, name="inproj_fwd", grid=(nt,), out_shape=outs,
        in_specs=[_rows(tm, D_MODEL), _resident((D_IN, D_MODEL))] + [_ANY] * (nb + ns),
        out_specs=[_rows(tm, 3 * CONV_WIDTH), _rows(tm, 2 * D_MODEL)]
        + [_stream_block(tm, d, 3 * GROUP_WIDTH) for d in DILATIONS] + [_ANY] * (nb + ns),
        scratch_shapes=[pltpu.VMEM((GROUP_WIDTH // LANES * tm, LANES), F32)] + _dma_sems(3 * (nb + ns)),
        compiler_params=_params(1))(h1, w_in_t, *big_shards, *small_shards)


def _head_of_lane(shape):
    return lax.broadcasted_iota(jnp.int32, shape, 1) // HEAD_DIM


def _stack_heads(v):
    head = _head_of_lane(v.shape)
    return jnp.concatenate([jnp.where(head == h, v, jnp.zeros_like(v)) for h in range(HEADS_PER_GROUP)], axis=0)


def _unstack_heads(v):
    q = ATTN_BLOCK
    head = _head_of_lane((q, v.shape[1]))
    out = jnp.zeros((q, v.shape[1]), v.dtype)
    for h in range(HEADS_PER_GROUP):
        out = jnp.where(head == h, v[h * q:(h + 1) * q], out)
    return out


def _per_head_rows(col):
    q = ATTN_BLOCK
    head = _head_of_lane((q, GROUP_WIDTH))
    out = jnp.zeros((q, GROUP_WIDTH), col.dtype)
    for h in range(HEADS_PER_GROUP):
        out = jnp.where(head == h, col[h * q:(h + 1) * q], out)
    return out


def _compact_heads(v):
    lane = lax.broadcasted_iota(jnp.int32, (v.shape[0], LANES), 1)
    return jnp.where((lane & 32) == 0, v[:, 0:LANES], v[:, LANES:2 * LANES])


def _compact_head_col(v):
    lane = lax.broadcasted_iota(jnp.int32, v.shape, 1)
    head = ((lane >> 6) & 1) + 2 * ((lane >> 5) & 1)
    cols = [jnp.max(jnp.where(head == h, v, -jnp.inf), axis=1, keepdims=True) for h in range(HEADS_PER_GROUP)]
    return jnp.concatenate(cols, axis=0)


ATTN_BLOCKS_PER_STEP = 4


def _band_bias(first_block):
    rows = HEADS_PER_GROUP * ATTN_BLOCK
    qi = lax.broadcasted_iota(jnp.int32, (rows, 2 * ATTN_BLOCK), 0) % ATTN_BLOCK
    kj = lax.broadcasted_iota(jnp.int32, (rows, 2 * ATTN_BLOCK), 1)
    dist = qi + ATTN_BLOCK - kj
    valid = (dist >= 0) & (dist <= ATTN_BLOCK)
    if first_block:
        valid = valid & (kj >= ATTN_BLOCK)
    return jnp.where(valid, 0.0, NEG_INF).astype(F32)


def _store_band_biases(bias_ref):
    bias_ref[0] = _band_bias(False)
    bias_ref[1] = _band_bias(True)


def _attn_block_specs(g, nb, clamp_last=False):
    q = ATTN_BLOCK
    last = nb // g - 1

    def cur(col, width=GROUP_WIDTH):
        if clamp_last:
            return pl.BlockSpec((None, g * q, width), lambda r, n: (r, jnp.minimum(n, last), col))
        return pl.BlockSpec((None, g * q, width), lambda r, n: (r, n, col))

    def prev(col):
        if clamp_last:
            return pl.BlockSpec((None, q, GROUP_WIDTH), lambda r, n: (r, jnp.clip(n * g - 1, 0, nb - 1), col))
        return pl.BlockSpec((None, q, GROUP_WIDTH), lambda r, n: (r, jnp.maximum(n * g - 1, 0), col))

    return cur, prev


def _attn_fwd(qkv, gi, forward=()):
    d, length, _ = qkv.shape
    nb = length // ATTN_BLOCK
    q = ATTN_BLOCK
    g = min(ATTN_BLOCKS_PER_STEP, nb)
    ns = nb // g
    nf = len(forward)

    def body(*refs):
        q_ref, kp_ref, kc_ref, vp_ref, vc_ref = refs[:5]
        o_ref, lse_ref = refs[5 + nf:7 + nf]
        buf_refs = refs[7 + nf:7 + 2 * nf]
        bias_ref = refs[7 + 2 * nf]
        sems = refs[8 + 2 * nf:]
        n = pl.program_id(1)
        first_step = (pl.program_id(0) == 0) & (n == 0)
        last_step = (pl.program_id(0) == d - 1) & (n == ns - 1)

        @pl.when(first_step)
        def _():
            _store_band_biases(bias_ref)
            for cp in _gather_forward_copies(buf_refs, *sems) if nf else ():
                cp.start()

        kfull = jnp.concatenate([kp_ref[...], kc_ref[...]], axis=0)
        vfull = jnp.concatenate([vp_ref[...], vc_ref[...]], axis=0)
        for j in range(g):
            qs = _stack_heads(q_ref[j * q:(j + 1) * q, :])
            k2 = kfull[j * q:(j + 2) * q]
            v2 = vfull[j * q:(j + 2) * q]
            bias = jnp.where(n == 0, bias_ref[1], bias_ref[0]) if j == 0 else bias_ref[0]
            sc = _dot_nt(qs, k2) + bias
            m = jnp.max(sc, axis=1, keepdims=True)
            p = jnp.exp(sc - m)
            l = jnp.sum(p, axis=1, keepdims=True)
            of = _dot(p.astype(BF16), v2) / l
            o_ref[j * q:(j + 1) * q, :] = _unstack_heads(of).astype(BF16)
            lse_ref[j * q:(j + 1) * q, :] = _per_head_rows(m + jnp.log(l))

        if nf:
            @pl.when(last_step)
            def _():
                for cp in _gather_forward_copies(buf_refs, *sems):
                    cp.wait()

    cur, prev = _attn_block_specs(g, nb)
    return pl.pallas_call(
        body, name=f"attn_fwd_g{gi}", grid=(d, ns),
        out_shape=[jax.ShapeDtypeStruct((d, length, GROUP_WIDTH), BF16),
                   jax.ShapeDtypeStruct((d, length, GROUP_WIDTH), F32)]
        + [jax.ShapeDtypeStruct(a.shape, a.dtype) for a in forward],
        in_specs=[cur(0), prev(1), cur(1), prev(2), cur(2)] + [_ANY] * nf,
        out_specs=[cur(0), cur(0)] + [_ANY] * nf,
        input_output_aliases={5 + a: 2 + a for a in range(nf)},
        scratch_shapes=[pltpu.VMEM((2, HEADS_PER_GROUP * q, 2 * q), F32)] + (_dma_sems(3 * nf) if nf else []),
        compiler_params=_params(2))(qkv, qkv, qkv, qkv, qkv, *forward)


def _conv_branch(ab, ac, av, halo_u, w, b):
    u = ac * av
    sh1 = _shift_down(u, 1, halo_u)
    sh2 = _shift_down(u, 2, halo_u)
    cv = w[0:1] * sh2 + w[1:2] * sh1 + w[2:3] * u + b
    return ab * cv, cv, u, sh1, sh2


def _mix_fwd(x, abcv, gates, o_list, lse_list, conv_w, conv_b, b_gate, w_pa, w_pb, w_out):
    s = x.shape[0]
    tm = ROW_TILE

    def body(x_ref, abcv_ref, gates_ref, o0_ref, o1_ref, o2_ref, l0_ref, l1_ref, l2_ref,
             cw_ref, cb_ref, bg_ref, wpa_ref, wpb_ref, wout_ref,
             x1_ref, ya0_ref, yb0_ref, mrg_ref, ya_ref, yb_ref, lsetot_ref, halo_ref, scr):
        i = pl.program_id(0)

        @pl.when(i == 0)
        def _():
            halo_ref[...] = jnp.zeros_like(halo_ref)

        ab = abcv_ref[:, 0:CONV_WIDTH].astype(F32)
        ac = abcv_ref[:, CONV_WIDTH:2 * CONV_WIDTH].astype(F32)
        av = abcv_ref[:, 2 * CONV_WIDTH:3 * CONV_WIDTH].astype(F32)
        ya0, _, u, _, _ = _conv_branch(ab, ac, av, halo_ref[...], cw_ref[...], cb_ref[...])
        halo_ref[...] = u[tm - SUBLANES:tm]
        ya0 = ya0.astype(BF16)
        ya0_ref[...] = ya0
        ya = _dot(ya0, wpa_ref[...])

        o_refs, l_refs = (o0_ref, o1_ref, o2_ref), (l0_ref, l1_ref, l2_ref)
        lses = [_from_streams(l_refs[g], scr, DILATIONS[g], 0, GROUP_WIDTH) for g in range(3)]
        top = jnp.maximum(jnp.maximum(lses[0], lses[1]), lses[2])
        weights = [jnp.exp(lse - top) for lse in lses]
        total = weights[0] + weights[1] + weights[2]
        lsetot_ref[...] = top + jnp.log(total)
        inv_total = 1.0 / total
        yb = jnp.zeros((tm, D_MODEL), F32)
        for g in range(3):
            og = _from_streams(o_refs[g], scr, DILATIONS[g], 0, GROUP_WIDTH)
            yb0 = (weights[g] * inv_total * og).astype(BF16)
            yb0_ref[:, g * GROUP_WIDTH:(g + 1) * GROUP_WIDTH] = yb0
            yb = yb + _dot(yb0, wpb_ref[g * GROUP_WIDTH:(g + 1) * GROUP_WIDTH, :])

        sa = _sigmoid(gates_ref[:, 0:D_MODEL].astype(F32) + bg_ref[0:1, :])
        sb = _sigmoid(gates_ref[:, D_MODEL:2 * D_MODEL].astype(F32) + bg_ref[1:2, :])
        ya_ref[...] = ya.astype(BF16)
        yb_ref[...] = yb.astype(BF16)
        mrg = (sa * ya + sb * yb).astype(BF16)
        mrg_ref[...] = mrg
        x1_ref[...] = x_ref[...] + _dot(mrg, wout_ref[...])

    outs = [jax.ShapeDtypeStruct((s, D_MODEL), F32),
            jax.ShapeDtypeStruct((s, CONV_WIDTH), BF16),
            jax.ShapeDtypeStruct((s, ATTN_WIDTH), BF16),
            jax.ShapeDtypeStruct((s, D_MODEL), BF16),
            jax.ShapeDtypeStruct((s, D_MODEL), BF16),
            jax.ShapeDtypeStruct((s, D_MODEL), BF16),
            jax.ShapeDtypeStruct((s, GROUP_WIDTH), F32)]
    return pl.pallas_call(
        body, name="mix_fwd", grid=(s // tm,), out_shape=outs,
        in_specs=[_rows(tm, D_MODEL), _rows(tm, 3 * CONV_WIDTH), _rows(tm, 2 * D_MODEL)]
        + [_stream_block(tm, d, GROUP_WIDTH) for d in DILATIONS] * 2
        + [_resident((3, CONV_WIDTH)), _resident((1, CONV_WIDTH)), _resident((2, D_MODEL)),
           _resident((CONV_WIDTH, D_MODEL)), _resident((ATTN_WIDTH, D_MODEL)), _resident((D_MODEL, D_MODEL))],
        out_specs=[_rows(tm, D_MODEL), _rows(tm, CONV_WIDTH), _rows(tm, ATTN_WIDTH), _rows(tm, D_MODEL),
                   _rows(tm, D_MODEL), _rows(tm, D_MODEL), _rows(tm, GROUP_WIDTH)],
        scratch_shapes=[pltpu.VMEM((SUBLANES, CONV_WIDTH), F32),
                        pltpu.VMEM((GROUP_WIDTH // LANES * tm, LANES), F32)],
        compiler_params=_params(1))(x, abcv, gates, *o_list, *lse_list, conv_w, conv_b, b_gate, w_pa, w_pb, w_out)


FFN_CHUNK = 512
FFN_UP_ROW_TILE = 256


def _ffn_up_fwd(x1, g, w_up, conv_w, conv_b):
    s = x1.shape[0]
    n = w_up.shape[1]
    tm = FFN_UP_ROW_TILE

    def body(x_ref, g_ref, w_ref, cw_ref, cb_ref, h_ref, up0_ref, up_ref, halo_ref):
        @pl.when(pl.program_id(0) == 0)
        def _():
            halo_ref[...] = jnp.zeros_like(halo_ref)

        xv = x_ref[...]
        r = lax.rsqrt(jnp.mean(xv * xv, axis=-1, keepdims=True) + EPS)
        h = (xv * r * g_ref[...]).astype(BF16)
        h_ref[...] = h
        for lo, size in _col_chunks(n, FFN_CHUNK):
            cols = slice(lo, lo + size)
            y = _dot(h, w_ref[:, cols])
            up0_ref[:, cols] = y.astype(BF16)
            halo = halo_ref[:, cols]
            w = cw_ref[:, cols]
            up = w[0:1] * _shift_down(y, 2, halo) + w[1:2] * _shift_down(y, 1, halo) + w[2:3] * y + cb_ref[:, cols]
            up_ref[:, cols] = up.astype(BF16)
            halo_ref[:, cols] = y[tm - SUBLANES:tm]

    return pl.pallas_call(
        body, name="ffn_up_fwd", grid=(s // tm,),
        out_shape=[jax.ShapeDtypeStruct((s, D_MODEL), BF16), jax.ShapeDtypeStruct((s, n), BF16),
                   jax.ShapeDtypeStruct((s, n), BF16)],
        in_specs=[_rows(tm, D_MODEL), _resident((1, D_MODEL)), _resident((D_MODEL, n)), _resident((3, n)),
                  _resident((1, n))],
        out_specs=[_rows(tm, D_MODEL), _rows(tm, n), _rows(tm, n)],
        scratch_shapes=[pltpu.VMEM((SUBLANES, n), F32)],
        compiler_params=_params(1))(x1, g, w_up, conv_w, conv_b)


def _ffn_act_fwd(x1, up, target, w_down, g_final):
    s = x1.shape[0]
    tm = ROW_TILE

    def body(x1_ref, up_ref, tgt_ref, wd_ref, gf_ref, act_ref, dx2_ref, dx2b_ref, dgf_ref, loss_ref):
        @pl.when(pl.program_id(0) == 0)
        def _():
            dgf_ref[...] = jnp.zeros_like(dgf_ref)
            loss_ref[...] = jnp.zeros_like(loss_ref)

        acc = jnp.zeros((tm, D_MODEL), F32)
        for lo, size in _col_chunks(D_FF, FFN_CHUNK):
            gate = up_ref[:, lo:lo + size].astype(F32)
            val = up_ref[:, D_FF + lo:D_FF + lo + size].astype(F32)
            act = (_silu(gate) * val).astype(BF16)
            act_ref[:, lo:lo + size] = act
            acc = acc + _dot(act, wd_ref[lo:lo + size, :])

        x2 = x1_ref[...] + acc
        r = lax.rsqrt(jnp.mean(x2 * x2, axis=-1, keepdims=True) + EPS)
        xn = x2 * r
        gf = gf_ref[...]
        err = xn * gf - tgt_ref[...]
        loss_ref[...] += (0.5 / D_MODEL) * jnp.sum(err * err)
        dy = err * (1.0 / D_MODEL)
        dgf_ref[...] += _colsum(dy * xn)
        dxn = dy * gf
        dx2 = r * (dxn - xn * jnp.mean(dxn * xn, axis=-1, keepdims=True))
        dx2_ref[...] = dx2
        dx2b_ref[...] = dx2.astype(BF16)

    return pl.pallas_call(
        body, name="ffn_act_fwd", grid=(s // tm,),
        out_shape=[jax.ShapeDtypeStruct((s, D_FF), BF16), jax.ShapeDtypeStruct((s, D_MODEL), F32),
                   jax.ShapeDtypeStruct((s, D_MODEL), BF16),
                   jax.ShapeDtypeStruct((1, D_MODEL), F32), jax.ShapeDtypeStruct((1, LANES), F32)],
        in_specs=[_rows(tm, D_MODEL), _rows(tm, 2 * D_FF), _rows(tm, D_MODEL),
                  _resident((D_FF, D_MODEL)), _resident((1, D_MODEL))],
        out_specs=[_rows(tm, D_FF), _rows(tm, D_MODEL), _rows(tm, D_MODEL),
                   pl.BlockSpec((1, D_MODEL), lambda i: (0, 0)), pl.BlockSpec((1, LANES), lambda i: (0, 0))],
        compiler_params=_params(1))(x1, up, target, w_down, g_final)


def _ffn_act_bwd(dx2b, up, w_down):
    s = dx2b.shape[0]
    tm = ROW_TILE

    def body(dx2_ref, up_ref, wd_ref, dup_ref):
        dx2 = dx2_ref[...]
        for lo, size in _col_chunks(D_FF, FFN_CHUNK):
            gate = up_ref[:, lo:lo + size].astype(F32)
            val = up_ref[:, D_FF + lo:D_FF + lo + size].astype(F32)
            dact = _dot_nt(dx2, wd_ref[lo:lo + size, :])
            sg = _sigmoid(gate)
            dup_ref[:, lo:lo + size] = (dact * val * (sg * (1.0 + gate * (1.0 - sg)))).astype(BF16)
            dup_ref[:, D_FF + lo:D_FF + lo + size] = (dact * (gate * sg)).astype(BF16)

    return pl.pallas_call(
        body, name="ffn_act_bwd", grid=(s // tm,),
        out_shape=jax.ShapeDtypeStruct((s, 2 * D_FF), BF16),
        in_specs=[_rows(tm, D_MODEL), _rows(tm, 2 * D_FF), _resident((D_FF, D_MODEL))],
        out_specs=_rows(tm, 2 * D_FF),
        compiler_params=_params(1))(dx2b, up, w_down)


def _ffn_up_bwd(d_up, up0, w_up, conv_w, x1, g, dres, swap=()):
    s = x1.shape[0]
    n = w_up.shape[1]
    tm = FFN_UP_ROW_TILE
    nt = s // tm
    nw = len(swap)

    def body(*refs):
        dup_ref, up0_ref, w_ref, cw_ref, x_ref, g_ref, dres_ref = refs[:7]
        slab_refs = refs[7:7 + nw]
        dx_ref, dxb_ref, dg_ref, dup0_ref, small_ref = refs[7 + nw:12 + nw]
        swapped_refs = refs[12 + nw:12 + 2 * nw]
        next_ref = refs[12 + 2 * nw]
        sems = refs[13 + 2 * nw:]

        @pl.when(pl.program_id(0) == 0)
        def _():
            next_ref[...] = jnp.zeros_like(next_ref)
            small_ref[...] = jnp.zeros_like(small_ref)
            dg_ref[...] = jnp.zeros_like(dg_ref)
            for cp in _sibling_swap_copies(slab_refs, swapped_refs, *sems) if nw else ():
                cp.start()

        dh = jnp.zeros((tm, D_MODEL), F32)
        for lo, size in _col_chunks(n, FFN_CHUNK):
            cols = slice(lo, lo + size)
            dz = dup_ref[:, cols].astype(F32)
            x0 = up0_ref[:, cols].astype(F32)
            nxt = next_ref[:, cols]
            dz1 = _shift_up(dz, 1, nxt)
            dz2 = _shift_up(dz, 2, nxt)
            next_ref[:, cols] = dz[0:SUBLANES]
            small_ref[0:1, cols] += _colsum(dz2 * x0)
            small_ref[1:2, cols] += _colsum(dz1 * x0)
            small_ref[2:3, cols] += _colsum(dz * x0)
            small_ref[3:4, cols] += _colsum(dz)
            w = cw_ref[:, cols]
            dup0 = (w[2:3] * dz + w[1:2] * dz1 + w[0:1] * dz2).astype(BF16)
            dup0_ref[:, cols] = dup0
            dh = dh + _dot_nt(dup0, w_ref[:, cols])
        xv = x_ref[...]
        r = lax.rsqrt(jnp.mean(xv * xv, axis=-1, keepdims=True) + EPS)
        xn = xv * r
        dg_ref[...] += _colsum(dh * xn)
        dxn = dh * g_ref[...]
        dx = dres_ref[...] + r * (dxn - xn * jnp.mean(dxn * xn, axis=-1, keepdims=True))
        dx_ref[...] = dx
        dxb_ref[...] = dx.astype(BF16)

        if nw:
            @pl.when(pl.program_id(0) == nt - 1)
            def _():
                for cp in _sibling_swap_copies(slab_refs, swapped_refs, *sems):
                    cp.wait()

    rows = lambda width: pl.BlockSpec((tm, width), lambda i: (nt - 1 - i, 0))
    return pl.pallas_call(
        body, name="ffn_up_bwd", grid=(nt,),
        out_shape=[jax.ShapeDtypeStruct((s, D_MODEL), F32), jax.ShapeDtypeStruct((s, D_MODEL), BF16),
                   jax.ShapeDtypeStruct((1, D_MODEL), F32), jax.ShapeDtypeStruct((s, n), BF16),
                   jax.ShapeDtypeStruct((SUBLANES, n), F32)] + _swap_shapes(swap),
        in_specs=[rows(n), rows(n), _resident((D_MODEL, n)), _resident((3, n)), rows(D_MODEL),
                  _resident((1, D_MODEL)), rows(D_MODEL)] + [_ANY] * nw,
        out_specs=[rows(D_MODEL), rows(D_MODEL), pl.BlockSpec((1, D_MODEL), lambda i: (0, 0)), rows(n),
                   pl.BlockSpec((SUBLANES, n), lambda i: (0, 0))] + [_ANY] * nw,
        scratch_shapes=[pltpu.VMEM((SUBLANES, n), F32)] + (_dma_sems(nw) if nw else []),
        compiler_params=_params(1))(d_up, up0, w_up, conv_w, x1, g, dres, *swap)


def _inproj_bwd(d_abcv, d_gates, d_qkvs, w_in_t, x, g, dres):
    s = x.shape[0]
    tm = ROW_TILE
    gw = GROUP_WIDTH

    def body(dabcv_ref, dgates_ref, dq0_ref, dq1_ref, dq2_ref, w_ref, x_ref, g_ref, dres_ref, dx_ref, dg_ref, scr):
        @pl.when(pl.program_id(0) == 0)
        def _():
            dg_ref[...] = jnp.zeros_like(dg_ref)

        dh = jnp.zeros((tm, D_MODEL), F32)
        for src, width, wrow in ((dabcv_ref, 3 * CONV_WIDTH, COL_ABCV), (dgates_ref, 2 * D_MODEL, COL_GATES)):
            for lo, size in _col_chunks(width, 512):
                dh = dh + _dot(src[:, lo:lo + size], w_ref[wrow + lo:wrow + lo + size, :])
        for gi, (d, dq_ref) in enumerate(zip(DILATIONS, (dq0_ref, dq1_ref, dq2_ref))):
            for j, base in enumerate((COL_Q, COL_K, COL_V)):
                dy = _from_streams(dq_ref, scr, d, j * gw, gw).astype(BF16)
                wrow = base + gi * gw
                dh = dh + _dot(dy, w_ref[wrow:wrow + gw, :])
        xv = x_ref[...]
        r = lax.rsqrt(jnp.mean(xv * xv, axis=-1, keepdims=True) + EPS)
        xn = xv * r
        dg_ref[...] += _colsum(dh * xn)
        dxn = dh * g_ref[...]
        dx_ref[...] = dres_ref[...] + r * (dxn - xn * jnp.mean(dxn * xn, axis=-1, keepdims=True))

    return pl.pallas_call(
        body, name="inproj_bwd", grid=(s // tm,),
        out_shape=[jax.ShapeDtypeStruct((s, D_MODEL), F32), jax.ShapeDtypeStruct((1, D_MODEL), F32)],
        in_specs=[_rows(tm, 3 * CONV_WIDTH), _rows(tm, 2 * D_MODEL)]
        + [_stream_block(tm, d, 3 * gw) for d in DILATIONS]
        + [_resident((D_IN, D_MODEL)), _rows(tm, D_MODEL), _resident((1, D_MODEL)), _rows(tm, D_MODEL)],
        out_specs=[_rows(tm, D_MODEL), pl.BlockSpec((1, D_MODEL), lambda i: (0, 0))],
        scratch_shapes=[pltpu.VMEM((gw // LANES * tm, LANES), F32)],
        compiler_params=_params(1))(d_abcv, d_gates, *d_qkvs, w_in_t, x, g, dres)


def _mix_bwd(dx1, abcv, gates, ya, yb, yb0, lsetot, conv_w, conv_b, b_gate, w_pa, w_pb, w_out, exchange=()):
    s = dx1.shape[0]
    tm = ROW_TILE
    nt = s // tm
    hb = tm // (2 * SUBLANES)
    nx = len(exchange)

    def body(*refs):
        (dx1_ref, abcv_ref, pre_ref, gates_ref, ya_ref, yb_ref, yb0_ref, lsetot_ref,
         cw_ref, cb_ref, bg_ref, wpa_ref, wpb_ref, wout_ref) = refs[:14]
        part_refs = refs[14:14 + nx]
        (dya_ref, dyb_ref, dgates_ref, dabcv_ref, dyb0_ref, dyl0_ref, dyl1_ref, dyl2_ref, aux0_ref, aux1_ref,
         aux2_ref, sm_gate_ref, sm_conv_ref) = refs[14 + nx:27 + nx]
        recv_refs = refs[27 + nx:27 + 2 * nx]
        next_ref, scr = refs[27 + 2 * nx:29 + 2 * nx]
        sems = refs[29 + 2 * nx:]
        i = pl.program_id(0)

        @pl.when(i == 0)
        def _():
            next_ref[...] = jnp.zeros_like(next_ref)
            sm_gate_ref[...] = jnp.zeros_like(sm_gate_ref)
            sm_conv_ref[...] = jnp.zeros_like(sm_conv_ref)
            for cp in _chip_exchange_copies(part_refs, recv_refs, *sems) if nx else ():
                cp.start()

        not_first = (i < nt - 1).astype(F32)
        dm = _dot_nt(dx1_ref[...].astype(BF16), wout_ref[...])
        sa = _sigmoid(gates_ref[:, 0:D_MODEL].astype(F32) + bg_ref[0:1, :])
        sb = _sigmoid(gates_ref[:, D_MODEL:2 * D_MODEL].astype(F32) + bg_ref[1:2, :])
        dya = (dm * sa).astype(BF16)
        dyb = (dm * sb).astype(BF16)
        dya_ref[...] = dya
        dyb_ref[...] = dyb
        dga = dm * ya_ref[...].astype(F32) * (sa * (1.0 - sa))
        dgb = dm * yb_ref[...].astype(F32) * (sb * (1.0 - sb))
        dgates_ref[:, 0:D_MODEL] = dga.astype(BF16)
        dgates_ref[:, D_MODEL:2 * D_MODEL] = dgb.astype(BF16)
        sm_gate_ref[0:1, :] += _colsum(dga)
        sm_gate_ref[1:2, :] += _colsum(dgb)

        dya0 = _dot_nt(dya, wpa_ref[...])
        ab = abcv_ref[:, 0:CONV_WIDTH].astype(F32)
        ac = abcv_ref[:, CONV_WIDTH:2 * CONV_WIDTH].astype(F32)
        av = abcv_ref[:, 2 * CONV_WIDTH:3 * CONV_WIDTH].astype(F32)
        pre = pre_ref[...].astype(F32) * not_first
        halo_u = (pre[:, CONV_WIDTH:2 * CONV_WIDTH] * pre[:, 2 * CONV_WIDTH:3 * CONV_WIDTH])[SUBLANES:2 * SUBLANES]
        w = cw_ref[...]
        _, cv, u, sh1, sh2 = _conv_branch(ab, ac, av, halo_u, w, cb_ref[...])
        dcv = dya0 * ab
        sm_conv_ref[0:1, :] += _colsum(dcv * sh2)
        sm_conv_ref[1:2, :] += _colsum(dcv * sh1)
        sm_conv_ref[2:3, :] += _colsum(dcv * u)
        sm_conv_ref[3:4, :] += _colsum(dcv)
        nxt = next_ref[...]
        du = w[2:3] * dcv + w[1:2] * _shift_up(dcv, 1, nxt) + w[0:1] * _shift_up(dcv, 2, nxt)
        next_ref[...] = dcv[0:SUBLANES]
        dabcv_ref[:, 0:CONV_WIDTH] = (dya0 * cv).astype(BF16)
        dabcv_ref[:, CONV_WIDTH:2 * CONV_WIDTH] = (du * av).astype(BF16)
        dabcv_ref[:, 2 * CONV_WIDTH:3 * CONV_WIDTH] = (du * ac).astype(BF16)

        head_r = lax.broadcasted_iota(jnp.int32, (GROUP_WIDTH, GROUP_WIDTH), 0) // HEAD_DIM
        head_c = lax.broadcasted_iota(jnp.int32, (GROUP_WIDTH, GROUP_WIDTH), 1) // HEAD_DIM
        same_head = (head_r == head_c).astype(BF16)
        prod = jnp.zeros((tm, GROUP_WIDTH), F32)
        dyb0s = []
        for g in range(3):
            cols = slice(g * GROUP_WIDTH, (g + 1) * GROUP_WIDTH)
            dyb0 = _dot_nt(dyb, wpb_ref[cols, :])
            dyb0_ref[:, cols] = dyb0.astype(BF16)
            dyb0s.append(dyb0)
            prod = prod + dyb0 * yb0_ref[:, cols].astype(F32)
        hi = prod.astype(BF16)
        mid = (prod - hi.astype(F32)).astype(BF16)
        lo = (prod - hi.astype(F32) - mid.astype(F32)).astype(BF16)
        delta = _dot(hi, same_head) + _dot(mid, same_head) + _dot(lo, same_head)
        lse_c = _compact_heads(lsetot_ref[...])
        delta_c = _compact_heads(delta)
        for g, (dy_ref, aux_ref) in enumerate(zip((dyl0_ref, dyl1_ref, dyl2_ref), (aux0_ref, aux1_ref, aux2_ref))):
            d = DILATIONS[g]
            _to_streams(dyb0s[g], scr, dy_ref, d, 0)
            _to_streams(lse_c, scr, aux_ref, d, 0)
            _to_streams(delta_c, scr, aux_ref, d, LANES)

        if nx:
            @pl.when(i == nt - 1)
            def _():
                for cp in _chip_exchange_copies(part_refs, recv_refs, *sems):
                    cp.wait()

    rev = lambda i: (nt - 1 - i, 0)
    pre = lambda i: (jnp.maximum((nt - 1 - i) * hb - 1, 0), 0)
    rows = lambda width: pl.BlockSpec((tm, width), rev)
    outs = [jax.ShapeDtypeStruct((s, D_MODEL), BF16), jax.ShapeDtypeStruct((s, D_MODEL), BF16),
            jax.ShapeDtypeStruct((s, 2 * D_MODEL), BF16), jax.ShapeDtypeStruct((s, 3 * CONV_WIDTH), BF16),
            jax.ShapeDtypeStruct((s, ATTN_WIDTH), BF16)]
    outs += [jax.ShapeDtypeStruct((d, s // d, GROUP_WIDTH), BF16) for d in DILATIONS]
    outs += [jax.ShapeDtypeStruct((d, s // d, 2 * LANES), F32) for d in DILATIONS]
    outs += [jax.ShapeDtypeStruct((SUBLANES, D_MODEL), F32), jax.ShapeDtypeStruct((SUBLANES, CONV_WIDTH), F32)]
    outs += [jax.ShapeDtypeStruct((3,) + a.shape[1:], a.dtype) for a in exchange]
    return pl.pallas_call(
        body, name="mix_bwd", grid=(nt,), out_shape=outs,
        in_specs=[rows(D_MODEL), rows(3 * CONV_WIDTH), pl.BlockSpec((2 * SUBLANES, 3 * CONV_WIDTH), pre),
                  rows(2 * D_MODEL), rows(D_MODEL), rows(D_MODEL), rows(ATTN_WIDTH), rows(GROUP_WIDTH),
                  _resident((3, CONV_WIDTH)), _resident((1, CONV_WIDTH)), _resident((2, D_MODEL)),
                  _resident((CONV_WIDTH, D_MODEL)), _resident((ATTN_WIDTH, D_MODEL)), _resident((D_MODEL, D_MODEL))]
        + [_ANY] * nx,
        out_specs=[rows(D_MODEL), rows(D_MODEL), rows(2 * D_MODEL), rows(3 * CONV_WIDTH), rows(ATTN_WIDTH)]
        + [_rev_stream_block(tm, d, GROUP_WIDTH, nt) for d in DILATIONS]
        + [_rev_stream_block(tm, d, 2 * LANES, nt) for d in DILATIONS]
        + [pl.BlockSpec((SUBLANES, D_MODEL), lambda i: (0, 0)), pl.BlockSpec((SUBLANES, CONV_WIDTH), lambda i: (0, 0))]
        + [_ANY] * nx,
        scratch_shapes=[pltpu.VMEM((SUBLANES, CONV_WIDTH), F32),
                        pltpu.VMEM((GROUP_WIDTH // LANES * tm, LANES), F32)] + (_dma_sems(3 * nx) if nx else []),
        compiler_params=_params(1))(dx1, abcv, abcv, gates, ya, yb, yb0, lsetot,
                                    conv_w, conv_b, b_gate, w_pa, w_pb, w_out, *exchange)


def _attn_bwd(qkv, dy, aux, gi, exchange=(), swap=()):
    d, length, _ = qkv.shape
    nb = length // ATTN_BLOCK
    q = ATTN_BLOCK
    gw = GROUP_WIDTH
    g = min(ATTN_BLOCKS_PER_STEP, nb)
    assert g >= 2 and nb % g == 0
    ns = nb // g
    lag = 1 if ns > 1 else 0
    tail = (g - 1) * q
    nx, nw = len(exchange), len(swap)

    def body(*refs):
        q_ref, kp_ref, kc_ref, vp_ref, vc_ref, dy_ref, aux_ref = refs[:7]
        part_refs = refs[7:7 + nx]
        slab_refs = refs[7 + nx:7 + nx + nw]
        pos = 7 + nx + nw
        out_ref = refs[pos]
        recv_refs = refs[pos + 1:pos + 1 + nx]
        swapped_refs = refs[pos + 1 + nx:pos + 1 + nx + nw]
        pos += 1 + nx + nw
        dq_ref, dkv_ref, bias_ref = refs[pos:pos + 3]
        sems = refs[pos + 3:]
        n = pl.program_id(1)

        def copies():
            cps = _chip_exchange_copies(part_refs, recv_refs, sems[0], sems[1]) if nx else []
            return cps + (_sibling_swap_copies(slab_refs, swapped_refs, sems[-2], sems[-1]) if nw else [])

        @pl.when((pl.program_id(0) == 0) & (n == 0))
        def _():
            _store_band_biases(bias_ref)
            for cp in copies():
                cp.start()

        if nx or nw:
            @pl.when((pl.program_id(0) == d - 1) & (n == ns - 1 + lag))
            def _():
                for cp in copies():
                    cp.wait()

        def emit(rows):
            out_ref[rows, gw:2 * gw] = dkv_ref[0, rows].astype(BF16)
            out_ref[rows, 2 * gw:3 * gw] = dkv_ref[1, rows].astype(BF16)

        if lag:
            @pl.when(n > 0)
            def _():
                out_ref[:, 0:gw] = dq_ref[...].astype(BF16)
                emit(slice(0, tail))

            @pl.when(n == ns)
            def _():
                emit(slice(tail, g * q))

        @pl.when(n < ns)
        def _():
            kfull = jnp.concatenate([kp_ref[...], kc_ref[...]], axis=0)
            vfull = jnp.concatenate([vp_ref[...], vc_ref[...]], axis=0)
            for j in range(g):
                rows = slice(j * q, (j + 1) * q)
                qs = _stack_heads(q_ref[rows, :])
                dys = _stack_heads(dy_ref[rows, :])
                k2 = kfull[j * q:(j + 2) * q]
                v2 = vfull[j * q:(j + 2) * q]
                lse = _compact_head_col(aux_ref[rows, 0:LANES])
                delta = _compact_head_col(aux_ref[rows, LANES:2 * LANES])
                bias = jnp.where(n == 0, bias_ref[1], bias_ref[0]) if j == 0 else bias_ref[0]
                p = jnp.exp(_dot_nt(qs, k2) + bias - lse)
                dp = _dot_nt(dys, v2)
                ds = (p * (dp - delta)).astype(BF16)
                dq_j = _unstack_heads(_dot(ds, k2)) * ATTN_SCALE
                dk2 = _dot_tn(ds, qs)
                dv2 = _dot_tn(p.astype(BF16), dys)
                if j == 0:
                    @pl.when(n > 0)
                    def _():
                        out_ref[tail:g * q, gw:2 * gw] = (dkv_ref[0, tail:g * q] + dk2[0:q]).astype(BF16)
                        out_ref[tail:g * q, 2 * gw:3 * gw] = (dkv_ref[1, tail:g * q] + dv2[0:q]).astype(BF16)
                else:
                    dkv_ref[0, (j - 1) * q:j * q] += dk2[0:q]
                    dkv_ref[1, (j - 1) * q:j * q] += dv2[0:q]
                dkv_ref[0, rows] = dk2[q:2 * q]
                dkv_ref[1, rows] = dv2[q:2 * q]
                dq_ref[rows, :] = dq_j
            if not lag:
                out_ref[:, 0:gw] = dq_ref[...].astype(BF16)
                emit(slice(0, g * q))

    cur, prev = _attn_block_specs(g, nb, clamp_last=True)
    return pl.pallas_call(
        body, name=f"attn_bwd_g{gi}", grid=(d, ns + lag),
        out_shape=[jax.ShapeDtypeStruct((d, length, 3 * gw), BF16)]
        + [jax.ShapeDtypeStruct((3,) + a.shape[1:], a.dtype) for a in exchange] + _swap_shapes(swap),
        in_specs=[cur(0), prev(1), cur(1), prev(2), cur(2), cur(0), cur(0, 2 * LANES)] + [_ANY] * (nx + nw),
        out_specs=[pl.BlockSpec((None, g * q, 3 * gw), lambda r, n: (r, jnp.maximum(n - lag, 0), 0))]
        + [_ANY] * (nx + nw),
        scratch_shapes=[pltpu.VMEM((g * q, gw), F32), pltpu.VMEM((2, g * q, gw), F32),
                        pltpu.VMEM((2, HEADS_PER_GROUP * q, 2 * q), F32)]
        + (_dma_sems(3 * nx) if nx else []) + (_dma_sems(nw) if nw else []),
        compiler_params=_params(2))(qkv, qkv, qkv, qkv, qkv, dy, aux, *exchange, *swap)


def _matmul_tn(name, a, b, col_tile=1024, row_tile=2048, slabs=0, swap=()):
    s, k = a.shape
    n = b.shape[1]
    tk = min(row_tile, s)
    tn = col_tile
    steps = s // tk
    nw = len(swap)

    def body(*refs):
        a_ref, b_ref = refs[:2]
        slab_refs = refs[2:2 + nw]
        o_ref = refs[2 + nw]
        swapped_refs = refs[3 + nw:3 + 2 * nw]
        acc_ref = refs[3 + 2 * nw]
        sems = refs[4 + 2 * nw:]
        t = pl.program_id(1)

        if nw:
            @pl.when((pl.program_id(0) == 0) & (t == 0))
            def _():
                for cp in _sibling_swap_copies(slab_refs, swapped_refs, *sems):
                    cp.start()

            @pl.when((pl.program_id(0) == n // tn - 1) & (t == steps - 1))
            def _():
                for cp in _sibling_swap_copies(slab_refs, swapped_refs, *sems):
                    cp.wait()

        @pl.when(t == 0)
        def _():
            acc_ref[...] = jnp.zeros_like(acc_ref)

        acc_ref[...] += _dot_tn(a_ref[...], b_ref[...])

        @pl.when(t == steps - 1)
        def _():
            if slabs:
                for q in range(per_tile):
                    o_ref[q] = acc_ref[:, q * width:(q + 1) * width].astype(BF16)
            else:
                o_ref[...] = acc_ref[...].astype(BF16)

    if slabs:
        width = n // slabs
        per_tile = tn // width
        out_shape = jax.ShapeDtypeStruct((slabs, k, width), BF16)
        out_spec = pl.BlockSpec((per_tile, k, width), lambda j, t: (j, 0, 0))
    else:
        out_shape = jax.ShapeDtypeStruct((k, n), BF16)
        out_spec = pl.BlockSpec((k, tn), lambda j, t: (0, j))
    res = pl.pallas_call(
        body, name=name, grid=(n // tn, steps), out_shape=[out_shape] + _swap_shapes(swap),
        in_specs=[pl.BlockSpec((tk, k), lambda j, t: (t, 0)), pl.BlockSpec((tk, tn), lambda j, t: (t, j))] + [_ANY] * nw,
        out_specs=[out_spec] + [_ANY] * nw,
        scratch_shapes=[pltpu.VMEM((k, tn), F32)] + (_dma_sems(nw) if nw else []),
        compiler_params=_params(2))(a, b, *swap)
    return res if nw else res[0]


def _sibling_swap_halves(name, slabs):
    na = len(slabs)

    def body(*refs):
        src_refs, out_refs = refs[:na], refs[na:2 * na]
        send_sems, recv_sems = refs[2 * na:]
        x, y, c, _ = _mesh_position()
        cps = []
        for a in range(na):
            theirs = _half_rows(1 - c, src_refs[a].shape[1] // 2)
            cps.append(pltpu.make_async_remote_copy(
                src_ref=src_refs[a].at[:, theirs, :], dst_ref=out_refs[a], send_sem=send_sems.at[a],
                recv_sem=recv_sems.at[a], device_id=(x, y, 1 - c), device_id_type=MESH_ID))
        for cp in cps:
            cp.start()
        for cp in cps:
            cp.wait()

    return pl.pallas_call(
        body, name=name,
        out_shape=[jax.ShapeDtypeStruct((a.shape[0], a.shape[1] // 2, a.shape[2]), a.dtype) for a in slabs],
        in_specs=[_ANY] * na, out_specs=[_ANY] * na,
        scratch_shapes=[pltpu.SemaphoreType.DMA((na,)), pltpu.SemaphoreType.DMA((na,))])(*slabs)


_HBM = pl.BlockSpec(memory_space=pltpu.HBM)
_SEM = pl.BlockSpec(memory_space=pltpu.SEMAPHORE)
_DATAFLOW = pltpu.SideEffectType.DATAFLOW_SIDE_EFFECTING


def _gather_start(shard):
    gathered = jax.ShapeDtypeStruct((N_CHIPS,) + shard.shape[1:], shard.dtype)

    def body(src_ref, buf_ref, send_sems, recv_sems, src_thru, buf_thru, token):
        for cp in _gather_first_copies([src_ref], [], [buf_ref], [], send_sems, recv_sems):
            cp.start()
        token[...] = jnp.zeros_like(token)

    return pl.pallas_call(
        body, name="gather_start",
        out_shape=(pltpu.SemaphoreType.DMA((3,)), pltpu.SemaphoreType.DMA((3,)),
                   pltpu.HBM(shard.shape, shard.dtype), pltpu.HBM(gathered.shape, gathered.dtype),
                   jax.ShapeDtypeStruct((SUBLANES, LANES), F32)),
        in_specs=(_HBM, _HBM), out_specs=(_SEM, _SEM, _HBM, _HBM, _VMEM), input_output_aliases={0: 2, 1: 3},
        compiler_params=pltpu.CompilerParams(has_side_effects=_DATAFLOW),
    )(pltpu.with_memory_space_constraint(shard, pltpu.HBM),
      pltpu.with_memory_space_constraint(lax.empty(gathered.shape, gathered.dtype), pltpu.HBM))


def _gather_forward(send_sems, recv_sems, shard_thru, buf_thru, after):
    def body(src_ref, buf_ref, send_sems, recv_sems, after_ref, fwd_send, fwd_recv, buf_out):
        for cp in _gather_first_copies([src_ref], [], [buf_ref], [], send_sems, recv_sems):
            cp.wait_send()
            cp.wait_recv()
        for cp in _gather_forward_copies([buf_ref], fwd_send, fwd_recv):
            cp.start()

    return pl.pallas_call(
        body, name="gather_forward",
        out_shape=(pltpu.SemaphoreType.DMA((3,)), pltpu.SemaphoreType.DMA((3,)),
                   pltpu.HBM(buf_thru.shape, buf_thru.dtype)),
        in_specs=(_HBM, _HBM, _SEM, _SEM, _ANY), out_specs=(_SEM, _SEM, _HBM), input_output_aliases={1: 2},
        compiler_params=pltpu.CompilerParams(has_side_effects=_DATAFLOW),
    )(shard_thru, buf_thru, send_sems, recv_sems, after)


def _gather_wait(fwd_send, fwd_recv, buf_thru):
    def body(buf_ref, fwd_send, fwd_recv, buf_out):
        for cp in _gather_forward_copies([buf_ref], fwd_send, fwd_recv):
            cp.wait_send()
            cp.wait_recv()

    return pl.pallas_call(
        body, name="gather_wait", out_shape=pltpu.HBM(buf_thru.shape, buf_thru.dtype),
        in_specs=(_HBM, _SEM, _SEM), out_specs=_HBM, input_output_aliases={0: 0},
        compiler_params=pltpu.CompilerParams(has_side_effects=_DATAFLOW),
    )(buf_thru, fwd_send, fwd_recv)


def _chip_exchange_start(partial):
    _, rows, cols = partial.shape
    landing = jax.ShapeDtypeStruct((3, rows, cols), partial.dtype)

    def body(src_ref, land_ref, send_sems, recv_sems, src_thru, land_thru, token):
        for cp in _chip_exchange_copies([src_ref], [land_ref], send_sems, recv_sems):
            cp.start()
        token[...] = jnp.zeros_like(token)

    return pl.pallas_call(
        body, name="grad_exchange_start",
        out_shape=(pltpu.SemaphoreType.DMA((3,)), pltpu.SemaphoreType.DMA((3,)),
                   pltpu.HBM(partial.shape, partial.dtype), pltpu.HBM(landing.shape, landing.dtype),
                   jax.ShapeDtypeStruct((SUBLANES, LANES), F32)),
        in_specs=(_HBM, _HBM), out_specs=(_SEM, _SEM, _HBM, _HBM, _VMEM), input_output_aliases={0: 2, 1: 3},
        compiler_params=pltpu.CompilerParams(has_side_effects=_DATAFLOW),
    )(pltpu.with_memory_space_constraint(partial, pltpu.HBM),
      pltpu.with_memory_space_constraint(lax.empty(landing.shape, landing.dtype), pltpu.HBM))


def _chip_exchange_wait(send_sems, recv_sems, src_thru, land_thru, after):
    def body(src_ref, land_ref, send_sems, recv_sems, after_ref, src_out, land_out):
        for cp in _chip_exchange_copies([src_ref], [land_ref], send_sems, recv_sems):
            cp.wait_send()
            cp.wait_recv()

    return pl.pallas_call(
        body, name="grad_exchange_wait",
        out_shape=(pltpu.HBM(src_thru.shape, src_thru.dtype), pltpu.HBM(land_thru.shape, land_thru.dtype)),
        in_specs=(_HBM, _HBM, _SEM, _SEM, _ANY), out_specs=(_HBM, _HBM), input_output_aliases={0: 0, 1: 1},
        compiler_params=pltpu.CompilerParams(has_side_effects=_DATAFLOW),
    )(src_thru, land_thru, send_sems, recv_sems, after)


def _sibling_share(halves):
    na = len(halves)

    def body(*refs):
        out_refs = refs[na:2 * na]
        send_sems, recv_sems = refs[2 * na:]
        x, y, c, _ = _mesh_position()
        cps = []
        for a in range(na):
            mine = out_refs[a].at[0, _half_rows(c, out_refs[a].shape[1] // 2)]
            cps.append(pltpu.make_async_remote_copy(src_ref=mine, dst_ref=mine, send_sem=send_sems.at[a],
                                                    recv_sem=recv_sems.at[a], device_id=(x, y, 1 - c),
                                                    device_id_type=MESH_ID))
        for cp in cps:
            cp.start()
        for a, cp in enumerate(cps):
            cp.wait_send()
            theirs = out_refs[a].at[0, _half_rows(1 - c, out_refs[a].shape[1] // 2)]
            pltpu.make_async_remote_copy(src_ref=theirs, dst_ref=theirs, send_sem=send_sems.at[a],
                                         recv_sem=recv_sems.at[a], device_id=(x, y, 1 - c),
                                         device_id_type=MESH_ID).wait_recv()

    return pl.pallas_call(
        body, name="grad_sibling_share", out_shape=[jax.ShapeDtypeStruct(a.shape, a.dtype) for a in halves],
        in_specs=[_ANY] * na, out_specs=[_ANY] * na, input_output_aliases={a: a for a in range(na)},
        scratch_shapes=[pltpu.SemaphoreType.DMA((na,)), pltpu.SemaphoreType.DMA((na,))])(*halves)


def _add_sibling(name, slabs, received, core):
    na = len(slabs)
    halves = [a.shape[1] // 2 for a in slabs]

    def body(core_ref, *refs):
        for a in range(na):
            refs[2 * na + a][...] = (refs[a][...].astype(F32) + refs[na + a][...].astype(F32)).astype(BF16)

    def block(a, mine):
        if mine:
            return pl.BlockSpec((None, halves[a], slabs[a].shape[2]), lambda s, core_ref: (s, core_ref[0], 0))
        return pl.BlockSpec((None, halves[a], slabs[a].shape[2]), lambda s, core_ref: (s, 0, 0))

    grid_spec = pltpu.PrefetchScalarGridSpec(
        num_scalar_prefetch=1, grid=(N_CHIPS,),
        in_specs=[block(a, True) for a in range(na)] + [block(a, False) for a in range(na)],
        out_specs=[block(a, False) for a in range(na)])
    return pl.pallas_call(body, name=name, grid_spec=grid_spec,
                          out_shape=[jax.ShapeDtypeStruct(r.shape, BF16) for r in received],
                          compiler_params=_params(1))(core, *slabs, *received)


def _sum_chips(name, partials, received, chip_core):
    na = len(partials)

    def body(cc_ref, *refs):
        for a in range(na):
            acc = refs[a][...].astype(F32)
            for k in range(3):
                acc = acc + refs[na + a][k].astype(F32)
            refs[2 * na + a][...] = acc

    def own(p):
        return pl.BlockSpec((None,) + p.shape[1:], lambda i, cc_ref: (cc_ref[0], 0, 0))

    def mine(p):
        return pl.BlockSpec((None,) + p.shape[1:], lambda i, cc_ref: (0, cc_ref[1], 0))

    grid_spec = pltpu.PrefetchScalarGridSpec(
        num_scalar_prefetch=1, grid=(1,),
        in_specs=[own(p) for p in partials] + [pl.BlockSpec(r.shape, lambda i, cc_ref: (0, 0, 0)) for r in received],
        out_specs=[mine(p) for p in partials])
    return pl.pallas_call(body, name=name, grid_spec=grid_spec,
                          out_shape=[jax.ShapeDtypeStruct((1, 2 * p.shape[1], p.shape[2]), F32) for p in partials],
                          compiler_params=_params(1))(chip_core, *partials, *received)


def _adam_math(w, g, m, v):
    nm = ADAM_B1 * m + (1.0 - ADAM_B1) * g
    nv = ADAM_B2 * v + (1.0 - ADAM_B2) * jnp.square(g)
    m_hat = nm / (1.0 - ADAM_B1 ** ADAM_STEP)
    v_hat = nv / (1.0 - ADAM_B2 ** ADAM_STEP)
    delta = -ADAM_LR * (m_hat / (jnp.sqrt(v_hat) + ADAM_EPS) + ADAM_WD * w)
    return delta, nm, nv


def _adamw(name, w, g, m, v):
    _, rows, cols = w.shape
    tr = next(t for t in (736, 512, 384, 352, 256, 128, 64, 32, 16, 8) if rows % t == 0)

    def body(w_ref, g_ref, m_ref, v_ref, g_out_ref, d_ref, nm_ref, nv_ref):
        gv = g_ref[...]
        g_out_ref[...] = gv
        d_ref[...], nm_ref[...], nv_ref[...] = _adam_math(w_ref[...], gv, m_ref[...], v_ref[...])

    spec = pl.BlockSpec((None, tr, cols), lambda i: (0, i, 0))
    return pl.pallas_call(
        body, name=name, grid=(rows // tr,), out_shape=[jax.ShapeDtypeStruct(w.shape, F32)] * 4,
        in_specs=[spec] * 4, out_specs=[spec] * 4, compiler_params=_params(1))(w, g, m, v)


SMALL_PARAMS = ("norm_mix_g", "b_gate", "conv_a_w", "conv_a_b", "norm_ffn_g", "ffn_conv_w", "ffn_conv_b", "final_norm_g")


def _small_update(partials, params, moments_m, moments_v):
    na = len(partials)
    npar = len(SMALL_PARAMS)

    def body(*refs):
        in_refs = refs[:na]
        w_refs = refs[na:na + npar]
        m_refs = refs[na + npar:na + 2 * npar]
        v_refs = refs[na + 2 * npar:na + 3 * npar]
        pos = na + 3 * npar
        loss_ref = refs[pos]
        out_refs = refs[pos + 1:pos + 1 + 4 * npar]
        pos += 1 + 4 * npar
        acc_refs = refs[pos:pos + na]
        recv_refs = refs[pos + na:pos + 4 * na]
        send_sems, recv_sems = refs[pos + 4 * na:]
        x, y, c, _ = _mesh_position()
        chip = 2 * x + y
        for a in range(na):
            acc_refs[a][...] = in_refs[a][...]
        for stage, peer in enumerate(((x, y, 1 - c), (x, 1 - y, c), (1 - x, y, c))):
            cps = []
            for a in range(na):
                k = stage * na + a
                cps.append(pltpu.make_async_remote_copy(src_ref=acc_refs[a], dst_ref=recv_refs[k], send_sem=send_sems.at[k],
                                                        recv_sem=recv_sems.at[k], device_id=peer, device_id_type=MESH_ID))
            for cp in cps:
                cp.start()
            for cp in cps:
                cp.wait()
            for a in range(na):
                acc_refs[a][...] = acc_refs[a][...] + recv_refs[stage * na + a][...]

        mix, ffn, fin, gate, conv, ffnc, loss = acc_refs
        loss_ref[...] = loss[...]

        def cols(width):
            return pl.ds(pl.multiple_of(chip * width, LANES), width)

        grads = {
            "norm_mix_g": mix[...], "norm_ffn_g": ffn[...], "final_norm_g": fin[...],
            "b_gate": gate[0:2, cols(D_MODEL // N_CHIPS)],
            "conv_a_w": conv[0:3, cols(CONV_WIDTH // N_CHIPS)], "conv_a_b": conv[3:4, :],
            "ffn_conv_w": ffnc[0:3, cols(2 * D_FF // N_CHIPS)], "ffn_conv_b": ffnc[3:4, :]}
        for i, name in enumerate(SMALL_PARAMS):
            g = grads[name]
            if len(w_refs[i].shape) == 3:
                results = (g,) + _adam_math(w_refs[i][0], g, m_refs[i][0], v_refs[i][0])
                for o_ref, val in zip(out_refs[4 * i:4 * i + 4], results):
                    o_ref[0] = val
            else:
                results = (g,) + _adam_math(w_refs[i][...], g, m_refs[i][...], v_refs[i][...])
                for o_ref, val in zip(out_refs[4 * i:4 * i + 4], results):
                    o_ref[...] = val

    outs = [jax.ShapeDtypeStruct(partials[-1].shape, F32)]
    for w in params:
        outs += [jax.ShapeDtypeStruct(w.shape, F32)] * 4
    scratch = [pltpu.VMEM(p.shape, F32) for p in partials]
    scratch += [pltpu.VMEM(p.shape, F32) for _ in range(3) for p in partials]
    scratch += [pltpu.SemaphoreType.DMA((3 * na,)), pltpu.SemaphoreType.DMA((3 * na,))]
    n_in = na + 3 * npar
    return pl.pallas_call(
        body, name="small_update", out_shape=outs, in_specs=[_VMEM] * n_in, out_specs=[_VMEM] * len(outs),
        scratch_shapes=scratch)(*partials, *params, *moments_m, *moments_v)


def _gathered_columns(g):
    return jnp.transpose(g, (1, 0, 2)).reshape(g.shape[1], N_CHIPS * g.shape[2])


def kernel(x, norm_mix_g, w_in, b_gate, conv_a_w, conv_a_b, w_proj_a, w_proj_b, w_out, norm_ffn_g, w_up, ffn_conv_w, ffn_conv_b, w_down, final_norm_g, loss_target, m_norm_mix_g, m_w_in, m_b_gate, m_conv_a_w, m_conv_a_b, m_w_proj_a, m_w_proj_b, m_w_out, m_norm_ffn_g, m_w_up, m_ffn_conv_w, m_ffn_conv_b, m_w_down, m_final_norm_g, v_norm_mix_g, v_w_in, v_b_gate, v_conv_a_w, v_conv_a_b, v_w_proj_a, v_w_proj_b, v_w_out, v_norm_ffn_g, v_w_up, v_ffn_conv_w, v_ffn_conv_b, v_w_down, v_final_norm_g):
    chip = (2 * lax.axis_index("x") + lax.axis_index("y")).astype(jnp.int32)
    core = lax.axis_index("c").astype(jnp.int32)
    core_arr = core.reshape(1)
    chip_core = jnp.stack([chip, core])
    xs, target = x[0], loss_target[0]
    g_final = final_norm_g.reshape(1, D_MODEL)

    def own_slot(gathered, own):
        return lax.dynamic_update_slice(gathered, own, (chip, 0, 0))

    w_in_t, m_w_in_t, v_w_in_t = (jnp.swapaxes(a, 1, 2) for a in (w_in, m_w_in, v_w_in))
    w_in_tb = w_in_t.astype(BF16)
    send1, recv1, shard_thru, g_in, token = _gather_start(w_in_tb)
    h1, h1_streams4, h1_streams16 = _norm_fwd(xs, norm_mix_g + token[0:1, 0:1])
    send2, recv2, g_in = _gather_forward(send1, recv1, shard_thru, g_in, h1)
    g_in = _gather_wait(send2, recv2, g_in)
    w_in_full_t = own_slot(g_in, w_in_tb).reshape(D_IN, D_MODEL)
    later_w = [w_proj_a, w_proj_b, w_out, w_up, w_down]
    later_b = [w.astype(BF16) for w in later_w]
    small_sharded = [b_gate, conv_a_w, ffn_conv_w]
    fwd = _inproj_fwd(h1, w_in_full_t, later_b, small_sharded)
    abcv, gates, qkv0, qkv1, qkv2 = fwd[:5]
    gathered_big, gathered_small = fwd[5:10], fwd[10:13]
    attn0 = _attn_fwd(qkv0, 0, forward=gathered_big)
    attn = [attn0[:2], _attn_fwd(qkv1, 1), _attn_fwd(qkv2, 2)]
    g_pa, g_pb, g_out, g_up, g_down = [own_slot(g, own) for g, own in zip(attn0[2:], later_b)]
    g_bgate, g_convw, g_ffnw = [own_slot(g, own) for g, own in zip(gathered_small, small_sharded)]
    w_pa_full, w_pb_full, w_up_full = _gathered_columns(g_pa), _gathered_columns(g_pb), _gathered_columns(g_up)
    w_out_full, w_down_full = g_out.reshape(D_MODEL, D_MODEL), g_down.reshape(D_FF, D_MODEL)
    b_gate_full, conv_w_full, ffn_w_full = (_gathered_columns(g) for g in (g_bgate, g_convw, g_ffnw))

    x1, ya0, yb0, mrg, ya, yb, lsetot = _mix_fwd(
        xs, abcv, gates, [a[0] for a in attn], [a[1] for a in attn], conv_w_full, conv_a_b, b_gate_full,
        w_pa_full, w_pb_full, w_out_full)
    h2, up0, up = _ffn_up_fwd(x1, norm_ffn_g, w_up_full, ffn_w_full, ffn_conv_b)
    act, dx2, dx2b, d_g_final, loss = _ffn_act_fwd(x1, up, target, w_down_full, g_final)

    d_up = _ffn_act_bwd(dx2b, up, w_down_full)
    slab_down = _matmul_tn("dw_down", act, dx2b, col_tile=512).reshape(N_CHIPS, D_FF // N_CHIPS, D_MODEL)
    dx1, dx1b, d_g_ffn, d_up0, ffn_small, swapped_down = _ffn_up_bwd(
        d_up, up0, w_up_full, ffn_w_full, x1, norm_ffn_g, dx2, swap=[slab_down])
    (partial_down,) = _add_sibling("grad_add_w_down", [slab_down], [swapped_down], core_arr)
    slab_up = _matmul_tn("dw_up", h2, d_up0, col_tile=2 * D_FF // N_CHIPS, slabs=N_CHIPS)
    d_w_out, swapped_up = _matmul_tn("dw_out", mrg, dx1b, swap=[slab_up])
    (partial_up,) = _add_sibling("grad_add_w_up", [slab_up], [swapped_up], core_arr)

    mix_res = _mix_bwd(dx1, abcv, gates, ya, yb, yb0, lsetot, conv_w_full, conv_a_b, b_gate_full,
                       w_pa_full, w_pb_full, w_out_full, exchange=[partial_up, partial_down])
    (d_ya, d_yb, d_gates, d_abcv, d_yb0, dyl0, dyl1, dyl2, aux0, aux1, aux2, gate_small, conv_small) = mix_res[:13]
    halves_ffn = _sum_chips("grad_sum_ffn", [partial_up, partial_down], mix_res[13:], chip_core)

    slabs_mix = [_matmul_tn("dw_proj_a", ya0, d_ya, slabs=N_CHIPS), _matmul_tn("dw_proj_b", yb0, d_yb, slabs=N_CHIPS),
                 d_w_out.reshape(N_CHIPS, D_MODEL // N_CHIPS, D_MODEL)]
    res0 = _attn_bwd(qkv0, dyl0, aux0, 0, swap=slabs_mix)
    d_qkv0, partials_mix = res0[0], _add_sibling("grad_add_mix", slabs_mix, res0[1:], core_arr)
    res1 = _attn_bwd(qkv1, dyl1, aux1, 1, exchange=partials_mix)
    d_qkv1, halves_mix = res1[0], _sum_chips("grad_sum_mix", partials_mix, res1[1:], chip_core)
    (d_qkv2,) = _attn_bwd(qkv2, dyl2, aux2, 2)

    dq = [d_qkv0, d_qkv1, d_qkv2]
    seq = xs.shape[0]
    d_w_abcv = _matmul_tn("dw_in_abcv", d_abcv, h1)
    d_w_gates = _matmul_tn("dw_in_gates", d_gates, h1)
    d_w_groups = [_matmul_tn(f"dw_in_qkv{g}", t.reshape(seq, 3 * GROUP_WIDTH), h.reshape(seq, D_MODEL))
                  for g, (t, h) in enumerate(zip(dq, (h1, h1_streams4, h1_streams16)))]
    gw = GROUP_WIDTH
    d_w_in_t = jnp.concatenate(
        [d_w_abcv] + [d_w_groups[g][j * gw:(j + 1) * gw] for j in range(3) for g in range(3)] + [d_w_gates], axis=0)

    slab_in = d_w_in_t.reshape(N_CHIPS, D_IN // N_CHIPS, D_MODEL)
    (from_sibling_in,) = _sibling_swap_halves("grad_swap_w_in", [slab_in])
    (partial_in,) = _add_sibling("grad_add_w_in", [slab_in], [from_sibling_in], core_arr)
    send_sems, recv_sems, partial_thru, landing_thru, token = _chip_exchange_start(partial_in)
    g_mix_after_start = norm_mix_g + token[0:1, 0:1]
    grad_x, d_g_mix = _inproj_bwd(d_abcv, d_gates, dq, w_in_full_t, xs, g_mix_after_start, dx1)
    partial_in, received_in = _chip_exchange_wait(send_sems, recv_sems, partial_thru, landing_thru, d_g_mix)
    halves_in = _sum_chips("grad_sum_w_in", [partial_in], [received_in], chip_core)

    big_names = ("w_in", "w_proj_a", "w_proj_b", "w_out", "w_up", "w_down")
    big_grads = _sibling_share(halves_in + halves_mix + halves_ffn)
    big_w = dict(w_in=w_in_t, w_proj_a=w_proj_a, w_proj_b=w_proj_b, w_out=w_out, w_up=w_up, w_down=w_down)
    big_m = dict(w_in=m_w_in_t, w_proj_a=m_w_proj_a, w_proj_b=m_w_proj_b, w_out=m_w_out, w_up=m_w_up, w_down=m_w_down)
    big_v = dict(w_in=v_w_in_t, w_proj_a=v_w_proj_a, w_proj_b=v_w_proj_b, w_out=v_w_out, w_up=v_w_up, w_down=v_w_down)

    fin_w, fin_m, fin_v = (a.reshape(1, D_MODEL) for a in (final_norm_g, m_final_norm_g, v_final_norm_g))
    small_w = [norm_mix_g, b_gate, conv_a_w, conv_a_b, norm_ffn_g, ffn_conv_w, ffn_conv_b, fin_w]
    small_m = [m_norm_mix_g, m_b_gate, m_conv_a_w, m_conv_a_b, m_norm_ffn_g, m_ffn_conv_w, m_ffn_conv_b, fin_m]
    small_v = [v_norm_mix_g, v_b_gate, v_conv_a_w, v_conv_a_b, v_norm_ffn_g, v_ffn_conv_w, v_ffn_conv_b, fin_v]
    small_out = _small_update([d_g_mix, d_g_ffn, d_g_final, gate_small, conv_small, ffn_small, loss],
                              small_w, small_m, small_v)
    total_loss = small_out[0][0, 0]

    grads, delta, new_m, new_v = {}, {}, {}, {}
    for i, n in enumerate(SMALL_PARAMS):
        vals = small_out[1 + 4 * i:5 + 4 * i]
        if n == "final_norm_g":
            vals = [a.reshape(D_MODEL) for a in vals]
        grads[n], delta[n], new_m[n], new_v[n] = vals
    for n, g in zip(big_names, big_grads):
        vals = _adamw("adamw_" + n, big_w[n], g, big_m[n], big_v[n])
        if n == "w_in":
            vals = [jnp.swapaxes(a, 1, 2) for a in vals]
        grads[n], delta[n], new_m[n], new_v[n] = vals

    names = ["norm_mix_g", "w_in", "b_gate", "conv_a_w", "conv_a_b", "w_proj_a", "w_proj_b", "w_out", "norm_ffn_g", "w_up",
             "ffn_conv_w", "ffn_conv_b", "w_down", "final_norm_g"]
    out = [total_loss, grad_x[None]]
    for group in (grads, delta, new_m, new_v):
        out += [group[n] for n in names]
    return tuple(out)
```

```python
import jax
import jax.numpy as jnp
from jax import lax
from jax.experimental import pallas as pl
from jax.experimental.pallas import tpu as pltpu

F32 = jnp.float32
BF16 = jnp.bfloat16

D_MODEL = 1024
CONV_WIDTH = 512
ATTN_WIDTH = 768
GROUP_WIDTH = 256
HEAD_DIM = 64
HEADS_PER_GROUP = 4
DILATIONS = (1, 4, 16)
ATTN_BLOCK = 128
D_FF = 2816
D_IN = 5888
EPS = 1e-6
NEG_INF = -1e30
ATTN_SCALE = HEAD_DIM ** -0.5

COL_ABCV = 0
COL_Q = 1536
COL_K = 2304
COL_V = 3072
COL_GATES = 3840

ADAM_LR = 0.001
ADAM_B1 = 0.9
ADAM_B2 = 0.999
ADAM_EPS = 1e-08
ADAM_WD = 0.01
ADAM_STEP = 10

LANES = 128
SUBLANES = 8
BF16_ROWS = 16
ROW_TILE = 512
VMEM_LIMIT = 56 * 1024 * 1024

_NT = (((1,), (1,)), ((), ()))
_TN = (((0,), (0,)), ((), ()))


def _params(n_axes, vmem=VMEM_LIMIT):
    return pltpu.CompilerParams(dimension_semantics=("arbitrary",) * n_axes, vmem_limit_bytes=vmem)


def _resident(shape):
    nd = len(shape)
    return pl.BlockSpec(shape, lambda *_: (0,) * nd, pipeline_mode=pl.Buffered(1))


def _rows(tm, width, col_block=0):
    return pl.BlockSpec((tm, width), lambda i: (i, col_block))


def _col_chunks(n, cmax):
    out, lo = [], 0
    while lo < n:
        size = min(cmax, n - lo)
        out.append((lo, size))
        lo += size
    return out


def _dot(a, b):
    return jnp.dot(a, b, preferred_element_type=F32)


def _dot_nt(a, b):
    return lax.dot_general(a, b, _NT, preferred_element_type=F32)


def _dot_tn(a, b):
    return lax.dot_general(a, b, _TN, preferred_element_type=F32)


def _sigmoid(x):
    return 0.5 * jnp.tanh(0.5 * x) + 0.5


def _silu(x):
    hx = 0.5 * x
    return hx + hx * jnp.tanh(hx)


def _shift_down(v, k, halo8):
    tm = v.shape[0]
    rolled = pltpu.roll(v, k, 0)
    fix = jnp.tile(pltpu.roll(halo8, k, 0), (tm // SUBLANES, 1))
    row = lax.broadcasted_iota(jnp.int32, v.shape, 0)
    return jnp.where(row < k, fix, rolled)


def _shift_up(v, k, halo8):
    tm = v.shape[0]
    rolled = pltpu.roll(v, tm - k, 0)
    fix = jnp.tile(pltpu.roll(halo8, SUBLANES - k, 0), (tm // SUBLANES, 1))
    row = lax.broadcasted_iota(jnp.int32, v.shape, 0)
    return jnp.where(row >= tm - k, fix, rolled)


def _colsum(v):
    return jnp.sum(v, axis=0, keepdims=True)


def _to_streams(val, scr, out_ref, d, col0):
    tm = val.shape[0]
    panels = val.shape[1] // LANES
    if d == 1:
        out_ref[0, :, col0:col0 + val.shape[1]] = val.astype(out_ref.dtype)
        return
    for p in range(panels):
        scr[pl.ds(p * tm, tm), :] = val[:, p * LANES:(p + 1) * LANES]
    for r in range(d):
        for p in range(panels):
            piece = scr[pl.ds(p * tm + r, tm // d, stride=d), :]
            out_ref[r, :, col0 + p * LANES: col0 + (p + 1) * LANES] = piece.astype(out_ref.dtype)


def _from_streams(in_ref, scr, d, col0, width):
    panels = width // LANES
    rows = in_ref.shape[1]
    tm = rows * d
    if d == 1:
        return in_ref[0, :, col0:col0 + width].astype(F32)
    for r in range(d):
        for p in range(panels):
            scr[pl.ds(p * tm + r, rows, stride=d), :] = in_ref[r, :, col0 + p * LANES: col0 + (p + 1) * LANES].astype(F32)
    return jnp.concatenate([scr[pl.ds(p * tm, tm), :] for p in range(panels)], axis=1)


def _stream_block(tm, d, width):
    return pl.BlockSpec((d, tm // d, width), lambda i: (0, i, 0))


def _rev_stream_block(tm, d, width, nt):
    return pl.BlockSpec((d, tm // d, width), lambda i: (0, nt - 1 - i, 0))


N_CHIPS = 4
MESH_ID = pl.DeviceIdType.MESH
_ANY = pl.BlockSpec(memory_space=pl.ANY)
_VMEM = pl.BlockSpec(memory_space=pltpu.VMEM)


def _mesh_position():
    x, y, c = lax.axis_index("x"), lax.axis_index("y"), lax.axis_index("c")
    other_chips = [(1 - x, y), (x, 1 - y), (1 - x, 1 - y)]
    return x, y, c, other_chips


def _half_rows(c, half):
    return pl.ds(pl.multiple_of(c * half, BF16_ROWS), half)


def _remote_copy(k, src, dst, to, send_sems, recv_sems):
    return pltpu.make_async_remote_copy(src_ref=src, dst_ref=dst, send_sem=send_sems.at[k], recv_sem=recv_sems.at[k],
                                        device_id=to, device_id_type=MESH_ID)


def _gather_first_copies(big_refs, small_refs, big_outs, small_outs, send_sems, recv_sems):
    x, y, c, chips = _mesh_position()
    me = 2 * x + y
    nb = len(big_refs)
    cps = []
    for j, (px, py) in enumerate(chips):
        for b in range(nb):
            mine = _half_rows(c, big_refs[b].shape[1] // 2)
            cps.append(_remote_copy(3 * b + j, big_refs[b].at[0, mine], big_outs[b].at[me, mine], (px, py, c),
                                    send_sems, recv_sems))
        for s in range(len(small_refs)):
            cps.append(_remote_copy(3 * (nb + s) + j, small_refs[s].at[0], small_outs[s].at[me], (px, py, c),
                                    send_sems, recv_sems))
    return cps


def _gather_forward_copies(bufs, send_sems, recv_sems):
    x, y, c, chips = _mesh_position()
    cps = []
    for j, (px, py) in enumerate(chips):
        for b in range(len(bufs)):
            landed = bufs[b].at[2 * px + py, _half_rows(c, bufs[b].shape[1] // 2)]
            cps.append(_remote_copy(3 * b + j, landed, landed, (x, y, 1 - c), send_sems, recv_sems))
    return cps


def _chip_exchange_copies(src_refs, out_refs, send_sems, recv_sems):
    x, y, c, chips = _mesh_position()
    cps = []
    for j, (px, py) in enumerate(chips):
        for a in range(len(src_refs)):
            cps.append(_remote_copy(3 * a + j, src_refs[a].at[2 * px + py], out_refs[a].at[j], (px, py, c),
                                    send_sems, recv_sems))
    return cps


def _sibling_swap_copies(src_refs, out_refs, send_sems, recv_sems):
    x, y, c, _ = _mesh_position()
    cps = []
    for a in range(len(src_refs)):
        theirs = _half_rows(1 - c, src_refs[a].shape[1] // 2)
        cps.append(_remote_copy(a, src_refs[a].at[:, theirs, :], out_refs[a], (x, y, 1 - c), send_sems, recv_sems))
    return cps


def _swap_shapes(slabs):
    return [jax.ShapeDtypeStruct((a.shape[0], a.shape[1] // 2, a.shape[2]), a.dtype) for a in slabs]


def _dma_sems(n):
    return [pltpu.SemaphoreType.DMA((n,)), pltpu.SemaphoreType.DMA((n,))]


def _norm_fwd(x, g, shard):
    s = x.shape[0]
    tm = ROW_TILE
    nt = s // tm

    def body(x_ref, g_ref, shard_ref, h_ref, hs1_ref, hs2_ref, buf_ref, scr, send1, recv1, send2, recv2):
        i = pl.program_id(0)

        def level_one():
            return _gather_first_copies([shard_ref], [], [buf_ref], [], send1, recv1)

        @pl.when(i == 0)
        def _():
            for cp in level_one():
                cp.start()

        xv = x_ref[...]
        r = lax.rsqrt(jnp.mean(xv * xv, axis=-1, keepdims=True) + EPS)
        hf = xv * r * g_ref[...]
        h_ref[...] = hf.astype(BF16)
        for d, hs_ref in zip(DILATIONS[1:], (hs1_ref, hs2_ref)):
            for lo, size in _col_chunks(D_MODEL, GROUP_WIDTH):
                _to_streams(hf[:, lo:lo + size], scr, hs_ref, d, lo)

        @pl.when(i == nt - 1)
        def _():
            for cp in level_one():
                cp.wait()
            forwards = _gather_forward_copies([buf_ref], send2, recv2)
            for cp in forwards:
                cp.start()
            for cp in forwards:
                cp.wait()

    return pl.pallas_call(
        body, name="norm_fwd", grid=(nt,),
        out_shape=[jax.ShapeDtypeStruct((s, D_MODEL), BF16)]
        + [jax.ShapeDtypeStruct((d, s // d, D_MODEL), BF16) for d in DILATIONS[1:]]
        + [jax.ShapeDtypeStruct((N_CHIPS,) + shard.shape[1:], shard.dtype)],
        in_specs=[_rows(tm, D_MODEL), _resident((1, D_MODEL)), _ANY],
        out_specs=[_rows(tm, D_MODEL)] + [_stream_block(tm, d, D_MODEL) for d in DILATIONS[1:]] + [_ANY],
        scratch_shapes=[pltpu.VMEM((GROUP_WIDTH // LANES * tm, LANES), F32)] + _dma_sems(3) + _dma_sems(3),
        compiler_params=_params(1))(x, g, shard)


def _inproj_fwd(h1, w_in_t, big_shards, small_shards):
    s = h1.shape[0]
    tm = ROW_TILE
    nt = s // tm
    nb, ns = len(big_shards), len(small_shards)
    n_fixed_in, n_fixed_out = 2, 5

    def body(*refs):
        h_ref, w_ref = refs[:n_fixed_in]
        shard_refs = refs[n_fixed_in:n_fixed_in + nb + ns]
        pos = n_fixed_in + nb + ns
        abcv_ref, gates_ref, qkv0_ref, qkv1_ref, qkv2_ref = refs[pos:pos + n_fixed_out]
        gathered_refs = refs[pos + n_fixed_out:pos + n_fixed_out + nb + ns]
        scr, send_sems, recv_sems = refs[pos + n_fixed_out + nb + ns:]
        i = pl.program_id(0)

        def gather_copies():
            return _gather_first_copies(shard_refs[:nb], shard_refs[nb:], gathered_refs[:nb], gathered_refs[nb:],
                                        send_sems, recv_sems)

        @pl.when(i == 0)
        def _():
            for cp in gather_copies():
                cp.start()

        h = h_ref[...]
        for lo, size in _col_chunks(3 * CONV_WIDTH, 512):
            abcv_ref[:, lo:lo + size] = _dot_nt(h, w_ref[COL_ABCV + lo: COL_ABCV + lo + size, :]).astype(BF16)
        for lo, size in _col_chunks(2 * D_MODEL, 512):
            gates_ref[:, lo:lo + size] = _dot_nt(h, w_ref[COL_GATES + lo: COL_GATES + lo + size, :]).astype(BF16)
        for gi, (d, out_ref) in enumerate(zip(DILATIONS, (qkv0_ref, qkv1_ref, qkv2_ref))):
            for j, base in enumerate((COL_Q, COL_K, COL_V)):
                lo = base + gi * GROUP_WIDTH
                y = _dot_nt(h, w_ref[lo:lo + GROUP_WIDTH, :])
                if j == 0:
                    y = y * ATTN_SCALE
                _to_streams(y, scr, out_ref, d, j * GROUP_WIDTH)

        @pl.when(i == nt - 1)
        def _():
            for cp in gather_copies():
                cp.wait()

    outs = [jax.ShapeDtypeStruct((s, 3 * CONV_WIDTH), BF16), jax.ShapeDtypeStruct((s, 2 * D_MODEL), BF16)]
    outs += [jax.ShapeDtypeStruct((d, s // d, 3 * GROUP_WIDTH), BF16) for d in DILATIONS]
    outs += [jax.ShapeDtypeStruct((N_CHIPS,) + a.shape[1:], a.dtype) for a in list(big_shards) + list(small_shards)]
    return pl.pallas_call(
        body, name="inproj_fwd", grid=(nt,), out_shape=outs,
        in_specs=[_rows(tm, D_MODEL), _resident((D_IN, D_MODEL))] + [_ANY] * (nb + ns),
        out_specs=[_rows(tm, 3 * CONV_WIDTH), _rows(tm, 2 * D_MODEL)]
        + [_stream_block(tm, d, 3 * GROUP_WIDTH) for d in DILATIONS] + [_ANY] * (nb + ns),
        scratch_shapes=[pltpu.VMEM((GROUP_WIDTH // LANES * tm, LANES), F32)] + _dma_sems(3 * (nb + ns)),
        compiler_params=_params(1))(h1, w_in_t, *big_shards, *small_shards)


def _head_of_lane(shape):
    return lax.broadcasted_iota(jnp.int32, shape, 1) // HEAD_DIM


def _stack_heads(v):
    head = _head_of_lane(v.shape)
    return jnp.concatenate([jnp.where(head == h, v, jnp.zeros_like(v)) for h in range(HEADS_PER_GROUP)], axis=0)


def _unstack_heads(v):
    q = ATTN_BLOCK
    head = _head_of_lane((q, v.shape[1]))
    out = jnp.zeros((q, v.shape[1]), v.dtype)
    for h in range(HEADS_PER_GROUP):
        out = jnp.where(head == h, v[h * q:(h + 1) * q], out)
    return out


def _per_head_rows(col):
    q = ATTN_BLOCK
    head = _head_of_lane((q, GROUP_WIDTH))
    out = jnp.zeros((q, GROUP_WIDTH), col.dtype)
    for h in range(HEADS_PER_GROUP):
        out = jnp.where(head == h, col[h * q:(h + 1) * q], out)
    return out


def _compact_heads(v):
    lane = lax.broadcasted_iota(jnp.int32, (v.shape[0], LANES), 1)
    return jnp.where((lane & 32) == 0, v[:, 0:LANES], v[:, LANES:2 * LANES])


def _compact_head_col(v):
    lane = lax.broadcasted_iota(jnp.int32, v.shape, 1)
    head = ((lane >> 6) & 1) + 2 * ((lane >> 5) & 1)
    cols = [jnp.max(jnp.where(head == h, v, -jnp.inf), axis=1, keepdims=True) for h in range(HEADS_PER_GROUP)]
    return jnp.concatenate(cols, axis=0)


ATTN_BLOCKS_PER_STEP = 4


def _band_bias(first_block):
    rows = HEADS_PER_GROUP * ATTN_BLOCK
    qi = lax.broadcasted_iota(jnp.int32, (rows, 2 * ATTN_BLOCK), 0) % ATTN_BLOCK
    kj = lax.broadcasted_iota(jnp.int32, (rows, 2 * ATTN_BLOCK), 1)
    dist = qi + ATTN_BLOCK - kj
    valid = (dist >= 0) & (dist <= ATTN_BLOCK)
    if first_block:
        valid = valid & (kj >= ATTN_BLOCK)
    return jnp.where(valid, 0.0, NEG_INF).astype(F32)


def _store_band_biases(bias_ref):
    bias_ref[0] = _band_bias(False)
    bias_ref[1] = _band_bias(True)


def _attn_block_specs(g, nb, clamp_last=False):
    q = ATTN_BLOCK
    last = nb // g - 1

    def cur(col, width=GROUP_WIDTH):
        if clamp_last:
            return pl.BlockSpec((None, g * q, width), lambda r, n: (r, jnp.minimum(n, last), col))
        return pl.BlockSpec((None, g * q, width), lambda r, n: (r, n, col))

    def prev(col):
        if clamp_last:
            return pl.BlockSpec((None, q, GROUP_WIDTH), lambda r, n: (r, jnp.clip(n * g - 1, 0, nb - 1), col))
        return pl.BlockSpec((None, q, GROUP_WIDTH), lambda r, n: (r, jnp.maximum(n * g - 1, 0), col))

    return cur, prev


def _attn_fwd(qkv, gi, forward=()):
    d, length, _ = qkv.shape
    nb = length // ATTN_BLOCK
    q = ATTN_BLOCK
    g = min(ATTN_BLOCKS_PER_STEP, nb)
    ns = nb // g
    nf = len(forward)

    def body(*refs):
        q_ref, kp_ref, kc_ref, vp_ref, vc_ref = refs[:5]
        o_ref, lse_ref = refs[5 + nf:7 + nf]
        buf_refs = refs[7 + nf:7 + 2 * nf]
        bias_ref = refs[7 + 2 * nf]
        sems = refs[8 + 2 * nf:]
        n = pl.program_id(1)
        first_step = (pl.program_id(0) == 0) & (n == 0)
        last_step = (pl.program_id(0) == d - 1) & (n == ns - 1)

        @pl.when(first_step)
        def _():
            _store_band_biases(bias_ref)
            for cp in _gather_forward_copies(buf_refs, *sems) if nf else ():
                cp.start()

        kfull = jnp.concatenate([kp_ref[...], kc_ref[...]], axis=0)
        vfull = jnp.concatenate([vp_ref[...], vc_ref[...]], axis=0)
        for j in range(g):
            qs = _stack_heads(q_ref[j * q:(j + 1) * q, :])
            k2 = kfull[j * q:(j + 2) * q]
            v2 = vfull[j * q:(j + 2) * q]
            bias = jnp.where(n == 0, bias_ref[1], bias_ref[0]) if j == 0 else bias_ref[0]
            sc = _dot_nt(qs, k2) + bias
            m = jnp.max(sc, axis=1, keepdims=True)
            p = jnp.exp(sc - m)
            l = jnp.sum(p, axis=1, keepdims=True)
            of = _dot(p.astype(BF16), v2) / l
            o_ref[j * q:(j + 1) * q, :] = _unstack_heads(of).astype(BF16)
            lse_ref[j * q:(j + 1) * q, :] = _per_head_rows(m + jnp.log(l))

        if nf:
            @pl.when(last_step)
            def _():
                for cp in _gather_forward_copies(buf_refs, *sems):
                    cp.wait()

    cur, prev = _attn_block_specs(g, nb)
    return pl.pallas_call(
        body, name=f"attn_fwd_g{gi}", grid=(d, ns),
        out_shape=[jax.ShapeDtypeStruct((d, length, GROUP_WIDTH), BF16),
                   jax.ShapeDtypeStruct((d, length, GROUP_WIDTH), F32)]
        + [jax.ShapeDtypeStruct(a.shape, a.dtype) for a in forward],
        in_specs=[cur(0), prev(1), cur(1), prev(2), cur(2)] + [_ANY] * nf,
        out_specs=[cur(0), cur(0)] + [_ANY] * nf,
        input_output_aliases={5 + a: 2 + a for a in range(nf)},
        scratch_shapes=[pltpu.VMEM((2, HEADS_PER_GROUP * q, 2 * q), F32)] + (_dma_sems(3 * nf) if nf else []),
        compiler_params=_params(2))(qkv, qkv, qkv, qkv, qkv, *forward)


def _conv_branch(ab, ac, av, halo_u, w, b):
    u = ac * av
    sh1 = _shift_down(u, 1, halo_u)
    sh2 = _shift_down(u, 2, halo_u)
    cv = w[0:1] * sh2 + w[1:2] * sh1 + w[2:3] * u + b
    return ab * cv, cv, u, sh1, sh2


def _mix_fwd(x, abcv, gates, o_list, lse_list, conv_w, conv_b, b_gate, w_pa, w_pb, w_out):
    s = x.shape[0]
    tm = ROW_TILE

    def body(x_ref, abcv_ref, gates_ref, o0_ref, o1_ref, o2_ref, l0_ref, l1_ref, l2_ref,
             cw_ref, cb_ref, bg_ref, wpa_ref, wpb_ref, wout_ref,
             x1_ref, ya0_ref, yb0_ref, mrg_ref, ya_ref, yb_ref, lsetot_ref, halo_ref, scr):
        i = pl.program_id(0)

        @pl.when(i == 0)
        def _():
            halo_ref[...] = jnp.zeros_like(halo_ref)

        ab = abcv_ref[:, 0:CONV_WIDTH].astype(F32)
        ac = abcv_ref[:, CONV_WIDTH:2 * CONV_WIDTH].astype(F32)
        av = abcv_ref[:, 2 * CONV_WIDTH:3 * CONV_WIDTH].astype(F32)
        ya0, _, u, _, _ = _conv_branch(ab, ac, av, halo_ref[...], cw_ref[...], cb_ref[...])
        halo_ref[...] = u[tm - SUBLANES:tm]
        ya0 = ya0.astype(BF16)
        ya0_ref[...] = ya0
        ya = _dot(ya0, wpa_ref[...])

        o_refs, l_refs = (o0_ref, o1_ref, o2_ref), (l0_ref, l1_ref, l2_ref)
        lses = [_from_streams(l_refs[g], scr, DILATIONS[g], 0, GROUP_WIDTH) for g in range(3)]
        top = jnp.maximum(jnp.maximum(lses[0], lses[1]), lses[2])
        weights = [jnp.exp(lse - top) for lse in lses]
        total = weights[0] + weights[1] + weights[2]
        lsetot_ref[...] = top + jnp.log(total)
        inv_total = 1.0 / total
        yb = jnp.zeros((tm, D_MODEL), F32)
        for g in range(3):
            og = _from_streams(o_refs[g], scr, DILATIONS[g], 0, GROUP_WIDTH)
            yb0 = (weights[g] * inv_total * og).astype(BF16)
            yb0_ref[:, g * GROUP_WIDTH:(g + 1) * GROUP_WIDTH] = yb0
            yb = yb + _dot(yb0, wpb_ref[g * GROUP_WIDTH:(g + 1) * GROUP_WIDTH, :])

        sa = _sigmoid(gates_ref[:, 0:D_MODEL].astype(F32) + bg_ref[0:1, :])
        sb = _sigmoid(gates_ref[:, D_MODEL:2 * D_MODEL].astype(F32) + bg_ref[1:2, :])
        ya_ref[...] = ya.astype(BF16)
        yb_ref[...] = yb.astype(BF16)
        mrg = (sa * ya + sb * yb).astype(BF16)
        mrg_ref[...] = mrg
        x1_ref[...] = x_ref[...] + _dot(mrg, wout_ref[...])

    outs = [jax.ShapeDtypeStruct((s, D_MODEL), F32),
            jax.ShapeDtypeStruct((s, CONV_WIDTH), BF16),
            jax.ShapeDtypeStruct((s, ATTN_WIDTH), BF16),
            jax.ShapeDtypeStruct((s, D_MODEL), BF16),
            jax.ShapeDtypeStruct((s, D_MODEL), BF16),
            jax.ShapeDtypeStruct((s, D_MODEL), BF16),
            jax.ShapeDtypeStruct((s, GROUP_WIDTH), F32)]
    return pl.pallas_call(
        body, name="mix_fwd", grid=(s // tm,), out_shape=outs,
        in_specs=[_rows(tm, D_MODEL), _rows(tm, 3 * CONV_WIDTH), _rows(tm, 2 * D_MODEL)]
        + [_stream_block(tm, d, GROUP_WIDTH) for d in DILATIONS] * 2
        + [_resident((3, CONV_WIDTH)), _resident((1, CONV_WIDTH)), _resident((2, D_MODEL)),
           _resident((CONV_WIDTH, D_MODEL)), _resident((ATTN_WIDTH, D_MODEL)), _resident((D_MODEL, D_MODEL))],
        out_specs=[_rows(tm, D_MODEL), _rows(tm, CONV_WIDTH), _rows(tm, ATTN_WIDTH), _rows(tm, D_MODEL),
                   _rows(tm, D_MODEL), _rows(tm, D_MODEL), _rows(tm, GROUP_WIDTH)],
        scratch_shapes=[pltpu.VMEM((SUBLANES, CONV_WIDTH), F32),
                        pltpu.VMEM((GROUP_WIDTH // LANES * tm, LANES), F32)],
        compiler_params=_params(1))(x, abcv, gates, *o_list, *lse_list, conv_w, conv_b, b_gate, w_pa, w_pb, w_out)


FFN_CHUNK = 512
FFN_UP_ROW_TILE = 256


def _ffn_up_fwd(x1, g, w_up, conv_w, conv_b):
    s = x1.shape[0]
    n = w_up.shape[1]
    tm = FFN_UP_ROW_TILE

    def body(x_ref, g_ref, w_ref, cw_ref, cb_ref, h_ref, up0_ref, up_ref, halo_ref):
        @pl.when(pl.program_id(0) == 0)
        def _():
            halo_ref[...] = jnp.zeros_like(halo_ref)

        xv = x_ref[...]
        r = lax.rsqrt(jnp.mean(xv * xv, axis=-1, keepdims=True) + EPS)
        h = (xv * r * g_ref[...]).astype(BF16)
        h_ref[...] = h
        for lo, size in _col_chunks(n, FFN_CHUNK):
            cols = slice(lo, lo + size)
            y = _dot(h, w_ref[:, cols])
            up0_ref[:, cols] = y.astype(BF16)
            halo = halo_ref[:, cols]
            w = cw_ref[:, cols]
            up = w[0:1] * _shift_down(y, 2, halo) + w[1:2] * _shift_down(y, 1, halo) + w[2:3] * y + cb_ref[:, cols]
            up_ref[:, cols] = up.astype(BF16)
            halo_ref[:, cols] = y[tm - SUBLANES:tm]

    return pl.pallas_call(
        body, name="ffn_up_fwd", grid=(s // tm,),
        out_shape=[jax.ShapeDtypeStruct((s, D_MODEL), BF16), jax.ShapeDtypeStruct((s, n), BF16),
                   jax.ShapeDtypeStruct((s, n), BF16)],
        in_specs=[_rows(tm, D_MODEL), _resident((1, D_MODEL)), _resident((D_MODEL, n)), _resident((3, n)),
                  _resident((1, n))],
        out_specs=[_rows(tm, D_MODEL), _rows(tm, n), _rows(tm, n)],
        scratch_shapes=[pltpu.VMEM((SUBLANES, n), F32)],
        compiler_params=_params(1))(x1, g, w_up, conv_w, conv_b)


def _ffn_act_fwd(x1, up, target, w_down, g_final):
    s = x1.shape[0]
    tm = ROW_TILE

    def body(x1_ref, up_ref, tgt_ref, wd_ref, gf_ref, act_ref, dx2_ref, dx2b_ref, dgf_ref, loss_ref):
        @pl.when(pl.program_id(0) == 0)
        def _():
            dgf_ref[...] = jnp.zeros_like(dgf_ref)
            loss_ref[...] = jnp.zeros_like(loss_ref)

        acc = jnp.zeros((tm, D_MODEL), F32)
        for lo, size in _col_chunks(D_FF, FFN_CHUNK):
            gate = up_ref[:, lo:lo + size].astype(F32)
            val = up_ref[:, D_FF + lo:D_FF + lo + size].astype(F32)
            act = (_silu(gate) * val).astype(BF16)
            act_ref[:, lo:lo + size] = act
            acc = acc + _dot(act, wd_ref[lo:lo + size, :])

        x2 = x1_ref[...] + acc
        r = lax.rsqrt(jnp.mean(x2 * x2, axis=-1, keepdims=True) + EPS)
        xn = x2 * r
        gf = gf_ref[...]
        err = xn * gf - tgt_ref[...]
        loss_ref[...] += (0.5 / D_MODEL) * jnp.sum(err * err)
        dy = err * (1.0 / D_MODEL)
        dgf_ref[...] += _colsum(dy * xn)
        dxn = dy * gf
        dx2 = r * (dxn - xn * jnp.mean(dxn * xn, axis=-1, keepdims=True))
        dx2_ref[...] = dx2
        dx2b_ref[...] = dx2.astype(BF16)

    return pl.pallas_call(
        body, name="ffn_act_fwd", grid=(s // tm,),
        out_shape=[jax.ShapeDtypeStruct((s, D_FF), BF16), jax.ShapeDtypeStruct((s, D_MODEL), F32),
                   jax.ShapeDtypeStruct((s, D_MODEL), BF16),
                   jax.ShapeDtypeStruct((1, D_MODEL), F32), jax.ShapeDtypeStruct((1, LANES), F32)],
        in_specs=[_rows(tm, D_MODEL), _rows(tm, 2 * D_FF), _rows(tm, D_MODEL),
                  _resident((D_FF, D_MODEL)), _resident((1, D_MODEL))],
        out_specs=[_rows(tm, D_FF), _rows(tm, D_MODEL), _rows(tm, D_MODEL),
                   pl.BlockSpec((1, D_MODEL), lambda i: (0, 0)), pl.BlockSpec((1, LANES), lambda i: (0, 0))],
        compiler_params=_params(1))(x1, up, target, w_down, g_final)


def _ffn_act_bwd(dx2b, up, w_down):
    s = dx2b.shape[0]
    tm = ROW_TILE

    def body(dx2_ref, up_ref, wd_ref, dup_ref):
        dx2 = dx2_ref[...]
        for lo, size in _col_chunks(D_FF, FFN_CHUNK):
            gate = up_ref[:, lo:lo + size].astype(F32)
            val = up_ref[:, D_FF + lo:D_FF + lo + size].astype(F32)
            dact = _dot_nt(dx2, wd_ref[lo:lo + size, :])
            sg = _sigmoid(gate)
            dup_ref[:, lo:lo + size] = (dact * val * (sg * (1.0 + gate * (1.0 - sg)))).astype(BF16)
            dup_ref[:, D_FF + lo:D_FF + lo + size] = (dact * (gate * sg)).astype(BF16)

    return pl.pallas_call(
        body, name="ffn_act_bwd", grid=(s // tm,),
        out_shape=jax.ShapeDtypeStruct((s, 2 * D_FF), BF16),
        in_specs=[_rows(tm, D_MODEL), _rows(tm, 2 * D_FF), _resident((D_FF, D_MODEL))],
        out_specs=_rows(tm, 2 * D_FF),
        compiler_params=_params(1))(dx2b, up, w_down)


def _ffn_up_bwd(d_up, up0, w_up, conv_w, x1, g, dres, swap=()):
    s = x1.shape[0]
    n = w_up.shape[1]
    tm = FFN_UP_ROW_TILE
    nt = s // tm
    nw = len(swap)

    def body(*refs):
        dup_ref, up0_ref, w_ref, cw_ref, x_ref, g_ref, dres_ref = refs[:7]
        slab_refs = refs[7:7 + nw]
        dx_ref, dxb_ref, dg_ref, dup0_ref, small_ref = refs[7 + nw:12 + nw]
        swapped_refs = refs[12 + nw:12 + 2 * nw]
        next_ref = refs[12 + 2 * nw]
        sems = refs[13 + 2 * nw:]

        @pl.when(pl.program_id(0) == 0)
        def _():
            next_ref[...] = jnp.zeros_like(next_ref)
            small_ref[...] = jnp.zeros_like(small_ref)
            dg_ref[...] = jnp.zeros_like(dg_ref)
            for cp in _sibling_swap_copies(slab_refs, swapped_refs, *sems) if nw else ():
                cp.start()

        dh = jnp.zeros((tm, D_MODEL), F32)
        for lo, size in _col_chunks(n, FFN_CHUNK):
            cols = slice(lo, lo + size)
            dz = dup_ref[:, cols].astype(F32)
            x0 = up0_ref[:, cols].astype(F32)
            nxt = next_ref[:, cols]
            dz1 = _shift_up(dz, 1, nxt)
            dz2 = _shift_up(dz, 2, nxt)
            next_ref[:, cols] = dz[0:SUBLANES]
            small_ref[0:1, cols] += _colsum(dz2 * x0)
            small_ref[1:2, cols] += _colsum(dz1 * x0)
            small_ref[2:3, cols] += _colsum(dz * x0)
            small_ref[3:4, cols] += _colsum(dz)
            w = cw_ref[:, cols]
            dup0 = (w[2:3] * dz + w[1:2] * dz1 + w[0:1] * dz2).astype(BF16)
            dup0_ref[:, cols] = dup0
            dh = dh + _dot_nt(dup0, w_ref[:, cols])
        xv = x_ref[...]
        r = lax.rsqrt(jnp.mean(xv * xv, axis=-1, keepdims=True) + EPS)
        xn = xv * r
        dg_ref[...] += _colsum(dh * xn)
        dxn = dh * g_ref[...]
        dx = dres_ref[...] + r * (dxn - xn * jnp.mean(dxn * xn, axis=-1, keepdims=True))
        dx_ref[...] = dx
        dxb_ref[...] = dx.astype(BF16)

        if nw:
            @pl.when(pl.program_id(0) == nt - 1)
            def _():
                for cp in _sibling_swap_copies(slab_refs, swapped_refs, *sems):
                    cp.wait()

    rows = lambda width: pl.BlockSpec((tm, width), lambda i: (nt - 1 - i, 0))
    return pl.pallas_call(
        body, name="ffn_up_bwd", grid=(nt,),
        out_shape=[jax.ShapeDtypeStruct((s, D_MODEL), F32), jax.ShapeDtypeStruct((s, D_MODEL), BF16),
                   jax.ShapeDtypeStruct((1, D_MODEL), F32), jax.ShapeDtypeStruct((s, n), BF16),
                   jax.ShapeDtypeStruct((SUBLANES, n), F32)] + _swap_shapes(swap),
        in_specs=[rows(n), rows(n), _resident((D_MODEL, n)), _resident((3, n)), rows(D_MODEL),
                  _resident((1, D_MODEL)), rows(D_MODEL)] + [_ANY] * nw,
        out_specs=[rows(D_MODEL), rows(D_MODEL), pl.BlockSpec((1, D_MODEL), lambda i: (0, 0)), rows(n),
                   pl.BlockSpec((SUBLANES, n), lambda i: (0, 0))] + [_ANY] * nw,
        scratch_shapes=[pltpu.VMEM((SUBLANES, n), F32)] + (_dma_sems(nw) if nw else []),
        compiler_params=_params(1))(d_up, up0, w_up, conv_w, x1, g, dres, *swap)


def _inproj_bwd(d_abcv, d_gates, d_qkvs, w_in_t, x, g, dres):
    s = x.shape[0]
    tm = ROW_TILE
    gw = GROUP_WIDTH

    def body(dabcv_ref, dgates_ref, dq0_ref, dq1_ref, dq2_ref, w_ref, x_ref, g_ref, dres_ref, dx_ref, dg_ref, scr):
        @pl.when(pl.program_id(0) == 0)
        def _():
            dg_ref[...] = jnp.zeros_like(dg_ref)

        dh = jnp.zeros((tm, D_MODEL), F32)
        for src, width, wrow in ((dabcv_ref, 3 * CONV_WIDTH, COL_ABCV), (dgates_ref, 2 * D_MODEL, COL_GATES)):
            for lo, size in _col_chunks(width, 512):
                dh = dh + _dot(src[:, lo:lo + size], w_ref[wrow + lo:wrow + lo + size, :])
        for gi, (d, dq_ref) in enumerate(zip(DILATIONS, (dq0_ref, dq1_ref, dq2_ref))):
            for j, base in enumerate((COL_Q, COL_K, COL_V)):
                dy = _from_streams(dq_ref, scr, d, j * gw, gw).astype(BF16)
                wrow = base + gi * gw
                dh = dh + _dot(dy, w_ref[wrow:wrow + gw, :])
        xv = x_ref[...]
        r = lax.rsqrt(jnp.mean(xv * xv, axis=-1, keepdims=True) + EPS)
        xn = xv * r
        dg_ref[...] += _colsum(dh * xn)
        dxn = dh * g_ref[...]
        dx_ref[...] = dres_ref[...] + r * (dxn - xn * jnp.mean(dxn * xn, axis=-1, keepdims=True))

    return pl.pallas_call(
        body, name="inproj_bwd", grid=(s // tm,),
        out_shape=[jax.ShapeDtypeStruct((s, D_MODEL), F32), jax.ShapeDtypeStruct((1, D_MODEL), F32)],
        in_specs=[_rows(tm, 3 * CONV_WIDTH), _rows(tm, 2 * D_MODEL)]
        + [_stream_block(tm, d, 3 * gw) for d in DILATIONS]
        + [_resident((D_IN, D_MODEL)), _rows(tm, D_MODEL), _resident((1, D_MODEL)), _rows(tm, D_MODEL)],
        out_specs=[_rows(tm, D_MODEL), pl.BlockSpec((1, D_MODEL), lambda i: (0, 0))],
        scratch_shapes=[pltpu.VMEM((gw // LANES * tm, LANES), F32)],
        compiler_params=_params(1))(d_abcv, d_gates, *d_qkvs, w_in_t, x, g, dres)


def _mix_bwd(dx1, abcv, gates, ya, yb, yb0, lsetot, conv_w, conv_b, b_gate, w_pa, w_pb, w_out, exchange=()):
    s = dx1.shape[0]
    tm = ROW_TILE
    nt = s // tm
    hb = tm // (2 * SUBLANES)
    nx = len(exchange)

    def body(*refs):
        (dx1_ref, abcv_ref, pre_ref, gates_ref, ya_ref, yb_ref, yb0_ref, lsetot_ref,
         cw_ref, cb_ref, bg_ref, wpa_ref, wpb_ref, wout_ref) = refs[:14]
        part_refs = refs[14:14 + nx]
        (dya_ref, dyb_ref, dgates_ref, dabcv_ref, dyb0_ref, dyl0_ref, dyl1_ref, dyl2_ref, aux0_ref, aux1_ref,
         aux2_ref, sm_gate_ref, sm_conv_ref) = refs[14 + nx:27 + nx]
        recv_refs = refs[27 + nx:27 + 2 * nx]
        next_ref, scr = refs[27 + 2 * nx:29 + 2 * nx]
        sems = refs[29 + 2 * nx:]
        i = pl.program_id(0)

        @pl.when(i == 0)
        def _():
            next_ref[...] = jnp.zeros_like(next_ref)
            sm_gate_ref[...] = jnp.zeros_like(sm_gate_ref)
            sm_conv_ref[...] = jnp.zeros_like(sm_conv_ref)
            for cp in _chip_exchange_copies(part_refs, recv_refs, *sems) if nx else ():
                cp.start()

        not_first = (i < nt - 1).astype(F32)
        dm = _dot_nt(dx1_ref[...].astype(BF16), wout_ref[...])
        sa = _sigmoid(gates_ref[:, 0:D_MODEL].astype(F32) + bg_ref[0:1, :])
        sb = _sigmoid(gates_ref[:, D_MODEL:2 * D_MODEL].astype(F32) + bg_ref[1:2, :])
        dya = (dm * sa).astype(BF16)
        dyb = (dm * sb).astype(BF16)
        dya_ref[...] = dya
        dyb_ref[...] = dyb
        dga = dm * ya_ref[...].astype(F32) * (sa * (1.0 - sa))
        dgb = dm * yb_ref[...].astype(F32) * (sb * (1.0 - sb))
        dgates_ref[:, 0:D_MODEL] = dga.astype(BF16)
        dgates_ref[:, D_MODEL:2 * D_MODEL] = dgb.astype(BF16)
        sm_gate_ref[0:1, :] += _colsum(dga)
        sm_gate_ref[1:2, :] += _colsum(dgb)

        dya0 = _dot_nt(dya, wpa_ref[...])
        ab = abcv_ref[:, 0:CONV_WIDTH].astype(F32)
        ac = abcv_ref[:, CONV_WIDTH:2 * CONV_WIDTH].astype(F32)
        av = abcv_ref[:, 2 * CONV_WIDTH:3 * CONV_WIDTH].astype(F32)
        pre = pre_ref[...].astype(F32) * not_first
        halo_u = (pre[:, CONV_WIDTH:2 * CONV_WIDTH] * pre[:, 2 * CONV_WIDTH:3 * CONV_WIDTH])[SUBLANES:2 * SUBLANES]
        w = cw_ref[...]
        _, cv, u, sh1, sh2 = _conv_branch(ab, ac, av, halo_u, w, cb_ref[...])
        dcv = dya0 * ab
        sm_conv_ref[0:1, :] += _colsum(dcv * sh2)
        sm_conv_ref[1:2, :] += _colsum(dcv * sh1)
        sm_conv_ref[2:3, :] += _colsum(dcv * u)
        sm_conv_ref[3:4, :] += _colsum(dcv)
        nxt = next_ref[...]
        du = w[2:3] * dcv + w[1:2] * _shift_up(dcv, 1, nxt) + w[0:1] * _shift_up(dcv, 2, nxt)
        next_ref[...] = dcv[0:SUBLANES]
        dabcv_ref[:, 0:CONV_WIDTH] = (dya0 * cv).astype(BF16)
        dabcv_ref[:, CONV_WIDTH:2 * CONV_WIDTH] = (du * av).astype(BF16)
        dabcv_ref[:, 2 * CONV_WIDTH:3 * CONV_WIDTH] = (du * ac).astype(BF16)

        head_r = lax.broadcasted_iota(jnp.int32, (GROUP_WIDTH, GROUP_WIDTH), 0) // HEAD_DIM
        head_c = lax.broadcasted_iota(jnp.int32, (GROUP_WIDTH, GROUP_WIDTH), 1) // HEAD_DIM
        same_head = (head_r == head_c).astype(BF16)
        prod = jnp.zeros((tm, GROUP_WIDTH), F32)
        dyb0s = []
        for g in range(3):
            cols = slice(g * GROUP_WIDTH, (g + 1) * GROUP_WIDTH)
            dyb0 = _dot_nt(dyb, wpb_ref[cols, :])
            dyb0_ref[:, cols] = dyb0.astype(BF16)
            dyb0s.append(dyb0)
            prod = prod + dyb0 * yb0_ref[:, cols].astype(F32)
        hi = prod.astype(BF16)
        mid = (prod - hi.astype(F32)).astype(BF16)
        lo = (prod - hi.astype(F32) - mid.astype(F32)).astype(BF16)
        delta = _dot(hi, same_head) + _dot(mid, same_head) + _dot(lo, same_head)
        lse_c = _compact_heads(lsetot_ref[...])
        delta_c = _compact_heads(delta)
        for g, (dy_ref, aux_ref) in enumerate(zip((dyl0_ref, dyl1_ref, dyl2_ref), (aux0_ref, aux1_ref, aux2_ref))):
            d = DILATIONS[g]
            _to_streams(dyb0s[g], scr, dy_ref, d, 0)
            _to_streams(lse_c, scr, aux_ref, d, 0)
            _to_streams(delta_c, scr, aux_ref, d, LANES)

        if nx:
            @pl.when(i == nt - 1)
            def _():
                for cp in _chip_exchange_copies(part_refs, recv_refs, *sems):
                    cp.wait()

    rev = lambda i: (nt - 1 - i, 0)
    pre = lambda i: (jnp.maximum((nt - 1 - i) * hb - 1, 0), 0)
    rows = lambda width: pl.BlockSpec((tm, width), rev)
    outs = [jax.ShapeDtypeStruct((s, D_MODEL), BF16), jax.ShapeDtypeStruct((s, D_MODEL), BF16),
            jax.ShapeDtypeStruct((s, 2 * D_MODEL), BF16), jax.ShapeDtypeStruct((s, 3 * CONV_WIDTH), BF16),
            jax.ShapeDtypeStruct((s, ATTN_WIDTH), BF16)]
    outs += [jax.ShapeDtypeStruct((d, s // d, GROUP_WIDTH), BF16) for d in DILATIONS]
    outs += [jax.ShapeDtypeStruct((d, s // d, 2 * LANES), F32) for d in DILATIONS]
    outs += [jax.ShapeDtypeStruct((SUBLANES, D_MODEL), F32), jax.ShapeDtypeStruct((SUBLANES, CONV_WIDTH), F32)]
    outs += [jax.ShapeDtypeStruct((3,) + a.shape[1:], a.dtype) for a in exchange]
    return pl.pallas_call(
        body, name="mix_bwd", grid=(nt,), out_shape=outs,
        in_specs=[rows(D_MODEL), rows(3 * CONV_WIDTH), pl.BlockSpec((2 * SUBLANES, 3 * CONV_WIDTH), pre),
                  rows(2 * D_MODEL), rows(D_MODEL), rows(D_MODEL), rows(ATTN_WIDTH), rows(GROUP_WIDTH),
                  _resident((3, CONV_WIDTH)), _resident((1, CONV_WIDTH)), _resident((2, D_MODEL)),
                  _resident((CONV_WIDTH, D_MODEL)), _resident((ATTN_WIDTH, D_MODEL)), _resident((D_MODEL, D_MODEL))]
        + [_ANY] * nx,
        out_specs=[rows(D_MODEL), rows(D_MODEL), rows(2 * D_MODEL), rows(3 * CONV_WIDTH), rows(ATTN_WIDTH)]
        + [_rev_stream_block(tm, d, GROUP_WIDTH, nt) for d in DILATIONS]
        + [_rev_stream_block(tm, d, 2 * LANES, nt) for d in DILATIONS]
        + [pl.BlockSpec((SUBLANES, D_MODEL), lambda i: (0, 0)), pl.BlockSpec((SUBLANES, CONV_WIDTH), lambda i: (0, 0))]
        + [_ANY] * nx,
        scratch_shapes=[pltpu.VMEM((SUBLANES, CONV_WIDTH), F32),
                        pltpu.VMEM((GROUP_WIDTH // LANES * tm, LANES), F32)] + (_dma_sems(3 * nx) if nx else []),
        compiler_params=_params(1))(dx1, abcv, abcv, gates, ya, yb, yb0, lsetot,
                                    conv_w, conv_b, b_gate, w_pa, w_pb, w_out, *exchange)


def _attn_bwd(qkv, dy, aux, gi, exchange=(), swap=()):
    d, length, _ = qkv.shape
    nb = length // ATTN_BLOCK
    q = ATTN_BLOCK
    gw = GROUP_WIDTH
    g = min(ATTN_BLOCKS_PER_STEP, nb)
    assert g >= 2 and nb % g == 0
    ns = nb // g
    lag = 1 if ns > 1 else 0
    tail = (g - 1) * q
    nx, nw = len(exchange), len(swap)

    def body(*refs):
        q_ref, kp_ref, kc_ref, vp_ref, vc_ref, dy_ref, aux_ref = refs[:7]
        part_refs = refs[7:7 + nx]
        slab_refs = refs[7 + nx:7 + nx + nw]
        pos = 7 + nx + nw
        out_ref = refs[pos]
        recv_refs = refs[pos + 1:pos + 1 + nx]
        swapped_refs = refs[pos + 1 + nx:pos + 1 + nx + nw]
        pos += 1 + nx + nw
        dq_ref, dkv_ref, bias_ref = refs[pos:pos + 3]
        sems = refs[pos + 3:]
        n = pl.program_id(1)

        def copies():
            cps = _chip_exchange_copies(part_refs, recv_refs, sems[0], sems[1]) if nx else []
            return cps + (_sibling_swap_copies(slab_refs, swapped_refs, sems[-2], sems[-1]) if nw else [])

        @pl.when((pl.program_id(0) == 0) & (n == 0))
        def _():
            _store_band_biases(bias_ref)
            for cp in copies():
                cp.start()

        if nx or nw:
            @pl.when((pl.program_id(0) == d - 1) & (n == ns - 1 + lag))
            def _():
                for cp in copies():
                    cp.wait()

        def emit(rows):
            out_ref[rows, gw:2 * gw] = dkv_ref[0, rows].astype(BF16)
            out_ref[rows, 2 * gw:3 * gw] = dkv_ref[1, rows].astype(BF16)

        if lag:
            @pl.when(n > 0)
            def _():
                out_ref[:, 0:gw] = dq_ref[...].astype(BF16)
                emit(slice(0, tail))

            @pl.when(n == ns)
            def _():
                emit(slice(tail, g * q))

        @pl.when(n < ns)
        def _():
            kfull = jnp.concatenate([kp_ref[...], kc_ref[...]], axis=0)
            vfull = jnp.concatenate([vp_ref[...], vc_ref[...]], axis=0)
            for j in range(g):
                rows = slice(j * q, (j + 1) * q)
                qs = _stack_heads(q_ref[rows, :])
                dys = _stack_heads(dy_ref[rows, :])
                k2 = kfull[j * q:(j + 2) * q]
                v2 = vfull[j * q:(j + 2) * q]
                lse = _compact_head_col(aux_ref[rows, 0:LANES])
                delta = _compact_head_col(aux_ref[rows, LANES:2 * LANES])
                bias = jnp.where(n == 0, bias_ref[1], bias_ref[0]) if j == 0 else bias_ref[0]
                p = jnp.exp(_dot_nt(qs, k2) + bias - lse)
                dp = _dot_nt(dys, v2)
                ds = (p * (dp - delta)).astype(BF16)
                dq_j = _unstack_heads(_dot(ds, k2)) * ATTN_SCALE
                dk2 = _dot_tn(ds, qs)
                dv2 = _dot_tn(p.astype(BF16), dys)
                if j == 0:
                    @pl.when(n > 0)
                    def _():
                        out_ref[tail:g * q, gw:2 * gw] = (dkv_ref[0, tail:g * q] + dk2[0:q]).astype(BF16)
                        out_ref[tail:g * q, 2 * gw:3 * gw] = (dkv_ref[1, tail:g * q] + dv2[0:q]).astype(BF16)
                else:
                    dkv_ref[0, (j - 1) * q:j * q] += dk2[0:q]
                    dkv_ref[1, (j - 1) * q:j * q] += dv2[0:q]
                dkv_ref[0, rows] = dk2[q:2 * q]
                dkv_ref[1, rows] = dv2[q:2 * q]
                dq_ref[rows, :] = dq_j
            if not lag:
                out_ref[:, 0:gw] = dq_ref[...].astype(BF16)
                emit(slice(0, g * q))

    cur, prev = _attn_block_specs(g, nb, clamp_last=True)
    return pl.pallas_call(
        body, name=f"attn_bwd_g{gi}", grid=(d, ns + lag),
        out_shape=[jax.ShapeDtypeStruct((d, length, 3 * gw), BF16)]
        + [jax.ShapeDtypeStruct((3,) + a.shape[1:], a.dtype) for a in exchange] + _swap_shapes(swap),
        in_specs=[cur(0), prev(1), cur(1), prev(2), cur(2), cur(0), cur(0, 2 * LANES)] + [_ANY] * (nx + nw),
        out_specs=[pl.BlockSpec((None, g * q, 3 * gw), lambda r, n: (r, jnp.maximum(n - lag, 0), 0))]
        + [_ANY] * (nx + nw),
        scratch_shapes=[pltpu.VMEM((g * q, gw), F32), pltpu.VMEM((2, g * q, gw), F32),
                        pltpu.VMEM((2, HEADS_PER_GROUP * q, 2 * q), F32)]
        + (_dma_sems(3 * nx) if nx else []) + (_dma_sems(nw) if nw else []),
        compiler_params=_params(2))(qkv, qkv, qkv, qkv, qkv, dy, aux, *exchange, *swap)


def _matmul_tn(name, a, b, col_tile=1024, row_tile=2048, slabs=0, swap=()):
    s, k = a.shape
    n = b.shape[1]
    tk = min(row_tile, s)
    tn = col_tile
    steps = s // tk
    nw = len(swap)

    def body(*refs):
        a_ref, b_ref = refs[:2]
        slab_refs = refs[2:2 + nw]
        o_ref = refs[2 + nw]
        swapped_refs = refs[3 + nw:3 + 2 * nw]
        acc_ref = refs[3 + 2 * nw]
        sems = refs[4 + 2 * nw:]
        t = pl.program_id(1)

        if nw:
            @pl.when((pl.program_id(0) == 0) & (t == 0))
            def _():
                for cp in _sibling_swap_copies(slab_refs, swapped_refs, *sems):
                    cp.start()

            @pl.when((pl.program_id(0) == n // tn - 1) & (t == steps - 1))
            def _():
                for cp in _sibling_swap_copies(slab_refs, swapped_refs, *sems):
                    cp.wait()

        @pl.when(t == 0)
        def _():
            acc_ref[...] = jnp.zeros_like(acc_ref)

        acc_ref[...] += _dot_tn(a_ref[...], b_ref[...])

        @pl.when(t == steps - 1)
        def _():
            if slabs:
                for q in range(per_tile):
                    o_ref[q] = acc_ref[:, q * width:(q + 1) * width].astype(BF16)
            else:
                o_ref[...] = acc_ref[...].astype(BF16)

    if slabs:
        width = n // slabs
        per_tile = tn // width
        out_shape = jax.ShapeDtypeStruct((slabs, k, width), BF16)
        out_spec = pl.BlockSpec((per_tile, k, width), lambda j, t: (j, 0, 0))
    else:
        out_shape = jax.ShapeDtypeStruct((k, n), BF16)
        out_spec = pl.BlockSpec((k, tn), lambda j, t: (0, j))
    res = pl.pallas_call(
        body, name=name, grid=(n // tn, steps), out_shape=[out_shape] + _swap_shapes(swap),
        in_specs=[pl.BlockSpec((tk, k), lambda j, t: (t, 0)), pl.BlockSpec((tk, tn), lambda j, t: (t, j))] + [_ANY] * nw,
        out_specs=[out_spec] + [_ANY] * nw,
        scratch_shapes=[pltpu.VMEM((k, tn), F32)] + (_dma_sems(nw) if nw else []),
        compiler_params=_params(2))(a, b, *swap)
    return res if nw else res[0]


def _sibling_swap_halves(name, slabs):
    na = len(slabs)

    def body(*refs):
        src_refs, out_refs = refs[:na], refs[na:2 * na]
        send_sems, recv_sems = refs[2 * na:]
        x, y, c, _ = _mesh_position()
        cps = []
        for a in range(na):
            theirs = _half_rows(1 - c, src_refs[a].shape[1] // 2)
            cps.append(pltpu.make_async_remote_copy(
                src_ref=src_refs[a].at[:, theirs, :], dst_ref=out_refs[a], send_sem=send_sems.at[a],
                recv_sem=recv_sems.at[a], device_id=(x, y, 1 - c), device_id_type=MESH_ID))
        for cp in cps:
            cp.start()
        for cp in cps:
            cp.wait()

    return pl.pallas_call(
        body, name=name,
        out_shape=[jax.ShapeDtypeStruct((a.shape[0], a.shape[1] // 2, a.shape[2]), a.dtype) for a in slabs],
        in_specs=[_ANY] * na, out_specs=[_ANY] * na,
        scratch_shapes=[pltpu.SemaphoreType.DMA((na,)), pltpu.SemaphoreType.DMA((na,))])(*slabs)


_HBM = pl.BlockSpec(memory_space=pltpu.HBM)
_SEM = pl.BlockSpec(memory_space=pltpu.SEMAPHORE)
_DATAFLOW = pltpu.SideEffectType.DATAFLOW_SIDE_EFFECTING


def _chip_exchange_start(partial):
    _, rows, cols = partial.shape
    landing = jax.ShapeDtypeStruct((3, rows, cols), partial.dtype)

    def body(src_ref, land_ref, send_sems, recv_sems, src_thru, land_thru, token):
        for cp in _chip_exchange_copies([src_ref], [land_ref], send_sems, recv_sems):
            cp.start()
        token[...] = jnp.zeros_like(token)

    return pl.pallas_call(
        body, name="grad_exchange_start",
        out_shape=(pltpu.SemaphoreType.DMA((3,)), pltpu.SemaphoreType.DMA((3,)),
                   pltpu.HBM(partial.shape, partial.dtype), pltpu.HBM(landing.shape, landing.dtype),
                   jax.ShapeDtypeStruct((SUBLANES, LANES), F32)),
        in_specs=(_HBM, _HBM), out_specs=(_SEM, _SEM, _HBM, _HBM, _VMEM), input_output_aliases={0: 2, 1: 3},
        compiler_params=pltpu.CompilerParams(has_side_effects=_DATAFLOW),
    )(pltpu.with_memory_space_constraint(partial, pltpu.HBM),
      pltpu.with_memory_space_constraint(lax.empty(landing.shape, landing.dtype), pltpu.HBM))


def _chip_exchange_wait(send_sems, recv_sems, src_thru, land_thru, after):
    def body(src_ref, land_ref, send_sems, recv_sems, after_ref, src_out, land_out):
        for cp in _chip_exchange_copies([src_ref], [land_ref], send_sems, recv_sems):
            cp.wait_send()
            cp.wait_recv()

    return pl.pallas_call(
        body, name="grad_exchange_wait",
        out_shape=(pltpu.HBM(src_thru.shape, src_thru.dtype), pltpu.HBM(land_thru.shape, land_thru.dtype)),
        in_specs=(_HBM, _HBM, _SEM, _SEM, _ANY), out_specs=(_HBM, _HBM), input_output_aliases={0: 0, 1: 1},
        compiler_params=pltpu.CompilerParams(has_side_effects=_DATAFLOW),
    )(src_thru, land_thru, send_sems, recv_sems, after)


def _sibling_share(halves):
    na = len(halves)

    def body(*refs):
        out_refs = refs[na:2 * na]
        send_sems, recv_sems = refs[2 * na:]
        x, y, c, _ = _mesh_position()
        cps = []
        for a in range(na):
            mine = out_refs[a].at[0, _half_rows(c, out_refs[a].shape[1] // 2)]
            cps.append(pltpu.make_async_remote_copy(src_ref=mine, dst_ref=mine, send_sem=send_sems.at[a],
                                                    recv_sem=recv_sems.at[a], device_id=(x, y, 1 - c),
                                                    device_id_type=MESH_ID))
        for cp in cps:
            cp.start()
        for a, cp in enumerate(cps):
            cp.wait_send()
            theirs = out_refs[a].at[0, _half_rows(1 - c, out_refs[a].shape[1] // 2)]
            pltpu.make_async_remote_copy(src_ref=theirs, dst_ref=theirs, send_sem=send_sems.at[a],
                                         recv_sem=recv_sems.at[a], device_id=(x, y, 1 - c),
                                         device_id_type=MESH_ID).wait_recv()

    return pl.pallas_call(
        body, name="grad_sibling_share", out_shape=[jax.ShapeDtypeStruct(a.shape, a.dtype) for a in halves],
        in_specs=[_ANY] * na, out_specs=[_ANY] * na, input_output_aliases={a: a for a in range(na)},
        scratch_shapes=[pltpu.SemaphoreType.DMA((na,)), pltpu.SemaphoreType.DMA((na,))])(*halves)


def _add_sibling(name, slabs, received, core):
    na = len(slabs)
    halves = [a.shape[1] // 2 for a in slabs]

    def body(core_ref, *refs):
        for a in range(na):
            refs[2 * na + a][...] = (refs[a][...].astype(F32) + refs[na + a][...].astype(F32)).astype(BF16)

    def block(a, mine):
        if mine:
            return pl.BlockSpec((None, halves[a], slabs[a].shape[2]), lambda s, core_ref: (s, core_ref[0], 0))
        return pl.BlockSpec((None, halves[a], slabs[a].shape[2]), lambda s, core_ref: (s, 0, 0))

    grid_spec = pltpu.PrefetchScalarGridSpec(
        num_scalar_prefetch=1, grid=(N_CHIPS,),
        in_specs=[block(a, True) for a in range(na)] + [block(a, False) for a in range(na)],
        out_specs=[block(a, False) for a in range(na)])
    return pl.pallas_call(body, name=name, grid_spec=grid_spec,
                          out_shape=[jax.ShapeDtypeStruct(r.shape, BF16) for r in received],
                          compiler_params=_params(1))(core, *slabs, *received)


def _sum_chips(name, partials, received, chip_core):
    na = len(partials)

    def body(cc_ref, *refs):
        for a in range(na):
            acc = refs[a][...].astype(F32)
            for k in range(3):
                acc = acc + refs[na + a][k].astype(F32)
            refs[2 * na + a][...] = acc

    def own(p):
        return pl.BlockSpec((None,) + p.shape[1:], lambda i, cc_ref: (cc_ref[0], 0, 0))

    def mine(p):
        return pl.BlockSpec((None,) + p.shape[1:], lambda i, cc_ref: (0, cc_ref[1], 0))

    grid_spec = pltpu.PrefetchScalarGridSpec(
        num_scalar_prefetch=1, grid=(1,),
        in_specs=[own(p) for p in partials] + [pl.BlockSpec(r.shape, lambda i, cc_ref: (0, 0, 0)) for r in received],
        out_specs=[mine(p) for p in partials])
    return pl.pallas_call(body, name=name, grid_spec=grid_spec,
                          out_shape=[jax.ShapeDtypeStruct((1, 2 * p.shape[1], p.shape[2]), F32) for p in partials],
                          compiler_params=_params(1))(chip_core, *partials, *received)


def _adam_math(w, g, m, v):
    nm = ADAM_B1 * m + (1.0 - ADAM_B1) * g
    nv = ADAM_B2 * v + (1.0 - ADAM_B2) * jnp.square(g)
    m_hat = nm / (1.0 - ADAM_B1 ** ADAM_STEP)
    v_hat = nv / (1.0 - ADAM_B2 ** ADAM_STEP)
    delta = -ADAM_LR * (m_hat / (jnp.sqrt(v_hat) + ADAM_EPS) + ADAM_WD * w)
    return delta, nm, nv


ADAMW_STEPS = 8


def _adamw(ws, gs, ms, vs):
    na = len(ws)

    def body(*refs):
        for a in range(na):
            w_ref, g_ref, m_ref, v_ref = (refs[k * na + a] for k in range(4))
            g_out_ref, d_ref, nm_ref, nv_ref = (refs[(4 + k) * na + a] for k in range(4))
            gv = g_ref[...]
            g_out_ref[...] = gv
            d_ref[...], nm_ref[...], nv_ref[...] = _adam_math(w_ref[...], gv, m_ref[...], v_ref[...])

    specs = [pl.BlockSpec((None, w.shape[1] // ADAMW_STEPS, w.shape[2]), lambda i: (0, i, 0)) for w in ws]
    outs = pl.pallas_call(
        body, name="adamw", grid=(ADAMW_STEPS,), out_shape=[jax.ShapeDtypeStruct(w.shape, F32) for w in ws] * 4,
        in_specs=specs * 4, out_specs=specs * 4, compiler_params=_params(1))(*ws, *gs, *ms, *vs)
    return [[outs[k * na + a] for k in range(4)] for a in range(na)]


SMALL_PARAMS = ("norm_mix_g", "b_gate", "conv_a_w", "conv_a_b", "norm_ffn_g", "ffn_conv_w", "ffn_conv_b", "final_norm_g")


def _small_update(partials, params, moments_m, moments_v):
    na = len(partials)
    npar = len(SMALL_PARAMS)

    def body(*refs):
        in_refs = refs[:na]
        w_refs = refs[na:na + npar]
        m_refs = refs[na + npar:na + 2 * npar]
        v_refs = refs[na + 2 * npar:na + 3 * npar]
        pos = na + 3 * npar
        loss_ref = refs[pos]
        out_refs = refs[pos + 1:pos + 1 + 4 * npar]
        pos += 1 + 4 * npar
        acc_refs = refs[pos:pos + na]
        recv_refs = refs[pos + na:pos + 4 * na]
        send_sems, recv_sems = refs[pos + 4 * na:]
        x, y, c, _ = _mesh_position()
        chip = 2 * x + y
        for a in range(na):
            acc_refs[a][...] = in_refs[a][...]
        for stage, peer in enumerate(((x, y, 1 - c), (x, 1 - y, c), (1 - x, y, c))):
            cps = []
            for a in range(na):
                k = stage * na + a
                cps.append(pltpu.make_async_remote_copy(src_ref=acc_refs[a], dst_ref=recv_refs[k], send_sem=send_sems.at[k],
                                                        recv_sem=recv_sems.at[k], device_id=peer, device_id_type=MESH_ID))
            for cp in cps:
                cp.start()
            for cp in cps:
                cp.wait()
            for a in range(na):
                acc_refs[a][...] = acc_refs[a][...] + recv_refs[stage * na + a][...]

        mix, ffn, fin, gate, conv, ffnc, loss = acc_refs
        loss_ref[...] = loss[...]

        def cols(width):
            return pl.ds(pl.multiple_of(chip * width, LANES), width)

        grads = {
            "norm_mix_g": mix[...], "norm_ffn_g": ffn[...], "final_norm_g": fin[...],
            "b_gate": gate[0:2, cols(D_MODEL // N_CHIPS)],
            "conv_a_w": conv[0:3, cols(CONV_WIDTH // N_CHIPS)], "conv_a_b": conv[3:4, :],
            "ffn_conv_w": ffnc[0:3, cols(2 * D_FF // N_CHIPS)], "ffn_conv_b": ffnc[3:4, :]}
        for i, name in enumerate(SMALL_PARAMS):
            g = grads[name]
            if len(w_refs[i].shape) == 3:
                results = (g,) + _adam_math(w_refs[i][0], g, m_refs[i][0], v_refs[i][0])
                for o_ref, val in zip(out_refs[4 * i:4 * i + 4], results):
                    o_ref[0] = val
            else:
                results = (g,) + _adam_math(w_refs[i][...], g, m_refs[i][...], v_refs[i][...])
                for o_ref, val in zip(out_refs[4 * i:4 * i + 4], results):
                    o_ref[...] = val

    outs = [jax.ShapeDtypeStruct(partials[-1].shape, F32)]
    for w in params:
        outs += [jax.ShapeDtypeStruct(w.shape, F32)] * 4
    scratch = [pltpu.VMEM(p.shape, F32) for p in partials]
    scratch += [pltpu.VMEM(p.shape, F32) for _ in range(3) for p in partials]
    scratch += [pltpu.SemaphoreType.DMA((3 * na,)), pltpu.SemaphoreType.DMA((3 * na,))]
    n_in = na + 3 * npar
    return pl.pallas_call(
        body, name="small_update", out_shape=outs, in_specs=[_VMEM] * n_in, out_specs=[_VMEM] * len(outs),
        scratch_shapes=scratch)(*partials, *params, *moments_m, *moments_v)


def _gathered_columns(g):
    return jnp.transpose(g, (1, 0, 2)).reshape(g.shape[1], N_CHIPS * g.shape[2])


def kernel(x, norm_mix_g, w_in, b_gate, conv_a_w, conv_a_b, w_proj_a, w_proj_b, w_out, norm_ffn_g, w_up, ffn_conv_w, ffn_conv_b, w_down, final_norm_g, loss_target, m_norm_mix_g, m_w_in, m_b_gate, m_conv_a_w, m_conv_a_b, m_w_proj_a, m_w_proj_b, m_w_out, m_norm_ffn_g, m_w_up, m_ffn_conv_w, m_ffn_conv_b, m_w_down, m_final_norm_g, v_norm_mix_g, v_w_in, v_b_gate, v_conv_a_w, v_conv_a_b, v_w_proj_a, v_w_proj_b, v_w_out, v_norm_ffn_g, v_w_up, v_ffn_conv_w, v_ffn_conv_b, v_w_down, v_final_norm_g):
    chip = (2 * lax.axis_index("x") + lax.axis_index("y")).astype(jnp.int32)
    core = lax.axis_index("c").astype(jnp.int32)
    core_arr = core.reshape(1)
    chip_core = jnp.stack([chip, core])
    xs, target = x[0], loss_target[0]
    g_final = final_norm_g.reshape(1, D_MODEL)

    def own_slot(gathered, own):
        return lax.dynamic_update_slice(gathered, own, (chip, 0, 0))

    w_in_t, m_w_in_t, v_w_in_t = (jnp.swapaxes(a, 1, 2) for a in (w_in, m_w_in, v_w_in))
    w_in_tb = w_in_t.astype(BF16)
    h1, h1_streams4, h1_streams16, g_in = _norm_fwd(xs, norm_mix_g, w_in_tb)
    w_in_full_t = own_slot(g_in, w_in_tb).reshape(D_IN, D_MODEL)
    later_w = [w_proj_a, w_proj_b, w_out, w_up, w_down]
    later_b = [w.astype(BF16) for w in later_w]
    small_sharded = [b_gate, conv_a_w, ffn_conv_w]
    fwd = _inproj_fwd(h1, w_in_full_t, later_b, small_sharded)
    abcv, gates, qkv0, qkv1, qkv2 = fwd[:5]
    gathered_big, gathered_small = fwd[5:10], fwd[10:13]
    attn0 = _attn_fwd(qkv0, 0, forward=gathered_big)
    attn = [attn0[:2], _attn_fwd(qkv1, 1), _attn_fwd(qkv2, 2)]
    g_pa, g_pb, g_out, g_up, g_down = [own_slot(g, own) for g, own in zip(attn0[2:], later_b)]
    g_bgate, g_convw, g_ffnw = [own_slot(g, own) for g, own in zip(gathered_small, small_sharded)]
    w_pa_full, w_pb_full, w_up_full = _gathered_columns(g_pa), _gathered_columns(g_pb), _gathered_columns(g_up)
    w_out_full, w_down_full = g_out.reshape(D_MODEL, D_MODEL), g_down.reshape(D_FF, D_MODEL)
    b_gate_full, conv_w_full, ffn_w_full = (_gathered_columns(g) for g in (g_bgate, g_convw, g_ffnw))

    x1, ya0, yb0, mrg, ya, yb, lsetot = _mix_fwd(
        xs, abcv, gates, [a[0] for a in attn], [a[1] for a in attn], conv_w_full, conv_a_b, b_gate_full,
        w_pa_full, w_pb_full, w_out_full)
    h2, up0, up = _ffn_up_fwd(x1, norm_ffn_g, w_up_full, ffn_w_full, ffn_conv_b)
    act, dx2, dx2b, d_g_final, loss = _ffn_act_fwd(x1, up, target, w_down_full, g_final)

    d_up = _ffn_act_bwd(dx2b, up, w_down_full)
    slab_down = _matmul_tn("dw_down", act, dx2b, col_tile=512).reshape(N_CHIPS, D_FF // N_CHIPS, D_MODEL)
    dx1, dx1b, d_g_ffn, d_up0, ffn_small, swapped_down = _ffn_up_bwd(
        d_up, up0, w_up_full, ffn_w_full, x1, norm_ffn_g, dx2, swap=[slab_down])
    (partial_down,) = _add_sibling("grad_add_w_down", [slab_down], [swapped_down], core_arr)
    slab_up = _matmul_tn("dw_up", h2, d_up0, col_tile=2 * D_FF // N_CHIPS, slabs=N_CHIPS)
    d_w_out, swapped_up = _matmul_tn("dw_out", mrg, dx1b, swap=[slab_up])
    (partial_up,) = _add_sibling("grad_add_w_up", [slab_up], [swapped_up], core_arr)

    mix_res = _mix_bwd(dx1, abcv, gates, ya, yb, yb0, lsetot, conv_w_full, conv_a_b, b_gate_full,
                       w_pa_full, w_pb_full, w_out_full, exchange=[partial_up, partial_down])
    (d_ya, d_yb, d_gates, d_abcv, d_yb0, dyl0, dyl1, dyl2, aux0, aux1, aux2, gate_small, conv_small) = mix_res[:13]
    halves_ffn = _sum_chips("grad_sum_ffn", [partial_up, partial_down], mix_res[13:], chip_core)

    slabs_mix = [_matmul_tn("dw_proj_a", ya0, d_ya, slabs=N_CHIPS), _matmul_tn("dw_proj_b", yb0, d_yb, slabs=N_CHIPS),
                 d_w_out.reshape(N_CHIPS, D_MODEL // N_CHIPS, D_MODEL)]
    res0 = _attn_bwd(qkv0, dyl0, aux0, 0, swap=slabs_mix)
    d_qkv0, partials_mix = res0[0], _add_sibling("grad_add_mix", slabs_mix, res0[1:], core_arr)
    res1 = _attn_bwd(qkv1, dyl1, aux1, 1, exchange=partials_mix)
    d_qkv1, halves_mix = res1[0], _sum_chips("grad_sum_mix", partials_mix, res1[1:], chip_core)
    (d_qkv2,) = _attn_bwd(qkv2, dyl2, aux2, 2)

    dq = [d_qkv0, d_qkv1, d_qkv2]
    seq = xs.shape[0]
    d_w_abcv = _matmul_tn("dw_in_abcv", d_abcv, h1)
    d_w_gates = _matmul_tn("dw_in_gates", d_gates, h1)
    d_w_groups = [_matmul_tn(f"dw_in_qkv{g}", t.reshape(seq, 3 * GROUP_WIDTH), h.reshape(seq, D_MODEL))
                  for g, (t, h) in enumerate(zip(dq, (h1, h1_streams4, h1_streams16)))]
    gw = GROUP_WIDTH
    d_w_in_t = jnp.concatenate(
        [d_w_abcv] + [d_w_groups[g][j * gw:(j + 1) * gw] for j in range(3) for g in range(3)] + [d_w_gates], axis=0)

    slab_in = d_w_in_t.reshape(N_CHIPS, D_IN // N_CHIPS, D_MODEL)
    (from_sibling_in,) = _sibling_swap_halves("grad_swap_w_in", [slab_in])
    (partial_in,) = _add_sibling("grad_add_w_in", [slab_in], [from_sibling_in], core_arr)
    send_sems, recv_sems, partial_thru, landing_thru, token = _chip_exchange_start(partial_in)
    g_mix_after_start = norm_mix_g + token[0:1, 0:1]
    grad_x, d_g_mix = _inproj_bwd(d_abcv, d_gates, dq, w_in_full_t, xs, g_mix_after_start, dx1)
    partial_in, received_in = _chip_exchange_wait(send_sems, recv_sems, partial_thru, landing_thru, d_g_mix)
    halves_in = _sum_chips("grad_sum_w_in", [partial_in], [received_in], chip_core)

    big_names = ("w_in", "w_proj_a", "w_proj_b", "w_out", "w_up", "w_down")
    big_grads = _sibling_share(halves_in + halves_mix + halves_ffn)
    big_w = dict(w_in=w_in_t, w_proj_a=w_proj_a, w_proj_b=w_proj_b, w_out=w_out, w_up=w_up, w_down=w_down)
    big_m = dict(w_in=m_w_in_t, w_proj_a=m_w_proj_a, w_proj_b=m_w_proj_b, w_out=m_w_out, w_up=m_w_up, w_down=m_w_down)
    big_v = dict(w_in=v_w_in_t, w_proj_a=v_w_proj_a, w_proj_b=v_w_proj_b, w_out=v_w_out, w_up=v_w_up, w_down=v_w_down)

    fin_w, fin_m, fin_v = (a.reshape(1, D_MODEL) for a in (final_norm_g, m_final_norm_g, v_final_norm_g))
    small_w = [norm_mix_g, b_gate, conv_a_w, conv_a_b, norm_ffn_g, ffn_conv_w, ffn_conv_b, fin_w]
    small_m = [m_norm_mix_g, m_b_gate, m_conv_a_w, m_conv_a_b, m_norm_ffn_g, m_ffn_conv_w, m_ffn_conv_b, fin_m]
    small_v = [v_norm_mix_g, v_b_gate, v_conv_a_w, v_conv_a_b, v_norm_ffn_g, v_ffn_conv_w, v_ffn_conv_b, fin_v]
    small_out = _small_update([d_g_mix, d_g_ffn, d_g_final, gate_small, conv_small, ffn_small, loss],
                              small_w, small_m, small_v)
    total_loss = small_out[0][0, 0]

    grads, delta, new_m, new_v = {}, {}, {}, {}
    for i, n in enumerate(SMALL_PARAMS):
        vals = small_out[1 + 4 * i:5 + 4 * i]
        if n == "final_norm_g":
            vals = [a.reshape(D_MODEL) for a in vals]
        grads[n], delta[n], new_m[n], new_v[n] = vals
    updates = _adamw([big_w[n] for n in big_names], big_grads, [big_m[n] for n in big_names],
                     [big_v[n] for n in big_names])
    for n, vals in zip(big_names, updates):
        if n == "w_in":
            vals = [jnp.swapaxes(a, 1, 2) for a in vals]
        grads[n], delta[n], new_m[n], new_v[n] = vals

    names = ["norm_mix_g", "w_in", "b_gate", "conv_a_w", "conv_a_b", "w_proj_a", "w_proj_b", "w_out", "norm_ffn_g", "w_up",
             "ffn_conv_w", "ffn_conv_b", "w_down", "final_norm_g"]
    out = [total_loss, grad_x[None]]
    for group in (grads, delta, new_m, new_v):
        out += [group[n] for n in names]
    return tuple(out)
```

```python
import jax
import jax.numpy as jnp
from jax import lax
from jax.experimental import pallas as pl
from jax.experimental.pallas import tpu as pltpu

F32 = jnp.float32
BF16 = jnp.bfloat16

D_MODEL = 1024
CONV_WIDTH = 512
ATTN_WIDTH = 768
GROUP_WIDTH = 256
HEAD_DIM = 64
HEADS_PER_GROUP = 4
DILATIONS = (1, 4, 16)
ATTN_BLOCK = 128
D_FF = 2816
D_IN = 5888
EPS = 1e-6
NEG_INF = -1e30
ATTN_SCALE = HEAD_DIM ** -0.5

COL_ABCV = 0
COL_Q = 1536
COL_K = 2304
COL_V = 3072
COL_GATES = 3840

ADAM_LR = 0.001
ADAM_B1 = 0.9
ADAM_B2 = 0.999
ADAM_EPS = 1e-08
ADAM_WD = 0.01
ADAM_STEP = 10

LANES = 128
SUBLANES = 8
BF16_ROWS = 16
ROW_TILE = 512
VMEM_LIMIT = 56 * 1024 * 1024

_NT = (((1,), (1,)), ((), ()))
_TN = (((0,), (0,)), ((), ()))


def _params(n_axes, vmem=VMEM_LIMIT):
    return pltpu.CompilerParams(dimension_semantics=("arbitrary",) * n_axes, vmem_limit_bytes=vmem)


def _resident(shape):
    nd = len(shape)
    return pl.BlockSpec(shape, lambda *_: (0,) * nd, pipeline_mode=pl.Buffered(1))


def _rows(tm, width, col_block=0):
    return pl.BlockSpec((tm, width), lambda i: (i, col_block))


def _col_chunks(n, cmax):
    out, lo = [], 0
    while lo < n:
        size = min(cmax, n - lo)
        out.append((lo, size))
        lo += size
    return out


def _dot(a, b):
    return jnp.dot(a, b, preferred_element_type=F32)


def _dot_nt(a, b):
    return lax.dot_general(a, b, _NT, preferred_element_type=F32)


def _dot_tn(a, b):
    return lax.dot_general(a, b, _TN, preferred_element_type=F32)


def _sigmoid(x):
    return 0.5 * jnp.tanh(0.5 * x) + 0.5


def _silu(x):
    hx = 0.5 * x
    return hx + hx * jnp.tanh(hx)


def _shift_down(v, k, halo8):
    tm = v.shape[0]
    rolled = pltpu.roll(v, k, 0)
    fix = jnp.tile(pltpu.roll(halo8, k, 0), (tm // SUBLANES, 1))
    row = lax.broadcasted_iota(jnp.int32, v.shape, 0)
    return jnp.where(row < k, fix, rolled)


def _shift_up(v, k, halo8):
    tm = v.shape[0]
    rolled = pltpu.roll(v, tm - k, 0)
    fix = jnp.tile(pltpu.roll(halo8, SUBLANES - k, 0), (tm // SUBLANES, 1))
    row = lax.broadcasted_iota(jnp.int32, v.shape, 0)
    return jnp.where(row >= tm - k, fix, rolled)


def _colsum(v):
    return jnp.sum(v, axis=0, keepdims=True)


def _to_streams(val, scr, out_ref, d, col0):
    tm = val.shape[0]
    panels = val.shape[1] // LANES
    if d == 1:
        out_ref[0, :, col0:col0 + val.shape[1]] = val.astype(out_ref.dtype)
        return
    for p in range(panels):
        scr[pl.ds(p * tm, tm), :] = val[:, p * LANES:(p + 1) * LANES]
    for r in range(d):
        for p in range(panels):
            piece = scr[pl.ds(p * tm + r, tm // d, stride=d), :]
            out_ref[r, :, col0 + p * LANES: col0 + (p + 1) * LANES] = piece.astype(out_ref.dtype)


def _from_streams(in_ref, scr, d, col0, width):
    panels = width // LANES
    rows = in_ref.shape[1]
    tm = rows * d
    if d == 1:
        return in_ref[0, :, col0:col0 + width].astype(F32)
    for r in range(d):
        for p in range(panels):
            scr[pl.ds(p * tm + r, rows, stride=d), :] = in_ref[r, :, col0 + p * LANES: col0 + (p + 1) * LANES].astype(F32)
    return jnp.concatenate([scr[pl.ds(p * tm, tm), :] for p in range(panels)], axis=1)


def _stream_block(tm, d, width):
    return pl.BlockSpec((d, tm // d, width), lambda i: (0, i, 0))


def _rev_stream_block(tm, d, width, nt):
    return pl.BlockSpec((d, tm // d, width), lambda i: (0, nt - 1 - i, 0))


N_CHIPS = 4
MESH_ID = pl.DeviceIdType.MESH
_ANY = pl.BlockSpec(memory_space=pl.ANY)
_VMEM = pl.BlockSpec(memory_space=pltpu.VMEM)


def _mesh_position():
    x, y, c = lax.axis_index("x"), lax.axis_index("y"), lax.axis_index("c")
    other_chips = [(1 - x, y), (x, 1 - y), (1 - x, 1 - y)]
    return x, y, c, other_chips


def _half_rows(c, half):
    return pl.ds(pl.multiple_of(c * half, BF16_ROWS), half)


def _remote_copy(k, src, dst, to, send_sems, recv_sems):
    return pltpu.make_async_remote_copy(src_ref=src, dst_ref=dst, send_sem=send_sems.at[k], recv_sem=recv_sems.at[k],
                                        device_id=to, device_id_type=MESH_ID)


def _gather_first_copies(big_refs, small_refs, big_outs, small_outs, send_sems, recv_sems):
    x, y, c, chips = _mesh_position()
    me = 2 * x + y
    nb = len(big_refs)
    cps = []
    for j, (px, py) in enumerate(chips):
        for b in range(nb):
            mine = _half_rows(c, big_refs[b].shape[1] // 2)
            cps.append(_remote_copy(3 * b + j, big_refs[b].at[0, mine], big_outs[b].at[me, mine], (px, py, c),
                                    send_sems, recv_sems))
        for s in range(len(small_refs)):
            cps.append(_remote_copy(3 * (nb + s) + j, small_refs[s].at[0], small_outs[s].at[me], (px, py, c),
                                    send_sems, recv_sems))
    return cps


def _gather_forward_copies(bufs, send_sems, recv_sems):
    x, y, c, chips = _mesh_position()
    cps = []
    for j, (px, py) in enumerate(chips):
        for b in range(len(bufs)):
            landed = bufs[b].at[2 * px + py, _half_rows(c, bufs[b].shape[1] // 2)]
            cps.append(_remote_copy(3 * b + j, landed, landed, (x, y, 1 - c), send_sems, recv_sems))
    return cps


def _chip_exchange_copies(src_refs, out_refs, send_sems, recv_sems):
    x, y, c, chips = _mesh_position()
    cps = []
    for j, (px, py) in enumerate(chips):
        for a in range(len(src_refs)):
            cps.append(_remote_copy(3 * a + j, src_refs[a].at[2 * px + py], out_refs[a].at[j], (px, py, c),
                                    send_sems, recv_sems))
    return cps


def _sibling_swap_copies(src_refs, out_refs, send_sems, recv_sems):
    x, y, c, _ = _mesh_position()
    cps = []
    for a in range(len(src_refs)):
        theirs = _half_rows(1 - c, src_refs[a].shape[1] // 2)
        cps.append(_remote_copy(a, src_refs[a].at[:, theirs, :], out_refs[a], (x, y, 1 - c), send_sems, recv_sems))
    return cps


def _swap_shapes(slabs):
    return [jax.ShapeDtypeStruct((a.shape[0], a.shape[1] // 2, a.shape[2]), a.dtype) for a in slabs]


def _dma_sems(n):
    return [pltpu.SemaphoreType.DMA((n,)), pltpu.SemaphoreType.DMA((n,))]


def _norm_fwd(x, g, shard):
    s = x.shape[0]
    tm = ROW_TILE
    nt = s // tm

    def body(x_ref, g_ref, shard_ref, h_ref, hs1_ref, hs2_ref, buf_ref, scr, send1, recv1, send2, recv2):
        i = pl.program_id(0)

        def level_one():
            return _gather_first_copies([shard_ref], [], [buf_ref], [], send1, recv1)

        @pl.when(i == 0)
        def _():
            for cp in level_one():
                cp.start()

        xv = x_ref[...]
        r = lax.rsqrt(jnp.mean(xv * xv, axis=-1, keepdims=True) + EPS)
        hf = xv * r * g_ref[...]
        h_ref[...] = hf.astype(BF16)
        for d, hs_ref in zip(DILATIONS[1:], (hs1_ref, hs2_ref)):
            for lo, size in _col_chunks(D_MODEL, GROUP_WIDTH):
                _to_streams(hf[:, lo:lo + size], scr, hs_ref, d, lo)

        @pl.when(i == nt - 1)
        def _():
            forwards = _gather_forward_copies([buf_ref], send2, recv2)
            for cp, fwd in zip(level_one(), forwards):
                cp.wait()
                fwd.start()
            for fwd in forwards:
                fwd.wait()

    return pl.pallas_call(
        body, name="norm_fwd", grid=(nt,),
        out_shape=[jax.ShapeDtypeStruct((s, D_MODEL), BF16)]
        + [jax.ShapeDtypeStruct((d, s // d, D_MODEL), BF16) for d in DILATIONS[1:]]
        + [jax.ShapeDtypeStruct((N_CHIPS,) + shard.shape[1:], shard.dtype)],
        in_specs=[_rows(tm, D_MODEL), _resident((1, D_MODEL)), _ANY],
        out_specs=[_rows(tm, D_MODEL)] + [_stream_block(tm, d, D_MODEL) for d in DILATIONS[1:]] + [_ANY],
        scratch_shapes=[pltpu.VMEM((GROUP_WIDTH // LANES * tm, LANES), F32)] + _dma_sems(3) + _dma_sems(3),
        compiler_params=_params(1))(x, g, shard)


def _inproj_fwd(h1, w_in_t, big_shards, small_shards):
    s = h1.shape[0]
    tm = ROW_TILE
    nt = s // tm
    nb, ns = len(big_shards), len(small_shards)
    n_fixed_in, n_fixed_out = 2, 5

    def body(*refs):
        h_ref, w_ref = refs[:n_fixed_in]
        shard_refs = refs[n_fixed_in:n_fixed_in + nb + ns]
        pos = n_fixed_in + nb + ns
        abcv_ref, gates_ref, qkv0_ref, qkv1_ref, qkv2_ref = refs[pos:pos + n_fixed_out]
        gathered_refs = refs[pos + n_fixed_out:pos + n_fixed_out + nb + ns]
        scr, send_sems, recv_sems = refs[pos + n_fixed_out + nb + ns:]
        i = pl.program_id(0)

        def gather_copies():
            return _gather_first_copies(shard_refs[:nb], shard_refs[nb:], gathered_refs[:nb], gathered_refs[nb:],
                                        send_sems, recv_sems)

        @pl.when(i == 0)
        def _():
            for cp in gather_copies():
                cp.start()

        h = h_ref[...]
        for lo, size in _col_chunks(3 * CONV_WIDTH, 512):
            abcv_ref[:, lo:lo + size] = _dot_nt(h, w_ref[COL_ABCV + lo: COL_ABCV + lo + size, :]).astype(BF16)
        for lo, size in _col_chunks(2 * D_MODEL, 512):
            gates_ref[:, lo:lo + size] = _dot_nt(h, w_ref[COL_GATES + lo: COL_GATES + lo + size, :]).astype(BF16)
        for gi, (d, out_ref) in enumerate(zip(DILATIONS, (qkv0_ref, qkv1_ref, qkv2_ref))):
            for j, base in enumerate((COL_Q, COL_K, COL_V)):
                lo = base + gi * GROUP_WIDTH
                y = _dot_nt(h, w_ref[lo:lo + GROUP_WIDTH, :])
                if j == 0:
                    y = y * ATTN_SCALE
                _to_streams(y, scr, out_ref, d, j * GROUP_WIDTH)

        @pl.when(i == nt - 1)
        def _():
            for cp in gather_copies():
                cp.wait()

    outs = [jax.ShapeDtypeStruct((s, 3 * CONV_WIDTH), BF16), jax.ShapeDtypeStruct((s, 2 * D_MODEL), BF16)]
    outs += [jax.ShapeDtypeStruct((d, s // d, 3 * GROUP_WIDTH), BF16) for d in DILATIONS]
    outs += [jax.ShapeDtypeStruct((N_CHIPS,) + a.shape[1:], a.dtype) for a in list(big_shards) + list(small_shards)]
    return pl.pallas_call(
        body, name="inproj_fwd", grid=(nt,), out_shape=outs,
        in_specs=[_rows(tm, D_MODEL), _resident((D_IN, D_MODEL))] + [_ANY] * (nb + ns),
        out_specs=[_rows(tm, 3 * CONV_WIDTH), _rows(tm, 2 * D_MODEL)]
        + [_stream_block(tm, d, 3 * GROUP_WIDTH) for d in DILATIONS] + [_ANY] * (nb + ns),
        scratch_shapes=[pltpu.VMEM((GROUP_WIDTH // LANES * tm, LANES), F32)] + _dma_sems(3 * (nb + ns)),
        compiler_params=_params(1))(h1, w_in_t, *big_shards, *small_shards)


def _head_of_lane(shape):
    return lax.broadcasted_iota(jnp.int32, shape, 1) // HEAD_DIM


def _stack_heads(v):
    head = _head_of_lane(v.shape)
    return jnp.concatenate([jnp.where(head == h, v, jnp.zeros_like(v)) for h in range(HEADS_PER_GROUP)], axis=0)


def _unstack_heads(v):
    q = ATTN_BLOCK
    head = _head_of_lane((q, v.shape[1]))
    out = jnp.zeros((q, v.shape[1]), v.dtype)
    for h in range(HEADS_PER_GROUP):
        out = jnp.where(head == h, v[h * q:(h + 1) * q], out)
    return out


def _per_head_rows(col):
    q = ATTN_BLOCK
    head = _head_of_lane((q, GROUP_WIDTH))
    out = jnp.zeros((q, GROUP_WIDTH), col.dtype)
    for h in range(HEADS_PER_GROUP):
        out = jnp.where(head == h, col[h * q:(h + 1) * q], out)
    return out


def _compact_heads(v):
    lane = lax.broadcasted_iota(jnp.int32, (v.shape[0], LANES), 1)
    return jnp.where((lane & 32) == 0, v[:, 0:LANES], v[:, LANES:2 * LANES])


def _compact_head_col(v):
    lane = lax.broadcasted_iota(jnp.int32, v.shape, 1)
    head = ((lane >> 6) & 1) + 2 * ((lane >> 5) & 1)
    cols = [jnp.max(jnp.where(head == h, v, -jnp.inf), axis=1, keepdims=True) for h in range(HEADS_PER_GROUP)]
    return jnp.concatenate(cols, axis=0)


ATTN_BLOCKS_PER_STEP = 4


def _band_bias(first_block):
    rows = HEADS_PER_GROUP * ATTN_BLOCK
    qi = lax.broadcasted_iota(jnp.int32, (rows, 2 * ATTN_BLOCK), 0) % ATTN_BLOCK
    kj = lax.broadcasted_iota(jnp.int32, (rows, 2 * ATTN_BLOCK), 1)
    dist = qi + ATTN_BLOCK - kj
    valid = (dist >= 0) & (dist <= ATTN_BLOCK)
    if first_block:
        valid = valid & (kj >= ATTN_BLOCK)
    return jnp.where(valid, 0.0, NEG_INF).astype(F32)


def _store_band_biases(bias_ref):
    bias_ref[0] = _band_bias(False)
    bias_ref[1] = _band_bias(True)


def _attn_block_specs(g, nb, clamp_last=False):
    q = ATTN_BLOCK
    last = nb // g - 1

    def cur(col, width=GROUP_WIDTH):
        if clamp_last:
            return pl.BlockSpec((None, g * q, width), lambda r, n: (r, jnp.minimum(n, last), col))
        return pl.BlockSpec((None, g * q, width), lambda r, n: (r, n, col))

    def prev(col):
        if clamp_last:
            return pl.BlockSpec((None, q, GROUP_WIDTH), lambda r, n: (r, jnp.clip(n * g - 1, 0, nb - 1), col))
        return pl.BlockSpec((None, q, GROUP_WIDTH), lambda r, n: (r, jnp.maximum(n * g - 1, 0), col))

    return cur, prev


def _attn_fwd(qkv, gi, forward=()):
    d, length, _ = qkv.shape
    nb = length // ATTN_BLOCK
    q = ATTN_BLOCK
    g = min(ATTN_BLOCKS_PER_STEP, nb)
    ns = nb // g
    nf = len(forward)

    def body(*refs):
        q_ref, kp_ref, kc_ref, vp_ref, vc_ref = refs[:5]
        o_ref, lse_ref = refs[5 + nf:7 + nf]
        buf_refs = refs[7 + nf:7 + 2 * nf]
        bias_ref = refs[7 + 2 * nf]
        sems = refs[8 + 2 * nf:]
        n = pl.program_id(1)
        first_step = (pl.program_id(0) == 0) & (n == 0)
        last_step = (pl.program_id(0) == d - 1) & (n == ns - 1)

        @pl.when(first_step)
        def _():
            _store_band_biases(bias_ref)
            for cp in _gather_forward_copies(buf_refs, *sems) if nf else ():
                cp.start()

        kfull = jnp.concatenate([kp_ref[...], kc_ref[...]], axis=0)
        vfull = jnp.concatenate([vp_ref[...], vc_ref[...]], axis=0)
        for j in range(g):
            qs = _stack_heads(q_ref[j * q:(j + 1) * q, :])
            k2 = kfull[j * q:(j + 2) * q]
            v2 = vfull[j * q:(j + 2) * q]
            bias = jnp.where(n == 0, bias_ref[1], bias_ref[0]) if j == 0 else bias_ref[0]
            sc = _dot_nt(qs, k2) + bias
            m = jnp.max(sc, axis=1, keepdims=True)
            p = jnp.exp(sc - m)
            l = jnp.sum(p, axis=1, keepdims=True)
            of = _dot(p.astype(BF16), v2) / l
            o_ref[j * q:(j + 1) * q, :] = _unstack_heads(of).astype(BF16)
            lse_ref[j * q:(j + 1) * q, :] = _per_head_rows(m + jnp.log(l))

        if nf:
            @pl.when(last_step)
            def _():
                for cp in _gather_forward_copies(buf_refs, *sems):
                    cp.wait()

    cur, prev = _attn_block_specs(g, nb)
    return pl.pallas_call(
        body, name=f"attn_fwd_g{gi}", grid=(d, ns),
        out_shape=[jax.ShapeDtypeStruct((d, length, GROUP_WIDTH), BF16),
                   jax.ShapeDtypeStruct((d, length, GROUP_WIDTH), F32)]
        + [jax.ShapeDtypeStruct(a.shape, a.dtype) for a in forward],
        in_specs=[cur(0), prev(1), cur(1), prev(2), cur(2)] + [_ANY] * nf,
        out_specs=[cur(0), cur(0)] + [_ANY] * nf,
        input_output_aliases={5 + a: 2 + a for a in range(nf)},
        scratch_shapes=[pltpu.VMEM((2, HEADS_PER_GROUP * q, 2 * q), F32)] + (_dma_sems(3 * nf) if nf else []),
        compiler_params=_params(2))(qkv, qkv, qkv, qkv, qkv, *forward)


def _conv_branch(ab, ac, av, halo_u, w, b):
    u = ac * av
    sh1 = _shift_down(u, 1, halo_u)
    sh2 = _shift_down(u, 2, halo_u)
    cv = w[0:1] * sh2 + w[1:2] * sh1 + w[2:3] * u + b
    return ab * cv, cv, u, sh1, sh2


def _mix_fwd(x, abcv, gates, o_list, lse_list, conv_w, conv_b, b_gate, w_pa, w_pb, w_out):
    s = x.shape[0]
    tm = ROW_TILE

    def body(x_ref, abcv_ref, gates_ref, o0_ref, o1_ref, o2_ref, l0_ref, l1_ref, l2_ref,
             cw_ref, cb_ref, bg_ref, wpa_ref, wpb_ref, wout_ref,
             x1_ref, ya0_ref, yb0_ref, mrg_ref, ya_ref, yb_ref, lsetot_ref, halo_ref, scr):
        i = pl.program_id(0)

        @pl.when(i == 0)
        def _():
            halo_ref[...] = jnp.zeros_like(halo_ref)

        ab = abcv_ref[:, 0:CONV_WIDTH].astype(F32)
        ac = abcv_ref[:, CONV_WIDTH:2 * CONV_WIDTH].astype(F32)
        av = abcv_ref[:, 2 * CONV_WIDTH:3 * CONV_WIDTH].astype(F32)
        ya0, _, u, _, _ = _conv_branch(ab, ac, av, halo_ref[...], cw_ref[...], cb_ref[...])
        halo_ref[...] = u[tm - SUBLANES:tm]
        ya0 = ya0.astype(BF16)
        ya0_ref[...] = ya0
        ya = _dot(ya0, wpa_ref[...])

        o_refs, l_refs = (o0_ref, o1_ref, o2_ref), (l0_ref, l1_ref, l2_ref)
        lses = [_from_streams(l_refs[g], scr, DILATIONS[g], 0, GROUP_WIDTH) for g in range(3)]
        top = jnp.maximum(jnp.maximum(lses[0], lses[1]), lses[2])
        weights = [jnp.exp(lse - top) for lse in lses]
        total = weights[0] + weights[1] + weights[2]
        lsetot_ref[...] = top + jnp.log(total)
        inv_total = 1.0 / total
        yb = jnp.zeros((tm, D_MODEL), F32)
        for g in range(3):
            og = _from_streams(o_refs[g], scr, DILATIONS[g], 0, GROUP_WIDTH)
            yb0 = (weights[g] * inv_total * og).astype(BF16)
            yb0_ref[:, g * GROUP_WIDTH:(g + 1) * GROUP_WIDTH] = yb0
            yb = yb + _dot(yb0, wpb_ref[g * GROUP_WIDTH:(g + 1) * GROUP_WIDTH, :])

        sa = _sigmoid(gates_ref[:, 0:D_MODEL].astype(F32) + bg_ref[0:1, :])
        sb = _sigmoid(gates_ref[:, D_MODEL:2 * D_MODEL].astype(F32) + bg_ref[1:2, :])
        ya_ref[...] = ya.astype(BF16)
        yb_ref[...] = yb.astype(BF16)
        mrg = (sa * ya + sb * yb).astype(BF16)
        mrg_ref[...] = mrg
        x1_ref[...] = x_ref[...] + _dot(mrg, wout_ref[...])

    outs = [jax.ShapeDtypeStruct((s, D_MODEL), F32),
            jax.ShapeDtypeStruct((s, CONV_WIDTH), BF16),
            jax.ShapeDtypeStruct((s, ATTN_WIDTH), BF16),
            jax.ShapeDtypeStruct((s, D_MODEL), BF16),
            jax.ShapeDtypeStruct((s, D_MODEL), BF16),
            jax.ShapeDtypeStruct((s, D_MODEL), BF16),
            jax.ShapeDtypeStruct((s, GROUP_WIDTH), F32)]
    return pl.pallas_call(
        body, name="mix_fwd", grid=(s // tm,), out_shape=outs,
        in_specs=[_rows(tm, D_MODEL), _rows(tm, 3 * CONV_WIDTH), _rows(tm, 2 * D_MODEL)]
        + [_stream_block(tm, d, GROUP_WIDTH) for d in DILATIONS] * 2
        + [_resident((3, CONV_WIDTH)), _resident((1, CONV_WIDTH)), _resident((2, D_MODEL)),
           _resident((CONV_WIDTH, D_MODEL)), _resident((ATTN_WIDTH, D_MODEL)), _resident((D_MODEL, D_MODEL))],
        out_specs=[_rows(tm, D_MODEL), _rows(tm, CONV_WIDTH), _rows(tm, ATTN_WIDTH), _rows(tm, D_MODEL),
                   _rows(tm, D_MODEL), _rows(tm, D_MODEL), _rows(tm, GROUP_WIDTH)],
        scratch_shapes=[pltpu.VMEM((SUBLANES, CONV_WIDTH), F32),
                        pltpu.VMEM((GROUP_WIDTH // LANES * tm, LANES), F32)],
        compiler_params=_params(1))(x, abcv, gates, *o_list, *lse_list, conv_w, conv_b, b_gate, w_pa, w_pb, w_out)


FFN_CHUNK = 512
FFN_UP_ROW_TILE = 256


def _ffn_up_fwd(x1, g, w_up, conv_w, conv_b):
    s = x1.shape[0]
    n = w_up.shape[1]
    tm = FFN_UP_ROW_TILE

    def body(x_ref, g_ref, w_ref, cw_ref, cb_ref, h_ref, up0_ref, up_ref, halo_ref):
        @pl.when(pl.program_id(0) == 0)
        def _():
            halo_ref[...] = jnp.zeros_like(halo_ref)

        xv = x_ref[...]
        r = lax.rsqrt(jnp.mean(xv * xv, axis=-1, keepdims=True) + EPS)
        h = (xv * r * g_ref[...]).astype(BF16)
        h_ref[...] = h
        for lo, size in _col_chunks(n, FFN_CHUNK):
            cols = slice(lo, lo + size)
            y = _dot(h, w_ref[:, cols])
            up0_ref[:, cols] = y.astype(BF16)
            halo = halo_ref[:, cols]
            w = cw_ref[:, cols]
            up = w[0:1] * _shift_down(y, 2, halo) + w[1:2] * _shift_down(y, 1, halo) + w[2:3] * y + cb_ref[:, cols]
            up_ref[:, cols] = up.astype(BF16)
            halo_ref[:, cols] = y[tm - SUBLANES:tm]

    return pl.pallas_call(
        body, name="ffn_up_fwd", grid=(s // tm,),
        out_shape=[jax.ShapeDtypeStruct((s, D_MODEL), BF16), jax.ShapeDtypeStruct((s, n), BF16),
                   jax.ShapeDtypeStruct((s, n), BF16)],
        in_specs=[_rows(tm, D_MODEL), _resident((1, D_MODEL)), _resident((D_MODEL, n)), _resident((3, n)),
                  _resident((1, n))],
        out_specs=[_rows(tm, D_MODEL), _rows(tm, n), _rows(tm, n)],
        scratch_shapes=[pltpu.VMEM((SUBLANES, n), F32)],
        compiler_params=_params(1))(x1, g, w_up, conv_w, conv_b)


def _ffn_act_fwd(x1, up, target, w_down, g_final):
    s = x1.shape[0]
    tm = ROW_TILE

    def body(x1_ref, up_ref, tgt_ref, wd_ref, gf_ref, act_ref, dx2_ref, dx2b_ref, dgf_ref, loss_ref):
        @pl.when(pl.program_id(0) == 0)
        def _():
            dgf_ref[...] = jnp.zeros_like(dgf_ref)
            loss_ref[...] = jnp.zeros_like(loss_ref)

        acc = jnp.zeros((tm, D_MODEL), F32)
        for lo, size in _col_chunks(D_FF, FFN_CHUNK):
            gate = up_ref[:, lo:lo + size].astype(F32)
            val = up_ref[:, D_FF + lo:D_FF + lo + size].astype(F32)
            act = (_silu(gate) * val).astype(BF16)
            act_ref[:, lo:lo + size] = act
            acc = acc + _dot(act, wd_ref[lo:lo + size, :])

        x2 = x1_ref[...] + acc
        r = lax.rsqrt(jnp.mean(x2 * x2, axis=-1, keepdims=True) + EPS)
        xn = x2 * r
        gf = gf_ref[...]
        err = xn * gf - tgt_ref[...]
        loss_ref[...] += (0.5 / D_MODEL) * jnp.sum(err * err)
        dy = err * (1.0 / D_MODEL)
        dgf_ref[...] += _colsum(dy * xn)
        dxn = dy * gf
        dx2 = r * (dxn - xn * jnp.mean(dxn * xn, axis=-1, keepdims=True))
        dx2_ref[...] = dx2
        dx2b_ref[...] = dx2.astype(BF16)

    return pl.pallas_call(
        body, name="ffn_act_fwd", grid=(s // tm,),
        out_shape=[jax.ShapeDtypeStruct((s, D_FF), BF16), jax.ShapeDtypeStruct((s, D_MODEL), F32),
                   jax.ShapeDtypeStruct((s, D_MODEL), BF16),
                   jax.ShapeDtypeStruct((1, D_MODEL), F32), jax.ShapeDtypeStruct((1, LANES), F32)],
        in_specs=[_rows(tm, D_MODEL), _rows(tm, 2 * D_FF), _rows(tm, D_MODEL),
                  _resident((D_FF, D_MODEL)), _resident((1, D_MODEL))],
        out_specs=[_rows(tm, D_FF), _rows(tm, D_MODEL), _rows(tm, D_MODEL),
                   pl.BlockSpec((1, D_MODEL), lambda i: (0, 0)), pl.BlockSpec((1, LANES), lambda i: (0, 0))],
        compiler_params=_params(1))(x1, up, target, w_down, g_final)


def _ffn_act_bwd(dx2b, up, w_down):
    s = dx2b.shape[0]
    tm = ROW_TILE

    def body(dx2_ref, up_ref, wd_ref, dup_ref):
        dx2 = dx2_ref[...]
        for lo, size in _col_chunks(D_FF, FFN_CHUNK):
            gate = up_ref[:, lo:lo + size].astype(F32)
            val = up_ref[:, D_FF + lo:D_FF + lo + size].astype(F32)
            dact = _dot_nt(dx2, wd_ref[lo:lo + size, :])
            sg = _sigmoid(gate)
            dup_ref[:, lo:lo + size] = (dact * val * (sg * (1.0 + gate * (1.0 - sg)))).astype(BF16)
            dup_ref[:, D_FF + lo:D_FF + lo + size] = (dact * (gate * sg)).astype(BF16)

    return pl.pallas_call(
        body, name="ffn_act_bwd", grid=(s // tm,),
        out_shape=jax.ShapeDtypeStruct((s, 2 * D_FF), BF16),
        in_specs=[_rows(tm, D_MODEL), _rows(tm, 2 * D_FF), _resident((D_FF, D_MODEL))],
        out_specs=_rows(tm, 2 * D_FF),
        compiler_params=_params(1))(dx2b, up, w_down)


def _ffn_up_bwd(d_up, up0, w_up, conv_w, x1, g, dres, swap=()):
    s = x1.shape[0]
    n = w_up.shape[1]
    tm = FFN_UP_ROW_TILE
    nt = s // tm
    nw = len(swap)

    def body(*refs):
        dup_ref, up0_ref, w_ref, cw_ref, x_ref, g_ref, dres_ref = refs[:7]
        slab_refs = refs[7:7 + nw]
        dx_ref, dxb_ref, dg_ref, dup0_ref, small_ref = refs[7 + nw:12 + nw]
        swapped_refs = refs[12 + nw:12 + 2 * nw]
        next_ref = refs[12 + 2 * nw]
        sems = refs[13 + 2 * nw:]

        @pl.when(pl.program_id(0) == 0)
        def _():
            next_ref[...] = jnp.zeros_like(next_ref)
            small_ref[...] = jnp.zeros_like(small_ref)
            dg_ref[...] = jnp.zeros_like(dg_ref)
            for cp in _sibling_swap_copies(slab_refs, swapped_refs, *sems) if nw else ():
                cp.start()

        dh = jnp.zeros((tm, D_MODEL), F32)
        for lo, size in _col_chunks(n, FFN_CHUNK):
            cols = slice(lo, lo + size)
            dz = dup_ref[:, cols].astype(F32)
            x0 = up0_ref[:, cols].astype(F32)
            nxt = next_ref[:, cols]
            dz1 = _shift_up(dz, 1, nxt)
            dz2 = _shift_up(dz, 2, nxt)
            next_ref[:, cols] = dz[0:SUBLANES]
            small_ref[0:1, cols] += _colsum(dz2 * x0)
            small_ref[1:2, cols] += _colsum(dz1 * x0)
            small_ref[2:3, cols] += _colsum(dz * x0)
            small_ref[3:4, cols] += _colsum(dz)
            w = cw_ref[:, cols]
            dup0 = (w[2:3] * dz + w[1:2] * dz1 + w[0:1] * dz2).astype(BF16)
            dup0_ref[:, cols] = dup0
            dh = dh + _dot_nt(dup0, w_ref[:, cols])
        xv = x_ref[...]
        r = lax.rsqrt(jnp.mean(xv * xv, axis=-1, keepdims=True) + EPS)
        xn = xv * r
        dg_ref[...] += _colsum(dh * xn)
        dxn = dh * g_ref[...]
        dx = dres_ref[...] + r * (dxn - xn * jnp.mean(dxn * xn, axis=-1, keepdims=True))
        dx_ref[...] = dx
        dxb_ref[...] = dx.astype(BF16)

        if nw:
            @pl.when(pl.program_id(0) == nt - 1)
            def _():
                for cp in _sibling_swap_copies(slab_refs, swapped_refs, *sems):
                    cp.wait()

    rows = lambda width: pl.BlockSpec((tm, width), lambda i: (nt - 1 - i, 0))
    return pl.pallas_call(
        body, name="ffn_up_bwd", grid=(nt,),
        out_shape=[jax.ShapeDtypeStruct((s, D_MODEL), F32), jax.ShapeDtypeStruct((s, D_MODEL), BF16),
                   jax.ShapeDtypeStruct((1, D_MODEL), F32), jax.ShapeDtypeStruct((s, n), BF16),
                   jax.ShapeDtypeStruct((SUBLANES, n), F32)] + _swap_shapes(swap),
        in_specs=[rows(n), rows(n), _resident((D_MODEL, n)), _resident((3, n)), rows(D_MODEL),
                  _resident((1, D_MODEL)), rows(D_MODEL)] + [_ANY] * nw,
        out_specs=[rows(D_MODEL), rows(D_MODEL), pl.BlockSpec((1, D_MODEL), lambda i: (0, 0)), rows(n),
                   pl.BlockSpec((SUBLANES, n), lambda i: (0, 0))] + [_ANY] * nw,
        scratch_shapes=[pltpu.VMEM((SUBLANES, n), F32)] + (_dma_sems(nw) if nw else []),
        compiler_params=_params(1))(d_up, up0, w_up, conv_w, x1, g, dres, *swap)


def _inproj_bwd(d_abcv, d_gates, d_qkvs, w_in_t, x, g, dres):
    s = x.shape[0]
    tm = ROW_TILE
    gw = GROUP_WIDTH

    def body(dabcv_ref, dgates_ref, dq0_ref, dq1_ref, dq2_ref, w_ref, x_ref, g_ref, dres_ref, dx_ref, dg_ref, scr):
        @pl.when(pl.program_id(0) == 0)
        def _():
            dg_ref[...] = jnp.zeros_like(dg_ref)

        dh = jnp.zeros((tm, D_MODEL), F32)
        for src, width, wrow in ((dabcv_ref, 3 * CONV_WIDTH, COL_ABCV), (dgates_ref, 2 * D_MODEL, COL_GATES)):
            for lo, size in _col_chunks(width, 512):
                dh = dh + _dot(src[:, lo:lo + size], w_ref[wrow + lo:wrow + lo + size, :])
        for gi, (d, dq_ref) in enumerate(zip(DILATIONS, (dq0_ref, dq1_ref, dq2_ref))):
            for j, base in enumerate((COL_Q, COL_K, COL_V)):
                dy = _from_streams(dq_ref, scr, d, j * gw, gw).astype(BF16)
                wrow = base + gi * gw
                dh = dh + _dot(dy, w_ref[wrow:wrow + gw, :])
        xv = x_ref[...]
        r = lax.rsqrt(jnp.mean(xv * xv, axis=-1, keepdims=True) + EPS)
        xn = xv * r
        dg_ref[...] += _colsum(dh * xn)
        dxn = dh * g_ref[...]
        dx_ref[...] = dres_ref[...] + r * (dxn - xn * jnp.mean(dxn * xn, axis=-1, keepdims=True))

    return pl.pallas_call(
        body, name="inproj_bwd", grid=(s // tm,),
        out_shape=[jax.ShapeDtypeStruct((s, D_MODEL), F32), jax.ShapeDtypeStruct((1, D_MODEL), F32)],
        in_specs=[_rows(tm, 3 * CONV_WIDTH), _rows(tm, 2 * D_MODEL)]
        + [_stream_block(tm, d, 3 * gw) for d in DILATIONS]
        + [_resident((D_IN, D_MODEL)), _rows(tm, D_MODEL), _resident((1, D_MODEL)), _rows(tm, D_MODEL)],
        out_specs=[_rows(tm, D_MODEL), pl.BlockSpec((1, D_MODEL), lambda i: (0, 0))],
        scratch_shapes=[pltpu.VMEM((gw // LANES * tm, LANES), F32)],
        compiler_params=_params(1))(d_abcv, d_gates, *d_qkvs, w_in_t, x, g, dres)


def _mix_bwd(dx1, abcv, gates, ya, yb, yb0, lsetot, conv_w, conv_b, b_gate, w_pa, w_pb, w_out, exchange=()):
    s = dx1.shape[0]
    tm = ROW_TILE
    nt = s // tm
    hb = tm // (2 * SUBLANES)
    nx = len(exchange)

    def body(*refs):
        (dx1_ref, abcv_ref, pre_ref, gates_ref, ya_ref, yb_ref, yb0_ref, lsetot_ref,
         cw_ref, cb_ref, bg_ref, wpa_ref, wpb_ref, wout_ref) = refs[:14]
        part_refs = refs[14:14 + nx]
        (dya_ref, dyb_ref, dgates_ref, dabcv_ref, dyb0_ref, dyl0_ref, dyl1_ref, dyl2_ref, aux0_ref, aux1_ref,
         aux2_ref, sm_gate_ref, sm_conv_ref) = refs[14 + nx:27 + nx]
        recv_refs = refs[27 + nx:27 + 2 * nx]
        next_ref, scr = refs[27 + 2 * nx:29 + 2 * nx]
        sems = refs[29 + 2 * nx:]
        i = pl.program_id(0)

        @pl.when(i == 0)
        def _():
            next_ref[...] = jnp.zeros_like(next_ref)
            sm_gate_ref[...] = jnp.zeros_like(sm_gate_ref)
            sm_conv_ref[...] = jnp.zeros_like(sm_conv_ref)
            for cp in _chip_exchange_copies(part_refs, recv_refs, *sems) if nx else ():
                cp.start()

        not_first = (i < nt - 1).astype(F32)
        dm = _dot_nt(dx1_ref[...].astype(BF16), wout_ref[...])
        sa = _sigmoid(gates_ref[:, 0:D_MODEL].astype(F32) + bg_ref[0:1, :])
        sb = _sigmoid(gates_ref[:, D_MODEL:2 * D_MODEL].astype(F32) + bg_ref[1:2, :])
        dya = (dm * sa).astype(BF16)
        dyb = (dm * sb).astype(BF16)
        dya_ref[...] = dya
        dyb_ref[...] = dyb
        dga = dm * ya_ref[...].astype(F32) * (sa * (1.0 - sa))
        dgb = dm * yb_ref[...].astype(F32) * (sb * (1.0 - sb))
        dgates_ref[:, 0:D_MODEL] = dga.astype(BF16)
        dgates_ref[:, D_MODEL:2 * D_MODEL] = dgb.astype(BF16)
        sm_gate_ref[0:1, :] += _colsum(dga)
        sm_gate_ref[1:2, :] += _colsum(dgb)

        dya0 = _dot_nt(dya, wpa_ref[...])
        ab = abcv_ref[:, 0:CONV_WIDTH].astype(F32)
        ac = abcv_ref[:, CONV_WIDTH:2 * CONV_WIDTH].astype(F32)
        av = abcv_ref[:, 2 * CONV_WIDTH:3 * CONV_WIDTH].astype(F32)
        pre = pre_ref[...].astype(F32) * not_first
        halo_u = (pre[:, CONV_WIDTH:2 * CONV_WIDTH] * pre[:, 2 * CONV_WIDTH:3 * CONV_WIDTH])[SUBLANES:2 * SUBLANES]
        w = cw_ref[...]
        _, cv, u, sh1, sh2 = _conv_branch(ab, ac, av, halo_u, w, cb_ref[...])
        dcv = dya0 * ab
        sm_conv_ref[0:1, :] += _colsum(dcv * sh2)
        sm_conv_ref[1:2, :] += _colsum(dcv * sh1)
        sm_conv_ref[2:3, :] += _colsum(dcv * u)
        sm_conv_ref[3:4, :] += _colsum(dcv)
        nxt = next_ref[...]
        du = w[2:3] * dcv + w[1:2] * _shift_up(dcv, 1, nxt) + w[0:1] * _shift_up(dcv, 2, nxt)
        next_ref[...] = dcv[0:SUBLANES]
        dabcv_ref[:, 0:CONV_WIDTH] = (dya0 * cv).astype(BF16)
        dabcv_ref[:, CONV_WIDTH:2 * CONV_WIDTH] = (du * av).astype(BF16)
        dabcv_ref[:, 2 * CONV_WIDTH:3 * CONV_WIDTH] = (du * ac).astype(BF16)

        head_r = lax.broadcasted_iota(jnp.int32, (GROUP_WIDTH, GROUP_WIDTH), 0) // HEAD_DIM
        head_c = lax.broadcasted_iota(jnp.int32, (GROUP_WIDTH, GROUP_WIDTH), 1) // HEAD_DIM
        same_head = (head_r == head_c).astype(BF16)
        prod = jnp.zeros((tm, GROUP_WIDTH), F32)
        dyb0s = []
        for g in range(3):
            cols = slice(g * GROUP_WIDTH, (g + 1) * GROUP_WIDTH)
            dyb0 = _dot_nt(dyb, wpb_ref[cols, :])
            dyb0_ref[:, cols] = dyb0.astype(BF16)
            dyb0s.append(dyb0)
            prod = prod + dyb0 * yb0_ref[:, cols].astype(F32)
        hi = prod.astype(BF16)
        mid = (prod - hi.astype(F32)).astype(BF16)
        lo = (prod - hi.astype(F32) - mid.astype(F32)).astype(BF16)
        delta = _dot(hi, same_head) + _dot(mid, same_head) + _dot(lo, same_head)
        lse_c = _compact_heads(lsetot_ref[...])
        delta_c = _compact_heads(delta)
        for g, (dy_ref, aux_ref) in enumerate(zip((dyl0_ref, dyl1_ref, dyl2_ref), (aux0_ref, aux1_ref, aux2_ref))):
            d = DILATIONS[g]
            _to_streams(dyb0s[g], scr, dy_ref, d, 0)
            _to_streams(lse_c, scr, aux_ref, d, 0)
            _to_streams(delta_c, scr, aux_ref, d, LANES)

        if nx:
            @pl.when(i == nt - 1)
            def _():
                for cp in _chip_exchange_copies(part_refs, recv_refs, *sems):
                    cp.wait()

    rev = lambda i: (nt - 1 - i, 0)
    pre = lambda i: (jnp.maximum((nt - 1 - i) * hb - 1, 0), 0)
    rows = lambda width: pl.BlockSpec((tm, width), rev)
    outs = [jax.ShapeDtypeStruct((s, D_MODEL), BF16), jax.ShapeDtypeStruct((s, D_MODEL), BF16),
            jax.ShapeDtypeStruct((s, 2 * D_MODEL), BF16), jax.ShapeDtypeStruct((s, 3 * CONV_WIDTH), BF16),
            jax.ShapeDtypeStruct((s, ATTN_WIDTH), BF16)]
    outs += [jax.ShapeDtypeStruct((d, s // d, GROUP_WIDTH), BF16) for d in DILATIONS]
    outs += [jax.ShapeDtypeStruct((d, s // d, 2 * LANES), F32) for d in DILATIONS]
    outs += [jax.ShapeDtypeStruct((SUBLANES, D_MODEL), F32), jax.ShapeDtypeStruct((SUBLANES, CONV_WIDTH), F32)]
    outs += [jax.ShapeDtypeStruct((3,) + a.shape[1:], a.dtype) for a in exchange]
    return pl.pallas_call(
        body, name="mix_bwd", grid=(nt,), out_shape=outs,
        in_specs=[rows(D_MODEL), rows(3 * CONV_WIDTH), pl.BlockSpec((2 * SUBLANES, 3 * CONV_WIDTH), pre),
                  rows(2 * D_MODEL), rows(D_MODEL), rows(D_MODEL), rows(ATTN_WIDTH), rows(GROUP_WIDTH),
                  _resident((3, CONV_WIDTH)), _resident((1, CONV_WIDTH)), _resident((2, D_MODEL)),
                  _resident((CONV_WIDTH, D_MODEL)), _resident((ATTN_WIDTH, D_MODEL)), _resident((D_MODEL, D_MODEL))]
        + [_ANY] * nx,
        out_specs=[rows(D_MODEL), rows(D_MODEL), rows(2 * D_MODEL), rows(3 * CONV_WIDTH), rows(ATTN_WIDTH)]
        + [_rev_stream_block(tm, d, GROUP_WIDTH, nt) for d in DILATIONS]
        + [_rev_stream_block(tm, d, 2 * LANES, nt) for d in DILATIONS]
        + [pl.BlockSpec((SUBLANES, D_MODEL), lambda i: (0, 0)), pl.BlockSpec((SUBLANES, CONV_WIDTH), lambda i: (0, 0))]
        + [_ANY] * nx,
        scratch_shapes=[pltpu.VMEM((SUBLANES, CONV_WIDTH), F32),
                        pltpu.VMEM((GROUP_WIDTH // LANES * tm, LANES), F32)] + (_dma_sems(3 * nx) if nx else []),
        compiler_params=_params(1))(dx1, abcv, abcv, gates, ya, yb, yb0, lsetot,
                                    conv_w, conv_b, b_gate, w_pa, w_pb, w_out, *exchange)


def _attn_bwd(qkv, dy, aux, gi, exchange=(), swap=()):
    d, length, _ = qkv.shape
    nb = length // ATTN_BLOCK
    q = ATTN_BLOCK
    gw = GROUP_WIDTH
    g = min(ATTN_BLOCKS_PER_STEP, nb)
    assert g >= 2 and nb % g == 0
    ns = nb // g
    lag = 1 if ns > 1 else 0
    tail = (g - 1) * q
    nx, nw = len(exchange), len(swap)

    def body(*refs):
        q_ref, kp_ref, kc_ref, vp_ref, vc_ref, dy_ref, aux_ref = refs[:7]
        part_refs = refs[7:7 + nx]
        slab_refs = refs[7 + nx:7 + nx + nw]
        pos = 7 + nx + nw
        out_ref = refs[pos]
        recv_refs = refs[pos + 1:pos + 1 + nx]
        swapped_refs = refs[pos + 1 + nx:pos + 1 + nx + nw]
        pos += 1 + nx + nw
        dq_ref, dkv_ref, bias_ref = refs[pos:pos + 3]
        sems = refs[pos + 3:]
        n = pl.program_id(1)

        def copies():
            cps = _chip_exchange_copies(part_refs, recv_refs, sems[0], sems[1]) if nx else []
            return cps + (_sibling_swap_copies(slab_refs, swapped_refs, sems[-2], sems[-1]) if nw else [])

        @pl.when((pl.program_id(0) == 0) & (n == 0))
        def _():
            _store_band_biases(bias_ref)
            for cp in copies():
                cp.start()

        if nx or nw:
            @pl.when((pl.program_id(0) == d - 1) & (n == ns - 1 + lag))
            def _():
                for cp in copies():
                    cp.wait()

        def emit(rows):
            out_ref[rows, gw:2 * gw] = dkv_ref[0, rows].astype(BF16)
            out_ref[rows, 2 * gw:3 * gw] = dkv_ref[1, rows].astype(BF16)

        if lag:
            @pl.when(n > 0)
            def _():
                out_ref[:, 0:gw] = dq_ref[...].astype(BF16)
                emit(slice(0, tail))

            @pl.when(n == ns)
            def _():
                emit(slice(tail, g * q))

        @pl.when(n < ns)
        def _():
            kfull = jnp.concatenate([kp_ref[...], kc_ref[...]], axis=0)
            vfull = jnp.concatenate([vp_ref[...], vc_ref[...]], axis=0)
            for j in range(g):
                rows = slice(j * q, (j + 1) * q)
                qs = _stack_heads(q_ref[rows, :])
                dys = _stack_heads(dy_ref[rows, :])
                k2 = kfull[j * q:(j + 2) * q]
                v2 = vfull[j * q:(j + 2) * q]
                lse = _compact_head_col(aux_ref[rows, 0:LANES])
                delta = _compact_head_col(aux_ref[rows, LANES:2 * LANES])
                bias = jnp.where(n == 0, bias_ref[1], bias_ref[0]) if j == 0 else bias_ref[0]
                p = jnp.exp(_dot_nt(qs, k2) + bias - lse)
                dp = _dot_nt(dys, v2)
                ds = (p * (dp - delta)).astype(BF16)
                dq_j = _unstack_heads(_dot(ds, k2)) * ATTN_SCALE
                dk2 = _dot_tn(ds, qs)
                dv2 = _dot_tn(p.astype(BF16), dys)
                if j == 0:
                    @pl.when(n > 0)
                    def _():
                        out_ref[tail:g * q, gw:2 * gw] = (dkv_ref[0, tail:g * q] + dk2[0:q]).astype(BF16)
                        out_ref[tail:g * q, 2 * gw:3 * gw] = (dkv_ref[1, tail:g * q] + dv2[0:q]).astype(BF16)
                else:
                    dkv_ref[0, (j - 1) * q:j * q] += dk2[0:q]
                    dkv_ref[1, (j - 1) * q:j * q] += dv2[0:q]
                dkv_ref[0, rows] = dk2[q:2 * q]
                dkv_ref[1, rows] = dv2[q:2 * q]
                dq_ref[rows, :] = dq_j
            if not lag:
                out_ref[:, 0:gw] = dq_ref[...].astype(BF16)
                emit(slice(0, g * q))

    cur, prev = _attn_block_specs(g, nb, clamp_last=True)
    return pl.pallas_call(
        body, name=f"attn_bwd_g{gi}", grid=(d, ns + lag),
        out_shape=[jax.ShapeDtypeStruct((d, length, 3 * gw), BF16)]
        + [jax.ShapeDtypeStruct((3,) + a.shape[1:], a.dtype) for a in exchange] + _swap_shapes(swap),
        in_specs=[cur(0), prev(1), cur(1), prev(2), cur(2), cur(0), cur(0, 2 * LANES)] + [_ANY] * (nx + nw),
        out_specs=[pl.BlockSpec((None, g * q, 3 * gw), lambda r, n: (r, jnp.maximum(n - lag, 0), 0))]
        + [_ANY] * (nx + nw),
        scratch_shapes=[pltpu.VMEM((g * q, gw), F32), pltpu.VMEM((2, g * q, gw), F32),
                        pltpu.VMEM((2, HEADS_PER_GROUP * q, 2 * q), F32)]
        + (_dma_sems(3 * nx) if nx else []) + (_dma_sems(nw) if nw else []),
        compiler_params=_params(2))(qkv, qkv, qkv, qkv, qkv, dy, aux, *exchange, *swap)


def _matmul_tn(name, a, b, col_tile=1024, row_tile=2048, slabs=0, swap=()):
    s, k = a.shape
    n = b.shape[1]
    tk = min(row_tile, s)
    tn = col_tile
    steps = s // tk
    nw = len(swap)

    def body(*refs):
        a_ref, b_ref = refs[:2]
        slab_refs = refs[2:2 + nw]
        o_ref = refs[2 + nw]
        swapped_refs = refs[3 + nw:3 + 2 * nw]
        acc_ref = refs[3 + 2 * nw]
        sems = refs[4 + 2 * nw:]
        t = pl.program_id(1)

        if nw:
            @pl.when((pl.program_id(0) == 0) & (t == 0))
            def _():
                for cp in _sibling_swap_copies(slab_refs, swapped_refs, *sems):
                    cp.start()

            @pl.when((pl.program_id(0) == n // tn - 1) & (t == steps - 1))
            def _():
                for cp in _sibling_swap_copies(slab_refs, swapped_refs, *sems):
                    cp.wait()

        @pl.when(t == 0)
        def _():
            acc_ref[...] = jnp.zeros_like(acc_ref)

        acc_ref[...] += _dot_tn(a_ref[...], b_ref[...])

        @pl.when(t == steps - 1)
        def _():
            if slabs:
                for q in range(per_tile):
                    o_ref[q] = acc_ref[:, q * width:(q + 1) * width].astype(BF16)
            else:
                o_ref[...] = acc_ref[...].astype(BF16)

    if slabs:
        width = n // slabs
        per_tile = tn // width
        out_shape = jax.ShapeDtypeStruct((slabs, k, width), BF16)
        out_spec = pl.BlockSpec((per_tile, k, width), lambda j, t: (j, 0, 0))
    else:
        out_shape = jax.ShapeDtypeStruct((k, n), BF16)
        out_spec = pl.BlockSpec((k, tn), lambda j, t: (0, j))
    res = pl.pallas_call(
        body, name=name, grid=(n // tn, steps), out_shape=[out_shape] + _swap_shapes(swap),
        in_specs=[pl.BlockSpec((tk, k), lambda j, t: (t, 0)), pl.BlockSpec((tk, tn), lambda j, t: (t, j))] + [_ANY] * nw,
        out_specs=[out_spec] + [_ANY] * nw,
        scratch_shapes=[pltpu.VMEM((k, tn), F32)] + (_dma_sems(nw) if nw else []),
        compiler_params=_params(2))(a, b, *swap)
    return res if nw else res[0]


def _sibling_swap_halves(name, slabs):
    na = len(slabs)

    def body(*refs):
        src_refs, out_refs = refs[:na], refs[na:2 * na]
        send_sems, recv_sems = refs[2 * na:]
        x, y, c, _ = _mesh_position()
        cps = []
        for a in range(na):
            theirs = _half_rows(1 - c, src_refs[a].shape[1] // 2)
            cps.append(pltpu.make_async_remote_copy(
                src_ref=src_refs[a].at[:, theirs, :], dst_ref=out_refs[a], send_sem=send_sems.at[a],
                recv_sem=recv_sems.at[a], device_id=(x, y, 1 - c), device_id_type=MESH_ID))
        for cp in cps:
            cp.start()
        for cp in cps:
            cp.wait()

    return pl.pallas_call(
        body, name=name,
        out_shape=[jax.ShapeDtypeStruct((a.shape[0], a.shape[1] // 2, a.shape[2]), a.dtype) for a in slabs],
        in_specs=[_ANY] * na, out_specs=[_ANY] * na,
        scratch_shapes=[pltpu.SemaphoreType.DMA((na,)), pltpu.SemaphoreType.DMA((na,))])(*slabs)


_HBM = pl.BlockSpec(memory_space=pltpu.HBM)
_SEM = pl.BlockSpec(memory_space=pltpu.SEMAPHORE)
_DATAFLOW = pltpu.SideEffectType.DATAFLOW_SIDE_EFFECTING


def _chip_exchange_start(partial):
    _, rows, cols = partial.shape
    landing = jax.ShapeDtypeStruct((3, rows, cols), partial.dtype)

    def body(src_ref, land_ref, send_sems, recv_sems, src_thru, land_thru, token):
        for cp in _chip_exchange_copies([src_ref], [land_ref], send_sems, recv_sems):
            cp.start()
        token[...] = jnp.zeros_like(token)

    return pl.pallas_call(
        body, name="grad_exchange_start",
        out_shape=(pltpu.SemaphoreType.DMA((3,)), pltpu.SemaphoreType.DMA((3,)),
                   pltpu.HBM(partial.shape, partial.dtype), pltpu.HBM(landing.shape, landing.dtype),
                   jax.ShapeDtypeStruct((SUBLANES, LANES), F32)),
        in_specs=(_HBM, _HBM), out_specs=(_SEM, _SEM, _HBM, _HBM, _VMEM), input_output_aliases={0: 2, 1: 3},
        compiler_params=pltpu.CompilerParams(has_side_effects=_DATAFLOW),
    )(pltpu.with_memory_space_constraint(partial, pltpu.HBM),
      pltpu.with_memory_space_constraint(lax.empty(landing.shape, landing.dtype), pltpu.HBM))


def _chip_exchange_wait(send_sems, recv_sems, src_thru, land_thru, after):
    def body(src_ref, land_ref, send_sems, recv_sems, after_ref, src_out, land_out):
        for cp in _chip_exchange_copies([src_ref], [land_ref], send_sems, recv_sems):
            cp.wait_send()
            cp.wait_recv()

    return pl.pallas_call(
        body, name="grad_exchange_wait",
        out_shape=(pltpu.HBM(src_thru.shape, src_thru.dtype), pltpu.HBM(land_thru.shape, land_thru.dtype)),
        in_specs=(_HBM, _HBM, _SEM, _SEM, _ANY), out_specs=(_HBM, _HBM), input_output_aliases={0: 0, 1: 1},
        compiler_params=pltpu.CompilerParams(has_side_effects=_DATAFLOW),
    )(src_thru, land_thru, send_sems, recv_sems, after)


def _sibling_share_copies(refs, send_sems, recv_sems):
    x, y, c, _ = _mesh_position()
    cps = []
    for a, ref in enumerate(refs):
        mine = ref.at[0, _half_rows(c, ref.shape[1] // 2)]
        cps.append(_remote_copy(a, mine, mine, (x, y, 1 - c), send_sems, recv_sems))
    return cps


def _add_sibling(name, slabs, received, core):
    na = len(slabs)
    halves = [a.shape[1] // 2 for a in slabs]

    def body(core_ref, *refs):
        for a in range(na):
            refs[2 * na + a][...] = (refs[a][...].astype(F32) + refs[na + a][...].astype(F32)).astype(BF16)

    def block(a, mine):
        if mine:
            return pl.BlockSpec((None, halves[a], slabs[a].shape[2]), lambda s, core_ref: (s, core_ref[0], 0))
        return pl.BlockSpec((None, halves[a], slabs[a].shape[2]), lambda s, core_ref: (s, 0, 0))

    grid_spec = pltpu.PrefetchScalarGridSpec(
        num_scalar_prefetch=1, grid=(N_CHIPS,),
        in_specs=[block(a, True) for a in range(na)] + [block(a, False) for a in range(na)],
        out_specs=[block(a, False) for a in range(na)])
    return pl.pallas_call(body, name=name, grid_spec=grid_spec,
                          out_shape=[jax.ShapeDtypeStruct(r.shape, BF16) for r in received],
                          compiler_params=_params(1))(core, *slabs, *received)


def _sum_chips(name, partials, received, chip_core):
    na = len(partials)

    def body(cc_ref, *refs):
        for a in range(na):
            acc = refs[a][...].astype(F32)
            for k in range(3):
                acc = acc + refs[na + a][k].astype(F32)
            refs[2 * na + a][...] = acc

    def own(p):
        return pl.BlockSpec((None,) + p.shape[1:], lambda i, cc_ref: (cc_ref[0], 0, 0))

    def mine(p):
        return pl.BlockSpec((None,) + p.shape[1:], lambda i, cc_ref: (0, cc_ref[1], 0))

    grid_spec = pltpu.PrefetchScalarGridSpec(
        num_scalar_prefetch=1, grid=(1,),
        in_specs=[own(p) for p in partials] + [pl.BlockSpec(r.shape, lambda i, cc_ref: (0, 0, 0)) for r in received],
        out_specs=[mine(p) for p in partials])
    return pl.pallas_call(body, name=name, grid_spec=grid_spec,
                          out_shape=[jax.ShapeDtypeStruct((1, 2 * p.shape[1], p.shape[2]), F32) for p in partials],
                          compiler_params=_params(1))(chip_core, *partials, *received)


def _adam_math(w, g, m, v):
    nm = ADAM_B1 * m + (1.0 - ADAM_B1) * g
    nv = ADAM_B2 * v + (1.0 - ADAM_B2) * jnp.square(g)
    m_hat = nm / (1.0 - ADAM_B1 ** ADAM_STEP)
    v_hat = nv / (1.0 - ADAM_B2 ** ADAM_STEP)
    delta = -ADAM_LR * (m_hat / (jnp.sqrt(v_hat) + ADAM_EPS) + ADAM_WD * w)
    return delta, nm, nv


ADAMW_STEPS = 8


def _adamw(ws, gs, ms, vs):
    na = len(ws)

    def body(*refs):
        for a in range(na):
            w_ref, g_ref, m_ref, v_ref = (refs[k * na + a] for k in range(4))
            g_out_ref, d_ref, nm_ref, nv_ref = (refs[(4 + k) * na + a] for k in range(4))
            gv = g_ref[...]
            g_out_ref[...] = gv
            d_ref[...], nm_ref[...], nv_ref[...] = _adam_math(w_ref[...], gv, m_ref[...], v_ref[...])

    specs = [pl.BlockSpec((None, w.shape[1] // ADAMW_STEPS, w.shape[2]), lambda i: (0, i, 0)) for w in ws]
    outs = pl.pallas_call(
        body, name="adamw", grid=(ADAMW_STEPS,), out_shape=[jax.ShapeDtypeStruct(w.shape, F32) for w in ws] * 4,
        in_specs=specs * 4, out_specs=specs * 4, compiler_params=_params(1))(*ws, *gs, *ms, *vs)
    return [[outs[k * na + a] for k in range(4)] for a in range(na)]


SMALL_PARAMS = ("norm_mix_g", "b_gate", "conv_a_w", "conv_a_b", "norm_ffn_g", "ffn_conv_w", "ffn_conv_b", "final_norm_g")


def _small_update(partials, params, moments_m, moments_v, halves):
    na = len(partials)
    npar = len(SMALL_PARAMS)
    nh = len(halves)

    def body(*refs):
        in_refs = refs[:na]
        w_refs = refs[na:na + npar]
        m_refs = refs[na + npar:na + 2 * npar]
        v_refs = refs[na + 2 * npar:na + 3 * npar]
        pos = na + 3 * npar + nh
        loss_ref = refs[pos]
        out_refs = refs[pos + 1:pos + 1 + 4 * npar]
        big_refs = refs[pos + 1 + 4 * npar:pos + 1 + 4 * npar + nh]
        pos += 1 + 4 * npar + nh
        acc_refs = refs[pos:pos + na]
        recv_refs = refs[pos + na:pos + 4 * na]
        send_sems, recv_sems, share_send, share_recv = refs[pos + 4 * na:]
        x, y, c, _ = _mesh_position()
        chip = 2 * x + y
        for cp in _sibling_share_copies(big_refs, share_send, share_recv):
            cp.start()
        for a in range(na):
            acc_refs[a][...] = in_refs[a][...]
        for stage, peer in enumerate(((x, y, 1 - c), (x, 1 - y, c), (1 - x, y, c))):
            cps = []
            for a in range(na):
                k = stage * na + a
                cps.append(pltpu.make_async_remote_copy(src_ref=acc_refs[a], dst_ref=recv_refs[k], send_sem=send_sems.at[k],
                                                        recv_sem=recv_sems.at[k], device_id=peer, device_id_type=MESH_ID))
            for cp in cps:
                cp.start()
            for cp in cps:
                cp.wait()
            for a in range(na):
                acc_refs[a][...] = acc_refs[a][...] + recv_refs[stage * na + a][...]

        mix, ffn, fin, gate, conv, ffnc, loss = acc_refs
        loss_ref[...] = loss[...]

        def cols(width):
            return pl.ds(pl.multiple_of(chip * width, LANES), width)

        grads = {
            "norm_mix_g": mix[...], "norm_ffn_g": ffn[...], "final_norm_g": fin[...],
            "b_gate": gate[0:2, cols(D_MODEL // N_CHIPS)],
            "conv_a_w": conv[0:3, cols(CONV_WIDTH // N_CHIPS)], "conv_a_b": conv[3:4, :],
            "ffn_conv_w": ffnc[0:3, cols(2 * D_FF // N_CHIPS)], "ffn_conv_b": ffnc[3:4, :]}
        for i, name in enumerate(SMALL_PARAMS):
            g = grads[name]
            if len(w_refs[i].shape) == 3:
                results = (g,) + _adam_math(w_refs[i][0], g, m_refs[i][0], v_refs[i][0])
                for o_ref, val in zip(out_refs[4 * i:4 * i + 4], results):
                    o_ref[0] = val
            else:
                results = (g,) + _adam_math(w_refs[i][...], g, m_refs[i][...], v_refs[i][...])
                for o_ref, val in zip(out_refs[4 * i:4 * i + 4], results):
                    o_ref[...] = val

        for cp in _sibling_share_copies(big_refs, share_send, share_recv):
            cp.wait()

    outs = [jax.ShapeDtypeStruct(partials[-1].shape, F32)]
    for w in params:
        outs += [jax.ShapeDtypeStruct(w.shape, F32)] * 4
    n_small_out = len(outs)
    outs += [jax.ShapeDtypeStruct(h.shape, h.dtype) for h in halves]
    scratch = [pltpu.VMEM(p.shape, F32) for p in partials]
    scratch += [pltpu.VMEM(p.shape, F32) for _ in range(3) for p in partials]
    scratch += _dma_sems(3 * na) + _dma_sems(nh)
    n_in = na + 3 * npar
    return pl.pallas_call(
        body, name="small_update", out_shape=outs, in_specs=[_VMEM] * n_in + [_ANY] * nh,
        out_specs=[_VMEM] * n_small_out + [_ANY] * nh,
        input_output_aliases={n_in + a: n_small_out + a for a in range(nh)},
        scratch_shapes=scratch)(*partials, *params, *moments_m, *moments_v, *halves)


def _gathered_columns(g):
    return jnp.transpose(g, (1, 0, 2)).reshape(g.shape[1], N_CHIPS * g.shape[2])


def kernel(x, norm_mix_g, w_in, b_gate, conv_a_w, conv_a_b, w_proj_a, w_proj_b, w_out, norm_ffn_g, w_up, ffn_conv_w, ffn_conv_b, w_down, final_norm_g, loss_target, m_norm_mix_g, m_w_in, m_b_gate, m_conv_a_w, m_conv_a_b, m_w_proj_a, m_w_proj_b, m_w_out, m_norm_ffn_g, m_w_up, m_ffn_conv_w, m_ffn_conv_b, m_w_down, m_final_norm_g, v_norm_mix_g, v_w_in, v_b_gate, v_conv_a_w, v_conv_a_b, v_w_proj_a, v_w_proj_b, v_w_out, v_norm_ffn_g, v_w_up, v_ffn_conv_w, v_ffn_conv_b, v_w_down, v_final_norm_g):
    chip = (2 * lax.axis_index("x") + lax.axis_index("y")).astype(jnp.int32)
    core = lax.axis_index("c").astype(jnp.int32)
    core_arr = core.reshape(1)
    chip_core = jnp.stack([chip, core])
    xs, target = x[0], loss_target[0]
    g_final = final_norm_g.reshape(1, D_MODEL)

    def own_slot(gathered, own):
        return lax.dynamic_update_slice(gathered, own, (chip, 0, 0))

    w_in_t, m_w_in_t, v_w_in_t = (jnp.swapaxes(a, 1, 2) for a in (w_in, m_w_in, v_w_in))
    w_in_tb = w_in_t.astype(BF16)
    h1, h1_streams4, h1_streams16, g_in = _norm_fwd(xs, norm_mix_g, w_in_tb)
    w_in_full_t = own_slot(g_in, w_in_tb).reshape(D_IN, D_MODEL)
    later_w = [w_proj_a, w_proj_b, w_out, w_up, w_down]
    later_b = [w.astype(BF16) for w in later_w]
    small_sharded = [b_gate, conv_a_w, ffn_conv_w]
    fwd = _inproj_fwd(h1, w_in_full_t, later_b, small_sharded)
    abcv, gates, qkv0, qkv1, qkv2 = fwd[:5]
    gathered_big, gathered_small = fwd[5:10], fwd[10:13]
    attn0 = _attn_fwd(qkv0, 0, forward=gathered_big)
    attn = [attn0[:2], _attn_fwd(qkv1, 1), _attn_fwd(qkv2, 2)]
    g_pa, g_pb, g_out, g_up, g_down = [own_slot(g, own) for g, own in zip(attn0[2:], later_b)]
    g_bgate, g_convw, g_ffnw = [own_slot(g, own) for g, own in zip(gathered_small, small_sharded)]
    w_pa_full, w_pb_full, w_up_full = _gathered_columns(g_pa), _gathered_columns(g_pb), _gathered_columns(g_up)
    w_out_full, w_down_full = g_out.reshape(D_MODEL, D_MODEL), g_down.reshape(D_FF, D_MODEL)
    b_gate_full, conv_w_full, ffn_w_full = (_gathered_columns(g) for g in (g_bgate, g_convw, g_ffnw))

    x1, ya0, yb0, mrg, ya, yb, lsetot = _mix_fwd(
        xs, abcv, gates, [a[0] for a in attn], [a[1] for a in attn], conv_w_full, conv_a_b, b_gate_full,
        w_pa_full, w_pb_full, w_out_full)
    h2, up0, up = _ffn_up_fwd(x1, norm_ffn_g, w_up_full, ffn_w_full, ffn_conv_b)
    act, dx2, dx2b, d_g_final, loss = _ffn_act_fwd(x1, up, target, w_down_full, g_final)

    d_up = _ffn_act_bwd(dx2b, up, w_down_full)
    slab_down = _matmul_tn("dw_down", act, dx2b, col_tile=512).reshape(N_CHIPS, D_FF // N_CHIPS, D_MODEL)
    dx1, dx1b, d_g_ffn, d_up0, ffn_small, swapped_down = _ffn_up_bwd(
        d_up, up0, w_up_full, ffn_w_full, x1, norm_ffn_g, dx2, swap=[slab_down])
    (partial_down,) = _add_sibling("grad_add_w_down", [slab_down], [swapped_down], core_arr)
    slab_up = _matmul_tn("dw_up", h2, d_up0, col_tile=2 * D_FF // N_CHIPS, slabs=N_CHIPS)
    d_w_out, swapped_up = _matmul_tn("dw_out", mrg, dx1b, swap=[slab_up])
    (partial_up,) = _add_sibling("grad_add_w_up", [slab_up], [swapped_up], core_arr)

    mix_res = _mix_bwd(dx1, abcv, gates, ya, yb, yb0, lsetot, conv_w_full, conv_a_b, b_gate_full,
                       w_pa_full, w_pb_full, w_out_full, exchange=[partial_up, partial_down])
    (d_ya, d_yb, d_gates, d_abcv, d_yb0, dyl0, dyl1, dyl2, aux0, aux1, aux2, gate_small, conv_small) = mix_res[:13]
    halves_ffn = _sum_chips("grad_sum_ffn", [partial_up, partial_down], mix_res[13:], chip_core)

    slabs_mix = [_matmul_tn("dw_proj_a", ya0, d_ya, slabs=N_CHIPS), _matmul_tn("dw_proj_b", yb0, d_yb, slabs=N_CHIPS),
                 d_w_out.reshape(N_CHIPS, D_MODEL // N_CHIPS, D_MODEL)]
    res0 = _attn_bwd(qkv0, dyl0, aux0, 0, swap=slabs_mix)
    d_qkv0, partials_mix = res0[0], _add_sibling("grad_add_mix", slabs_mix, res0[1:], core_arr)
    res1 = _attn_bwd(qkv1, dyl1, aux1, 1, exchange=partials_mix)
    d_qkv1, halves_mix = res1[0], _sum_chips("grad_sum_mix", partials_mix, res1[1:], chip_core)
    (d_qkv2,) = _attn_bwd(qkv2, dyl2, aux2, 2)

    dq = [d_qkv0, d_qkv1, d_qkv2]
    seq = xs.shape[0]
    d_w_abcv = _matmul_tn("dw_in_abcv", d_abcv, h1)
    d_w_gates = _matmul_tn("dw_in_gates", d_gates, h1)
    d_w_groups = [_matmul_tn(f"dw_in_qkv{g}", t.reshape(seq, 3 * GROUP_WIDTH), h.reshape(seq, D_MODEL))
                  for g, (t, h) in enumerate(zip(dq, (h1, h1_streams4, h1_streams16)))]
    gw = GROUP_WIDTH
    d_w_in_t = jnp.concatenate(
        [d_w_abcv] + [d_w_groups[g][j * gw:(j + 1) * gw] for j in range(3) for g in range(3)] + [d_w_gates], axis=0)

    slab_in = d_w_in_t.reshape(N_CHIPS, D_IN // N_CHIPS, D_MODEL)
    (from_sibling_in,) = _sibling_swap_halves("grad_swap_w_in", [slab_in])
    (partial_in,) = _add_sibling("grad_add_w_in", [slab_in], [from_sibling_in], core_arr)
    send_sems, recv_sems, partial_thru, landing_thru, token = _chip_exchange_start(partial_in)
    g_mix_after_start = norm_mix_g + token[0:1, 0:1]
    grad_x, d_g_mix = _inproj_bwd(d_abcv, d_gates, dq, w_in_full_t, xs, g_mix_after_start, dx1)
    partial_in, received_in = _chip_exchange_wait(send_sems, recv_sems, partial_thru, landing_thru, d_g_mix)
    halves_in = _sum_chips("grad_sum_w_in", [partial_in], [received_in], chip_core)

    big_names = ("w_in", "w_proj_a", "w_proj_b", "w_out", "w_up", "w_down")
    big_w = dict(w_in=w_in_t, w_proj_a=w_proj_a, w_proj_b=w_proj_b, w_out=w_out, w_up=w_up, w_down=w_down)
    big_m = dict(w_in=m_w_in_t, w_proj_a=m_w_proj_a, w_proj_b=m_w_proj_b, w_out=m_w_out, w_up=m_w_up, w_down=m_w_down)
    big_v = dict(w_in=v_w_in_t, w_proj_a=v_w_proj_a, w_proj_b=v_w_proj_b, w_out=v_w_out, w_up=v_w_up, w_down=v_w_down)

    fin_w, fin_m, fin_v = (a.reshape(1, D_MODEL) for a in (final_norm_g, m_final_norm_g, v_final_norm_g))
    small_w = [norm_mix_g, b_gate, conv_a_w, conv_a_b, norm_ffn_g, ffn_conv_w, ffn_conv_b, fin_w]
    small_m = [m_norm_mix_g, m_b_gate, m_conv_a_w, m_conv_a_b, m_norm_ffn_g, m_ffn_conv_w, m_ffn_conv_b, fin_m]
    small_v = [v_norm_mix_g, v_b_gate, v_conv_a_w, v_conv_a_b, v_norm_ffn_g, v_ffn_conv_w, v_ffn_conv_b, fin_v]
    small_out = _small_update([d_g_mix, d_g_ffn, d_g_final, gate_small, conv_small, ffn_small, loss],
                              small_w, small_m, small_v, halves_in + halves_mix + halves_ffn)
    big_grads = small_out[1 + 4 * len(SMALL_PARAMS):]
    total_loss = small_out[0][0, 0]

    grads, delta, new_m, new_v = {}, {}, {}, {}
    for i, n in enumerate(SMALL_PARAMS):
        vals = small_out[1 + 4 * i:5 + 4 * i]
        if n == "final_norm_g":
            vals = [a.reshape(D_MODEL) for a in vals]
        grads[n], delta[n], new_m[n], new_v[n] = vals
    updates = _adamw([big_w[n] for n in big_names], big_grads, [big_m[n] for n in big_names],
                     [big_v[n] for n in big_names])
    for n, vals in zip(big_names, updates):
        if n == "w_in":
            vals = [jnp.swapaxes(a, 1, 2) for a in vals]
        grads[n], delta[n], new_m[n], new_v[n] = vals

    names = ["norm_mix_g", "w_in", "b_gate", "conv_a_w", "conv_a_b", "w_proj_a", "w_proj_b", "w_out", "norm_ffn_g", "w_up",
             "ffn_conv_w", "ffn_conv_b", "w_down", "final_norm_g"]
    out = [total_loss, grad_x[None]]
    for group in (grads, delta, new_m, new_v):
        out += [group[n] for n in names]
    return tuple(out)
```

```python
import jax
import jax.numpy as jnp
from jax import lax
from jax.experimental import pallas as pl
from jax.experimental.pallas import tpu as pltpu

F32 = jnp.float32
BF16 = jnp.bfloat16

D_MODEL = 1024
CONV_WIDTH = 512
ATTN_WIDTH = 768
GROUP_WIDTH = 256
HEAD_DIM = 64
HEADS_PER_GROUP = 4
DILATIONS = (1, 4, 16)
ATTN_BLOCK = 128
D_FF = 2816
D_IN = 5888
EPS = 1e-6
NEG_INF = -1e30
ATTN_SCALE = HEAD_DIM ** -0.5

COL_ABCV = 0
COL_Q = 1536
COL_K = 2304
COL_V = 3072
COL_GATES = 3840

ADAM_LR = 0.001
ADAM_B1 = 0.9
ADAM_B2 = 0.999
ADAM_EPS = 1e-08
ADAM_WD = 0.01
ADAM_STEP = 10

LANES = 128
SUBLANES = 8
BF16_ROWS = 16
ROW_TILE = 512
VMEM_LIMIT = 56 * 1024 * 1024

_NT = (((1,), (1,)), ((), ()))
_TN = (((0,), (0,)), ((), ()))


def _params(n_axes, vmem=VMEM_LIMIT):
    return pltpu.CompilerParams(dimension_semantics=("arbitrary",) * n_axes, vmem_limit_bytes=vmem)


def _resident(shape):
    nd = len(shape)
    return pl.BlockSpec(shape, lambda *_: (0,) * nd, pipeline_mode=pl.Buffered(1))


def _rows(tm, width, col_block=0):
    return pl.BlockSpec((tm, width), lambda i: (i, col_block))


def _col_chunks(n, cmax):
    out, lo = [], 0
    while lo < n:
        size = min(cmax, n - lo)
        out.append((lo, size))
        lo += size
    return out


def _dot(a, b):
    return jnp.dot(a, b, preferred_element_type=F32)


def _dot_nt(a, b):
    return lax.dot_general(a, b, _NT, preferred_element_type=F32)


def _dot_tn(a, b):
    return lax.dot_general(a, b, _TN, preferred_element_type=F32)


def _sigmoid(x):
    return 0.5 * jnp.tanh(0.5 * x) + 0.5


def _silu(x):
    hx = 0.5 * x
    return hx + hx * jnp.tanh(hx)


def _shift_down(v, k, halo8):
    tm = v.shape[0]
    rolled = pltpu.roll(v, k, 0)
    fix = jnp.tile(pltpu.roll(halo8, k, 0), (tm // SUBLANES, 1))
    row = lax.broadcasted_iota(jnp.int32, v.shape, 0)
    return jnp.where(row < k, fix, rolled)


def _shift_up(v, k, halo8):
    tm = v.shape[0]
    rolled = pltpu.roll(v, tm - k, 0)
    fix = jnp.tile(pltpu.roll(halo8, SUBLANES - k, 0), (tm // SUBLANES, 1))
    row = lax.broadcasted_iota(jnp.int32, v.shape, 0)
    return jnp.where(row >= tm - k, fix, rolled)


def _colsum(v):
    return jnp.sum(v, axis=0, keepdims=True)


def _to_streams(val, scr, out_ref, d, col0):
    tm = val.shape[0]
    panels = val.shape[1] // LANES
    if d == 1:
        out_ref[0, :, col0:col0 + val.shape[1]] = val.astype(out_ref.dtype)
        return
    for p in range(panels):
        scr[pl.ds(p * tm, tm), :] = val[:, p * LANES:(p + 1) * LANES]
    for r in range(d):
        for p in range(panels):
            piece = scr[pl.ds(p * tm + r, tm // d, stride=d), :]
            out_ref[r, :, col0 + p * LANES: col0 + (p + 1) * LANES] = piece.astype(out_ref.dtype)


def _from_streams(in_ref, scr, d, col0, width):
    panels = width // LANES
    rows = in_ref.shape[1]
    tm = rows * d
    if d == 1:
        return in_ref[0, :, col0:col0 + width].astype(F32)
    for r in range(d):
        for p in range(panels):
            scr[pl.ds(p * tm + r, rows, stride=d), :] = in_ref[r, :, col0 + p * LANES: col0 + (p + 1) * LANES].astype(F32)
    return jnp.concatenate([scr[pl.ds(p * tm, tm), :] for p in range(panels)], axis=1)


def _stream_block(tm, d, width):
    return pl.BlockSpec((d, tm // d, width), lambda i: (0, i, 0))


def _rev_stream_block(tm, d, width, nt):
    return pl.BlockSpec((d, tm // d, width), lambda i: (0, nt - 1 - i, 0))


N_CHIPS = 4
MESH_ID = pl.DeviceIdType.MESH
_ANY = pl.BlockSpec(memory_space=pl.ANY)
_VMEM = pl.BlockSpec(memory_space=pltpu.VMEM)


def _mesh_position():
    x, y, c = lax.axis_index("x"), lax.axis_index("y"), lax.axis_index("c")
    other_chips = [(1 - x, y), (x, 1 - y), (1 - x, 1 - y)]
    return x, y, c, other_chips


def _half_rows(c, half):
    return pl.ds(pl.multiple_of(c * half, BF16_ROWS), half)


def _remote_copy(k, src, dst, to, send_sems, recv_sems):
    return pltpu.make_async_remote_copy(src_ref=src, dst_ref=dst, send_sem=send_sems.at[k], recv_sem=recv_sems.at[k],
                                        device_id=to, device_id_type=MESH_ID)


def _gather_first_copies(big_refs, small_refs, big_outs, small_outs, send_sems, recv_sems):
    x, y, c, chips = _mesh_position()
    me = 2 * x + y
    nb = len(big_refs)
    cps = []
    for j, (px, py) in enumerate(chips):
        for b in range(nb):
            mine = _half_rows(c, big_refs[b].shape[1] // 2)
            cps.append(_remote_copy(3 * b + j, big_refs[b].at[0, mine], big_outs[b].at[me, mine], (px, py, c),
                                    send_sems, recv_sems))
        for s in range(len(small_refs)):
            cps.append(_remote_copy(3 * (nb + s) + j, small_refs[s].at[0], small_outs[s].at[me], (px, py, c),
                                    send_sems, recv_sems))
    return cps


def _gather_forward_copies(bufs, send_sems, recv_sems):
    x, y, c, chips = _mesh_position()
    cps = []
    for j, (px, py) in enumerate(chips):
        for b in range(len(bufs)):
            landed = bufs[b].at[2 * px + py, _half_rows(c, bufs[b].shape[1] // 2)]
            cps.append(_remote_copy(3 * b + j, landed, landed, (x, y, 1 - c), send_sems, recv_sems))
    return cps


def _chip_exchange_copies(src_refs, out_refs, send_sems, recv_sems):
    x, y, c, chips = _mesh_position()
    cps = []
    for j, (px, py) in enumerate(chips):
        for a in range(len(src_refs)):
            cps.append(_remote_copy(3 * a + j, src_refs[a].at[2 * px + py], out_refs[a].at[j], (px, py, c),
                                    send_sems, recv_sems))
    return cps


def _sibling_swap_copies(src_refs, out_refs, send_sems, recv_sems):
    x, y, c, _ = _mesh_position()
    cps = []
    for a in range(len(src_refs)):
        theirs = _half_rows(1 - c, src_refs[a].shape[1] // 2)
        cps.append(_remote_copy(a, src_refs[a].at[:, theirs, :], out_refs[a], (x, y, 1 - c), send_sems, recv_sems))
    return cps


def _swap_shapes(slabs):
    return [jax.ShapeDtypeStruct((a.shape[0], a.shape[1] // 2, a.shape[2]), a.dtype) for a in slabs]


def _dma_sems(n):
    return [pltpu.SemaphoreType.DMA((n,)), pltpu.SemaphoreType.DMA((n,))]


def _norm_fwd(x, g, shard):
    s = x.shape[0]
    tm = ROW_TILE
    nt = s // tm

    def body(x_ref, g_ref, shard_ref, h_ref, hs1_ref, hs2_ref, buf_ref, scr, send1, recv1, send2, recv2):
        i = pl.program_id(0)

        def level_one():
            return _gather_first_copies([shard_ref], [], [buf_ref], [], send1, recv1)

        @pl.when(i == 0)
        def _():
            for cp in level_one():
                cp.start()

        xv = x_ref[...]
        r = lax.rsqrt(jnp.mean(xv * xv, axis=-1, keepdims=True) + EPS)
        hf = xv * r * g_ref[...]
        h_ref[...] = hf.astype(BF16)
        for d, hs_ref in zip(DILATIONS[1:], (hs1_ref, hs2_ref)):
            for lo, size in _col_chunks(D_MODEL, GROUP_WIDTH):
                _to_streams(hf[:, lo:lo + size], scr, hs_ref, d, lo)

        @pl.when(i == nt - 1)
        def _():
            forwards = _gather_forward_copies([buf_ref], send2, recv2)
            for cp, fwd in zip(level_one(), forwards):
                cp.wait()
                fwd.start()
            for fwd in forwards:
                fwd.wait()

    return pl.pallas_call(
        body, name="norm_fwd", grid=(nt,),
        out_shape=[jax.ShapeDtypeStruct((s, D_MODEL), BF16)]
        + [jax.ShapeDtypeStruct((d, s // d, D_MODEL), BF16) for d in DILATIONS[1:]]
        + [jax.ShapeDtypeStruct((N_CHIPS,) + shard.shape[1:], shard.dtype)],
        in_specs=[_rows(tm, D_MODEL), _resident((1, D_MODEL)), _ANY],
        out_specs=[_rows(tm, D_MODEL)] + [_stream_block(tm, d, D_MODEL) for d in DILATIONS[1:]] + [_ANY],
        scratch_shapes=[pltpu.VMEM((GROUP_WIDTH // LANES * tm, LANES), F32)] + _dma_sems(3) + _dma_sems(3),
        compiler_params=_params(1))(x, g, shard)


def _inproj_fwd(h1, w_in_t, big_shards, small_shards):
    s = h1.shape[0]
    tm = ROW_TILE
    nt = s // tm
    nb, ns = len(big_shards), len(small_shards)
    n_fixed_in, n_fixed_out = 2, 5

    def body(*refs):
        h_ref, w_ref = refs[:n_fixed_in]
        shard_refs = refs[n_fixed_in:n_fixed_in + nb + ns]
        pos = n_fixed_in + nb + ns
        abcv_ref, gates_ref, qkv0_ref, qkv1_ref, qkv2_ref = refs[pos:pos + n_fixed_out]
        gathered_refs = refs[pos + n_fixed_out:pos + n_fixed_out + nb + ns]
        scr, send_sems, recv_sems = refs[pos + n_fixed_out + nb + ns:]
        i = pl.program_id(0)

        def gather_copies():
            return _gather_first_copies(shard_refs[:nb], shard_refs[nb:], gathered_refs[:nb], gathered_refs[nb:],
                                        send_sems, recv_sems)

        @pl.when(i == 0)
        def _():
            for cp in gather_copies():
                cp.start()

        h = h_ref[...]
        for lo, size in _col_chunks(3 * CONV_WIDTH, 512):
            abcv_ref[:, lo:lo + size] = _dot_nt(h, w_ref[COL_ABCV + lo: COL_ABCV + lo + size, :]).astype(BF16)
        for lo, size in _col_chunks(2 * D_MODEL, 512):
            gates_ref[:, lo:lo + size] = _dot_nt(h, w_ref[COL_GATES + lo: COL_GATES + lo + size, :]).astype(BF16)
        for gi, (d, out_ref) in enumerate(zip(DILATIONS, (qkv0_ref, qkv1_ref, qkv2_ref))):
            for j, base in enumerate((COL_Q, COL_K, COL_V)):
                lo = base + gi * GROUP_WIDTH
                y = _dot_nt(h, w_ref[lo:lo + GROUP_WIDTH, :])
                if j == 0:
                    y = y * ATTN_SCALE
                _to_streams(y, scr, out_ref, d, j * GROUP_WIDTH)

        @pl.when(i == nt - 1)
        def _():
            for cp in gather_copies():
                cp.wait()

    outs = [jax.ShapeDtypeStruct((s, 3 * CONV_WIDTH), BF16), jax.ShapeDtypeStruct((s, 2 * D_MODEL), BF16)]
    outs += [jax.ShapeDtypeStruct((d, s // d, 3 * GROUP_WIDTH), BF16) for d in DILATIONS]
    outs += [jax.ShapeDtypeStruct((N_CHIPS,) + a.shape[1:], a.dtype) for a in list(big_shards) + list(small_shards)]
    return pl.pallas_call(
        body, name="inproj_fwd", grid=(nt,), out_shape=outs,
        in_specs=[_rows(tm, D_MODEL), _resident((D_IN, D_MODEL))] + [_ANY] * (nb + ns),
        out_specs=[_rows(tm, 3 * CONV_WIDTH), _rows(tm, 2 * D_MODEL)]
        + [_stream_block(tm, d, 3 * GROUP_WIDTH) for d in DILATIONS] + [_ANY] * (nb + ns),
        scratch_shapes=[pltpu.VMEM((GROUP_WIDTH // LANES * tm, LANES), F32)] + _dma_sems(3 * (nb + ns)),
        compiler_params=_params(1))(h1, w_in_t, *big_shards, *small_shards)


def _head_of_lane(shape):
    return lax.broadcasted_iota(jnp.int32, shape, 1) // HEAD_DIM


def _stack_heads(v):
    head = _head_of_lane(v.shape)
    return jnp.concatenate([jnp.where(head == h, v, jnp.zeros_like(v)) for h in range(HEADS_PER_GROUP)], axis=0)


def _unstack_heads(v):
    q = ATTN_BLOCK
    head = _head_of_lane((q, v.shape[1]))
    out = jnp.zeros((q, v.shape[1]), v.dtype)
    for h in range(HEADS_PER_GROUP):
        out = jnp.where(head == h, v[h * q:(h + 1) * q], out)
    return out


def _per_head_rows(col):
    q = ATTN_BLOCK
    head = _head_of_lane((q, GROUP_WIDTH))
    out = jnp.zeros((q, GROUP_WIDTH), col.dtype)
    for h in range(HEADS_PER_GROUP):
        out = jnp.where(head == h, col[h * q:(h + 1) * q], out)
    return out


def _compact_heads(v):
    lane = lax.broadcasted_iota(jnp.int32, (v.shape[0], LANES), 1)
    return jnp.where((lane & 32) == 0, v[:, 0:LANES], v[:, LANES:2 * LANES])


def _compact_head_col(v):
    lane = lax.broadcasted_iota(jnp.int32, v.shape, 1)
    head = ((lane >> 6) & 1) + 2 * ((lane >> 5) & 1)
    cols = [jnp.max(jnp.where(head == h, v, -jnp.inf), axis=1, keepdims=True) for h in range(HEADS_PER_GROUP)]
    return jnp.concatenate(cols, axis=0)


ATTN_BLOCKS_PER_STEP = 8


def _band_bias(first_block):
    rows = HEADS_PER_GROUP * ATTN_BLOCK
    qi = lax.broadcasted_iota(jnp.int32, (rows, 2 * ATTN_BLOCK), 0) % ATTN_BLOCK
    kj = lax.broadcasted_iota(jnp.int32, (rows, 2 * ATTN_BLOCK), 1)
    dist = qi + ATTN_BLOCK - kj
    valid = (dist >= 0) & (dist <= ATTN_BLOCK)
    if first_block:
        valid = valid & (kj >= ATTN_BLOCK)
    return jnp.where(valid, 0.0, NEG_INF).astype(F32)


def _store_band_biases(bias_ref):
    bias_ref[0] = _band_bias(False)
    bias_ref[1] = _band_bias(True)


def _attn_block_specs(g, nb, clamp_last=False):
    q = ATTN_BLOCK
    last = nb // g - 1

    def cur(col, width=GROUP_WIDTH):
        if clamp_last:
            return pl.BlockSpec((None, g * q, width), lambda r, n: (r, jnp.minimum(n, last), col))
        return pl.BlockSpec((None, g * q, width), lambda r, n: (r, n, col))

    def prev(col):
        if clamp_last:
            return pl.BlockSpec((None, q, GROUP_WIDTH), lambda r, n: (r, jnp.clip(n * g - 1, 0, nb - 1), col))
        return pl.BlockSpec((None, q, GROUP_WIDTH), lambda r, n: (r, jnp.maximum(n * g - 1, 0), col))

    return cur, prev


def _attn_fwd(qkv, gi, forward=()):
    d, length, _ = qkv.shape
    nb = length // ATTN_BLOCK
    q = ATTN_BLOCK
    g = min(ATTN_BLOCKS_PER_STEP, nb)
    ns = nb // g
    nf = len(forward)

    def body(*refs):
        q_ref, kp_ref, kc_ref, vp_ref, vc_ref = refs[:5]
        o_ref, lse_ref = refs[5 + nf:7 + nf]
        buf_refs = refs[7 + nf:7 + 2 * nf]
        bias_ref = refs[7 + 2 * nf]
        sems = refs[8 + 2 * nf:]
        n = pl.program_id(1)
        first_step = (pl.program_id(0) == 0) & (n == 0)
        last_step = (pl.program_id(0) == d - 1) & (n == ns - 1)

        @pl.when(first_step)
        def _():
            _store_band_biases(bias_ref)
            for cp in _gather_forward_copies(buf_refs, *sems) if nf else ():
                cp.start()

        kfull = jnp.concatenate([kp_ref[...], kc_ref[...]], axis=0)
        vfull = jnp.concatenate([vp_ref[...], vc_ref[...]], axis=0)
        for j in range(g):
            qs = _stack_heads(q_ref[j * q:(j + 1) * q, :])
            k2 = kfull[j * q:(j + 2) * q]
            v2 = vfull[j * q:(j + 2) * q]
            bias = jnp.where(n == 0, bias_ref[1], bias_ref[0]) if j == 0 else bias_ref[0]
            sc = _dot_nt(qs, k2) + bias
            m = jnp.max(sc, axis=1, keepdims=True)
            p = jnp.exp(sc - m)
            l = jnp.sum(p, axis=1, keepdims=True)
            of = _dot(p.astype(BF16), v2) / l
            o_ref[j * q:(j + 1) * q, :] = _unstack_heads(of).astype(BF16)
            lse_ref[j * q:(j + 1) * q, :] = _per_head_rows(m + jnp.log(l))

        if nf:
            @pl.when(last_step)
            def _():
                for cp in _gather_forward_copies(buf_refs, *sems):
                    cp.wait()

    cur, prev = _attn_block_specs(g, nb)
    return pl.pallas_call(
        body, name=f"attn_fwd_g{gi}", grid=(d, ns),
        out_shape=[jax.ShapeDtypeStruct((d, length, GROUP_WIDTH), BF16),
                   jax.ShapeDtypeStruct((d, length, GROUP_WIDTH), F32)]
        + [jax.ShapeDtypeStruct(a.shape, a.dtype) for a in forward],
        in_specs=[cur(0), prev(1), cur(1), prev(2), cur(2)] + [_ANY] * nf,
        out_specs=[cur(0), cur(0)] + [_ANY] * nf,
        input_output_aliases={5 + a: 2 + a for a in range(nf)},
        scratch_shapes=[pltpu.VMEM((2, HEADS_PER_GROUP * q, 2 * q), F32)] + (_dma_sems(3 * nf) if nf else []),
        compiler_params=_params(2))(qkv, qkv, qkv, qkv, qkv, *forward)


def _conv_branch(ab, ac, av, halo_u, w, b):
    u = ac * av
    sh1 = _shift_down(u, 1, halo_u)
    sh2 = _shift_down(u, 2, halo_u)
    cv = w[0:1] * sh2 + w[1:2] * sh1 + w[2:3] * u + b
    return ab * cv, cv, u, sh1, sh2


def _mix_fwd(x, abcv, gates, o_list, lse_list, conv_w, conv_b, b_gate, w_pa, w_pb, w_out):
    s = x.shape[0]
    tm = ROW_TILE

    def body(x_ref, abcv_ref, gates_ref, o0_ref, o1_ref, o2_ref, l0_ref, l1_ref, l2_ref,
             cw_ref, cb_ref, bg_ref, wpa_ref, wpb_ref, wout_ref,
             x1_ref, ya0_ref, yb0_ref, mrg_ref, ya_ref, yb_ref, lsetot_ref, halo_ref, scr):
        i = pl.program_id(0)

        @pl.when(i == 0)
        def _():
            halo_ref[...] = jnp.zeros_like(halo_ref)

        ab = abcv_ref[:, 0:CONV_WIDTH].astype(F32)
        ac = abcv_ref[:, CONV_WIDTH:2 * CONV_WIDTH].astype(F32)
        av = abcv_ref[:, 2 * CONV_WIDTH:3 * CONV_WIDTH].astype(F32)
        ya0, _, u, _, _ = _conv_branch(ab, ac, av, halo_ref[...], cw_ref[...], cb_ref[...])
        halo_ref[...] = u[tm - SUBLANES:tm]
        ya0 = ya0.astype(BF16)
        ya0_ref[...] = ya0
        ya = _dot(ya0, wpa_ref[...])

        o_refs, l_refs = (o0_ref, o1_ref, o2_ref), (l0_ref, l1_ref, l2_ref)
        lses = [_from_streams(l_refs[g], scr, DILATIONS[g], 0, GROUP_WIDTH) for g in range(3)]
        top = jnp.maximum(jnp.maximum(lses[0], lses[1]), lses[2])
        weights = [jnp.exp(lse - top) for lse in lses]
        total = weights[0] + weights[1] + weights[2]
        lsetot_ref[...] = top + jnp.log(total)
        inv_total = 1.0 / total
        yb = jnp.zeros((tm, D_MODEL), F32)
        for g in range(3):
            og = _from_streams(o_refs[g], scr, DILATIONS[g], 0, GROUP_WIDTH)
            yb0 = (weights[g] * inv_total * og).astype(BF16)
            yb0_ref[:, g * GROUP_WIDTH:(g + 1) * GROUP_WIDTH] = yb0
            yb = yb + _dot(yb0, wpb_ref[g * GROUP_WIDTH:(g + 1) * GROUP_WIDTH, :])

        sa = _sigmoid(gates_ref[:, 0:D_MODEL].astype(F32) + bg_ref[0:1, :])
        sb = _sigmoid(gates_ref[:, D_MODEL:2 * D_MODEL].astype(F32) + bg_ref[1:2, :])
        ya_ref[...] = ya.astype(BF16)
        yb_ref[...] = yb.astype(BF16)
        mrg = (sa * ya + sb * yb).astype(BF16)
        mrg_ref[...] = mrg
        x1_ref[...] = x_ref[...] + _dot(mrg, wout_ref[...])

    outs = [jax.ShapeDtypeStruct((s, D_MODEL), F32),
            jax.ShapeDtypeStruct((s, CONV_WIDTH), BF16),
            jax.ShapeDtypeStruct((s, ATTN_WIDTH), BF16),
            jax.ShapeDtypeStruct((s, D_MODEL), BF16),
            jax.ShapeDtypeStruct((s, D_MODEL), BF16),
            jax.ShapeDtypeStruct((s, D_MODEL), BF16),
            jax.ShapeDtypeStruct((s, GROUP_WIDTH), F32)]
    return pl.pallas_call(
        body, name="mix_fwd", grid=(s // tm,), out_shape=outs,
        in_specs=[_rows(tm, D_MODEL), _rows(tm, 3 * CONV_WIDTH), _rows(tm, 2 * D_MODEL)]
        + [_stream_block(tm, d, GROUP_WIDTH) for d in DILATIONS] * 2
        + [_resident((3, CONV_WIDTH)), _resident((1, CONV_WIDTH)), _resident((2, D_MODEL)),
           _resident((CONV_WIDTH, D_MODEL)), _resident((ATTN_WIDTH, D_MODEL)), _resident((D_MODEL, D_MODEL))],
        out_specs=[_rows(tm, D_MODEL), _rows(tm, CONV_WIDTH), _rows(tm, ATTN_WIDTH), _rows(tm, D_MODEL),
                   _rows(tm, D_MODEL), _rows(tm, D_MODEL), _rows(tm, GROUP_WIDTH)],
        scratch_shapes=[pltpu.VMEM((SUBLANES, CONV_WIDTH), F32),
                        pltpu.VMEM((GROUP_WIDTH // LANES * tm, LANES), F32)],
        compiler_params=_params(1))(x, abcv, gates, *o_list, *lse_list, conv_w, conv_b, b_gate, w_pa, w_pb, w_out)


FFN_CHUNK = 512
FFN_UP_ROW_TILE = 256


def _ffn_up_fwd(x1, g, w_up, conv_w, conv_b):
    s = x1.shape[0]
    n = w_up.shape[1]
    tm = FFN_UP_ROW_TILE

    def body(x_ref, g_ref, w_ref, cw_ref, cb_ref, h_ref, up0_ref, up_ref, halo_ref):
        @pl.when(pl.program_id(0) == 0)
        def _():
            halo_ref[...] = jnp.zeros_like(halo_ref)

        xv = x_ref[...]
        r = lax.rsqrt(jnp.mean(xv * xv, axis=-1, keepdims=True) + EPS)
        h = (xv * r * g_ref[...]).astype(BF16)
        h_ref[...] = h
        for lo, size in _col_chunks(n, FFN_CHUNK):
            cols = slice(lo, lo + size)
            y = _dot(h, w_ref[:, cols])
            up0_ref[:, cols] = y.astype(BF16)
            halo = halo_ref[:, cols]
            w = cw_ref[:, cols]
            up = w[0:1] * _shift_down(y, 2, halo) + w[1:2] * _shift_down(y, 1, halo) + w[2:3] * y + cb_ref[:, cols]
            up_ref[:, cols] = up.astype(BF16)
            halo_ref[:, cols] = y[tm - SUBLANES:tm]

    return pl.pallas_call(
        body, name="ffn_up_fwd", grid=(s // tm,),
        out_shape=[jax.ShapeDtypeStruct((s, D_MODEL), BF16), jax.ShapeDtypeStruct((s, n), BF16),
                   jax.ShapeDtypeStruct((s, n), BF16)],
        in_specs=[_rows(tm, D_MODEL), _resident((1, D_MODEL)), _resident((D_MODEL, n)), _resident((3, n)),
                  _resident((1, n))],
        out_specs=[_rows(tm, D_MODEL), _rows(tm, n), _rows(tm, n)],
        scratch_shapes=[pltpu.VMEM((SUBLANES, n), F32)],
        compiler_params=_params(1))(x1, g, w_up, conv_w, conv_b)


def _ffn_act_fwd(x1, up, target, w_down, g_final):
    s = x1.shape[0]
    tm = ROW_TILE

    def body(x1_ref, up_ref, tgt_ref, wd_ref, gf_ref, act_ref, dx2_ref, dx2b_ref, dgf_ref, loss_ref):
        @pl.when(pl.program_id(0) == 0)
        def _():
            dgf_ref[...] = jnp.zeros_like(dgf_ref)
            loss_ref[...] = jnp.zeros_like(loss_ref)

        acc = jnp.zeros((tm, D_MODEL), F32)
        for lo, size in _col_chunks(D_FF, FFN_CHUNK):
            gate = up_ref[:, lo:lo + size].astype(F32)
            val = up_ref[:, D_FF + lo:D_FF + lo + size].astype(F32)
            act = (_silu(gate) * val).astype(BF16)
            act_ref[:, lo:lo + size] = act
            acc = acc + _dot(act, wd_ref[lo:lo + size, :])

        x2 = x1_ref[...] + acc
        r = lax.rsqrt(jnp.mean(x2 * x2, axis=-1, keepdims=True) + EPS)
        xn = x2 * r
        gf = gf_ref[...]
        err = xn * gf - tgt_ref[...]
        loss_ref[...] += (0.5 / D_MODEL) * jnp.sum(err * err)
        dy = err * (1.0 / D_MODEL)
        dgf_ref[...] += _colsum(dy * xn)
        dxn = dy * gf
        dx2 = r * (dxn - xn * jnp.mean(dxn * xn, axis=-1, keepdims=True))
        dx2_ref[...] = dx2
        dx2b_ref[...] = dx2.astype(BF16)

    return pl.pallas_call(
        body, name="ffn_act_fwd", grid=(s // tm,),
        out_shape=[jax.ShapeDtypeStruct((s, D_FF), BF16), jax.ShapeDtypeStruct((s, D_MODEL), F32),
                   jax.ShapeDtypeStruct((s, D_MODEL), BF16),
                   jax.ShapeDtypeStruct((1, D_MODEL), F32), jax.ShapeDtypeStruct((1, LANES), F32)],
        in_specs=[_rows(tm, D_MODEL), _rows(tm, 2 * D_FF), _rows(tm, D_MODEL),
                  _resident((D_FF, D_MODEL)), _resident((1, D_MODEL))],
        out_specs=[_rows(tm, D_FF), _rows(tm, D_MODEL), _rows(tm, D_MODEL),
                   pl.BlockSpec((1, D_MODEL), lambda i: (0, 0)), pl.BlockSpec((1, LANES), lambda i: (0, 0))],
        compiler_params=_params(1))(x1, up, target, w_down, g_final)


def _ffn_act_bwd(dx2b, up, w_down):
    s = dx2b.shape[0]
    tm = ROW_TILE

    def body(dx2_ref, up_ref, wd_ref, dup_ref):
        dx2 = dx2_ref[...]
        for lo, size in _col_chunks(D_FF, FFN_CHUNK):
            gate = up_ref[:, lo:lo + size].astype(F32)
            val = up_ref[:, D_FF + lo:D_FF + lo + size].astype(F32)
            dact = _dot_nt(dx2, wd_ref[lo:lo + size, :])
            sg = _sigmoid(gate)
            dup_ref[:, lo:lo + size] = (dact * val * (sg * (1.0 + gate * (1.0 - sg)))).astype(BF16)
            dup_ref[:, D_FF + lo:D_FF + lo + size] = (dact * (gate * sg)).astype(BF16)

    return pl.pallas_call(
        body, name="ffn_act_bwd", grid=(s // tm,),
        out_shape=jax.ShapeDtypeStruct((s, 2 * D_FF), BF16),
        in_specs=[_rows(tm, D_MODEL), _rows(tm, 2 * D_FF), _resident((D_FF, D_MODEL))],
        out_specs=_rows(tm, 2 * D_FF),
        compiler_params=_params(1))(dx2b, up, w_down)


def _ffn_up_bwd(d_up, up0, w_up, conv_w, x1, g, dres, swap=()):
    s = x1.shape[0]
    n = w_up.shape[1]
    tm = FFN_UP_ROW_TILE
    nt = s // tm
    nw = len(swap)

    def body(*refs):
        dup_ref, up0_ref, w_ref, cw_ref, x_ref, g_ref, dres_ref = refs[:7]
        slab_refs = refs[7:7 + nw]
        dx_ref, dxb_ref, dg_ref, dup0_ref, small_ref = refs[7 + nw:12 + nw]
        swapped_refs = refs[12 + nw:12 + 2 * nw]
        next_ref = refs[12 + 2 * nw]
        sems = refs[13 + 2 * nw:]

        @pl.when(pl.program_id(0) == 0)
        def _():
            next_ref[...] = jnp.zeros_like(next_ref)
            small_ref[...] = jnp.zeros_like(small_ref)
            dg_ref[...] = jnp.zeros_like(dg_ref)
            for cp in _sibling_swap_copies(slab_refs, swapped_refs, *sems) if nw else ():
                cp.start()

        dh = jnp.zeros((tm, D_MODEL), F32)
        for lo, size in _col_chunks(n, FFN_CHUNK):
            cols = slice(lo, lo + size)
            dz = dup_ref[:, cols].astype(F32)
            x0 = up0_ref[:, cols].astype(F32)
            nxt = next_ref[:, cols]
            dz1 = _shift_up(dz, 1, nxt)
            dz2 = _shift_up(dz, 2, nxt)
            next_ref[:, cols] = dz[0:SUBLANES]
            small_ref[0:1, cols] += _colsum(dz2 * x0)
            small_ref[1:2, cols] += _colsum(dz1 * x0)
            small_ref[2:3, cols] += _colsum(dz * x0)
            small_ref[3:4, cols] += _colsum(dz)
            w = cw_ref[:, cols]
            dup0 = (w[2:3] * dz + w[1:2] * dz1 + w[0:1] * dz2).astype(BF16)
            dup0_ref[:, cols] = dup0
            dh = dh + _dot_nt(dup0, w_ref[:, cols])
        xv = x_ref[...]
        r = lax.rsqrt(jnp.mean(xv * xv, axis=-1, keepdims=True) + EPS)
        xn = xv * r
        dg_ref[...] += _colsum(dh * xn)
        dxn = dh * g_ref[...]
        dx = dres_ref[...] + r * (dxn - xn * jnp.mean(dxn * xn, axis=-1, keepdims=True))
        dx_ref[...] = dx
        dxb_ref[...] = dx.astype(BF16)

        if nw:
            @pl.when(pl.program_id(0) == nt - 1)
            def _():
                for cp in _sibling_swap_copies(slab_refs, swapped_refs, *sems):
                    cp.wait()

    rows = lambda width: pl.BlockSpec((tm, width), lambda i: (nt - 1 - i, 0))
    return pl.pallas_call(
        body, name="ffn_up_bwd", grid=(nt,),
        out_shape=[jax.ShapeDtypeStruct((s, D_MODEL), F32), jax.ShapeDtypeStruct((s, D_MODEL), BF16),
                   jax.ShapeDtypeStruct((1, D_MODEL), F32), jax.ShapeDtypeStruct((s, n), BF16),
                   jax.ShapeDtypeStruct((SUBLANES, n), F32)] + _swap_shapes(swap),
        in_specs=[rows(n), rows(n), _resident((D_MODEL, n)), _resident((3, n)), rows(D_MODEL),
                  _resident((1, D_MODEL)), rows(D_MODEL)] + [_ANY] * nw,
        out_specs=[rows(D_MODEL), rows(D_MODEL), pl.BlockSpec((1, D_MODEL), lambda i: (0, 0)), rows(n),
                   pl.BlockSpec((SUBLANES, n), lambda i: (0, 0))] + [_ANY] * nw,
        scratch_shapes=[pltpu.VMEM((SUBLANES, n), F32)] + (_dma_sems(nw) if nw else []),
        compiler_params=_params(1))(d_up, up0, w_up, conv_w, x1, g, dres, *swap)


def _inproj_bwd(d_abcv, d_gates, d_qkvs, w_in_t, x, g, dres):
    s = x.shape[0]
    tm = ROW_TILE
    gw = GROUP_WIDTH

    def body(dabcv_ref, dgates_ref, dq0_ref, dq1_ref, dq2_ref, w_ref, x_ref, g_ref, dres_ref, dx_ref, dg_ref, scr):
        @pl.when(pl.program_id(0) == 0)
        def _():
            dg_ref[...] = jnp.zeros_like(dg_ref)

        dh = jnp.zeros((tm, D_MODEL), F32)
        for src, width, wrow in ((dabcv_ref, 3 * CONV_WIDTH, COL_ABCV), (dgates_ref, 2 * D_MODEL, COL_GATES)):
            for lo, size in _col_chunks(width, 512):
                dh = dh + _dot(src[:, lo:lo + size], w_ref[wrow + lo:wrow + lo + size, :])
        for gi, (d, dq_ref) in enumerate(zip(DILATIONS, (dq0_ref, dq1_ref, dq2_ref))):
            for j, base in enumerate((COL_Q, COL_K, COL_V)):
                dy = _from_streams(dq_ref, scr, d, j * gw, gw).astype(BF16)
                wrow = base + gi * gw
                dh = dh + _dot(dy, w_ref[wrow:wrow + gw, :])
        xv = x_ref[...]
        r = lax.rsqrt(jnp.mean(xv * xv, axis=-1, keepdims=True) + EPS)
        xn = xv * r
        dg_ref[...] += _colsum(dh * xn)
        dxn = dh * g_ref[...]
        dx_ref[...] = dres_ref[...] + r * (dxn - xn * jnp.mean(dxn * xn, axis=-1, keepdims=True))

    return pl.pallas_call(
        body, name="inproj_bwd", grid=(s // tm,),
        out_shape=[jax.ShapeDtypeStruct((s, D_MODEL), F32), jax.ShapeDtypeStruct((1, D_MODEL), F32)],
        in_specs=[_rows(tm, 3 * CONV_WIDTH), _rows(tm, 2 * D_MODEL)]
        + [_stream_block(tm, d, 3 * gw) for d in DILATIONS]
        + [_resident((D_IN, D_MODEL)), _rows(tm, D_MODEL), _resident((1, D_MODEL)), _rows(tm, D_MODEL)],
        out_specs=[_rows(tm, D_MODEL), pl.BlockSpec((1, D_MODEL), lambda i: (0, 0))],
        scratch_shapes=[pltpu.VMEM((gw // LANES * tm, LANES), F32)],
        compiler_params=_params(1))(d_abcv, d_gates, *d_qkvs, w_in_t, x, g, dres)


def _mix_bwd(dx1, abcv, gates, ya, yb, yb0, lsetot, conv_w, conv_b, b_gate, w_pa, w_pb, w_out, exchange=()):
    s = dx1.shape[0]
    tm = ROW_TILE
    nt = s // tm
    hb = tm // (2 * SUBLANES)
    nx = len(exchange)

    def body(*refs):
        (dx1_ref, abcv_ref, pre_ref, gates_ref, ya_ref, yb_ref, yb0_ref, lsetot_ref,
         cw_ref, cb_ref, bg_ref, wpa_ref, wpb_ref, wout_ref) = refs[:14]
        part_refs = refs[14:14 + nx]
        (dya_ref, dyb_ref, dgates_ref, dabcv_ref, dyb0_ref, dyl0_ref, dyl1_ref, dyl2_ref, aux0_ref, aux1_ref,
         aux2_ref, sm_gate_ref, sm_conv_ref) = refs[14 + nx:27 + nx]
        recv_refs = refs[27 + nx:27 + 2 * nx]
        next_ref, scr = refs[27 + 2 * nx:29 + 2 * nx]
        sems = refs[29 + 2 * nx:]
        i = pl.program_id(0)

        @pl.when(i == 0)
        def _():
            next_ref[...] = jnp.zeros_like(next_ref)
            sm_gate_ref[...] = jnp.zeros_like(sm_gate_ref)
            sm_conv_ref[...] = jnp.zeros_like(sm_conv_ref)
            for cp in _chip_exchange_copies(part_refs, recv_refs, *sems) if nx else ():
                cp.start()

        not_first = (i < nt - 1).astype(F32)
        dm = _dot_nt(dx1_ref[...].astype(BF16), wout_ref[...])
        sa = _sigmoid(gates_ref[:, 0:D_MODEL].astype(F32) + bg_ref[0:1, :])
        sb = _sigmoid(gates_ref[:, D_MODEL:2 * D_MODEL].astype(F32) + bg_ref[1:2, :])
        dya = (dm * sa).astype(BF16)
        dyb = (dm * sb).astype(BF16)
        dya_ref[...] = dya
        dyb_ref[...] = dyb
        dga = dm * ya_ref[...].astype(F32) * (sa * (1.0 - sa))
        dgb = dm * yb_ref[...].astype(F32) * (sb * (1.0 - sb))
        dgates_ref[:, 0:D_MODEL] = dga.astype(BF16)
        dgates_ref[:, D_MODEL:2 * D_MODEL] = dgb.astype(BF16)
        sm_gate_ref[0:1, :] += _colsum(dga)
        sm_gate_ref[1:2, :] += _colsum(dgb)

        dya0 = _dot_nt(dya, wpa_ref[...])
        ab = abcv_ref[:, 0:CONV_WIDTH].astype(F32)
        ac = abcv_ref[:, CONV_WIDTH:2 * CONV_WIDTH].astype(F32)
        av = abcv_ref[:, 2 * CONV_WIDTH:3 * CONV_WIDTH].astype(F32)
        pre = pre_ref[...].astype(F32) * not_first
        halo_u = (pre[:, CONV_WIDTH:2 * CONV_WIDTH] * pre[:, 2 * CONV_WIDTH:3 * CONV_WIDTH])[SUBLANES:2 * SUBLANES]
        w = cw_ref[...]
        _, cv, u, sh1, sh2 = _conv_branch(ab, ac, av, halo_u, w, cb_ref[...])
        dcv = dya0 * ab
        sm_conv_ref[0:1, :] += _colsum(dcv * sh2)
        sm_conv_ref[1:2, :] += _colsum(dcv * sh1)
        sm_conv_ref[2:3, :] += _colsum(dcv * u)
        sm_conv_ref[3:4, :] += _colsum(dcv)
        nxt = next_ref[...]
        du = w[2:3] * dcv + w[1:2] * _shift_up(dcv, 1, nxt) + w[0:1] * _shift_up(dcv, 2, nxt)
        next_ref[...] = dcv[0:SUBLANES]
        dabcv_ref[:, 0:CONV_WIDTH] = (dya0 * cv).astype(BF16)
        dabcv_ref[:, CONV_WIDTH:2 * CONV_WIDTH] = (du * av).astype(BF16)
        dabcv_ref[:, 2 * CONV_WIDTH:3 * CONV_WIDTH] = (du * ac).astype(BF16)

        head_r = lax.broadcasted_iota(jnp.int32, (GROUP_WIDTH, GROUP_WIDTH), 0) // HEAD_DIM
        head_c = lax.broadcasted_iota(jnp.int32, (GROUP_WIDTH, GROUP_WIDTH), 1) // HEAD_DIM
        same_head = (head_r == head_c).astype(BF16)
        prod = jnp.zeros((tm, GROUP_WIDTH), F32)
        dyb0s = []
        for g in range(3):
            cols = slice(g * GROUP_WIDTH, (g + 1) * GROUP_WIDTH)
            dyb0 = _dot_nt(dyb, wpb_ref[cols, :])
            dyb0_ref[:, cols] = dyb0.astype(BF16)
            dyb0s.append(dyb0)
            prod = prod + dyb0 * yb0_ref[:, cols].astype(F32)
        hi = prod.astype(BF16)
        mid = (prod - hi.astype(F32)).astype(BF16)
        lo = (prod - hi.astype(F32) - mid.astype(F32)).astype(BF16)
        delta = _dot(hi, same_head) + _dot(mid, same_head) + _dot(lo, same_head)
        lse_c = _compact_heads(lsetot_ref[...])
        delta_c = _compact_heads(delta)
        for g, (dy_ref, aux_ref) in enumerate(zip((dyl0_ref, dyl1_ref, dyl2_ref), (aux0_ref, aux1_ref, aux2_ref))):
            d = DILATIONS[g]
            _to_streams(dyb0s[g], scr, dy_ref, d, 0)
            _to_streams(lse_c, scr, aux_ref, d, 0)
            _to_streams(delta_c, scr, aux_ref, d, LANES)

        if nx:
            @pl.when(i == nt - 1)
            def _():
                for cp in _chip_exchange_copies(part_refs, recv_refs, *sems):
                    cp.wait()

    rev = lambda i: (nt - 1 - i, 0)
    pre = lambda i: (jnp.maximum((nt - 1 - i) * hb - 1, 0), 0)
    rows = lambda width: pl.BlockSpec((tm, width), rev)
    outs = [jax.ShapeDtypeStruct((s, D_MODEL), BF16), jax.ShapeDtypeStruct((s, D_MODEL), BF16),
            jax.ShapeDtypeStruct((s, 2 * D_MODEL), BF16), jax.ShapeDtypeStruct((s, 3 * CONV_WIDTH), BF16),
            jax.ShapeDtypeStruct((s, ATTN_WIDTH), BF16)]
    outs += [jax.ShapeDtypeStruct((d, s // d, GROUP_WIDTH), BF16) for d in DILATIONS]
    outs += [jax.ShapeDtypeStruct((d, s // d, 2 * LANES), F32) for d in DILATIONS]
    outs += [jax.ShapeDtypeStruct((SUBLANES, D_MODEL), F32), jax.ShapeDtypeStruct((SUBLANES, CONV_WIDTH), F32)]
    outs += [jax.ShapeDtypeStruct((3,) + a.shape[1:], a.dtype) for a in exchange]
    return pl.pallas_call(
        body, name="mix_bwd", grid=(nt,), out_shape=outs,
        in_specs=[rows(D_MODEL), rows(3 * CONV_WIDTH), pl.BlockSpec((2 * SUBLANES, 3 * CONV_WIDTH), pre),
                  rows(2 * D_MODEL), rows(D_MODEL), rows(D_MODEL), rows(ATTN_WIDTH), rows(GROUP_WIDTH),
                  _resident((3, CONV_WIDTH)), _resident((1, CONV_WIDTH)), _resident((2, D_MODEL)),
                  _resident((CONV_WIDTH, D_MODEL)), _resident((ATTN_WIDTH, D_MODEL)), _resident((D_MODEL, D_MODEL))]
        + [_ANY] * nx,
        out_specs=[rows(D_MODEL), rows(D_MODEL), rows(2 * D_MODEL), rows(3 * CONV_WIDTH), rows(ATTN_WIDTH)]
        + [_rev_stream_block(tm, d, GROUP_WIDTH, nt) for d in DILATIONS]
        + [_rev_stream_block(tm, d, 2 * LANES, nt) for d in DILATIONS]
        + [pl.BlockSpec((SUBLANES, D_MODEL), lambda i: (0, 0)), pl.BlockSpec((SUBLANES, CONV_WIDTH), lambda i: (0, 0))]
        + [_ANY] * nx,
        scratch_shapes=[pltpu.VMEM((SUBLANES, CONV_WIDTH), F32),
                        pltpu.VMEM((GROUP_WIDTH // LANES * tm, LANES), F32)] + (_dma_sems(3 * nx) if nx else []),
        compiler_params=_params(1))(dx1, abcv, abcv, gates, ya, yb, yb0, lsetot,
                                    conv_w, conv_b, b_gate, w_pa, w_pb, w_out, *exchange)


def _attn_bwd(qkv, dy, aux, gi, exchange=(), swap=()):
    d, length, _ = qkv.shape
    nb = length // ATTN_BLOCK
    q = ATTN_BLOCK
    gw = GROUP_WIDTH
    g = min(ATTN_BLOCKS_PER_STEP, nb)
    assert g >= 2 and nb % g == 0
    ns = nb // g
    lag = 1 if ns > 1 else 0
    tail = (g - 1) * q
    nx, nw = len(exchange), len(swap)

    def body(*refs):
        q_ref, kp_ref, kc_ref, vp_ref, vc_ref, dy_ref, aux_ref = refs[:7]
        part_refs = refs[7:7 + nx]
        slab_refs = refs[7 + nx:7 + nx + nw]
        pos = 7 + nx + nw
        out_ref = refs[pos]
        recv_refs = refs[pos + 1:pos + 1 + nx]
        swapped_refs = refs[pos + 1 + nx:pos + 1 + nx + nw]
        pos += 1 + nx + nw
        dq_ref, dkv_ref, bias_ref = refs[pos:pos + 3]
        sems = refs[pos + 3:]
        n = pl.program_id(1)

        def copies():
            cps = _chip_exchange_copies(part_refs, recv_refs, sems[0], sems[1]) if nx else []
            return cps + (_sibling_swap_copies(slab_refs, swapped_refs, sems[-2], sems[-1]) if nw else [])

        @pl.when((pl.program_id(0) == 0) & (n == 0))
        def _():
            _store_band_biases(bias_ref)
            for cp in copies():
                cp.start()

        if nx or nw:
            @pl.when((pl.program_id(0) == d - 1) & (n == ns - 1 + lag))
            def _():
                for cp in copies():
                    cp.wait()

        def emit(rows):
            out_ref[rows, gw:2 * gw] = dkv_ref[0, rows].astype(BF16)
            out_ref[rows, 2 * gw:3 * gw] = dkv_ref[1, rows].astype(BF16)

        if lag:
            @pl.when(n > 0)
            def _():
                out_ref[:, 0:gw] = dq_ref[...].astype(BF16)
                emit(slice(0, tail))

            @pl.when(n == ns)
            def _():
                emit(slice(tail, g * q))

        @pl.when(n < ns)
        def _():
            kfull = jnp.concatenate([kp_ref[...], kc_ref[...]], axis=0)
            vfull = jnp.concatenate([vp_ref[...], vc_ref[...]], axis=0)
            for j in range(g):
                rows = slice(j * q, (j + 1) * q)
                qs = _stack_heads(q_ref[rows, :])
                dys = _stack_heads(dy_ref[rows, :])
                k2 = kfull[j * q:(j + 2) * q]
                v2 = vfull[j * q:(j + 2) * q]
                lse = _compact_head_col(aux_ref[rows, 0:LANES])
                delta = _compact_head_col(aux_ref[rows, LANES:2 * LANES])
                bias = jnp.where(n == 0, bias_ref[1], bias_ref[0]) if j == 0 else bias_ref[0]
                p = jnp.exp(_dot_nt(qs, k2) + bias - lse)
                dp = _dot_nt(dys, v2)
                ds = (p * (dp - delta)).astype(BF16)
                dq_j = _unstack_heads(_dot(ds, k2)) * ATTN_SCALE
                dk2 = _dot_tn(ds, qs)
                dv2 = _dot_tn(p.astype(BF16), dys)
                if j == 0:
                    @pl.when(n > 0)
                    def _():
                        out_ref[tail:g * q, gw:2 * gw] = (dkv_ref[0, tail:g * q] + dk2[0:q]).astype(BF16)
                        out_ref[tail:g * q, 2 * gw:3 * gw] = (dkv_ref[1, tail:g * q] + dv2[0:q]).astype(BF16)
                else:
                    dkv_ref[0, (j - 1) * q:j * q] += dk2[0:q]
                    dkv_ref[1, (j - 1) * q:j * q] += dv2[0:q]
                dkv_ref[0, rows] = dk2[q:2 * q]
                dkv_ref[1, rows] = dv2[q:2 * q]
                dq_ref[rows, :] = dq_j
            if not lag:
                out_ref[:, 0:gw] = dq_ref[...].astype(BF16)
                emit(slice(0, g * q))

    cur, prev = _attn_block_specs(g, nb, clamp_last=True)
    return pl.pallas_call(
        body, name=f"attn_bwd_g{gi}", grid=(d, ns + lag),
        out_shape=[jax.ShapeDtypeStruct((d, length, 3 * gw), BF16)]
        + [jax.ShapeDtypeStruct((3,) + a.shape[1:], a.dtype) for a in exchange] + _swap_shapes(swap),
        in_specs=[cur(0), prev(1), cur(1), prev(2), cur(2), cur(0), cur(0, 2 * LANES)] + [_ANY] * (nx + nw),
        out_specs=[pl.BlockSpec((None, g * q, 3 * gw), lambda r, n: (r, jnp.maximum(n - lag, 0), 0))]
        + [_ANY] * (nx + nw),
        scratch_shapes=[pltpu.VMEM((g * q, gw), F32), pltpu.VMEM((2, g * q, gw), F32),
                        pltpu.VMEM((2, HEADS_PER_GROUP * q, 2 * q), F32)]
        + (_dma_sems(3 * nx) if nx else []) + (_dma_sems(nw) if nw else []),
        compiler_params=_params(2))(qkv, qkv, qkv, qkv, qkv, dy, aux, *exchange, *swap)


def _matmul_tn(name, a, b, col_tile=1024, row_tile=2048, slabs=0, swap=()):
    s, k = a.shape
    n = b.shape[1]
    tk = min(row_tile, s)
    tn = col_tile
    steps = s // tk
    nw = len(swap)

    def body(*refs):
        a_ref, b_ref = refs[:2]
        slab_refs = refs[2:2 + nw]
        o_ref = refs[2 + nw]
        swapped_refs = refs[3 + nw:3 + 2 * nw]
        acc_ref = refs[3 + 2 * nw]
        sems = refs[4 + 2 * nw:]
        t = pl.program_id(1)

        if nw:
            @pl.when((pl.program_id(0) == 0) & (t == 0))
            def _():
                for cp in _sibling_swap_copies(slab_refs, swapped_refs, *sems):
                    cp.start()

            @pl.when((pl.program_id(0) == n // tn - 1) & (t == steps - 1))
            def _():
                for cp in _sibling_swap_copies(slab_refs, swapped_refs, *sems):
                    cp.wait()

        @pl.when(t == 0)
        def _():
            acc_ref[...] = jnp.zeros_like(acc_ref)

        acc_ref[...] += _dot_tn(a_ref[...], b_ref[...])

        @pl.when(t == steps - 1)
        def _():
            if slabs:
                for q in range(per_tile):
                    o_ref[q] = acc_ref[:, q * width:(q + 1) * width].astype(BF16)
            else:
                o_ref[...] = acc_ref[...].astype(BF16)

    if slabs:
        width = n // slabs
        per_tile = tn // width
        out_shape = jax.ShapeDtypeStruct((slabs, k, width), BF16)
        out_spec = pl.BlockSpec((per_tile, k, width), lambda j, t: (j, 0, 0))
    else:
        out_shape = jax.ShapeDtypeStruct((k, n), BF16)
        out_spec = pl.BlockSpec((k, tn), lambda j, t: (0, j))
    res = pl.pallas_call(
        body, name=name, grid=(n // tn, steps), out_shape=[out_shape] + _swap_shapes(swap),
        in_specs=[pl.BlockSpec((tk, k), lambda j, t: (t, 0)), pl.BlockSpec((tk, tn), lambda j, t: (t, j))] + [_ANY] * nw,
        out_specs=[out_spec] + [_ANY] * nw,
        scratch_shapes=[pltpu.VMEM((k, tn), F32)] + (_dma_sems(nw) if nw else []),
        compiler_params=_params(2))(a, b, *swap)
    return res if nw else res[0]


def _sibling_swap_halves(name, slabs):
    na = len(slabs)

    def body(*refs):
        src_refs, out_refs = refs[:na], refs[na:2 * na]
        send_sems, recv_sems = refs[2 * na:]
        x, y, c, _ = _mesh_position()
        cps = []
        for a in range(na):
            theirs = _half_rows(1 - c, src_refs[a].shape[1] // 2)
            cps.append(pltpu.make_async_remote_copy(
                src_ref=src_refs[a].at[:, theirs, :], dst_ref=out_refs[a], send_sem=send_sems.at[a],
                recv_sem=recv_sems.at[a], device_id=(x, y, 1 - c), device_id_type=MESH_ID))
        for cp in cps:
            cp.start()
        for cp in cps:
            cp.wait()

    return pl.pallas_call(
        body, name=name,
        out_shape=[jax.ShapeDtypeStruct((a.shape[0], a.shape[1] // 2, a.shape[2]), a.dtype) for a in slabs],
        in_specs=[_ANY] * na, out_specs=[_ANY] * na,
        scratch_shapes=[pltpu.SemaphoreType.DMA((na,)), pltpu.SemaphoreType.DMA((na,))])(*slabs)


_HBM = pl.BlockSpec(memory_space=pltpu.HBM)
_SEM = pl.BlockSpec(memory_space=pltpu.SEMAPHORE)
_DATAFLOW = pltpu.SideEffectType.DATAFLOW_SIDE_EFFECTING


def _chip_exchange_start(partial):
    _, rows, cols = partial.shape
    landing = jax.ShapeDtypeStruct((3, rows, cols), partial.dtype)

    def body(src_ref, land_ref, send_sems, recv_sems, src_thru, land_thru, token):
        for cp in _chip_exchange_copies([src_ref], [land_ref], send_sems, recv_sems):
            cp.start()
        token[...] = jnp.zeros_like(token)

    return pl.pallas_call(
        body, name="grad_exchange_start",
        out_shape=(pltpu.SemaphoreType.DMA((3,)), pltpu.SemaphoreType.DMA((3,)),
                   pltpu.HBM(partial.shape, partial.dtype), pltpu.HBM(landing.shape, landing.dtype),
                   jax.ShapeDtypeStruct((SUBLANES, LANES), F32)),
        in_specs=(_HBM, _HBM), out_specs=(_SEM, _SEM, _HBM, _HBM, _VMEM), input_output_aliases={0: 2, 1: 3},
        compiler_params=pltpu.CompilerParams(has_side_effects=_DATAFLOW),
    )(pltpu.with_memory_space_constraint(partial, pltpu.HBM),
      pltpu.with_memory_space_constraint(lax.empty(landing.shape, landing.dtype), pltpu.HBM))


def _chip_exchange_wait(send_sems, recv_sems, src_thru, land_thru, after):
    def body(src_ref, land_ref, send_sems, recv_sems, after_ref, src_out, land_out):
        for cp in _chip_exchange_copies([src_ref], [land_ref], send_sems, recv_sems):
            cp.wait_send()
            cp.wait_recv()

    return pl.pallas_call(
        body, name="grad_exchange_wait",
        out_shape=(pltpu.HBM(src_thru.shape, src_thru.dtype), pltpu.HBM(land_thru.shape, land_thru.dtype)),
        in_specs=(_HBM, _HBM, _SEM, _SEM, _ANY), out_specs=(_HBM, _HBM), input_output_aliases={0: 0, 1: 1},
        compiler_params=pltpu.CompilerParams(has_side_effects=_DATAFLOW),
    )(src_thru, land_thru, send_sems, recv_sems, after)


def _sibling_share_copies(refs, send_sems, recv_sems):
    x, y, c, _ = _mesh_position()
    cps = []
    for a, ref in enumerate(refs):
        mine = ref.at[0, _half_rows(c, ref.shape[1] // 2)]
        cps.append(_remote_copy(a, mine, mine, (x, y, 1 - c), send_sems, recv_sems))
    return cps


def _add_sibling(name, slabs, received, core):
    na = len(slabs)
    halves = [a.shape[1] // 2 for a in slabs]

    def body(core_ref, *refs):
        for a in range(na):
            refs[2 * na + a][...] = (refs[a][...].astype(F32) + refs[na + a][...].astype(F32)).astype(BF16)

    def block(a, mine):
        if mine:
            return pl.BlockSpec((None, halves[a], slabs[a].shape[2]), lambda s, core_ref: (s, core_ref[0], 0))
        return pl.BlockSpec((None, halves[a], slabs[a].shape[2]), lambda s, core_ref: (s, 0, 0))

    grid_spec = pltpu.PrefetchScalarGridSpec(
        num_scalar_prefetch=1, grid=(N_CHIPS,),
        in_specs=[block(a, True) for a in range(na)] + [block(a, False) for a in range(na)],
        out_specs=[block(a, False) for a in range(na)])
    return pl.pallas_call(body, name=name, grid_spec=grid_spec,
                          out_shape=[jax.ShapeDtypeStruct(r.shape, BF16) for r in received],
                          compiler_params=_params(1))(core, *slabs, *received)


def _sum_chips(name, partials, received, chip_core):
    na = len(partials)

    def body(cc_ref, *refs):
        for a in range(na):
            acc = refs[a][...].astype(F32)
            for k in range(3):
                acc = acc + refs[na + a][k].astype(F32)
            refs[2 * na + a][...] = acc

    def own(p):
        return pl.BlockSpec((None,) + p.shape[1:], lambda i, cc_ref: (cc_ref[0], 0, 0))

    def mine(p):
        return pl.BlockSpec((None,) + p.shape[1:], lambda i, cc_ref: (0, cc_ref[1], 0))

    grid_spec = pltpu.PrefetchScalarGridSpec(
        num_scalar_prefetch=1, grid=(1,),
        in_specs=[own(p) for p in partials] + [pl.BlockSpec(r.shape, lambda i, cc_ref: (0, 0, 0)) for r in received],
        out_specs=[mine(p) for p in partials])
    return pl.pallas_call(body, name=name, grid_spec=grid_spec,
                          out_shape=[jax.ShapeDtypeStruct((1, 2 * p.shape[1], p.shape[2]), F32) for p in partials],
                          compiler_params=_params(1))(chip_core, *partials, *received)


def _adam_math(w, g, m, v):
    nm = ADAM_B1 * m + (1.0 - ADAM_B1) * g
    nv = ADAM_B2 * v + (1.0 - ADAM_B2) * jnp.square(g)
    m_hat = nm / (1.0 - ADAM_B1 ** ADAM_STEP)
    v_hat = nv / (1.0 - ADAM_B2 ** ADAM_STEP)
    delta = -ADAM_LR * (m_hat / (jnp.sqrt(v_hat) + ADAM_EPS) + ADAM_WD * w)
    return delta, nm, nv


ADAMW_STEPS = 8


def _adamw(ws, gs, ms, vs):
    na = len(ws)

    def body(*refs):
        for a in range(na):
            w_ref, g_ref, m_ref, v_ref = (refs[k * na + a] for k in range(4))
            g_out_ref, d_ref, nm_ref, nv_ref = (refs[(4 + k) * na + a] for k in range(4))
            gv = g_ref[...]
            g_out_ref[...] = gv
            d_ref[...], nm_ref[...], nv_ref[...] = _adam_math(w_ref[...], gv, m_ref[...], v_ref[...])

    specs = [pl.BlockSpec((None, w.shape[1] // ADAMW_STEPS, w.shape[2]), lambda i: (0, i, 0)) for w in ws]
    outs = pl.pallas_call(
        body, name="adamw", grid=(ADAMW_STEPS,), out_shape=[jax.ShapeDtypeStruct(w.shape, F32) for w in ws] * 4,
        in_specs=specs * 4, out_specs=specs * 4, compiler_params=_params(1))(*ws, *gs, *ms, *vs)
    return [[outs[k * na + a] for k in range(4)] for a in range(na)]


SMALL_PARAMS = ("norm_mix_g", "b_gate", "conv_a_w", "conv_a_b", "norm_ffn_g", "ffn_conv_w", "ffn_conv_b", "final_norm_g")


def _small_update(partials, params, moments_m, moments_v, halves):
    na = len(partials)
    npar = len(SMALL_PARAMS)
    nh = len(halves)

    def body(*refs):
        in_refs = refs[:na]
        w_refs = refs[na:na + npar]
        m_refs = refs[na + npar:na + 2 * npar]
        v_refs = refs[na + 2 * npar:na + 3 * npar]
        pos = na + 3 * npar + nh
        loss_ref = refs[pos]
        out_refs = refs[pos + 1:pos + 1 + 4 * npar]
        big_refs = refs[pos + 1 + 4 * npar:pos + 1 + 4 * npar + nh]
        pos += 1 + 4 * npar + nh
        acc_refs = refs[pos:pos + na]
        recv_refs = refs[pos + na:pos + 4 * na]
        send_sems, recv_sems, share_send, share_recv = refs[pos + 4 * na:]
        x, y, c, _ = _mesh_position()
        chip = 2 * x + y
        for cp in _sibling_share_copies(big_refs, share_send, share_recv):
            cp.start()
        for a in range(na):
            acc_refs[a][...] = in_refs[a][...]
        for stage, peer in enumerate(((x, y, 1 - c), (x, 1 - y, c), (1 - x, y, c))):
            cps = []
            for a in range(na):
                k = stage * na + a
                cps.append(pltpu.make_async_remote_copy(src_ref=acc_refs[a], dst_ref=recv_refs[k], send_sem=send_sems.at[k],
                                                        recv_sem=recv_sems.at[k], device_id=peer, device_id_type=MESH_ID))
            for cp in cps:
                cp.start()
            for cp in cps:
                cp.wait()
            for a in range(na):
                acc_refs[a][...] = acc_refs[a][...] + recv_refs[stage * na + a][...]

        mix, ffn, fin, gate, conv, ffnc, loss = acc_refs
        loss_ref[...] = loss[...]

        def cols(width):
            return pl.ds(pl.multiple_of(chip * width, LANES), width)

        grads = {
            "norm_mix_g": mix[...], "norm_ffn_g": ffn[...], "final_norm_g": fin[...],
            "b_gate": gate[0:2, cols(D_MODEL // N_CHIPS)],
            "conv_a_w": conv[0:3, cols(CONV_WIDTH // N_CHIPS)], "conv_a_b": conv[3:4, :],
            "ffn_conv_w": ffnc[0:3, cols(2 * D_FF // N_CHIPS)], "ffn_conv_b": ffnc[3:4, :]}
        for i, name in enumerate(SMALL_PARAMS):
            g = grads[name]
            if len(w_refs[i].shape) == 3:
                results = (g,) + _adam_math(w_refs[i][0], g, m_refs[i][0], v_refs[i][0])
                for o_ref, val in zip(out_refs[4 * i:4 * i + 4], results):
                    o_ref[0] = val
            else:
                results = (g,) + _adam_math(w_refs[i][...], g, m_refs[i][...], v_refs[i][...])
                for o_ref, val in zip(out_refs[4 * i:4 * i + 4], results):
                    o_ref[...] = val

        for cp in _sibling_share_copies(big_refs, share_send, share_recv):
            cp.wait()

    outs = [jax.ShapeDtypeStruct(partials[-1].shape, F32)]
    for w in params:
        outs += [jax.ShapeDtypeStruct(w.shape, F32)] * 4
    n_small_out = len(outs)
    outs += [jax.ShapeDtypeStruct(h.shape, h.dtype) for h in halves]
    scratch = [pltpu.VMEM(p.shape, F32) for p in partials]
    scratch += [pltpu.VMEM(p.shape, F32) for _ in range(3) for p in partials]
    scratch += _dma_sems(3 * na) + _dma_sems(nh)
    n_in = na + 3 * npar
    return pl.pallas_call(
        body, name="small_update", out_shape=outs, in_specs=[_VMEM] * n_in + [_ANY] * nh,
        out_specs=[_VMEM] * n_small_out + [_ANY] * nh,
        input_output_aliases={n_in + a: n_small_out + a for a in range(nh)},
        scratch_shapes=scratch)(*partials, *params, *moments_m, *moments_v, *halves)


def _gathered_columns(g):
    return jnp.transpose(g, (1, 0, 2)).reshape(g.shape[1], N_CHIPS * g.shape[2])


def kernel(x, norm_mix_g, w_in, b_gate, conv_a_w, conv_a_b, w_proj_a, w_proj_b, w_out, norm_ffn_g, w_up, ffn_conv_w, ffn_conv_b, w_down, final_norm_g, loss_target, m_norm_mix_g, m_w_in, m_b_gate, m_conv_a_w, m_conv_a_b, m_w_proj_a, m_w_proj_b, m_w_out, m_norm_ffn_g, m_w_up, m_ffn_conv_w, m_ffn_conv_b, m_w_down, m_final_norm_g, v_norm_mix_g, v_w_in, v_b_gate, v_conv_a_w, v_conv_a_b, v_w_proj_a, v_w_proj_b, v_w_out, v_norm_ffn_g, v_w_up, v_ffn_conv_w, v_ffn_conv_b, v_w_down, v_final_norm_g):
    chip = (2 * lax.axis_index("x") + lax.axis_index("y")).astype(jnp.int32)
    core = lax.axis_index("c").astype(jnp.int32)
    core_arr = core.reshape(1)
    chip_core = jnp.stack([chip, core])
    xs, target = x[0], loss_target[0]
    g_final = final_norm_g.reshape(1, D_MODEL)

    def own_slot(gathered, own):
        return lax.dynamic_update_slice(gathered, own, (chip, 0, 0))

    w_in_t, m_w_in_t, v_w_in_t = (jnp.swapaxes(a, 1, 2) for a in (w_in, m_w_in, v_w_in))
    w_in_tb = w_in_t.astype(BF16)
    h1, h1_streams4, h1_streams16, g_in = _norm_fwd(xs, norm_mix_g, w_in_tb)
    w_in_full_t = own_slot(g_in, w_in_tb).reshape(D_IN, D_MODEL)
    later_w = [w_proj_a, w_proj_b, w_out, w_up, w_down]
    later_b = [w.astype(BF16) for w in later_w]
    small_sharded = [b_gate, conv_a_w, ffn_conv_w]
    fwd = _inproj_fwd(h1, w_in_full_t, later_b, small_sharded)
    abcv, gates, qkv0, qkv1, qkv2 = fwd[:5]
    gathered_big, gathered_small = fwd[5:10], fwd[10:13]
    attn0 = _attn_fwd(qkv0, 0, forward=gathered_big)
    attn = [attn0[:2], _attn_fwd(qkv1, 1), _attn_fwd(qkv2, 2)]
    g_pa, g_pb, g_out, g_up, g_down = [own_slot(g, own) for g, own in zip(attn0[2:], later_b)]
    g_bgate, g_convw, g_ffnw = [own_slot(g, own) for g, own in zip(gathered_small, small_sharded)]
    w_pa_full, w_pb_full, w_up_full = _gathered_columns(g_pa), _gathered_columns(g_pb), _gathered_columns(g_up)
    w_out_full, w_down_full = g_out.reshape(D_MODEL, D_MODEL), g_down.reshape(D_FF, D_MODEL)
    b_gate_full, conv_w_full, ffn_w_full = (_gathered_columns(g) for g in (g_bgate, g_convw, g_ffnw))

    x1, ya0, yb0, mrg, ya, yb, lsetot = _mix_fwd(
        xs, abcv, gates, [a[0] for a in attn], [a[1] for a in attn], conv_w_full, conv_a_b, b_gate_full,
        w_pa_full, w_pb_full, w_out_full)
    h2, up0, up = _ffn_up_fwd(x1, norm_ffn_g, w_up_full, ffn_w_full, ffn_conv_b)
    act, dx2, dx2b, d_g_final, loss = _ffn_act_fwd(x1, up, target, w_down_full, g_final)

    d_up = _ffn_act_bwd(dx2b, up, w_down_full)
    slab_down = _matmul_tn("dw_down", act, dx2b, col_tile=512).reshape(N_CHIPS, D_FF // N_CHIPS, D_MODEL)
    dx1, dx1b, d_g_ffn, d_up0, ffn_small, swapped_down = _ffn_up_bwd(
        d_up, up0, w_up_full, ffn_w_full, x1, norm_ffn_g, dx2, swap=[slab_down])
    (partial_down,) = _add_sibling("grad_add_w_down", [slab_down], [swapped_down], core_arr)
    slab_up = _matmul_tn("dw_up", h2, d_up0, col_tile=2 * D_FF // N_CHIPS, slabs=N_CHIPS)
    d_w_out, swapped_up = _matmul_tn("dw_out", mrg, dx1b, swap=[slab_up])
    (partial_up,) = _add_sibling("grad_add_w_up", [slab_up], [swapped_up], core_arr)

    mix_res = _mix_bwd(dx1, abcv, gates, ya, yb, yb0, lsetot, conv_w_full, conv_a_b, b_gate_full,
                       w_pa_full, w_pb_full, w_out_full, exchange=[partial_up, partial_down])
    (d_ya, d_yb, d_gates, d_abcv, d_yb0, dyl0, dyl1, dyl2, aux0, aux1, aux2, gate_small, conv_small) = mix_res[:13]
    halves_ffn = _sum_chips("grad_sum_ffn", [partial_up, partial_down], mix_res[13:], chip_core)

    slabs_mix = [_matmul_tn("dw_proj_a", ya0, d_ya, slabs=N_CHIPS), _matmul_tn("dw_proj_b", yb0, d_yb, slabs=N_CHIPS),
                 d_w_out.reshape(N_CHIPS, D_MODEL // N_CHIPS, D_MODEL)]
    res0 = _attn_bwd(qkv0, dyl0, aux0, 0, swap=slabs_mix)
    d_qkv0, partials_mix = res0[0], _add_sibling("grad_add_mix", slabs_mix, res0[1:], core_arr)
    res1 = _attn_bwd(qkv1, dyl1, aux1, 1, exchange=partials_mix)
    d_qkv1, halves_mix = res1[0], _sum_chips("grad_sum_mix", partials_mix, res1[1:], chip_core)
    (d_qkv2,) = _attn_bwd(qkv2, dyl2, aux2, 2)

    dq = [d_qkv0, d_qkv1, d_qkv2]
    seq = xs.shape[0]
    d_w_abcv = _matmul_tn("dw_in_abcv", d_abcv, h1)
    d_w_gates = _matmul_tn("dw_in_gates", d_gates, h1)
    d_w_groups = [_matmul_tn(f"dw_in_qkv{g}", t.reshape(seq, 3 * GROUP_WIDTH), h.reshape(seq, D_MODEL))
                  for g, (t, h) in enumerate(zip(dq, (h1, h1_streams4, h1_streams16)))]
    gw = GROUP_WIDTH
    d_w_in_t = jnp.concatenate(
        [d_w_abcv] + [d_w_groups[g][j * gw:(j + 1) * gw] for j in range(3) for g in range(3)] + [d_w_gates], axis=0)

    slab_in = d_w_in_t.reshape(N_CHIPS, D_IN // N_CHIPS, D_MODEL)
    (from_sibling_in,) = _sibling_swap_halves("grad_swap_w_in", [slab_in])
    (partial_in,) = _add_sibling("grad_add_w_in", [slab_in], [from_sibling_in], core_arr)
    send_sems, recv_sems, partial_thru, landing_thru, token = _chip_exchange_start(partial_in)
    g_mix_after_start = norm_mix_g + token[0:1, 0:1]
    grad_x, d_g_mix = _inproj_bwd(d_abcv, d_gates, dq, w_in_full_t, xs, g_mix_after_start, dx1)
    partial_in, received_in = _chip_exchange_wait(send_sems, recv_sems, partial_thru, landing_thru, d_g_mix)
    halves_in = _sum_chips("grad_sum_w_in", [partial_in], [received_in], chip_core)

    big_names = ("w_in", "w_proj_a", "w_proj_b", "w_out", "w_up", "w_down")
    big_w = dict(w_in=w_in_t, w_proj_a=w_proj_a, w_proj_b=w_proj_b, w_out=w_out, w_up=w_up, w_down=w_down)
    big_m = dict(w_in=m_w_in_t, w_proj_a=m_w_proj_a, w_proj_b=m_w_proj_b, w_out=m_w_out, w_up=m_w_up, w_down=m_w_down)
    big_v = dict(w_in=v_w_in_t, w_proj_a=v_w_proj_a, w_proj_b=v_w_proj_b, w_out=v_w_out, w_up=v_w_up, w_down=v_w_down)

    fin_w, fin_m, fin_v = (a.reshape(1, D_MODEL) for a in (final_norm_g, m_final_norm_g, v_final_norm_g))
    small_w = [norm_mix_g, b_gate, conv_a_w, conv_a_b, norm_ffn_g, ffn_conv_w, ffn_conv_b, fin_w]
    small_m = [m_norm_mix_g, m_b_gate, m_conv_a_w, m_conv_a_b, m_norm_ffn_g, m_ffn_conv_w, m_ffn_conv_b, fin_m]
    small_v = [v_norm_mix_g, v_b_gate, v_conv_a_w, v_conv_a_b, v_norm_ffn_g, v_ffn_conv_w, v_ffn_conv_b, fin_v]
    small_out = _small_update([d_g_mix, d_g_ffn, d_g_final, gate_small, conv_small, ffn_small, loss],
                              small_w, small_m, small_v, halves_in + halves_mix + halves_ffn)
    big_grads = small_out[1 + 4 * len(SMALL_PARAMS):]
    total_loss = small_out[0][0, 0]

    grads, delta, new_m, new_v = {}, {}, {}, {}
    for i, n in enumerate(SMALL_PARAMS):
        vals = small_out[1 + 4 * i:5 + 4 * i]
        if n == "final_norm_g":
            vals = [a.reshape(D_MODEL) for a in vals]
        grads[n], delta[n], new_m[n], new_v[n] = vals
    updates = _adamw([big_w[n] for n in big_names], big_grads, [big_m[n] for n in big_names],
                     [big_v[n] for n in big_names])
    for n, vals in zip(big_names, updates):
        if n == "w_in":
            vals = [jnp.swapaxes(a, 1, 2) for a in vals]
        grads[n], delta[n], new_m[n], new_v[n] = vals

    names = ["norm_mix_g", "w_in", "b_gate", "conv_a_w", "conv_a_b", "w_proj_a", "w_proj_b", "w_out", "norm_ffn_g", "w_up",
             "ffn_conv_w", "ffn_conv_b", "w_down", "final_norm_g"]
    out = [total_loss, grad_x[None]]
    for group in (grads, delta, new_m, new_v):
        out += [group[n] for n in names]
    return tuple(out)
```

```python
import jax
import jax.numpy as jnp
from jax import lax
from jax.experimental import pallas as pl
from jax.experimental.pallas import tpu as pltpu

F32 = jnp.float32
BF16 = jnp.bfloat16

D_MODEL = 1024
CONV_WIDTH = 512
ATTN_WIDTH = 768
GROUP_WIDTH = 256
HEAD_DIM = 64
HEADS_PER_GROUP = 4
DILATIONS = (1, 4, 16)
ATTN_BLOCK = 128
D_FF = 2816
D_IN = 5888
EPS = 1e-6
NEG_INF = -1e30
ATTN_SCALE = HEAD_DIM ** -0.5

COL_ABCV = 0
COL_Q = 1536
COL_K = 2304
COL_V = 3072
COL_GATES = 3840

ADAM_LR = 0.001
ADAM_B1 = 0.9
ADAM_B2 = 0.999
ADAM_EPS = 1e-08
ADAM_WD = 0.01
ADAM_STEP = 10

LANES = 128
SUBLANES = 8
BF16_ROWS = 16
ROW_TILE = 512
VMEM_LIMIT = 56 * 1024 * 1024

_NT = (((1,), (1,)), ((), ()))
_TN = (((0,), (0,)), ((), ()))


def _params(n_axes, vmem=VMEM_LIMIT):
    return pltpu.CompilerParams(dimension_semantics=("arbitrary",) * n_axes, vmem_limit_bytes=vmem)


def _resident(shape):
    nd = len(shape)
    return pl.BlockSpec(shape, lambda *_: (0,) * nd, pipeline_mode=pl.Buffered(1))


def _rows(tm, width, col_block=0):
    return pl.BlockSpec((tm, width), lambda i: (i, col_block))


def _col_chunks(n, cmax):
    out, lo = [], 0
    while lo < n:
        size = min(cmax, n - lo)
        out.append((lo, size))
        lo += size
    return out


def _dot(a, b):
    return jnp.dot(a, b, preferred_element_type=F32)


def _dot_nt(a, b):
    return lax.dot_general(a, b, _NT, preferred_element_type=F32)


def _dot_tn(a, b):
    return lax.dot_general(a, b, _TN, preferred_element_type=F32)


def _sigmoid(x):
    return 0.5 * jnp.tanh(0.5 * x) + 0.5


def _silu(x):
    hx = 0.5 * x
    return hx + hx * jnp.tanh(hx)


def _shift_down(v, k, halo8):
    tm = v.shape[0]
    rolled = pltpu.roll(v, k, 0)
    fix = jnp.tile(pltpu.roll(halo8, k, 0), (tm // SUBLANES, 1))
    row = lax.broadcasted_iota(jnp.int32, v.shape, 0)
    return jnp.where(row < k, fix, rolled)


def _shift_up(v, k, halo8):
    tm = v.shape[0]
    rolled = pltpu.roll(v, tm - k, 0)
    fix = jnp.tile(pltpu.roll(halo8, SUBLANES - k, 0), (tm // SUBLANES, 1))
    row = lax.broadcasted_iota(jnp.int32, v.shape, 0)
    return jnp.where(row >= tm - k, fix, rolled)


def _colsum(v):
    return jnp.sum(v, axis=0, keepdims=True)


def _to_streams(val, scr, out_ref, d, col0):
    tm = val.shape[0]
    panels = val.shape[1] // LANES
    if d == 1:
        out_ref[0, :, col0:col0 + val.shape[1]] = val.astype(out_ref.dtype)
        return
    for p in range(panels):
        scr[pl.ds(p * tm, tm), :] = val[:, p * LANES:(p + 1) * LANES]
    for r in range(d):
        for p in range(panels):
            piece = scr[pl.ds(p * tm + r, tm // d, stride=d), :]
            out_ref[r, :, col0 + p * LANES: col0 + (p + 1) * LANES] = piece.astype(out_ref.dtype)


def _from_streams(in_ref, scr, d, col0, width):
    panels = width // LANES
    rows = in_ref.shape[1]
    tm = rows * d
    if d == 1:
        return in_ref[0, :, col0:col0 + width].astype(F32)
    for r in range(d):
        for p in range(panels):
            scr[pl.ds(p * tm + r, rows, stride=d), :] = in_ref[r, :, col0 + p * LANES: col0 + (p + 1) * LANES].astype(F32)
    return jnp.concatenate([scr[pl.ds(p * tm, tm), :] for p in range(panels)], axis=1)


def _stream_block(tm, d, width):
    return pl.BlockSpec((d, tm // d, width), lambda i: (0, i, 0))


def _rev_stream_block(tm, d, width, nt):
    return pl.BlockSpec((d, tm // d, width), lambda i: (0, nt - 1 - i, 0))


N_CHIPS = 4
MESH_ID = pl.DeviceIdType.MESH
_ANY = pl.BlockSpec(memory_space=pl.ANY)
_VMEM = pl.BlockSpec(memory_space=pltpu.VMEM)


def _mesh_position():
    x, y, c = lax.axis_index("x"), lax.axis_index("y"), lax.axis_index("c")
    other_chips = [(1 - x, y), (x, 1 - y), (1 - x, 1 - y)]
    return x, y, c, other_chips


def _half_rows(c, half):
    return pl.ds(pl.multiple_of(c * half, BF16_ROWS), half)


def _remote_copy(k, src, dst, to, send_sems, recv_sems):
    return pltpu.make_async_remote_copy(src_ref=src, dst_ref=dst, send_sem=send_sems.at[k], recv_sem=recv_sems.at[k],
                                        device_id=to, device_id_type=MESH_ID)


def _gather_first_copies(big_refs, small_refs, big_outs, small_outs, send_sems, recv_sems):
    x, y, c, chips = _mesh_position()
    me = 2 * x + y
    nb = len(big_refs)
    cps = []
    for j, (px, py) in enumerate(chips):
        for b in range(nb):
            mine = _half_rows(c, big_refs[b].shape[1] // 2)
            cps.append(_remote_copy(3 * b + j, big_refs[b].at[0, mine], big_outs[b].at[me, mine], (px, py, c),
                                    send_sems, recv_sems))
        for s in range(len(small_refs)):
            cps.append(_remote_copy(3 * (nb + s) + j, small_refs[s].at[0], small_outs[s].at[me], (px, py, c),
                                    send_sems, recv_sems))
    return cps


def _gather_forward_copies(bufs, send_sems, recv_sems):
    x, y, c, chips = _mesh_position()
    cps = []
    for j, (px, py) in enumerate(chips):
        for b in range(len(bufs)):
            landed = bufs[b].at[2 * px + py, _half_rows(c, bufs[b].shape[1] // 2)]
            cps.append(_remote_copy(3 * b + j, landed, landed, (x, y, 1 - c), send_sems, recv_sems))
    return cps


def _chip_exchange_copies(src_refs, out_refs, send_sems, recv_sems):
    x, y, c, chips = _mesh_position()
    cps = []
    for j, (px, py) in enumerate(chips):
        for a in range(len(src_refs)):
            cps.append(_remote_copy(3 * a + j, src_refs[a].at[2 * px + py], out_refs[a].at[j], (px, py, c),
                                    send_sems, recv_sems))
    return cps


def _sibling_swap_copies(src_refs, out_refs, send_sems, recv_sems):
    x, y, c, _ = _mesh_position()
    cps = []
    for a in range(len(src_refs)):
        theirs = _half_rows(1 - c, src_refs[a].shape[1] // 2)
        cps.append(_remote_copy(a, src_refs[a].at[:, theirs, :], out_refs[a], (x, y, 1 - c), send_sems, recv_sems))
    return cps


def _swap_shapes(slabs):
    return [jax.ShapeDtypeStruct((a.shape[0], a.shape[1] // 2, a.shape[2]), a.dtype) for a in slabs]


def _dma_sems(n):
    return [pltpu.SemaphoreType.DMA((n,)), pltpu.SemaphoreType.DMA((n,))]


def _norm_fwd(x, g):
    s = x.shape[0]
    tm = ROW_TILE

    def body(x_ref, g_ref, h_ref, hs1_ref, hs2_ref, scr):
        xv = x_ref[...]
        r = lax.rsqrt(jnp.mean(xv * xv, axis=-1, keepdims=True) + EPS)
        hf = xv * r * g_ref[...]
        h_ref[...] = hf.astype(BF16)
        for d, hs_ref in zip(DILATIONS[1:], (hs1_ref, hs2_ref)):
            for lo, size in _col_chunks(D_MODEL, GROUP_WIDTH):
                _to_streams(hf[:, lo:lo + size], scr, hs_ref, d, lo)

    return pl.pallas_call(
        body, name="norm_fwd", grid=(s // tm,),
        out_shape=[jax.ShapeDtypeStruct((s, D_MODEL), BF16)]
        + [jax.ShapeDtypeStruct((d, s // d, D_MODEL), BF16) for d in DILATIONS[1:]],
        in_specs=[_rows(tm, D_MODEL), _resident((1, D_MODEL))],
        out_specs=[_rows(tm, D_MODEL)] + [_stream_block(tm, d, D_MODEL) for d in DILATIONS[1:]],
        scratch_shapes=[pltpu.VMEM((GROUP_WIDTH // LANES * tm, LANES), F32)],
        compiler_params=_params(1))(x, g)


def _inproj_fwd(h1, w_in_t, big_shards, small_shards):
    s = h1.shape[0]
    tm = ROW_TILE
    nt = s // tm
    nb, ns = len(big_shards), len(small_shards)
    n_fixed_in, n_fixed_out = 2, 5

    def body(*refs):
        h_ref, w_ref = refs[:n_fixed_in]
        shard_refs = refs[n_fixed_in:n_fixed_in + nb + ns]
        pos = n_fixed_in + nb + ns
        abcv_ref, gates_ref, qkv0_ref, qkv1_ref, qkv2_ref = refs[pos:pos + n_fixed_out]
        gathered_refs = refs[pos + n_fixed_out:pos + n_fixed_out + nb + ns]
        scr, send_sems, recv_sems = refs[pos + n_fixed_out + nb + ns:]
        i = pl.program_id(0)

        def gather_copies():
            return _gather_first_copies(shard_refs[:nb], shard_refs[nb:], gathered_refs[:nb], gathered_refs[nb:],
                                        send_sems, recv_sems)

        @pl.when(i == 0)
        def _():
            for cp in gather_copies():
                cp.start()

        h = h_ref[...]
        for lo, size in _col_chunks(3 * CONV_WIDTH, 512):
            abcv_ref[:, lo:lo + size] = _dot_nt(h, w_ref[COL_ABCV + lo: COL_ABCV + lo + size, :]).astype(BF16)
        for lo, size in _col_chunks(2 * D_MODEL, 512):
            gates_ref[:, lo:lo + size] = _dot_nt(h, w_ref[COL_GATES + lo: COL_GATES + lo + size, :]).astype(BF16)
        for gi, (d, out_ref) in enumerate(zip(DILATIONS, (qkv0_ref, qkv1_ref, qkv2_ref))):
            for j, base in enumerate((COL_Q, COL_K, COL_V)):
                lo = base + gi * GROUP_WIDTH
                y = _dot_nt(h, w_ref[lo:lo + GROUP_WIDTH, :])
                if j == 0:
                    y = y * ATTN_SCALE
                _to_streams(y, scr, out_ref, d, j * GROUP_WIDTH)

        @pl.when(i == nt - 1)
        def _():
            for cp in gather_copies():
                cp.wait()

    outs = [jax.ShapeDtypeStruct((s, 3 * CONV_WIDTH), BF16), jax.ShapeDtypeStruct((s, 2 * D_MODEL), BF16)]
    outs += [jax.ShapeDtypeStruct((d, s // d, 3 * GROUP_WIDTH), BF16) for d in DILATIONS]
    outs += [jax.ShapeDtypeStruct((N_CHIPS,) + a.shape[1:], a.dtype) for a in list(big_shards) + list(small_shards)]
    return pl.pallas_call(
        body, name="inproj_fwd", grid=(nt,), out_shape=outs,
        in_specs=[_rows(tm, D_MODEL), _resident((D_IN, D_MODEL))] + [_ANY] * (nb + ns),
        out_specs=[_rows(tm, 3 * CONV_WIDTH), _rows(tm, 2 * D_MODEL)]
        + [_stream_block(tm, d, 3 * GROUP_WIDTH) for d in DILATIONS] + [_ANY] * (nb + ns),
        scratch_shapes=[pltpu.VMEM((GROUP_WIDTH // LANES * tm, LANES), F32)] + _dma_sems(3 * (nb + ns)),
        compiler_params=_params(1))(h1, w_in_t, *big_shards, *small_shards)


def _head_of_lane(shape):
    return lax.broadcasted_iota(jnp.int32, shape, 1) // HEAD_DIM


def _stack_heads(v):
    head = _head_of_lane(v.shape)
    return jnp.concatenate([jnp.where(head == h, v, jnp.zeros_like(v)) for h in range(HEADS_PER_GROUP)], axis=0)


def _unstack_heads(v):
    q = ATTN_BLOCK
    head = _head_of_lane((q, v.shape[1]))
    out = jnp.zeros((q, v.shape[1]), v.dtype)
    for h in range(HEADS_PER_GROUP):
        out = jnp.where(head == h, v[h * q:(h + 1) * q], out)
    return out


def _per_head_rows(col):
    q = ATTN_BLOCK
    head = _head_of_lane((q, GROUP_WIDTH))
    out = jnp.zeros((q, GROUP_WIDTH), col.dtype)
    for h in range(HEADS_PER_GROUP):
        out = jnp.where(head == h, col[h * q:(h + 1) * q], out)
    return out


def _compact_heads(v):
    lane = lax.broadcasted_iota(jnp.int32, (v.shape[0], LANES), 1)
    return jnp.where((lane & 32) == 0, v[:, 0:LANES], v[:, LANES:2 * LANES])


def _compact_head_col(v):
    lane = lax.broadcasted_iota(jnp.int32, v.shape, 1)
    head = ((lane >> 6) & 1) + 2 * ((lane >> 5) & 1)
    cols = [jnp.max(jnp.where(head == h, v, -jnp.inf), axis=1, keepdims=True) for h in range(HEADS_PER_GROUP)]
    return jnp.concatenate(cols, axis=0)


ATTN_BLOCKS_PER_STEP = 8


def _band_bias(first_block):
    rows = HEADS_PER_GROUP * ATTN_BLOCK
    qi = lax.broadcasted_iota(jnp.int32, (rows, 2 * ATTN_BLOCK), 0) % ATTN_BLOCK
    kj = lax.broadcasted_iota(jnp.int32, (rows, 2 * ATTN_BLOCK), 1)
    dist = qi + ATTN_BLOCK - kj
    valid = (dist >= 0) & (dist <= ATTN_BLOCK)
    if first_block:
        valid = valid & (kj >= ATTN_BLOCK)
    return jnp.where(valid, 0.0, NEG_INF).astype(F32)


def _store_band_biases(bias_ref):
    bias_ref[0] = _band_bias(False)
    bias_ref[1] = _band_bias(True)


def _attn_block_specs(g, nb, clamp_last=False):
    q = ATTN_BLOCK
    last = nb // g - 1

    def cur(col, width=GROUP_WIDTH):
        if clamp_last:
            return pl.BlockSpec((None, g * q, width), lambda r, n: (r, jnp.minimum(n, last), col))
        return pl.BlockSpec((None, g * q, width), lambda r, n: (r, n, col))

    def prev(col):
        if clamp_last:
            return pl.BlockSpec((None, q, GROUP_WIDTH), lambda r, n: (r, jnp.clip(n * g - 1, 0, nb - 1), col))
        return pl.BlockSpec((None, q, GROUP_WIDTH), lambda r, n: (r, jnp.maximum(n * g - 1, 0), col))

    return cur, prev


def _attn_fwd(qkv, gi, forward=()):
    d, length, _ = qkv.shape
    nb = length // ATTN_BLOCK
    q = ATTN_BLOCK
    g = min(ATTN_BLOCKS_PER_STEP, nb)
    ns = nb // g
    nf = len(forward)

    def body(*refs):
        q_ref, kp_ref, kc_ref, vp_ref, vc_ref = refs[:5]
        o_ref, lse_ref = refs[5 + nf:7 + nf]
        buf_refs = refs[7 + nf:7 + 2 * nf]
        bias_ref = refs[7 + 2 * nf]
        sems = refs[8 + 2 * nf:]
        n = pl.program_id(1)
        first_step = (pl.program_id(0) == 0) & (n == 0)
        last_step = (pl.program_id(0) == d - 1) & (n == ns - 1)

        @pl.when(first_step)
        def _():
            _store_band_biases(bias_ref)
            for cp in _gather_forward_copies(buf_refs, *sems) if nf else ():
                cp.start()

        kfull = jnp.concatenate([kp_ref[...], kc_ref[...]], axis=0)
        vfull = jnp.concatenate([vp_ref[...], vc_ref[...]], axis=0)
        for j in range(g):
            qs = _stack_heads(q_ref[j * q:(j + 1) * q, :])
            k2 = kfull[j * q:(j + 2) * q]
            v2 = vfull[j * q:(j + 2) * q]
            bias = jnp.where(n == 0, bias_ref[1], bias_ref[0]) if j == 0 else bias_ref[0]
            sc = _dot_nt(qs, k2) + bias
            m = jnp.max(sc, axis=1, keepdims=True)
            p = jnp.exp(sc - m)
            l = jnp.sum(p, axis=1, keepdims=True)
            of = _dot(p.astype(BF16), v2) / l
            o_ref[j * q:(j + 1) * q, :] = _unstack_heads(of).astype(BF16)
            lse_ref[j * q:(j + 1) * q, :] = _per_head_rows(m + jnp.log(l))

        if nf:
            @pl.when(last_step)
            def _():
                for cp in _gather_forward_copies(buf_refs, *sems):
                    cp.wait()

    cur, prev = _attn_block_specs(g, nb)
    return pl.pallas_call(
        body, name=f"attn_fwd_g{gi}", grid=(d, ns),
        out_shape=[jax.ShapeDtypeStruct((d, length, GROUP_WIDTH), BF16),
                   jax.ShapeDtypeStruct((d, length, GROUP_WIDTH), F32)]
        + [jax.ShapeDtypeStruct(a.shape, a.dtype) for a in forward],
        in_specs=[cur(0), prev(1), cur(1), prev(2), cur(2)] + [_ANY] * nf,
        out_specs=[cur(0), cur(0)] + [_ANY] * nf,
        input_output_aliases={5 + a: 2 + a for a in range(nf)},
        scratch_shapes=[pltpu.VMEM((2, HEADS_PER_GROUP * q, 2 * q), F32)] + (_dma_sems(3 * nf) if nf else []),
        compiler_params=_params(2))(qkv, qkv, qkv, qkv, qkv, *forward)


def _conv_branch(ab, ac, av, halo_u, w, b):
    u = ac * av
    sh1 = _shift_down(u, 1, halo_u)
    sh2 = _shift_down(u, 2, halo_u)
    cv = w[0:1] * sh2 + w[1:2] * sh1 + w[2:3] * u + b
    return ab * cv, cv, u, sh1, sh2


def _mix_fwd(x, abcv, gates, o_list, lse_list, conv_w, conv_b, b_gate, w_pa, w_pb, w_out):
    s = x.shape[0]
    tm = ROW_TILE

    def body(x_ref, abcv_ref, gates_ref, o0_ref, o1_ref, o2_ref, l0_ref, l1_ref, l2_ref,
             cw_ref, cb_ref, bg_ref, wpa_ref, wpb_ref, wout_ref,
             x1_ref, ya0_ref, yb0_ref, mrg_ref, ya_ref, yb_ref, lsetot_ref, halo_ref, scr):
        i = pl.program_id(0)

        @pl.when(i == 0)
        def _():
            halo_ref[...] = jnp.zeros_like(halo_ref)

        ab = abcv_ref[:, 0:CONV_WIDTH].astype(F32)
        ac = abcv_ref[:, CONV_WIDTH:2 * CONV_WIDTH].astype(F32)
        av = abcv_ref[:, 2 * CONV_WIDTH:3 * CONV_WIDTH].astype(F32)
        ya0, _, u, _, _ = _conv_branch(ab, ac, av, halo_ref[...], cw_ref[...], cb_ref[...])
        halo_ref[...] = u[tm - SUBLANES:tm]
        ya0 = ya0.astype(BF16)
        ya0_ref[...] = ya0
        ya = _dot(ya0, wpa_ref[...])

        o_refs, l_refs = (o0_ref, o1_ref, o2_ref), (l0_ref, l1_ref, l2_ref)
        lses = [_from_streams(l_refs[g], scr, DILATIONS[g], 0, GROUP_WIDTH) for g in range(3)]
        top = jnp.maximum(jnp.maximum(lses[0], lses[1]), lses[2])
        weights = [jnp.exp(lse - top) for lse in lses]
        total = weights[0] + weights[1] + weights[2]
        lsetot_ref[...] = top + jnp.log(total)
        inv_total = 1.0 / total
        yb = jnp.zeros((tm, D_MODEL), F32)
        for g in range(3):
            og = _from_streams(o_refs[g], scr, DILATIONS[g], 0, GROUP_WIDTH)
            yb0 = (weights[g] * inv_total * og).astype(BF16)
            yb0_ref[:, g * GROUP_WIDTH:(g + 1) * GROUP_WIDTH] = yb0
            yb = yb + _dot(yb0, wpb_ref[g * GROUP_WIDTH:(g + 1) * GROUP_WIDTH, :])

        sa = _sigmoid(gates_ref[:, 0:D_MODEL].astype(F32) + bg_ref[0:1, :])
        sb = _sigmoid(gates_ref[:, D_MODEL:2 * D_MODEL].astype(F32) + bg_ref[1:2, :])
        ya_ref[...] = ya.astype(BF16)
        yb_ref[...] = yb.astype(BF16)
        mrg = (sa * ya + sb * yb).astype(BF16)
        mrg_ref[...] = mrg
        x1_ref[...] = x_ref[...] + _dot(mrg, wout_ref[...])

    outs = [jax.ShapeDtypeStruct((s, D_MODEL), F32),
            jax.ShapeDtypeStruct((s, CONV_WIDTH), BF16),
            jax.ShapeDtypeStruct((s, ATTN_WIDTH), BF16),
            jax.ShapeDtypeStruct((s, D_MODEL), BF16),
            jax.ShapeDtypeStruct((s, D_MODEL), BF16),
            jax.ShapeDtypeStruct((s, D_MODEL), BF16),
            jax.ShapeDtypeStruct((s, GROUP_WIDTH), F32)]
    return pl.pallas_call(
        body, name="mix_fwd", grid=(s // tm,), out_shape=outs,
        in_specs=[_rows(tm, D_MODEL), _rows(tm, 3 * CONV_WIDTH), _rows(tm, 2 * D_MODEL)]
        + [_stream_block(tm, d, GROUP_WIDTH) for d in DILATIONS] * 2
        + [_resident((3, CONV_WIDTH)), _resident((1, CONV_WIDTH)), _resident((2, D_MODEL)),
           _resident((CONV_WIDTH, D_MODEL)), _resident((ATTN_WIDTH, D_MODEL)), _resident((D_MODEL, D_MODEL))],
        out_specs=[_rows(tm, D_MODEL), _rows(tm, CONV_WIDTH), _rows(tm, ATTN_WIDTH), _rows(tm, D_MODEL),
                   _rows(tm, D_MODEL), _rows(tm, D_MODEL), _rows(tm, GROUP_WIDTH)],
        scratch_shapes=[pltpu.VMEM((SUBLANES, CONV_WIDTH), F32),
                        pltpu.VMEM((GROUP_WIDTH // LANES * tm, LANES), F32)],
        compiler_params=_params(1))(x, abcv, gates, *o_list, *lse_list, conv_w, conv_b, b_gate, w_pa, w_pb, w_out)


FFN_CHUNK = 512
FFN_UP_ROW_TILE = 256


def _ffn_up_fwd(x1, g, w_up, conv_w, conv_b):
    s = x1.shape[0]
    n = w_up.shape[1]
    tm = FFN_UP_ROW_TILE

    def body(x_ref, g_ref, w_ref, cw_ref, cb_ref, h_ref, up0_ref, up_ref, halo_ref):
        @pl.when(pl.program_id(0) == 0)
        def _():
            halo_ref[...] = jnp.zeros_like(halo_ref)

        xv = x_ref[...]
        r = lax.rsqrt(jnp.mean(xv * xv, axis=-1, keepdims=True) + EPS)
        h = (xv * r * g_ref[...]).astype(BF16)
        h_ref[...] = h
        for lo, size in _col_chunks(n, FFN_CHUNK):
            cols = slice(lo, lo + size)
            y = _dot(h, w_ref[:, cols])
            up0_ref[:, cols] = y.astype(BF16)
            halo = halo_ref[:, cols]
            w = cw_ref[:, cols]
            up = w[0:1] * _shift_down(y, 2, halo) + w[1:2] * _shift_down(y, 1, halo) + w[2:3] * y + cb_ref[:, cols]
            up_ref[:, cols] = up.astype(BF16)
            halo_ref[:, cols] = y[tm - SUBLANES:tm]

    return pl.pallas_call(
        body, name="ffn_up_fwd", grid=(s // tm,),
        out_shape=[jax.ShapeDtypeStruct((s, D_MODEL), BF16), jax.ShapeDtypeStruct((s, n), BF16),
                   jax.ShapeDtypeStruct((s, n), BF16)],
        in_specs=[_rows(tm, D_MODEL), _resident((1, D_MODEL)), _resident((D_MODEL, n)), _resident((3, n)),
                  _resident((1, n))],
        out_specs=[_rows(tm, D_MODEL), _rows(tm, n), _rows(tm, n)],
        scratch_shapes=[pltpu.VMEM((SUBLANES, n), F32)],
        compiler_params=_params(1))(x1, g, w_up, conv_w, conv_b)


def _ffn_act_fwd(x1, up, target, w_down, g_final):
    s = x1.shape[0]
    tm = ROW_TILE

    def body(x1_ref, up_ref, tgt_ref, wd_ref, gf_ref, act_ref, dx2_ref, dx2b_ref, dgf_ref, loss_ref):
        @pl.when(pl.program_id(0) == 0)
        def _():
            dgf_ref[...] = jnp.zeros_like(dgf_ref)
            loss_ref[...] = jnp.zeros_like(loss_ref)

        acc = jnp.zeros((tm, D_MODEL), F32)
        for lo, size in _col_chunks(D_FF, FFN_CHUNK):
            gate = up_ref[:, lo:lo + size].astype(F32)
            val = up_ref[:, D_FF + lo:D_FF + lo + size].astype(F32)
            act = (_silu(gate) * val).astype(BF16)
            act_ref[:, lo:lo + size] = act
            acc = acc + _dot(act, wd_ref[lo:lo + size, :])

        x2 = x1_ref[...] + acc
        r = lax.rsqrt(jnp.mean(x2 * x2, axis=-1, keepdims=True) + EPS)
        xn = x2 * r
        gf = gf_ref[...]
        err = xn * gf - tgt_ref[...]
        loss_ref[...] += (0.5 / D_MODEL) * jnp.sum(err * err)
        dy = err * (1.0 / D_MODEL)
        dgf_ref[...] += _colsum(dy * xn)
        dxn = dy * gf
        dx2 = r * (dxn - xn * jnp.mean(dxn * xn, axis=-1, keepdims=True))
        dx2_ref[...] = dx2
        dx2b_ref[...] = dx2.astype(BF16)

    return pl.pallas_call(
        body, name="ffn_act_fwd", grid=(s // tm,),
        out_shape=[jax.ShapeDtypeStruct((s, D_FF), BF16), jax.ShapeDtypeStruct((s, D_MODEL), F32),
                   jax.ShapeDtypeStruct((s, D_MODEL), BF16),
                   jax.ShapeDtypeStruct((1, D_MODEL), F32), jax.ShapeDtypeStruct((1, LANES), F32)],
        in_specs=[_rows(tm, D_MODEL), _rows(tm, 2 * D_FF), _rows(tm, D_MODEL),
                  _resident((D_FF, D_MODEL)), _resident((1, D_MODEL))],
        out_specs=[_rows(tm, D_FF), _rows(tm, D_MODEL), _rows(tm, D_MODEL),
                   pl.BlockSpec((1, D_MODEL), lambda i: (0, 0)), pl.BlockSpec((1, LANES), lambda i: (0, 0))],
        compiler_params=_params(1))(x1, up, target, w_down, g_final)


def _ffn_act_bwd(dx2b, up, w_down):
    s = dx2b.shape[0]
    tm = ROW_TILE

    def body(dx2_ref, up_ref, wd_ref, dup_ref):
        dx2 = dx2_ref[...]
        for lo, size in _col_chunks(D_FF, FFN_CHUNK):
            gate = up_ref[:, lo:lo + size].astype(F32)
            val = up_ref[:, D_FF + lo:D_FF + lo + size].astype(F32)
            dact = _dot_nt(dx2, wd_ref[lo:lo + size, :])
            sg = _sigmoid(gate)
            dup_ref[:, lo:lo + size] = (dact * val * (sg * (1.0 + gate * (1.0 - sg)))).astype(BF16)
            dup_ref[:, D_FF + lo:D_FF + lo + size] = (dact * (gate * sg)).astype(BF16)

    return pl.pallas_call(
        body, name="ffn_act_bwd", grid=(s // tm,),
        out_shape=jax.ShapeDtypeStruct((s, 2 * D_FF), BF16),
        in_specs=[_rows(tm, D_MODEL), _rows(tm, 2 * D_FF), _resident((D_FF, D_MODEL))],
        out_specs=_rows(tm, 2 * D_FF),
        compiler_params=_params(1))(dx2b, up, w_down)


def _ffn_up_bwd(d_up, up0, w_up, conv_w, x1, g, dres, swap=()):
    s = x1.shape[0]
    n = w_up.shape[1]
    tm = FFN_UP_ROW_TILE
    nt = s // tm
    nw = len(swap)

    def body(*refs):
        dup_ref, up0_ref, w_ref, cw_ref, x_ref, g_ref, dres_ref = refs[:7]
        slab_refs = refs[7:7 + nw]
        dx_ref, dxb_ref, dg_ref, dup0_ref, small_ref = refs[7 + nw:12 + nw]
        swapped_refs = refs[12 + nw:12 + 2 * nw]
        next_ref = refs[12 + 2 * nw]
        sems = refs[13 + 2 * nw:]

        @pl.when(pl.program_id(0) == 0)
        def _():
            next_ref[...] = jnp.zeros_like(next_ref)
            small_ref[...] = jnp.zeros_like(small_ref)
            dg_ref[...] = jnp.zeros_like(dg_ref)
            for cp in _sibling_swap_copies(slab_refs, swapped_refs, *sems) if nw else ():
                cp.start()

        dh = jnp.zeros((tm, D_MODEL), F32)
        for lo, size in _col_chunks(n, FFN_CHUNK):
            cols = slice(lo, lo + size)
            dz = dup_ref[:, cols].astype(F32)
            x0 = up0_ref[:, cols].astype(F32)
            nxt = next_ref[:, cols]
            dz1 = _shift_up(dz, 1, nxt)
            dz2 = _shift_up(dz, 2, nxt)
            next_ref[:, cols] = dz[0:SUBLANES]
            small_ref[0:1, cols] += _colsum(dz2 * x0)
            small_ref[1:2, cols] += _colsum(dz1 * x0)
            small_ref[2:3, cols] += _colsum(dz * x0)
            small_ref[3:4, cols] += _colsum(dz)
            w = cw_ref[:, cols]
            dup0 = (w[2:3] * dz + w[1:2] * dz1 + w[0:1] * dz2).astype(BF16)
            dup0_ref[:, cols] = dup0
            dh = dh + _dot_nt(dup0, w_ref[:, cols])
        xv = x_ref[...]
        r = lax.rsqrt(jnp.mean(xv * xv, axis=-1, keepdims=True) + EPS)
        xn = xv * r
        dg_ref[...] += _colsum(dh * xn)
        dxn = dh * g_ref[...]
        dx = dres_ref[...] + r * (dxn - xn * jnp.mean(dxn * xn, axis=-1, keepdims=True))
        dx_ref[...] = dx
        dxb_ref[...] = dx.astype(BF16)

        if nw:
            @pl.when(pl.program_id(0) == nt - 1)
            def _():
                for cp in _sibling_swap_copies(slab_refs, swapped_refs, *sems):
                    cp.wait()

    rows = lambda width: pl.BlockSpec((tm, width), lambda i: (nt - 1 - i, 0))
    return pl.pallas_call(
        body, name="ffn_up_bwd", grid=(nt,),
        out_shape=[jax.ShapeDtypeStruct((s, D_MODEL), F32), jax.ShapeDtypeStruct((s, D_MODEL), BF16),
                   jax.ShapeDtypeStruct((1, D_MODEL), F32), jax.ShapeDtypeStruct((s, n), BF16),
                   jax.ShapeDtypeStruct((SUBLANES, n), F32)] + _swap_shapes(swap),
        in_specs=[rows(n), rows(n), _resident((D_MODEL, n)), _resident((3, n)), rows(D_MODEL),
                  _resident((1, D_MODEL)), rows(D_MODEL)] + [_ANY] * nw,
        out_specs=[rows(D_MODEL), rows(D_MODEL), pl.BlockSpec((1, D_MODEL), lambda i: (0, 0)), rows(n),
                   pl.BlockSpec((SUBLANES, n), lambda i: (0, 0))] + [_ANY] * nw,
        scratch_shapes=[pltpu.VMEM((SUBLANES, n), F32)] + (_dma_sems(nw) if nw else []),
        compiler_params=_params(1))(d_up, up0, w_up, conv_w, x1, g, dres, *swap)


def _inproj_bwd(d_abcv, d_gates, d_qkvs, w_in_t, x, g, dres):
    s = x.shape[0]
    tm = ROW_TILE
    gw = GROUP_WIDTH

    def body(dabcv_ref, dgates_ref, dq0_ref, dq1_ref, dq2_ref, w_ref, x_ref, g_ref, dres_ref, dx_ref, dg_ref, scr):
        @pl.when(pl.program_id(0) == 0)
        def _():
            dg_ref[...] = jnp.zeros_like(dg_ref)

        dh = jnp.zeros((tm, D_MODEL), F32)
        for src, width, wrow in ((dabcv_ref, 3 * CONV_WIDTH, COL_ABCV), (dgates_ref, 2 * D_MODEL, COL_GATES)):
            for lo, size in _col_chunks(width, 512):
                dh = dh + _dot(src[:, lo:lo + size], w_ref[wrow + lo:wrow + lo + size, :])
        for gi, (d, dq_ref) in enumerate(zip(DILATIONS, (dq0_ref, dq1_ref, dq2_ref))):
            for j, base in enumerate((COL_Q, COL_K, COL_V)):
                dy = _from_streams(dq_ref, scr, d, j * gw, gw).astype(BF16)
                wrow = base + gi * gw
                dh = dh + _dot(dy, w_ref[wrow:wrow + gw, :])
        xv = x_ref[...]
        r = lax.rsqrt(jnp.mean(xv * xv, axis=-1, keepdims=True) + EPS)
        xn = xv * r
        dg_ref[...] += _colsum(dh * xn)
        dxn = dh * g_ref[...]
        dx_ref[...] = dres_ref[...] + r * (dxn - xn * jnp.mean(dxn * xn, axis=-1, keepdims=True))

    return pl.pallas_call(
        body, name="inproj_bwd", grid=(s // tm,),
        out_shape=[jax.ShapeDtypeStruct((s, D_MODEL), F32), jax.ShapeDtypeStruct((1, D_MODEL), F32)],
        in_specs=[_rows(tm, 3 * CONV_WIDTH), _rows(tm, 2 * D_MODEL)]
        + [_stream_block(tm, d, 3 * gw) for d in DILATIONS]
        + [_resident((D_IN, D_MODEL)), _rows(tm, D_MODEL), _resident((1, D_MODEL)), _rows(tm, D_MODEL)],
        out_specs=[_rows(tm, D_MODEL), pl.BlockSpec((1, D_MODEL), lambda i: (0, 0))],
        scratch_shapes=[pltpu.VMEM((gw // LANES * tm, LANES), F32)],
        compiler_params=_params(1))(d_abcv, d_gates, *d_qkvs, w_in_t, x, g, dres)


def _mix_bwd(dx1, abcv, gates, ya, yb, yb0, lsetot, conv_w, conv_b, b_gate, w_pa, w_pb, w_out, exchange=()):
    s = dx1.shape[0]
    tm = ROW_TILE
    nt = s // tm
    hb = tm // (2 * SUBLANES)
    nx = len(exchange)

    def body(*refs):
        (dx1_ref, abcv_ref, pre_ref, gates_ref, ya_ref, yb_ref, yb0_ref, lsetot_ref,
         cw_ref, cb_ref, bg_ref, wpa_ref, wpb_ref, wout_ref) = refs[:14]
        part_refs = refs[14:14 + nx]
        (dya_ref, dyb_ref, dgates_ref, dabcv_ref, dyb0_ref, dyl0_ref, dyl1_ref, dyl2_ref, aux0_ref, aux1_ref,
         aux2_ref, sm_gate_ref, sm_conv_ref) = refs[14 + nx:27 + nx]
        recv_refs = refs[27 + nx:27 + 2 * nx]
        next_ref, scr = refs[27 + 2 * nx:29 + 2 * nx]
        sems = refs[29 + 2 * nx:]
        i = pl.program_id(0)

        @pl.when(i == 0)
        def _():
            next_ref[...] = jnp.zeros_like(next_ref)
            sm_gate_ref[...] = jnp.zeros_like(sm_gate_ref)
            sm_conv_ref[...] = jnp.zeros_like(sm_conv_ref)
            for cp in _chip_exchange_copies(part_refs, recv_refs, *sems) if nx else ():
                cp.start()

        not_first = (i < nt - 1).astype(F32)
        dm = _dot_nt(dx1_ref[...].astype(BF16), wout_ref[...])
        sa = _sigmoid(gates_ref[:, 0:D_MODEL].astype(F32) + bg_ref[0:1, :])
        sb = _sigmoid(gates_ref[:, D_MODEL:2 * D_MODEL].astype(F32) + bg_ref[1:2, :])
        dya = (dm * sa).astype(BF16)
        dyb = (dm * sb).astype(BF16)
        dya_ref[...] = dya
        dyb_ref[...] = dyb
        dga = dm * ya_ref[...].astype(F32) * (sa * (1.0 - sa))
        dgb = dm * yb_ref[...].astype(F32) * (sb * (1.0 - sb))
        dgates_ref[:, 0:D_MODEL] = dga.astype(BF16)
        dgates_ref[:, D_MODEL:2 * D_MODEL] = dgb.astype(BF16)
        sm_gate_ref[0:1, :] += _colsum(dga)
        sm_gate_ref[1:2, :] += _colsum(dgb)

        dya0 = _dot_nt(dya, wpa_ref[...])
        ab = abcv_ref[:, 0:CONV_WIDTH].astype(F32)
        ac = abcv_ref[:, CONV_WIDTH:2 * CONV_WIDTH].astype(F32)
        av = abcv_ref[:, 2 * CONV_WIDTH:3 * CONV_WIDTH].astype(F32)
        pre = pre_ref[...].astype(F32) * not_first
        halo_u = (pre[:, CONV_WIDTH:2 * CONV_WIDTH] * pre[:, 2 * CONV_WIDTH:3 * CONV_WIDTH])[SUBLANES:2 * SUBLANES]
        w = cw_ref[...]
        _, cv, u, sh1, sh2 = _conv_branch(ab, ac, av, halo_u, w, cb_ref[...])
        dcv = dya0 * ab
        sm_conv_ref[0:1, :] += _colsum(dcv * sh2)
        sm_conv_ref[1:2, :] += _colsum(dcv * sh1)
        sm_conv_ref[2:3, :] += _colsum(dcv * u)
        sm_conv_ref[3:4, :] += _colsum(dcv)
        nxt = next_ref[...]
        du = w[2:3] * dcv + w[1:2] * _shift_up(dcv, 1, nxt) + w[0:1] * _shift_up(dcv, 2, nxt)
        next_ref[...] = dcv[0:SUBLANES]
        dabcv_ref[:, 0:CONV_WIDTH] = (dya0 * cv).astype(BF16)
        dabcv_ref[:, CONV_WIDTH:2 * CONV_WIDTH] = (du * av).astype(BF16)
        dabcv_ref[:, 2 * CONV_WIDTH:3 * CONV_WIDTH] = (du * ac).astype(BF16)

        head_r = lax.broadcasted_iota(jnp.int32, (GROUP_WIDTH, GROUP_WIDTH), 0) // HEAD_DIM
        head_c = lax.broadcasted_iota(jnp.int32, (GROUP_WIDTH, GROUP_WIDTH), 1) // HEAD_DIM
        same_head = (head_r == head_c).astype(BF16)
        prod = jnp.zeros((tm, GROUP_WIDTH), F32)
        dyb0s = []
        for g in range(3):
            cols = slice(g * GROUP_WIDTH, (g + 1) * GROUP_WIDTH)
            dyb0 = _dot_nt(dyb, wpb_ref[cols, :])
            dyb0_ref[:, cols] = dyb0.astype(BF16)
            dyb0s.append(dyb0)
            prod = prod + dyb0 * yb0_ref[:, cols].astype(F32)
        hi = prod.astype(BF16)
        mid = (prod - hi.astype(F32)).astype(BF16)
        lo = (prod - hi.astype(F32) - mid.astype(F32)).astype(BF16)
        delta = _dot(hi, same_head) + _dot(mid, same_head) + _dot(lo, same_head)
        lse_c = _compact_heads(lsetot_ref[...])
        delta_c = _compact_heads(delta)
        for g, (dy_ref, aux_ref) in enumerate(zip((dyl0_ref, dyl1_ref, dyl2_ref), (aux0_ref, aux1_ref, aux2_ref))):
            d = DILATIONS[g]
            _to_streams(dyb0s[g], scr, dy_ref, d, 0)
            _to_streams(lse_c, scr, aux_ref, d, 0)
            _to_streams(delta_c, scr, aux_ref, d, LANES)

        if nx:
            @pl.when(i == nt - 1)
            def _():
                for cp in _chip_exchange_copies(part_refs, recv_refs, *sems):
                    cp.wait()

    rev = lambda i: (nt - 1 - i, 0)
    pre = lambda i: (jnp.maximum((nt - 1 - i) * hb - 1, 0), 0)
    rows = lambda width: pl.BlockSpec((tm, width), rev)
    outs = [jax.ShapeDtypeStruct((s, D_MODEL), BF16), jax.ShapeDtypeStruct((s, D_MODEL), BF16),
            jax.ShapeDtypeStruct((s, 2 * D_MODEL), BF16), jax.ShapeDtypeStruct((s, 3 * CONV_WIDTH), BF16),
            jax.ShapeDtypeStruct((s, ATTN_WIDTH), BF16)]
    outs += [jax.ShapeDtypeStruct((d, s // d, GROUP_WIDTH), BF16) for d in DILATIONS]
    outs += [jax.ShapeDtypeStruct((d, s // d, 2 * LANES), F32) for d in DILATIONS]
    outs += [jax.ShapeDtypeStruct((SUBLANES, D_MODEL), F32), jax.ShapeDtypeStruct((SUBLANES, CONV_WIDTH), F32)]
    outs += [jax.ShapeDtypeStruct((3,) + a.shape[1:], a.dtype) for a in exchange]
    return pl.pallas_call(
        body, name="mix_bwd", grid=(nt,), out_shape=outs,
        in_specs=[rows(D_MODEL), rows(3 * CONV_WIDTH), pl.BlockSpec((2 * SUBLANES, 3 * CONV_WIDTH), pre),
                  rows(2 * D_MODEL), rows(D_MODEL), rows(D_MODEL), rows(ATTN_WIDTH), rows(GROUP_WIDTH),
                  _resident((3, CONV_WIDTH)), _resident((1, CONV_WIDTH)), _resident((2, D_MODEL)),
                  _resident((CONV_WIDTH, D_MODEL)), _resident((ATTN_WIDTH, D_MODEL)), _resident((D_MODEL, D_MODEL))]
        + [_ANY] * nx,
        out_specs=[rows(D_MODEL), rows(D_MODEL), rows(2 * D_MODEL), rows(3 * CONV_WIDTH), rows(ATTN_WIDTH)]
        + [_rev_stream_block(tm, d, GROUP_WIDTH, nt) for d in DILATIONS]
        + [_rev_stream_block(tm, d, 2 * LANES, nt) for d in DILATIONS]
        + [pl.BlockSpec((SUBLANES, D_MODEL), lambda i: (0, 0)), pl.BlockSpec((SUBLANES, CONV_WIDTH), lambda i: (0, 0))]
        + [_ANY] * nx,
        scratch_shapes=[pltpu.VMEM((SUBLANES, CONV_WIDTH), F32),
                        pltpu.VMEM((GROUP_WIDTH // LANES * tm, LANES), F32)] + (_dma_sems(3 * nx) if nx else []),
        compiler_params=_params(1))(dx1, abcv, abcv, gates, ya, yb, yb0, lsetot,
                                    conv_w, conv_b, b_gate, w_pa, w_pb, w_out, *exchange)


def _attn_bwd(qkv, dy, aux, gi, exchange=(), swap=()):
    d, length, _ = qkv.shape
    nb = length // ATTN_BLOCK
    q = ATTN_BLOCK
    gw = GROUP_WIDTH
    g = min(ATTN_BLOCKS_PER_STEP, nb)
    assert g >= 2 and nb % g == 0
    ns = nb // g
    lag = 1 if ns > 1 else 0
    tail = (g - 1) * q
    nx, nw = len(exchange), len(swap)

    def body(*refs):
        q_ref, kp_ref, kc_ref, vp_ref, vc_ref, dy_ref, aux_ref = refs[:7]
        part_refs = refs[7:7 + nx]
        slab_refs = refs[7 + nx:7 + nx + nw]
        pos = 7 + nx + nw
        out_ref = refs[pos]
        recv_refs = refs[pos + 1:pos + 1 + nx]
        swapped_refs = refs[pos + 1 + nx:pos + 1 + nx + nw]
        pos += 1 + nx + nw
        dq_ref, dkv_ref, bias_ref = refs[pos:pos + 3]
        sems = refs[pos + 3:]
        n = pl.program_id(1)

        def copies():
            cps = _chip_exchange_copies(part_refs, recv_refs, sems[0], sems[1]) if nx else []
            return cps + (_sibling_swap_copies(slab_refs, swapped_refs, sems[-2], sems[-1]) if nw else [])

        @pl.when((pl.program_id(0) == 0) & (n == 0))
        def _():
            _store_band_biases(bias_ref)
            for cp in copies():
                cp.start()

        if nx or nw:
            @pl.when((pl.program_id(0) == d - 1) & (n == ns - 1 + lag))
            def _():
                for cp in copies():
                    cp.wait()

        def emit(rows):
            out_ref[rows, gw:2 * gw] = dkv_ref[0, rows].astype(BF16)
            out_ref[rows, 2 * gw:3 * gw] = dkv_ref[1, rows].astype(BF16)

        if lag:
            @pl.when(n > 0)
            def _():
                out_ref[:, 0:gw] = dq_ref[...].astype(BF16)
                emit(slice(0, tail))

            @pl.when(n == ns)
            def _():
                emit(slice(tail, g * q))

        @pl.when(n < ns)
        def _():
            kfull = jnp.concatenate([kp_ref[...], kc_ref[...]], axis=0)
            vfull = jnp.concatenate([vp_ref[...], vc_ref[...]], axis=0)
            for j in range(g):
                rows = slice(j * q, (j + 1) * q)
                qs = _stack_heads(q_ref[rows, :])
                dys = _stack_heads(dy_ref[rows, :])
                k2 = kfull[j * q:(j + 2) * q]
                v2 = vfull[j * q:(j + 2) * q]
                lse = _compact_head_col(aux_ref[rows, 0:LANES])
                delta = _compact_head_col(aux_ref[rows, LANES:2 * LANES])
                bias = jnp.where(n == 0, bias_ref[1], bias_ref[0]) if j == 0 else bias_ref[0]
                p = jnp.exp(_dot_nt(qs, k2) + bias - lse)
                dp = _dot_nt(dys, v2)
                ds = (p * (dp - delta)).astype(BF16)
                dq_j = _unstack_heads(_dot(ds, k2)) * ATTN_SCALE
                dk2 = _dot_tn(ds, qs)
                dv2 = _dot_tn(p.astype(BF16), dys)
                if j == 0:
                    @pl.when(n > 0)
                    def _():
                        out_ref[tail:g * q, gw:2 * gw] = (dkv_ref[0, tail:g * q] + dk2[0:q]).astype(BF16)
                        out_ref[tail:g * q, 2 * gw:3 * gw] = (dkv_ref[1, tail:g * q] + dv2[0:q]).astype(BF16)
                else:
                    dkv_ref[0, (j - 1) * q:j * q] += dk2[0:q]
                    dkv_ref[1, (j - 1) * q:j * q] += dv2[0:q]
                dkv_ref[0, rows] = dk2[q:2 * q]
                dkv_ref[1, rows] = dv2[q:2 * q]
                dq_ref[rows, :] = dq_j
            if not lag:
                out_ref[:, 0:gw] = dq_ref[...].astype(BF16)
                emit(slice(0, g * q))

    cur, prev = _attn_block_specs(g, nb, clamp_last=True)
    return pl.pallas_call(
        body, name=f"attn_bwd_g{gi}", grid=(d, ns + lag),
        out_shape=[jax.ShapeDtypeStruct((d, length, 3 * gw), BF16)]
        + [jax.ShapeDtypeStruct((3,) + a.shape[1:], a.dtype) for a in exchange] + _swap_shapes(swap),
        in_specs=[cur(0), prev(1), cur(1), prev(2), cur(2), cur(0), cur(0, 2 * LANES)] + [_ANY] * (nx + nw),
        out_specs=[pl.BlockSpec((None, g * q, 3 * gw), lambda r, n: (r, jnp.maximum(n - lag, 0), 0))]
        + [_ANY] * (nx + nw),
        scratch_shapes=[pltpu.VMEM((g * q, gw), F32), pltpu.VMEM((2, g * q, gw), F32),
                        pltpu.VMEM((2, HEADS_PER_GROUP * q, 2 * q), F32)]
        + (_dma_sems(3 * nx) if nx else []) + (_dma_sems(nw) if nw else []),
        compiler_params=_params(2))(qkv, qkv, qkv, qkv, qkv, dy, aux, *exchange, *swap)


def _matmul_tn(name, a, b, col_tile=1024, row_tile=2048, slabs=0, swap=()):
    s, k = a.shape
    n = b.shape[1]
    tk = min(row_tile, s)
    tn = col_tile
    steps = s // tk
    nw = len(swap)

    def body(*refs):
        a_ref, b_ref = refs[:2]
        slab_refs = refs[2:2 + nw]
        o_ref = refs[2 + nw]
        swapped_refs = refs[3 + nw:3 + 2 * nw]
        acc_ref = refs[3 + 2 * nw]
        sems = refs[4 + 2 * nw:]
        t = pl.program_id(1)

        if nw:
            @pl.when((pl.program_id(0) == 0) & (t == 0))
            def _():
                for cp in _sibling_swap_copies(slab_refs, swapped_refs, *sems):
                    cp.start()

            @pl.when((pl.program_id(0) == n // tn - 1) & (t == steps - 1))
            def _():
                for cp in _sibling_swap_copies(slab_refs, swapped_refs, *sems):
                    cp.wait()

        @pl.when(t == 0)
        def _():
            acc_ref[...] = jnp.zeros_like(acc_ref)

        acc_ref[...] += _dot_tn(a_ref[...], b_ref[...])

        @pl.when(t == steps - 1)
        def _():
            if slabs:
                for q in range(per_tile):
                    o_ref[q] = acc_ref[:, q * width:(q + 1) * width].astype(BF16)
            else:
                o_ref[...] = acc_ref[...].astype(BF16)

    if slabs:
        width = n // slabs
        per_tile = tn // width
        out_shape = jax.ShapeDtypeStruct((slabs, k, width), BF16)
        out_spec = pl.BlockSpec((per_tile, k, width), lambda j, t: (j, 0, 0))
    else:
        out_shape = jax.ShapeDtypeStruct((k, n), BF16)
        out_spec = pl.BlockSpec((k, tn), lambda j, t: (0, j))
    res = pl.pallas_call(
        body, name=name, grid=(n // tn, steps), out_shape=[out_shape] + _swap_shapes(swap),
        in_specs=[pl.BlockSpec((tk, k), lambda j, t: (t, 0)), pl.BlockSpec((tk, tn), lambda j, t: (t, j))] + [_ANY] * nw,
        out_specs=[out_spec] + [_ANY] * nw,
        scratch_shapes=[pltpu.VMEM((k, tn), F32)] + (_dma_sems(nw) if nw else []),
        compiler_params=_params(2))(a, b, *swap)
    return res if nw else res[0]


def _sibling_swap_halves(name, slabs):
    na = len(slabs)

    def body(*refs):
        src_refs, out_refs = refs[:na], refs[na:2 * na]
        send_sems, recv_sems = refs[2 * na:]
        x, y, c, _ = _mesh_position()
        cps = []
        for a in range(na):
            theirs = _half_rows(1 - c, src_refs[a].shape[1] // 2)
            cps.append(pltpu.make_async_remote_copy(
                src_ref=src_refs[a].at[:, theirs, :], dst_ref=out_refs[a], send_sem=send_sems.at[a],
                recv_sem=recv_sems.at[a], device_id=(x, y, 1 - c), device_id_type=MESH_ID))
        for cp in cps:
            cp.start()
        for cp in cps:
            cp.wait()

    return pl.pallas_call(
        body, name=name,
        out_shape=[jax.ShapeDtypeStruct((a.shape[0], a.shape[1] // 2, a.shape[2]), a.dtype) for a in slabs],
        in_specs=[_ANY] * na, out_specs=[_ANY] * na,
        scratch_shapes=[pltpu.SemaphoreType.DMA((na,)), pltpu.SemaphoreType.DMA((na,))])(*slabs)


_HBM = pl.BlockSpec(memory_space=pltpu.HBM)
_SEM = pl.BlockSpec(memory_space=pltpu.SEMAPHORE)
_DATAFLOW = pltpu.SideEffectType.DATAFLOW_SIDE_EFFECTING


def _gather_start(shard):
    gathered = jax.ShapeDtypeStruct((N_CHIPS,) + shard.shape[1:], shard.dtype)

    def body(src_ref, buf_ref, send_sems, recv_sems, src_thru, buf_thru, token):
        for cp in _gather_first_copies([src_ref], [], [buf_ref], [], send_sems, recv_sems):
            cp.start()
        token[...] = jnp.zeros_like(token)

    return pl.pallas_call(
        body, name="gather_start",
        out_shape=(pltpu.SemaphoreType.DMA((3,)), pltpu.SemaphoreType.DMA((3,)),
                   pltpu.HBM(shard.shape, shard.dtype), pltpu.HBM(gathered.shape, gathered.dtype),
                   jax.ShapeDtypeStruct((SUBLANES, LANES), F32)),
        in_specs=(_HBM, _HBM), out_specs=(_SEM, _SEM, _HBM, _HBM, _VMEM), input_output_aliases={0: 2, 1: 3},
        compiler_params=pltpu.CompilerParams(has_side_effects=_DATAFLOW),
    )(pltpu.with_memory_space_constraint(shard, pltpu.HBM),
      pltpu.with_memory_space_constraint(lax.empty(gathered.shape, gathered.dtype), pltpu.HBM))


def _gather_forward(send_sems, recv_sems, shard_thru, buf_thru, after):
    def body(src_ref, buf_ref, send_sems, recv_sems, after_ref, fwd_send, fwd_recv, buf_out):
        first = _gather_first_copies([src_ref], [], [buf_ref], [], send_sems, recv_sems)
        for cp, fwd in zip(first, _gather_forward_copies([buf_ref], fwd_send, fwd_recv)):
            cp.wait_send()
            cp.wait_recv()
            fwd.start()

    return pl.pallas_call(
        body, name="gather_forward",
        out_shape=(pltpu.SemaphoreType.DMA((3,)), pltpu.SemaphoreType.DMA((3,)),
                   pltpu.HBM(buf_thru.shape, buf_thru.dtype)),
        in_specs=(_HBM, _HBM, _SEM, _SEM, _ANY), out_specs=(_SEM, _SEM, _HBM), input_output_aliases={1: 2},
        compiler_params=pltpu.CompilerParams(has_side_effects=_DATAFLOW),
    )(shard_thru, buf_thru, send_sems, recv_sems, after)


def _gather_wait(fwd_send, fwd_recv, buf_thru):
    def body(buf_ref, fwd_send, fwd_recv, buf_out):
        for cp in _gather_forward_copies([buf_ref], fwd_send, fwd_recv):
            cp.wait_send()
            cp.wait_recv()

    return pl.pallas_call(
        body, name="gather_wait", out_shape=pltpu.HBM(buf_thru.shape, buf_thru.dtype),
        in_specs=(_HBM, _SEM, _SEM), out_specs=_HBM, input_output_aliases={0: 0},
        compiler_params=pltpu.CompilerParams(has_side_effects=_DATAFLOW),
    )(buf_thru, fwd_send, fwd_recv)


def _chip_exchange_start(partial):
    _, rows, cols = partial.shape
    landing = jax.ShapeDtypeStruct((3, rows, cols), partial.dtype)

    def body(src_ref, land_ref, send_sems, recv_sems, src_thru, land_thru, token):
        for cp in _chip_exchange_copies([src_ref], [land_ref], send_sems, recv_sems):
            cp.start()
        token[...] = jnp.zeros_like(token)

    return pl.pallas_call(
        body, name="grad_exchange_start",
        out_shape=(pltpu.SemaphoreType.DMA((3,)), pltpu.SemaphoreType.DMA((3,)),
                   pltpu.HBM(partial.shape, partial.dtype), pltpu.HBM(landing.shape, landing.dtype),
                   jax.ShapeDtypeStruct((SUBLANES, LANES), F32)),
        in_specs=(_HBM, _HBM), out_specs=(_SEM, _SEM, _HBM, _HBM, _VMEM), input_output_aliases={0: 2, 1: 3},
        compiler_params=pltpu.CompilerParams(has_side_effects=_DATAFLOW),
    )(pltpu.with_memory_space_constraint(partial, pltpu.HBM),
      pltpu.with_memory_space_constraint(lax.empty(landing.shape, landing.dtype), pltpu.HBM))


def _chip_exchange_wait(send_sems, recv_sems, src_thru, land_thru, after):
    def body(src_ref, land_ref, send_sems, recv_sems, after_ref, src_out, land_out):
        for cp in _chip_exchange_copies([src_ref], [land_ref], send_sems, recv_sems):
            cp.wait_send()
            cp.wait_recv()

    return pl.pallas_call(
        body, name="grad_exchange_wait",
        out_shape=(pltpu.HBM(src_thru.shape, src_thru.dtype), pltpu.HBM(land_thru.shape, land_thru.dtype)),
        in_specs=(_HBM, _HBM, _SEM, _SEM, _ANY), out_specs=(_HBM, _HBM), input_output_aliases={0: 0, 1: 1},
        compiler_params=pltpu.CompilerParams(has_side_effects=_DATAFLOW),
    )(src_thru, land_thru, send_sems, recv_sems, after)


def _sibling_share_copies(refs, send_sems, recv_sems):
    x, y, c, _ = _mesh_position()
    cps = []
    for a, ref in enumerate(refs):
        mine = ref.at[0, _half_rows(c, ref.shape[1] // 2)]
        cps.append(_remote_copy(a, mine, mine, (x, y, 1 - c), send_sems, recv_sems))
    return cps


def _add_sibling(name, slabs, received, core):
    na = len(slabs)
    halves = [a.shape[1] // 2 for a in slabs]

    def body(core_ref, *refs):
        for a in range(na):
            refs[2 * na + a][...] = (refs[a][...].astype(F32) + refs[na + a][...].astype(F32)).astype(BF16)

    def block(a, mine):
        if mine:
            return pl.BlockSpec((None, halves[a], slabs[a].shape[2]), lambda s, core_ref: (s, core_ref[0], 0))
        return pl.BlockSpec((None, halves[a], slabs[a].shape[2]), lambda s, core_ref: (s, 0, 0))

    grid_spec = pltpu.PrefetchScalarGridSpec(
        num_scalar_prefetch=1, grid=(N_CHIPS,),
        in_specs=[block(a, True) for a in range(na)] + [block(a, False) for a in range(na)],
        out_specs=[block(a, False) for a in range(na)])
    return pl.pallas_call(body, name=name, grid_spec=grid_spec,
                          out_shape=[jax.ShapeDtypeStruct(r.shape, BF16) for r in received],
                          compiler_params=_params(1))(core, *slabs, *received)


def _sum_chips(name, partials, received, chip_core):
    na = len(partials)

    def body(cc_ref, *refs):
        for a in range(na):
            acc = refs[a][...].astype(F32)
            for k in range(3):
                acc = acc + refs[na + a][k].astype(F32)
            refs[2 * na + a][...] = acc

    def own(p):
        return pl.BlockSpec((None,) + p.shape[1:], lambda i, cc_ref: (cc_ref[0], 0, 0))

    def mine(p):
        return pl.BlockSpec((None,) + p.shape[1:], lambda i, cc_ref: (0, cc_ref[1], 0))

    grid_spec = pltpu.PrefetchScalarGridSpec(
        num_scalar_prefetch=1, grid=(1,),
        in_specs=[own(p) for p in partials] + [pl.BlockSpec(r.shape, lambda i, cc_ref: (0, 0, 0)) for r in received],
        out_specs=[mine(p) for p in partials])
    return pl.pallas_call(body, name=name, grid_spec=grid_spec,
                          out_shape=[jax.ShapeDtypeStruct((1, 2 * p.shape[1], p.shape[2]), F32) for p in partials],
                          compiler_params=_params(1))(chip_core, *partials, *received)


def _adam_math(w, g, m, v):
    nm = ADAM_B1 * m + (1.0 - ADAM_B1) * g
    nv = ADAM_B2 * v + (1.0 - ADAM_B2) * jnp.square(g)
    m_hat = nm / (1.0 - ADAM_B1 ** ADAM_STEP)
    v_hat = nv / (1.0 - ADAM_B2 ** ADAM_STEP)
    delta = -ADAM_LR * (m_hat / (jnp.sqrt(v_hat) + ADAM_EPS) + ADAM_WD * w)
    return delta, nm, nv


ADAMW_STEPS = 8


def _adamw(ws, gs, ms, vs):
    na = len(ws)

    def body(*refs):
        for a in range(na):
            w_ref, g_ref, m_ref, v_ref = (refs[k * na + a] for k in range(4))
            g_out_ref, d_ref, nm_ref, nv_ref = (refs[(4 + k) * na + a] for k in range(4))
            gv = g_ref[...]
            g_out_ref[...] = gv
            d_ref[...], nm_ref[...], nv_ref[...] = _adam_math(w_ref[...], gv, m_ref[...], v_ref[...])

    specs = [pl.BlockSpec((None, w.shape[1] // ADAMW_STEPS, w.shape[2]), lambda i: (0, i, 0)) for w in ws]
    outs = pl.pallas_call(
        body, name="adamw", grid=(ADAMW_STEPS,), out_shape=[jax.ShapeDtypeStruct(w.shape, F32) for w in ws] * 4,
        in_specs=specs * 4, out_specs=specs * 4, compiler_params=_params(1))(*ws, *gs, *ms, *vs)
    return [[outs[k * na + a] for k in range(4)] for a in range(na)]


SMALL_PARAMS = ("norm_mix_g", "b_gate", "conv_a_w", "conv_a_b", "norm_ffn_g", "ffn_conv_w", "ffn_conv_b", "final_norm_g")


def _small_update(partials, params, moments_m, moments_v, halves):
    na = len(partials)
    npar = len(SMALL_PARAMS)
    nh = len(halves)

    def body(*refs):
        in_refs = refs[:na]
        w_refs = refs[na:na + npar]
        m_refs = refs[na + npar:na + 2 * npar]
        v_refs = refs[na + 2 * npar:na + 3 * npar]
        pos = na + 3 * npar + nh
        loss_ref = refs[pos]
        out_refs = refs[pos + 1:pos + 1 + 4 * npar]
        big_refs = refs[pos + 1 + 4 * npar:pos + 1 + 4 * npar + nh]
        pos += 1 + 4 * npar + nh
        acc_refs = refs[pos:pos + na]
        recv_refs = refs[pos + na:pos + 4 * na]
        send_sems, recv_sems, share_send, share_recv = refs[pos + 4 * na:]
        x, y, c, _ = _mesh_position()
        chip = 2 * x + y
        for cp in _sibling_share_copies(big_refs, share_send, share_recv):
            cp.start()
        for a in range(na):
            acc_refs[a][...] = in_refs[a][...]
        for stage, peer in enumerate(((x, y, 1 - c), (x, 1 - y, c), (1 - x, y, c))):
            cps = []
            for a in range(na):
                k = stage * na + a
                cps.append(pltpu.make_async_remote_copy(src_ref=acc_refs[a], dst_ref=recv_refs[k], send_sem=send_sems.at[k],
                                                        recv_sem=recv_sems.at[k], device_id=peer, device_id_type=MESH_ID))
            for cp in cps:
                cp.start()
            for cp in cps:
                cp.wait()
            for a in range(na):
                acc_refs[a][...] = acc_refs[a][...] + recv_refs[stage * na + a][...]

        mix, ffn, fin, gate, conv, ffnc, loss = acc_refs
        loss_ref[...] = loss[...]

        def cols(width):
            return pl.ds(pl.multiple_of(chip * width, LANES), width)

        grads = {
            "norm_mix_g": mix[...], "norm_ffn_g": ffn[...], "final_norm_g": fin[...],
            "b_gate": gate[0:2, cols(D_MODEL // N_CHIPS)],
            "conv_a_w": conv[0:3, cols(CONV_WIDTH // N_CHIPS)], "conv_a_b": conv[3:4, :],
            "ffn_conv_w": ffnc[0:3, cols(2 * D_FF // N_CHIPS)], "ffn_conv_b": ffnc[3:4, :]}
        for i, name in enumerate(SMALL_PARAMS):
            g = grads[name]
            if len(w_refs[i].shape) == 3:
                results = (g,) + _adam_math(w_refs[i][0], g, m_refs[i][0], v_refs[i][0])
                for o_ref, val in zip(out_refs[4 * i:4 * i + 4], results):
                    o_ref[0] = val
            else:
                results = (g,) + _adam_math(w_refs[i][...], g, m_refs[i][...], v_refs[i][...])
                for o_ref, val in zip(out_refs[4 * i:4 * i + 4], results):
                    o_ref[...] = val

        for cp in _sibling_share_copies(big_refs, share_send, share_recv):
            cp.wait()

    outs = [jax.ShapeDtypeStruct(partials[-1].shape, F32)]
    for w in params:
        outs += [jax.ShapeDtypeStruct(w.shape, F32)] * 4
    n_small_out = len(outs)
    outs += [jax.ShapeDtypeStruct(h.shape, h.dtype) for h in halves]
    scratch = [pltpu.VMEM(p.shape, F32) for p in partials]
    scratch += [pltpu.VMEM(p.shape, F32) for _ in range(3) for p in partials]
    scratch += _dma_sems(3 * na) + _dma_sems(nh)
    n_in = na + 3 * npar
    return pl.pallas_call(
        body, name="small_update", out_shape=outs, in_specs=[_VMEM] * n_in + [_ANY] * nh,
        out_specs=[_VMEM] * n_small_out + [_ANY] * nh,
        input_output_aliases={n_in + a: n_small_out + a for a in range(nh)},
        scratch_shapes=scratch)(*partials, *params, *moments_m, *moments_v, *halves)


def _gathered_columns(g):
    return jnp.transpose(g, (1, 0, 2)).reshape(g.shape[1], N_CHIPS * g.shape[2])


def kernel(x, norm_mix_g, w_in, b_gate, conv_a_w, conv_a_b, w_proj_a, w_proj_b, w_out, norm_ffn_g, w_up, ffn_conv_w, ffn_conv_b, w_down, final_norm_g, loss_target, m_norm_mix_g, m_w_in, m_b_gate, m_conv_a_w, m_conv_a_b, m_w_proj_a, m_w_proj_b, m_w_out, m_norm_ffn_g, m_w_up, m_ffn_conv_w, m_ffn_conv_b, m_w_down, m_final_norm_g, v_norm_mix_g, v_w_in, v_b_gate, v_conv_a_w, v_conv_a_b, v_w_proj_a, v_w_proj_b, v_w_out, v_norm_ffn_g, v_w_up, v_ffn_conv_w, v_ffn_conv_b, v_w_down, v_final_norm_g):
    chip = (2 * lax.axis_index("x") + lax.axis_index("y")).astype(jnp.int32)
    core = lax.axis_index("c").astype(jnp.int32)
    core_arr = core.reshape(1)
    chip_core = jnp.stack([chip, core])
    xs, target = x[0], loss_target[0]
    g_final = final_norm_g.reshape(1, D_MODEL)

    def own_slot(gathered, own):
        return lax.dynamic_update_slice(gathered, own, (chip, 0, 0))

    w_in_t, m_w_in_t, v_w_in_t = (jnp.swapaxes(a, 1, 2) for a in (w_in, m_w_in, v_w_in))
    w_in_tb = w_in_t.astype(BF16)
    send1, recv1, shard_thru, g_in, token = _gather_start(w_in_tb)
    h1, h1_streams4, h1_streams16 = _norm_fwd(xs, norm_mix_g + token[0:1, 0:1])
    send2, recv2, g_in = _gather_forward(send1, recv1, shard_thru, g_in, h1)
    g_in = _gather_wait(send2, recv2, g_in)
    w_in_full_t = own_slot(g_in, w_in_tb).reshape(D_IN, D_MODEL)
    later_w = [w_proj_a, w_proj_b, w_out, w_up, w_down]
    later_b = [w.astype(BF16) for w in later_w]
    small_sharded = [b_gate, conv_a_w, ffn_conv_w]
    fwd = _inproj_fwd(h1, w_in_full_t, later_b, small_sharded)
    abcv, gates, qkv0, qkv1, qkv2 = fwd[:5]
    gathered_big, gathered_small = fwd[5:10], fwd[10:13]
    attn0 = _attn_fwd(qkv0, 0, forward=gathered_big)
    attn = [attn0[:2], _attn_fwd(qkv1, 1), _attn_fwd(qkv2, 2)]
    g_pa, g_pb, g_out, g_up, g_down = [own_slot(g, own) for g, own in zip(attn0[2:], later_b)]
    g_bgate, g_convw, g_ffnw = [own_slot(g, own) for g, own in zip(gathered_small, small_sharded)]
    w_pa_full, w_pb_full, w_up_full = _gathered_columns(g_pa), _gathered_columns(g_pb), _gathered_columns(g_up)
    w_out_full, w_down_full = g_out.reshape(D_MODEL, D_MODEL), g_down.reshape(D_FF, D_MODEL)
    b_gate_full, conv_w_full, ffn_w_full = (_gathered_columns(g) for g in (g_bgate, g_convw, g_ffnw))

    x1, ya0, yb0, mrg, ya, yb, lsetot = _mix_fwd(
        xs, abcv, gates, [a[0] for a in attn], [a[1] for a in attn], conv_w_full, conv_a_b, b_gate_full,
        w_pa_full, w_pb_full, w_out_full)
    h2, up0, up = _ffn_up_fwd(x1, norm_ffn_g, w_up_full, ffn_w_full, ffn_conv_b)
    act, dx2, dx2b, d_g_final, loss = _ffn_act_fwd(x1, up, target, w_down_full, g_final)

    d_up = _ffn_act_bwd(dx2b, up, w_down_full)
    slab_down = _matmul_tn("dw_down", act, dx2b, col_tile=512).reshape(N_CHIPS, D_FF // N_CHIPS, D_MODEL)
    dx1, dx1b, d_g_ffn, d_up0, ffn_small, swapped_down = _ffn_up_bwd(
        d_up, up0, w_up_full, ffn_w_full, x1, norm_ffn_g, dx2, swap=[slab_down])
    (partial_down,) = _add_sibling("grad_add_w_down", [slab_down], [swapped_down], core_arr)
    slab_up = _matmul_tn("dw_up", h2, d_up0, col_tile=2 * D_FF // N_CHIPS, slabs=N_CHIPS)
    d_w_out, swapped_up = _matmul_tn("dw_out", mrg, dx1b, swap=[slab_up])
    (partial_up,) = _add_sibling("grad_add_w_up", [slab_up], [swapped_up], core_arr)

    mix_res = _mix_bwd(dx1, abcv, gates, ya, yb, yb0, lsetot, conv_w_full, conv_a_b, b_gate_full,
                       w_pa_full, w_pb_full, w_out_full, exchange=[partial_up, partial_down])
    (d_ya, d_yb, d_gates, d_abcv, d_yb0, dyl0, dyl1, dyl2, aux0, aux1, aux2, gate_small, conv_small) = mix_res[:13]
    halves_ffn = _sum_chips("grad_sum_ffn", [partial_up, partial_down], mix_res[13:], chip_core)

    slabs_mix = [_matmul_tn("dw_proj_a", ya0, d_ya, slabs=N_CHIPS), _matmul_tn("dw_proj_b", yb0, d_yb, slabs=N_CHIPS),
                 d_w_out.reshape(N_CHIPS, D_MODEL // N_CHIPS, D_MODEL)]
    res0 = _attn_bwd(qkv0, dyl0, aux0, 0, swap=slabs_mix)
    d_qkv0, partials_mix = res0[0], _add_sibling("grad_add_mix", slabs_mix, res0[1:], core_arr)
    res1 = _attn_bwd(qkv1, dyl1, aux1, 1, exchange=partials_mix)
    d_qkv1, halves_mix = res1[0], _sum_chips("grad_sum_mix", partials_mix, res1[1:], chip_core)
    (d_qkv2,) = _attn_bwd(qkv2, dyl2, aux2, 2)

    dq = [d_qkv0, d_qkv1, d_qkv2]
    seq = xs.shape[0]
    d_w_abcv = _matmul_tn("dw_in_abcv", d_abcv, h1)
    d_w_gates = _matmul_tn("dw_in_gates", d_gates, h1)
    d_w_groups = [_matmul_tn(f"dw_in_qkv{g}", t.reshape(seq, 3 * GROUP_WIDTH), h.reshape(seq, D_MODEL))
                  for g, (t, h) in enumerate(zip(dq, (h1, h1_streams4, h1_streams16)))]
    gw = GROUP_WIDTH
    d_w_in_t = jnp.concatenate(
        [d_w_abcv] + [d_w_groups[g][j * gw:(j + 1) * gw] for j in range(3) for g in range(3)] + [d_w_gates], axis=0)

    slab_in = d_w_in_t.reshape(N_CHIPS, D_IN // N_CHIPS, D_MODEL)
    (from_sibling_in,) = _sibling_swap_halves("grad_swap_w_in", [slab_in])
    (partial_in,) = _add_sibling("grad_add_w_in", [slab_in], [from_sibling_in], core_arr)
    send_sems, recv_sems, partial_thru, landing_thru, token = _chip_exchange_start(partial_in)
    g_mix_after_start = norm_mix_g + token[0:1, 0:1]
    grad_x, d_g_mix = _inproj_bwd(d_abcv, d_gates, dq, w_in_full_t, xs, g_mix_after_start, dx1)
    partial_in, received_in = _chip_exchange_wait(send_sems, recv_sems, partial_thru, landing_thru, d_g_mix)
    halves_in = _sum_chips("grad_sum_w_in", [partial_in], [received_in], chip_core)

    big_names = ("w_in", "w_proj_a", "w_proj_b", "w_out", "w_up", "w_down")
    big_w = dict(w_in=w_in_t, w_proj_a=w_proj_a, w_proj_b=w_proj_b, w_out=w_out, w_up=w_up, w_down=w_down)
    big_m = dict(w_in=m_w_in_t, w_proj_a=m_w_proj_a, w_proj_b=m_w_proj_b, w_out=m_w_out, w_up=m_w_up, w_down=m_w_down)
    big_v = dict(w_in=v_w_in_t, w_proj_a=v_w_proj_a, w_proj_b=v_w_proj_b, w_out=v_w_out, w_up=v_w_up, w_down=v_w_down)

    fin_w, fin_m, fin_v = (a.reshape(1, D_MODEL) for a in (final_norm_g, m_final_norm_g, v_final_norm_g))
    small_w = [norm_mix_g, b_gate, conv_a_w, conv_a_b, norm_ffn_g, ffn_conv_w, ffn_conv_b, fin_w]
    small_m = [m_norm_mix_g, m_b_gate, m_conv_a_w, m_conv_a_b, m_norm_ffn_g, m_ffn_conv_w, m_ffn_conv_b, fin_m]
    small_v = [v_norm_mix_g, v_b_gate, v_conv_a_w, v_conv_a_b, v_norm_ffn_g, v_ffn_conv_w, v_ffn_conv_b, fin_v]
    small_out = _small_update([d_g_mix, d_g_ffn, d_g_final, gate_small, conv_small, ffn_small, loss],
                              small_w, small_m, small_v, halves_in + halves_mix + halves_ffn)
    big_grads = small_out[1 + 4 * len(SMALL_PARAMS):]
    total_loss = small_out[0][0, 0]

    grads, delta, new_m, new_v = {}, {}, {}, {}
    for i, n in enumerate(SMALL_PARAMS):
        vals = small_out[1 + 4 * i:5 + 4 * i]
        if n == "final_norm_g":
            vals = [a.reshape(D_MODEL) for a in vals]
        grads[n], delta[n], new_m[n], new_v[n] = vals
    updates = _adamw([big_w[n] for n in big_names], big_grads, [big_m[n] for n in big_names],
                     [big_v[n] for n in big_names])
    for n, vals in zip(big_names, updates):
        if n == "w_in":
            vals = [jnp.swapaxes(a, 1, 2) for a in vals]
        grads[n], delta[n], new_m[n], new_v[n] = vals

    names = ["norm_mix_g", "w_in", "b_gate", "conv_a_w", "conv_a_b", "w_proj_a", "w_proj_b", "w_out", "norm_ffn_g", "w_up",
             "ffn_conv_w", "ffn_conv_b", "w_down", "final_norm_g"]
    out = [total_loss, grad_x[None]]
    for group in (grads, delta, new_m, new_v):
        out += [group[n] for n in names]
    return tuple(out)
```

```python
import jax
import jax.numpy as jnp
from jax import lax
from jax.experimental import pallas as pl
from jax.experimental.pallas import tpu as pltpu

F32 = jnp.float32
BF16 = jnp.bfloat16

D_MODEL = 1024
CONV_WIDTH = 512
ATTN_WIDTH = 768
GROUP_WIDTH = 256
HEAD_DIM = 64
HEADS_PER_GROUP = 4
DILATIONS = (1, 4, 16)
ATTN_BLOCK = 128
D_FF = 2816
D_IN = 5888
EPS = 1e-6
NEG_INF = -1e30
ATTN_SCALE = HEAD_DIM ** -0.5

COL_ABCV = 0
COL_Q = 1536
COL_K = 2304
COL_V = 3072
COL_GATES = 3840

ADAM_LR = 0.001
ADAM_B1 = 0.9
ADAM_B2 = 0.999
ADAM_EPS = 1e-08
ADAM_WD = 0.01
ADAM_STEP = 10

LANES = 128
SUBLANES = 8
BF16_ROWS = 16
ROW_TILE = 512
VMEM_LIMIT = 56 * 1024 * 1024

_NT = (((1,), (1,)), ((), ()))
_TN = (((0,), (0,)), ((), ()))


def _params(n_axes, vmem=VMEM_LIMIT):
    return pltpu.CompilerParams(dimension_semantics=("arbitrary",) * n_axes, vmem_limit_bytes=vmem)


def _resident(shape):
    nd = len(shape)
    return pl.BlockSpec(shape, lambda *_: (0,) * nd, pipeline_mode=pl.Buffered(1))


def _rows(tm, width, col_block=0):
    return pl.BlockSpec((tm, width), lambda i: (i, col_block))


def _col_chunks(n, cmax):
    out, lo = [], 0
    while lo < n:
        size = min(cmax, n - lo)
        out.append((lo, size))
        lo += size
    return out


def _dot(a, b):
    return jnp.dot(a, b, preferred_element_type=F32)


def _dot_nt(a, b):
    return lax.dot_general(a, b, _NT, preferred_element_type=F32)


def _dot_tn(a, b):
    return lax.dot_general(a, b, _TN, preferred_element_type=F32)


def _sigmoid(x):
    return 0.5 * jnp.tanh(0.5 * x) + 0.5


def _silu(x):
    hx = 0.5 * x
    return hx + hx * jnp.tanh(hx)


def _shift_down(v, k, halo8):
    tm = v.shape[0]
    rolled = pltpu.roll(v, k, 0)
    fix = jnp.tile(pltpu.roll(halo8, k, 0), (tm // SUBLANES, 1))
    row = lax.broadcasted_iota(jnp.int32, v.shape, 0)
    return jnp.where(row < k, fix, rolled)


def _shift_up(v, k, halo8):
    tm = v.shape[0]
    rolled = pltpu.roll(v, tm - k, 0)
    fix = jnp.tile(pltpu.roll(halo8, SUBLANES - k, 0), (tm // SUBLANES, 1))
    row = lax.broadcasted_iota(jnp.int32, v.shape, 0)
    return jnp.where(row >= tm - k, fix, rolled)


def _colsum(v):
    return jnp.sum(v, axis=0, keepdims=True)


def _to_streams(val, scr, out_ref, d, col0):
    tm = val.shape[0]
    panels = val.shape[1] // LANES
    if d == 1:
        out_ref[0, :, col0:col0 + val.shape[1]] = val.astype(out_ref.dtype)
        return
    for p in range(panels):
        scr[pl.ds(p * tm, tm), :] = val[:, p * LANES:(p + 1) * LANES]
    for r in range(d):
        for p in range(panels):
            piece = scr[pl.ds(p * tm + r, tm // d, stride=d), :]
            out_ref[r, :, col0 + p * LANES: col0 + (p + 1) * LANES] = piece.astype(out_ref.dtype)


def _from_streams(in_ref, scr, d, col0, width):
    panels = width // LANES
    rows = in_ref.shape[1]
    tm = rows * d
    if d == 1:
        return in_ref[0, :, col0:col0 + width].astype(F32)
    for r in range(d):
        for p in range(panels):
            scr[pl.ds(p * tm + r, rows, stride=d), :] = in_ref[r, :, col0 + p * LANES: col0 + (p + 1) * LANES].astype(F32)
    return jnp.concatenate([scr[pl.ds(p * tm, tm), :] for p in range(panels)], axis=1)


def _stream_block(tm, d, width):
    return pl.BlockSpec((d, tm // d, width), lambda i: (0, i, 0))


def _rev_stream_block(tm, d, width, nt):
    return pl.BlockSpec((d, tm // d, width), lambda i: (0, nt - 1 - i, 0))


N_CHIPS = 4
MESH_ID = pl.DeviceIdType.MESH
_ANY = pl.BlockSpec(memory_space=pl.ANY)
_VMEM = pl.BlockSpec(memory_space=pltpu.VMEM)


def _mesh_position():
    x, y, c = lax.axis_index("x"), lax.axis_index("y"), lax.axis_index("c")
    other_chips = [(1 - x, y), (x, 1 - y), (1 - x, 1 - y)]
    return x, y, c, other_chips


def _half_rows(c, half):
    return pl.ds(pl.multiple_of(c * half, BF16_ROWS), half)


def _remote_copy(k, src, dst, to, send_sems, recv_sems):
    return pltpu.make_async_remote_copy(src_ref=src, dst_ref=dst, send_sem=send_sems.at[k], recv_sem=recv_sems.at[k],
                                        device_id=to, device_id_type=MESH_ID)


def _gather_first_copies(big_refs, small_refs, big_outs, small_outs, send_sems, recv_sems):
    x, y, c, chips = _mesh_position()
    me = 2 * x + y
    nb = len(big_refs)
    cps = []
    for j, (px, py) in enumerate(chips):
        for b in range(nb):
            mine = _half_rows(c, big_refs[b].shape[1] // 2)
            cps.append(_remote_copy(3 * b + j, big_refs[b].at[0, mine], big_outs[b].at[me, mine], (px, py, c),
                                    send_sems, recv_sems))
        for s in range(len(small_refs)):
            cps.append(_remote_copy(3 * (nb + s) + j, small_refs[s].at[0], small_outs[s].at[me], (px, py, c),
                                    send_sems, recv_sems))
    return cps


def _gather_forward_copies(bufs, send_sems, recv_sems):
    x, y, c, chips = _mesh_position()
    cps = []
    for j, (px, py) in enumerate(chips):
        for b in range(len(bufs)):
            landed = bufs[b].at[2 * px + py, _half_rows(c, bufs[b].shape[1] // 2)]
            cps.append(_remote_copy(3 * b + j, landed, landed, (x, y, 1 - c), send_sems, recv_sems))
    return cps


def _chip_exchange_copies(src_refs, out_refs, send_sems, recv_sems):
    x, y, c, chips = _mesh_position()
    cps = []
    for j, (px, py) in enumerate(chips):
        for a in range(len(src_refs)):
            cps.append(_remote_copy(3 * a + j, src_refs[a].at[2 * px + py], out_refs[a].at[j], (px, py, c),
                                    send_sems, recv_sems))
    return cps


def _sibling_swap_copies(src_refs, out_refs, send_sems, recv_sems):
    x, y, c, _ = _mesh_position()
    cps = []
    for a in range(len(src_refs)):
        theirs = _half_rows(1 - c, src_refs[a].shape[1] // 2)
        cps.append(_remote_copy(a, src_refs[a].at[:, theirs, :], out_refs[a], (x, y, 1 - c), send_sems, recv_sems))
    return cps


def _swap_shapes(slabs):
    return [jax.ShapeDtypeStruct((a.shape[0], a.shape[1] // 2, a.shape[2]), a.dtype) for a in slabs]


def _dma_sems(n):
    return [pltpu.SemaphoreType.DMA((n,)), pltpu.SemaphoreType.DMA((n,))]


def _norm_fwd(x, g):
    s = x.shape[0]
    tm = ROW_TILE

    def body(x_ref, g_ref, h_ref, hs1_ref, hs2_ref, scr):
        xv = x_ref[...]
        r = lax.rsqrt(jnp.mean(xv * xv, axis=-1, keepdims=True) + EPS)
        hf = xv * r * g_ref[...]
        h_ref[...] = hf.astype(BF16)
        for d, hs_ref in zip(DILATIONS[1:], (hs1_ref, hs2_ref)):
            for lo, size in _col_chunks(D_MODEL, GROUP_WIDTH):
                _to_streams(hf[:, lo:lo + size], scr, hs_ref, d, lo)

    return pl.pallas_call(
        body, name="norm_fwd", grid=(s // tm,),
        out_shape=[jax.ShapeDtypeStruct((s, D_MODEL), BF16)]
        + [jax.ShapeDtypeStruct((d, s // d, D_MODEL), BF16) for d in DILATIONS[1:]],
        in_specs=[_rows(tm, D_MODEL), _resident((1, D_MODEL))],
        out_specs=[_rows(tm, D_MODEL)] + [_stream_block(tm, d, D_MODEL) for d in DILATIONS[1:]],
        scratch_shapes=[pltpu.VMEM((GROUP_WIDTH // LANES * tm, LANES), F32)],
        compiler_params=_params(1))(x, g)


def _inproj_fwd(h1, w_in_t, big_shards, small_shards):
    s = h1.shape[0]
    tm = ROW_TILE
    nt = s // tm
    nb, ns = len(big_shards), len(small_shards)
    n_fixed_in, n_fixed_out = 2, 5

    def body(*refs):
        h_ref, w_ref = refs[:n_fixed_in]
        shard_refs = refs[n_fixed_in:n_fixed_in + nb + ns]
        pos = n_fixed_in + nb + ns
        abcv_ref, gates_ref, qkv0_ref, qkv1_ref, qkv2_ref = refs[pos:pos + n_fixed_out]
        gathered_refs = refs[pos + n_fixed_out:pos + n_fixed_out + nb + ns]
        scr, send_sems, recv_sems = refs[pos + n_fixed_out + nb + ns:]
        i = pl.program_id(0)

        def gather_copies():
            return _gather_first_copies(shard_refs[:nb], shard_refs[nb:], gathered_refs[:nb], gathered_refs[nb:],
                                        send_sems, recv_sems)

        @pl.when(i == 0)
        def _():
            for cp in gather_copies():
                cp.start()

        h = h_ref[...]
        for lo, size in _col_chunks(3 * CONV_WIDTH, 512):
            abcv_ref[:, lo:lo + size] = _dot_nt(h, w_ref[COL_ABCV + lo: COL_ABCV + lo + size, :]).astype(BF16)
        for lo, size in _col_chunks(2 * D_MODEL, 512):
            gates_ref[:, lo:lo + size] = _dot_nt(h, w_ref[COL_GATES + lo: COL_GATES + lo + size, :]).astype(BF16)
        for gi, (d, out_ref) in enumerate(zip(DILATIONS, (qkv0_ref, qkv1_ref, qkv2_ref))):
            for j, base in enumerate((COL_Q, COL_K, COL_V)):
                lo = base + gi * GROUP_WIDTH
                y = _dot_nt(h, w_ref[lo:lo + GROUP_WIDTH, :])
                if j == 0:
                    y = y * ATTN_SCALE
                _to_streams(y, scr, out_ref, d, j * GROUP_WIDTH)

        @pl.when(i == nt - 1)
        def _():
            for cp in gather_copies():
                cp.wait()

    outs = [jax.ShapeDtypeStruct((s, 3 * CONV_WIDTH), BF16), jax.ShapeDtypeStruct((s, 2 * D_MODEL), BF16)]
    outs += [jax.ShapeDtypeStruct((d, s // d, 3 * GROUP_WIDTH), BF16) for d in DILATIONS]
    outs += [jax.ShapeDtypeStruct((N_CHIPS,) + a.shape[1:], a.dtype) for a in list(big_shards) + list(small_shards)]
    return pl.pallas_call(
        body, name="inproj_fwd", grid=(nt,), out_shape=outs,
        in_specs=[_rows(tm, D_MODEL), _resident((D_IN, D_MODEL))] + [_ANY] * (nb + ns),
        out_specs=[_rows(tm, 3 * CONV_WIDTH), _rows(tm, 2 * D_MODEL)]
        + [_stream_block(tm, d, 3 * GROUP_WIDTH) for d in DILATIONS] + [_ANY] * (nb + ns),
        scratch_shapes=[pltpu.VMEM((GROUP_WIDTH // LANES * tm, LANES), F32)] + _dma_sems(3 * (nb + ns)),
        compiler_params=_params(1))(h1, w_in_t, *big_shards, *small_shards)


def _head_of_lane(shape):
    return lax.broadcasted_iota(jnp.int32, shape, 1) // HEAD_DIM


def _stack_heads(v):
    head = _head_of_lane(v.shape)
    return jnp.concatenate([jnp.where(head == h, v, jnp.zeros_like(v)) for h in range(HEADS_PER_GROUP)], axis=0)


def _unstack_heads(v):
    q = ATTN_BLOCK
    head = _head_of_lane((q, v.shape[1]))
    out = jnp.zeros((q, v.shape[1]), v.dtype)
    for h in range(HEADS_PER_GROUP):
        out = jnp.where(head == h, v[h * q:(h + 1) * q], out)
    return out


def _per_head_rows(col):
    q = ATTN_BLOCK
    head = _head_of_lane((q, GROUP_WIDTH))
    out = jnp.zeros((q, GROUP_WIDTH), col.dtype)
    for h in range(HEADS_PER_GROUP):
        out = jnp.where(head == h, col[h * q:(h + 1) * q], out)
    return out


def _compact_heads(v):
    lane = lax.broadcasted_iota(jnp.int32, (v.shape[0], LANES), 1)
    return jnp.where((lane & 32) == 0, v[:, 0:LANES], v[:, LANES:2 * LANES])


def _compact_head_col(v):
    lane = lax.broadcasted_iota(jnp.int32, v.shape, 1)
    head = ((lane >> 6) & 1) + 2 * ((lane >> 5) & 1)
    cols = [jnp.max(jnp.where(head == h, v, -jnp.inf), axis=1, keepdims=True) for h in range(HEADS_PER_GROUP)]
    return jnp.concatenate(cols, axis=0)


ATTN_BLOCKS_PER_STEP = 16


def _band_bias(first_block):
    rows = HEADS_PER_GROUP * ATTN_BLOCK
    qi = lax.broadcasted_iota(jnp.int32, (rows, 2 * ATTN_BLOCK), 0) % ATTN_BLOCK
    kj = lax.broadcasted_iota(jnp.int32, (rows, 2 * ATTN_BLOCK), 1)
    dist = qi + ATTN_BLOCK - kj
    valid = (dist >= 0) & (dist <= ATTN_BLOCK)
    if first_block:
        valid = valid & (kj >= ATTN_BLOCK)
    return jnp.where(valid, 0.0, NEG_INF).astype(F32)


def _store_band_biases(bias_ref):
    bias_ref[0] = _band_bias(False)
    bias_ref[1] = _band_bias(True)


def _attn_block_specs(g, nb, clamp_last=False):
    q = ATTN_BLOCK
    last = nb // g - 1

    def cur(col, width=GROUP_WIDTH):
        if clamp_last:
            return pl.BlockSpec((None, g * q, width), lambda r, n: (r, jnp.minimum(n, last), col))
        return pl.BlockSpec((None, g * q, width), lambda r, n: (r, n, col))

    def prev(col):
        if clamp_last:
            return pl.BlockSpec((None, q, GROUP_WIDTH), lambda r, n: (r, jnp.clip(n * g - 1, 0, nb - 1), col))
        return pl.BlockSpec((None, q, GROUP_WIDTH), lambda r, n: (r, jnp.maximum(n * g - 1, 0), col))

    return cur, prev


def _attn_fwd(qkv, gi, forward=()):
    d, length, _ = qkv.shape
    nb = length // ATTN_BLOCK
    q = ATTN_BLOCK
    g = min(ATTN_BLOCKS_PER_STEP, nb)
    ns = nb // g
    nf = len(forward)

    def body(*refs):
        q_ref, kp_ref, kc_ref, vp_ref, vc_ref = refs[:5]
        o_ref, lse_ref = refs[5 + nf:7 + nf]
        buf_refs = refs[7 + nf:7 + 2 * nf]
        bias_ref = refs[7 + 2 * nf]
        sems = refs[8 + 2 * nf:]
        n = pl.program_id(1)
        first_step = (pl.program_id(0) == 0) & (n == 0)
        last_step = (pl.program_id(0) == d - 1) & (n == ns - 1)

        @pl.when(first_step)
        def _():
            _store_band_biases(bias_ref)
            for cp in _gather_forward_copies(buf_refs, *sems) if nf else ():
                cp.start()

        kfull = jnp.concatenate([kp_ref[...], kc_ref[...]], axis=0)
        vfull = jnp.concatenate([vp_ref[...], vc_ref[...]], axis=0)
        for j in range(g):
            qs = _stack_heads(q_ref[j * q:(j + 1) * q, :])
            k2 = kfull[j * q:(j + 2) * q]
            v2 = vfull[j * q:(j + 2) * q]
            bias = jnp.where(n == 0, bias_ref[1], bias_ref[0]) if j == 0 else bias_ref[0]
            sc = _dot_nt(qs, k2) + bias
            m = jnp.max(sc, axis=1, keepdims=True)
            p = jnp.exp(sc - m)
            l = jnp.sum(p, axis=1, keepdims=True)
            of = _dot(p.astype(BF16), v2) / l
            o_ref[j * q:(j + 1) * q, :] = _unstack_heads(of).astype(BF16)
            lse_ref[j * q:(j + 1) * q, :] = _per_head_rows(m + jnp.log(l))

        if nf:
            @pl.when(last_step)
            def _():
                for cp in _gather_forward_copies(buf_refs, *sems):
                    cp.wait()

    cur, prev = _attn_block_specs(g, nb)
    return pl.pallas_call(
        body, name=f"attn_fwd_g{gi}", grid=(d, ns),
        out_shape=[jax.ShapeDtypeStruct((d, length, GROUP_WIDTH), BF16),
                   jax.ShapeDtypeStruct((d, length, GROUP_WIDTH), F32)]
        + [jax.ShapeDtypeStruct(a.shape, a.dtype) for a in forward],
        in_specs=[cur(0), prev(1), cur(1), prev(2), cur(2)] + [_ANY] * nf,
        out_specs=[cur(0), cur(0)] + [_ANY] * nf,
        input_output_aliases={5 + a: 2 + a for a in range(nf)},
        scratch_shapes=[pltpu.VMEM((2, HEADS_PER_GROUP * q, 2 * q), F32)] + (_dma_sems(3 * nf) if nf else []),
        compiler_params=_params(2))(qkv, qkv, qkv, qkv, qkv, *forward)


def _conv_branch(ab, ac, av, halo_u, w, b):
    u = ac * av
    sh1 = _shift_down(u, 1, halo_u)
    sh2 = _shift_down(u, 2, halo_u)
    cv = w[0:1] * sh2 + w[1:2] * sh1 + w[2:3] * u + b
    return ab * cv, cv, u, sh1, sh2


def _mix_fwd(x, abcv, gates, o_list, lse_list, conv_w, conv_b, b_gate, w_pa, w_pb, w_out):
    s = x.shape[0]
    tm = ROW_TILE

    def body(x_ref, abcv_ref, gates_ref, o0_ref, o1_ref, o2_ref, l0_ref, l1_ref, l2_ref,
             cw_ref, cb_ref, bg_ref, wpa_ref, wpb_ref, wout_ref,
             x1_ref, ya0_ref, yb0_ref, mrg_ref, ya_ref, yb_ref, lsetot_ref, halo_ref, scr):
        i = pl.program_id(0)

        @pl.when(i == 0)
        def _():
            halo_ref[...] = jnp.zeros_like(halo_ref)

        ab = abcv_ref[:, 0:CONV_WIDTH].astype(F32)
        ac = abcv_ref[:, CONV_WIDTH:2 * CONV_WIDTH].astype(F32)
        av = abcv_ref[:, 2 * CONV_WIDTH:3 * CONV_WIDTH].astype(F32)
        ya0, _, u, _, _ = _conv_branch(ab, ac, av, halo_ref[...], cw_ref[...], cb_ref[...])
        halo_ref[...] = u[tm - SUBLANES:tm]
        ya0 = ya0.astype(BF16)
        ya0_ref[...] = ya0
        ya = _dot(ya0, wpa_ref[...])

        o_refs, l_refs = (o0_ref, o1_ref, o2_ref), (l0_ref, l1_ref, l2_ref)
        lses = [_from_streams(l_refs[g], scr, DILATIONS[g], 0, GROUP_WIDTH) for g in range(3)]
        top = jnp.maximum(jnp.maximum(lses[0], lses[1]), lses[2])
        weights = [jnp.exp(lse - top) for lse in lses]
        total = weights[0] + weights[1] + weights[2]
        lsetot_ref[...] = top + jnp.log(total)
        inv_total = 1.0 / total
        yb = jnp.zeros((tm, D_MODEL), F32)
        for g in range(3):
            og = _from_streams(o_refs[g], scr, DILATIONS[g], 0, GROUP_WIDTH)
            yb0 = (weights[g] * inv_total * og).astype(BF16)
            yb0_ref[:, g * GROUP_WIDTH:(g + 1) * GROUP_WIDTH] = yb0
            yb = yb + _dot(yb0, wpb_ref[g * GROUP_WIDTH:(g + 1) * GROUP_WIDTH, :])

        sa = _sigmoid(gates_ref[:, 0:D_MODEL].astype(F32) + bg_ref[0:1, :])
        sb = _sigmoid(gates_ref[:, D_MODEL:2 * D_MODEL].astype(F32) + bg_ref[1:2, :])
        ya_ref[...] = ya.astype(BF16)
        yb_ref[...] = yb.astype(BF16)
        mrg = (sa * ya + sb * yb).astype(BF16)
        mrg_ref[...] = mrg
        x1_ref[...] = x_ref[...] + _dot(mrg, wout_ref[...])

    outs = [jax.ShapeDtypeStruct((s, D_MODEL), F32),
            jax.ShapeDtypeStruct((s, CONV_WIDTH), BF16),
            jax.ShapeDtypeStruct((s, ATTN_WIDTH), BF16),
            jax.ShapeDtypeStruct((s, D_MODEL), BF16),
            jax.ShapeDtypeStruct((s, D_MODEL), BF16),
            jax.ShapeDtypeStruct((s, D_MODEL), BF16),
            jax.ShapeDtypeStruct((s, GROUP_WIDTH), F32)]
    return pl.pallas_call(
        body, name="mix_fwd", grid=(s // tm,), out_shape=outs,
        in_specs=[_rows(tm, D_MODEL), _rows(tm, 3 * CONV_WIDTH), _rows(tm, 2 * D_MODEL)]
        + [_stream_block(tm, d, GROUP_WIDTH) for d in DILATIONS] * 2
        + [_resident((3, CONV_WIDTH)), _resident((1, CONV_WIDTH)), _resident((2, D_MODEL)),
           _resident((CONV_WIDTH, D_MODEL)), _resident((ATTN_WIDTH, D_MODEL)), _resident((D_MODEL, D_MODEL))],
        out_specs=[_rows(tm, D_MODEL), _rows(tm, CONV_WIDTH), _rows(tm, ATTN_WIDTH), _rows(tm, D_MODEL),
                   _rows(tm, D_MODEL), _rows(tm, D_MODEL), _rows(tm, GROUP_WIDTH)],
        scratch_shapes=[pltpu.VMEM((SUBLANES, CONV_WIDTH), F32),
                        pltpu.VMEM((GROUP_WIDTH // LANES * tm, LANES), F32)],
        compiler_params=_params(1))(x, abcv, gates, *o_list, *lse_list, conv_w, conv_b, b_gate, w_pa, w_pb, w_out)


FFN_CHUNK = 512
FFN_UP_ROW_TILE = 256


def _ffn_up_fwd(x1, g, w_up, conv_w, conv_b):
    s = x1.shape[0]
    n = w_up.shape[1]
    tm = FFN_UP_ROW_TILE

    def body(x_ref, g_ref, w_ref, cw_ref, cb_ref, h_ref, up0_ref, up_ref, halo_ref):
        @pl.when(pl.program_id(0) == 0)
        def _():
            halo_ref[...] = jnp.zeros_like(halo_ref)

        xv = x_ref[...]
        r = lax.rsqrt(jnp.mean(xv * xv, axis=-1, keepdims=True) + EPS)
        h = (xv * r * g_ref[...]).astype(BF16)
        h_ref[...] = h
        for lo, size in _col_chunks(n, FFN_CHUNK):
            cols = slice(lo, lo + size)
            y = _dot(h, w_ref[:, cols])
            up0_ref[:, cols] = y.astype(BF16)
            halo = halo_ref[:, cols]
            w = cw_ref[:, cols]
            up = w[0:1] * _shift_down(y, 2, halo) + w[1:2] * _shift_down(y, 1, halo) + w[2:3] * y + cb_ref[:, cols]
            up_ref[:, cols] = up.astype(BF16)
            halo_ref[:, cols] = y[tm - SUBLANES:tm]

    return pl.pallas_call(
        body, name="ffn_up_fwd", grid=(s // tm,),
        out_shape=[jax.ShapeDtypeStruct((s, D_MODEL), BF16), jax.ShapeDtypeStruct((s, n), BF16),
                   jax.ShapeDtypeStruct((s, n), BF16)],
        in_specs=[_rows(tm, D_MODEL), _resident((1, D_MODEL)), _resident((D_MODEL, n)), _resident((3, n)),
                  _resident((1, n))],
        out_specs=[_rows(tm, D_MODEL), _rows(tm, n), _rows(tm, n)],
        scratch_shapes=[pltpu.VMEM((SUBLANES, n), F32)],
        compiler_params=_params(1))(x1, g, w_up, conv_w, conv_b)


def _ffn_act_fwd(x1, up, target, w_down, g_final):
    s = x1.shape[0]
    tm = ROW_TILE

    def body(x1_ref, up_ref, tgt_ref, wd_ref, gf_ref, act_ref, dx2_ref, dx2b_ref, dgf_ref, loss_ref):
        @pl.when(pl.program_id(0) == 0)
        def _():
            dgf_ref[...] = jnp.zeros_like(dgf_ref)
            loss_ref[...] = jnp.zeros_like(loss_ref)

        acc = jnp.zeros((tm, D_MODEL), F32)
        for lo, size in _col_chunks(D_FF, FFN_CHUNK):
            gate = up_ref[:, lo:lo + size].astype(F32)
            val = up_ref[:, D_FF + lo:D_FF + lo + size].astype(F32)
            act = (_silu(gate) * val).astype(BF16)
            act_ref[:, lo:lo + size] = act
            acc = acc + _dot(act, wd_ref[lo:lo + size, :])

        x2 = x1_ref[...] + acc
        r = lax.rsqrt(jnp.mean(x2 * x2, axis=-1, keepdims=True) + EPS)
        xn = x2 * r
        gf = gf_ref[...]
        err = xn * gf - tgt_ref[...]
        loss_ref[...] += (0.5 / D_MODEL) * jnp.sum(err * err)
        dy = err * (1.0 / D_MODEL)
        dgf_ref[...] += _colsum(dy * xn)
        dxn = dy * gf
        dx2 = r * (dxn - xn * jnp.mean(dxn * xn, axis=-1, keepdims=True))
        dx2_ref[...] = dx2
        dx2b_ref[...] = dx2.astype(BF16)

    return pl.pallas_call(
        body, name="ffn_act_fwd", grid=(s // tm,),
        out_shape=[jax.ShapeDtypeStruct((s, D_FF), BF16), jax.ShapeDtypeStruct((s, D_MODEL), F32),
                   jax.ShapeDtypeStruct((s, D_MODEL), BF16),
                   jax.ShapeDtypeStruct((1, D_MODEL), F32), jax.ShapeDtypeStruct((1, LANES), F32)],
        in_specs=[_rows(tm, D_MODEL), _rows(tm, 2 * D_FF), _rows(tm, D_MODEL),
                  _resident((D_FF, D_MODEL)), _resident((1, D_MODEL))],
        out_specs=[_rows(tm, D_FF), _rows(tm, D_MODEL), _rows(tm, D_MODEL),
                   pl.BlockSpec((1, D_MODEL), lambda i: (0, 0)), pl.BlockSpec((1, LANES), lambda i: (0, 0))],
        compiler_params=_params(1))(x1, up, target, w_down, g_final)


def _ffn_act_bwd(dx2b, up, w_down):
    s = dx2b.shape[0]
    tm = ROW_TILE

    def body(dx2_ref, up_ref, wd_ref, dup_ref):
        dx2 = dx2_ref[...]
        for lo, size in _col_chunks(D_FF, FFN_CHUNK):
            gate = up_ref[:, lo:lo + size].astype(F32)
            val = up_ref[:, D_FF + lo:D_FF + lo + size].astype(F32)
            dact = _dot_nt(dx2, wd_ref[lo:lo + size, :])
            sg = _sigmoid(gate)
            dup_ref[:, lo:lo + size] = (dact * val * (sg * (1.0 + gate * (1.0 - sg)))).astype(BF16)
            dup_ref[:, D_FF + lo:D_FF + lo + size] = (dact * (gate * sg)).astype(BF16)

    return pl.pallas_call(
        body, name="ffn_act_bwd", grid=(s // tm,),
        out_shape=jax.ShapeDtypeStruct((s, 2 * D_FF), BF16),
        in_specs=[_rows(tm, D_MODEL), _rows(tm, 2 * D_FF), _resident((D_FF, D_MODEL))],
        out_specs=_rows(tm, 2 * D_FF),
        compiler_params=_params(1))(dx2b, up, w_down)


def _ffn_up_bwd(d_up, up0, w_up, conv_w, x1, g, dres, swap=()):
    s = x1.shape[0]
    n = w_up.shape[1]
    tm = FFN_UP_ROW_TILE
    nt = s // tm
    nw = len(swap)

    def body(*refs):
        dup_ref, up0_ref, w_ref, cw_ref, x_ref, g_ref, dres_ref = refs[:7]
        slab_refs = refs[7:7 + nw]
        dx_ref, dxb_ref, dg_ref, dup0_ref, small_ref = refs[7 + nw:12 + nw]
        swapped_refs = refs[12 + nw:12 + 2 * nw]
        next_ref = refs[12 + 2 * nw]
        sems = refs[13 + 2 * nw:]

        @pl.when(pl.program_id(0) == 0)
        def _():
            next_ref[...] = jnp.zeros_like(next_ref)
            small_ref[...] = jnp.zeros_like(small_ref)
            dg_ref[...] = jnp.zeros_like(dg_ref)
            for cp in _sibling_swap_copies(slab_refs, swapped_refs, *sems) if nw else ():
                cp.start()

        dh = jnp.zeros((tm, D_MODEL), F32)
        for lo, size in _col_chunks(n, FFN_CHUNK):
            cols = slice(lo, lo + size)
            dz = dup_ref[:, cols].astype(F32)
            x0 = up0_ref[:, cols].astype(F32)
            nxt = next_ref[:, cols]
            dz1 = _shift_up(dz, 1, nxt)
            dz2 = _shift_up(dz, 2, nxt)
            next_ref[:, cols] = dz[0:SUBLANES]
            small_ref[0:1, cols] += _colsum(dz2 * x0)
            small_ref[1:2, cols] += _colsum(dz1 * x0)
            small_ref[2:3, cols] += _colsum(dz * x0)
            small_ref[3:4, cols] += _colsum(dz)
            w = cw_ref[:, cols]
            dup0 = (w[2:3] * dz + w[1:2] * dz1 + w[0:1] * dz2).astype(BF16)
            dup0_ref[:, cols] = dup0
            dh = dh + _dot_nt(dup0, w_ref[:, cols])
        xv = x_ref[...]
        r = lax.rsqrt(jnp.mean(xv * xv, axis=-1, keepdims=True) + EPS)
        xn = xv * r
        dg_ref[...] += _colsum(dh * xn)
        dxn = dh * g_ref[...]
        dx = dres_ref[...] + r * (dxn - xn * jnp.mean(dxn * xn, axis=-1, keepdims=True))
        dx_ref[...] = dx
        dxb_ref[...] = dx.astype(BF16)

        if nw:
            @pl.when(pl.program_id(0) == nt - 1)
            def _():
                for cp in _sibling_swap_copies(slab_refs, swapped_refs, *sems):
                    cp.wait()

    rows = lambda width: pl.BlockSpec((tm, width), lambda i: (nt - 1 - i, 0))
    return pl.pallas_call(
        body, name="ffn_up_bwd", grid=(nt,),
        out_shape=[jax.ShapeDtypeStruct((s, D_MODEL), F32), jax.ShapeDtypeStruct((s, D_MODEL), BF16),
                   jax.ShapeDtypeStruct((1, D_MODEL), F32), jax.ShapeDtypeStruct((s, n), BF16),
                   jax.ShapeDtypeStruct((SUBLANES, n), F32)] + _swap_shapes(swap),
        in_specs=[rows(n), rows(n), _resident((D_MODEL, n)), _resident((3, n)), rows(D_MODEL),
                  _resident((1, D_MODEL)), rows(D_MODEL)] + [_ANY] * nw,
        out_specs=[rows(D_MODEL), rows(D_MODEL), pl.BlockSpec((1, D_MODEL), lambda i: (0, 0)), rows(n),
                   pl.BlockSpec((SUBLANES, n), lambda i: (0, 0))] + [_ANY] * nw,
        scratch_shapes=[pltpu.VMEM((SUBLANES, n), F32)] + (_dma_sems(nw) if nw else []),
        compiler_params=_params(1))(d_up, up0, w_up, conv_w, x1, g, dres, *swap)


def _inproj_bwd(d_abcv, d_gates, d_qkvs, w_in_t, x, g, dres):
    s = x.shape[0]
    tm = ROW_TILE
    gw = GROUP_WIDTH

    def body(dabcv_ref, dgates_ref, dq0_ref, dq1_ref, dq2_ref, w_ref, x_ref, g_ref, dres_ref, dx_ref, dg_ref, scr):
        @pl.when(pl.program_id(0) == 0)
        def _():
            dg_ref[...] = jnp.zeros_like(dg_ref)

        dh = jnp.zeros((tm, D_MODEL), F32)
        for src, width, wrow in ((dabcv_ref, 3 * CONV_WIDTH, COL_ABCV), (dgates_ref, 2 * D_MODEL, COL_GATES)):
            for lo, size in _col_chunks(width, 512):
                dh = dh + _dot(src[:, lo:lo + size], w_ref[wrow + lo:wrow + lo + size, :])
        for gi, (d, dq_ref) in enumerate(zip(DILATIONS, (dq0_ref, dq1_ref, dq2_ref))):
            for j, base in enumerate((COL_Q, COL_K, COL_V)):
                dy = _from_streams(dq_ref, scr, d, j * gw, gw).astype(BF16)
                wrow = base + gi * gw
                dh = dh + _dot(dy, w_ref[wrow:wrow + gw, :])
        xv = x_ref[...]
        r = lax.rsqrt(jnp.mean(xv * xv, axis=-1, keepdims=True) + EPS)
        xn = xv * r
        dg_ref[...] += _colsum(dh * xn)
        dxn = dh * g_ref[...]
        dx_ref[...] = dres_ref[...] + r * (dxn - xn * jnp.mean(dxn * xn, axis=-1, keepdims=True))

    return pl.pallas_call(
        body, name="inproj_bwd", grid=(s // tm,),
        out_shape=[jax.ShapeDtypeStruct((s, D_MODEL), F32), jax.ShapeDtypeStruct((1, D_MODEL), F32)],
        in_specs=[_rows(tm, 3 * CONV_WIDTH), _rows(tm, 2 * D_MODEL)]
        + [_stream_block(tm, d, 3 * gw) for d in DILATIONS]
        + [_resident((D_IN, D_MODEL)), _rows(tm, D_MODEL), _resident((1, D_MODEL)), _rows(tm, D_MODEL)],
        out_specs=[_rows(tm, D_MODEL), pl.BlockSpec((1, D_MODEL), lambda i: (0, 0))],
        scratch_shapes=[pltpu.VMEM((gw // LANES * tm, LANES), F32)],
        compiler_params=_params(1))(d_abcv, d_gates, *d_qkvs, w_in_t, x, g, dres)


def _mix_bwd(dx1, abcv, gates, ya, yb, yb0, lsetot, conv_w, conv_b, b_gate, w_pa, w_pb, w_out, exchange=()):
    s = dx1.shape[0]
    tm = ROW_TILE
    nt = s // tm
    hb = tm // (2 * SUBLANES)
    nx = len(exchange)

    def body(*refs):
        (dx1_ref, abcv_ref, pre_ref, gates_ref, ya_ref, yb_ref, yb0_ref, lsetot_ref,
         cw_ref, cb_ref, bg_ref, wpa_ref, wpb_ref, wout_ref) = refs[:14]
        part_refs = refs[14:14 + nx]
        (dya_ref, dyb_ref, dgates_ref, dabcv_ref, dyb0_ref, dyl0_ref, dyl1_ref, dyl2_ref, aux0_ref, aux1_ref,
         aux2_ref, sm_gate_ref, sm_conv_ref) = refs[14 + nx:27 + nx]
        recv_refs = refs[27 + nx:27 + 2 * nx]
        next_ref, scr = refs[27 + 2 * nx:29 + 2 * nx]
        sems = refs[29 + 2 * nx:]
        i = pl.program_id(0)

        @pl.when(i == 0)
        def _():
            next_ref[...] = jnp.zeros_like(next_ref)
            sm_gate_ref[...] = jnp.zeros_like(sm_gate_ref)
            sm_conv_ref[...] = jnp.zeros_like(sm_conv_ref)
            for cp in _chip_exchange_copies(part_refs, recv_refs, *sems) if nx else ():
                cp.start()

        not_first = (i < nt - 1).astype(F32)
        dm = _dot_nt(dx1_ref[...].astype(BF16), wout_ref[...])
        sa = _sigmoid(gates_ref[:, 0:D_MODEL].astype(F32) + bg_ref[0:1, :])
        sb = _sigmoid(gates_ref[:, D_MODEL:2 * D_MODEL].astype(F32) + bg_ref[1:2, :])
        dya = (dm * sa).astype(BF16)
        dyb = (dm * sb).astype(BF16)
        dya_ref[...] = dya
        dyb_ref[...] = dyb
        dga = dm * ya_ref[...].astype(F32) * (sa * (1.0 - sa))
        dgb = dm * yb_ref[...].astype(F32) * (sb * (1.0 - sb))
        dgates_ref[:, 0:D_MODEL] = dga.astype(BF16)
        dgates_ref[:, D_MODEL:2 * D_MODEL] = dgb.astype(BF16)
        sm_gate_ref[0:1, :] += _colsum(dga)
        sm_gate_ref[1:2, :] += _colsum(dgb)

        dya0 = _dot_nt(dya, wpa_ref[...])
        ab = abcv_ref[:, 0:CONV_WIDTH].astype(F32)
        ac = abcv_ref[:, CONV_WIDTH:2 * CONV_WIDTH].astype(F32)
        av = abcv_ref[:, 2 * CONV_WIDTH:3 * CONV_WIDTH].astype(F32)
        pre = pre_ref[...].astype(F32) * not_first
        halo_u = (pre[:, CONV_WIDTH:2 * CONV_WIDTH] * pre[:, 2 * CONV_WIDTH:3 * CONV_WIDTH])[SUBLANES:2 * SUBLANES]
        w = cw_ref[...]
        _, cv, u, sh1, sh2 = _conv_branch(ab, ac, av, halo_u, w, cb_ref[...])
        dcv = dya0 * ab
        sm_conv_ref[0:1, :] += _colsum(dcv * sh2)
        sm_conv_ref[1:2, :] += _colsum(dcv * sh1)
        sm_conv_ref[2:3, :] += _colsum(dcv * u)
        sm_conv_ref[3:4, :] += _colsum(dcv)
        nxt = next_ref[...]
        du = w[2:3] * dcv + w[1:2] * _shift_up(dcv, 1, nxt) + w[0:1] * _shift_up(dcv, 2, nxt)
        next_ref[...] = dcv[0:SUBLANES]
        dabcv_ref[:, 0:CONV_WIDTH] = (dya0 * cv).astype(BF16)
        dabcv_ref[:, CONV_WIDTH:2 * CONV_WIDTH] = (du * av).astype(BF16)
        dabcv_ref[:, 2 * CONV_WIDTH:3 * CONV_WIDTH] = (du * ac).astype(BF16)

        head_r = lax.broadcasted_iota(jnp.int32, (GROUP_WIDTH, GROUP_WIDTH), 0) // HEAD_DIM
        head_c = lax.broadcasted_iota(jnp.int32, (GROUP_WIDTH, GROUP_WIDTH), 1) // HEAD_DIM
        same_head = (head_r == head_c).astype(BF16)
        prod = jnp.zeros((tm, GROUP_WIDTH), F32)
        dyb0s = []
        for g in range(3):
            cols = slice(g * GROUP_WIDTH, (g + 1) * GROUP_WIDTH)
            dyb0 = _dot_nt(dyb, wpb_ref[cols, :])
            dyb0_ref[:, cols] = dyb0.astype(BF16)
            dyb0s.append(dyb0)
            prod = prod + dyb0 * yb0_ref[:, cols].astype(F32)
        hi = prod.astype(BF16)
        mid = (prod - hi.astype(F32)).astype(BF16)
        lo = (prod - hi.astype(F32) - mid.astype(F32)).astype(BF16)
        delta = _dot(hi, same_head) + _dot(mid, same_head) + _dot(lo, same_head)
        lse_c = _compact_heads(lsetot_ref[...])
        delta_c = _compact_heads(delta)
        for g, (dy_ref, aux_ref) in enumerate(zip((dyl0_ref, dyl1_ref, dyl2_ref), (aux0_ref, aux1_ref, aux2_ref))):
            d = DILATIONS[g]
            _to_streams(dyb0s[g], scr, dy_ref, d, 0)
            _to_streams(lse_c, scr, aux_ref, d, 0)
            _to_streams(delta_c, scr, aux_ref, d, LANES)

        if nx:
            @pl.when(i == nt - 1)
            def _():
                for cp in _chip_exchange_copies(part_refs, recv_refs, *sems):
                    cp.wait()

    rev = lambda i: (nt - 1 - i, 0)
    pre = lambda i: (jnp.maximum((nt - 1 - i) * hb - 1, 0), 0)
    rows = lambda width: pl.BlockSpec((tm, width), rev)
    outs = [jax.ShapeDtypeStruct((s, D_MODEL), BF16), jax.ShapeDtypeStruct((s, D_MODEL), BF16),
            jax.ShapeDtypeStruct((s, 2 * D_MODEL), BF16), jax.ShapeDtypeStruct((s, 3 * CONV_WIDTH), BF16),
            jax.ShapeDtypeStruct((s, ATTN_WIDTH), BF16)]
    outs += [jax.ShapeDtypeStruct((d, s // d, GROUP_WIDTH), BF16) for d in DILATIONS]
    outs += [jax.ShapeDtypeStruct((d, s // d, 2 * LANES), F32) for d in DILATIONS]
    outs += [jax.ShapeDtypeStruct((SUBLANES, D_MODEL), F32), jax.ShapeDtypeStruct((SUBLANES, CONV_WIDTH), F32)]
    outs += [jax.ShapeDtypeStruct((3,) + a.shape[1:], a.dtype) for a in exchange]
    return pl.pallas_call(
        body, name="mix_bwd", grid=(nt,), out_shape=outs,
        in_specs=[rows(D_MODEL), rows(3 * CONV_WIDTH), pl.BlockSpec((2 * SUBLANES, 3 * CONV_WIDTH), pre),
                  rows(2 * D_MODEL), rows(D_MODEL), rows(D_MODEL), rows(ATTN_WIDTH), rows(GROUP_WIDTH),
                  _resident((3, CONV_WIDTH)), _resident((1, CONV_WIDTH)), _resident((2, D_MODEL)),
                  _resident((CONV_WIDTH, D_MODEL)), _resident((ATTN_WIDTH, D_MODEL)), _resident((D_MODEL, D_MODEL))]
        + [_ANY] * nx,
        out_specs=[rows(D_MODEL), rows(D_MODEL), rows(2 * D_MODEL), rows(3 * CONV_WIDTH), rows(ATTN_WIDTH)]
        + [_rev_stream_block(tm, d, GROUP_WIDTH, nt) for d in DILATIONS]
        + [_rev_stream_block(tm, d, 2 * LANES, nt) for d in DILATIONS]
        + [pl.BlockSpec((SUBLANES, D_MODEL), lambda i: (0, 0)), pl.BlockSpec((SUBLANES, CONV_WIDTH), lambda i: (0, 0))]
        + [_ANY] * nx,
        scratch_shapes=[pltpu.VMEM((SUBLANES, CONV_WIDTH), F32),
                        pltpu.VMEM((GROUP_WIDTH // LANES * tm, LANES), F32)] + (_dma_sems(3 * nx) if nx else []),
        compiler_params=_params(1))(dx1, abcv, abcv, gates, ya, yb, yb0, lsetot,
                                    conv_w, conv_b, b_gate, w_pa, w_pb, w_out, *exchange)


def _attn_bwd(qkv, dy, aux, gi, exchange=(), swap=()):
    d, length, _ = qkv.shape
    nb = length // ATTN_BLOCK
    q = ATTN_BLOCK
    gw = GROUP_WIDTH
    g = min(ATTN_BLOCKS_PER_STEP, nb)
    assert g >= 2 and nb % g == 0
    ns = nb // g
    lag = 1 if ns > 1 else 0
    tail = (g - 1) * q
    nx, nw = len(exchange), len(swap)

    def body(*refs):
        q_ref, kp_ref, kc_ref, vp_ref, vc_ref, dy_ref, aux_ref = refs[:7]
        part_refs = refs[7:7 + nx]
        slab_refs = refs[7 + nx:7 + nx + nw]
        pos = 7 + nx + nw
        out_ref = refs[pos]
        recv_refs = refs[pos + 1:pos + 1 + nx]
        swapped_refs = refs[pos + 1 + nx:pos + 1 + nx + nw]
        pos += 1 + nx + nw
        dq_ref, dkv_ref, bias_ref = refs[pos:pos + 3]
        sems = refs[pos + 3:]
        n = pl.program_id(1)

        def copies():
            cps = _chip_exchange_copies(part_refs, recv_refs, sems[0], sems[1]) if nx else []
            return cps + (_sibling_swap_copies(slab_refs, swapped_refs, sems[-2], sems[-1]) if nw else [])

        @pl.when((pl.program_id(0) == 0) & (n == 0))
        def _():
            _store_band_biases(bias_ref)
            for cp in copies():
                cp.start()

        if nx or nw:
            @pl.when((pl.program_id(0) == d - 1) & (n == ns - 1 + lag))
            def _():
                for cp in copies():
                    cp.wait()

        def emit(rows):
            out_ref[rows, gw:2 * gw] = dkv_ref[0, rows].astype(BF16)
            out_ref[rows, 2 * gw:3 * gw] = dkv_ref[1, rows].astype(BF16)

        if lag:
            @pl.when(n > 0)
            def _():
                out_ref[:, 0:gw] = dq_ref[...].astype(BF16)
                emit(slice(0, tail))

            @pl.when(n == ns)
            def _():
                emit(slice(tail, g * q))

        @pl.when(n < ns)
        def _():
            kfull = jnp.concatenate([kp_ref[...], kc_ref[...]], axis=0)
            vfull = jnp.concatenate([vp_ref[...], vc_ref[...]], axis=0)
            for j in range(g):
                rows = slice(j * q, (j + 1) * q)
                qs = _stack_heads(q_ref[rows, :])
                dys = _stack_heads(dy_ref[rows, :])
                k2 = kfull[j * q:(j + 2) * q]
                v2 = vfull[j * q:(j + 2) * q]
                lse = _compact_head_col(aux_ref[rows, 0:LANES])
                delta = _compact_head_col(aux_ref[rows, LANES:2 * LANES])
                bias = jnp.where(n == 0, bias_ref[1], bias_ref[0]) if j == 0 else bias_ref[0]
                p = jnp.exp(_dot_nt(qs, k2) + bias - lse)
                dp = _dot_nt(dys, v2)
                ds = (p * (dp - delta)).astype(BF16)
                dq_j = _unstack_heads(_dot(ds, k2)) * ATTN_SCALE
                dk2 = _dot_tn(ds, qs)
                dv2 = _dot_tn(p.astype(BF16), dys)
                if j == 0:
                    @pl.when(n > 0)
                    def _():
                        out_ref[tail:g * q, gw:2 * gw] = (dkv_ref[0, tail:g * q] + dk2[0:q]).astype(BF16)
                        out_ref[tail:g * q, 2 * gw:3 * gw] = (dkv_ref[1, tail:g * q] + dv2[0:q]).astype(BF16)
                else:
                    dkv_ref[0, (j - 1) * q:j * q] += dk2[0:q]
                    dkv_ref[1, (j - 1) * q:j * q] += dv2[0:q]
                dkv_ref[0, rows] = dk2[q:2 * q]
                dkv_ref[1, rows] = dv2[q:2 * q]
                dq_ref[rows, :] = dq_j
            if not lag:
                out_ref[:, 0:gw] = dq_ref[...].astype(BF16)
                emit(slice(0, g * q))

    cur, prev = _attn_block_specs(g, nb, clamp_last=True)
    return pl.pallas_call(
        body, name=f"attn_bwd_g{gi}", grid=(d, ns + lag),
        out_shape=[jax.ShapeDtypeStruct((d, length, 3 * gw), BF16)]
        + [jax.ShapeDtypeStruct((3,) + a.shape[1:], a.dtype) for a in exchange] + _swap_shapes(swap),
        in_specs=[cur(0), prev(1), cur(1), prev(2), cur(2), cur(0), cur(0, 2 * LANES)] + [_ANY] * (nx + nw),
        out_specs=[pl.BlockSpec((None, g * q, 3 * gw), lambda r, n: (r, jnp.maximum(n - lag, 0), 0))]
        + [_ANY] * (nx + nw),
        scratch_shapes=[pltpu.VMEM((g * q, gw), F32), pltpu.VMEM((2, g * q, gw), F32),
                        pltpu.VMEM((2, HEADS_PER_GROUP * q, 2 * q), F32)]
        + (_dma_sems(3 * nx) if nx else []) + (_dma_sems(nw) if nw else []),
        compiler_params=_params(2))(qkv, qkv, qkv, qkv, qkv, dy, aux, *exchange, *swap)


def _matmul_tn(name, a, b, col_tile=1024, row_tile=2048, slabs=0, swap=()):
    s, k = a.shape
    n = b.shape[1]
    tk = min(row_tile, s)
    tn = col_tile
    steps = s // tk
    nw = len(swap)

    def body(*refs):
        a_ref, b_ref = refs[:2]
        slab_refs = refs[2:2 + nw]
        o_ref = refs[2 + nw]
        swapped_refs = refs[3 + nw:3 + 2 * nw]
        acc_ref = refs[3 + 2 * nw]
        sems = refs[4 + 2 * nw:]
        t = pl.program_id(1)

        if nw:
            @pl.when((pl.program_id(0) == 0) & (t == 0))
            def _():
                for cp in _sibling_swap_copies(slab_refs, swapped_refs, *sems):
                    cp.start()

            @pl.when((pl.program_id(0) == n // tn - 1) & (t == steps - 1))
            def _():
                for cp in _sibling_swap_copies(slab_refs, swapped_refs, *sems):
                    cp.wait()

        @pl.when(t == 0)
        def _():
            acc_ref[...] = jnp.zeros_like(acc_ref)

        acc_ref[...] += _dot_tn(a_ref[...], b_ref[...])

        @pl.when(t == steps - 1)
        def _():
            if slabs:
                for q in range(per_tile):
                    o_ref[q] = acc_ref[:, q * width:(q + 1) * width].astype(BF16)
            else:
                o_ref[...] = acc_ref[...].astype(BF16)

    if slabs:
        width = n // slabs
        per_tile = tn // width
        out_shape = jax.ShapeDtypeStruct((slabs, k, width), BF16)
        out_spec = pl.BlockSpec((per_tile, k, width), lambda j, t: (j, 0, 0))
    else:
        out_shape = jax.ShapeDtypeStruct((k, n), BF16)
        out_spec = pl.BlockSpec((k, tn), lambda j, t: (0, j))
    res = pl.pallas_call(
        body, name=name, grid=(n // tn, steps), out_shape=[out_shape] + _swap_shapes(swap),
        in_specs=[pl.BlockSpec((tk, k), lambda j, t: (t, 0)), pl.BlockSpec((tk, tn), lambda j, t: (t, j))] + [_ANY] * nw,
        out_specs=[out_spec] + [_ANY] * nw,
        scratch_shapes=[pltpu.VMEM((k, tn), F32)] + (_dma_sems(nw) if nw else []),
        compiler_params=_params(2))(a, b, *swap)
    return res if nw else res[0]


def _sibling_swap_halves(name, slabs):
    na = len(slabs)

    def body(*refs):
        src_refs, out_refs = refs[:na], refs[na:2 * na]
        send_sems, recv_sems = refs[2 * na:]
        x, y, c, _ = _mesh_position()
        cps = []
        for a in range(na):
            theirs = _half_rows(1 - c, src_refs[a].shape[1] // 2)
            cps.append(pltpu.make_async_remote_copy(
                src_ref=src_refs[a].at[:, theirs, :], dst_ref=out_refs[a], send_sem=send_sems.at[a],
                recv_sem=recv_sems.at[a], device_id=(x, y, 1 - c), device_id_type=MESH_ID))
        for cp in cps:
            cp.start()
        for cp in cps:
            cp.wait()

    return pl.pallas_call(
        body, name=name,
        out_shape=[jax.ShapeDtypeStruct((a.shape[0], a.shape[1] // 2, a.shape[2]), a.dtype) for a in slabs],
        in_specs=[_ANY] * na, out_specs=[_ANY] * na,
        scratch_shapes=[pltpu.SemaphoreType.DMA((na,)), pltpu.SemaphoreType.DMA((na,))])(*slabs)


_HBM = pl.BlockSpec(memory_space=pltpu.HBM)
_SEM = pl.BlockSpec(memory_space=pltpu.SEMAPHORE)
_DATAFLOW = pltpu.SideEffectType.DATAFLOW_SIDE_EFFECTING


def _gather_start(shard):
    gathered = jax.ShapeDtypeStruct((N_CHIPS,) + shard.shape[1:], shard.dtype)

    def body(src_ref, buf_ref, send_sems, recv_sems, src_thru, buf_thru, token):
        for cp in _gather_first_copies([src_ref], [], [buf_ref], [], send_sems, recv_sems):
            cp.start()
        token[...] = jnp.zeros_like(token)

    return pl.pallas_call(
        body, name="gather_start",
        out_shape=(pltpu.SemaphoreType.DMA((3,)), pltpu.SemaphoreType.DMA((3,)),
                   pltpu.HBM(shard.shape, shard.dtype), pltpu.HBM(gathered.shape, gathered.dtype),
                   jax.ShapeDtypeStruct((SUBLANES, LANES), F32)),
        in_specs=(_HBM, _HBM), out_specs=(_SEM, _SEM, _HBM, _HBM, _VMEM), input_output_aliases={0: 2, 1: 3},
        compiler_params=pltpu.CompilerParams(has_side_effects=_DATAFLOW),
    )(pltpu.with_memory_space_constraint(shard, pltpu.HBM),
      pltpu.with_memory_space_constraint(lax.empty(gathered.shape, gathered.dtype), pltpu.HBM))


def _gather_forward(send_sems, recv_sems, shard_thru, buf_thru, after):
    def body(src_ref, buf_ref, send_sems, recv_sems, after_ref, fwd_send, fwd_recv, buf_out):
        first = _gather_first_copies([src_ref], [], [buf_ref], [], send_sems, recv_sems)
        for cp, fwd in zip(first, _gather_forward_copies([buf_ref], fwd_send, fwd_recv)):
            cp.wait_send()
            cp.wait_recv()
            fwd.start()

    return pl.pallas_call(
        body, name="gather_forward",
        out_shape=(pltpu.SemaphoreType.DMA((3,)), pltpu.SemaphoreType.DMA((3,)),
                   pltpu.HBM(buf_thru.shape, buf_thru.dtype)),
        in_specs=(_HBM, _HBM, _SEM, _SEM, _ANY), out_specs=(_SEM, _SEM, _HBM), input_output_aliases={1: 2},
        compiler_params=pltpu.CompilerParams(has_side_effects=_DATAFLOW),
    )(shard_thru, buf_thru, send_sems, recv_sems, after)


def _gather_wait(fwd_send, fwd_recv, buf_thru):
    def body(buf_ref, fwd_send, fwd_recv, buf_out):
        for cp in _gather_forward_copies([buf_ref], fwd_send, fwd_recv):
            cp.wait_send()
            cp.wait_recv()

    return pl.pallas_call(
        body, name="gather_wait", out_shape=pltpu.HBM(buf_thru.shape, buf_thru.dtype),
        in_specs=(_HBM, _SEM, _SEM), out_specs=_HBM, input_output_aliases={0: 0},
        compiler_params=pltpu.CompilerParams(has_side_effects=_DATAFLOW),
    )(buf_thru, fwd_send, fwd_recv)


def _chip_exchange_start(partial):
    _, rows, cols = partial.shape
    landing = jax.ShapeDtypeStruct((3, rows, cols), partial.dtype)

    def body(src_ref, land_ref, send_sems, recv_sems, src_thru, land_thru, token):
        for cp in _chip_exchange_copies([src_ref], [land_ref], send_sems, recv_sems):
            cp.start()
        token[...] = jnp.zeros_like(token)

    return pl.pallas_call(
        body, name="grad_exchange_start",
        out_shape=(pltpu.SemaphoreType.DMA((3,)), pltpu.SemaphoreType.DMA((3,)),
                   pltpu.HBM(partial.shape, partial.dtype), pltpu.HBM(landing.shape, landing.dtype),
                   jax.ShapeDtypeStruct((SUBLANES, LANES), F32)),
        in_specs=(_HBM, _HBM), out_specs=(_SEM, _SEM, _HBM, _HBM, _VMEM), input_output_aliases={0: 2, 1: 3},
        compiler_params=pltpu.CompilerParams(has_side_effects=_DATAFLOW),
    )(pltpu.with_memory_space_constraint(partial, pltpu.HBM),
      pltpu.with_memory_space_constraint(lax.empty(landing.shape, landing.dtype), pltpu.HBM))


def _chip_exchange_wait(send_sems, recv_sems, src_thru, land_thru, after):
    def body(src_ref, land_ref, send_sems, recv_sems, after_ref, src_out, land_out):
        for cp in _chip_exchange_copies([src_ref], [land_ref], send_sems, recv_sems):
            cp.wait_send()
            cp.wait_recv()

    return pl.pallas_call(
        body, name="grad_exchange_wait",
        out_shape=(pltpu.HBM(src_thru.shape, src_thru.dtype), pltpu.HBM(land_thru.shape, land_thru.dtype)),
        in_specs=(_HBM, _HBM, _SEM, _SEM, _ANY), out_specs=(_HBM, _HBM), input_output_aliases={0: 0, 1: 1},
        compiler_params=pltpu.CompilerParams(has_side_effects=_DATAFLOW),
    )(src_thru, land_thru, send_sems, recv_sems, after)


def _sibling_share_copies(refs, send_sems, recv_sems):
    x, y, c, _ = _mesh_position()
    cps = []
    for a, ref in enumerate(refs):
        mine = ref.at[0, _half_rows(c, ref.shape[1] // 2)]
        cps.append(_remote_copy(a, mine, mine, (x, y, 1 - c), send_sems, recv_sems))
    return cps


def _add_sibling(name, slabs, received, core):
    na = len(slabs)
    halves = [a.shape[1] // 2 for a in slabs]

    def body(core_ref, *refs):
        for a in range(na):
            refs[2 * na + a][...] = (refs[a][...].astype(F32) + refs[na + a][...].astype(F32)).astype(BF16)

    def block(a, mine):
        if mine:
            return pl.BlockSpec((None, halves[a], slabs[a].shape[2]), lambda s, core_ref: (s, core_ref[0], 0))
        return pl.BlockSpec((None, halves[a], slabs[a].shape[2]), lambda s, core_ref: (s, 0, 0))

    grid_spec = pltpu.PrefetchScalarGridSpec(
        num_scalar_prefetch=1, grid=(N_CHIPS,),
        in_specs=[block(a, True) for a in range(na)] + [block(a, False) for a in range(na)],
        out_specs=[block(a, False) for a in range(na)])
    return pl.pallas_call(body, name=name, grid_spec=grid_spec,
                          out_shape=[jax.ShapeDtypeStruct(r.shape, BF16) for r in received],
                          compiler_params=_params(1))(core, *slabs, *received)


def _sum_chips(name, partials, received, chip_core):
    na = len(partials)

    def body(cc_ref, *refs):
        for a in range(na):
            acc = refs[a][...].astype(F32)
            for k in range(3):
                acc = acc + refs[na + a][k].astype(F32)
            refs[2 * na + a][...] = acc

    def own(p):
        return pl.BlockSpec((None,) + p.shape[1:], lambda i, cc_ref: (cc_ref[0], 0, 0))

    def mine(p):
        return pl.BlockSpec((None,) + p.shape[1:], lambda i, cc_ref: (0, cc_ref[1], 0))

    grid_spec = pltpu.PrefetchScalarGridSpec(
        num_scalar_prefetch=1, grid=(1,),
        in_specs=[own(p) for p in partials] + [pl.BlockSpec(r.shape, lambda i, cc_ref: (0, 0, 0)) for r in received],
        out_specs=[mine(p) for p in partials])
    return pl.pallas_call(body, name=name, grid_spec=grid_spec,
                          out_shape=[jax.ShapeDtypeStruct((1, 2 * p.shape[1], p.shape[2]), F32) for p in partials],
                          compiler_params=_params(1))(chip_core, *partials, *received)


def _adam_math(w, g, m, v):
    nm = ADAM_B1 * m + (1.0 - ADAM_B1) * g
    nv = ADAM_B2 * v + (1.0 - ADAM_B2) * jnp.square(g)
    m_hat = nm / (1.0 - ADAM_B1 ** ADAM_STEP)
    v_hat = nv / (1.0 - ADAM_B2 ** ADAM_STEP)
    delta = -ADAM_LR * (m_hat / (jnp.sqrt(v_hat) + ADAM_EPS) + ADAM_WD * w)
    return delta, nm, nv


ADAMW_STEPS = 8


def _adamw(ws, gs, ms, vs):
    na = len(ws)

    def body(*refs):
        for a in range(na):
            w_ref, g_ref, m_ref, v_ref = (refs[k * na + a] for k in range(4))
            g_out_ref, d_ref, nm_ref, nv_ref = (refs[(4 + k) * na + a] for k in range(4))
            gv = g_ref[...]
            g_out_ref[...] = gv
            d_ref[...], nm_ref[...], nv_ref[...] = _adam_math(w_ref[...], gv, m_ref[...], v_ref[...])

    specs = [pl.BlockSpec((None, w.shape[1] // ADAMW_STEPS, w.shape[2]), lambda i: (0, i, 0)) for w in ws]
    outs = pl.pallas_call(
        body, name="adamw", grid=(ADAMW_STEPS,), out_shape=[jax.ShapeDtypeStruct(w.shape, F32) for w in ws] * 4,
        in_specs=specs * 4, out_specs=specs * 4, compiler_params=_params(1))(*ws, *gs, *ms, *vs)
    return [[outs[k * na + a] for k in range(4)] for a in range(na)]


SMALL_PARAMS = ("norm_mix_g", "b_gate", "conv_a_w", "conv_a_b", "norm_ffn_g", "ffn_conv_w", "ffn_conv_b", "final_norm_g")


def _small_update(partials, params, moments_m, moments_v, halves):
    na = len(partials)
    npar = len(SMALL_PARAMS)
    nh = len(halves)

    def body(*refs):
        in_refs = refs[:na]
        w_refs = refs[na:na + npar]
        m_refs = refs[na + npar:na + 2 * npar]
        v_refs = refs[na + 2 * npar:na + 3 * npar]
        pos = na + 3 * npar + nh
        loss_ref = refs[pos]
        out_refs = refs[pos + 1:pos + 1 + 4 * npar]
        big_refs = refs[pos + 1 + 4 * npar:pos + 1 + 4 * npar + nh]
        pos += 1 + 4 * npar + nh
        acc_refs = refs[pos:pos + na]
        recv_refs = refs[pos + na:pos + 4 * na]
        send_sems, recv_sems, share_send, share_recv = refs[pos + 4 * na:]
        x, y, c, _ = _mesh_position()
        chip = 2 * x + y
        for cp in _sibling_share_copies(big_refs, share_send, share_recv):
            cp.start()
        for a in range(na):
            acc_refs[a][...] = in_refs[a][...]
        for stage, peer in enumerate(((x, y, 1 - c), (x, 1 - y, c), (1 - x, y, c))):
            cps = []
            for a in range(na):
                k = stage * na + a
                cps.append(pltpu.make_async_remote_copy(src_ref=acc_refs[a], dst_ref=recv_refs[k], send_sem=send_sems.at[k],
                                                        recv_sem=recv_sems.at[k], device_id=peer, device_id_type=MESH_ID))
            for cp in cps:
                cp.start()
            for cp in cps:
                cp.wait()
            for a in range(na):
                acc_refs[a][...] = acc_refs[a][...] + recv_refs[stage * na + a][...]

        mix, ffn, fin, gate, conv, ffnc, loss = acc_refs
        loss_ref[...] = loss[...]

        def cols(width):
            return pl.ds(pl.multiple_of(chip * width, LANES), width)

        grads = {
            "norm_mix_g": mix[...], "norm_ffn_g": ffn[...], "final_norm_g": fin[...],
            "b_gate": gate[0:2, cols(D_MODEL // N_CHIPS)],
            "conv_a_w": conv[0:3, cols(CONV_WIDTH // N_CHIPS)], "conv_a_b": conv[3:4, :],
            "ffn_conv_w": ffnc[0:3, cols(2 * D_FF // N_CHIPS)], "ffn_conv_b": ffnc[3:4, :]}
        for i, name in enumerate(SMALL_PARAMS):
            g = grads[name]
            if len(w_refs[i].shape) == 3:
                results = (g,) + _adam_math(w_refs[i][0], g, m_refs[i][0], v_refs[i][0])
                for o_ref, val in zip(out_refs[4 * i:4 * i + 4], results):
                    o_ref[0] = val
            else:
                results = (g,) + _adam_math(w_refs[i][...], g, m_refs[i][...], v_refs[i][...])
                for o_ref, val in zip(out_refs[4 * i:4 * i + 4], results):
                    o_ref[...] = val

        for cp in _sibling_share_copies(big_refs, share_send, share_recv):
            cp.wait()

    outs = [jax.ShapeDtypeStruct(partials[-1].shape, F32)]
    for w in params:
        outs += [jax.ShapeDtypeStruct(w.shape, F32)] * 4
    n_small_out = len(outs)
    outs += [jax.ShapeDtypeStruct(h.shape, h.dtype) for h in halves]
    scratch = [pltpu.VMEM(p.shape, F32) for p in partials]
    scratch += [pltpu.VMEM(p.shape, F32) for _ in range(3) for p in partials]
    scratch += _dma_sems(3 * na) + _dma_sems(nh)
    n_in = na + 3 * npar
    return pl.pallas_call(
        body, name="small_update", out_shape=outs, in_specs=[_VMEM] * n_in + [_ANY] * nh,
        out_specs=[_VMEM] * n_small_out + [_ANY] * nh,
        input_output_aliases={n_in + a: n_small_out + a for a in range(nh)},
        scratch_shapes=scratch)(*partials, *params, *moments_m, *moments_v, *halves)


def _gathered_columns(g):
    return jnp.transpose(g, (1, 0, 2)).reshape(g.shape[1], N_CHIPS * g.shape[2])


def kernel(x, norm_mix_g, w_in, b_gate, conv_a_w, conv_a_b, w_proj_a, w_proj_b, w_out, norm_ffn_g, w_up, ffn_conv_w, ffn_conv_b, w_down, final_norm_g, loss_target, m_norm_mix_g, m_w_in, m_b_gate, m_conv_a_w, m_conv_a_b, m_w_proj_a, m_w_proj_b, m_w_out, m_norm_ffn_g, m_w_up, m_ffn_conv_w, m_ffn_conv_b, m_w_down, m_final_norm_g, v_norm_mix_g, v_w_in, v_b_gate, v_conv_a_w, v_conv_a_b, v_w_proj_a, v_w_proj_b, v_w_out, v_norm_ffn_g, v_w_up, v_ffn_conv_w, v_ffn_conv_b, v_w_down, v_final_norm_g):
    chip = (2 * lax.axis_index("x") + lax.axis_index("y")).astype(jnp.int32)
    core = lax.axis_index("c").astype(jnp.int32)
    core_arr = core.reshape(1)
    chip_core = jnp.stack([chip, core])
    xs, target = x[0], loss_target[0]
    g_final = final_norm_g.reshape(1, D_MODEL)

    def own_slot(gathered, own):
        return lax.dynamic_update_slice(gathered, own, (chip, 0, 0))

    w_in_t, m_w_in_t, v_w_in_t = (jnp.swapaxes(a, 1, 2) for a in (w_in, m_w_in, v_w_in))
    w_in_tb = w_in_t.astype(BF16)
    send1, recv1, shard_thru, g_in, token = _gather_start(w_in_tb)
    h1, h1_streams4, h1_streams16 = _norm_fwd(xs, norm_mix_g + token[0:1, 0:1])
    send2, recv2, g_in = _gather_forward(send1, recv1, shard_thru, g_in, h1)
    g_in = _gather_wait(send2, recv2, g_in)
    w_in_full_t = own_slot(g_in, w_in_tb).reshape(D_IN, D_MODEL)
    later_w = [w_proj_a, w_proj_b, w_out, w_up, w_down]
    later_b = [w.astype(BF16) for w in later_w]
    small_sharded = [b_gate, conv_a_w, ffn_conv_w]
    fwd = _inproj_fwd(h1, w_in_full_t, later_b, small_sharded)
    abcv, gates, qkv0, qkv1, qkv2 = fwd[:5]
    gathered_big, gathered_small = fwd[5:10], fwd[10:13]
    attn0 = _attn_fwd(qkv0, 0, forward=gathered_big)
    attn = [attn0[:2], _attn_fwd(qkv1, 1), _attn_fwd(qkv2, 2)]
    g_pa, g_pb, g_out, g_up, g_down = [own_slot(g, own) for g, own in zip(attn0[2:], later_b)]
    g_bgate, g_convw, g_ffnw = [own_slot(g, own) for g, own in zip(gathered_small, small_sharded)]
    w_pa_full, w_pb_full, w_up_full = _gathered_columns(g_pa), _gathered_columns(g_pb), _gathered_columns(g_up)
    w_out_full, w_down_full = g_out.reshape(D_MODEL, D_MODEL), g_down.reshape(D_FF, D_MODEL)
    b_gate_full, conv_w_full, ffn_w_full = (_gathered_columns(g) for g in (g_bgate, g_convw, g_ffnw))

    x1, ya0, yb0, mrg, ya, yb, lsetot = _mix_fwd(
        xs, abcv, gates, [a[0] for a in attn], [a[1] for a in attn], conv_w_full, conv_a_b, b_gate_full,
        w_pa_full, w_pb_full, w_out_full)
    h2, up0, up = _ffn_up_fwd(x1, norm_ffn_g, w_up_full, ffn_w_full, ffn_conv_b)
    act, dx2, dx2b, d_g_final, loss = _ffn_act_fwd(x1, up, target, w_down_full, g_final)

    d_up = _ffn_act_bwd(dx2b, up, w_down_full)
    slab_down = _matmul_tn("dw_down", act, dx2b, col_tile=512).reshape(N_CHIPS, D_FF // N_CHIPS, D_MODEL)
    dx1, dx1b, d_g_ffn, d_up0, ffn_small, swapped_down = _ffn_up_bwd(
        d_up, up0, w_up_full, ffn_w_full, x1, norm_ffn_g, dx2, swap=[slab_down])
    (partial_down,) = _add_sibling("grad_add_w_down", [slab_down], [swapped_down], core_arr)
    slab_up = _matmul_tn("dw_up", h2, d_up0, col_tile=2 * D_FF // N_CHIPS, slabs=N_CHIPS)
    d_w_out, swapped_up = _matmul_tn("dw_out", mrg, dx1b, swap=[slab_up])
    (partial_up,) = _add_sibling("grad_add_w_up", [slab_up], [swapped_up], core_arr)

    mix_res = _mix_bwd(dx1, abcv, gates, ya, yb, yb0, lsetot, conv_w_full, conv_a_b, b_gate_full,
                       w_pa_full, w_pb_full, w_out_full, exchange=[partial_up, partial_down])
    (d_ya, d_yb, d_gates, d_abcv, d_yb0, dyl0, dyl1, dyl2, aux0, aux1, aux2, gate_small, conv_small) = mix_res[:13]
    halves_ffn = _sum_chips("grad_sum_ffn", [partial_up, partial_down], mix_res[13:], chip_core)

    slabs_mix = [_matmul_tn("dw_proj_a", ya0, d_ya, slabs=N_CHIPS), _matmul_tn("dw_proj_b", yb0, d_yb, slabs=N_CHIPS),
                 d_w_out.reshape(N_CHIPS, D_MODEL // N_CHIPS, D_MODEL)]
    res0 = _attn_bwd(qkv0, dyl0, aux0, 0, swap=slabs_mix)
    d_qkv0, partials_mix = res0[0], _add_sibling("grad_add_mix", slabs_mix, res0[1:], core_arr)
    res1 = _attn_bwd(qkv1, dyl1, aux1, 1, exchange=partials_mix)
    d_qkv1, halves_mix = res1[0], _sum_chips("grad_sum_mix", partials_mix, res1[1:], chip_core)
    (d_qkv2,) = _attn_bwd(qkv2, dyl2, aux2, 2)

    dq = [d_qkv0, d_qkv1, d_qkv2]
    seq = xs.shape[0]
    d_w_abcv = _matmul_tn("dw_in_abcv", d_abcv, h1)
    d_w_gates = _matmul_tn("dw_in_gates", d_gates, h1)
    d_w_groups = [_matmul_tn(f"dw_in_qkv{g}", t.reshape(seq, 3 * GROUP_WIDTH), h.reshape(seq, D_MODEL))
                  for g, (t, h) in enumerate(zip(dq, (h1, h1_streams4, h1_streams16)))]
    gw = GROUP_WIDTH
    d_w_in_t = jnp.concatenate(
        [d_w_abcv] + [d_w_groups[g][j * gw:(j + 1) * gw] for j in range(3) for g in range(3)] + [d_w_gates], axis=0)

    slab_in = d_w_in_t.reshape(N_CHIPS, D_IN // N_CHIPS, D_MODEL)
    (from_sibling_in,) = _sibling_swap_halves("grad_swap_w_in", [slab_in])
    (partial_in,) = _add_sibling("grad_add_w_in", [slab_in], [from_sibling_in], core_arr)
    send_sems, recv_sems, partial_thru, landing_thru, token = _chip_exchange_start(partial_in)
    g_mix_after_start = norm_mix_g + token[0:1, 0:1]
    grad_x, d_g_mix = _inproj_bwd(d_abcv, d_gates, dq, w_in_full_t, xs, g_mix_after_start, dx1)
    partial_in, received_in = _chip_exchange_wait(send_sems, recv_sems, partial_thru, landing_thru, d_g_mix)
    halves_in = _sum_chips("grad_sum_w_in", [partial_in], [received_in], chip_core)

    big_names = ("w_in", "w_proj_a", "w_proj_b", "w_out", "w_up", "w_down")
    big_w = dict(w_in=w_in_t, w_proj_a=w_proj_a, w_proj_b=w_proj_b, w_out=w_out, w_up=w_up, w_down=w_down)
    big_m = dict(w_in=m_w_in_t, w_proj_a=m_w_proj_a, w_proj_b=m_w_proj_b, w_out=m_w_out, w_up=m_w_up, w_down=m_w_down)
    big_v = dict(w_in=v_w_in_t, w_proj_a=v_w_proj_a, w_proj_b=v_w_proj_b, w_out=v_w_out, w_up=v_w_up, w_down=v_w_down)

    fin_w, fin_m, fin_v = (a.reshape(1, D_MODEL) for a in (final_norm_g, m_final_norm_g, v_final_norm_g))
    small_w = [norm_mix_g, b_gate, conv_a_w, conv_a_b, norm_ffn_g, ffn_conv_w, ffn_conv_b, fin_w]
    small_m = [m_norm_mix_g, m_b_gate, m_conv_a_w, m_conv_a_b, m_norm_ffn_g, m_ffn_conv_w, m_ffn_conv_b, fin_m]
    small_v = [v_norm_mix_g, v_b_gate, v_conv_a_w, v_conv_a_b, v_norm_ffn_g, v_ffn_conv_w, v_ffn_conv_b, fin_v]
    small_out = _small_update([d_g_mix, d_g_ffn, d_g_final, gate_small, conv_small, ffn_small, loss],
                              small_w, small_m, small_v, halves_in + halves_mix + halves_ffn)
    big_grads = small_out[1 + 4 * len(SMALL_PARAMS):]
    total_loss = small_out[0][0, 0]

    grads, delta, new_m, new_v = {}, {}, {}, {}
    for i, n in enumerate(SMALL_PARAMS):
        vals = small_out[1 + 4 * i:5 + 4 * i]
        if n == "final_norm_g":
            vals = [a.reshape(D_MODEL) for a in vals]
        grads[n], delta[n], new_m[n], new_v[n] = vals
    updates = _adamw([big_w[n] for n in big_names], big_grads, [big_m[n] for n in big_names],
                     [big_v[n] for n in big_names])
    for n, vals in zip(big_names, updates):
        if n == "w_in":
            vals = [jnp.swapaxes(a, 1, 2) for a in vals]
        grads[n], delta[n], new_m[n], new_v[n] = vals

    names = ["norm_mix_g", "w_in", "b_gate", "conv_a_w", "conv_a_b", "w_proj_a", "w_proj_b", "w_out", "norm_ffn_g", "w_up",
             "ffn_conv_w", "ffn_conv_b", "w_down", "final_norm_g"]
    out = [total_loss, grad_x[None]]
    for group in (grads, delta, new_m, new_v):
        out += [group[n] for n in names]
    return tuple(out)
```

```python
import jax
import jax.numpy as jnp
from jax import lax
from jax.experimental import pallas as pl
from jax.experimental.pallas import tpu as pltpu

F32 = jnp.float32
BF16 = jnp.bfloat16

D_MODEL = 1024
CONV_WIDTH = 512
ATTN_WIDTH = 768
GROUP_WIDTH = 256
HEAD_DIM = 64
HEADS_PER_GROUP = 4
DILATIONS = (1, 4, 16)
ATTN_BLOCK = 128
D_FF = 2816
D_IN = 5888
EPS = 1e-6
NEG_INF = -1e30
ATTN_SCALE = HEAD_DIM ** -0.5

COL_ABCV = 0
COL_Q = 1536
COL_K = 2304
COL_V = 3072
COL_GATES = 3840

ADAM_LR = 0.001
ADAM_B1 = 0.9
ADAM_B2 = 0.999
ADAM_EPS = 1e-08
ADAM_WD = 0.01
ADAM_STEP = 10

LANES = 128
SUBLANES = 8
BF16_ROWS = 16
ROW_TILE = 512
VMEM_LIMIT = 56 * 1024 * 1024

_NT = (((1,), (1,)), ((), ()))
_TN = (((0,), (0,)), ((), ()))


def _params(n_axes, vmem=VMEM_LIMIT):
    return pltpu.CompilerParams(dimension_semantics=("arbitrary",) * n_axes, vmem_limit_bytes=vmem)


def _resident(shape):
    nd = len(shape)
    return pl.BlockSpec(shape, lambda *_: (0,) * nd, pipeline_mode=pl.Buffered(1))


def _rows(tm, width, col_block=0):
    return pl.BlockSpec((tm, width), lambda i: (i, col_block))


def _col_chunks(n, cmax):
    out, lo = [], 0
    while lo < n:
        size = min(cmax, n - lo)
        out.append((lo, size))
        lo += size
    return out


def _dot(a, b):
    return jnp.dot(a, b, preferred_element_type=F32)


def _dot_nt(a, b):
    return lax.dot_general(a, b, _NT, preferred_element_type=F32)


def _dot_tn(a, b):
    return lax.dot_general(a, b, _TN, preferred_element_type=F32)


def _sigmoid(x):
    return 0.5 * jnp.tanh(0.5 * x) + 0.5


def _silu(x):
    hx = 0.5 * x
    return hx + hx * jnp.tanh(hx)


def _shift_down(v, k, halo8):
    tm = v.shape[0]
    rolled = pltpu.roll(v, k, 0)
    fix = jnp.tile(pltpu.roll(halo8, k, 0), (tm // SUBLANES, 1))
    row = lax.broadcasted_iota(jnp.int32, v.shape, 0)
    return jnp.where(row < k, fix, rolled)


def _shift_up(v, k, halo8):
    tm = v.shape[0]
    rolled = pltpu.roll(v, tm - k, 0)
    fix = jnp.tile(pltpu.roll(halo8, SUBLANES - k, 0), (tm // SUBLANES, 1))
    row = lax.broadcasted_iota(jnp.int32, v.shape, 0)
    return jnp.where(row >= tm - k, fix, rolled)


def _colsum(v):
    return jnp.sum(v, axis=0, keepdims=True)


def _to_streams(val, scr, out_ref, d, col0):
    tm = val.shape[0]
    panels = val.shape[1] // LANES
    if d == 1:
        out_ref[0, :, col0:col0 + val.shape[1]] = val.astype(out_ref.dtype)
        return
    for p in range(panels):
        scr[pl.ds(p * tm, tm), :] = val[:, p * LANES:(p + 1) * LANES]
    for r in range(d):
        for p in range(panels):
            piece = scr[pl.ds(p * tm + r, tm // d, stride=d), :]
            out_ref[r, :, col0 + p * LANES: col0 + (p + 1) * LANES] = piece.astype(out_ref.dtype)


def _from_streams(in_ref, scr, d, col0, width):
    panels = width // LANES
    rows = in_ref.shape[1]
    tm = rows * d
    if d == 1:
        return in_ref[0, :, col0:col0 + width].astype(F32)
    for r in range(d):
        for p in range(panels):
            scr[pl.ds(p * tm + r, rows, stride=d), :] = in_ref[r, :, col0 + p * LANES: col0 + (p + 1) * LANES].astype(F32)
    return jnp.concatenate([scr[pl.ds(p * tm, tm), :] for p in range(panels)], axis=1)


def _stream_block(tm, d, width):
    return pl.BlockSpec((d, tm // d, width), lambda i: (0, i, 0))


def _rev_stream_block(tm, d, width, nt):
    return pl.BlockSpec((d, tm // d, width), lambda i: (0, nt - 1 - i, 0))


N_CHIPS = 4
MESH_ID = pl.DeviceIdType.MESH
_ANY = pl.BlockSpec(memory_space=pl.ANY)
_VMEM = pl.BlockSpec(memory_space=pltpu.VMEM)


def _mesh_position():
    x, y, c = lax.axis_index("x"), lax.axis_index("y"), lax.axis_index("c")
    other_chips = [(1 - x, y), (x, 1 - y), (1 - x, 1 - y)]
    return x, y, c, other_chips


def _half_rows(c, half):
    return pl.ds(pl.multiple_of(c * half, BF16_ROWS), half)


def _remote_copy(k, src, dst, to, send_sems, recv_sems):
    return pltpu.make_async_remote_copy(src_ref=src, dst_ref=dst, send_sem=send_sems.at[k], recv_sem=recv_sems.at[k],
                                        device_id=to, device_id_type=MESH_ID)


def _gather_first_copies(big_refs, small_refs, big_outs, small_outs, send_sems, recv_sems):
    x, y, c, chips = _mesh_position()
    me = 2 * x + y
    nb = len(big_refs)
    cps = []
    for j, (px, py) in enumerate(chips):
        for b in range(nb):
            mine = _half_rows(c, big_refs[b].shape[1] // 2)
            cps.append(_remote_copy(3 * b + j, big_refs[b].at[0, mine], big_outs[b].at[me, mine], (px, py, c),
                                    send_sems, recv_sems))
        for s in range(len(small_refs)):
            cps.append(_remote_copy(3 * (nb + s) + j, small_refs[s].at[0], small_outs[s].at[me], (px, py, c),
                                    send_sems, recv_sems))
    return cps


def _gather_forward_copies(bufs, send_sems, recv_sems):
    x, y, c, chips = _mesh_position()
    cps = []
    for j, (px, py) in enumerate(chips):
        for b in range(len(bufs)):
            landed = bufs[b].at[2 * px + py, _half_rows(c, bufs[b].shape[1] // 2)]
            cps.append(_remote_copy(3 * b + j, landed, landed, (x, y, 1 - c), send_sems, recv_sems))
    return cps


def _chip_exchange_copies(src_refs, out_refs, send_sems, recv_sems):
    x, y, c, chips = _mesh_position()
    cps = []
    for j, (px, py) in enumerate(chips):
        for a in range(len(src_refs)):
            cps.append(_remote_copy(3 * a + j, src_refs[a].at[2 * px + py], out_refs[a].at[j], (px, py, c),
                                    send_sems, recv_sems))
    return cps


def _sibling_swap_copies(src_refs, out_refs, send_sems, recv_sems):
    x, y, c, _ = _mesh_position()
    cps = []
    for a in range(len(src_refs)):
        theirs = _half_rows(1 - c, src_refs[a].shape[1] // 2)
        cps.append(_remote_copy(a, src_refs[a].at[:, theirs, :], out_refs[a], (x, y, 1 - c), send_sems, recv_sems))
    return cps


def _swap_shapes(slabs):
    return [jax.ShapeDtypeStruct((a.shape[0], a.shape[1] // 2, a.shape[2]), a.dtype) for a in slabs]


def _dma_sems(n):
    return [pltpu.SemaphoreType.DMA((n,)), pltpu.SemaphoreType.DMA((n,))]


def _norm_fwd(x, g):
    s = x.shape[0]
    tm = ROW_TILE

    def body(x_ref, g_ref, h_ref, hs1_ref, hs2_ref, scr):
        xv = x_ref[...]
        r = lax.rsqrt(jnp.mean(xv * xv, axis=-1, keepdims=True) + EPS)
        hf = xv * r * g_ref[...]
        h_ref[...] = hf.astype(BF16)
        for d, hs_ref in zip(DILATIONS[1:], (hs1_ref, hs2_ref)):
            for lo, size in _col_chunks(D_MODEL, GROUP_WIDTH):
                _to_streams(hf[:, lo:lo + size], scr, hs_ref, d, lo)

    return pl.pallas_call(
        body, name="norm_fwd", grid=(s // tm,),
        out_shape=[jax.ShapeDtypeStruct((s, D_MODEL), BF16)]
        + [jax.ShapeDtypeStruct((d, s // d, D_MODEL), BF16) for d in DILATIONS[1:]],
        in_specs=[_rows(tm, D_MODEL), _resident((1, D_MODEL))],
        out_specs=[_rows(tm, D_MODEL)] + [_stream_block(tm, d, D_MODEL) for d in DILATIONS[1:]],
        scratch_shapes=[pltpu.VMEM((GROUP_WIDTH // LANES * tm, LANES), F32)],
        compiler_params=_params(1))(x, g)


def _inproj_fwd(h1, w_in_t, big_shards, small_shards):
    s = h1.shape[0]
    tm = ROW_TILE
    nt = s // tm
    nb, ns = len(big_shards), len(small_shards)
    n_fixed_in, n_fixed_out = 2, 5

    def body(*refs):
        h_ref, w_ref = refs[:n_fixed_in]
        shard_refs = refs[n_fixed_in:n_fixed_in + nb + ns]
        pos = n_fixed_in + nb + ns
        abcv_ref, gates_ref, qkv0_ref, qkv1_ref, qkv2_ref = refs[pos:pos + n_fixed_out]
        gathered_refs = refs[pos + n_fixed_out:pos + n_fixed_out + nb + ns]
        scr, send_sems, recv_sems = refs[pos + n_fixed_out + nb + ns:]
        i = pl.program_id(0)

        def gather_copies():
            return _gather_first_copies(shard_refs[:nb], shard_refs[nb:], gathered_refs[:nb], gathered_refs[nb:],
                                        send_sems, recv_sems)

        @pl.when(i == 0)
        def _():
            for cp in gather_copies():
                cp.start()

        h = h_ref[...]
        for lo, size in _col_chunks(3 * CONV_WIDTH, 512):
            abcv_ref[:, lo:lo + size] = _dot_nt(h, w_ref[COL_ABCV + lo: COL_ABCV + lo + size, :]).astype(BF16)
        for lo, size in _col_chunks(2 * D_MODEL, 512):
            gates_ref[:, lo:lo + size] = _dot_nt(h, w_ref[COL_GATES + lo: COL_GATES + lo + size, :]).astype(BF16)
        for gi, (d, out_ref) in enumerate(zip(DILATIONS, (qkv0_ref, qkv1_ref, qkv2_ref))):
            for j, base in enumerate((COL_Q, COL_K, COL_V)):
                lo = base + gi * GROUP_WIDTH
                y = _dot_nt(h, w_ref[lo:lo + GROUP_WIDTH, :])
                if j == 0:
                    y = y * ATTN_SCALE
                _to_streams(y, scr, out_ref, d, j * GROUP_WIDTH)

        @pl.when(i == nt - 1)
        def _():
            for cp in gather_copies():
                cp.wait()

    outs = [jax.ShapeDtypeStruct((s, 3 * CONV_WIDTH), BF16), jax.ShapeDtypeStruct((s, 2 * D_MODEL), BF16)]
    outs += [jax.ShapeDtypeStruct((d, s // d, 3 * GROUP_WIDTH), BF16) for d in DILATIONS]
    outs += [jax.ShapeDtypeStruct((N_CHIPS,) + a.shape[1:], a.dtype) for a in list(big_shards) + list(small_shards)]
    return pl.pallas_call(
        body, name="inproj_fwd", grid=(nt,), out_shape=outs,
        in_specs=[_rows(tm, D_MODEL), _resident((D_IN, D_MODEL))] + [_ANY] * (nb + ns),
        out_specs=[_rows(tm, 3 * CONV_WIDTH), _rows(tm, 2 * D_MODEL)]
        + [_stream_block(tm, d, 3 * GROUP_WIDTH) for d in DILATIONS] + [_ANY] * (nb + ns),
        scratch_shapes=[pltpu.VMEM((GROUP_WIDTH // LANES * tm, LANES), F32)] + _dma_sems(3 * (nb + ns)),
        compiler_params=_params(1))(h1, w_in_t, *big_shards, *small_shards)


def _head_of_lane(shape):
    return lax.broadcasted_iota(jnp.int32, shape, 1) // HEAD_DIM


def _stack_heads(v):
    head = _head_of_lane(v.shape)
    return jnp.concatenate([jnp.where(head == h, v, jnp.zeros_like(v)) for h in range(HEADS_PER_GROUP)], axis=0)


def _unstack_heads(v):
    q = ATTN_BLOCK
    head = _head_of_lane((q, v.shape[1]))
    out = jnp.zeros((q, v.shape[1]), v.dtype)
    for h in range(HEADS_PER_GROUP):
        out = jnp.where(head == h, v[h * q:(h + 1) * q], out)
    return out


def _per_head_rows(col):
    q = ATTN_BLOCK
    head = _head_of_lane((q, GROUP_WIDTH))
    out = jnp.zeros((q, GROUP_WIDTH), col.dtype)
    for h in range(HEADS_PER_GROUP):
        out = jnp.where(head == h, col[h * q:(h + 1) * q], out)
    return out


def _compact_heads(v):
    lane = lax.broadcasted_iota(jnp.int32, (v.shape[0], LANES), 1)
    return jnp.where((lane & 32) == 0, v[:, 0:LANES], v[:, LANES:2 * LANES])


def _compact_head_col(v):
    lane = lax.broadcasted_iota(jnp.int32, v.shape, 1)
    head = ((lane >> 6) & 1) + 2 * ((lane >> 5) & 1)
    cols = [jnp.max(jnp.where(head == h, v, -jnp.inf), axis=1, keepdims=True) for h in range(HEADS_PER_GROUP)]
    return jnp.concatenate(cols, axis=0)


ATTN_FWD_BLOCKS = 16
ATTN_BWD_BLOCKS = 8


def _band_bias(first_block):
    rows = HEADS_PER_GROUP * ATTN_BLOCK
    qi = lax.broadcasted_iota(jnp.int32, (rows, 2 * ATTN_BLOCK), 0) % ATTN_BLOCK
    kj = lax.broadcasted_iota(jnp.int32, (rows, 2 * ATTN_BLOCK), 1)
    dist = qi + ATTN_BLOCK - kj
    valid = (dist >= 0) & (dist <= ATTN_BLOCK)
    if first_block:
        valid = valid & (kj >= ATTN_BLOCK)
    return jnp.where(valid, 0.0, NEG_INF).astype(F32)


def _store_band_biases(bias_ref):
    bias_ref[0] = _band_bias(False)
    bias_ref[1] = _band_bias(True)


def _attn_block_specs(g, nb, clamp_last=False):
    q = ATTN_BLOCK
    last = nb // g - 1

    def cur(col, width=GROUP_WIDTH):
        if clamp_last:
            return pl.BlockSpec((None, g * q, width), lambda r, n: (r, jnp.minimum(n, last), col))
        return pl.BlockSpec((None, g * q, width), lambda r, n: (r, n, col))

    def prev(col):
        if clamp_last:
            return pl.BlockSpec((None, q, GROUP_WIDTH), lambda r, n: (r, jnp.clip(n * g - 1, 0, nb - 1), col))
        return pl.BlockSpec((None, q, GROUP_WIDTH), lambda r, n: (r, jnp.maximum(n * g - 1, 0), col))

    return cur, prev


def _attn_fwd(qkv, gi, forward=()):
    d, length, _ = qkv.shape
    nb = length // ATTN_BLOCK
    q = ATTN_BLOCK
    g = min(ATTN_FWD_BLOCKS, nb)
    ns = nb // g
    nf = len(forward)

    def body(*refs):
        q_ref, kp_ref, kc_ref, vp_ref, vc_ref = refs[:5]
        o_ref, lse_ref = refs[5 + nf:7 + nf]
        buf_refs = refs[7 + nf:7 + 2 * nf]
        bias_ref = refs[7 + 2 * nf]
        sems = refs[8 + 2 * nf:]
        n = pl.program_id(1)
        first_step = (pl.program_id(0) == 0) & (n == 0)
        last_step = (pl.program_id(0) == d - 1) & (n == ns - 1)

        @pl.when(first_step)
        def _():
            _store_band_biases(bias_ref)
            for cp in _gather_forward_copies(buf_refs, *sems) if nf else ():
                cp.start()

        kfull = jnp.concatenate([kp_ref[...], kc_ref[...]], axis=0)
        vfull = jnp.concatenate([vp_ref[...], vc_ref[...]], axis=0)
        for j in range(g):
            qs = _stack_heads(q_ref[j * q:(j + 1) * q, :])
            k2 = kfull[j * q:(j + 2) * q]
            v2 = vfull[j * q:(j + 2) * q]
            bias = jnp.where(n == 0, bias_ref[1], bias_ref[0]) if j == 0 else bias_ref[0]
            sc = _dot_nt(qs, k2) + bias
            m = jnp.max(sc, axis=1, keepdims=True)
            p = jnp.exp(sc - m)
            l = jnp.sum(p, axis=1, keepdims=True)
            of = _dot(p.astype(BF16), v2) / l
            o_ref[j * q:(j + 1) * q, :] = _unstack_heads(of).astype(BF16)
            lse_ref[j * q:(j + 1) * q, :] = _per_head_rows(m + jnp.log(l))

        if nf:
            @pl.when(last_step)
            def _():
                for cp in _gather_forward_copies(buf_refs, *sems):
                    cp.wait()

    cur, prev = _attn_block_specs(g, nb)
    return pl.pallas_call(
        body, name=f"attn_fwd_g{gi}", grid=(d, ns),
        out_shape=[jax.ShapeDtypeStruct((d, length, GROUP_WIDTH), BF16),
                   jax.ShapeDtypeStruct((d, length, GROUP_WIDTH), F32)]
        + [jax.ShapeDtypeStruct(a.shape, a.dtype) for a in forward],
        in_specs=[cur(0), prev(1), cur(1), prev(2), cur(2)] + [_ANY] * nf,
        out_specs=[cur(0), cur(0)] + [_ANY] * nf,
        input_output_aliases={5 + a: 2 + a for a in range(nf)},
        scratch_shapes=[pltpu.VMEM((2, HEADS_PER_GROUP * q, 2 * q), F32)] + (_dma_sems(3 * nf) if nf else []),
        compiler_params=_params(2))(qkv, qkv, qkv, qkv, qkv, *forward)


def _conv_branch(ab, ac, av, halo_u, w, b):
    u = ac * av
    sh1 = _shift_down(u, 1, halo_u)
    sh2 = _shift_down(u, 2, halo_u)
    cv = w[0:1] * sh2 + w[1:2] * sh1 + w[2:3] * u + b
    return ab * cv, cv, u, sh1, sh2


def _mix_fwd(x, abcv, gates, o_list, lse_list, conv_w, conv_b, b_gate, w_pa, w_pb, w_out):
    s = x.shape[0]
    tm = ROW_TILE

    def body(x_ref, abcv_ref, gates_ref, o0_ref, o1_ref, o2_ref, l0_ref, l1_ref, l2_ref,
             cw_ref, cb_ref, bg_ref, wpa_ref, wpb_ref, wout_ref,
             x1_ref, ya0_ref, yb0_ref, mrg_ref, ya_ref, yb_ref, lsetot_ref, halo_ref, scr):
        i = pl.program_id(0)

        @pl.when(i == 0)
        def _():
            halo_ref[...] = jnp.zeros_like(halo_ref)

        ab = abcv_ref[:, 0:CONV_WIDTH].astype(F32)
        ac = abcv_ref[:, CONV_WIDTH:2 * CONV_WIDTH].astype(F32)
        av = abcv_ref[:, 2 * CONV_WIDTH:3 * CONV_WIDTH].astype(F32)
        ya0, _, u, _, _ = _conv_branch(ab, ac, av, halo_ref[...], cw_ref[...], cb_ref[...])
        halo_ref[...] = u[tm - SUBLANES:tm]
        ya0 = ya0.astype(BF16)
        ya0_ref[...] = ya0
        ya = _dot(ya0, wpa_ref[...])

        o_refs, l_refs = (o0_ref, o1_ref, o2_ref), (l0_ref, l1_ref, l2_ref)
        lses = [_from_streams(l_refs[g], scr, DILATIONS[g], 0, GROUP_WIDTH) for g in range(3)]
        top = jnp.maximum(jnp.maximum(lses[0], lses[1]), lses[2])
        weights = [jnp.exp(lse - top) for lse in lses]
        total = weights[0] + weights[1] + weights[2]
        lsetot_ref[...] = top + jnp.log(total)
        inv_total = 1.0 / total
        yb = jnp.zeros((tm, D_MODEL), F32)
        for g in range(3):
            og = _from_streams(o_refs[g], scr, DILATIONS[g], 0, GROUP_WIDTH)
            yb0 = (weights[g] * inv_total * og).astype(BF16)
            yb0_ref[:, g * GROUP_WIDTH:(g + 1) * GROUP_WIDTH] = yb0
            yb = yb + _dot(yb0, wpb_ref[g * GROUP_WIDTH:(g + 1) * GROUP_WIDTH, :])

        sa = _sigmoid(gates_ref[:, 0:D_MODEL].astype(F32) + bg_ref[0:1, :])
        sb = _sigmoid(gates_ref[:, D_MODEL:2 * D_MODEL].astype(F32) + bg_ref[1:2, :])
        ya_ref[...] = ya.astype(BF16)
        yb_ref[...] = yb.astype(BF16)
        mrg = (sa * ya + sb * yb).astype(BF16)
        mrg_ref[...] = mrg
        x1_ref[...] = x_ref[...] + _dot(mrg, wout_ref[...])

    outs = [jax.ShapeDtypeStruct((s, D_MODEL), F32),
            jax.ShapeDtypeStruct((s, CONV_WIDTH), BF16),
            jax.ShapeDtypeStruct((s, ATTN_WIDTH), BF16),
            jax.ShapeDtypeStruct((s, D_MODEL), BF16),
            jax.ShapeDtypeStruct((s, D_MODEL), BF16),
            jax.ShapeDtypeStruct((s, D_MODEL), BF16),
            jax.ShapeDtypeStruct((s, GROUP_WIDTH), F32)]
    return pl.pallas_call(
        body, name="mix_fwd", grid=(s // tm,), out_shape=outs,
        in_specs=[_rows(tm, D_MODEL), _rows(tm, 3 * CONV_WIDTH), _rows(tm, 2 * D_MODEL)]
        + [_stream_block(tm, d, GROUP_WIDTH) for d in DILATIONS] * 2
        + [_resident((3, CONV_WIDTH)), _resident((1, CONV_WIDTH)), _resident((2, D_MODEL)),
           _resident((CONV_WIDTH, D_MODEL)), _resident((ATTN_WIDTH, D_MODEL)), _resident((D_MODEL, D_MODEL))],
        out_specs=[_rows(tm, D_MODEL), _rows(tm, CONV_WIDTH), _rows(tm, ATTN_WIDTH), _rows(tm, D_MODEL),
                   _rows(tm, D_MODEL), _rows(tm, D_MODEL), _rows(tm, GROUP_WIDTH)],
        scratch_shapes=[pltpu.VMEM((SUBLANES, CONV_WIDTH), F32),
                        pltpu.VMEM((GROUP_WIDTH // LANES * tm, LANES), F32)],
        compiler_params=_params(1))(x, abcv, gates, *o_list, *lse_list, conv_w, conv_b, b_gate, w_pa, w_pb, w_out)


FFN_CHUNK = 512
FFN_UP_ROW_TILE = 256


def _ffn_up_fwd(x1, g, w_up, conv_w, conv_b):
    s = x1.shape[0]
    n = w_up.shape[1]
    tm = FFN_UP_ROW_TILE

    def body(x_ref, g_ref, w_ref, cw_ref, cb_ref, h_ref, up0_ref, up_ref, halo_ref):
        @pl.when(pl.program_id(0) == 0)
        def _():
            halo_ref[...] = jnp.zeros_like(halo_ref)

        xv = x_ref[...]
        r = lax.rsqrt(jnp.mean(xv * xv, axis=-1, keepdims=True) + EPS)
        h = (xv * r * g_ref[...]).astype(BF16)
        h_ref[...] = h
        for lo, size in _col_chunks(n, FFN_CHUNK):
            cols = slice(lo, lo + size)
            y = _dot(h, w_ref[:, cols])
            up0_ref[:, cols] = y.astype(BF16)
            halo = halo_ref[:, cols]
            w = cw_ref[:, cols]
            up = w[0:1] * _shift_down(y, 2, halo) + w[1:2] * _shift_down(y, 1, halo) + w[2:3] * y + cb_ref[:, cols]
            up_ref[:, cols] = up.astype(BF16)
            halo_ref[:, cols] = y[tm - SUBLANES:tm]

    return pl.pallas_call(
        body, name="ffn_up_fwd", grid=(s // tm,),
        out_shape=[jax.ShapeDtypeStruct((s, D_MODEL), BF16), jax.ShapeDtypeStruct((s, n), BF16),
                   jax.ShapeDtypeStruct((s, n), BF16)],
        in_specs=[_rows(tm, D_MODEL), _resident((1, D_MODEL)), _resident((D_MODEL, n)), _resident((3, n)),
                  _resident((1, n))],
        out_specs=[_rows(tm, D_MODEL), _rows(tm, n), _rows(tm, n)],
        scratch_shapes=[pltpu.VMEM((SUBLANES, n), F32)],
        compiler_params=_params(1))(x1, g, w_up, conv_w, conv_b)


def _ffn_act_fwd(x1, up, target, w_down, g_final):
    s = x1.shape[0]
    tm = ROW_TILE

    def body(x1_ref, up_ref, tgt_ref, wd_ref, gf_ref, act_ref, dx2_ref, dx2b_ref, dgf_ref, loss_ref):
        @pl.when(pl.program_id(0) == 0)
        def _():
            dgf_ref[...] = jnp.zeros_like(dgf_ref)
            loss_ref[...] = jnp.zeros_like(loss_ref)

        acc = jnp.zeros((tm, D_MODEL), F32)
        for lo, size in _col_chunks(D_FF, FFN_CHUNK):
            gate = up_ref[:, lo:lo + size].astype(F32)
            val = up_ref[:, D_FF + lo:D_FF + lo + size].astype(F32)
            act = (_silu(gate) * val).astype(BF16)
            act_ref[:, lo:lo + size] = act
            acc = acc + _dot(act, wd_ref[lo:lo + size, :])

        x2 = x1_ref[...] + acc
        r = lax.rsqrt(jnp.mean(x2 * x2, axis=-1, keepdims=True) + EPS)
        xn = x2 * r
        gf = gf_ref[...]
        err = xn * gf - tgt_ref[...]
        loss_ref[...] += (0.5 / D_MODEL) * jnp.sum(err * err)
        dy = err * (1.0 / D_MODEL)
        dgf_ref[...] += _colsum(dy * xn)
        dxn = dy * gf
        dx2 = r * (dxn - xn * jnp.mean(dxn * xn, axis=-1, keepdims=True))
        dx2_ref[...] = dx2
        dx2b_ref[...] = dx2.astype(BF16)

    return pl.pallas_call(
        body, name="ffn_act_fwd", grid=(s // tm,),
        out_shape=[jax.ShapeDtypeStruct((s, D_FF), BF16), jax.ShapeDtypeStruct((s, D_MODEL), F32),
                   jax.ShapeDtypeStruct((s, D_MODEL), BF16),
                   jax.ShapeDtypeStruct((1, D_MODEL), F32), jax.ShapeDtypeStruct((1, LANES), F32)],
        in_specs=[_rows(tm, D_MODEL), _rows(tm, 2 * D_FF), _rows(tm, D_MODEL),
                  _resident((D_FF, D_MODEL)), _resident((1, D_MODEL))],
        out_specs=[_rows(tm, D_FF), _rows(tm, D_MODEL), _rows(tm, D_MODEL),
                   pl.BlockSpec((1, D_MODEL), lambda i: (0, 0)), pl.BlockSpec((1, LANES), lambda i: (0, 0))],
        compiler_params=_params(1))(x1, up, target, w_down, g_final)


def _ffn_act_bwd(dx2b, up, w_down):
    s = dx2b.shape[0]
    tm = ROW_TILE

    def body(dx2_ref, up_ref, wd_ref, dup_ref):
        dx2 = dx2_ref[...]
        for lo, size in _col_chunks(D_FF, FFN_CHUNK):
            gate = up_ref[:, lo:lo + size].astype(F32)
            val = up_ref[:, D_FF + lo:D_FF + lo + size].astype(F32)
            dact = _dot_nt(dx2, wd_ref[lo:lo + size, :])
            sg = _sigmoid(gate)
            dup_ref[:, lo:lo + size] = (dact * val * (sg * (1.0 + gate * (1.0 - sg)))).astype(BF16)
            dup_ref[:, D_FF + lo:D_FF + lo + size] = (dact * (gate * sg)).astype(BF16)

    return pl.pallas_call(
        body, name="ffn_act_bwd", grid=(s // tm,),
        out_shape=jax.ShapeDtypeStruct((s, 2 * D_FF), BF16),
        in_specs=[_rows(tm, D_MODEL), _rows(tm, 2 * D_FF), _resident((D_FF, D_MODEL))],
        out_specs=_rows(tm, 2 * D_FF),
        compiler_params=_params(1))(dx2b, up, w_down)


def _ffn_up_bwd(d_up, up0, w_up, conv_w, x1, g, dres, swap=()):
    s = x1.shape[0]
    n = w_up.shape[1]
    tm = FFN_UP_ROW_TILE
    nt = s // tm
    nw = len(swap)

    def body(*refs):
        dup_ref, up0_ref, w_ref, cw_ref, x_ref, g_ref, dres_ref = refs[:7]
        slab_refs = refs[7:7 + nw]
        dx_ref, dxb_ref, dg_ref, dup0_ref, small_ref = refs[7 + nw:12 + nw]
        swapped_refs = refs[12 + nw:12 + 2 * nw]
        next_ref = refs[12 + 2 * nw]
        sems = refs[13 + 2 * nw:]

        @pl.when(pl.program_id(0) == 0)
        def _():
            next_ref[...] = jnp.zeros_like(next_ref)
            small_ref[...] = jnp.zeros_like(small_ref)
            dg_ref[...] = jnp.zeros_like(dg_ref)
            for cp in _sibling_swap_copies(slab_refs, swapped_refs, *sems) if nw else ():
                cp.start()

        dh = jnp.zeros((tm, D_MODEL), F32)
        for lo, size in _col_chunks(n, FFN_CHUNK):
            cols = slice(lo, lo + size)
            dz = dup_ref[:, cols].astype(F32)
            x0 = up0_ref[:, cols].astype(F32)
            nxt = next_ref[:, cols]
            dz1 = _shift_up(dz, 1, nxt)
            dz2 = _shift_up(dz, 2, nxt)
            next_ref[:, cols] = dz[0:SUBLANES]
            small_ref[0:1, cols] += _colsum(dz2 * x0)
            small_ref[1:2, cols] += _colsum(dz1 * x0)
            small_ref[2:3, cols] += _colsum(dz * x0)
            small_ref[3:4, cols] += _colsum(dz)
            w = cw_ref[:, cols]
            dup0 = (w[2:3] * dz + w[1:2] * dz1 + w[0:1] * dz2).astype(BF16)
            dup0_ref[:, cols] = dup0
            dh = dh + _dot_nt(dup0, w_ref[:, cols])
        xv = x_ref[...]
        r = lax.rsqrt(jnp.mean(xv * xv, axis=-1, keepdims=True) + EPS)
        xn = xv * r
        dg_ref[...] += _colsum(dh * xn)
        dxn = dh * g_ref[...]
        dx = dres_ref[...] + r * (dxn - xn * jnp.mean(dxn * xn, axis=-1, keepdims=True))
        dx_ref[...] = dx
        dxb_ref[...] = dx.astype(BF16)

        if nw:
            @pl.when(pl.program_id(0) == nt - 1)
            def _():
                for cp in _sibling_swap_copies(slab_refs, swapped_refs, *sems):
                    cp.wait()

    rows = lambda width: pl.BlockSpec((tm, width), lambda i: (nt - 1 - i, 0))
    return pl.pallas_call(
        body, name="ffn_up_bwd", grid=(nt,),
        out_shape=[jax.ShapeDtypeStruct((s, D_MODEL), F32), jax.ShapeDtypeStruct((s, D_MODEL), BF16),
                   jax.ShapeDtypeStruct((1, D_MODEL), F32), jax.ShapeDtypeStruct((s, n), BF16),
                   jax.ShapeDtypeStruct((SUBLANES, n), F32)] + _swap_shapes(swap),
        in_specs=[rows(n), rows(n), _resident((D_MODEL, n)), _resident((3, n)), rows(D_MODEL),
                  _resident((1, D_MODEL)), rows(D_MODEL)] + [_ANY] * nw,
        out_specs=[rows(D_MODEL), rows(D_MODEL), pl.BlockSpec((1, D_MODEL), lambda i: (0, 0)), rows(n),
                   pl.BlockSpec((SUBLANES, n), lambda i: (0, 0))] + [_ANY] * nw,
        scratch_shapes=[pltpu.VMEM((SUBLANES, n), F32)] + (_dma_sems(nw) if nw else []),
        compiler_params=_params(1))(d_up, up0, w_up, conv_w, x1, g, dres, *swap)


def _inproj_bwd(d_abcv, d_gates, d_qkvs, w_in_t, x, g, dres):
    s = x.shape[0]
    tm = ROW_TILE
    gw = GROUP_WIDTH

    def body(dabcv_ref, dgates_ref, dq0_ref, dq1_ref, dq2_ref, w_ref, x_ref, g_ref, dres_ref, dx_ref, dg_ref, scr):
        @pl.when(pl.program_id(0) == 0)
        def _():
            dg_ref[...] = jnp.zeros_like(dg_ref)

        dh = jnp.zeros((tm, D_MODEL), F32)
        for src, width, wrow in ((dabcv_ref, 3 * CONV_WIDTH, COL_ABCV), (dgates_ref, 2 * D_MODEL, COL_GATES)):
            for lo, size in _col_chunks(width, 512):
                dh = dh + _dot(src[:, lo:lo + size], w_ref[wrow + lo:wrow + lo + size, :])
        for gi, (d, dq_ref) in enumerate(zip(DILATIONS, (dq0_ref, dq1_ref, dq2_ref))):
            for j, base in enumerate((COL_Q, COL_K, COL_V)):
                dy = _from_streams(dq_ref, scr, d, j * gw, gw).astype(BF16)
                wrow = base + gi * gw
                dh = dh + _dot(dy, w_ref[wrow:wrow + gw, :])
        xv = x_ref[...]
        r = lax.rsqrt(jnp.mean(xv * xv, axis=-1, keepdims=True) + EPS)
        xn = xv * r
        dg_ref[...] += _colsum(dh * xn)
        dxn = dh * g_ref[...]
        dx_ref[...] = dres_ref[...] + r * (dxn - xn * jnp.mean(dxn * xn, axis=-1, keepdims=True))

    return pl.pallas_call(
        body, name="inproj_bwd", grid=(s // tm,),
        out_shape=[jax.ShapeDtypeStruct((s, D_MODEL), F32), jax.ShapeDtypeStruct((1, D_MODEL), F32)],
        in_specs=[_rows(tm, 3 * CONV_WIDTH), _rows(tm, 2 * D_MODEL)]
        + [_stream_block(tm, d, 3 * gw) for d in DILATIONS]
        + [_resident((D_IN, D_MODEL)), _rows(tm, D_MODEL), _resident((1, D_MODEL)), _rows(tm, D_MODEL)],
        out_specs=[_rows(tm, D_MODEL), pl.BlockSpec((1, D_MODEL), lambda i: (0, 0))],
        scratch_shapes=[pltpu.VMEM((gw // LANES * tm, LANES), F32)],
        compiler_params=_params(1))(d_abcv, d_gates, *d_qkvs, w_in_t, x, g, dres)


def _mix_bwd(dx1, abcv, gates, ya, yb, yb0, lsetot, conv_w, conv_b, b_gate, w_pa, w_pb, w_out, exchange=()):
    s = dx1.shape[0]
    tm = ROW_TILE
    nt = s // tm
    hb = tm // (2 * SUBLANES)
    nx = len(exchange)

    def body(*refs):
        (dx1_ref, abcv_ref, pre_ref, gates_ref, ya_ref, yb_ref, yb0_ref, lsetot_ref,
         cw_ref, cb_ref, bg_ref, wpa_ref, wpb_ref, wout_ref) = refs[:14]
        part_refs = refs[14:14 + nx]
        (dya_ref, dyb_ref, dgates_ref, dabcv_ref, dyb0_ref, dyl0_ref, dyl1_ref, dyl2_ref, aux0_ref, aux1_ref,
         aux2_ref, sm_gate_ref, sm_conv_ref) = refs[14 + nx:27 + nx]
        recv_refs = refs[27 + nx:27 + 2 * nx]
        next_ref, scr = refs[27 + 2 * nx:29 + 2 * nx]
        sems = refs[29 + 2 * nx:]
        i = pl.program_id(0)

        @pl.when(i == 0)
        def _():
            next_ref[...] = jnp.zeros_like(next_ref)
            sm_gate_ref[...] = jnp.zeros_like(sm_gate_ref)
            sm_conv_ref[...] = jnp.zeros_like(sm_conv_ref)
            for cp in _chip_exchange_copies(part_refs, recv_refs, *sems) if nx else ():
                cp.start()

        not_first = (i < nt - 1).astype(F32)
        dm = _dot_nt(dx1_ref[...].astype(BF16), wout_ref[...])
        sa = _sigmoid(gates_ref[:, 0:D_MODEL].astype(F32) + bg_ref[0:1, :])
        sb = _sigmoid(gates_ref[:, D_MODEL:2 * D_MODEL].astype(F32) + bg_ref[1:2, :])
        dya = (dm * sa).astype(BF16)
        dyb = (dm * sb).astype(BF16)
        dya_ref[...] = dya
        dyb_ref[...] = dyb
        dga = dm * ya_ref[...].astype(F32) * (sa * (1.0 - sa))
        dgb = dm * yb_ref[...].astype(F32) * (sb * (1.0 - sb))
        dgates_ref[:, 0:D_MODEL] = dga.astype(BF16)
        dgates_ref[:, D_MODEL:2 * D_MODEL] = dgb.astype(BF16)
        sm_gate_ref[0:1, :] += _colsum(dga)
        sm_gate_ref[1:2, :] += _colsum(dgb)

        dya0 = _dot_nt(dya, wpa_ref[...])
        ab = abcv_ref[:, 0:CONV_WIDTH].astype(F32)
        ac = abcv_ref[:, CONV_WIDTH:2 * CONV_WIDTH].astype(F32)
        av = abcv_ref[:, 2 * CONV_WIDTH:3 * CONV_WIDTH].astype(F32)
        pre = pre_ref[...].astype(F32) * not_first
        halo_u = (pre[:, CONV_WIDTH:2 * CONV_WIDTH] * pre[:, 2 * CONV_WIDTH:3 * CONV_WIDTH])[SUBLANES:2 * SUBLANES]
        w = cw_ref[...]
        _, cv, u, sh1, sh2 = _conv_branch(ab, ac, av, halo_u, w, cb_ref[...])
        dcv = dya0 * ab
        sm_conv_ref[0:1, :] += _colsum(dcv * sh2)
        sm_conv_ref[1:2, :] += _colsum(dcv * sh1)
        sm_conv_ref[2:3, :] += _colsum(dcv * u)
        sm_conv_ref[3:4, :] += _colsum(dcv)
        nxt = next_ref[...]
        du = w[2:3] * dcv + w[1:2] * _shift_up(dcv, 1, nxt) + w[0:1] * _shift_up(dcv, 2, nxt)
        next_ref[...] = dcv[0:SUBLANES]
        dabcv_ref[:, 0:CONV_WIDTH] = (dya0 * cv).astype(BF16)
        dabcv_ref[:, CONV_WIDTH:2 * CONV_WIDTH] = (du * av).astype(BF16)
        dabcv_ref[:, 2 * CONV_WIDTH:3 * CONV_WIDTH] = (du * ac).astype(BF16)

        head_r = lax.broadcasted_iota(jnp.int32, (GROUP_WIDTH, GROUP_WIDTH), 0) // HEAD_DIM
        head_c = lax.broadcasted_iota(jnp.int32, (GROUP_WIDTH, GROUP_WIDTH), 1) // HEAD_DIM
        same_head = (head_r == head_c).astype(BF16)
        prod = jnp.zeros((tm, GROUP_WIDTH), F32)
        dyb0s = []
        for g in range(3):
            cols = slice(g * GROUP_WIDTH, (g + 1) * GROUP_WIDTH)
            dyb0 = _dot_nt(dyb, wpb_ref[cols, :])
            dyb0_ref[:, cols] = dyb0.astype(BF16)
            dyb0s.append(dyb0)
            prod = prod + dyb0 * yb0_ref[:, cols].astype(F32)
        hi = prod.astype(BF16)
        mid = (prod - hi.astype(F32)).astype(BF16)
        lo = (prod - hi.astype(F32) - mid.astype(F32)).astype(BF16)
        delta = _dot(hi, same_head) + _dot(mid, same_head) + _dot(lo, same_head)
        lse_c = _compact_heads(lsetot_ref[...])
        delta_c = _compact_heads(delta)
        for g, (dy_ref, aux_ref) in enumerate(zip((dyl0_ref, dyl1_ref, dyl2_ref), (aux0_ref, aux1_ref, aux2_ref))):
            d = DILATIONS[g]
            _to_streams(dyb0s[g], scr, dy_ref, d, 0)
            _to_streams(lse_c, scr, aux_ref, d, 0)
            _to_streams(delta_c, scr, aux_ref, d, LANES)

        if nx:
            @pl.when(i == nt - 1)
            def _():
                for cp in _chip_exchange_copies(part_refs, recv_refs, *sems):
                    cp.wait()

    rev = lambda i: (nt - 1 - i, 0)
    pre = lambda i: (jnp.maximum((nt - 1 - i) * hb - 1, 0), 0)
    rows = lambda width: pl.BlockSpec((tm, width), rev)
    outs = [jax.ShapeDtypeStruct((s, D_MODEL), BF16), jax.ShapeDtypeStruct((s, D_MODEL), BF16),
            jax.ShapeDtypeStruct((s, 2 * D_MODEL), BF16), jax.ShapeDtypeStruct((s, 3 * CONV_WIDTH), BF16),
            jax.ShapeDtypeStruct((s, ATTN_WIDTH), BF16)]
    outs += [jax.ShapeDtypeStruct((d, s // d, GROUP_WIDTH), BF16) for d in DILATIONS]
    outs += [jax.ShapeDtypeStruct((d, s // d, 2 * LANES), F32) for d in DILATIONS]
    outs += [jax.ShapeDtypeStruct((SUBLANES, D_MODEL), F32), jax.ShapeDtypeStruct((SUBLANES, CONV_WIDTH), F32)]
    outs += [jax.ShapeDtypeStruct((3,) + a.shape[1:], a.dtype) for a in exchange]
    return pl.pallas_call(
        body, name="mix_bwd", grid=(nt,), out_shape=outs,
        in_specs=[rows(D_MODEL), rows(3 * CONV_WIDTH), pl.BlockSpec((2 * SUBLANES, 3 * CONV_WIDTH), pre),
                  rows(2 * D_MODEL), rows(D_MODEL), rows(D_MODEL), rows(ATTN_WIDTH), rows(GROUP_WIDTH),
                  _resident((3, CONV_WIDTH)), _resident((1, CONV_WIDTH)), _resident((2, D_MODEL)),
                  _resident((CONV_WIDTH, D_MODEL)), _resident((ATTN_WIDTH, D_MODEL)), _resident((D_MODEL, D_MODEL))]
        + [_ANY] * nx,
        out_specs=[rows(D_MODEL), rows(D_MODEL), rows(2 * D_MODEL), rows(3 * CONV_WIDTH), rows(ATTN_WIDTH)]
        + [_rev_stream_block(tm, d, GROUP_WIDTH, nt) for d in DILATIONS]
        + [_rev_stream_block(tm, d, 2 * LANES, nt) for d in DILATIONS]
        + [pl.BlockSpec((SUBLANES, D_MODEL), lambda i: (0, 0)), pl.BlockSpec((SUBLANES, CONV_WIDTH), lambda i: (0, 0))]
        + [_ANY] * nx,
        scratch_shapes=[pltpu.VMEM((SUBLANES, CONV_WIDTH), F32),
                        pltpu.VMEM((GROUP_WIDTH // LANES * tm, LANES), F32)] + (_dma_sems(3 * nx) if nx else []),
        compiler_params=_params(1))(dx1, abcv, abcv, gates, ya, yb, yb0, lsetot,
                                    conv_w, conv_b, b_gate, w_pa, w_pb, w_out, *exchange)


def _attn_bwd(qkv, dy, aux, gi, exchange=(), swap=()):
    d, length, _ = qkv.shape
    nb = length // ATTN_BLOCK
    q = ATTN_BLOCK
    gw = GROUP_WIDTH
    g = nb if nb <= ATTN_FWD_BLOCKS else ATTN_BWD_BLOCKS
    assert g >= 2 and nb % g == 0
    ns = nb // g
    lag = 1 if ns > 1 else 0
    tail = (g - 1) * q
    nx, nw = len(exchange), len(swap)

    def body(*refs):
        q_ref, kp_ref, kc_ref, vp_ref, vc_ref, dy_ref, aux_ref = refs[:7]
        part_refs = refs[7:7 + nx]
        slab_refs = refs[7 + nx:7 + nx + nw]
        pos = 7 + nx + nw
        out_ref = refs[pos]
        recv_refs = refs[pos + 1:pos + 1 + nx]
        swapped_refs = refs[pos + 1 + nx:pos + 1 + nx + nw]
        pos += 1 + nx + nw
        dq_ref, dkv_ref, bias_ref = refs[pos:pos + 3]
        sems = refs[pos + 3:]
        n = pl.program_id(1)

        def copies():
            cps = _chip_exchange_copies(part_refs, recv_refs, sems[0], sems[1]) if nx else []
            return cps + (_sibling_swap_copies(slab_refs, swapped_refs, sems[-2], sems[-1]) if nw else [])

        @pl.when((pl.program_id(0) == 0) & (n == 0))
        def _():
            _store_band_biases(bias_ref)
            for cp in copies():
                cp.start()

        if nx or nw:
            @pl.when((pl.program_id(0) == d - 1) & (n == ns - 1 + lag))
            def _():
                for cp in copies():
                    cp.wait()

        def emit(rows):
            out_ref[rows, gw:2 * gw] = dkv_ref[0, rows].astype(BF16)
            out_ref[rows, 2 * gw:3 * gw] = dkv_ref[1, rows].astype(BF16)

        if lag:
            @pl.when(n > 0)
            def _():
                out_ref[:, 0:gw] = dq_ref[...].astype(BF16)
                emit(slice(0, tail))

            @pl.when(n == ns)
            def _():
                emit(slice(tail, g * q))

        @pl.when(n < ns)
        def _():
            kfull = jnp.concatenate([kp_ref[...], kc_ref[...]], axis=0)
            vfull = jnp.concatenate([vp_ref[...], vc_ref[...]], axis=0)
            for j in range(g):
                rows = slice(j * q, (j + 1) * q)
                qs = _stack_heads(q_ref[rows, :])
                dys = _stack_heads(dy_ref[rows, :])
                k2 = kfull[j * q:(j + 2) * q]
                v2 = vfull[j * q:(j + 2) * q]
                lse = _compact_head_col(aux_ref[rows, 0:LANES])
                delta = _compact_head_col(aux_ref[rows, LANES:2 * LANES])
                bias = jnp.where(n == 0, bias_ref[1], bias_ref[0]) if j == 0 else bias_ref[0]
                p = jnp.exp(_dot_nt(qs, k2) + bias - lse)
                dp = _dot_nt(dys, v2)
                ds = (p * (dp - delta)).astype(BF16)
                dq_j = _unstack_heads(_dot(ds, k2)) * ATTN_SCALE
                dk2 = _dot_tn(ds, qs)
                dv2 = _dot_tn(p.astype(BF16), dys)
                if j == 0:
                    @pl.when(n > 0)
                    def _():
                        out_ref[tail:g * q, gw:2 * gw] = (dkv_ref[0, tail:g * q] + dk2[0:q]).astype(BF16)
                        out_ref[tail:g * q, 2 * gw:3 * gw] = (dkv_ref[1, tail:g * q] + dv2[0:q]).astype(BF16)
                else:
                    dkv_ref[0, (j - 1) * q:j * q] += dk2[0:q]
                    dkv_ref[1, (j - 1) * q:j * q] += dv2[0:q]
                dkv_ref[0, rows] = dk2[q:2 * q]
                dkv_ref[1, rows] = dv2[q:2 * q]
                dq_ref[rows, :] = dq_j
            if not lag:
                out_ref[:, 0:gw] = dq_ref[...].astype(BF16)
                emit(slice(0, g * q))

    cur, prev = _attn_block_specs(g, nb, clamp_last=True)
    return pl.pallas_call(
        body, name=f"attn_bwd_g{gi}", grid=(d, ns + lag),
        out_shape=[jax.ShapeDtypeStruct((d, length, 3 * gw), BF16)]
        + [jax.ShapeDtypeStruct((3,) + a.shape[1:], a.dtype) for a in exchange] + _swap_shapes(swap),
        in_specs=[cur(0), prev(1), cur(1), prev(2), cur(2), cur(0), cur(0, 2 * LANES)] + [_ANY] * (nx + nw),
        out_specs=[pl.BlockSpec((None, g * q, 3 * gw), lambda r, n: (r, jnp.maximum(n - lag, 0), 0))]
        + [_ANY] * (nx + nw),
        scratch_shapes=[pltpu.VMEM((g * q, gw), F32), pltpu.VMEM((2, g * q, gw), F32),
                        pltpu.VMEM((2, HEADS_PER_GROUP * q, 2 * q), F32)]
        + (_dma_sems(3 * nx) if nx else []) + (_dma_sems(nw) if nw else []),
        compiler_params=_params(2))(qkv, qkv, qkv, qkv, qkv, dy, aux, *exchange, *swap)


def _matmul_tn(name, a, b, col_tile=1024, row_tile=2048, slabs=0, swap=()):
    s, k = a.shape
    n = b.shape[1]
    tk = min(row_tile, s)
    tn = col_tile
    steps = s // tk
    nw = len(swap)

    def body(*refs):
        a_ref, b_ref = refs[:2]
        slab_refs = refs[2:2 + nw]
        o_ref = refs[2 + nw]
        swapped_refs = refs[3 + nw:3 + 2 * nw]
        acc_ref = refs[3 + 2 * nw]
        sems = refs[4 + 2 * nw:]
        t = pl.program_id(1)

        if nw:
            @pl.when((pl.program_id(0) == 0) & (t == 0))
            def _():
                for cp in _sibling_swap_copies(slab_refs, swapped_refs, *sems):
                    cp.start()

            @pl.when((pl.program_id(0) == n // tn - 1) & (t == steps - 1))
            def _():
                for cp in _sibling_swap_copies(slab_refs, swapped_refs, *sems):
                    cp.wait()

        @pl.when(t == 0)
        def _():
            acc_ref[...] = jnp.zeros_like(acc_ref)

        acc_ref[...] += _dot_tn(a_ref[...], b_ref[...])

        @pl.when(t == steps - 1)
        def _():
            if slabs:
                for q in range(per_tile):
                    o_ref[q] = acc_ref[:, q * width:(q + 1) * width].astype(BF16)
            else:
                o_ref[...] = acc_ref[...].astype(BF16)

    if slabs:
        width = n // slabs
        per_tile = tn // width
        out_shape = jax.ShapeDtypeStruct((slabs, k, width), BF16)
        out_spec = pl.BlockSpec((per_tile, k, width), lambda j, t: (j, 0, 0))
    else:
        out_shape = jax.ShapeDtypeStruct((k, n), BF16)
        out_spec = pl.BlockSpec((k, tn), lambda j, t: (0, j))
    res = pl.pallas_call(
        body, name=name, grid=(n // tn, steps), out_shape=[out_shape] + _swap_shapes(swap),
        in_specs=[pl.BlockSpec((tk, k), lambda j, t: (t, 0)), pl.BlockSpec((tk, tn), lambda j, t: (t, j))] + [_ANY] * nw,
        out_specs=[out_spec] + [_ANY] * nw,
        scratch_shapes=[pltpu.VMEM((k, tn), F32)] + (_dma_sems(nw) if nw else []),
        compiler_params=_params(2))(a, b, *swap)
    return res if nw else res[0]


def _sibling_swap_halves(name, slabs):
    na = len(slabs)

    def body(*refs):
        src_refs, out_refs = refs[:na], refs[na:2 * na]
        send_sems, recv_sems = refs[2 * na:]
        x, y, c, _ = _mesh_position()
        cps = []
        for a in range(na):
            theirs = _half_rows(1 - c, src_refs[a].shape[1] // 2)
            cps.append(pltpu.make_async_remote_copy(
                src_ref=src_refs[a].at[:, theirs, :], dst_ref=out_refs[a], send_sem=send_sems.at[a],
                recv_sem=recv_sems.at[a], device_id=(x, y, 1 - c), device_id_type=MESH_ID))
        for cp in cps:
            cp.start()
        for cp in cps:
            cp.wait()

    return pl.pallas_call(
        body, name=name,
        out_shape=[jax.ShapeDtypeStruct((a.shape[0], a.shape[1] // 2, a.shape[2]), a.dtype) for a in slabs],
        in_specs=[_ANY] * na, out_specs=[_ANY] * na,
        scratch_shapes=[pltpu.SemaphoreType.DMA((na,)), pltpu.SemaphoreType.DMA((na,))])(*slabs)


_HBM = pl.BlockSpec(memory_space=pltpu.HBM)
_SEM = pl.BlockSpec(memory_space=pltpu.SEMAPHORE)
_DATAFLOW = pltpu.SideEffectType.DATAFLOW_SIDE_EFFECTING


def _gather_start(shard):
    gathered = jax.ShapeDtypeStruct((N_CHIPS,) + shard.shape[1:], shard.dtype)

    def body(src_ref, buf_ref, send_sems, recv_sems, src_thru, buf_thru, token):
        for cp in _gather_first_copies([src_ref], [], [buf_ref], [], send_sems, recv_sems):
            cp.start()
        token[...] = jnp.zeros_like(token)

    return pl.pallas_call(
        body, name="gather_start",
        out_shape=(pltpu.SemaphoreType.DMA((3,)), pltpu.SemaphoreType.DMA((3,)),
                   pltpu.HBM(shard.shape, shard.dtype), pltpu.HBM(gathered.shape, gathered.dtype),
                   jax.ShapeDtypeStruct((SUBLANES, LANES), F32)),
        in_specs=(_HBM, _HBM), out_specs=(_SEM, _SEM, _HBM, _HBM, _VMEM), input_output_aliases={0: 2, 1: 3},
        compiler_params=pltpu.CompilerParams(has_side_effects=_DATAFLOW),
    )(pltpu.with_memory_space_constraint(shard, pltpu.HBM),
      pltpu.with_memory_space_constraint(lax.empty(gathered.shape, gathered.dtype), pltpu.HBM))


def _gather_forward(send_sems, recv_sems, shard_thru, buf_thru, after):
    def body(src_ref, buf_ref, send_sems, recv_sems, after_ref, fwd_send, fwd_recv, buf_out):
        first = _gather_first_copies([src_ref], [], [buf_ref], [], send_sems, recv_sems)
        for cp, fwd in zip(first, _gather_forward_copies([buf_ref], fwd_send, fwd_recv)):
            cp.wait_send()
            cp.wait_recv()
            fwd.start()

    return pl.pallas_call(
        body, name="gather_forward",
        out_shape=(pltpu.SemaphoreType.DMA((3,)), pltpu.SemaphoreType.DMA((3,)),
                   pltpu.HBM(buf_thru.shape, buf_thru.dtype)),
        in_specs=(_HBM, _HBM, _SEM, _SEM, _ANY), out_specs=(_SEM, _SEM, _HBM), input_output_aliases={1: 2},
        compiler_params=pltpu.CompilerParams(has_side_effects=_DATAFLOW),
    )(shard_thru, buf_thru, send_sems, recv_sems, after)


def _gather_wait(fwd_send, fwd_recv, buf_thru):
    def body(buf_ref, fwd_send, fwd_recv, buf_out):
        for cp in _gather_forward_copies([buf_ref], fwd_send, fwd_recv):
            cp.wait_send()
            cp.wait_recv()

    return pl.pallas_call(
        body, name="gather_wait", out_shape=pltpu.HBM(buf_thru.shape, buf_thru.dtype),
        in_specs=(_HBM, _SEM, _SEM), out_specs=_HBM, input_output_aliases={0: 0},
        compiler_params=pltpu.CompilerParams(has_side_effects=_DATAFLOW),
    )(buf_thru, fwd_send, fwd_recv)


def _chip_exchange_start(partial):
    _, rows, cols = partial.shape
    landing = jax.ShapeDtypeStruct((3, rows, cols), partial.dtype)

    def body(src_ref, land_ref, send_sems, recv_sems, src_thru, land_thru, token):
        for cp in _chip_exchange_copies([src_ref], [land_ref], send_sems, recv_sems):
            cp.start()
        token[...] = jnp.zeros_like(token)

    return pl.pallas_call(
        body, name="grad_exchange_start",
        out_shape=(pltpu.SemaphoreType.DMA((3,)), pltpu.SemaphoreType.DMA((3,)),
                   pltpu.HBM(partial.shape, partial.dtype), pltpu.HBM(landing.shape, landing.dtype),
                   jax.ShapeDtypeStruct((SUBLANES, LANES), F32)),
        in_specs=(_HBM, _HBM), out_specs=(_SEM, _SEM, _HBM, _HBM, _VMEM), input_output_aliases={0: 2, 1: 3},
        compiler_params=pltpu.CompilerParams(has_side_effects=_DATAFLOW),
    )(pltpu.with_memory_space_constraint(partial, pltpu.HBM),
      pltpu.with_memory_space_constraint(lax.empty(landing.shape, landing.dtype), pltpu.HBM))


def _chip_exchange_wait(send_sems, recv_sems, src_thru, land_thru, after):
    def body(src_ref, land_ref, send_sems, recv_sems, after_ref, src_out, land_out):
        for cp in _chip_exchange_copies([src_ref], [land_ref], send_sems, recv_sems):
            cp.wait_send()
            cp.wait_recv()

    return pl.pallas_call(
        body, name="grad_exchange_wait",
        out_shape=(pltpu.HBM(src_thru.shape, src_thru.dtype), pltpu.HBM(land_thru.shape, land_thru.dtype)),
        in_specs=(_HBM, _HBM, _SEM, _SEM, _ANY), out_specs=(_HBM, _HBM), input_output_aliases={0: 0, 1: 1},
        compiler_params=pltpu.CompilerParams(has_side_effects=_DATAFLOW),
    )(src_thru, land_thru, send_sems, recv_sems, after)


def _sibling_share_copies(refs, send_sems, recv_sems):
    x, y, c, _ = _mesh_position()
    cps = []
    for a, ref in enumerate(refs):
        mine = ref.at[0, _half_rows(c, ref.shape[1] // 2)]
        cps.append(_remote_copy(a, mine, mine, (x, y, 1 - c), send_sems, recv_sems))
    return cps


def _add_sibling(name, slabs, received, core):
    na = len(slabs)
    halves = [a.shape[1] // 2 for a in slabs]

    def body(core_ref, *refs):
        for a in range(na):
            refs[2 * na + a][...] = (refs[a][...].astype(F32) + refs[na + a][...].astype(F32)).astype(BF16)

    def block(a, mine):
        if mine:
            return pl.BlockSpec((None, halves[a], slabs[a].shape[2]), lambda s, core_ref: (s, core_ref[0], 0))
        return pl.BlockSpec((None, halves[a], slabs[a].shape[2]), lambda s, core_ref: (s, 0, 0))

    grid_spec = pltpu.PrefetchScalarGridSpec(
        num_scalar_prefetch=1, grid=(N_CHIPS,),
        in_specs=[block(a, True) for a in range(na)] + [block(a, False) for a in range(na)],
        out_specs=[block(a, False) for a in range(na)])
    return pl.pallas_call(body, name=name, grid_spec=grid_spec,
                          out_shape=[jax.ShapeDtypeStruct(r.shape, BF16) for r in received],
                          compiler_params=_params(1))(core, *slabs, *received)


def _sum_chips(name, partials, received, chip_core):
    na = len(partials)

    def body(cc_ref, *refs):
        for a in range(na):
            acc = refs[a][...].astype(F32)
            for k in range(3):
                acc = acc + refs[na + a][k].astype(F32)
            refs[2 * na + a][...] = acc

    def own(p):
        return pl.BlockSpec((None,) + p.shape[1:], lambda i, cc_ref: (cc_ref[0], 0, 0))

    def mine(p):
        return pl.BlockSpec((None,) + p.shape[1:], lambda i, cc_ref: (0, cc_ref[1], 0))

    grid_spec = pltpu.PrefetchScalarGridSpec(
        num_scalar_prefetch=1, grid=(1,),
        in_specs=[own(p) for p in partials] + [pl.BlockSpec(r.shape, lambda i, cc_ref: (0, 0, 0)) for r in received],
        out_specs=[mine(p) for p in partials])
    return pl.pallas_call(body, name=name, grid_spec=grid_spec,
                          out_shape=[jax.ShapeDtypeStruct((1, 2 * p.shape[1], p.shape[2]), F32) for p in partials],
                          compiler_params=_params(1))(chip_core, *partials, *received)


def _adam_math(w, g, m, v):
    nm = ADAM_B1 * m + (1.0 - ADAM_B1) * g
    nv = ADAM_B2 * v + (1.0 - ADAM_B2) * jnp.square(g)
    m_hat = nm / (1.0 - ADAM_B1 ** ADAM_STEP)
    v_hat = nv / (1.0 - ADAM_B2 ** ADAM_STEP)
    delta = -ADAM_LR * (m_hat / (jnp.sqrt(v_hat) + ADAM_EPS) + ADAM_WD * w)
    return delta, nm, nv


ADAMW_STEPS = 8


def _adamw(ws, gs, ms, vs):
    na = len(ws)

    def body(*refs):
        for a in range(na):
            w_ref, g_ref, m_ref, v_ref = (refs[k * na + a] for k in range(4))
            g_out_ref, d_ref, nm_ref, nv_ref = (refs[(4 + k) * na + a] for k in range(4))
            gv = g_ref[...]
            g_out_ref[...] = gv
            d_ref[...], nm_ref[...], nv_ref[...] = _adam_math(w_ref[...], gv, m_ref[...], v_ref[...])

    specs = [pl.BlockSpec((None, w.shape[1] // ADAMW_STEPS, w.shape[2]), lambda i: (0, i, 0)) for w in ws]
    outs = pl.pallas_call(
        body, name="adamw", grid=(ADAMW_STEPS,), out_shape=[jax.ShapeDtypeStruct(w.shape, F32) for w in ws] * 4,
        in_specs=specs * 4, out_specs=specs * 4, compiler_params=_params(1))(*ws, *gs, *ms, *vs)
    return [[outs[k * na + a] for k in range(4)] for a in range(na)]


SMALL_PARAMS = ("norm_mix_g", "b_gate", "conv_a_w", "conv_a_b", "norm_ffn_g", "ffn_conv_w", "ffn_conv_b", "final_norm_g")


def _small_update(partials, params, moments_m, moments_v, halves):
    na = len(partials)
    npar = len(SMALL_PARAMS)
    nh = len(halves)

    def body(*refs):
        in_refs = refs[:na]
        w_refs = refs[na:na + npar]
        m_refs = refs[na + npar:na + 2 * npar]
        v_refs = refs[na + 2 * npar:na + 3 * npar]
        pos = na + 3 * npar + nh
        loss_ref = refs[pos]
        out_refs = refs[pos + 1:pos + 1 + 4 * npar]
        big_refs = refs[pos + 1 + 4 * npar:pos + 1 + 4 * npar + nh]
        pos += 1 + 4 * npar + nh
        acc_refs = refs[pos:pos + na]
        recv_refs = refs[pos + na:pos + 4 * na]
        send_sems, recv_sems, share_send, share_recv = refs[pos + 4 * na:]
        x, y, c, _ = _mesh_position()
        chip = 2 * x + y
        for cp in _sibling_share_copies(big_refs, share_send, share_recv):
            cp.start()
        for a in range(na):
            acc_refs[a][...] = in_refs[a][...]
        for stage, peer in enumerate(((x, y, 1 - c), (x, 1 - y, c), (1 - x, y, c))):
            cps = []
            for a in range(na):
                k = stage * na + a
                cps.append(pltpu.make_async_remote_copy(src_ref=acc_refs[a], dst_ref=recv_refs[k], send_sem=send_sems.at[k],
                                                        recv_sem=recv_sems.at[k], device_id=peer, device_id_type=MESH_ID))
            for cp in cps:
                cp.start()
            for cp in cps:
                cp.wait()
            for a in range(na):
                acc_refs[a][...] = acc_refs[a][...] + recv_refs[stage * na + a][...]

        mix, ffn, fin, gate, conv, ffnc, loss = acc_refs
        loss_ref[...] = loss[...]

        def cols(width):
            return pl.ds(pl.multiple_of(chip * width, LANES), width)

        grads = {
            "norm_mix_g": mix[...], "norm_ffn_g": ffn[...], "final_norm_g": fin[...],
            "b_gate": gate[0:2, cols(D_MODEL // N_CHIPS)],
            "conv_a_w": conv[0:3, cols(CONV_WIDTH // N_CHIPS)], "conv_a_b": conv[3:4, :],
            "ffn_conv_w": ffnc[0:3, cols(2 * D_FF // N_CHIPS)], "ffn_conv_b": ffnc[3:4, :]}
        for i, name in enumerate(SMALL_PARAMS):
            g = grads[name]
            if len(w_refs[i].shape) == 3:
                results = (g,) + _adam_math(w_refs[i][0], g, m_refs[i][0], v_refs[i][0])
                for o_ref, val in zip(out_refs[4 * i:4 * i + 4], results):
                    o_ref[0] = val
            else:
                results = (g,) + _adam_math(w_refs[i][...], g, m_refs[i][...], v_refs[i][...])
                for o_ref, val in zip(out_refs[4 * i:4 * i + 4], results):
                    o_ref[...] = val

        for cp in _sibling_share_copies(big_refs, share_send, share_recv):
            cp.wait()

    outs = [jax.ShapeDtypeStruct(partials[-1].shape, F32)]
    for w in params:
        outs += [jax.ShapeDtypeStruct(w.shape, F32)] * 4
    n_small_out = len(outs)
    outs += [jax.ShapeDtypeStruct(h.shape, h.dtype) for h in halves]
    scratch = [pltpu.VMEM(p.shape, F32) for p in partials]
    scratch += [pltpu.VMEM(p.shape, F32) for _ in range(3) for p in partials]
    scratch += _dma_sems(3 * na) + _dma_sems(nh)
    n_in = na + 3 * npar
    return pl.pallas_call(
        body, name="small_update", out_shape=outs, in_specs=[_VMEM] * n_in + [_ANY] * nh,
        out_specs=[_VMEM] * n_small_out + [_ANY] * nh,
        input_output_aliases={n_in + a: n_small_out + a for a in range(nh)},
        scratch_shapes=scratch)(*partials, *params, *moments_m, *moments_v, *halves)


def _gathered_columns(g):
    return jnp.transpose(g, (1, 0, 2)).reshape(g.shape[1], N_CHIPS * g.shape[2])


def kernel(x, norm_mix_g, w_in, b_gate, conv_a_w, conv_a_b, w_proj_a, w_proj_b, w_out, norm_ffn_g, w_up, ffn_conv_w, ffn_conv_b, w_down, final_norm_g, loss_target, m_norm_mix_g, m_w_in, m_b_gate, m_conv_a_w, m_conv_a_b, m_w_proj_a, m_w_proj_b, m_w_out, m_norm_ffn_g, m_w_up, m_ffn_conv_w, m_ffn_conv_b, m_w_down, m_final_norm_g, v_norm_mix_g, v_w_in, v_b_gate, v_conv_a_w, v_conv_a_b, v_w_proj_a, v_w_proj_b, v_w_out, v_norm_ffn_g, v_w_up, v_ffn_conv_w, v_ffn_conv_b, v_w_down, v_final_norm_g):
    chip = (2 * lax.axis_index("x") + lax.axis_index("y")).astype(jnp.int32)
    core = lax.axis_index("c").astype(jnp.int32)
    core_arr = core.reshape(1)
    chip_core = jnp.stack([chip, core])
    xs, target = x[0], loss_target[0]
    g_final = final_norm_g.reshape(1, D_MODEL)

    def own_slot(gathered, own):
        return lax.dynamic_update_slice(gathered, own, (chip, 0, 0))

    w_in_t, m_w_in_t, v_w_in_t = (jnp.swapaxes(a, 1, 2) for a in (w_in, m_w_in, v_w_in))
    w_in_tb = w_in_t.astype(BF16)
    send1, recv1, shard_thru, g_in, token = _gather_start(w_in_tb)
    h1, h1_streams4, h1_streams16 = _norm_fwd(xs, norm_mix_g + token[0:1, 0:1])
    send2, recv2, g_in = _gather_forward(send1, recv1, shard_thru, g_in, h1)
    g_in = _gather_wait(send2, recv2, g_in)
    w_in_full_t = own_slot(g_in, w_in_tb).reshape(D_IN, D_MODEL)
    later_w = [w_proj_a, w_proj_b, w_out, w_up, w_down]
    later_b = [w.astype(BF16) for w in later_w]
    small_sharded = [b_gate, conv_a_w, ffn_conv_w]
    fwd = _inproj_fwd(h1, w_in_full_t, later_b, small_sharded)
    abcv, gates, qkv0, qkv1, qkv2 = fwd[:5]
    gathered_big, gathered_small = fwd[5:10], fwd[10:13]
    attn0 = _attn_fwd(qkv0, 0, forward=gathered_big)
    attn = [attn0[:2], _attn_fwd(qkv1, 1), _attn_fwd(qkv2, 2)]
    g_pa, g_pb, g_out, g_up, g_down = [own_slot(g, own) for g, own in zip(attn0[2:], later_b)]
    g_bgate, g_convw, g_ffnw = [own_slot(g, own) for g, own in zip(gathered_small, small_sharded)]
    w_pa_full, w_pb_full, w_up_full = _gathered_columns(g_pa), _gathered_columns(g_pb), _gathered_columns(g_up)
    w_out_full, w_down_full = g_out.reshape(D_MODEL, D_MODEL), g_down.reshape(D_FF, D_MODEL)
    b_gate_full, conv_w_full, ffn_w_full = (_gathered_columns(g) for g in (g_bgate, g_convw, g_ffnw))

    x1, ya0, yb0, mrg, ya, yb, lsetot = _mix_fwd(
        xs, abcv, gates, [a[0] for a in attn], [a[1] for a in attn], conv_w_full, conv_a_b, b_gate_full,
        w_pa_full, w_pb_full, w_out_full)
    h2, up0, up = _ffn_up_fwd(x1, norm_ffn_g, w_up_full, ffn_w_full, ffn_conv_b)
    act, dx2, dx2b, d_g_final, loss = _ffn_act_fwd(x1, up, target, w_down_full, g_final)

    d_up = _ffn_act_bwd(dx2b, up, w_down_full)
    slab_down = _matmul_tn("dw_down", act, dx2b, col_tile=512).reshape(N_CHIPS, D_FF // N_CHIPS, D_MODEL)
    dx1, dx1b, d_g_ffn, d_up0, ffn_small, swapped_down = _ffn_up_bwd(
        d_up, up0, w_up_full, ffn_w_full, x1, norm_ffn_g, dx2, swap=[slab_down])
    (partial_down,) = _add_sibling("grad_add_w_down", [slab_down], [swapped_down], core_arr)
    slab_up = _matmul_tn("dw_up", h2, d_up0, col_tile=2 * D_FF // N_CHIPS, slabs=N_CHIPS)
    d_w_out, swapped_up = _matmul_tn("dw_out", mrg, dx1b, swap=[slab_up])
    (partial_up,) = _add_sibling("grad_add_w_up", [slab_up], [swapped_up], core_arr)

    mix_res = _mix_bwd(dx1, abcv, gates, ya, yb, yb0, lsetot, conv_w_full, conv_a_b, b_gate_full,
                       w_pa_full, w_pb_full, w_out_full, exchange=[partial_up, partial_down])
    (d_ya, d_yb, d_gates, d_abcv, d_yb0, dyl0, dyl1, dyl2, aux0, aux1, aux2, gate_small, conv_small) = mix_res[:13]
    halves_ffn = _sum_chips("grad_sum_ffn", [partial_up, partial_down], mix_res[13:], chip_core)

    slabs_mix = [_matmul_tn("dw_proj_a", ya0, d_ya, slabs=N_CHIPS), _matmul_tn("dw_proj_b", yb0, d_yb, slabs=N_CHIPS),
                 d_w_out.reshape(N_CHIPS, D_MODEL // N_CHIPS, D_MODEL)]
    res0 = _attn_bwd(qkv0, dyl0, aux0, 0, swap=slabs_mix)
    d_qkv0, partials_mix = res0[0], _add_sibling("grad_add_mix", slabs_mix, res0[1:], core_arr)
    res1 = _attn_bwd(qkv1, dyl1, aux1, 1, exchange=partials_mix)
    d_qkv1, halves_mix = res1[0], _sum_chips("grad_sum_mix", partials_mix, res1[1:], chip_core)
    (d_qkv2,) = _attn_bwd(qkv2, dyl2, aux2, 2)

    dq = [d_qkv0, d_qkv1, d_qkv2]
    seq = xs.shape[0]
    d_w_abcv = _matmul_tn("dw_in_abcv", d_abcv, h1)
    d_w_gates = _matmul_tn("dw_in_gates", d_gates, h1)
    d_w_groups = [_matmul_tn(f"dw_in_qkv{g}", t.reshape(seq, 3 * GROUP_WIDTH), h.reshape(seq, D_MODEL))
                  for g, (t, h) in enumerate(zip(dq, (h1, h1_streams4, h1_streams16)))]
    gw = GROUP_WIDTH
    d_w_in_t = jnp.concatenate(
        [d_w_abcv] + [d_w_groups[g][j * gw:(j + 1) * gw] for j in range(3) for g in range(3)] + [d_w_gates], axis=0)

    slab_in = d_w_in_t.reshape(N_CHIPS, D_IN // N_CHIPS, D_MODEL)
    (from_sibling_in,) = _sibling_swap_halves("grad_swap_w_in", [slab_in])
    (partial_in,) = _add_sibling("grad_add_w_in", [slab_in], [from_sibling_in], core_arr)
    send_sems, recv_sems, partial_thru, landing_thru, token = _chip_exchange_start(partial_in)
    g_mix_after_start = norm_mix_g + token[0:1, 0:1]
    grad_x, d_g_mix = _inproj_bwd(d_abcv, d_gates, dq, w_in_full_t, xs, g_mix_after_start, dx1)
    partial_in, received_in = _chip_exchange_wait(send_sems, recv_sems, partial_thru, landing_thru, d_g_mix)
    halves_in = _sum_chips("grad_sum_w_in", [partial_in], [received_in], chip_core)

    big_names = ("w_in", "w_proj_a", "w_proj_b", "w_out", "w_up", "w_down")
    big_w = dict(w_in=w_in_t, w_proj_a=w_proj_a, w_proj_b=w_proj_b, w_out=w_out, w_up=w_up, w_down=w_down)
    big_m = dict(w_in=m_w_in_t, w_proj_a=m_w_proj_a, w_proj_b=m_w_proj_b, w_out=m_w_out, w_up=m_w_up, w_down=m_w_down)
    big_v = dict(w_in=v_w_in_t, w_proj_a=v_w_proj_a, w_proj_b=v_w_proj_b, w_out=v_w_out, w_up=v_w_up, w_down=v_w_down)

    fin_w, fin_m, fin_v = (a.reshape(1, D_MODEL) for a in (final_norm_g, m_final_norm_g, v_final_norm_g))
    small_w = [norm_mix_g, b_gate, conv_a_w, conv_a_b, norm_ffn_g, ffn_conv_w, ffn_conv_b, fin_w]
    small_m = [m_norm_mix_g, m_b_gate, m_conv_a_w, m_conv_a_b, m_norm_ffn_g, m_ffn_conv_w, m_ffn_conv_b, fin_m]
    small_v = [v_norm_mix_g, v_b_gate, v_conv_a_w, v_conv_a_b, v_norm_ffn_g, v_ffn_conv_w, v_ffn_conv_b, fin_v]
    small_out = _small_update([d_g_mix, d_g_ffn, d_g_final, gate_small, conv_small, ffn_small, loss],
                              small_w, small_m, small_v, halves_in + halves_mix + halves_ffn)
    big_grads = small_out[1 + 4 * len(SMALL_PARAMS):]
    total_loss = small_out[0][0, 0]

    grads, delta, new_m, new_v = {}, {}, {}, {}
    for i, n in enumerate(SMALL_PARAMS):
        vals = small_out[1 + 4 * i:5 + 4 * i]
        if n == "final_norm_g":
            vals = [a.reshape(D_MODEL) for a in vals]
        grads[n], delta[n], new_m[n], new_v[n] = vals
    updates = _adamw([big_w[n] for n in big_names], big_grads, [big_m[n] for n in big_names],
                     [big_v[n] for n in big_names])
    for n, vals in zip(big_names, updates):
        if n == "w_in":
            vals = [jnp.swapaxes(a, 1, 2) for a in vals]
        grads[n], delta[n], new_m[n], new_v[n] = vals

    names = ["norm_mix_g", "w_in", "b_gate", "conv_a_w", "conv_a_b", "w_proj_a", "w_proj_b", "w_out", "norm_ffn_g", "w_up",
             "ffn_conv_w", "ffn_conv_b", "w_down", "final_norm_g"]
    out = [total_loss, grad_x[None]]
    for group in (grads, delta, new_m, new_v):
        out += [group[n] for n in names]
    return tuple(out)
```

```python
import jax
import jax.numpy as jnp
from jax import lax
from jax.experimental import pallas as pl
from jax.experimental.pallas import tpu as pltpu

F32 = jnp.float32
BF16 = jnp.bfloat16

D_MODEL = 1024
CONV_WIDTH = 512
ATTN_WIDTH = 768
GROUP_WIDTH = 256
HEAD_DIM = 64
HEADS_PER_GROUP = 4
DILATIONS = (1, 4, 16)
ATTN_BLOCK = 128
D_FF = 2816
D_IN = 5888
EPS = 1e-6
NEG_INF = -1e30
ATTN_SCALE = HEAD_DIM ** -0.5

COL_ABCV = 0
COL_Q = 1536
COL_K = 2304
COL_V = 3072
COL_GATES = 3840

ADAM_LR = 0.001
ADAM_B1 = 0.9
ADAM_B2 = 0.999
ADAM_EPS = 1e-08
ADAM_WD = 0.01
ADAM_STEP = 10

LANES = 128
SUBLANES = 8
BF16_ROWS = 16
ROW_TILE = 512
VMEM_LIMIT = 56 * 1024 * 1024

_NT = (((1,), (1,)), ((), ()))
_TN = (((0,), (0,)), ((), ()))


def _params(n_axes, vmem=VMEM_LIMIT):
    return pltpu.CompilerParams(dimension_semantics=("arbitrary",) * n_axes, vmem_limit_bytes=vmem)


def _resident(shape):
    nd = len(shape)
    return pl.BlockSpec(shape, lambda *_: (0,) * nd, pipeline_mode=pl.Buffered(1))


def _rows(tm, width, col_block=0):
    return pl.BlockSpec((tm, width), lambda i: (i, col_block))


def _col_chunks(n, cmax):
    out, lo = [], 0
    while lo < n:
        size = min(cmax, n - lo)
        out.append((lo, size))
        lo += size
    return out


def _dot(a, b):
    return jnp.dot(a, b, preferred_element_type=F32)


def _dot_nt(a, b):
    return lax.dot_general(a, b, _NT, preferred_element_type=F32)


def _dot_tn(a, b):
    return lax.dot_general(a, b, _TN, preferred_element_type=F32)


def _sigmoid(x):
    return 0.5 * jnp.tanh(0.5 * x) + 0.5


def _silu(x):
    hx = 0.5 * x
    return hx + hx * jnp.tanh(hx)


def _shift_down(v, k, halo8):
    tm = v.shape[0]
    rolled = pltpu.roll(v, k, 0)
    fix = jnp.tile(pltpu.roll(halo8, k, 0), (tm // SUBLANES, 1))
    row = lax.broadcasted_iota(jnp.int32, v.shape, 0)
    return jnp.where(row < k, fix, rolled)


def _shift_up(v, k, halo8):
    tm = v.shape[0]
    rolled = pltpu.roll(v, tm - k, 0)
    fix = jnp.tile(pltpu.roll(halo8, SUBLANES - k, 0), (tm // SUBLANES, 1))
    row = lax.broadcasted_iota(jnp.int32, v.shape, 0)
    return jnp.where(row >= tm - k, fix, rolled)


def _colsum(v):
    return jnp.sum(v, axis=0, keepdims=True)


def _to_streams(val, scr, out_ref, d, col0):
    tm = val.shape[0]
    panels = val.shape[1] // LANES
    if d == 1:
        out_ref[0, :, col0:col0 + val.shape[1]] = val.astype(out_ref.dtype)
        return
    for p in range(panels):
        scr[pl.ds(p * tm, tm), :] = val[:, p * LANES:(p + 1) * LANES]
    for r in range(d):
        for p in range(panels):
            piece = scr[pl.ds(p * tm + r, tm // d, stride=d), :]
            out_ref[r, :, col0 + p * LANES: col0 + (p + 1) * LANES] = piece.astype(out_ref.dtype)


def _from_streams(in_ref, scr, d, col0, width):
    panels = width // LANES
    rows = in_ref.shape[1]
    tm = rows * d
    if d == 1:
        return in_ref[0, :, col0:col0 + width].astype(F32)
    for r in range(d):
        for p in range(panels):
            scr[pl.ds(p * tm + r, rows, stride=d), :] = in_ref[r, :, col0 + p * LANES: col0 + (p + 1) * LANES].astype(F32)
    return jnp.concatenate([scr[pl.ds(p * tm, tm), :] for p in range(panels)], axis=1)


def _stream_block(tm, d, width):
    return pl.BlockSpec((d, tm // d, width), lambda i: (0, i, 0))


def _rev_stream_block(tm, d, width, nt):
    return pl.BlockSpec((d, tm // d, width), lambda i: (0, nt - 1 - i, 0))


N_CHIPS = 4
MESH_ID = pl.DeviceIdType.MESH
_ANY = pl.BlockSpec(memory_space=pl.ANY)
_VMEM = pl.BlockSpec(memory_space=pltpu.VMEM)


def _mesh_position():
    x, y, c = lax.axis_index("x"), lax.axis_index("y"), lax.axis_index("c")
    other_chips = [(1 - x, y), (x, 1 - y), (1 - x, 1 - y)]
    return x, y, c, other_chips


def _half_rows(c, half):
    return pl.ds(pl.multiple_of(c * half, BF16_ROWS), half)


def _remote_copy(k, src, dst, to, send_sems, recv_sems):
    return pltpu.make_async_remote_copy(src_ref=src, dst_ref=dst, send_sem=send_sems.at[k], recv_sem=recv_sems.at[k],
                                        device_id=to, device_id_type=MESH_ID)


def _gather_first_copies(big_refs, small_refs, big_outs, small_outs, send_sems, recv_sems):
    x, y, c, chips = _mesh_position()
    me = 2 * x + y
    nb = len(big_refs)
    cps = []
    for j, (px, py) in enumerate(chips):
        for b in range(nb):
            mine = _half_rows(c, big_refs[b].shape[1] // 2)
            cps.append(_remote_copy(3 * b + j, big_refs[b].at[0, mine], big_outs[b].at[me, mine], (px, py, c),
                                    send_sems, recv_sems))
        for s in range(len(small_refs)):
            cps.append(_remote_copy(3 * (nb + s) + j, small_refs[s].at[0], small_outs[s].at[me], (px, py, c),
                                    send_sems, recv_sems))
    return cps


def _gather_forward_copies(bufs, send_sems, recv_sems):
    x, y, c, chips = _mesh_position()
    cps = []
    for j, (px, py) in enumerate(chips):
        for b in range(len(bufs)):
            landed = bufs[b].at[2 * px + py, _half_rows(c, bufs[b].shape[1] // 2)]
            cps.append(_remote_copy(3 * b + j, landed, landed, (x, y, 1 - c), send_sems, recv_sems))
    return cps


def _chip_exchange_copies(src_refs, out_refs, send_sems, recv_sems):
    x, y, c, chips = _mesh_position()
    cps = []
    for j, (px, py) in enumerate(chips):
        for a in range(len(src_refs)):
            cps.append(_remote_copy(3 * a + j, src_refs[a].at[2 * px + py], out_refs[a].at[j], (px, py, c),
                                    send_sems, recv_sems))
    return cps


def _sibling_swap_copies(src_refs, out_refs, send_sems, recv_sems):
    x, y, c, _ = _mesh_position()
    cps = []
    for a in range(len(src_refs)):
        theirs = _half_rows(1 - c, src_refs[a].shape[1] // 2)
        cps.append(_remote_copy(a, src_refs[a].at[:, theirs, :], out_refs[a], (x, y, 1 - c), send_sems, recv_sems))
    return cps


def _swap_shapes(slabs):
    return [jax.ShapeDtypeStruct((a.shape[0], a.shape[1] // 2, a.shape[2]), a.dtype) for a in slabs]


def _dma_sems(n):
    return [pltpu.SemaphoreType.DMA((n,)), pltpu.SemaphoreType.DMA((n,))]


def _norm_fwd(x, g):
    s = x.shape[0]
    tm = ROW_TILE

    def body(x_ref, g_ref, h_ref, hs1_ref, hs2_ref, scr):
        xv = x_ref[...]
        r = lax.rsqrt(jnp.mean(xv * xv, axis=-1, keepdims=True) + EPS)
        hf = xv * r * g_ref[...]
        h_ref[...] = hf.astype(BF16)
        for d, hs_ref in zip(DILATIONS[1:], (hs1_ref, hs2_ref)):
            for lo, size in _col_chunks(D_MODEL, GROUP_WIDTH):
                _to_streams(hf[:, lo:lo + size], scr, hs_ref, d, lo)

    return pl.pallas_call(
        body, name="norm_fwd", grid=(s // tm,),
        out_shape=[jax.ShapeDtypeStruct((s, D_MODEL), BF16)]
        + [jax.ShapeDtypeStruct((d, s // d, D_MODEL), BF16) for d in DILATIONS[1:]],
        in_specs=[_rows(tm, D_MODEL), _resident((1, D_MODEL))],
        out_specs=[_rows(tm, D_MODEL)] + [_stream_block(tm, d, D_MODEL) for d in DILATIONS[1:]],
        scratch_shapes=[pltpu.VMEM((GROUP_WIDTH // LANES * tm, LANES), F32)],
        compiler_params=_params(1))(x, g)


def _inproj_fwd(h1, w_in_t, big_shards, small_shards):
    s = h1.shape[0]
    tm = ROW_TILE
    nt = s // tm
    nb, ns = len(big_shards), len(small_shards)
    n_fixed_in, n_fixed_out = 2, 5

    def body(*refs):
        h_ref, w_ref = refs[:n_fixed_in]
        shard_refs = refs[n_fixed_in:n_fixed_in + nb + ns]
        pos = n_fixed_in + nb + ns
        abcv_ref, gates_ref, qkv0_ref, qkv1_ref, qkv2_ref = refs[pos:pos + n_fixed_out]
        gathered_refs = refs[pos + n_fixed_out:pos + n_fixed_out + nb + ns]
        scr, send_sems, recv_sems = refs[pos + n_fixed_out + nb + ns:]
        i = pl.program_id(0)

        def gather_copies():
            return _gather_first_copies(shard_refs[:nb], shard_refs[nb:], gathered_refs[:nb], gathered_refs[nb:],
                                        send_sems, recv_sems)

        @pl.when(i == 0)
        def _():
            for cp in gather_copies():
                cp.start()

        h = h_ref[...]
        for lo, size in _col_chunks(3 * CONV_WIDTH, 512):
            abcv_ref[:, lo:lo + size] = _dot_nt(h, w_ref[COL_ABCV + lo: COL_ABCV + lo + size, :]).astype(BF16)
        for lo, size in _col_chunks(2 * D_MODEL, 512):
            gates_ref[:, lo:lo + size] = _dot_nt(h, w_ref[COL_GATES + lo: COL_GATES + lo + size, :]).astype(BF16)
        for gi, (d, out_ref) in enumerate(zip(DILATIONS, (qkv0_ref, qkv1_ref, qkv2_ref))):
            for j, base in enumerate((COL_Q, COL_K, COL_V)):
                lo = base + gi * GROUP_WIDTH
                y = _dot_nt(h, w_ref[lo:lo + GROUP_WIDTH, :])
                if j == 0:
                    y = y * ATTN_SCALE
                _to_streams(y, scr, out_ref, d, j * GROUP_WIDTH)

        @pl.when(i == nt - 1)
        def _():
            for cp in gather_copies():
                cp.wait()

    outs = [jax.ShapeDtypeStruct((s, 3 * CONV_WIDTH), BF16), jax.ShapeDtypeStruct((s, 2 * D_MODEL), BF16)]
    outs += [jax.ShapeDtypeStruct((d, s // d, 3 * GROUP_WIDTH), BF16) for d in DILATIONS]
    outs += [jax.ShapeDtypeStruct((N_CHIPS,) + a.shape[1:], a.dtype) for a in list(big_shards) + list(small_shards)]
    return pl.pallas_call(
        body, name="inproj_fwd", grid=(nt,), out_shape=outs,
        in_specs=[_rows(tm, D_MODEL), _resident((D_IN, D_MODEL))] + [_ANY] * (nb + ns),
        out_specs=[_rows(tm, 3 * CONV_WIDTH), _rows(tm, 2 * D_MODEL)]
        + [_stream_block(tm, d, 3 * GROUP_WIDTH) for d in DILATIONS] + [_ANY] * (nb + ns),
        scratch_shapes=[pltpu.VMEM((GROUP_WIDTH // LANES * tm, LANES), F32)] + _dma_sems(3 * (nb + ns)),
        compiler_params=_params(1))(h1, w_in_t, *big_shards, *small_shards)


def _head_of_lane(shape):
    return lax.broadcasted_iota(jnp.int32, shape, 1) // HEAD_DIM


def _stack_heads(v):
    head = _head_of_lane(v.shape)
    return jnp.concatenate([jnp.where(head == h, v, jnp.zeros_like(v)) for h in range(HEADS_PER_GROUP)], axis=0)


def _unstack_heads(v):
    q = ATTN_BLOCK
    head = _head_of_lane((q, v.shape[1]))
    out = jnp.zeros((q, v.shape[1]), v.dtype)
    for h in range(HEADS_PER_GROUP):
        out = jnp.where(head == h, v[h * q:(h + 1) * q], out)
    return out


def _per_head_rows(col):
    q = ATTN_BLOCK
    head = _head_of_lane((q, GROUP_WIDTH))
    out = jnp.zeros((q, GROUP_WIDTH), col.dtype)
    for h in range(HEADS_PER_GROUP):
        out = jnp.where(head == h, col[h * q:(h + 1) * q], out)
    return out


def _compact_heads(v):
    lane = lax.broadcasted_iota(jnp.int32, (v.shape[0], LANES), 1)
    return jnp.where((lane & 32) == 0, v[:, 0:LANES], v[:, LANES:2 * LANES])


def _compact_head_col(v):
    lane = lax.broadcasted_iota(jnp.int32, v.shape, 1)
    head = ((lane >> 6) & 1) + 2 * ((lane >> 5) & 1)
    cols = [jnp.max(jnp.where(head == h, v, -jnp.inf), axis=1, keepdims=True) for h in range(HEADS_PER_GROUP)]
    return jnp.concatenate(cols, axis=0)


ATTN_FWD_BLOCKS = 16
ATTN_BWD_BLOCKS = 8


def _band_bias(first_block):
    rows = HEADS_PER_GROUP * ATTN_BLOCK
    qi = lax.broadcasted_iota(jnp.int32, (rows, 2 * ATTN_BLOCK), 0) % ATTN_BLOCK
    kj = lax.broadcasted_iota(jnp.int32, (rows, 2 * ATTN_BLOCK), 1)
    dist = qi + ATTN_BLOCK - kj
    valid = (dist >= 0) & (dist <= ATTN_BLOCK)
    if first_block:
        valid = valid & (kj >= ATTN_BLOCK)
    return jnp.where(valid, 0.0, NEG_INF).astype(F32)


def _store_band_biases(bias_ref):
    bias_ref[0] = _band_bias(False)
    bias_ref[1] = _band_bias(True)


def _attn_block_specs(g, nb, clamp_last=False):
    q = ATTN_BLOCK
    last = nb // g - 1

    def cur(col, width=GROUP_WIDTH):
        if clamp_last:
            return pl.BlockSpec((None, g * q, width), lambda r, n: (r, jnp.minimum(n, last), col))
        return pl.BlockSpec((None, g * q, width), lambda r, n: (r, n, col))

    def prev(col):
        if clamp_last:
            return pl.BlockSpec((None, q, GROUP_WIDTH), lambda r, n: (r, jnp.clip(n * g - 1, 0, nb - 1), col))
        return pl.BlockSpec((None, q, GROUP_WIDTH), lambda r, n: (r, jnp.maximum(n * g - 1, 0), col))

    return cur, prev


def _attn_fwd(qkv, gi, forward=()):
    d, length, _ = qkv.shape
    nb = length // ATTN_BLOCK
    q = ATTN_BLOCK
    g = min(ATTN_FWD_BLOCKS, nb)
    ns = nb // g
    nf = len(forward)

    def body(*refs):
        q_ref, kp_ref, kc_ref, vp_ref, vc_ref = refs[:5]
        o_ref, lse_ref = refs[5 + nf:7 + nf]
        buf_refs = refs[7 + nf:7 + 2 * nf]
        bias_ref = refs[7 + 2 * nf]
        sems = refs[8 + 2 * nf:]
        n = pl.program_id(1)
        first_step = (pl.program_id(0) == 0) & (n == 0)
        last_step = (pl.program_id(0) == d - 1) & (n == ns - 1)

        @pl.when(first_step)
        def _():
            _store_band_biases(bias_ref)
            for cp in _gather_forward_copies(buf_refs, *sems) if nf else ():
                cp.start()

        kfull = jnp.concatenate([kp_ref[...], kc_ref[...]], axis=0)
        vfull = jnp.concatenate([vp_ref[...], vc_ref[...]], axis=0)
        for j in range(g):
            qs = _stack_heads(q_ref[j * q:(j + 1) * q, :])
            k2 = kfull[j * q:(j + 2) * q]
            v2 = vfull[j * q:(j + 2) * q]
            bias = jnp.where(n == 0, bias_ref[1], bias_ref[0]) if j == 0 else bias_ref[0]
            sc = _dot_nt(qs, k2) + bias
            m = jnp.max(sc, axis=1, keepdims=True)
            p = jnp.exp(sc - m)
            l = jnp.sum(p, axis=1, keepdims=True)
            of = _dot(p.astype(BF16), v2) / l
            o_ref[j * q:(j + 1) * q, :] = _unstack_heads(of).astype(BF16)
            lse_ref[j * q:(j + 1) * q, :] = _per_head_rows(m + jnp.log(l))

        if nf:
            @pl.when(last_step)
            def _():
                for cp in _gather_forward_copies(buf_refs, *sems):
                    cp.wait()

    cur, prev = _attn_block_specs(g, nb)
    return pl.pallas_call(
        body, name=f"attn_fwd_g{gi}", grid=(d, ns),
        out_shape=[jax.ShapeDtypeStruct((d, length, GROUP_WIDTH), BF16),
                   jax.ShapeDtypeStruct((d, length, GROUP_WIDTH), F32)]
        + [jax.ShapeDtypeStruct(a.shape, a.dtype) for a in forward],
        in_specs=[cur(0), prev(1), cur(1), prev(2), cur(2)] + [_ANY] * nf,
        out_specs=[cur(0), cur(0)] + [_ANY] * nf,
        input_output_aliases={5 + a: 2 + a for a in range(nf)},
        scratch_shapes=[pltpu.VMEM((2, HEADS_PER_GROUP * q, 2 * q), F32)] + (_dma_sems(3 * nf) if nf else []),
        compiler_params=_params(2))(qkv, qkv, qkv, qkv, qkv, *forward)


def _conv_branch(ab, ac, av, halo_u, w, b):
    u = ac * av
    sh1 = _shift_down(u, 1, halo_u)
    sh2 = _shift_down(u, 2, halo_u)
    cv = w[0:1] * sh2 + w[1:2] * sh1 + w[2:3] * u + b
    return ab * cv, cv, u, sh1, sh2


def _mix_fwd(x, abcv, gates, o_list, lse_list, conv_w, conv_b, b_gate, w_pa, w_pb, w_out):
    s = x.shape[0]
    tm = ROW_TILE

    def body(x_ref, abcv_ref, gates_ref, o0_ref, o1_ref, o2_ref, l0_ref, l1_ref, l2_ref,
             cw_ref, cb_ref, bg_ref, wpa_ref, wpb_ref, wout_ref,
             x1_ref, ya0_ref, yb0_ref, mrg_ref, ya_ref, yb_ref, lsetot_ref, halo_ref, scr):
        i = pl.program_id(0)

        @pl.when(i == 0)
        def _():
            halo_ref[...] = jnp.zeros_like(halo_ref)

        ab = abcv_ref[:, 0:CONV_WIDTH].astype(F32)
        ac = abcv_ref[:, CONV_WIDTH:2 * CONV_WIDTH].astype(F32)
        av = abcv_ref[:, 2 * CONV_WIDTH:3 * CONV_WIDTH].astype(F32)
        ya0, _, u, _, _ = _conv_branch(ab, ac, av, halo_ref[...], cw_ref[...], cb_ref[...])
        halo_ref[...] = u[tm - SUBLANES:tm]
        ya0 = ya0.astype(BF16)
        ya0_ref[...] = ya0
        ya = _dot(ya0, wpa_ref[...])

        o_refs, l_refs = (o0_ref, o1_ref, o2_ref), (l0_ref, l1_ref, l2_ref)
        lses = [_from_streams(l_refs[g], scr, DILATIONS[g], 0, GROUP_WIDTH) for g in range(3)]
        top = jnp.maximum(jnp.maximum(lses[0], lses[1]), lses[2])
        weights = [jnp.exp(lse - top) for lse in lses]
        total = weights[0] + weights[1] + weights[2]
        lsetot_ref[...] = top + jnp.log(total)
        inv_total = 1.0 / total
        yb = jnp.zeros((tm, D_MODEL), F32)
        for g in range(3):
            og = _from_streams(o_refs[g], scr, DILATIONS[g], 0, GROUP_WIDTH)
            yb0 = (weights[g] * inv_total * og).astype(BF16)
            yb0_ref[:, g * GROUP_WIDTH:(g + 1) * GROUP_WIDTH] = yb0
            yb = yb + _dot(yb0, wpb_ref[g * GROUP_WIDTH:(g + 1) * GROUP_WIDTH, :])

        sa = _sigmoid(gates_ref[:, 0:D_MODEL].astype(F32) + bg_ref[0:1, :])
        sb = _sigmoid(gates_ref[:, D_MODEL:2 * D_MODEL].astype(F32) + bg_ref[1:2, :])
        ya_ref[...] = ya.astype(BF16)
        yb_ref[...] = yb.astype(BF16)
        mrg = (sa * ya + sb * yb).astype(BF16)
        mrg_ref[...] = mrg
        x1_ref[...] = x_ref[...] + _dot(mrg, wout_ref[...])

    outs = [jax.ShapeDtypeStruct((s, D_MODEL), F32),
            jax.ShapeDtypeStruct((s, CONV_WIDTH), BF16),
            jax.ShapeDtypeStruct((s, ATTN_WIDTH), BF16),
            jax.ShapeDtypeStruct((s, D_MODEL), BF16),
            jax.ShapeDtypeStruct((s, D_MODEL), BF16),
            jax.ShapeDtypeStruct((s, D_MODEL), BF16),
            jax.ShapeDtypeStruct((s, GROUP_WIDTH), F32)]
    return pl.pallas_call(
        body, name="mix_fwd", grid=(s // tm,), out_shape=outs,
        in_specs=[_rows(tm, D_MODEL), _rows(tm, 3 * CONV_WIDTH), _rows(tm, 2 * D_MODEL)]
        + [_stream_block(tm, d, GROUP_WIDTH) for d in DILATIONS] * 2
        + [_resident((3, CONV_WIDTH)), _resident((1, CONV_WIDTH)), _resident((2, D_MODEL)),
           _resident((CONV_WIDTH, D_MODEL)), _resident((ATTN_WIDTH, D_MODEL)), _resident((D_MODEL, D_MODEL))],
        out_specs=[_rows(tm, D_MODEL), _rows(tm, CONV_WIDTH), _rows(tm, ATTN_WIDTH), _rows(tm, D_MODEL),
                   _rows(tm, D_MODEL), _rows(tm, D_MODEL), _rows(tm, GROUP_WIDTH)],
        scratch_shapes=[pltpu.VMEM((SUBLANES, CONV_WIDTH), F32),
                        pltpu.VMEM((GROUP_WIDTH // LANES * tm, LANES), F32)],
        compiler_params=_params(1))(x, abcv, gates, *o_list, *lse_list, conv_w, conv_b, b_gate, w_pa, w_pb, w_out)


FFN_CHUNK = 512
FFN_UP_ROW_TILE = 256


def _ffn_up_fwd(x1, g, w_up, conv_w, conv_b):
    s = x1.shape[0]
    n = w_up.shape[1]
    tm = FFN_UP_ROW_TILE

    def body(x_ref, g_ref, w_ref, cw_ref, cb_ref, h_ref, up0_ref, up_ref, halo_ref):
        @pl.when(pl.program_id(0) == 0)
        def _():
            halo_ref[...] = jnp.zeros_like(halo_ref)

        xv = x_ref[...]
        r = lax.rsqrt(jnp.mean(xv * xv, axis=-1, keepdims=True) + EPS)
        h = (xv * r * g_ref[...]).astype(BF16)
        h_ref[...] = h
        for lo, size in _col_chunks(n, FFN_CHUNK):
            cols = slice(lo, lo + size)
            y = _dot(h, w_ref[:, cols])
            up0_ref[:, cols] = y.astype(BF16)
            halo = halo_ref[:, cols]
            w = cw_ref[:, cols]
            up = w[0:1] * _shift_down(y, 2, halo) + w[1:2] * _shift_down(y, 1, halo) + w[2:3] * y + cb_ref[:, cols]
            up_ref[:, cols] = up.astype(BF16)
            halo_ref[:, cols] = y[tm - SUBLANES:tm]

    return pl.pallas_call(
        body, name="ffn_up_fwd", grid=(s // tm,),
        out_shape=[jax.ShapeDtypeStruct((s, D_MODEL), BF16), jax.ShapeDtypeStruct((s, n), BF16),
                   jax.ShapeDtypeStruct((s, n), BF16)],
        in_specs=[_rows(tm, D_MODEL), _resident((1, D_MODEL)), _resident((D_MODEL, n)), _resident((3, n)),
                  _resident((1, n))],
        out_specs=[_rows(tm, D_MODEL), _rows(tm, n), _rows(tm, n)],
        scratch_shapes=[pltpu.VMEM((SUBLANES, n), F32)],
        compiler_params=_params(1))(x1, g, w_up, conv_w, conv_b)


def _ffn_act_fwd(x1, up, target, w_down, g_final):
    s = x1.shape[0]
    tm = ROW_TILE

    def body(x1_ref, up_ref, tgt_ref, wd_ref, gf_ref, act_ref, dx2_ref, dx2b_ref, dgf_ref, loss_ref):
        @pl.when(pl.program_id(0) == 0)
        def _():
            dgf_ref[...] = jnp.zeros_like(dgf_ref)
            loss_ref[...] = jnp.zeros_like(loss_ref)

        acc = jnp.zeros((tm, D_MODEL), F32)
        for lo, size in _col_chunks(D_FF, FFN_CHUNK):
            gate = up_ref[:, lo:lo + size].astype(F32)
            val = up_ref[:, D_FF + lo:D_FF + lo + size].astype(F32)
            act = (_silu(gate) * val).astype(BF16)
            act_ref[:, lo:lo + size] = act
            acc = acc + _dot(act, wd_ref[lo:lo + size, :])

        x2 = x1_ref[...] + acc
        r = lax.rsqrt(jnp.mean(x2 * x2, axis=-1, keepdims=True) + EPS)
        xn = x2 * r
        gf = gf_ref[...]
        err = xn * gf - tgt_ref[...]
        loss_ref[...] += (0.5 / D_MODEL) * jnp.sum(err * err)
        dy = err * (1.0 / D_MODEL)
        dgf_ref[...] += _colsum(dy * xn)
        dxn = dy * gf
        dx2 = r * (dxn - xn * jnp.mean(dxn * xn, axis=-1, keepdims=True))
        dx2_ref[...] = dx2
        dx2b_ref[...] = dx2.astype(BF16)

    return pl.pallas_call(
        body, name="ffn_act_fwd", grid=(s // tm,),
        out_shape=[jax.ShapeDtypeStruct((s, D_FF), BF16), jax.ShapeDtypeStruct((s, D_MODEL), F32),
                   jax.ShapeDtypeStruct((s, D_MODEL), BF16),
                   jax.ShapeDtypeStruct((1, D_MODEL), F32), jax.ShapeDtypeStruct((1, LANES), F32)],
        in_specs=[_rows(tm, D_MODEL), _rows(tm, 2 * D_FF), _rows(tm, D_MODEL),
                  _resident((D_FF, D_MODEL)), _resident((1, D_MODEL))],
        out_specs=[_rows(tm, D_FF), _rows(tm, D_MODEL), _rows(tm, D_MODEL),
                   pl.BlockSpec((1, D_MODEL), lambda i: (0, 0)), pl.BlockSpec((1, LANES), lambda i: (0, 0))],
        compiler_params=_params(1))(x1, up, target, w_down, g_final)


def _ffn_act_bwd(dx2b, up, w_down):
    s = dx2b.shape[0]
    tm = ROW_TILE

    def body(dx2_ref, up_ref, wd_ref, dup_ref):
        dx2 = dx2_ref[...]
        for lo, size in _col_chunks(D_FF, FFN_CHUNK):
            gate = up_ref[:, lo:lo + size].astype(F32)
            val = up_ref[:, D_FF + lo:D_FF + lo + size].astype(F32)
            dact = _dot_nt(dx2, wd_ref[lo:lo + size, :])
            sg = _sigmoid(gate)
            dup_ref[:, lo:lo + size] = (dact * val * (sg * (1.0 + gate * (1.0 - sg)))).astype(BF16)
            dup_ref[:, D_FF + lo:D_FF + lo + size] = (dact * (gate * sg)).astype(BF16)

    return pl.pallas_call(
        body, name="ffn_act_bwd", grid=(s // tm,),
        out_shape=jax.ShapeDtypeStruct((s, 2 * D_FF), BF16),
        in_specs=[_rows(tm, D_MODEL), _rows(tm, 2 * D_FF), _resident((D_FF, D_MODEL))],
        out_specs=_rows(tm, 2 * D_FF),
        compiler_params=_params(1))(dx2b, up, w_down)


def _ffn_up_bwd(d_up, up0, w_up, conv_w, x1, g, dres, swap=()):
    s = x1.shape[0]
    n = w_up.shape[1]
    tm = FFN_UP_ROW_TILE
    nt = s // tm
    nw = len(swap)

    def body(*refs):
        dup_ref, up0_ref, w_ref, cw_ref, x_ref, g_ref, dres_ref = refs[:7]
        slab_refs = refs[7:7 + nw]
        dx_ref, dxb_ref, dg_ref, dup0_ref, small_ref = refs[7 + nw:12 + nw]
        swapped_refs = refs[12 + nw:12 + 2 * nw]
        next_ref = refs[12 + 2 * nw]
        sems = refs[13 + 2 * nw:]

        @pl.when(pl.program_id(0) == 0)
        def _():
            next_ref[...] = jnp.zeros_like(next_ref)
            small_ref[...] = jnp.zeros_like(small_ref)
            dg_ref[...] = jnp.zeros_like(dg_ref)
            for cp in _sibling_swap_copies(slab_refs, swapped_refs, *sems) if nw else ():
                cp.start()

        dh = jnp.zeros((tm, D_MODEL), F32)
        for lo, size in _col_chunks(n, FFN_CHUNK):
            cols = slice(lo, lo + size)
            dz = dup_ref[:, cols].astype(F32)
            x0 = up0_ref[:, cols].astype(F32)
            nxt = next_ref[:, cols]
            dz1 = _shift_up(dz, 1, nxt)
            dz2 = _shift_up(dz, 2, nxt)
            next_ref[:, cols] = dz[0:SUBLANES]
            small_ref[0:1, cols] += _colsum(dz2 * x0)
            small_ref[1:2, cols] += _colsum(dz1 * x0)
            small_ref[2:3, cols] += _colsum(dz * x0)
            small_ref[3:4, cols] += _colsum(dz)
            w = cw_ref[:, cols]
            dup0 = (w[2:3] * dz + w[1:2] * dz1 + w[0:1] * dz2).astype(BF16)
            dup0_ref[:, cols] = dup0
            dh = dh + _dot_nt(dup0, w_ref[:, cols])
        xv = x_ref[...]
        r = lax.rsqrt(jnp.mean(xv * xv, axis=-1, keepdims=True) + EPS)
        xn = xv * r
        dg_ref[...] += _colsum(dh * xn)
        dxn = dh * g_ref[...]
        dx = dres_ref[...] + r * (dxn - xn * jnp.mean(dxn * xn, axis=-1, keepdims=True))
        dx_ref[...] = dx
        dxb_ref[...] = dx.astype(BF16)

        if nw:
            @pl.when(pl.program_id(0) == nt - 1)
            def _():
                for cp in _sibling_swap_copies(slab_refs, swapped_refs, *sems):
                    cp.wait()

    rows = lambda width: pl.BlockSpec((tm, width), lambda i: (nt - 1 - i, 0))
    return pl.pallas_call(
        body, name="ffn_up_bwd", grid=(nt,),
        out_shape=[jax.ShapeDtypeStruct((s, D_MODEL), F32), jax.ShapeDtypeStruct((s, D_MODEL), BF16),
                   jax.ShapeDtypeStruct((1, D_MODEL), F32), jax.ShapeDtypeStruct((s, n), BF16),
                   jax.ShapeDtypeStruct((SUBLANES, n), F32)] + _swap_shapes(swap),
        in_specs=[rows(n), rows(n), _resident((D_MODEL, n)), _resident((3, n)), rows(D_MODEL),
                  _resident((1, D_MODEL)), rows(D_MODEL)] + [_ANY] * nw,
        out_specs=[rows(D_MODEL), rows(D_MODEL), pl.BlockSpec((1, D_MODEL), lambda i: (0, 0)), rows(n),
                   pl.BlockSpec((SUBLANES, n), lambda i: (0, 0))] + [_ANY] * nw,
        scratch_shapes=[pltpu.VMEM((SUBLANES, n), F32)] + (_dma_sems(nw) if nw else []),
        compiler_params=_params(1))(d_up, up0, w_up, conv_w, x1, g, dres, *swap)


def _inproj_bwd(d_abcv, d_gates, d_qkvs, w_in_t, x, g, dres):
    s = x.shape[0]
    tm = ROW_TILE
    gw = GROUP_WIDTH

    def body(dabcv_ref, dgates_ref, dq0_ref, dq1_ref, dq2_ref, w_ref, x_ref, g_ref, dres_ref, dx_ref, dg_ref, scr):
        @pl.when(pl.program_id(0) == 0)
        def _():
            dg_ref[...] = jnp.zeros_like(dg_ref)

        dh = jnp.zeros((tm, D_MODEL), F32)
        for src, width, wrow in ((dabcv_ref, 3 * CONV_WIDTH, COL_ABCV), (dgates_ref, 2 * D_MODEL, COL_GATES)):
            for lo, size in _col_chunks(width, 512):
                dh = dh + _dot(src[:, lo:lo + size], w_ref[wrow + lo:wrow + lo + size, :])
        for gi, (d, dq_ref) in enumerate(zip(DILATIONS, (dq0_ref, dq1_ref, dq2_ref))):
            for j, base in enumerate((COL_Q, COL_K, COL_V)):
                dy = _from_streams(dq_ref, scr, d, j * gw, gw).astype(BF16)
                wrow = base + gi * gw
                dh = dh + _dot(dy, w_ref[wrow:wrow + gw, :])
        xv = x_ref[...]
        r = lax.rsqrt(jnp.mean(xv * xv, axis=-1, keepdims=True) + EPS)
        xn = xv * r
        dg_ref[...] += _colsum(dh * xn)
        dxn = dh * g_ref[...]
        dx_ref[...] = dres_ref[...] + r * (dxn - xn * jnp.mean(dxn * xn, axis=-1, keepdims=True))

    return pl.pallas_call(
        body, name="inproj_bwd", grid=(s // tm,),
        out_shape=[jax.ShapeDtypeStruct((s, D_MODEL), F32), jax.ShapeDtypeStruct((1, D_MODEL), F32)],
        in_specs=[_rows(tm, 3 * CONV_WIDTH), _rows(tm, 2 * D_MODEL)]
        + [_stream_block(tm, d, 3 * gw) for d in DILATIONS]
        + [_resident((D_IN, D_MODEL)), _rows(tm, D_MODEL), _resident((1, D_MODEL)), _rows(tm, D_MODEL)],
        out_specs=[_rows(tm, D_MODEL), pl.BlockSpec((1, D_MODEL), lambda i: (0, 0))],
        scratch_shapes=[pltpu.VMEM((gw // LANES * tm, LANES), F32)],
        compiler_params=_params(1))(d_abcv, d_gates, *d_qkvs, w_in_t, x, g, dres)


def _mix_bwd(dx1, abcv, gates, ya, yb, yb0, lsetot, conv_w, conv_b, b_gate, w_pa, w_pb, w_out, exchange=()):
    s = dx1.shape[0]
    tm = ROW_TILE
    nt = s // tm
    hb = tm // (2 * SUBLANES)
    nx = len(exchange)

    def body(*refs):
        (dx1_ref, abcv_ref, pre_ref, gates_ref, ya_ref, yb_ref, yb0_ref, lsetot_ref,
         cw_ref, cb_ref, bg_ref, wpa_ref, wpb_ref, wout_ref) = refs[:14]
        part_refs = refs[14:14 + nx]
        (dya_ref, dyb_ref, dgates_ref, dabcv_ref, dyb0_ref, dyl0_ref, dyl1_ref, dyl2_ref, aux0_ref, aux1_ref,
         aux2_ref, sm_gate_ref, sm_conv_ref) = refs[14 + nx:27 + nx]
        recv_refs = refs[27 + nx:27 + 2 * nx]
        next_ref, scr = refs[27 + 2 * nx:29 + 2 * nx]
        sems = refs[29 + 2 * nx:]
        i = pl.program_id(0)

        @pl.when(i == 0)
        def _():
            next_ref[...] = jnp.zeros_like(next_ref)
            sm_gate_ref[...] = jnp.zeros_like(sm_gate_ref)
            sm_conv_ref[...] = jnp.zeros_like(sm_conv_ref)
            for cp in _chip_exchange_copies(part_refs, recv_refs, *sems) if nx else ():
                cp.start()

        not_first = (i < nt - 1).astype(F32)
        dx1b = dx1_ref[...].astype(BF16)
        dya0 = jnp.zeros((tm, CONV_WIDTH), F32)
        dyb0s = [jnp.zeros((tm, GROUP_WIDTH), F32) for _ in range(3)]
        for lo, size in _col_chunks(D_MODEL, GROUP_WIDTH):
            ca, cb_ = slice(lo, lo + size), slice(D_MODEL + lo, D_MODEL + lo + size)
            dm = _dot_nt(dx1b, wout_ref[ca, :])
            sa = _sigmoid(gates_ref[:, ca].astype(F32) + bg_ref[0:1, ca])
            sb = _sigmoid(gates_ref[:, cb_].astype(F32) + bg_ref[1:2, ca])
            dya = (dm * sa).astype(BF16)
            dyb = (dm * sb).astype(BF16)
            dya_ref[:, ca] = dya
            dyb_ref[:, ca] = dyb
            dga = dm * ya_ref[:, ca].astype(F32) * (sa * (1.0 - sa))
            dgb = dm * yb_ref[:, ca].astype(F32) * (sb * (1.0 - sb))
            dgates_ref[:, ca] = dga.astype(BF16)
            dgates_ref[:, cb_] = dgb.astype(BF16)
            sm_gate_ref[0:1, ca] += _colsum(dga)
            sm_gate_ref[1:2, ca] += _colsum(dgb)
            dya0 = dya0 + _dot_nt(dya, wpa_ref[:, ca])
            for g in range(3):
                dyb0s[g] = dyb0s[g] + _dot_nt(dyb, wpb_ref[g * GROUP_WIDTH:(g + 1) * GROUP_WIDTH, ca])

        ab = abcv_ref[:, 0:CONV_WIDTH].astype(F32)
        ac = abcv_ref[:, CONV_WIDTH:2 * CONV_WIDTH].astype(F32)
        av = abcv_ref[:, 2 * CONV_WIDTH:3 * CONV_WIDTH].astype(F32)
        pre = pre_ref[...].astype(F32) * not_first
        halo_u = (pre[:, CONV_WIDTH:2 * CONV_WIDTH] * pre[:, 2 * CONV_WIDTH:3 * CONV_WIDTH])[SUBLANES:2 * SUBLANES]
        w = cw_ref[...]
        _, cv, u, sh1, sh2 = _conv_branch(ab, ac, av, halo_u, w, cb_ref[...])
        dcv = dya0 * ab
        sm_conv_ref[0:1, :] += _colsum(dcv * sh2)
        sm_conv_ref[1:2, :] += _colsum(dcv * sh1)
        sm_conv_ref[2:3, :] += _colsum(dcv * u)
        sm_conv_ref[3:4, :] += _colsum(dcv)
        nxt = next_ref[...]
        du = w[2:3] * dcv + w[1:2] * _shift_up(dcv, 1, nxt) + w[0:1] * _shift_up(dcv, 2, nxt)
        next_ref[...] = dcv[0:SUBLANES]
        dabcv_ref[:, 0:CONV_WIDTH] = (dya0 * cv).astype(BF16)
        dabcv_ref[:, CONV_WIDTH:2 * CONV_WIDTH] = (du * av).astype(BF16)
        dabcv_ref[:, 2 * CONV_WIDTH:3 * CONV_WIDTH] = (du * ac).astype(BF16)

        head_r = lax.broadcasted_iota(jnp.int32, (GROUP_WIDTH, GROUP_WIDTH), 0) // HEAD_DIM
        head_c = lax.broadcasted_iota(jnp.int32, (GROUP_WIDTH, GROUP_WIDTH), 1) // HEAD_DIM
        same_head = (head_r == head_c).astype(BF16)
        prod = jnp.zeros((tm, GROUP_WIDTH), F32)
        for g in range(3):
            cols = slice(g * GROUP_WIDTH, (g + 1) * GROUP_WIDTH)
            dyb0_ref[:, cols] = dyb0s[g].astype(BF16)
            prod = prod + dyb0s[g] * yb0_ref[:, cols].astype(F32)
        hi = prod.astype(BF16)
        mid = (prod - hi.astype(F32)).astype(BF16)
        lo = (prod - hi.astype(F32) - mid.astype(F32)).astype(BF16)
        delta = _dot(hi, same_head) + _dot(mid, same_head) + _dot(lo, same_head)
        lse_c = _compact_heads(lsetot_ref[...])
        delta_c = _compact_heads(delta)
        for g, (dy_ref, aux_ref) in enumerate(zip((dyl0_ref, dyl1_ref, dyl2_ref), (aux0_ref, aux1_ref, aux2_ref))):
            d = DILATIONS[g]
            _to_streams(dyb0s[g], scr, dy_ref, d, 0)
            _to_streams(lse_c, scr, aux_ref, d, 0)
            _to_streams(delta_c, scr, aux_ref, d, LANES)

        if nx:
            @pl.when(i == nt - 1)
            def _():
                for cp in _chip_exchange_copies(part_refs, recv_refs, *sems):
                    cp.wait()

    rev = lambda i: (nt - 1 - i, 0)
    pre = lambda i: (jnp.maximum((nt - 1 - i) * hb - 1, 0), 0)
    rows = lambda width: pl.BlockSpec((tm, width), rev)
    outs = [jax.ShapeDtypeStruct((s, D_MODEL), BF16), jax.ShapeDtypeStruct((s, D_MODEL), BF16),
            jax.ShapeDtypeStruct((s, 2 * D_MODEL), BF16), jax.ShapeDtypeStruct((s, 3 * CONV_WIDTH), BF16),
            jax.ShapeDtypeStruct((s, ATTN_WIDTH), BF16)]
    outs += [jax.ShapeDtypeStruct((d, s // d, GROUP_WIDTH), BF16) for d in DILATIONS]
    outs += [jax.ShapeDtypeStruct((d, s // d, 2 * LANES), F32) for d in DILATIONS]
    outs += [jax.ShapeDtypeStruct((SUBLANES, D_MODEL), F32), jax.ShapeDtypeStruct((SUBLANES, CONV_WIDTH), F32)]
    outs += [jax.ShapeDtypeStruct((3,) + a.shape[1:], a.dtype) for a in exchange]
    return pl.pallas_call(
        body, name="mix_bwd", grid=(nt,), out_shape=outs,
        in_specs=[rows(D_MODEL), rows(3 * CONV_WIDTH), pl.BlockSpec((2 * SUBLANES, 3 * CONV_WIDTH), pre),
                  rows(2 * D_MODEL), rows(D_MODEL), rows(D_MODEL), rows(ATTN_WIDTH), rows(GROUP_WIDTH),
                  _resident((3, CONV_WIDTH)), _resident((1, CONV_WIDTH)), _resident((2, D_MODEL)),
                  _resident((CONV_WIDTH, D_MODEL)), _resident((ATTN_WIDTH, D_MODEL)), _resident((D_MODEL, D_MODEL))]
        + [_ANY] * nx,
        out_specs=[rows(D_MODEL), rows(D_MODEL), rows(2 * D_MODEL), rows(3 * CONV_WIDTH), rows(ATTN_WIDTH)]
        + [_rev_stream_block(tm, d, GROUP_WIDTH, nt) for d in DILATIONS]
        + [_rev_stream_block(tm, d, 2 * LANES, nt) for d in DILATIONS]
        + [pl.BlockSpec((SUBLANES, D_MODEL), lambda i: (0, 0)), pl.BlockSpec((SUBLANES, CONV_WIDTH), lambda i: (0, 0))]
        + [_ANY] * nx,
        scratch_shapes=[pltpu.VMEM((SUBLANES, CONV_WIDTH), F32),
                        pltpu.VMEM((GROUP_WIDTH // LANES * tm, LANES), F32)] + (_dma_sems(3 * nx) if nx else []),
        compiler_params=_params(1))(dx1, abcv, abcv, gates, ya, yb, yb0, lsetot,
                                    conv_w, conv_b, b_gate, w_pa, w_pb, w_out, *exchange)


def _attn_bwd(qkv, dy, aux, gi, exchange=(), swap=()):
    d, length, _ = qkv.shape
    nb = length // ATTN_BLOCK
    q = ATTN_BLOCK
    gw = GROUP_WIDTH
    g = nb if nb <= ATTN_FWD_BLOCKS else ATTN_BWD_BLOCKS
    assert g >= 2 and nb % g == 0
    ns = nb // g
    lag = 1 if ns > 1 else 0
    tail = (g - 1) * q
    nx, nw = len(exchange), len(swap)

    def body(*refs):
        q_ref, kp_ref, kc_ref, vp_ref, vc_ref, dy_ref, aux_ref = refs[:7]
        part_refs = refs[7:7 + nx]
        slab_refs = refs[7 + nx:7 + nx + nw]
        pos = 7 + nx + nw
        out_ref = refs[pos]
        recv_refs = refs[pos + 1:pos + 1 + nx]
        swapped_refs = refs[pos + 1 + nx:pos + 1 + nx + nw]
        pos += 1 + nx + nw
        dq_ref, dkv_ref, bias_ref = refs[pos:pos + 3]
        sems = refs[pos + 3:]
        n = pl.program_id(1)

        def copies():
            cps = _chip_exchange_copies(part_refs, recv_refs, sems[0], sems[1]) if nx else []
            return cps + (_sibling_swap_copies(slab_refs, swapped_refs, sems[-2], sems[-1]) if nw else [])

        @pl.when((pl.program_id(0) == 0) & (n == 0))
        def _():
            _store_band_biases(bias_ref)
            for cp in copies():
                cp.start()

        if nx or nw:
            @pl.when((pl.program_id(0) == d - 1) & (n == ns - 1 + lag))
            def _():
                for cp in copies():
                    cp.wait()

        def emit(rows):
            out_ref[rows, gw:2 * gw] = dkv_ref[0, rows].astype(BF16)
            out_ref[rows, 2 * gw:3 * gw] = dkv_ref[1, rows].astype(BF16)

        if lag:
            @pl.when(n > 0)
            def _():
                out_ref[:, 0:gw] = dq_ref[...].astype(BF16)
                emit(slice(0, tail))

            @pl.when(n == ns)
            def _():
                emit(slice(tail, g * q))

        @pl.when(n < ns)
        def _():
            kfull = jnp.concatenate([kp_ref[...], kc_ref[...]], axis=0)
            vfull = jnp.concatenate([vp_ref[...], vc_ref[...]], axis=0)
            for j in range(g):
                rows = slice(j * q, (j + 1) * q)
                qs = _stack_heads(q_ref[rows, :])
                dys = _stack_heads(dy_ref[rows, :])
                k2 = kfull[j * q:(j + 2) * q]
                v2 = vfull[j * q:(j + 2) * q]
                lse = _compact_head_col(aux_ref[rows, 0:LANES])
                delta = _compact_head_col(aux_ref[rows, LANES:2 * LANES])
                bias = jnp.where(n == 0, bias_ref[1], bias_ref[0]) if j == 0 else bias_ref[0]
                p = jnp.exp(_dot_nt(qs, k2) + bias - lse)
                dp = _dot_nt(dys, v2)
                ds = (p * (dp - delta)).astype(BF16)
                dq_j = _unstack_heads(_dot(ds, k2)) * ATTN_SCALE
                dk2 = _dot_tn(ds, qs)
                dv2 = _dot_tn(p.astype(BF16), dys)
                if j == 0:
                    @pl.when(n > 0)
                    def _():
                        out_ref[tail:g * q, gw:2 * gw] = (dkv_ref[0, tail:g * q] + dk2[0:q]).astype(BF16)
                        out_ref[tail:g * q, 2 * gw:3 * gw] = (dkv_ref[1, tail:g * q] + dv2[0:q]).astype(BF16)
                else:
                    dkv_ref[0, (j - 1) * q:j * q] += dk2[0:q]
                    dkv_ref[1, (j - 1) * q:j * q] += dv2[0:q]
                dkv_ref[0, rows] = dk2[q:2 * q]
                dkv_ref[1, rows] = dv2[q:2 * q]
                dq_ref[rows, :] = dq_j
            if not lag:
                out_ref[:, 0:gw] = dq_ref[...].astype(BF16)
                emit(slice(0, g * q))

    cur, prev = _attn_block_specs(g, nb, clamp_last=True)
    return pl.pallas_call(
        body, name=f"attn_bwd_g{gi}", grid=(d, ns + lag),
        out_shape=[jax.ShapeDtypeStruct((d, length, 3 * gw), BF16)]
        + [jax.ShapeDtypeStruct((3,) + a.shape[1:], a.dtype) for a in exchange] + _swap_shapes(swap),
        in_specs=[cur(0), prev(1), cur(1), prev(2), cur(2), cur(0), cur(0, 2 * LANES)] + [_ANY] * (nx + nw),
        out_specs=[pl.BlockSpec((None, g * q, 3 * gw), lambda r, n: (r, jnp.maximum(n - lag, 0), 0))]
        + [_ANY] * (nx + nw),
        scratch_shapes=[pltpu.VMEM((g * q, gw), F32), pltpu.VMEM((2, g * q, gw), F32),
                        pltpu.VMEM((2, HEADS_PER_GROUP * q, 2 * q), F32)]
        + (_dma_sems(3 * nx) if nx else []) + (_dma_sems(nw) if nw else []),
        compiler_params=_params(2))(qkv, qkv, qkv, qkv, qkv, dy, aux, *exchange, *swap)


def _matmul_tn(name, a, b, col_tile=1024, row_tile=2048, slabs=0, swap=()):
    s, k = a.shape
    n = b.shape[1]
    tk = min(row_tile, s)
    tn = col_tile
    steps = s // tk
    nw = len(swap)

    def body(*refs):
        a_ref, b_ref = refs[:2]
        slab_refs = refs[2:2 + nw]
        o_ref = refs[2 + nw]
        swapped_refs = refs[3 + nw:3 + 2 * nw]
        acc_ref = refs[3 + 2 * nw]
        sems = refs[4 + 2 * nw:]
        t = pl.program_id(1)

        if nw:
            @pl.when((pl.program_id(0) == 0) & (t == 0))
            def _():
                for cp in _sibling_swap_copies(slab_refs, swapped_refs, *sems):
                    cp.start()

            @pl.when((pl.program_id(0) == n // tn - 1) & (t == steps - 1))
            def _():
                for cp in _sibling_swap_copies(slab_refs, swapped_refs, *sems):
                    cp.wait()

        @pl.when(t == 0)
        def _():
            acc_ref[...] = jnp.zeros_like(acc_ref)

        acc_ref[...] += _dot_tn(a_ref[...], b_ref[...])

        @pl.when(t == steps - 1)
        def _():
            if slabs:
                for q in range(per_tile):
                    o_ref[q] = acc_ref[:, q * width:(q + 1) * width].astype(BF16)
            else:
                o_ref[...] = acc_ref[...].astype(BF16)

    if slabs:
        width = n // slabs
        per_tile = tn // width
        out_shape = jax.ShapeDtypeStruct((slabs, k, width), BF16)
        out_spec = pl.BlockSpec((per_tile, k, width), lambda j, t: (j, 0, 0))
    else:
        out_shape = jax.ShapeDtypeStruct((k, n), BF16)
        out_spec = pl.BlockSpec((k, tn), lambda j, t: (0, j))
    res = pl.pallas_call(
        body, name=name, grid=(n // tn, steps), out_shape=[out_shape] + _swap_shapes(swap),
        in_specs=[pl.BlockSpec((tk, k), lambda j, t: (t, 0)), pl.BlockSpec((tk, tn), lambda j, t: (t, j))] + [_ANY] * nw,
        out_specs=[out_spec] + [_ANY] * nw,
        scratch_shapes=[pltpu.VMEM((k, tn), F32)] + (_dma_sems(nw) if nw else []),
        compiler_params=_params(2))(a, b, *swap)
    return res if nw else res[0]


def _sibling_swap_halves(name, slabs):
    na = len(slabs)

    def body(*refs):
        src_refs, out_refs = refs[:na], refs[na:2 * na]
        send_sems, recv_sems = refs[2 * na:]
        x, y, c, _ = _mesh_position()
        cps = []
        for a in range(na):
            theirs = _half_rows(1 - c, src_refs[a].shape[1] // 2)
            cps.append(pltpu.make_async_remote_copy(
                src_ref=src_refs[a].at[:, theirs, :], dst_ref=out_refs[a], send_sem=send_sems.at[a],
                recv_sem=recv_sems.at[a], device_id=(x, y, 1 - c), device_id_type=MESH_ID))
        for cp in cps:
            cp.start()
        for cp in cps:
            cp.wait()

    return pl.pallas_call(
        body, name=name,
        out_shape=[jax.ShapeDtypeStruct((a.shape[0], a.shape[1] // 2, a.shape[2]), a.dtype) for a in slabs],
        in_specs=[_ANY] * na, out_specs=[_ANY] * na,
        scratch_shapes=[pltpu.SemaphoreType.DMA((na,)), pltpu.SemaphoreType.DMA((na,))])(*slabs)


_HBM = pl.BlockSpec(memory_space=pltpu.HBM)
_SEM = pl.BlockSpec(memory_space=pltpu.SEMAPHORE)
_DATAFLOW = pltpu.SideEffectType.DATAFLOW_SIDE_EFFECTING


def _gather_start(shard):
    gathered = jax.ShapeDtypeStruct((N_CHIPS,) + shard.shape[1:], shard.dtype)

    def body(src_ref, buf_ref, send_sems, recv_sems, src_thru, buf_thru, token):
        for cp in _gather_first_copies([src_ref], [], [buf_ref], [], send_sems, recv_sems):
            cp.start()
        token[...] = jnp.zeros_like(token)

    return pl.pallas_call(
        body, name="gather_start",
        out_shape=(pltpu.SemaphoreType.DMA((3,)), pltpu.SemaphoreType.DMA((3,)),
                   pltpu.HBM(shard.shape, shard.dtype), pltpu.HBM(gathered.shape, gathered.dtype),
                   jax.ShapeDtypeStruct((SUBLANES, LANES), F32)),
        in_specs=(_HBM, _HBM), out_specs=(_SEM, _SEM, _HBM, _HBM, _VMEM), input_output_aliases={0: 2, 1: 3},
        compiler_params=pltpu.CompilerParams(has_side_effects=_DATAFLOW),
    )(pltpu.with_memory_space_constraint(shard, pltpu.HBM),
      pltpu.with_memory_space_constraint(lax.empty(gathered.shape, gathered.dtype), pltpu.HBM))


def _gather_forward(send_sems, recv_sems, shard_thru, buf_thru, after):
    def body(src_ref, buf_ref, send_sems, recv_sems, after_ref, fwd_send, fwd_recv, buf_out):
        first = _gather_first_copies([src_ref], [], [buf_ref], [], send_sems, recv_sems)
        for cp, fwd in zip(first, _gather_forward_copies([buf_ref], fwd_send, fwd_recv)):
            cp.wait_send()
            cp.wait_recv()
            fwd.start()

    return pl.pallas_call(
        body, name="gather_forward",
        out_shape=(pltpu.SemaphoreType.DMA((3,)), pltpu.SemaphoreType.DMA((3,)),
                   pltpu.HBM(buf_thru.shape, buf_thru.dtype)),
        in_specs=(_HBM, _HBM, _SEM, _SEM, _ANY), out_specs=(_SEM, _SEM, _HBM), input_output_aliases={1: 2},
        compiler_params=pltpu.CompilerParams(has_side_effects=_DATAFLOW),
    )(shard_thru, buf_thru, send_sems, recv_sems, after)


def _gather_wait(fwd_send, fwd_recv, buf_thru):
    def body(buf_ref, fwd_send, fwd_recv, buf_out):
        for cp in _gather_forward_copies([buf_ref], fwd_send, fwd_recv):
            cp.wait_send()
            cp.wait_recv()

    return pl.pallas_call(
        body, name="gather_wait", out_shape=pltpu.HBM(buf_thru.shape, buf_thru.dtype),
        in_specs=(_HBM, _SEM, _SEM), out_specs=_HBM, input_output_aliases={0: 0},
        compiler_params=pltpu.CompilerParams(has_side_effects=_DATAFLOW),
    )(buf_thru, fwd_send, fwd_recv)


def _chip_exchange_start(partial):
    _, rows, cols = partial.shape
    landing = jax.ShapeDtypeStruct((3, rows, cols), partial.dtype)

    def body(src_ref, land_ref, send_sems, recv_sems, src_thru, land_thru, token):
        for cp in _chip_exchange_copies([src_ref], [land_ref], send_sems, recv_sems):
            cp.start()
        token[...] = jnp.zeros_like(token)

    return pl.pallas_call(
        body, name="grad_exchange_start",
        out_shape=(pltpu.SemaphoreType.DMA((3,)), pltpu.SemaphoreType.DMA((3,)),
                   pltpu.HBM(partial.shape, partial.dtype), pltpu.HBM(landing.shape, landing.dtype),
                   jax.ShapeDtypeStruct((SUBLANES, LANES), F32)),
        in_specs=(_HBM, _HBM), out_specs=(_SEM, _SEM, _HBM, _HBM, _VMEM), input_output_aliases={0: 2, 1: 3},
        compiler_params=pltpu.CompilerParams(has_side_effects=_DATAFLOW),
    )(pltpu.with_memory_space_constraint(partial, pltpu.HBM),
      pltpu.with_memory_space_constraint(lax.empty(landing.shape, landing.dtype), pltpu.HBM))


def _chip_exchange_wait(send_sems, recv_sems, src_thru, land_thru, after):
    def body(src_ref, land_ref, send_sems, recv_sems, after_ref, src_out, land_out):
        for cp in _chip_exchange_copies([src_ref], [land_ref], send_sems, recv_sems):
            cp.wait_send()
            cp.wait_recv()

    return pl.pallas_call(
        body, name="grad_exchange_wait",
        out_shape=(pltpu.HBM(src_thru.shape, src_thru.dtype), pltpu.HBM(land_thru.shape, land_thru.dtype)),
        in_specs=(_HBM, _HBM, _SEM, _SEM, _ANY), out_specs=(_HBM, _HBM), input_output_aliases={0: 0, 1: 1},
        compiler_params=pltpu.CompilerParams(has_side_effects=_DATAFLOW),
    )(src_thru, land_thru, send_sems, recv_sems, after)


def _sibling_share_copies(refs, send_sems, recv_sems):
    x, y, c, _ = _mesh_position()
    cps = []
    for a, ref in enumerate(refs):
        mine = ref.at[0, _half_rows(c, ref.shape[1] // 2)]
        cps.append(_remote_copy(a, mine, mine, (x, y, 1 - c), send_sems, recv_sems))
    return cps


def _add_sibling(name, slabs, received, core):
    na = len(slabs)
    halves = [a.shape[1] // 2 for a in slabs]

    def body(core_ref, *refs):
        for a in range(na):
            refs[2 * na + a][...] = (refs[a][...].astype(F32) + refs[na + a][...].astype(F32)).astype(BF16)

    def block(a, mine):
        if mine:
            return pl.BlockSpec((None, halves[a], slabs[a].shape[2]), lambda s, core_ref: (s, core_ref[0], 0))
        return pl.BlockSpec((None, halves[a], slabs[a].shape[2]), lambda s, core_ref: (s, 0, 0))

    grid_spec = pltpu.PrefetchScalarGridSpec(
        num_scalar_prefetch=1, grid=(N_CHIPS,),
        in_specs=[block(a, True) for a in range(na)] + [block(a, False) for a in range(na)],
        out_specs=[block(a, False) for a in range(na)])
    return pl.pallas_call(body, name=name, grid_spec=grid_spec,
                          out_shape=[jax.ShapeDtypeStruct(r.shape, BF16) for r in received],
                          compiler_params=_params(1))(core, *slabs, *received)


def _sum_chips(name, partials, received, chip_core):
    na = len(partials)

    def body(cc_ref, *refs):
        for a in range(na):
            acc = refs[a][...].astype(F32)
            for k in range(3):
                acc = acc + refs[na + a][k].astype(F32)
            refs[2 * na + a][...] = acc

    def own(p):
        return pl.BlockSpec((None,) + p.shape[1:], lambda i, cc_ref: (cc_ref[0], 0, 0))

    def mine(p):
        return pl.BlockSpec((None,) + p.shape[1:], lambda i, cc_ref: (0, cc_ref[1], 0))

    grid_spec = pltpu.PrefetchScalarGridSpec(
        num_scalar_prefetch=1, grid=(1,),
        in_specs=[own(p) for p in partials] + [pl.BlockSpec(r.shape, lambda i, cc_ref: (0, 0, 0)) for r in received],
        out_specs=[mine(p) for p in partials])
    return pl.pallas_call(body, name=name, grid_spec=grid_spec,
                          out_shape=[jax.ShapeDtypeStruct((1, 2 * p.shape[1], p.shape[2]), F32) for p in partials],
                          compiler_params=_params(1))(chip_core, *partials, *received)


def _adam_math(w, g, m, v):
    nm = ADAM_B1 * m + (1.0 - ADAM_B1) * g
    nv = ADAM_B2 * v + (1.0 - ADAM_B2) * jnp.square(g)
    m_hat = nm / (1.0 - ADAM_B1 ** ADAM_STEP)
    v_hat = nv / (1.0 - ADAM_B2 ** ADAM_STEP)
    delta = -ADAM_LR * (m_hat / (jnp.sqrt(v_hat) + ADAM_EPS) + ADAM_WD * w)
    return delta, nm, nv


ADAMW_STEPS = 8


def _adamw(ws, gs, ms, vs):
    na = len(ws)

    def body(*refs):
        for a in range(na):
            w_ref, g_ref, m_ref, v_ref = (refs[k * na + a] for k in range(4))
            g_out_ref, d_ref, nm_ref, nv_ref = (refs[(4 + k) * na + a] for k in range(4))
            gv = g_ref[...]
            g_out_ref[...] = gv
            d_ref[...], nm_ref[...], nv_ref[...] = _adam_math(w_ref[...], gv, m_ref[...], v_ref[...])

    specs = [pl.BlockSpec((None, w.shape[1] // ADAMW_STEPS, w.shape[2]), lambda i: (0, i, 0)) for w in ws]
    outs = pl.pallas_call(
        body, name="adamw", grid=(ADAMW_STEPS,), out_shape=[jax.ShapeDtypeStruct(w.shape, F32) for w in ws] * 4,
        in_specs=specs * 4, out_specs=specs * 4, compiler_params=_params(1))(*ws, *gs, *ms, *vs)
    return [[outs[k * na + a] for k in range(4)] for a in range(na)]


SMALL_PARAMS = ("norm_mix_g", "b_gate", "conv_a_w", "conv_a_b", "norm_ffn_g", "ffn_conv_w", "ffn_conv_b", "final_norm_g")


def _small_update(partials, params, moments_m, moments_v, halves):
    na = len(partials)
    npar = len(SMALL_PARAMS)
    nh = len(halves)

    def body(*refs):
        in_refs = refs[:na]
        w_refs = refs[na:na + npar]
        m_refs = refs[na + npar:na + 2 * npar]
        v_refs = refs[na + 2 * npar:na + 3 * npar]
        pos = na + 3 * npar + nh
        loss_ref = refs[pos]
        out_refs = refs[pos + 1:pos + 1 + 4 * npar]
        big_refs = refs[pos + 1 + 4 * npar:pos + 1 + 4 * npar + nh]
        pos += 1 + 4 * npar + nh
        acc_refs = refs[pos:pos + na]
        recv_refs = refs[pos + na:pos + 4 * na]
        send_sems, recv_sems, share_send, share_recv = refs[pos + 4 * na:]
        x, y, c, _ = _mesh_position()
        chip = 2 * x + y
        for cp in _sibling_share_copies(big_refs, share_send, share_recv):
            cp.start()
        for a in range(na):
            acc_refs[a][...] = in_refs[a][...]
        for stage, peer in enumerate(((x, y, 1 - c), (x, 1 - y, c), (1 - x, y, c))):
            cps = []
            for a in range(na):
                k = stage * na + a
                cps.append(pltpu.make_async_remote_copy(src_ref=acc_refs[a], dst_ref=recv_refs[k], send_sem=send_sems.at[k],
                                                        recv_sem=recv_sems.at[k], device_id=peer, device_id_type=MESH_ID))
            for cp in cps:
                cp.start()
            for cp in cps:
                cp.wait()
            for a in range(na):
                acc_refs[a][...] = acc_refs[a][...] + recv_refs[stage * na + a][...]

        mix, ffn, fin, gate, conv, ffnc, loss = acc_refs
        loss_ref[...] = loss[...]

        def cols(width):
            return pl.ds(pl.multiple_of(chip * width, LANES), width)

        grads = {
            "norm_mix_g": mix[...], "norm_ffn_g": ffn[...], "final_norm_g": fin[...],
            "b_gate": gate[0:2, cols(D_MODEL // N_CHIPS)],
            "conv_a_w": conv[0:3, cols(CONV_WIDTH // N_CHIPS)], "conv_a_b": conv[3:4, :],
            "ffn_conv_w": ffnc[0:3, cols(2 * D_FF // N_CHIPS)], "ffn_conv_b": ffnc[3:4, :]}
        for i, name in enumerate(SMALL_PARAMS):
            g = grads[name]
            if len(w_refs[i].shape) == 3:
                results = (g,) + _adam_math(w_refs[i][0], g, m_refs[i][0], v_refs[i][0])
                for o_ref, val in zip(out_refs[4 * i:4 * i + 4], results):
                    o_ref[0] = val
            else:
                results = (g,) + _adam_math(w_refs[i][...], g, m_refs[i][...], v_refs[i][...])
                for o_ref, val in zip(out_refs[4 * i:4 * i + 4], results):
                    o_ref[...] = val

        for cp in _sibling_share_copies(big_refs, share_send, share_recv):
            cp.wait()

    outs = [jax.ShapeDtypeStruct(partials[-1].shape, F32)]
    for w in params:
        outs += [jax.ShapeDtypeStruct(w.shape, F32)] * 4
    n_small_out = len(outs)
    outs += [jax.ShapeDtypeStruct(h.shape, h.dtype) for h in halves]
    scratch = [pltpu.VMEM(p.shape, F32) for p in partials]
    scratch += [pltpu.VMEM(p.shape, F32) for _ in range(3) for p in partials]
    scratch += _dma_sems(3 * na) + _dma_sems(nh)
    n_in = na + 3 * npar
    return pl.pallas_call(
        body, name="small_update", out_shape=outs, in_specs=[_VMEM] * n_in + [_ANY] * nh,
        out_specs=[_VMEM] * n_small_out + [_ANY] * nh,
        input_output_aliases={n_in + a: n_small_out + a for a in range(nh)},
        scratch_shapes=scratch)(*partials, *params, *moments_m, *moments_v, *halves)


def _gathered_columns(g):
    return jnp.transpose(g, (1, 0, 2)).reshape(g.shape[1], N_CHIPS * g.shape[2])


def kernel(x, norm_mix_g, w_in, b_gate, conv_a_w, conv_a_b, w_proj_a, w_proj_b, w_out, norm_ffn_g, w_up, ffn_conv_w, ffn_conv_b, w_down, final_norm_g, loss_target, m_norm_mix_g, m_w_in, m_b_gate, m_conv_a_w, m_conv_a_b, m_w_proj_a, m_w_proj_b, m_w_out, m_norm_ffn_g, m_w_up, m_ffn_conv_w, m_ffn_conv_b, m_w_down, m_final_norm_g, v_norm_mix_g, v_w_in, v_b_gate, v_conv_a_w, v_conv_a_b, v_w_proj_a, v_w_proj_b, v_w_out, v_norm_ffn_g, v_w_up, v_ffn_conv_w, v_ffn_conv_b, v_w_down, v_final_norm_g):
    chip = (2 * lax.axis_index("x") + lax.axis_index("y")).astype(jnp.int32)
    core = lax.axis_index("c").astype(jnp.int32)
    core_arr = core.reshape(1)
    chip_core = jnp.stack([chip, core])
    xs, target = x[0], loss_target[0]
    g_final = final_norm_g.reshape(1, D_MODEL)

    def own_slot(gathered, own):
        return lax.dynamic_update_slice(gathered, own, (chip, 0, 0))

    w_in_t, m_w_in_t, v_w_in_t = (jnp.swapaxes(a, 1, 2) for a in (w_in, m_w_in, v_w_in))
    w_in_tb = w_in_t.astype(BF16)
    send1, recv1, shard_thru, g_in, token = _gather_start(w_in_tb)
    h1, h1_streams4, h1_streams16 = _norm_fwd(xs, norm_mix_g + token[0:1, 0:1])
    send2, recv2, g_in = _gather_forward(send1, recv1, shard_thru, g_in, h1)
    g_in = _gather_wait(send2, recv2, g_in)
    w_in_full_t = own_slot(g_in, w_in_tb).reshape(D_IN, D_MODEL)
    later_w = [w_proj_a, w_proj_b, w_out, w_up, w_down]
    later_b = [w.astype(BF16) for w in later_w]
    small_sharded = [b_gate, conv_a_w, ffn_conv_w]
    fwd = _inproj_fwd(h1, w_in_full_t, later_b, small_sharded)
    abcv, gates, qkv0, qkv1, qkv2 = fwd[:5]
    gathered_big, gathered_small = fwd[5:10], fwd[10:13]
    attn0 = _attn_fwd(qkv0, 0, forward=gathered_big)
    attn = [attn0[:2], _attn_fwd(qkv1, 1), _attn_fwd(qkv2, 2)]
    g_pa, g_pb, g_out, g_up, g_down = [own_slot(g, own) for g, own in zip(attn0[2:], later_b)]
    g_bgate, g_convw, g_ffnw = [own_slot(g, own) for g, own in zip(gathered_small, small_sharded)]
    w_pa_full, w_pb_full, w_up_full = _gathered_columns(g_pa), _gathered_columns(g_pb), _gathered_columns(g_up)
    w_out_full, w_down_full = g_out.reshape(D_MODEL, D_MODEL), g_down.reshape(D_FF, D_MODEL)
    b_gate_full, conv_w_full, ffn_w_full = (_gathered_columns(g) for g in (g_bgate, g_convw, g_ffnw))

    x1, ya0, yb0, mrg, ya, yb, lsetot = _mix_fwd(
        xs, abcv, gates, [a[0] for a in attn], [a[1] for a in attn], conv_w_full, conv_a_b, b_gate_full,
        w_pa_full, w_pb_full, w_out_full)
    h2, up0, up = _ffn_up_fwd(x1, norm_ffn_g, w_up_full, ffn_w_full, ffn_conv_b)
    act, dx2, dx2b, d_g_final, loss = _ffn_act_fwd(x1, up, target, w_down_full, g_final)

    d_up = _ffn_act_bwd(dx2b, up, w_down_full)
    slab_down = _matmul_tn("dw_down", act, dx2b, col_tile=512).reshape(N_CHIPS, D_FF // N_CHIPS, D_MODEL)
    dx1, dx1b, d_g_ffn, d_up0, ffn_small, swapped_down = _ffn_up_bwd(
        d_up, up0, w_up_full, ffn_w_full, x1, norm_ffn_g, dx2, swap=[slab_down])
    (partial_down,) = _add_sibling("grad_add_w_down", [slab_down], [swapped_down], core_arr)
    slab_up = _matmul_tn("dw_up", h2, d_up0, col_tile=2 * D_FF // N_CHIPS, slabs=N_CHIPS)
    d_w_out, swapped_up = _matmul_tn("dw_out", mrg, dx1b, swap=[slab_up])
    (partial_up,) = _add_sibling("grad_add_w_up", [slab_up], [swapped_up], core_arr)

    mix_res = _mix_bwd(dx1, abcv, gates, ya, yb, yb0, lsetot, conv_w_full, conv_a_b, b_gate_full,
                       w_pa_full, w_pb_full, w_out_full, exchange=[partial_up, partial_down])
    (d_ya, d_yb, d_gates, d_abcv, d_yb0, dyl0, dyl1, dyl2, aux0, aux1, aux2, gate_small, conv_small) = mix_res[:13]
    halves_ffn = _sum_chips("grad_sum_ffn", [partial_up, partial_down], mix_res[13:], chip_core)

    slabs_mix = [_matmul_tn("dw_proj_a", ya0, d_ya, slabs=N_CHIPS), _matmul_tn("dw_proj_b", yb0, d_yb, slabs=N_CHIPS),
                 d_w_out.reshape(N_CHIPS, D_MODEL // N_CHIPS, D_MODEL)]
    res0 = _attn_bwd(qkv0, dyl0, aux0, 0, swap=slabs_mix)
    d_qkv0, partials_mix = res0[0], _add_sibling("grad_add_mix", slabs_mix, res0[1:], core_arr)
    res1 = _attn_bwd(qkv1, dyl1, aux1, 1, exchange=partials_mix)
    d_qkv1, halves_mix = res1[0], _sum_chips("grad_sum_mix", partials_mix, res1[1:], chip_core)
    (d_qkv2,) = _attn_bwd(qkv2, dyl2, aux2, 2)

    dq = [d_qkv0, d_qkv1, d_qkv2]
    seq = xs.shape[0]
    d_w_abcv = _matmul_tn("dw_in_abcv", d_abcv, h1)
    d_w_gates = _matmul_tn("dw_in_gates", d_gates, h1)
    d_w_groups = [_matmul_tn(f"dw_in_qkv{g}", t.reshape(seq, 3 * GROUP_WIDTH), h.reshape(seq, D_MODEL))
                  for g, (t, h) in enumerate(zip(dq, (h1, h1_streams4, h1_streams16)))]
    gw = GROUP_WIDTH
    d_w_in_t = jnp.concatenate(
        [d_w_abcv] + [d_w_groups[g][j * gw:(j + 1) * gw] for j in range(3) for g in range(3)] + [d_w_gates], axis=0)

    slab_in = d_w_in_t.reshape(N_CHIPS, D_IN // N_CHIPS, D_MODEL)
    (from_sibling_in,) = _sibling_swap_halves("grad_swap_w_in", [slab_in])
    (partial_in,) = _add_sibling("grad_add_w_in", [slab_in], [from_sibling_in], core_arr)
    send_sems, recv_sems, partial_thru, landing_thru, token = _chip_exchange_start(partial_in)
    g_mix_after_start = norm_mix_g + token[0:1, 0:1]
    grad_x, d_g_mix = _inproj_bwd(d_abcv, d_gates, dq, w_in_full_t, xs, g_mix_after_start, dx1)
    partial_in, received_in = _chip_exchange_wait(send_sems, recv_sems, partial_thru, landing_thru, d_g_mix)
    halves_in = _sum_chips("grad_sum_w_in", [partial_in], [received_in], chip_core)

    big_names = ("w_in", "w_proj_a", "w_proj_b", "w_out", "w_up", "w_down")
    big_w = dict(w_in=w_in_t, w_proj_a=w_proj_a, w_proj_b=w_proj_b, w_out=w_out, w_up=w_up, w_down=w_down)
    big_m = dict(w_in=m_w_in_t, w_proj_a=m_w_proj_a, w_proj_b=m_w_proj_b, w_out=m_w_out, w_up=m_w_up, w_down=m_w_down)
    big_v = dict(w_in=v_w_in_t, w_proj_a=v_w_proj_a, w_proj_b=v_w_proj_b, w_out=v_w_out, w_up=v_w_up, w_down=v_w_down)

    fin_w, fin_m, fin_v = (a.reshape(1, D_MODEL) for a in (final_norm_g, m_final_norm_g, v_final_norm_g))
    small_w = [norm_mix_g, b_gate, conv_a_w, conv_a_b, norm_ffn_g, ffn_conv_w, ffn_conv_b, fin_w]
    small_m = [m_norm_mix_g, m_b_gate, m_conv_a_w, m_conv_a_b, m_norm_ffn_g, m_ffn_conv_w, m_ffn_conv_b, fin_m]
    small_v = [v_norm_mix_g, v_b_gate, v_conv_a_w, v_conv_a_b, v_norm_ffn_g, v_ffn_conv_w, v_ffn_conv_b, fin_v]
    small_out = _small_update([d_g_mix, d_g_ffn, d_g_final, gate_small, conv_small, ffn_small, loss],
                              small_w, small_m, small_v, halves_in + halves_mix + halves_ffn)
    big_grads = small_out[1 + 4 * len(SMALL_PARAMS):]
    total_loss = small_out[0][0, 0]

    grads, delta, new_m, new_v = {}, {}, {}, {}
    for i, n in enumerate(SMALL_PARAMS):
        vals = small_out[1 + 4 * i:5 + 4 * i]
        if n == "final_norm_g":
            vals = [a.reshape(D_MODEL) for a in vals]
        grads[n], delta[n], new_m[n], new_v[n] = vals
    updates = _adamw([big_w[n] for n in big_names], big_grads, [big_m[n] for n in big_names],
                     [big_v[n] for n in big_names])
    for n, vals in zip(big_names, updates):
        if n == "w_in":
            vals = [jnp.swapaxes(a, 1, 2) for a in vals]
        grads[n], delta[n], new_m[n], new_v[n] = vals

    names = ["norm_mix_g", "w_in", "b_gate", "conv_a_w", "conv_a_b", "w_proj_a", "w_proj_b", "w_out", "norm_ffn_g", "w_up",
             "ffn_conv_w", "ffn_conv_b", "w_down", "final_norm_g"]
    out = [total_loss, grad_x[None]]
    for group in (grads, delta, new_m, new_v):
        out += [group[n] for n in names]
    return tuple(out)
```

```python
import jax
import jax.numpy as jnp
from jax import lax
from jax.experimental import pallas as pl
from jax.experimental.pallas import tpu as pltpu

F32 = jnp.float32
BF16 = jnp.bfloat16

D_MODEL = 1024
CONV_WIDTH = 512
ATTN_WIDTH = 768
GROUP_WIDTH = 256
HEAD_DIM = 64
HEADS_PER_GROUP = 4
DILATIONS = (1, 4, 16)
ATTN_BLOCK = 128
D_FF = 2816
D_IN = 5888
EPS = 1e-6
NEG_INF = -1e30
ATTN_SCALE = HEAD_DIM ** -0.5

COL_ABCV = 0
COL_Q = 1536
COL_K = 2304
COL_V = 3072
COL_GATES = 3840

ADAM_LR = 0.001
ADAM_B1 = 0.9
ADAM_B2 = 0.999
ADAM_EPS = 1e-08
ADAM_WD = 0.01
ADAM_STEP = 10

LANES = 128
SUBLANES = 8
BF16_ROWS = 16
ROW_TILE = 512
VMEM_LIMIT = 56 * 1024 * 1024

_NT = (((1,), (1,)), ((), ()))
_TN = (((0,), (0,)), ((), ()))


def _params(n_axes, vmem=VMEM_LIMIT):
    return pltpu.CompilerParams(dimension_semantics=("arbitrary",) * n_axes, vmem_limit_bytes=vmem)


def _resident(shape):
    nd = len(shape)
    return pl.BlockSpec(shape, lambda *_: (0,) * nd, pipeline_mode=pl.Buffered(1))


def _rows(tm, width, col_block=0):
    return pl.BlockSpec((tm, width), lambda i: (i, col_block))


def _col_chunks(n, cmax):
    out, lo = [], 0
    while lo < n:
        size = min(cmax, n - lo)
        out.append((lo, size))
        lo += size
    return out


def _dot(a, b):
    return jnp.dot(a, b, preferred_element_type=F32)


def _dot_nt(a, b):
    return lax.dot_general(a, b, _NT, preferred_element_type=F32)


def _dot_tn(a, b):
    return lax.dot_general(a, b, _TN, preferred_element_type=F32)


def _sigmoid(x):
    return 0.5 * jnp.tanh(0.5 * x) + 0.5


def _silu(x):
    hx = 0.5 * x
    return hx + hx * jnp.tanh(hx)


def _shift_down(v, k, halo8):
    tm = v.shape[0]
    rolled = pltpu.roll(v, k, 0)
    fix = jnp.tile(pltpu.roll(halo8, k, 0), (tm // SUBLANES, 1))
    row = lax.broadcasted_iota(jnp.int32, v.shape, 0)
    return jnp.where(row < k, fix, rolled)


def _shift_up(v, k, halo8):
    tm = v.shape[0]
    rolled = pltpu.roll(v, tm - k, 0)
    fix = jnp.tile(pltpu.roll(halo8, SUBLANES - k, 0), (tm // SUBLANES, 1))
    row = lax.broadcasted_iota(jnp.int32, v.shape, 0)
    return jnp.where(row >= tm - k, fix, rolled)


def _colsum(v):
    return jnp.sum(v, axis=0, keepdims=True)


def _to_streams(val, scr, out_ref, d, col0):
    tm = val.shape[0]
    panels = val.shape[1] // LANES
    if d == 1:
        out_ref[0, :, col0:col0 + val.shape[1]] = val.astype(out_ref.dtype)
        return
    for p in range(panels):
        scr[pl.ds(p * tm, tm), :] = val[:, p * LANES:(p + 1) * LANES]
    for r in range(d):
        for p in range(panels):
            piece = scr[pl.ds(p * tm + r, tm // d, stride=d), :]
            out_ref[r, :, col0 + p * LANES: col0 + (p + 1) * LANES] = piece.astype(out_ref.dtype)


def _from_streams(in_ref, scr, d, col0, width):
    panels = width // LANES
    rows = in_ref.shape[1]
    tm = rows * d
    if d == 1:
        return in_ref[0, :, col0:col0 + width].astype(F32)
    for r in range(d):
        for p in range(panels):
            scr[pl.ds(p * tm + r, rows, stride=d), :] = in_ref[r, :, col0 + p * LANES: col0 + (p + 1) * LANES].astype(F32)
    return jnp.concatenate([scr[pl.ds(p * tm, tm), :] for p in range(panels)], axis=1)


def _stream_block(tm, d, width):
    return pl.BlockSpec((d, tm // d, width), lambda i: (0, i, 0))


def _rev_stream_block(tm, d, width, nt):
    return pl.BlockSpec((d, tm // d, width), lambda i: (0, nt - 1 - i, 0))


N_CHIPS = 4
MESH_ID = pl.DeviceIdType.MESH
_ANY = pl.BlockSpec(memory_space=pl.ANY)
_VMEM = pl.BlockSpec(memory_space=pltpu.VMEM)


def _mesh_position():
    x, y, c = lax.axis_index("x"), lax.axis_index("y"), lax.axis_index("c")
    other_chips = [(1 - x, y), (x, 1 - y), (1 - x, 1 - y)]
    return x, y, c, other_chips


def _half_rows(c, half):
    return pl.ds(pl.multiple_of(c * half, BF16_ROWS), half)


def _remote_copy(k, src, dst, to, send_sems, recv_sems):
    return pltpu.make_async_remote_copy(src_ref=src, dst_ref=dst, send_sem=send_sems.at[k], recv_sem=recv_sems.at[k],
                                        device_id=to, device_id_type=MESH_ID)


def _gather_first_copies(big_refs, small_refs, big_outs, small_outs, send_sems, recv_sems):
    x, y, c, chips = _mesh_position()
    me = 2 * x + y
    nb = len(big_refs)
    cps = []
    for j, (px, py) in enumerate(chips):
        for b in range(nb):
            mine = _half_rows(c, big_refs[b].shape[1] // 2)
            cps.append(_remote_copy(3 * b + j, big_refs[b].at[0, mine], big_outs[b].at[me, mine], (px, py, c),
                                    send_sems, recv_sems))
        for s in range(len(small_refs)):
            cps.append(_remote_copy(3 * (nb + s) + j, small_refs[s].at[0], small_outs[s].at[me], (px, py, c),
                                    send_sems, recv_sems))
    return cps


def _gather_forward_copies(bufs, send_sems, recv_sems):
    x, y, c, chips = _mesh_position()
    cps = []
    for j, (px, py) in enumerate(chips):
        for b in range(len(bufs)):
            landed = bufs[b].at[2 * px + py, _half_rows(c, bufs[b].shape[1] // 2)]
            cps.append(_remote_copy(3 * b + j, landed, landed, (x, y, 1 - c), send_sems, recv_sems))
    return cps


def _chip_exchange_copies(src_refs, out_refs, send_sems, recv_sems):
    x, y, c, chips = _mesh_position()
    cps = []
    for j, (px, py) in enumerate(chips):
        for a in range(len(src_refs)):
            cps.append(_remote_copy(3 * a + j, src_refs[a].at[2 * px + py], out_refs[a].at[j], (px, py, c),
                                    send_sems, recv_sems))
    return cps


def _sibling_swap_copies(src_refs, out_refs, send_sems, recv_sems):
    x, y, c, _ = _mesh_position()
    cps = []
    for a in range(len(src_refs)):
        theirs = _half_rows(1 - c, src_refs[a].shape[1] // 2)
        cps.append(_remote_copy(a, src_refs[a].at[:, theirs, :], out_refs[a], (x, y, 1 - c), send_sems, recv_sems))
    return cps


def _swap_shapes(slabs):
    return [jax.ShapeDtypeStruct((a.shape[0], a.shape[1] // 2, a.shape[2]), a.dtype) for a in slabs]


def _dma_sems(n):
    return [pltpu.SemaphoreType.DMA((n,)), pltpu.SemaphoreType.DMA((n,))]


def _norm_fwd(x, g):
    s = x.shape[0]
    tm = ROW_TILE

    def body(x_ref, g_ref, h_ref, hs1_ref, hs2_ref, scr):
        xv = x_ref[...]
        r = lax.rsqrt(jnp.mean(xv * xv, axis=-1, keepdims=True) + EPS)
        hf = xv * r * g_ref[...]
        h_ref[...] = hf.astype(BF16)
        for d, hs_ref in zip(DILATIONS[1:], (hs1_ref, hs2_ref)):
            for lo, size in _col_chunks(D_MODEL, GROUP_WIDTH):
                _to_streams(hf[:, lo:lo + size], scr, hs_ref, d, lo)

    return pl.pallas_call(
        body, name="norm_fwd", grid=(s // tm,),
        out_shape=[jax.ShapeDtypeStruct((s, D_MODEL), BF16)]
        + [jax.ShapeDtypeStruct((d, s // d, D_MODEL), BF16) for d in DILATIONS[1:]],
        in_specs=[_rows(tm, D_MODEL), _resident((1, D_MODEL))],
        out_specs=[_rows(tm, D_MODEL)] + [_stream_block(tm, d, D_MODEL) for d in DILATIONS[1:]],
        scratch_shapes=[pltpu.VMEM((GROUP_WIDTH // LANES * tm, LANES), F32)],
        compiler_params=_params(1))(x, g)


def _inproj_fwd(h1, w_in_t, big_shards, small_shards):
    s = h1.shape[0]
    tm = ROW_TILE
    nt = s // tm
    nb, ns = len(big_shards), len(small_shards)
    n_fixed_in, n_fixed_out = 2, 5

    def body(*refs):
        h_ref, w_ref = refs[:n_fixed_in]
        shard_refs = refs[n_fixed_in:n_fixed_in + nb + ns]
        pos = n_fixed_in + nb + ns
        abcv_ref, gates_ref, qkv0_ref, qkv1_ref, qkv2_ref = refs[pos:pos + n_fixed_out]
        gathered_refs = refs[pos + n_fixed_out:pos + n_fixed_out + nb + ns]
        scr, send_sems, recv_sems = refs[pos + n_fixed_out + nb + ns:]
        i = pl.program_id(0)

        def gather_copies():
            return _gather_first_copies(shard_refs[:nb], shard_refs[nb:], gathered_refs[:nb], gathered_refs[nb:],
                                        send_sems, recv_sems)

        @pl.when(i == 0)
        def _():
            for cp in gather_copies():
                cp.start()

        h = h_ref[...]
        for lo, size in _col_chunks(3 * CONV_WIDTH, 512):
            abcv_ref[:, lo:lo + size] = _dot_nt(h, w_ref[COL_ABCV + lo: COL_ABCV + lo + size, :]).astype(BF16)
        for lo, size in _col_chunks(2 * D_MODEL, 512):
            gates_ref[:, lo:lo + size] = _dot_nt(h, w_ref[COL_GATES + lo: COL_GATES + lo + size, :]).astype(BF16)
        for gi, (d, out_ref) in enumerate(zip(DILATIONS, (qkv0_ref, qkv1_ref, qkv2_ref))):
            for j, base in enumerate((COL_Q, COL_K, COL_V)):
                lo = base + gi * GROUP_WIDTH
                y = _dot_nt(h, w_ref[lo:lo + GROUP_WIDTH, :])
                if j == 0:
                    y = y * ATTN_SCALE
                _to_streams(y, scr, out_ref, d, j * GROUP_WIDTH)

        @pl.when(i == nt - 1)
        def _():
            for cp in gather_copies():
                cp.wait()

    outs = [jax.ShapeDtypeStruct((s, 3 * CONV_WIDTH), BF16), jax.ShapeDtypeStruct((s, 2 * D_MODEL), BF16)]
    outs += [jax.ShapeDtypeStruct((d, s // d, 3 * GROUP_WIDTH), BF16) for d in DILATIONS]
    outs += [jax.ShapeDtypeStruct((N_CHIPS,) + a.shape[1:], a.dtype) for a in list(big_shards) + list(small_shards)]
    return pl.pallas_call(
        body, name="inproj_fwd", grid=(nt,), out_shape=outs,
        in_specs=[_rows(tm, D_MODEL), _resident((D_IN, D_MODEL))] + [_ANY] * (nb + ns),
        out_specs=[_rows(tm, 3 * CONV_WIDTH), _rows(tm, 2 * D_MODEL)]
        + [_stream_block(tm, d, 3 * GROUP_WIDTH) for d in DILATIONS] + [_ANY] * (nb + ns),
        scratch_shapes=[pltpu.VMEM((GROUP_WIDTH // LANES * tm, LANES), F32)] + _dma_sems(3 * (nb + ns)),
        compiler_params=_params(1))(h1, w_in_t, *big_shards, *small_shards)


def _head_of_lane(shape):
    return lax.broadcasted_iota(jnp.int32, shape, 1) // HEAD_DIM


def _stack_heads(v):
    head = _head_of_lane(v.shape)
    return jnp.concatenate([jnp.where(head == h, v, jnp.zeros_like(v)) for h in range(HEADS_PER_GROUP)], axis=0)


def _unstack_heads(v):
    q = ATTN_BLOCK
    head = _head_of_lane((q, v.shape[1]))
    out = jnp.zeros((q, v.shape[1]), v.dtype)
    for h in range(HEADS_PER_GROUP):
        out = jnp.where(head == h, v[h * q:(h + 1) * q], out)
    return out


def _per_head_rows(col):
    q = ATTN_BLOCK
    head = _head_of_lane((q, GROUP_WIDTH))
    out = jnp.zeros((q, GROUP_WIDTH), col.dtype)
    for h in range(HEADS_PER_GROUP):
        out = jnp.where(head == h, col[h * q:(h + 1) * q], out)
    return out


def _compact_heads(v):
    lane = lax.broadcasted_iota(jnp.int32, (v.shape[0], LANES), 1)
    return jnp.where((lane & 32) == 0, v[:, 0:LANES], v[:, LANES:2 * LANES])


def _compact_head_col(v):
    lane = lax.broadcasted_iota(jnp.int32, v.shape, 1)
    head = ((lane >> 6) & 1) + 2 * ((lane >> 5) & 1)
    cols = [jnp.max(jnp.where(head == h, v, -jnp.inf), axis=1, keepdims=True) for h in range(HEADS_PER_GROUP)]
    return jnp.concatenate(cols, axis=0)


ATTN_FWD_BLOCKS = 16
ATTN_BWD_BLOCKS = 8


def _band_bias(first_block):
    rows = HEADS_PER_GROUP * ATTN_BLOCK
    qi = lax.broadcasted_iota(jnp.int32, (rows, 2 * ATTN_BLOCK), 0) % ATTN_BLOCK
    kj = lax.broadcasted_iota(jnp.int32, (rows, 2 * ATTN_BLOCK), 1)
    dist = qi + ATTN_BLOCK - kj
    valid = (dist >= 0) & (dist <= ATTN_BLOCK)
    if first_block:
        valid = valid & (kj >= ATTN_BLOCK)
    return jnp.where(valid, 0.0, NEG_INF).astype(F32)


def _store_band_biases(bias_ref):
    bias_ref[0] = _band_bias(False)
    bias_ref[1] = _band_bias(True)


def _attn_block_specs(g, nb, clamp_last=False):
    q = ATTN_BLOCK
    last = nb // g - 1

    def cur(col, width=GROUP_WIDTH):
        if clamp_last:
            return pl.BlockSpec((None, g * q, width), lambda r, n: (r, jnp.minimum(n, last), col))
        return pl.BlockSpec((None, g * q, width), lambda r, n: (r, n, col))

    def prev(col):
        if clamp_last:
            return pl.BlockSpec((None, q, GROUP_WIDTH), lambda r, n: (r, jnp.clip(n * g - 1, 0, nb - 1), col))
        return pl.BlockSpec((None, q, GROUP_WIDTH), lambda r, n: (r, jnp.maximum(n * g - 1, 0), col))

    return cur, prev


def _attn_fwd(qkv, gi, forward=()):
    d, length, _ = qkv.shape
    nb = length // ATTN_BLOCK
    q = ATTN_BLOCK
    g = min(ATTN_FWD_BLOCKS, nb)
    ns = nb // g
    nf = len(forward)

    def body(*refs):
        q_ref, kp_ref, kc_ref, vp_ref, vc_ref = refs[:5]
        o_ref, lse_ref = refs[5 + nf:7 + nf]
        buf_refs = refs[7 + nf:7 + 2 * nf]
        bias_ref = refs[7 + 2 * nf]
        sems = refs[8 + 2 * nf:]
        n = pl.program_id(1)
        first_step = (pl.program_id(0) == 0) & (n == 0)
        last_step = (pl.program_id(0) == d - 1) & (n == ns - 1)

        @pl.when(first_step)
        def _():
            _store_band_biases(bias_ref)
            for cp in _gather_forward_copies(buf_refs, *sems) if nf else ():
                cp.start()

        kfull = jnp.concatenate([kp_ref[...], kc_ref[...]], axis=0)
        vfull = jnp.concatenate([vp_ref[...], vc_ref[...]], axis=0)
        for j in range(g):
            qs = _stack_heads(q_ref[j * q:(j + 1) * q, :])
            k2 = kfull[j * q:(j + 2) * q]
            v2 = vfull[j * q:(j + 2) * q]
            bias = jnp.where(n == 0, bias_ref[1], bias_ref[0]) if j == 0 else bias_ref[0]
            sc = _dot_nt(qs, k2) + bias
            m = jnp.max(sc, axis=1, keepdims=True)
            p = jnp.exp(sc - m)
            l = jnp.sum(p, axis=1, keepdims=True)
            of = _dot(p.astype(BF16), v2) / l
            o_ref[j * q:(j + 1) * q, :] = _unstack_heads(of).astype(BF16)
            lse_ref[j * q:(j + 1) * q, :] = _per_head_rows(m + jnp.log(l))

        if nf:
            @pl.when(last_step)
            def _():
                for cp in _gather_forward_copies(buf_refs, *sems):
                    cp.wait()

    cur, prev = _attn_block_specs(g, nb)
    return pl.pallas_call(
        body, name=f"attn_fwd_g{gi}", grid=(d, ns),
        out_shape=[jax.ShapeDtypeStruct((d, length, GROUP_WIDTH), BF16),
                   jax.ShapeDtypeStruct((d, length, GROUP_WIDTH), F32)]
        + [jax.ShapeDtypeStruct(a.shape, a.dtype) for a in forward],
        in_specs=[cur(0), prev(1), cur(1), prev(2), cur(2)] + [_ANY] * nf,
        out_specs=[cur(0), cur(0)] + [_ANY] * nf,
        input_output_aliases={5 + a: 2 + a for a in range(nf)},
        scratch_shapes=[pltpu.VMEM((2, HEADS_PER_GROUP * q, 2 * q), F32)] + (_dma_sems(3 * nf) if nf else []),
        compiler_params=_params(2))(qkv, qkv, qkv, qkv, qkv, *forward)


def _conv_branch(ab, ac, av, halo_u, w, b):
    u = ac * av
    sh1 = _shift_down(u, 1, halo_u)
    sh2 = _shift_down(u, 2, halo_u)
    cv = w[0:1] * sh2 + w[1:2] * sh1 + w[2:3] * u + b
    return ab * cv, cv, u, sh1, sh2


def _mix_fwd(x, abcv, gates, o_list, lse_list, conv_w, conv_b, b_gate, w_pa, w_pb, w_out):
    s = x.shape[0]
    tm = ROW_TILE

    def body(x_ref, abcv_ref, gates_ref, o0_ref, o1_ref, o2_ref, l0_ref, l1_ref, l2_ref,
             cw_ref, cb_ref, bg_ref, wpa_ref, wpb_ref, wout_ref,
             x1_ref, ya0_ref, yb0_ref, mrg_ref, ya_ref, yb_ref, lsetot_ref, halo_ref, scr):
        i = pl.program_id(0)

        @pl.when(i == 0)
        def _():
            halo_ref[...] = jnp.zeros_like(halo_ref)

        ab = abcv_ref[:, 0:CONV_WIDTH].astype(F32)
        ac = abcv_ref[:, CONV_WIDTH:2 * CONV_WIDTH].astype(F32)
        av = abcv_ref[:, 2 * CONV_WIDTH:3 * CONV_WIDTH].astype(F32)
        ya0, _, u, _, _ = _conv_branch(ab, ac, av, halo_ref[...], cw_ref[...], cb_ref[...])
        halo_ref[...] = u[tm - SUBLANES:tm]
        ya0 = ya0.astype(BF16)
        ya0_ref[...] = ya0
        ya = _dot(ya0, wpa_ref[...])

        o_refs, l_refs = (o0_ref, o1_ref, o2_ref), (l0_ref, l1_ref, l2_ref)
        lses = [_from_streams(l_refs[g], scr, DILATIONS[g], 0, GROUP_WIDTH) for g in range(3)]
        top = jnp.maximum(jnp.maximum(lses[0], lses[1]), lses[2])
        weights = [jnp.exp(lse - top) for lse in lses]
        total = weights[0] + weights[1] + weights[2]
        lsetot_ref[...] = top + jnp.log(total)
        inv_total = 1.0 / total
        yb = jnp.zeros((tm, D_MODEL), F32)
        for g in range(3):
            og = _from_streams(o_refs[g], scr, DILATIONS[g], 0, GROUP_WIDTH)
            yb0 = (weights[g] * inv_total * og).astype(BF16)
            yb0_ref[:, g * GROUP_WIDTH:(g + 1) * GROUP_WIDTH] = yb0
            yb = yb + _dot(yb0, wpb_ref[g * GROUP_WIDTH:(g + 1) * GROUP_WIDTH, :])

        sa = _sigmoid(gates_ref[:, 0:D_MODEL].astype(F32) + bg_ref[0:1, :])
        sb = _sigmoid(gates_ref[:, D_MODEL:2 * D_MODEL].astype(F32) + bg_ref[1:2, :])
        ya_ref[...] = ya.astype(BF16)
        yb_ref[...] = yb.astype(BF16)
        mrg = (sa * ya + sb * yb).astype(BF16)
        mrg_ref[...] = mrg
        x1_ref[...] = x_ref[...] + _dot(mrg, wout_ref[...])

    outs = [jax.ShapeDtypeStruct((s, D_MODEL), F32),
            jax.ShapeDtypeStruct((s, CONV_WIDTH), BF16),
            jax.ShapeDtypeStruct((s, ATTN_WIDTH), BF16),
            jax.ShapeDtypeStruct((s, D_MODEL), BF16),
            jax.ShapeDtypeStruct((s, D_MODEL), BF16),
            jax.ShapeDtypeStruct((s, D_MODEL), BF16),
            jax.ShapeDtypeStruct((s, GROUP_WIDTH), F32)]
    return pl.pallas_call(
        body, name="mix_fwd", grid=(s // tm,), out_shape=outs,
        in_specs=[_rows(tm, D_MODEL), _rows(tm, 3 * CONV_WIDTH), _rows(tm, 2 * D_MODEL)]
        + [_stream_block(tm, d, GROUP_WIDTH) for d in DILATIONS] * 2
        + [_resident((3, CONV_WIDTH)), _resident((1, CONV_WIDTH)), _resident((2, D_MODEL)),
           _resident((CONV_WIDTH, D_MODEL)), _resident((ATTN_WIDTH, D_MODEL)), _resident((D_MODEL, D_MODEL))],
        out_specs=[_rows(tm, D_MODEL), _rows(tm, CONV_WIDTH), _rows(tm, ATTN_WIDTH), _rows(tm, D_MODEL),
                   _rows(tm, D_MODEL), _rows(tm, D_MODEL), _rows(tm, GROUP_WIDTH)],
        scratch_shapes=[pltpu.VMEM((SUBLANES, CONV_WIDTH), F32),
                        pltpu.VMEM((GROUP_WIDTH // LANES * tm, LANES), F32)],
        compiler_params=_params(1))(x, abcv, gates, *o_list, *lse_list, conv_w, conv_b, b_gate, w_pa, w_pb, w_out)


FFN_CHUNK = 256
FFN_UP_ROW_TILE = 256


def _ffn_up_fwd(x1, g, w_up, conv_w, conv_b):
    s = x1.shape[0]
    n = w_up.shape[1]
    tm = FFN_UP_ROW_TILE

    def body(x_ref, g_ref, w_ref, cw_ref, cb_ref, h_ref, up0_ref, up_ref, halo_ref):
        @pl.when(pl.program_id(0) == 0)
        def _():
            halo_ref[...] = jnp.zeros_like(halo_ref)

        xv = x_ref[...]
        r = lax.rsqrt(jnp.mean(xv * xv, axis=-1, keepdims=True) + EPS)
        h = (xv * r * g_ref[...]).astype(BF16)
        h_ref[...] = h
        for lo, size in _col_chunks(n, FFN_CHUNK):
            cols = slice(lo, lo + size)
            y = _dot(h, w_ref[:, cols])
            up0_ref[:, cols] = y.astype(BF16)
            halo = halo_ref[:, cols]
            w = cw_ref[:, cols]
            up = w[0:1] * _shift_down(y, 2, halo) + w[1:2] * _shift_down(y, 1, halo) + w[2:3] * y + cb_ref[:, cols]
            up_ref[:, cols] = up.astype(BF16)
            halo_ref[:, cols] = y[tm - SUBLANES:tm]

    return pl.pallas_call(
        body, name="ffn_up_fwd", grid=(s // tm,),
        out_shape=[jax.ShapeDtypeStruct((s, D_MODEL), BF16), jax.ShapeDtypeStruct((s, n), BF16),
                   jax.ShapeDtypeStruct((s, n), BF16)],
        in_specs=[_rows(tm, D_MODEL), _resident((1, D_MODEL)), _resident((D_MODEL, n)), _resident((3, n)),
                  _resident((1, n))],
        out_specs=[_rows(tm, D_MODEL), _rows(tm, n), _rows(tm, n)],
        scratch_shapes=[pltpu.VMEM((SUBLANES, n), F32)],
        compiler_params=_params(1))(x1, g, w_up, conv_w, conv_b)


def _ffn_act_fwd(x1, up, target, w_down, g_final):
    s = x1.shape[0]
    tm = ROW_TILE

    def body(x1_ref, up_ref, tgt_ref, wd_ref, gf_ref, act_ref, dx2_ref, dx2b_ref, dgf_ref, loss_ref):
        @pl.when(pl.program_id(0) == 0)
        def _():
            dgf_ref[...] = jnp.zeros_like(dgf_ref)
            loss_ref[...] = jnp.zeros_like(loss_ref)

        acc = jnp.zeros((tm, D_MODEL), F32)
        for lo, size in _col_chunks(D_FF, FFN_CHUNK):
            gate = up_ref[:, lo:lo + size].astype(F32)
            val = up_ref[:, D_FF + lo:D_FF + lo + size].astype(F32)
            act = (_silu(gate) * val).astype(BF16)
            act_ref[:, lo:lo + size] = act
            acc = acc + _dot(act, wd_ref[lo:lo + size, :])

        x2 = x1_ref[...] + acc
        r = lax.rsqrt(jnp.mean(x2 * x2, axis=-1, keepdims=True) + EPS)
        xn = x2 * r
        gf = gf_ref[...]
        err = xn * gf - tgt_ref[...]
        loss_ref[...] += (0.5 / D_MODEL) * jnp.sum(err * err)
        dy = err * (1.0 / D_MODEL)
        dgf_ref[...] += _colsum(dy * xn)
        dxn = dy * gf
        dx2 = r * (dxn - xn * jnp.mean(dxn * xn, axis=-1, keepdims=True))
        dx2_ref[...] = dx2
        dx2b_ref[...] = dx2.astype(BF16)

    return pl.pallas_call(
        body, name="ffn_act_fwd", grid=(s // tm,),
        out_shape=[jax.ShapeDtypeStruct((s, D_FF), BF16), jax.ShapeDtypeStruct((s, D_MODEL), F32),
                   jax.ShapeDtypeStruct((s, D_MODEL), BF16),
                   jax.ShapeDtypeStruct((1, D_MODEL), F32), jax.ShapeDtypeStruct((1, LANES), F32)],
        in_specs=[_rows(tm, D_MODEL), _rows(tm, 2 * D_FF), _rows(tm, D_MODEL),
                  _resident((D_FF, D_MODEL)), _resident((1, D_MODEL))],
        out_specs=[_rows(tm, D_FF), _rows(tm, D_MODEL), _rows(tm, D_MODEL),
                   pl.BlockSpec((1, D_MODEL), lambda i: (0, 0)), pl.BlockSpec((1, LANES), lambda i: (0, 0))],
        compiler_params=_params(1))(x1, up, target, w_down, g_final)


def _ffn_act_bwd(dx2b, up, w_down):
    s = dx2b.shape[0]
    tm = ROW_TILE

    def body(dx2_ref, up_ref, wd_ref, dup_ref):
        dx2 = dx2_ref[...]
        for lo, size in _col_chunks(D_FF, FFN_CHUNK):
            gate = up_ref[:, lo:lo + size].astype(F32)
            val = up_ref[:, D_FF + lo:D_FF + lo + size].astype(F32)
            dact = _dot_nt(dx2, wd_ref[lo:lo + size, :])
            sg = _sigmoid(gate)
            dup_ref[:, lo:lo + size] = (dact * val * (sg * (1.0 + gate * (1.0 - sg)))).astype(BF16)
            dup_ref[:, D_FF + lo:D_FF + lo + size] = (dact * (gate * sg)).astype(BF16)

    return pl.pallas_call(
        body, name="ffn_act_bwd", grid=(s // tm,),
        out_shape=jax.ShapeDtypeStruct((s, 2 * D_FF), BF16),
        in_specs=[_rows(tm, D_MODEL), _rows(tm, 2 * D_FF), _resident((D_FF, D_MODEL))],
        out_specs=_rows(tm, 2 * D_FF),
        compiler_params=_params(1))(dx2b, up, w_down)


def _ffn_up_bwd(d_up, up0, w_up, conv_w, x1, g, dres, swap=()):
    s = x1.shape[0]
    n = w_up.shape[1]
    tm = FFN_UP_ROW_TILE
    nt = s // tm
    nw = len(swap)

    def body(*refs):
        dup_ref, up0_ref, w_ref, cw_ref, x_ref, g_ref, dres_ref = refs[:7]
        slab_refs = refs[7:7 + nw]
        dx_ref, dxb_ref, dg_ref, dup0_ref, small_ref = refs[7 + nw:12 + nw]
        swapped_refs = refs[12 + nw:12 + 2 * nw]
        next_ref = refs[12 + 2 * nw]
        sems = refs[13 + 2 * nw:]

        @pl.when(pl.program_id(0) == 0)
        def _():
            next_ref[...] = jnp.zeros_like(next_ref)
            small_ref[...] = jnp.zeros_like(small_ref)
            dg_ref[...] = jnp.zeros_like(dg_ref)
            for cp in _sibling_swap_copies(slab_refs, swapped_refs, *sems) if nw else ():
                cp.start()

        dh = jnp.zeros((tm, D_MODEL), F32)
        for lo, size in _col_chunks(n, FFN_CHUNK):
            cols = slice(lo, lo + size)
            dz = dup_ref[:, cols].astype(F32)
            x0 = up0_ref[:, cols].astype(F32)
            nxt = next_ref[:, cols]
            dz1 = _shift_up(dz, 1, nxt)
            dz2 = _shift_up(dz, 2, nxt)
            next_ref[:, cols] = dz[0:SUBLANES]
            small_ref[0:1, cols] += _colsum(dz2 * x0)
            small_ref[1:2, cols] += _colsum(dz1 * x0)
            small_ref[2:3, cols] += _colsum(dz * x0)
            small_ref[3:4, cols] += _colsum(dz)
            w = cw_ref[:, cols]
            dup0 = (w[2:3] * dz + w[1:2] * dz1 + w[0:1] * dz2).astype(BF16)
            dup0_ref[:, cols] = dup0
            dh = dh + _dot_nt(dup0, w_ref[:, cols])
        xv = x_ref[...]
        r = lax.rsqrt(jnp.mean(xv * xv, axis=-1, keepdims=True) + EPS)
        xn = xv * r
        dg_ref[...] += _colsum(dh * xn)
        dxn = dh * g_ref[...]
        dx = dres_ref[...] + r * (dxn - xn * jnp.mean(dxn * xn, axis=-1, keepdims=True))
        dx_ref[...] = dx
        dxb_ref[...] = dx.astype(BF16)

        if nw:
            @pl.when(pl.program_id(0) == nt - 1)
            def _():
                for cp in _sibling_swap_copies(slab_refs, swapped_refs, *sems):
                    cp.wait()

    rows = lambda width: pl.BlockSpec((tm, width), lambda i: (nt - 1 - i, 0))
    return pl.pallas_call(
        body, name="ffn_up_bwd", grid=(nt,),
        out_shape=[jax.ShapeDtypeStruct((s, D_MODEL), F32), jax.ShapeDtypeStruct((s, D_MODEL), BF16),
                   jax.ShapeDtypeStruct((1, D_MODEL), F32), jax.ShapeDtypeStruct((s, n), BF16),
                   jax.ShapeDtypeStruct((SUBLANES, n), F32)] + _swap_shapes(swap),
        in_specs=[rows(n), rows(n), _resident((D_MODEL, n)), _resident((3, n)), rows(D_MODEL),
                  _resident((1, D_MODEL)), rows(D_MODEL)] + [_ANY] * nw,
        out_specs=[rows(D_MODEL), rows(D_MODEL), pl.BlockSpec((1, D_MODEL), lambda i: (0, 0)), rows(n),
                   pl.BlockSpec((SUBLANES, n), lambda i: (0, 0))] + [_ANY] * nw,
        scratch_shapes=[pltpu.VMEM((SUBLANES, n), F32)] + (_dma_sems(nw) if nw else []),
        compiler_params=_params(1))(d_up, up0, w_up, conv_w, x1, g, dres, *swap)


def _inproj_bwd(d_abcv, d_gates, d_qkvs, w_in_t, x, g, dres):
    s = x.shape[0]
    tm = ROW_TILE
    gw = GROUP_WIDTH

    def body(dabcv_ref, dgates_ref, dq0_ref, dq1_ref, dq2_ref, w_ref, x_ref, g_ref, dres_ref, dx_ref, dg_ref, scr):
        @pl.when(pl.program_id(0) == 0)
        def _():
            dg_ref[...] = jnp.zeros_like(dg_ref)

        dh = jnp.zeros((tm, D_MODEL), F32)
        for src, width, wrow in ((dabcv_ref, 3 * CONV_WIDTH, COL_ABCV), (dgates_ref, 2 * D_MODEL, COL_GATES)):
            for lo, size in _col_chunks(width, 512):
                dh = dh + _dot(src[:, lo:lo + size], w_ref[wrow + lo:wrow + lo + size, :])
        for gi, (d, dq_ref) in enumerate(zip(DILATIONS, (dq0_ref, dq1_ref, dq2_ref))):
            for j, base in enumerate((COL_Q, COL_K, COL_V)):
                dy = _from_streams(dq_ref, scr, d, j * gw, gw).astype(BF16)
                wrow = base + gi * gw
                dh = dh + _dot(dy, w_ref[wrow:wrow + gw, :])
        xv = x_ref[...]
        r = lax.rsqrt(jnp.mean(xv * xv, axis=-1, keepdims=True) + EPS)
        xn = xv * r
        dg_ref[...] += _colsum(dh * xn)
        dxn = dh * g_ref[...]
        dx_ref[...] = dres_ref[...] + r * (dxn - xn * jnp.mean(dxn * xn, axis=-1, keepdims=True))

    return pl.pallas_call(
        body, name="inproj_bwd", grid=(s // tm,),
        out_shape=[jax.ShapeDtypeStruct((s, D_MODEL), F32), jax.ShapeDtypeStruct((1, D_MODEL), F32)],
        in_specs=[_rows(tm, 3 * CONV_WIDTH), _rows(tm, 2 * D_MODEL)]
        + [_stream_block(tm, d, 3 * gw) for d in DILATIONS]
        + [_resident((D_IN, D_MODEL)), _rows(tm, D_MODEL), _resident((1, D_MODEL)), _rows(tm, D_MODEL)],
        out_specs=[_rows(tm, D_MODEL), pl.BlockSpec((1, D_MODEL), lambda i: (0, 0))],
        scratch_shapes=[pltpu.VMEM((gw // LANES * tm, LANES), F32)],
        compiler_params=_params(1))(d_abcv, d_gates, *d_qkvs, w_in_t, x, g, dres)


def _mix_bwd(dx1, abcv, gates, ya, yb, yb0, lsetot, conv_w, conv_b, b_gate, w_pa, w_pb, w_out, exchange=()):
    s = dx1.shape[0]
    tm = ROW_TILE
    nt = s // tm
    hb = tm // (2 * SUBLANES)
    nx = len(exchange)

    def body(*refs):
        (dx1_ref, abcv_ref, pre_ref, gates_ref, ya_ref, yb_ref, yb0_ref, lsetot_ref,
         cw_ref, cb_ref, bg_ref, wpa_ref, wpb_ref, wout_ref) = refs[:14]
        part_refs = refs[14:14 + nx]
        (dya_ref, dyb_ref, dgates_ref, dabcv_ref, dyb0_ref, dyl0_ref, dyl1_ref, dyl2_ref, aux0_ref, aux1_ref,
         aux2_ref, sm_gate_ref, sm_conv_ref) = refs[14 + nx:27 + nx]
        recv_refs = refs[27 + nx:27 + 2 * nx]
        next_ref, scr = refs[27 + 2 * nx:29 + 2 * nx]
        sems = refs[29 + 2 * nx:]
        i = pl.program_id(0)

        @pl.when(i == 0)
        def _():
            next_ref[...] = jnp.zeros_like(next_ref)
            sm_gate_ref[...] = jnp.zeros_like(sm_gate_ref)
            sm_conv_ref[...] = jnp.zeros_like(sm_conv_ref)
            for cp in _chip_exchange_copies(part_refs, recv_refs, *sems) if nx else ():
                cp.start()

        not_first = (i < nt - 1).astype(F32)
        dm = _dot_nt(dx1_ref[...].astype(BF16), wout_ref[...])
        sa = _sigmoid(gates_ref[:, 0:D_MODEL].astype(F32) + bg_ref[0:1, :])
        sb = _sigmoid(gates_ref[:, D_MODEL:2 * D_MODEL].astype(F32) + bg_ref[1:2, :])
        dya = (dm * sa).astype(BF16)
        dyb = (dm * sb).astype(BF16)
        dya_ref[...] = dya
        dyb_ref[...] = dyb
        dga = dm * ya_ref[...].astype(F32) * (sa * (1.0 - sa))
        dgb = dm * yb_ref[...].astype(F32) * (sb * (1.0 - sb))
        dgates_ref[:, 0:D_MODEL] = dga.astype(BF16)
        dgates_ref[:, D_MODEL:2 * D_MODEL] = dgb.astype(BF16)
        sm_gate_ref[0:1, :] += _colsum(dga)
        sm_gate_ref[1:2, :] += _colsum(dgb)

        dya0 = _dot_nt(dya, wpa_ref[...])
        ab = abcv_ref[:, 0:CONV_WIDTH].astype(F32)
        ac = abcv_ref[:, CONV_WIDTH:2 * CONV_WIDTH].astype(F32)
        av = abcv_ref[:, 2 * CONV_WIDTH:3 * CONV_WIDTH].astype(F32)
        pre = pre_ref[...].astype(F32) * not_first
        halo_u = (pre[:, CONV_WIDTH:2 * CONV_WIDTH] * pre[:, 2 * CONV_WIDTH:3 * CONV_WIDTH])[SUBLANES:2 * SUBLANES]
        w = cw_ref[...]
        _, cv, u, sh1, sh2 = _conv_branch(ab, ac, av, halo_u, w, cb_ref[...])
        dcv = dya0 * ab
        sm_conv_ref[0:1, :] += _colsum(dcv * sh2)
        sm_conv_ref[1:2, :] += _colsum(dcv * sh1)
        sm_conv_ref[2:3, :] += _colsum(dcv * u)
        sm_conv_ref[3:4, :] += _colsum(dcv)
        nxt = next_ref[...]
        du = w[2:3] * dcv + w[1:2] * _shift_up(dcv, 1, nxt) + w[0:1] * _shift_up(dcv, 2, nxt)
        next_ref[...] = dcv[0:SUBLANES]
        dabcv_ref[:, 0:CONV_WIDTH] = (dya0 * cv).astype(BF16)
        dabcv_ref[:, CONV_WIDTH:2 * CONV_WIDTH] = (du * av).astype(BF16)
        dabcv_ref[:, 2 * CONV_WIDTH:3 * CONV_WIDTH] = (du * ac).astype(BF16)

        head_r = lax.broadcasted_iota(jnp.int32, (GROUP_WIDTH, GROUP_WIDTH), 0) // HEAD_DIM
        head_c = lax.broadcasted_iota(jnp.int32, (GROUP_WIDTH, GROUP_WIDTH), 1) // HEAD_DIM
        same_head = (head_r == head_c).astype(BF16)
        prod = jnp.zeros((tm, GROUP_WIDTH), F32)
        dyb0s = []
        for g in range(3):
            cols = slice(g * GROUP_WIDTH, (g + 1) * GROUP_WIDTH)
            dyb0 = _dot_nt(dyb, wpb_ref[cols, :])
            dyb0_ref[:, cols] = dyb0.astype(BF16)
            dyb0s.append(dyb0)
            prod = prod + dyb0 * yb0_ref[:, cols].astype(F32)
        hi = prod.astype(BF16)
        mid = (prod - hi.astype(F32)).astype(BF16)
        lo = (prod - hi.astype(F32) - mid.astype(F32)).astype(BF16)
        delta = _dot(hi, same_head) + _dot(mid, same_head) + _dot(lo, same_head)
        lse_c = _compact_heads(lsetot_ref[...])
        delta_c = _compact_heads(delta)
        for g, (dy_ref, aux_ref) in enumerate(zip((dyl0_ref, dyl1_ref, dyl2_ref), (aux0_ref, aux1_ref, aux2_ref))):
            d = DILATIONS[g]
            _to_streams(dyb0s[g], scr, dy_ref, d, 0)
            _to_streams(lse_c, scr, aux_ref, d, 0)
            _to_streams(delta_c, scr, aux_ref, d, LANES)

        if nx:
            @pl.when(i == nt - 1)
            def _():
                for cp in _chip_exchange_copies(part_refs, recv_refs, *sems):
                    cp.wait()

    rev = lambda i: (nt - 1 - i, 0)
    pre = lambda i: (jnp.maximum((nt - 1 - i) * hb - 1, 0), 0)
    rows = lambda width: pl.BlockSpec((tm, width), rev)
    outs = [jax.ShapeDtypeStruct((s, D_MODEL), BF16), jax.ShapeDtypeStruct((s, D_MODEL), BF16),
            jax.ShapeDtypeStruct((s, 2 * D_MODEL), BF16), jax.ShapeDtypeStruct((s, 3 * CONV_WIDTH), BF16),
            jax.ShapeDtypeStruct((s, ATTN_WIDTH), BF16)]
    outs += [jax.ShapeDtypeStruct((d, s // d, GROUP_WIDTH), BF16) for d in DILATIONS]
    outs += [jax.ShapeDtypeStruct((d, s // d, 2 * LANES), F32) for d in DILATIONS]
    outs += [jax.ShapeDtypeStruct((SUBLANES, D_MODEL), F32), jax.ShapeDtypeStruct((SUBLANES, CONV_WIDTH), F32)]
    outs += [jax.ShapeDtypeStruct((3,) + a.shape[1:], a.dtype) for a in exchange]
    return pl.pallas_call(
        body, name="mix_bwd", grid=(nt,), out_shape=outs,
        in_specs=[rows(D_MODEL), rows(3 * CONV_WIDTH), pl.BlockSpec((2 * SUBLANES, 3 * CONV_WIDTH), pre),
                  rows(2 * D_MODEL), rows(D_MODEL), rows(D_MODEL), rows(ATTN_WIDTH), rows(GROUP_WIDTH),
                  _resident((3, CONV_WIDTH)), _resident((1, CONV_WIDTH)), _resident((2, D_MODEL)),
                  _resident((CONV_WIDTH, D_MODEL)), _resident((ATTN_WIDTH, D_MODEL)), _resident((D_MODEL, D_MODEL))]
        + [_ANY] * nx,
        out_specs=[rows(D_MODEL), rows(D_MODEL), rows(2 * D_MODEL), rows(3 * CONV_WIDTH), rows(ATTN_WIDTH)]
        + [_rev_stream_block(tm, d, GROUP_WIDTH, nt) for d in DILATIONS]
        + [_rev_stream_block(tm, d, 2 * LANES, nt) for d in DILATIONS]
        + [pl.BlockSpec((SUBLANES, D_MODEL), lambda i: (0, 0)), pl.BlockSpec((SUBLANES, CONV_WIDTH), lambda i: (0, 0))]
        + [_ANY] * nx,
        scratch_shapes=[pltpu.VMEM((SUBLANES, CONV_WIDTH), F32),
                        pltpu.VMEM((GROUP_WIDTH // LANES * tm, LANES), F32)] + (_dma_sems(3 * nx) if nx else []),
        compiler_params=_params(1))(dx1, abcv, abcv, gates, ya, yb, yb0, lsetot,
                                    conv_w, conv_b, b_gate, w_pa, w_pb, w_out, *exchange)


def _attn_bwd(qkv, dy, aux, gi, exchange=(), swap=()):
    d, length, _ = qkv.shape
    nb = length // ATTN_BLOCK
    q = ATTN_BLOCK
    gw = GROUP_WIDTH
    g = nb if nb <= ATTN_FWD_BLOCKS else ATTN_BWD_BLOCKS
    assert g >= 2 and nb % g == 0
    ns = nb // g
    lag = 1 if ns > 1 else 0
    tail = (g - 1) * q
    nx, nw = len(exchange), len(swap)

    def body(*refs):
        q_ref, kp_ref, kc_ref, vp_ref, vc_ref, dy_ref, aux_ref = refs[:7]
        part_refs = refs[7:7 + nx]
        slab_refs = refs[7 + nx:7 + nx + nw]
        pos = 7 + nx + nw
        out_ref = refs[pos]
        recv_refs = refs[pos + 1:pos + 1 + nx]
        swapped_refs = refs[pos + 1 + nx:pos + 1 + nx + nw]
        pos += 1 + nx + nw
        dq_ref, dkv_ref, bias_ref = refs[pos:pos + 3]
        sems = refs[pos + 3:]
        n = pl.program_id(1)

        def copies():
            cps = _chip_exchange_copies(part_refs, recv_refs, sems[0], sems[1]) if nx else []
            return cps + (_sibling_swap_copies(slab_refs, swapped_refs, sems[-2], sems[-1]) if nw else [])

        @pl.when((pl.program_id(0) == 0) & (n == 0))
        def _():
            _store_band_biases(bias_ref)
            for cp in copies():
                cp.start()

        if nx or nw:
            @pl.when((pl.program_id(0) == d - 1) & (n == ns - 1 + lag))
            def _():
                for cp in copies():
                    cp.wait()

        def emit(rows):
            out_ref[rows, gw:2 * gw] = dkv_ref[0, rows].astype(BF16)
            out_ref[rows, 2 * gw:3 * gw] = dkv_ref[1, rows].astype(BF16)

        if lag:
            @pl.when(n > 0)
            def _():
                out_ref[:, 0:gw] = dq_ref[...].astype(BF16)
                emit(slice(0, tail))

            @pl.when(n == ns)
            def _():
                emit(slice(tail, g * q))

        @pl.when(n < ns)
        def _():
            kfull = jnp.concatenate([kp_ref[...], kc_ref[...]], axis=0)
            vfull = jnp.concatenate([vp_ref[...], vc_ref[...]], axis=0)
            for j in range(g):
                rows = slice(j * q, (j + 1) * q)
                qs = _stack_heads(q_ref[rows, :])
                dys = _stack_heads(dy_ref[rows, :])
                k2 = kfull[j * q:(j + 2) * q]
                v2 = vfull[j * q:(j + 2) * q]
                lse = _compact_head_col(aux_ref[rows, 0:LANES])
                delta = _compact_head_col(aux_ref[rows, LANES:2 * LANES])
                bias = jnp.where(n == 0, bias_ref[1], bias_ref[0]) if j == 0 else bias_ref[0]
                p = jnp.exp(_dot_nt(qs, k2) + bias - lse)
                dp = _dot_nt(dys, v2)
                ds = (p * (dp - delta)).astype(BF16)
                dq_j = _unstack_heads(_dot(ds, k2)) * ATTN_SCALE
                dk2 = _dot_tn(ds, qs)
                dv2 = _dot_tn(p.astype(BF16), dys)
                if j == 0:
                    @pl.when(n > 0)
                    def _():
                        out_ref[tail:g * q, gw:2 * gw] = (dkv_ref[0, tail:g * q] + dk2[0:q]).astype(BF16)
                        out_ref[tail:g * q, 2 * gw:3 * gw] = (dkv_ref[1, tail:g * q] + dv2[0:q]).astype(BF16)
                else:
                    dkv_ref[0, (j - 1) * q:j * q] += dk2[0:q]
                    dkv_ref[1, (j - 1) * q:j * q] += dv2[0:q]
                dkv_ref[0, rows] = dk2[q:2 * q]
                dkv_ref[1, rows] = dv2[q:2 * q]
                dq_ref[rows, :] = dq_j
            if not lag:
                out_ref[:, 0:gw] = dq_ref[...].astype(BF16)
                emit(slice(0, g * q))

    cur, prev = _attn_block_specs(g, nb, clamp_last=True)
    return pl.pallas_call(
        body, name=f"attn_bwd_g{gi}", grid=(d, ns + lag),
        out_shape=[jax.ShapeDtypeStruct((d, length, 3 * gw), BF16)]
        + [jax.ShapeDtypeStruct((3,) + a.shape[1:], a.dtype) for a in exchange] + _swap_shapes(swap),
        in_specs=[cur(0), prev(1), cur(1), prev(2), cur(2), cur(0), cur(0, 2 * LANES)] + [_ANY] * (nx + nw),
        out_specs=[pl.BlockSpec((None, g * q, 3 * gw), lambda r, n: (r, jnp.maximum(n - lag, 0), 0))]
        + [_ANY] * (nx + nw),
        scratch_shapes=[pltpu.VMEM((g * q, gw), F32), pltpu.VMEM((2, g * q, gw), F32),
                        pltpu.VMEM((2, HEADS_PER_GROUP * q, 2 * q), F32)]
        + (_dma_sems(3 * nx) if nx else []) + (_dma_sems(nw) if nw else []),
        compiler_params=_params(2))(qkv, qkv, qkv, qkv, qkv, dy, aux, *exchange, *swap)


def _matmul_tn(name, a, b, col_tile=1024, row_tile=2048, slabs=0, swap=()):
    s, k = a.shape
    n = b.shape[1]
    tk = min(row_tile, s)
    tn = col_tile
    steps = s // tk
    nw = len(swap)

    def body(*refs):
        a_ref, b_ref = refs[:2]
        slab_refs = refs[2:2 + nw]
        o_ref = refs[2 + nw]
        swapped_refs = refs[3 + nw:3 + 2 * nw]
        acc_ref = refs[3 + 2 * nw]
        sems = refs[4 + 2 * nw:]
        t = pl.program_id(1)

        if nw:
            @pl.when((pl.program_id(0) == 0) & (t == 0))
            def _():
                for cp in _sibling_swap_copies(slab_refs, swapped_refs, *sems):
                    cp.start()

            @pl.when((pl.program_id(0) == n // tn - 1) & (t == steps - 1))
            def _():
                for cp in _sibling_swap_copies(slab_refs, swapped_refs, *sems):
                    cp.wait()

        @pl.when(t == 0)
        def _():
            acc_ref[...] = jnp.zeros_like(acc_ref)

        acc_ref[...] += _dot_tn(a_ref[...], b_ref[...])

        @pl.when(t == steps - 1)
        def _():
            if slabs:
                for q in range(per_tile):
                    o_ref[q] = acc_ref[:, q * width:(q + 1) * width].astype(BF16)
            else:
                o_ref[...] = acc_ref[...].astype(BF16)

    if slabs:
        width = n // slabs
        per_tile = tn // width
        out_shape = jax.ShapeDtypeStruct((slabs, k, width), BF16)
        out_spec = pl.BlockSpec((per_tile, k, width), lambda j, t: (j, 0, 0))
    else:
        out_shape = jax.ShapeDtypeStruct((k, n), BF16)
        out_spec = pl.BlockSpec((k, tn), lambda j, t: (0, j))
    res = pl.pallas_call(
        body, name=name, grid=(n // tn, steps), out_shape=[out_shape] + _swap_shapes(swap),
        in_specs=[pl.BlockSpec((tk, k), lambda j, t: (t, 0)), pl.BlockSpec((tk, tn), lambda j, t: (t, j))] + [_ANY] * nw,
        out_specs=[out_spec] + [_ANY] * nw,
        scratch_shapes=[pltpu.VMEM((k, tn), F32)] + (_dma_sems(nw) if nw else []),
        compiler_params=_params(2))(a, b, *swap)
    return res if nw else res[0]


def _sibling_swap_halves(name, slabs):
    na = len(slabs)

    def body(*refs):
        src_refs, out_refs = refs[:na], refs[na:2 * na]
        send_sems, recv_sems = refs[2 * na:]
        x, y, c, _ = _mesh_position()
        cps = []
        for a in range(na):
            theirs = _half_rows(1 - c, src_refs[a].shape[1] // 2)
            cps.append(pltpu.make_async_remote_copy(
                src_ref=src_refs[a].at[:, theirs, :], dst_ref=out_refs[a], send_sem=send_sems.at[a],
                recv_sem=recv_sems.at[a], device_id=(x, y, 1 - c), device_id_type=MESH_ID))
        for cp in cps:
            cp.start()
        for cp in cps:
            cp.wait()

    return pl.pallas_call(
        body, name=name,
        out_shape=[jax.ShapeDtypeStruct((a.shape[0], a.shape[1] // 2, a.shape[2]), a.dtype) for a in slabs],
        in_specs=[_ANY] * na, out_specs=[_ANY] * na,
        scratch_shapes=[pltpu.SemaphoreType.DMA((na,)), pltpu.SemaphoreType.DMA((na,))])(*slabs)


_HBM = pl.BlockSpec(memory_space=pltpu.HBM)
_SEM = pl.BlockSpec(memory_space=pltpu.SEMAPHORE)
_DATAFLOW = pltpu.SideEffectType.DATAFLOW_SIDE_EFFECTING


def _gather_start(shard):
    gathered = jax.ShapeDtypeStruct((N_CHIPS,) + shard.shape[1:], shard.dtype)

    def body(src_ref, buf_ref, send_sems, recv_sems, src_thru, buf_thru, token):
        for cp in _gather_first_copies([src_ref], [], [buf_ref], [], send_sems, recv_sems):
            cp.start()
        token[...] = jnp.zeros_like(token)

    return pl.pallas_call(
        body, name="gather_start",
        out_shape=(pltpu.SemaphoreType.DMA((3,)), pltpu.SemaphoreType.DMA((3,)),
                   pltpu.HBM(shard.shape, shard.dtype), pltpu.HBM(gathered.shape, gathered.dtype),
                   jax.ShapeDtypeStruct((SUBLANES, LANES), F32)),
        in_specs=(_HBM, _HBM), out_specs=(_SEM, _SEM, _HBM, _HBM, _VMEM), input_output_aliases={0: 2, 1: 3},
        compiler_params=pltpu.CompilerParams(has_side_effects=_DATAFLOW),
    )(pltpu.with_memory_space_constraint(shard, pltpu.HBM),
      pltpu.with_memory_space_constraint(lax.empty(gathered.shape, gathered.dtype), pltpu.HBM))


def _gather_forward(send_sems, recv_sems, shard_thru, buf_thru, after):
    def body(src_ref, buf_ref, send_sems, recv_sems, after_ref, fwd_send, fwd_recv, buf_out):
        first = _gather_first_copies([src_ref], [], [buf_ref], [], send_sems, recv_sems)
        for cp, fwd in zip(first, _gather_forward_copies([buf_ref], fwd_send, fwd_recv)):
            cp.wait_send()
            cp.wait_recv()
            fwd.start()

    return pl.pallas_call(
        body, name="gather_forward",
        out_shape=(pltpu.SemaphoreType.DMA((3,)), pltpu.SemaphoreType.DMA((3,)),
                   pltpu.HBM(buf_thru.shape, buf_thru.dtype)),
        in_specs=(_HBM, _HBM, _SEM, _SEM, _ANY), out_specs=(_SEM, _SEM, _HBM), input_output_aliases={1: 2},
        compiler_params=pltpu.CompilerParams(has_side_effects=_DATAFLOW),
    )(shard_thru, buf_thru, send_sems, recv_sems, after)


def _gather_wait(fwd_send, fwd_recv, buf_thru):
    def body(buf_ref, fwd_send, fwd_recv, buf_out):
        for cp in _gather_forward_copies([buf_ref], fwd_send, fwd_recv):
            cp.wait_send()
            cp.wait_recv()

    return pl.pallas_call(
        body, name="gather_wait", out_shape=pltpu.HBM(buf_thru.shape, buf_thru.dtype),
        in_specs=(_HBM, _SEM, _SEM), out_specs=_HBM, input_output_aliases={0: 0},
        compiler_params=pltpu.CompilerParams(has_side_effects=_DATAFLOW),
    )(buf_thru, fwd_send, fwd_recv)


def _chip_exchange_start(partial):
    _, rows, cols = partial.shape
    landing = jax.ShapeDtypeStruct((3, rows, cols), partial.dtype)

    def body(src_ref, land_ref, send_sems, recv_sems, src_thru, land_thru, token):
        for cp in _chip_exchange_copies([src_ref], [land_ref], send_sems, recv_sems):
            cp.start()
        token[...] = jnp.zeros_like(token)

    return pl.pallas_call(
        body, name="grad_exchange_start",
        out_shape=(pltpu.SemaphoreType.DMA((3,)), pltpu.SemaphoreType.DMA((3,)),
                   pltpu.HBM(partial.shape, partial.dtype), pltpu.HBM(landing.shape, landing.dtype),
                   jax.ShapeDtypeStruct((SUBLANES, LANES), F32)),
        in_specs=(_HBM, _HBM), out_specs=(_SEM, _SEM, _HBM, _HBM, _VMEM), input_output_aliases={0: 2, 1: 3},
        compiler_params=pltpu.CompilerParams(has_side_effects=_DATAFLOW),
    )(pltpu.with_memory_space_constraint(partial, pltpu.HBM),
      pltpu.with_memory_space_constraint(lax.empty(landing.shape, landing.dtype), pltpu.HBM))


def _chip_exchange_wait(send_sems, recv_sems, src_thru, land_thru, after):
    def body(src_ref, land_ref, send_sems, recv_sems, after_ref, src_out, land_out):
        for cp in _chip_exchange_copies([src_ref], [land_ref], send_sems, recv_sems):
            cp.wait_send()
            cp.wait_recv()

    return pl.pallas_call(
        body, name="grad_exchange_wait",
        out_shape=(pltpu.HBM(src_thru.shape, src_thru.dtype), pltpu.HBM(land_thru.shape, land_thru.dtype)),
        in_specs=(_HBM, _HBM, _SEM, _SEM, _ANY), out_specs=(_HBM, _HBM), input_output_aliases={0: 0, 1: 1},
        compiler_params=pltpu.CompilerParams(has_side_effects=_DATAFLOW),
    )(src_thru, land_thru, send_sems, recv_sems, after)


def _sibling_share_copies(refs, send_sems, recv_sems):
    x, y, c, _ = _mesh_position()
    cps = []
    for a, ref in enumerate(refs):
        mine = ref.at[0, _half_rows(c, ref.shape[1] // 2)]
        cps.append(_remote_copy(a, mine, mine, (x, y, 1 - c), send_sems, recv_sems))
    return cps


def _add_sibling(name, slabs, received, core):
    na = len(slabs)
    halves = [a.shape[1] // 2 for a in slabs]

    def body(core_ref, *refs):
        for a in range(na):
            refs[2 * na + a][...] = (refs[a][...].astype(F32) + refs[na + a][...].astype(F32)).astype(BF16)

    def block(a, mine):
        if mine:
            return pl.BlockSpec((None, halves[a], slabs[a].shape[2]), lambda s, core_ref: (s, core_ref[0], 0))
        return pl.BlockSpec((None, halves[a], slabs[a].shape[2]), lambda s, core_ref: (s, 0, 0))

    grid_spec = pltpu.PrefetchScalarGridSpec(
        num_scalar_prefetch=1, grid=(N_CHIPS,),
        in_specs=[block(a, True) for a in range(na)] + [block(a, False) for a in range(na)],
        out_specs=[block(a, False) for a in range(na)])
    return pl.pallas_call(body, name=name, grid_spec=grid_spec,
                          out_shape=[jax.ShapeDtypeStruct(r.shape, BF16) for r in received],
                          compiler_params=_params(1))(core, *slabs, *received)


def _sum_chips(name, partials, received, chip_core):
    na = len(partials)

    def body(cc_ref, *refs):
        for a in range(na):
            acc = refs[a][...].astype(F32)
            for k in range(3):
                acc = acc + refs[na + a][k].astype(F32)
            refs[2 * na + a][...] = acc

    def own(p):
        return pl.BlockSpec((None,) + p.shape[1:], lambda i, cc_ref: (cc_ref[0], 0, 0))

    def mine(p):
        return pl.BlockSpec((None,) + p.shape[1:], lambda i, cc_ref: (0, cc_ref[1], 0))

    grid_spec = pltpu.PrefetchScalarGridSpec(
        num_scalar_prefetch=1, grid=(1,),
        in_specs=[own(p) for p in partials] + [pl.BlockSpec(r.shape, lambda i, cc_ref: (0, 0, 0)) for r in received],
        out_specs=[mine(p) for p in partials])
    return pl.pallas_call(body, name=name, grid_spec=grid_spec,
                          out_shape=[jax.ShapeDtypeStruct((1, 2 * p.shape[1], p.shape[2]), F32) for p in partials],
                          compiler_params=_params(1))(chip_core, *partials, *received)


def _adam_math(w, g, m, v):
    nm = ADAM_B1 * m + (1.0 - ADAM_B1) * g
    nv = ADAM_B2 * v + (1.0 - ADAM_B2) * jnp.square(g)
    m_hat = nm / (1.0 - ADAM_B1 ** ADAM_STEP)
    v_hat = nv / (1.0 - ADAM_B2 ** ADAM_STEP)
    delta = -ADAM_LR * (m_hat / (jnp.sqrt(v_hat) + ADAM_EPS) + ADAM_WD * w)
    return delta, nm, nv


ADAMW_STEPS = 8


def _adamw(ws, gs, ms, vs):
    na = len(ws)

    def body(*refs):
        for a in range(na):
            w_ref, g_ref, m_ref, v_ref = (refs[k * na + a] for k in range(4))
            g_out_ref, d_ref, nm_ref, nv_ref = (refs[(4 + k) * na + a] for k in range(4))
            gv = g_ref[...]
            g_out_ref[...] = gv
            d_ref[...], nm_ref[...], nv_ref[...] = _adam_math(w_ref[...], gv, m_ref[...], v_ref[...])

    specs = [pl.BlockSpec((None, w.shape[1] // ADAMW_STEPS, w.shape[2]), lambda i: (0, i, 0)) for w in ws]
    outs = pl.pallas_call(
        body, name="adamw", grid=(ADAMW_STEPS,), out_shape=[jax.ShapeDtypeStruct(w.shape, F32) for w in ws] * 4,
        in_specs=specs * 4, out_specs=specs * 4, compiler_params=_params(1))(*ws, *gs, *ms, *vs)
    return [[outs[k * na + a] for k in range(4)] for a in range(na)]


SMALL_PARAMS = ("norm_mix_g", "b_gate", "conv_a_w", "conv_a_b", "norm_ffn_g", "ffn_conv_w", "ffn_conv_b", "final_norm_g")


def _small_update(partials, params, moments_m, moments_v, halves):
    na = len(partials)
    npar = len(SMALL_PARAMS)
    nh = len(halves)

    def body(*refs):
        in_refs = refs[:na]
        w_refs = refs[na:na + npar]
        m_refs = refs[na + npar:na + 2 * npar]
        v_refs = refs[na + 2 * npar:na + 3 * npar]
        pos = na + 3 * npar + nh
        loss_ref = refs[pos]
        out_refs = refs[pos + 1:pos + 1 + 4 * npar]
        big_refs = refs[pos + 1 + 4 * npar:pos + 1 + 4 * npar + nh]
        pos += 1 + 4 * npar + nh
        acc_refs = refs[pos:pos + na]
        recv_refs = refs[pos + na:pos + 4 * na]
        send_sems, recv_sems, share_send, share_recv = refs[pos + 4 * na:]
        x, y, c, _ = _mesh_position()
        chip = 2 * x + y
        for cp in _sibling_share_copies(big_refs, share_send, share_recv):
            cp.start()
        for a in range(na):
            acc_refs[a][...] = in_refs[a][...]
        for stage, peer in enumerate(((x, y, 1 - c), (x, 1 - y, c), (1 - x, y, c))):
            cps = []
            for a in range(na):
                k = stage * na + a
                cps.append(pltpu.make_async_remote_copy(src_ref=acc_refs[a], dst_ref=recv_refs[k], send_sem=send_sems.at[k],
                                                        recv_sem=recv_sems.at[k], device_id=peer, device_id_type=MESH_ID))
            for cp in cps:
                cp.start()
            for cp in cps:
                cp.wait()
            for a in range(na):
                acc_refs[a][...] = acc_refs[a][...] + recv_refs[stage * na + a][...]

        mix, ffn, fin, gate, conv, ffnc, loss = acc_refs
        loss_ref[...] = loss[...]

        def cols(width):
            return pl.ds(pl.multiple_of(chip * width, LANES), width)

        grads = {
            "norm_mix_g": mix[...], "norm_ffn_g": ffn[...], "final_norm_g": fin[...],
            "b_gate": gate[0:2, cols(D_MODEL // N_CHIPS)],
            "conv_a_w": conv[0:3, cols(CONV_WIDTH // N_CHIPS)], "conv_a_b": conv[3:4, :],
            "ffn_conv_w": ffnc[0:3, cols(2 * D_FF // N_CHIPS)], "ffn_conv_b": ffnc[3:4, :]}
        for i, name in enumerate(SMALL_PARAMS):
            g = grads[name]
            if len(w_refs[i].shape) == 3:
                results = (g,) + _adam_math(w_refs[i][0], g, m_refs[i][0], v_refs[i][0])
                for o_ref, val in zip(out_refs[4 * i:4 * i + 4], results):
                    o_ref[0] = val
            else:
                results = (g,) + _adam_math(w_refs[i][...], g, m_refs[i][...], v_refs[i][...])
                for o_ref, val in zip(out_refs[4 * i:4 * i + 4], results):
                    o_ref[...] = val

        for cp in _sibling_share_copies(big_refs, share_send, share_recv):
            cp.wait()

    outs = [jax.ShapeDtypeStruct(partials[-1].shape, F32)]
    for w in params:
        outs += [jax.ShapeDtypeStruct(w.shape, F32)] * 4
    n_small_out = len(outs)
    outs += [jax.ShapeDtypeStruct(h.shape, h.dtype) for h in halves]
    scratch = [pltpu.VMEM(p.shape, F32) for p in partials]
    scratch += [pltpu.VMEM(p.shape, F32) for _ in range(3) for p in partials]
    scratch += _dma_sems(3 * na) + _dma_sems(nh)
    n_in = na + 3 * npar
    return pl.pallas_call(
        body, name="small_update", out_shape=outs, in_specs=[_VMEM] * n_in + [_ANY] * nh,
        out_specs=[_VMEM] * n_small_out + [_ANY] * nh,
        input_output_aliases={n_in + a: n_small_out + a for a in range(nh)},
        scratch_shapes=scratch)(*partials, *params, *moments_m, *moments_v, *halves)


def _gathered_columns(g):
    return jnp.transpose(g, (1, 0, 2)).reshape(g.shape[1], N_CHIPS * g.shape[2])


def kernel(x, norm_mix_g, w_in, b_gate, conv_a_w, conv_a_b, w_proj_a, w_proj_b, w_out, norm_ffn_g, w_up, ffn_conv_w, ffn_conv_b, w_down, final_norm_g, loss_target, m_norm_mix_g, m_w_in, m_b_gate, m_conv_a_w, m_conv_a_b, m_w_proj_a, m_w_proj_b, m_w_out, m_norm_ffn_g, m_w_up, m_ffn_conv_w, m_ffn_conv_b, m_w_down, m_final_norm_g, v_norm_mix_g, v_w_in, v_b_gate, v_conv_a_w, v_conv_a_b, v_w_proj_a, v_w_proj_b, v_w_out, v_norm_ffn_g, v_w_up, v_ffn_conv_w, v_ffn_conv_b, v_w_down, v_final_norm_g):
    chip = (2 * lax.axis_index("x") + lax.axis_index("y")).astype(jnp.int32)
    core = lax.axis_index("c").astype(jnp.int32)
    core_arr = core.reshape(1)
    chip_core = jnp.stack([chip, core])
    xs, target = x[0], loss_target[0]
    g_final = final_norm_g.reshape(1, D_MODEL)

    def own_slot(gathered, own):
        return lax.dynamic_update_slice(gathered, own, (chip, 0, 0))

    w_in_t, m_w_in_t, v_w_in_t = (jnp.swapaxes(a, 1, 2) for a in (w_in, m_w_in, v_w_in))
    w_in_tb = w_in_t.astype(BF16)
    send1, recv1, shard_thru, g_in, token = _gather_start(w_in_tb)
    h1, h1_streams4, h1_streams16 = _norm_fwd(xs, norm_mix_g + token[0:1, 0:1])
    send2, recv2, g_in = _gather_forward(send1, recv1, shard_thru, g_in, h1)
    g_in = _gather_wait(send2, recv2, g_in)
    w_in_full_t = own_slot(g_in, w_in_tb).reshape(D_IN, D_MODEL)
    later_w = [w_proj_a, w_proj_b, w_out, w_up, w_down]
    later_b = [w.astype(BF16) for w in later_w]
    small_sharded = [b_gate, conv_a_w, ffn_conv_w]
    fwd = _inproj_fwd(h1, w_in_full_t, later_b, small_sharded)
    abcv, gates, qkv0, qkv1, qkv2 = fwd[:5]
    gathered_big, gathered_small = fwd[5:10], fwd[10:13]
    attn0 = _attn_fwd(qkv0, 0, forward=gathered_big)
    attn = [attn0[:2], _attn_fwd(qkv1, 1), _attn_fwd(qkv2, 2)]
    g_pa, g_pb, g_out, g_up, g_down = [own_slot(g, own) for g, own in zip(attn0[2:], later_b)]
    g_bgate, g_convw, g_ffnw = [own_slot(g, own) for g, own in zip(gathered_small, small_sharded)]
    w_pa_full, w_pb_full, w_up_full = _gathered_columns(g_pa), _gathered_columns(g_pb), _gathered_columns(g_up)
    w_out_full, w_down_full = g_out.reshape(D_MODEL, D_MODEL), g_down.reshape(D_FF, D_MODEL)
    b_gate_full, conv_w_full, ffn_w_full = (_gathered_columns(g) for g in (g_bgate, g_convw, g_ffnw))

    x1, ya0, yb0, mrg, ya, yb, lsetot = _mix_fwd(
        xs, abcv, gates, [a[0] for a in attn], [a[1] for a in attn], conv_w_full, conv_a_b, b_gate_full,
        w_pa_full, w_pb_full, w_out_full)
    h2, up0, up = _ffn_up_fwd(x1, norm_ffn_g, w_up_full, ffn_w_full, ffn_conv_b)
    act, dx2, dx2b, d_g_final, loss = _ffn_act_fwd(x1, up, target, w_down_full, g_final)

    d_up = _ffn_act_bwd(dx2b, up, w_down_full)
    slab_down = _matmul_tn("dw_down", act, dx2b, col_tile=512).reshape(N_CHIPS, D_FF // N_CHIPS, D_MODEL)
    dx1, dx1b, d_g_ffn, d_up0, ffn_small, swapped_down = _ffn_up_bwd(
        d_up, up0, w_up_full, ffn_w_full, x1, norm_ffn_g, dx2, swap=[slab_down])
    (partial_down,) = _add_sibling("grad_add_w_down", [slab_down], [swapped_down], core_arr)
    slab_up = _matmul_tn("dw_up", h2, d_up0, col_tile=2 * D_FF // N_CHIPS, slabs=N_CHIPS)
    d_w_out, swapped_up = _matmul_tn("dw_out", mrg, dx1b, swap=[slab_up])
    (partial_up,) = _add_sibling("grad_add_w_up", [slab_up], [swapped_up], core_arr)

    mix_res = _mix_bwd(dx1, abcv, gates, ya, yb, yb0, lsetot, conv_w_full, conv_a_b, b_gate_full,
                       w_pa_full, w_pb_full, w_out_full, exchange=[partial_up, partial_down])
    (d_ya, d_yb, d_gates, d_abcv, d_yb0, dyl0, dyl1, dyl2, aux0, aux1, aux2, gate_small, conv_small) = mix_res[:13]
    halves_ffn = _sum_chips("grad_sum_ffn", [partial_up, partial_down], mix_res[13:], chip_core)

    slabs_mix = [_matmul_tn("dw_proj_a", ya0, d_ya, slabs=N_CHIPS), _matmul_tn("dw_proj_b", yb0, d_yb, slabs=N_CHIPS),
                 d_w_out.reshape(N_CHIPS, D_MODEL // N_CHIPS, D_MODEL)]
    res0 = _attn_bwd(qkv0, dyl0, aux0, 0, swap=slabs_mix)
    d_qkv0, partials_mix = res0[0], _add_sibling("grad_add_mix", slabs_mix, res0[1:], core_arr)
    res1 = _attn_bwd(qkv1, dyl1, aux1, 1, exchange=partials_mix)
    d_qkv1, halves_mix = res1[0], _sum_chips("grad_sum_mix", partials_mix, res1[1:], chip_core)
    (d_qkv2,) = _attn_bwd(qkv2, dyl2, aux2, 2)

    dq = [d_qkv0, d_qkv1, d_qkv2]
    seq = xs.shape[0]
    d_w_abcv = _matmul_tn("dw_in_abcv", d_abcv, h1)
    d_w_gates = _matmul_tn("dw_in_gates", d_gates, h1)
    d_w_groups = [_matmul_tn(f"dw_in_qkv{g}", t.reshape(seq, 3 * GROUP_WIDTH), h.reshape(seq, D_MODEL))
                  for g, (t, h) in enumerate(zip(dq, (h1, h1_streams4, h1_streams16)))]
    gw = GROUP_WIDTH
    d_w_in_t = jnp.concatenate(
        [d_w_abcv] + [d_w_groups[g][j * gw:(j + 1) * gw] for j in range(3) for g in range(3)] + [d_w_gates], axis=0)

    slab_in = d_w_in_t.reshape(N_CHIPS, D_IN // N_CHIPS, D_MODEL)
    (from_sibling_in,) = _sibling_swap_halves("grad_swap_w_in", [slab_in])
    (partial_in,) = _add_sibling("grad_add_w_in", [slab_in], [from_sibling_in], core_arr)
    send_sems, recv_sems, partial_thru, landing_thru, token = _chip_exchange_start(partial_in)
    g_mix_after_start = norm_mix_g + token[0:1, 0:1]
    grad_x, d_g_mix = _inproj_bwd(d_abcv, d_gates, dq, w_in_full_t, xs, g_mix_after_start, dx1)
    partial_in, received_in = _chip_exchange_wait(send_sems, recv_sems, partial_thru, landing_thru, d_g_mix)
    halves_in = _sum_chips("grad_sum_w_in", [partial_in], [received_in], chip_core)

    big_names = ("w_in", "w_proj_a", "w_proj_b", "w_out", "w_up", "w_down")
    big_w = dict(w_in=w_in_t, w_proj_a=w_proj_a, w_proj_b=w_proj_b, w_out=w_out, w_up=w_up, w_down=w_down)
    big_m = dict(w_in=m_w_in_t, w_proj_a=m_w_proj_a, w_proj_b=m_w_proj_b, w_out=m_w_out, w_up=m_w_up, w_down=m_w_down)
    big_v = dict(w_in=v_w_in_t, w_proj_a=v_w_proj_a, w_proj_b=v_w_proj_b, w_out=v_w_out, w_up=v_w_up, w_down=v_w_down)

    fin_w, fin_m, fin_v = (a.reshape(1, D_MODEL) for a in (final_norm_g, m_final_norm_g, v_final_norm_g))
    small_w = [norm_mix_g, b_gate, conv_a_w, conv_a_b, norm_ffn_g, ffn_conv_w, ffn_conv_b, fin_w]
    small_m = [m_norm_mix_g, m_b_gate, m_conv_a_w, m_conv_a_b, m_norm_ffn_g, m_ffn_conv_w, m_ffn_conv_b, fin_m]
    small_v = [v_norm_mix_g, v_b_gate, v_conv_a_w, v_conv_a_b, v_norm_ffn_g, v_ffn_conv_w, v_ffn_conv_b, fin_v]
    small_out = _small_update([d_g_mix, d_g_ffn, d_g_final, gate_small, conv_small, ffn_small, loss],
                              small_w, small_m, small_v, halves_in + halves_mix + halves_ffn)
    big_grads = small_out[1 + 4 * len(SMALL_PARAMS):]
    total_loss = small_out[0][0, 0]

    grads, delta, new_m, new_v = {}, {}, {}, {}
    for i, n in enumerate(SMALL_PARAMS):
        vals = small_out[1 + 4 * i:5 + 4 * i]
        if n == "final_norm_g":
            vals = [a.reshape(D_MODEL) for a in vals]
        grads[n], delta[n], new_m[n], new_v[n] = vals
    updates = _adamw([big_w[n] for n in big_names], big_grads, [big_m[n] for n in big_names],
                     [big_v[n] for n in big_names])
    for n, vals in zip(big_names, updates):
        if n == "w_in":
            vals = [jnp.swapaxes(a, 1, 2) for a in vals]
        grads[n], delta[n], new_m[n], new_v[n] = vals

    names = ["norm_mix_g", "w_in", "b_gate", "conv_a_w", "conv_a_b", "w_proj_a", "w_proj_b", "w_out", "norm_ffn_g", "w_up",
             "ffn_conv_w", "ffn_conv_b", "w_down", "final_norm_g"]
    out = [total_loss, grad_x[None]]
    for group in (grads, delta, new_m, new_v):
        out += [group[n] for n in names]
    return tuple(out)
```
